```python
import math
import jax, jax.numpy as jnp
from jax import lax
import numpy as np

D_MODEL = 1024
BATCH = 8
SEQ = 4096
DEPTH = 1

CTX_LEN = 256
GRID_W = 64
D_ATTN = 512
N_HEADS_A = 8
HEAD_DIM = D_ATTN // N_HEADS_A
D_CONV = D_MODEL - D_ATTN
CONV_WIDTH = 31
NA_ROWS = 8
NA_COLS = 16
Q_BLOCK_COLS = 16
K_BLOCK_COLS = Q_BLOCK_COLS + NA_COLS
D_FF = 2816
FFN_CONV_WIDTH = 3
EPS = 1e-6
ATTN_SCALE = HEAD_DIM ** -0.5
SPLITS = [D_ATTN, 2 * D_ATTN, 3 * D_ATTN, 3 * D_ATTN + D_CONV]

kernel_name = 'hybrid_natten_conformer_dit_layer'


def rms_norm(x, g):
    xf = x.astype(jnp.float32)
    y = xf * lax.rsqrt(jnp.mean(xf * xf, axis=-1, keepdims=True) + EPS)
    return (y * g.astype(jnp.float32)).astype(x.dtype)


def layer_norm(x, g, b):
    xf = x.astype(jnp.float32)
    mu = jnp.mean(xf, axis=-1, keepdims=True)
    xc = xf - mu
    var = jnp.mean(xc * xc, axis=-1, keepdims=True)
    y = xc * lax.rsqrt(var + EPS) * g.astype(jnp.float32) + b.astype(jnp.float32)
    return y.astype(x.dtype)


def modulate(h, shift, scale):
    return h * (1 + scale) + shift


def depthwise_conv(x, w, b):
    y = lax.conv_general_dilated(x, w[:, None, :], window_strides=(1,), padding='SAME',
                                 dimension_numbers=('NWC', 'WIO', 'NWC'),
                                 feature_group_count=x.shape[-1])
    return y + b


def heads(t):
    return t.reshape(t.shape[0], t.shape[1], N_HEADS_A, HEAD_DIM)


def neighbourhood_tables(rows):
    wr = min(NA_ROWS, rows)
    n_blk = GRID_W // Q_BLOCK_COLS
    r = np.arange(rows)
    key_rows = np.clip(r - wr // 2, 0, rows - wr)[:, None] + np.arange(wr)[None, :]
    blk = np.arange(n_blk)
    key_cols = (np.clip(blk * Q_BLOCK_COLS - NA_COLS // 2, 0, GRID_W - K_BLOCK_COLS)[:, None]
                + np.arange(K_BLOCK_COLS)[None, :])
    q_cols = blk[:, None] * Q_BLOCK_COLS + np.arange(Q_BLOCK_COLS)[None, :]
    col_start = np.clip(q_cols - NA_COLS // 2, 0, GRID_W - NA_COLS)
    kc = key_cols[:, None, :]
    valid = (kc >= col_start[..., None]) & (kc < col_start[..., None] + NA_COLS)
    key_tok = key_rows[:, None, :, None] * GRID_W + key_cols[None, :, None, :]
    row_off = key_rows - r[:, None] + (NA_ROWS - 1)
    col_off = np.clip(kc - q_cols[..., None] + (NA_COLS - 1), 0, 2 * NA_COLS - 2)
    shape5 = (rows, n_blk, Q_BLOCK_COLS, wr, K_BLOCK_COLS)
    n_keys = wr * K_BLOCK_COLS
    shape4 = (rows, n_blk, Q_BLOCK_COLS, n_keys)
    valid = np.broadcast_to(valid[None, :, :, None, :], shape5).reshape(shape4)
    row_idx = np.broadcast_to(row_off[:, None, None, :, None], shape5).reshape(shape4)
    col_idx = np.broadcast_to(col_off[None, :, :, None, :], shape5).reshape(shape4)
    return (jnp.asarray(key_tok.reshape(rows, n_blk, n_keys), jnp.int32),
            jnp.asarray(row_idx, jnp.int32), jnp.asarray(col_idx, jnp.int32), jnp.asarray(valid))


def neighbourhood_bias(rpb, row_idx, col_idx, valid):
    b = rpb[:, row_idx, col_idx].astype(jnp.float32)
    b = jnp.where(valid[None], b, -jnp.inf)
    return jnp.transpose(b, (1, 2, 0, 3, 4))


def neighbourhood_attention(q, k, v, k_ctx, v_ctx, key_tok, bias):
    B, S = q.shape[0], q.shape[1]
    rows, n_blk, n_keys = key_tok.shape
    qb = q.reshape(B, rows, n_blk, Q_BLOCK_COLS, N_HEADS_A, HEAD_DIM)
    kg = k[:, key_tok]
    vg = v[:, key_tok]
    s_loc = jnp.einsum('brnqhd,brnkhd->brnhqk', qb, kg, preferred_element_type=jnp.float32) * ATTN_SCALE + bias
    s_ctx = jnp.einsum('brnqhd,bchd->brnhqc', qb, k_ctx, preferred_element_type=jnp.float32) * ATTN_SCALE
    p = jax.nn.softmax(jnp.concatenate([s_loc, s_ctx], axis=-1), axis=-1).astype(v.dtype)
    out = (jnp.einsum('brnhqk,brnkhd->brnqhd', p[..., :n_keys], vg)
           + jnp.einsum('brnhqc,bchd->brnqhd', p[..., n_keys:], v_ctx))
    return out.reshape(B, S, D_ATTN)


def context_attention(q, k, v):
    s = jnp.einsum('bqhd,bkhd->bhqk', q, k, preferred_element_type=jnp.float32) * ATTN_SCALE
    p = jax.nn.softmax(s, axis=-1).astype(v.dtype)
    out = jnp.einsum('bhqk,bkhd->bqhd', p, v)
    return out.reshape(q.shape[0], q.shape[1], D_ATTN)


def conformer_conv(a, g, conv_w, conv_b, ln_g, ln_b):
    u = a * jax.nn.sigmoid(g)
    u = depthwise_conv(u, conv_w, conv_b)
    u = layer_norm(u, ln_g, ln_b)
    return jax.nn.silu(u)


def conv_ffn(h, w_up, ffn_w, ffn_b, w_down):
    u = depthwise_conv(h @ w_up, ffn_w, ffn_b)
    gate, val = jnp.split(u, 2, axis=-1)
    return (jax.nn.silu(gate) * val) @ w_down


def _fwd_setup_inputs(seed: int = 0) -> dict:
    key = jax.random.key(seed)
    ks = jax.random.split(key, 20)
    D = D_MODEL
    n_in = 3 * D_ATTN + 2 * D_CONV

    def nrm(k, shape, scale):
        return jax.random.normal(k, shape, jnp.float32) * scale

    return {
        'x': nrm(ks[0], (BATCH, SEQ, D), 1.0),
        'c': nrm(ks[1], (BATCH, D), 1.0),
        'ctx': nrm(ks[2], (BATCH, CTX_LEN, D), 1.0),
        'c_ctx': nrm(ks[3], (D,), 1.0),
        'w_mod': nrm(ks[4], (DEPTH, D, 6 * D), D ** -0.5),
        'b_mod': nrm(ks[5], (DEPTH, 6 * D), 0.02),
        'g_norm1': 1.0 + nrm(ks[6], (DEPTH, D), 0.05),
        'w_in': nrm(ks[7], (DEPTH, D, n_in), D ** -0.5),
        'rpb': nrm(ks[8], (DEPTH, N_HEADS_A, 2 * NA_ROWS - 1, 2 * NA_COLS - 1), 0.5),
        'conv_w': nrm(ks[9], (DEPTH, CONV_WIDTH, D_CONV), CONV_WIDTH ** -0.5),
        'conv_b': nrm(ks[10], (DEPTH, D_CONV), 0.02),
        'ln_g': 1.0 + nrm(ks[11], (DEPTH, D_CONV), 0.05),
        'ln_b': nrm(ks[12], (DEPTH, D_CONV), 0.02),
        'w_out': nrm(ks[13], (DEPTH, D, D), D ** -0.5),
        'g_norm2': 1.0 + nrm(ks[14], (DEPTH, D), 0.05),
        'w_up': nrm(ks[15], (DEPTH, D, 2 * D_FF), D ** -0.5),
        'ffn_conv_w': nrm(ks[16], (DEPTH, FFN_CONV_WIDTH, 2 * D_FF), FFN_CONV_WIDTH ** -0.5),
        'ffn_conv_b': nrm(ks[17], (DEPTH, 2 * D_FF), 0.02),
        'w_down': nrm(ks[18], (DEPTH, D_FF, D), D_FF ** -0.5),
        'g_final': 1.0 + nrm(ks[19], (D,), 0.05),
    }


def _fwd_reference(x, c, ctx, c_ctx, w_mod, b_mod, g_norm1, w_in, rpb, conv_w, conv_b, ln_g, ln_b,
              w_out, g_norm2, w_up, ffn_conv_w, ffn_conv_b, w_down, g_final):
    S = x.shape[1]
    rows = S // GRID_W
    key_tok, row_idx, col_idx, valid = neighbourhood_tables(rows)
    c_act = jax.nn.silu(c)
    cctx_act = jax.nn.silu(c_ctx)
    for l in range(DEPTH):
        last = l == DEPTH - 1
        mod = (c_act @ w_mod[l] + b_mod[l])[:, None, :]
        sh1, sc1, gt1, sh2, sc2, gt2 = jnp.split(mod, 6, axis=-1)
        mod_c = cctx_act @ w_mod[l] + b_mod[l]
        csh1, csc1, cgt1, csh2, csc2, cgt2 = jnp.split(mod_c, 6, axis=-1)
        bias = neighbourhood_bias(rpb[l], row_idx, col_idx, valid)

        h = modulate(rms_norm(x, g_norm1[l]), sh1, sc1)
        hc = modulate(rms_norm(ctx, g_norm1[l]), csh1, csc1)
        q, k, v, a, g = jnp.split(h @ w_in[l], SPLITS, axis=-1)
        if last:
            k_c, v_c = jnp.split(hc @ w_in[l][:, D_ATTN:3 * D_ATTN], 2, axis=-1)
        else:
            q_c, k_c, v_c, a_c, g_c = jnp.split(hc @ w_in[l], SPLITS, axis=-1)
        k_c, v_c = heads(k_c), heads(v_c)
        y_na = neighbourhood_attention(heads(q), heads(k), heads(v), k_c, v_c, key_tok, bias)
        y_cv = conformer_conv(a, g, conv_w[l], conv_b[l], ln_g[l], ln_b[l])
        x = x + gt1 * (jnp.concatenate([y_na, y_cv], axis=-1) @ w_out[l])

        if not last:
            yc_na = context_attention(heads(q_c), k_c, v_c)
            yc_cv = conformer_conv(a_c, g_c, conv_w[l], conv_b[l], ln_g[l], ln_b[l])
            ctx = ctx + cgt1 * (jnp.concatenate([yc_na, yc_cv], axis=-1) @ w_out[l])
            hc2 = modulate(rms_norm(ctx, g_norm2[l]), csh2, csc2)
            ctx = ctx + cgt2 * conv_ffn(hc2, w_up[l], ffn_conv_w[l], ffn_conv_b[l], w_down[l])

        h2 = modulate(rms_norm(x, g_norm2[l]), sh2, sc2)
        x = x + gt2 * conv_ffn(h2, w_up[l], ffn_conv_w[l], ffn_conv_b[l], w_down[l])
    return rms_norm(x, g_final)


import jax as _jax
import jax.numpy as _jnp

TWIN_FORMAT = 'train_step'
FWD_PARAMS = ['x', 'c', 'ctx', 'c_ctx', 'w_mod', 'b_mod', 'g_norm1', 'w_in', 'rpb', 'conv_w', 'conv_b', 'ln_g', 'ln_b', 'w_out', 'g_norm2', 'w_up', 'ffn_conv_w', 'ffn_conv_b', 'w_down', 'g_final']
TWIN_WEIGHTS = ['c_ctx', 'w_mod', 'b_mod', 'g_norm1', 'w_in', 'rpb', 'conv_w', 'conv_b', 'ln_g', 'ln_b', 'w_out', 'g_norm2', 'w_up', 'ffn_conv_w', 'ffn_conv_b', 'w_down', 'g_final']
TWIN_DIFF_INPUT = 'x'
TWIN_INPUTS = ['x', 'c', 'ctx', 'c_ctx', 'w_mod', 'b_mod', 'g_norm1', 'w_in', 'rpb', 'conv_w', 'conv_b', 'ln_g', 'ln_b', 'w_out', 'g_norm2', 'w_up', 'ffn_conv_w', 'ffn_conv_b', 'w_down', 'g_final', 'loss_target', 'm_c_ctx', 'm_w_mod', 'm_b_mod', 'm_g_norm1', 'm_w_in', 'm_rpb', 'm_conv_w', 'm_conv_b', 'm_ln_g', 'm_ln_b', 'm_w_out', 'm_g_norm2', 'm_w_up', 'm_ffn_conv_w', 'm_ffn_conv_b', 'm_w_down', 'm_g_final', 'v_c_ctx', 'v_w_mod', 'v_b_mod', 'v_g_norm1', 'v_w_in', 'v_rpb', 'v_conv_w', 'v_conv_b', 'v_ln_g', 'v_ln_b', 'v_w_out', 'v_g_norm2', 'v_w_up', 'v_ffn_conv_w', 'v_ffn_conv_b', 'v_w_down', 'v_g_final']
TWIN_OUTPUTS = ['loss', 'grad_x', 'grad_c_ctx', 'grad_w_mod', 'grad_b_mod', 'grad_g_norm1', 'grad_w_in', 'grad_rpb', 'grad_conv_w', 'grad_conv_b', 'grad_ln_g', 'grad_ln_b', 'grad_w_out', 'grad_g_norm2', 'grad_w_up', 'grad_ffn_conv_w', 'grad_ffn_conv_b', 'grad_w_down', 'grad_g_final', 'delta_c_ctx', 'delta_w_mod', 'delta_b_mod', 'delta_g_norm1', 'delta_w_in', 'delta_rpb', 'delta_conv_w', 'delta_conv_b', 'delta_ln_g', 'delta_ln_b', 'delta_w_out', 'delta_g_norm2', 'delta_w_up', 'delta_ffn_conv_w', 'delta_ffn_conv_b', 'delta_w_down', 'delta_g_final', 'new_m_c_ctx', 'new_m_w_mod', 'new_m_b_mod', 'new_m_g_norm1', 'new_m_w_in', 'new_m_rpb', 'new_m_conv_w', 'new_m_conv_b', 'new_m_ln_g', 'new_m_ln_b', 'new_m_w_out', 'new_m_g_norm2', 'new_m_w_up', 'new_m_ffn_conv_w', 'new_m_ffn_conv_b', 'new_m_w_down', 'new_m_g_final', 'new_v_c_ctx', 'new_v_w_mod', 'new_v_b_mod', 'new_v_g_norm1', 'new_v_w_in', 'new_v_rpb', 'new_v_conv_w', 'new_v_conv_b', 'new_v_ln_g', 'new_v_ln_b', 'new_v_w_out', 'new_v_g_norm2', 'new_v_w_up', 'new_v_ffn_conv_w', 'new_v_ffn_conv_b', 'new_v_w_down', 'new_v_g_final']
TWIN_LEAF_KINDS = {'loss': 'loss', 'grad_x': 'grad_x', 'grad_c_ctx': 'grad_w', 'grad_w_mod': 'grad_w', 'grad_b_mod': 'grad_w', 'grad_g_norm1': 'grad_w', 'grad_w_in': 'grad_w', 'grad_rpb': 'grad_w', 'grad_conv_w': 'grad_w', 'grad_conv_b': 'grad_w', 'grad_ln_g': 'grad_w', 'grad_ln_b': 'grad_w', 'grad_w_out': 'grad_w', 'grad_g_norm2': 'grad_w', 'grad_w_up': 'grad_w', 'grad_ffn_conv_w': 'grad_w', 'grad_ffn_conv_b': 'grad_w', 'grad_w_down': 'grad_w', 'grad_g_final': 'grad_w', 'delta_c_ctx': 'delta_w', 'delta_w_mod': 'delta_w', 'delta_b_mod': 'delta_w', 'delta_g_norm1': 'delta_w', 'delta_w_in': 'delta_w', 'delta_rpb': 'delta_w', 'delta_conv_w': 'delta_w', 'delta_conv_b': 'delta_w', 'delta_ln_g': 'delta_w', 'delta_ln_b': 'delta_w', 'delta_w_out': 'delta_w', 'delta_g_norm2': 'delta_w', 'delta_w_up': 'delta_w', 'delta_ffn_conv_w': 'delta_w', 'delta_ffn_conv_b': 'delta_w', 'delta_w_down': 'delta_w', 'delta_g_final': 'delta_w', 'new_m_c_ctx': 'new_m', 'new_m_w_mod': 'new_m', 'new_m_b_mod': 'new_m', 'new_m_g_norm1': 'new_m', 'new_m_w_in': 'new_m', 'new_m_rpb': 'new_m', 'new_m_conv_w': 'new_m', 'new_m_conv_b': 'new_m', 'new_m_ln_g': 'new_m', 'new_m_ln_b': 'new_m', 'new_m_w_out': 'new_m', 'new_m_g_norm2': 'new_m', 'new_m_w_up': 'new_m', 'new_m_ffn_conv_w': 'new_m', 'new_m_ffn_conv_b': 'new_m', 'new_m_w_down': 'new_m', 'new_m_g_final': 'new_m', 'new_v_c_ctx': 'new_v', 'new_v_w_mod': 'new_v', 'new_v_b_mod': 'new_v', 'new_v_g_norm1': 'new_v', 'new_v_w_in': 'new_v', 'new_v_rpb': 'new_v', 'new_v_conv_w': 'new_v', 'new_v_conv_b': 'new_v', 'new_v_ln_g': 'new_v', 'new_v_ln_b': 'new_v', 'new_v_w_out': 'new_v', 'new_v_g_norm2': 'new_v', 'new_v_w_up': 'new_v', 'new_v_ffn_conv_w': 'new_v', 'new_v_ffn_conv_b': 'new_v', 'new_v_w_down': 'new_v', 'new_v_g_final': 'new_v'}


def _forward(args):
    return _fwd_reference(*[args[k] for k in FWD_PARAMS])


def _output_shape():
    def fwd():
        inp = _fwd_setup_inputs(0)
        return _fwd_reference(*[inp[k] for k in FWD_PARAMS])
    out = _jax.eval_shape(fwd)
    return out.shape, out.dtype

N_MICROBATCH = 1
ADAM_LR = 0.001
ADAM_B1 = 0.9
ADAM_B2 = 0.999
ADAM_EPS = 1e-08
ADAM_WD = 0.01
ADAM_STEP = 10
PER_EXAMPLE_BATCH_AXIS = {'x': 0, 'c': 0, 'ctx': 0, 'loss_target': 0}
SHARED_INPUTS = []
_WEIGHT_DTYPES = {'c_ctx': _jnp.float32, 'w_mod': _jnp.float32, 'b_mod': _jnp.float32, 'g_norm1': _jnp.float32, 'w_in': _jnp.float32, 'rpb': _jnp.float32, 'conv_w': _jnp.float32, 'conv_b': _jnp.float32, 'ln_g': _jnp.float32, 'ln_b': _jnp.float32, 'w_out': _jnp.float32, 'g_norm2': _jnp.float32, 'w_up': _jnp.float32, 'ffn_conv_w': _jnp.float32, 'ffn_conv_b': _jnp.float32, 'w_down': _jnp.float32, 'g_final': _jnp.float32}
MOMENT_SCALE = {'c_ctx': 5.207294e-02, 'w_mod': 1.345634e-01, 'b_mod': 2.843859e-01, 'g_norm1': 6.042269e-02, 'w_in': 6.357515e-02, 'rpb': 5.374026e-03, 'conv_w': 7.939925e-02, 'conv_b': 1.856650e-01, 'ln_g': 1.223973e-01, 'ln_b': 1.378871e-01, 'w_out': 9.495841e-02, 'g_norm2': 1.222245e-01, 'w_up': 6.376893e-02, 'ffn_conv_w': 6.664613e-02, 'ffn_conv_b': 5.826470e-02, 'w_down': 1.080668e-01, 'g_final': 3.241439e+01}


def _to_microbatches(a, axis):
    t = _jnp.moveaxis(a, axis, 0)
    t = t.reshape((N_MICROBATCH, t.shape[0] // N_MICROBATCH) + t.shape[1:])
    return _jnp.moveaxis(t, 1, axis + 1)


def setup_inputs(seed: int = 0) -> dict:
    inp = _fwd_setup_inputs(seed)
    key = _jax.random.fold_in(_jax.random.key(seed), 7919)
    shape, _ = _output_shape()
    out = dict(inp)
    out["loss_target"] = _jax.random.normal(_jax.random.fold_in(key, 0), shape, _jnp.float32)
    for i, name in enumerate(TWIN_WEIGHTS):
        w = inp[name].astype(_jnp.float32)
        if MOMENT_SCALE is None:
            s = _jnp.sqrt(_jnp.mean(_jnp.square(w)) + 1e-30)
        else:
            s = MOMENT_SCALE[name]
        km, kv = _jax.random.split(_jax.random.fold_in(key, i + 1))
        out[name] = w
        out["m_" + name] = s * _jax.random.normal(km, w.shape, _jnp.float32)
        out["v_" + name] = (s * s) * _jax.random.uniform(kv, w.shape, _jnp.float32, 0.5, 1.5)
    if N_MICROBATCH > 1:
        for name, axis in PER_EXAMPLE_BATCH_AXIS.items():
            out[name] = _to_microbatches(out[name], axis)
    return {'x': out['x'], 'c': out['c'], 'ctx': out['ctx'], 'c_ctx': out['c_ctx'], 'w_mod': out['w_mod'], 'b_mod': out['b_mod'], 'g_norm1': out['g_norm1'], 'w_in': out['w_in'], 'rpb': out['rpb'], 'conv_w': out['conv_w'], 'conv_b': out['conv_b'], 'ln_g': out['ln_g'], 'ln_b': out['ln_b'], 'w_out': out['w_out'], 'g_norm2': out['g_norm2'], 'w_up': out['w_up'], 'ffn_conv_w': out['ffn_conv_w'], 'ffn_conv_b': out['ffn_conv_b'], 'w_down': out['w_down'], 'g_final': out['g_final'], 'loss_target': out['loss_target'], 'm_c_ctx': out['m_c_ctx'], 'm_w_mod': out['m_w_mod'], 'm_b_mod': out['m_b_mod'], 'm_g_norm1': out['m_g_norm1'], 'm_w_in': out['m_w_in'], 'm_rpb': out['m_rpb'], 'm_conv_w': out['m_conv_w'], 'm_conv_b': out['m_conv_b'], 'm_ln_g': out['m_ln_g'], 'm_ln_b': out['m_ln_b'], 'm_w_out': out['m_w_out'], 'm_g_norm2': out['m_g_norm2'], 'm_w_up': out['m_w_up'], 'm_ffn_conv_w': out['m_ffn_conv_w'], 'm_ffn_conv_b': out['m_ffn_conv_b'], 'm_w_down': out['m_w_down'], 'm_g_final': out['m_g_final'], 'v_c_ctx': out['v_c_ctx'], 'v_w_mod': out['v_w_mod'], 'v_b_mod': out['v_b_mod'], 'v_g_norm1': out['v_g_norm1'], 'v_w_in': out['v_w_in'], 'v_rpb': out['v_rpb'], 'v_conv_w': out['v_conv_w'], 'v_conv_b': out['v_conv_b'], 'v_ln_g': out['v_ln_g'], 'v_ln_b': out['v_ln_b'], 'v_w_out': out['v_w_out'], 'v_g_norm2': out['v_g_norm2'], 'v_w_up': out['v_w_up'], 'v_ffn_conv_w': out['v_ffn_conv_w'], 'v_ffn_conv_b': out['v_ffn_conv_b'], 'v_w_down': out['v_w_down'], 'v_g_final': out['v_g_final']}


def _loss(weights, diff, rest, loss_target):
    with _jax.named_scope("forward"):
        args = {**rest, TWIN_DIFF_INPUT: diff, **{k: w.astype(_WEIGHT_DTYPES[k]) for k, w in weights.items()}}
        y = _forward(args)
    with _jax.named_scope("loss_head"):
        err = _jnp.square(y.astype(_jnp.float32) - loss_target)
        return 0.5 * _jnp.sum(_jnp.mean(err, axis=-1)) if err.ndim else 0.5 * err


def _adamw(w, g, m, v):
    m = ADAM_B1 * m + (1.0 - ADAM_B1) * g
    v = ADAM_B2 * v + (1.0 - ADAM_B2) * _jnp.square(g)
    m_hat = m / (1.0 - ADAM_B1 ** ADAM_STEP)
    v_hat = v / (1.0 - ADAM_B2 ** ADAM_STEP)
    delta = -ADAM_LR * (m_hat / (_jnp.sqrt(v_hat) + ADAM_EPS) + ADAM_WD * w)
    return delta, m, v


def reference(x, c, ctx, c_ctx, w_mod, b_mod, g_norm1, w_in, rpb, conv_w, conv_b, ln_g, ln_b, w_out, g_norm2, w_up, ffn_conv_w, ffn_conv_b, w_down, g_final, loss_target, m_c_ctx, m_w_mod, m_b_mod, m_g_norm1, m_w_in, m_rpb, m_conv_w, m_conv_b, m_ln_g, m_ln_b, m_w_out, m_g_norm2, m_w_up, m_ffn_conv_w, m_ffn_conv_b, m_w_down, m_g_final, v_c_ctx, v_w_mod, v_b_mod, v_g_norm1, v_w_in, v_rpb, v_conv_w, v_conv_b, v_ln_g, v_ln_b, v_w_out, v_g_norm2, v_w_up, v_ffn_conv_w, v_ffn_conv_b, v_w_down, v_g_final):
    given = dict(x=x, c=c, ctx=ctx, c_ctx=c_ctx, w_mod=w_mod, b_mod=b_mod, g_norm1=g_norm1, w_in=w_in, rpb=rpb, conv_w=conv_w, conv_b=conv_b, ln_g=ln_g, ln_b=ln_b, w_out=w_out, g_norm2=g_norm2, w_up=w_up, ffn_conv_w=ffn_conv_w, ffn_conv_b=ffn_conv_b, w_down=w_down, g_final=g_final, loss_target=loss_target, m_c_ctx=m_c_ctx, m_w_mod=m_w_mod, m_b_mod=m_b_mod, m_g_norm1=m_g_norm1, m_w_in=m_w_in, m_rpb=m_rpb, m_conv_w=m_conv_w, m_conv_b=m_conv_b, m_ln_g=m_ln_g, m_ln_b=m_ln_b, m_w_out=m_w_out, m_g_norm2=m_g_norm2, m_w_up=m_w_up, m_ffn_conv_w=m_ffn_conv_w, m_ffn_conv_b=m_ffn_conv_b, m_w_down=m_w_down, m_g_final=m_g_final, v_c_ctx=v_c_ctx, v_w_mod=v_w_mod, v_b_mod=v_b_mod, v_g_norm1=v_g_norm1, v_w_in=v_w_in, v_rpb=v_rpb, v_conv_w=v_conv_w, v_conv_b=v_conv_b, v_ln_g=v_ln_g, v_ln_b=v_ln_b, v_w_out=v_w_out, v_g_norm2=v_g_norm2, v_w_up=v_w_up, v_ffn_conv_w=v_ffn_conv_w, v_ffn_conv_b=v_ffn_conv_b, v_w_down=v_w_down, v_g_final=v_g_final)
    weights = {n: given[n] for n in TWIN_WEIGHTS}
    shared = {n: given[n] for n in SHARED_INPUTS}
    per_example = {n: given[n] for n in ['x', 'c', 'ctx']}
    grad_fn = _jax.value_and_grad(_loss, argnums=(0, 1))

    def one_microbatch(ex, loss_target):
        ex = dict(ex)
        diff = ex.pop(TWIN_DIFF_INPUT)
        return grad_fn(weights, diff, {**shared, **ex}, loss_target)

    if N_MICROBATCH == 1:
        loss, (grad_w, grad_x) = one_microbatch(per_example, given["loss_target"])
    else:
        def body(carry, xs):
            loss_sum, grad_sum = carry
            l_k, (gw_k, gx_k) = one_microbatch(xs[0], xs[1])
            with _jax.named_scope("update"):
                return (loss_sum + l_k, _jax.tree.map(_jnp.add, grad_sum, gw_k)), gx_k

        init = (_jnp.zeros((), _jnp.float32), _jax.tree.map(_jnp.zeros_like, weights))
        (loss, grad_w), grad_x = _jax.lax.scan(body, init, (per_example, given["loss_target"]))
    with _jax.named_scope("update"):
        delta_w, new_m, new_v = {}, {}, {}
        for n in TWIN_WEIGHTS:
            delta_w[n], new_m[n], new_v[n] = _adamw(weights[n], grad_w[n], given["m_" + n], given["v_" + n])
    return (loss, grad_x, *[grad_w[n] for n in TWIN_WEIGHTS], *[delta_w[n] for n in TWIN_WEIGHTS],
            *[new_m[n] for n in TWIN_WEIGHTS], *[new_v[n] for n in TWIN_WEIGHTS])
```

```python
import functools

import numpy as np
import jax
import jax.numpy as jnp
from jax import lax
from jax.experimental import pallas as pl
from jax.experimental.pallas import tpu as pltpu

f32 = jnp.float32
bf16 = jnp.bfloat16

D = 1024
T = 4096
TC = 256
TA = T + TC
DA = 512
NH = 8
HD = 64
GW = 64
WR = 8
NCOL = 16
F = 2816
F2 = 2 * F
CW = 31
NDEV = 8
EPS = 1e-6
SCALE = HD ** -0.5
NEG = -1e30
MESH = pl.DeviceIdType.MESH

NT = (((1,), (1,)), ((), ()))
TN = (((0,), (0,)), ((), ()))

ADAM_LR, ADAM_B1, ADAM_B2, ADAM_EPS, ADAM_WD, ADAM_STEP = 0.001, 0.9, 0.999, 1e-08, 0.01, 10

VMEM_LIMIT = 56 * 1024 * 1024


def _cp(*sem):
    return pltpu.CompilerParams(dimension_semantics=sem or None, vmem_limit_bytes=VMEM_LIMIT)


def _dot(a, b):
    return jnp.dot(a, b, preferred_element_type=f32)


def _dg(a, b, dims):
    return lax.dot_general(a, b, dims, preferred_element_type=f32)


def _sigmoid(x):
    return 1.0 / (1.0 + jnp.exp(-x))


def _full(shape):
    n = len(shape)
    return pl.BlockSpec(shape, lambda *_: (0,) * n)


def _my_pos():
    return lax.axis_index("x"), lax.axis_index("y"), lax.axis_index("c")


def _small_allgather(v, name):
    n = v.shape[0]

    def body(v_ref, out_ref, send_sems, recv_sems):
        x, y, c = _my_pos()
        me = 4 * x + 2 * y + c
        out_ref[me] = v_ref[...]
        peers = []
        for k in range(1, NDEV):
            kx, ky, kc = (k >> 2) & 1, (k >> 1) & 1, k & 1
            peers.append((x ^ kx, y ^ ky, c ^ kc))

        def copy(k, slot, to):
            return pltpu.make_async_remote_copy(
                src_ref=v_ref, dst_ref=out_ref.at[slot], send_sem=send_sems.at[k], recv_sem=recv_sems.at[k],
                device_id=to, device_id_type=MESH)

        sends = [copy(k, me, p) for k, p in enumerate(peers)]
        for cp in sends:
            cp.start()
        for k, (px, py, pc) in enumerate(peers):
            copy(k, 4 * px + 2 * py + pc, (x, y, c)).wait_recv()
        for cp in sends:
            cp.wait_send()

    return pl.pallas_call(
        body, name=name,
        out_shape=jax.ShapeDtypeStruct((NDEV, n, 128), f32),
        in_specs=[pl.BlockSpec(memory_space=pltpu.VMEM)],
        out_specs=pl.BlockSpec(memory_space=pltpu.VMEM),
        scratch_shapes=[pltpu.SemaphoreType.DMA((NDEV - 1,)), pltpu.SemaphoreType.DMA((NDEV - 1,))],
    )(v)


def _allgather2(slab, name):
    m_per, n = slab.shape

    def body(x_ref, out_ref, send_sems, recv_sems, local_sem):
        x, y, c = _my_pos()
        me, sibling = (x, y, c), (x, y, 1 - c)
        chips = [(1 - x, y), (x, 1 - y), (1 - x, 1 - y)]

        def rows(px, py, pc):
            return out_ref.at[pl.ds(pl.multiple_of((4 * px + 2 * py + pc) * m_per, 16), m_per), :]

        def copy(k, block, to, src=None):
            return pltpu.make_async_remote_copy(
                src_ref=rows(*block) if src is None else src, dst_ref=rows(*block),
                send_sem=send_sems.at[k], recv_sem=recv_sems.at[k], device_id=to, device_id_type=MESH)

        mine = pltpu.make_async_copy(x_ref, rows(*me), local_sem)
        mine.start()
        first = [copy(0, me, sibling, src=x_ref)]
        first += [copy(1 + j, me, (*chip, c), src=x_ref) for j, chip in enumerate(chips)]
        for cp in first:
            cp.start()
        passed = [copy(4 + j, (*chip, c), sibling) for j, chip in enumerate(chips)]
        for j, chip in enumerate(chips):
            copy(1 + j, (*chip, c), me).wait_recv()
            passed[j].start()
        copy(0, sibling, me).wait_recv()
        for j, chip in enumerate(chips):
            copy(4 + j, (*chip, 1 - c), me).wait_recv()
        for cp in first + passed:
            cp.wait_send()
        mine.wait()

    return pl.pallas_call(
        body, name=name,
        out_shape=jax.ShapeDtypeStruct((NDEV * m_per, n), slab.dtype),
        in_specs=[pl.BlockSpec(memory_space=pltpu.VMEM)],
        out_specs=pl.BlockSpec(memory_space=pltpu.VMEM),
        scratch_shapes=[pltpu.SemaphoreType.DMA((7,)), pltpu.SemaphoreType.DMA((7,)), pltpu.SemaphoreType.DMA],
        compiler_params=pltpu.CompilerParams(vmem_limit_bytes=VMEM_LIMIT),
    )(slab)


def _reduce_scatter2(g, name):
    _, r, n = g.shape
    ch = 16
    nch = r // ch

    def body(g_ref, out_ref, a_ref, h_ref, b_ref, s1_send, s1_recv, s2_send, s2_recv):
        x, y, c = _my_pos()
        sibling = (x, y, 1 - c)
        s1 = []
        for j in range(4):
            cp = pltpu.make_async_remote_copy(
                src_ref=g_ref.at[2 * j + (1 - c)], dst_ref=a_ref.at[j], send_sem=s1_send.at[j], recv_sem=s1_recv.at[j],
                device_id=sibling, device_id_type=MESH)
            cp.start()
            s1.append(cp)
        for cp in s1:
            cp.wait_recv()

        def add1(i, _):
            rr = pl.ds(pl.multiple_of(i * ch, ch), ch)
            for j in range(4):
                h_ref[j, rr, :] = (g_ref[2 * j + c, rr, :].astype(f32) + a_ref[j, rr, :].astype(f32)).astype(bf16)
            return 0
        lax.fori_loop(0, nch, add1, 0)
        mychip = 2 * x + y
        s2 = []
        for m in range(1, 4):
            mx, my_ = (m >> 1) & 1, m & 1
            px, py = x ^ mx, y ^ my_
            cp = pltpu.make_async_remote_copy(
                src_ref=h_ref.at[2 * px + py], dst_ref=b_ref.at[m - 1], send_sem=s2_send.at[m - 1], recv_sem=s2_recv.at[m - 1],
                device_id=(px, py, c), device_id_type=MESH)
            cp.start()
            s2.append(cp)
        for cp in s2:
            cp.wait_recv()

        def add2(i, _):
            rr = pl.ds(pl.multiple_of(i * ch, ch), ch)
            acc = h_ref[mychip, rr, :].astype(f32)
            for m in range(3):
                acc = acc + b_ref[m, rr, :].astype(f32)
            out_ref[rr, :] = acc
            return 0
        lax.fori_loop(0, nch, add2, 0)
        for cp in s1 + s2:
            cp.wait_send()

    return pl.pallas_call(
        body, name=name,
        out_shape=jax.ShapeDtypeStruct((r, n), f32),
        in_specs=[pl.BlockSpec(memory_space=pltpu.VMEM)],
        out_specs=pl.BlockSpec(memory_space=pltpu.VMEM),
        scratch_shapes=[pltpu.VMEM((4, r, n), bf16), pltpu.VMEM((4, r, n), bf16), pltpu.VMEM((3, r, n), bf16),
                        pltpu.SemaphoreType.DMA((4,)), pltpu.SemaphoreType.DMA((4,)),
                        pltpu.SemaphoreType.DMA((3,)), pltpu.SemaphoreType.DMA((3,))],
        compiler_params=pltpu.CompilerParams(vmem_limit_bytes=VMEM_LIMIT),
    )(g)


def _mod_fwd(cvec, w_sh, b_sh):
    def body(c_ref, w_ref, b_ref, o_ref):
        cv = c_ref[...]
        act = (cv * _sigmoid(cv)).astype(bf16)
        o_ref[...] = _dot(act, w_ref[...].astype(bf16)) + b_ref[...]
    return pl.pallas_call(body, name="mod_fwd", out_shape=jax.ShapeDtypeStruct((16, w_sh.shape[1]), f32))(cvec, w_sh, b_sh)


def _mod_bwd(cvec, dm_sh, w_sh):
    def body(c_ref, dm_ref, w_ref, gw_ref, gc_ref):
        cv = c_ref[...]
        act = (cv * _sigmoid(cv)).astype(bf16)
        gw_ref[...] = _dg(act, dm_ref[...].astype(bf16), TN)
        gc_ref[...] = _dg(dm_ref[8:16, :].astype(bf16), w_ref[...].astype(bf16), NT)
    return pl.pallas_call(
        body, name="mod_bwd",
        out_shape=(jax.ShapeDtypeStruct(w_sh.shape, f32), jax.ShapeDtypeStruct((8, D), f32)))(cvec, dm_sh, w_sh)


def _sum_rows8(a, name):
    n = a.shape[1]

    def body(a_ref, o_ref):
        acc = a_ref[0]
        for d in range(1, NDEV):
            acc = acc + a_ref[d]
        o_ref[...] = acc
    return pl.pallas_call(body, name=name, out_shape=jax.ShapeDtypeStruct((n, 128), f32))(a)


def _in_proj(xa, g1, modv, w_inT):
    tm = 256
    nt = TA // tm

    def body(x_ref, g_ref, mod_ref, w_ref, h_ref, q_ref, k_ref, v_ref, a_ref, gg_ref):
        i = pl.program_id(0)
        xv = x_ref[...]
        rstd = lax.rsqrt(jnp.mean(xv * xv, axis=-1, keepdims=True) + EPS)
        is_ctx = i == nt - 1
        sh = jnp.where(is_ctx, mod_ref[6:7, :], mod_ref[0:1, :])
        sc = jnp.where(is_ctx, mod_ref[7:8, :], mod_ref[1:2, :])
        h = ((xv * rstd * g_ref[...]) * (1.0 + sc) + sh).astype(bf16)
        h_ref[...] = h
        for j, o_ref in enumerate((q_ref, k_ref, v_ref, a_ref, gg_ref)):
            o_ref[...] = _dg(h, w_ref[j * DA:(j + 1) * DA, :], NT).astype(o_ref.dtype)

    row = lambda w: pl.BlockSpec((tm, w), lambda i: (i, 0))
    return pl.pallas_call(
        body, name="in_proj", grid=(nt,),
        in_specs=[row(D), _full((1, D)), _full((8, D)), _full((5 * DA, D))],
        out_specs=[row(D), row(DA), row(DA), row(DA), row(DA), row(DA)],
        out_shape=[jax.ShapeDtypeStruct((TA, D), bf16)] + [jax.ShapeDtypeStruct((TA, DA), bf16)] * 3
                  + [jax.ShapeDtypeStruct((TA, DA), f32)] * 2,
        compiler_params=_cp("parallel"),
    )(xa, g1, modv, w_inT)


def _win_start(r):
    return jnp.clip(r - WR // 2, 0, GW - WR)


def _pattern(r):
    return _win_start(r) - r + (WR - 1)


def _bias_table(rpb):
    p = np.arange(8)[:, None]
    i = np.arange(WR)[None, :]
    ro = (p - (WR - 1)) + i + (WR - 1)
    r_hot = (ro[:, :, None] == np.arange(2 * WR - 1)[None, None, :]).astype(np.float32)
    qc = np.arange(GW)[:, None]
    kc = np.arange(GW)[None, :]
    cs = np.clip(qc - NCOL // 2, 0, GW - NCOL)
    valid = (kc >= cs) & (kc < cs + NCOL)
    co = np.clip(kc - qc + (NCOL - 1), 0, 2 * NCOL - 2)
    c_hot = ((co[:, :, None] == np.arange(2 * NCOL - 1)[None, None, :]) & valid[:, :, None]).astype(np.float32)
    hi = lax.Precision.HIGHEST
    t = jnp.einsum("hrc,pir->hpic", rpb, jnp.asarray(r_hot), precision=hi)
    b = jnp.einsum("hpic,qkc->phqik", t, jnp.asarray(c_hot), precision=hi)
    b = jnp.where(jnp.asarray(valid)[None, None, :, None, :], b, NEG)
    return b.reshape(8, NH, GW, WR * GW)


def _rpb_tables():
    qc = np.arange(GW)[:, None]
    kc = np.arange(GW)[None, :]
    cs = np.clip(qc - NCOL // 2, 0, GW - NCOL)
    valid = (kc >= cs) & (kc < cs + NCOL)
    co = np.where(valid, np.clip(kc - qc + (NCOL - 1), 0, 2 * NCOL - 2), -1).astype(np.int32)
    cotab = np.tile(co, (1, WR))
    p = np.arange(8)[:, None]
    i = np.arange(WR)[None, :]
    ro = p + i
    r_hot = (ro[:, :, None] == np.arange(2 * WR - 1)[None, None, :]).astype(np.float32)
    return cotab, r_hot


def _attn_fwd(q, k, v, bias_tab):
    def body(q_ref, k_ref, v_ref, b_ref, y_ref, lse_ref):
        r = pl.program_id(0)
        ks = pl.multiple_of(_win_start(r) * GW, GW)
        qq = q_ref[...]
        for h in range(NH):
            hs = slice(h * HD, (h + 1) * HD)
            qh = qq[:, hs]
            kw = k_ref[pl.ds(ks, WR * GW), hs]
            vw = v_ref[pl.ds(ks, WR * GW), hs]
            kc = k_ref[T:TA, hs]
            vc = v_ref[T:TA, hs]
            sl = _dg(qh, kw, NT) * SCALE + b_ref[0, h]
            sc = _dg(qh, kc, NT) * SCALE
            m = jnp.maximum(jnp.max(sl, axis=-1, keepdims=True), jnp.max(sc, axis=-1, keepdims=True))
            pl_ = jnp.exp(sl - m)
            pc = jnp.exp(sc - m)
            l = jnp.sum(pl_, axis=-1, keepdims=True) + jnp.sum(pc, axis=-1, keepdims=True)
            o = _dot(pl_.astype(bf16), vw) + _dot(pc.astype(bf16), vc)
            y_ref[:, hs] = (o / l).astype(bf16)
            lse_ref[:, h:h + 1] = m + jnp.log(l)

    return pl.pallas_call(
        body, name="attn_fwd", grid=(GW,),
        in_specs=[pl.BlockSpec((GW, DA), lambda r: (r, 0)), _full((TA, DA)), _full((TA, DA)),
                  pl.BlockSpec((1, NH, GW, WR * GW), lambda r: (_pattern(r), 0, 0, 0))],
        out_specs=[pl.BlockSpec((GW, DA), lambda r: (r, 0)), pl.BlockSpec((GW, NH), lambda r: (r, 0))],
        out_shape=[jax.ShapeDtypeStruct((T, DA), bf16), jax.ShapeDtypeStruct((T, NH), f32)],
        compiler_params=_cp("parallel"),
    )(q, k, v, bias_tab)


CONV_TT = 256
HALO = 16


def _halo_specs(tt, w, nrows_blocks):
    per = tt // HALO
    prev = pl.BlockSpec((HALO, w), lambda i: (jnp.maximum(i * per - 1, 0), 0))
    cur = pl.BlockSpec((tt, w), lambda i: (i, 0))
    nxt = pl.BlockSpec((HALO, w), lambda i: (jnp.minimum((i + 1) * per, nrows_blocks - 1), 0))
    return [prev, cur, nxt]


def _conf_fwd(a, g, conv_w, conv_b, ln_g, ln_b):
    tt = CONV_TT
    nt = T // tt
    sub = 32

    def body(ap, ac, an, gp, gc, gn, w_ref, b_ref, lg_ref, lb_ref, y_ref, cv_ref, win_ref):
        i = pl.program_id(0)
        win_ref[0:HALO, :] = jnp.where(i > 0, ap[...] * _sigmoid(gp[...]), 0.0)
        win_ref[HALO:HALO + tt, :] = ac[...] * _sigmoid(gc[...])
        win_ref[HALO + tt:, :] = jnp.where(i < nt - 1, an[...] * _sigmoid(gn[...]), 0.0)
        w = w_ref[...]
        for s in range(tt // sub):
            acc = jnp.zeros((sub, DA), f32)
            for j in range(CW):
                acc = acc + win_ref[pl.ds(s * sub + 1 + j, sub), :] * w[j:j + 1, :]
            cv = acc + b_ref[...]
            cv_ref[pl.ds(s * sub, sub), :] = cv
            mu = jnp.mean(cv, axis=-1, keepdims=True)
            xc = cv - mu
            rstd = lax.rsqrt(jnp.mean(xc * xc, axis=-1, keepdims=True) + EPS)
            z = xc * rstd * lg_ref[...] + lb_ref[...]
            y_ref[pl.ds(s * sub, sub), :] = (z * _sigmoid(z)).astype(bf16)

    hs = _halo_specs(tt, DA, T // HALO)
    return pl.pallas_call(
        body, name="conf_fwd", grid=(nt,),
        in_specs=hs + hs + [_full((CW, DA)), _full((1, DA)), _full((1, DA)), _full((1, DA))],
        out_specs=[pl.BlockSpec((tt, DA), lambda i: (i, 0)), pl.BlockSpec((tt, DA), lambda i: (i, 0))],
        out_shape=[jax.ShapeDtypeStruct((T, DA), bf16), jax.ShapeDtypeStruct((T, DA), f32)],
        scratch_shapes=[pltpu.VMEM((tt + 2 * HALO, DA), f32)],
        compiler_params=_cp("parallel"),
    )(a, a, a, g, g, g, conv_w, conv_b, ln_g, ln_b)


def _out_proj(xa, y_na, y_cv, w_out, modv, g2):
    tm = 256

    def body(x_ref, ya_ref, yc_ref, w_ref, mod_ref, g_ref, x1_ref, pj_ref, h2_ref):
        proj = _dot(ya_ref[...], w_ref[0:DA, :]) + _dot(yc_ref[...], w_ref[DA:D, :])
        x1 = x_ref[...] + mod_ref[2:3, :] * proj
        x1_ref[...] = x1
        pj_ref[...] = proj.astype(bf16)
        rstd = lax.rsqrt(jnp.mean(x1 * x1, axis=-1, keepdims=True) + EPS)
        h2_ref[...] = ((x1 * rstd * g_ref[...]) * (1.0 + mod_ref[4:5, :]) + mod_ref[3:4, :]).astype(bf16)

    row = lambda w: pl.BlockSpec((tm, w), lambda i: (i, 0))
    return pl.pallas_call(
        body, name="out_proj", grid=(T // tm,),
        in_specs=[row(D), row(DA), row(DA), _full((D, D)), _full((8, D)), _full((1, D))],
        out_specs=[row(D), row(D), row(D)],
        out_shape=[jax.ShapeDtypeStruct((T, D), f32), jax.ShapeDtypeStruct((T, D), bf16), jax.ShapeDtypeStruct((T, D), bf16)],
        compiler_params=_cp("parallel"),
    )(xa, y_na, y_cv, w_out, modv, g2)


FFN_TT = 512
FFN_CT = 256
FFN_NC = F // FFN_CT


def _ffn_fwd(h2, w_up_r, cw_r, cb_r, w_down):
    tt, ct = FFN_TT, FFN_CT
    nt = T // tt

    def body(hp, hc, hn, w_ref, cw_ref, cb_ref, wd_ref, o_ref, hwin, uwin):
        t = pl.program_id(0)
        c = pl.program_id(1)

        @pl.when(c == 0)
        def _():
            hwin[0:HALO, :] = jnp.where(t > 0, hp[...], jnp.zeros_like(hp[...]))
            hwin[HALO:HALO + tt, :] = hc[...]
            hwin[HALO + tt:, :] = jnp.where(t < nt - 1, hn[...], jnp.zeros_like(hn[...]))
            o_ref[...] = jnp.zeros_like(o_ref)

        uwin[...] = _dg(hwin[...], w_ref[...], NT)
        cw = cw_ref[...]
        u2 = (uwin[pl.ds(HALO - 1, tt), :] * cw[0:1, :] + uwin[pl.ds(HALO, tt), :] * cw[1:2, :]
              + uwin[pl.ds(HALO + 1, tt), :] * cw[2:3, :] + cb_ref[...])
        gate = u2[:, :ct]
        val = u2[:, ct:]
        act = (gate * _sigmoid(gate) * val).astype(bf16)
        o_ref[...] += _dot(act, wd_ref[...])

    hs = [pl.BlockSpec((HALO, D), lambda t, c: (jnp.maximum(t * (tt // HALO) - 1, 0), 0)),
          pl.BlockSpec((tt, D), lambda t, c: (t, 0)),
          pl.BlockSpec((HALO, D), lambda t, c: (jnp.minimum((t + 1) * (tt // HALO), T // HALO - 1), 0))]
    return pl.pallas_call(
        body, name="ffn_fwd", grid=(nt, FFN_NC),
        in_specs=hs + [pl.BlockSpec((2 * ct, D), lambda t, c: (c, 0)), pl.BlockSpec((3, 2 * ct), lambda t, c: (0, c)),
                       pl.BlockSpec((1, 2 * ct), lambda t, c: (0, c)), pl.BlockSpec((ct, D), lambda t, c: (c, 0))],
        out_specs=pl.BlockSpec((tt, D), lambda t, c: (t, 0)),
        out_shape=jax.ShapeDtypeStruct((T, D), f32),
        scratch_shapes=[pltpu.VMEM((tt + 2 * HALO, D), bf16), pltpu.VMEM((tt + 2 * HALO, 2 * ct), f32)],
        compiler_params=_cp("parallel", "arbitrary"),
    )(h2, h2, h2, w_up_r, cw_r, cb_r, w_down)


def _loss_bwd(ffn, x1, tgt, modv, gf):
    tm = 256
    nt = T // tm

    def body(f_ref, x1_ref, t_ref, mod_ref, g_ref, dx2_ref, df_ref, s_ref):
        i = pl.program_id(0)

        @pl.when(i == 0)
        def _():
            s_ref[...] = jnp.zeros_like(s_ref)

        ff = f_ref[...]
        gt2 = mod_ref[5:6, :]
        x2 = x1_ref[...] + gt2 * ff
        rstd = lax.rsqrt(jnp.mean(x2 * x2, axis=-1, keepdims=True) + EPS)
        xh = x2 * rstd
        gfv = g_ref[...]
        e = xh * gfv - t_ref[...]
        dy = e * (1.0 / D)
        dxh = dy * gfv
        dx2 = rstd * (dxh - xh * jnp.mean(dxh * xh, axis=-1, keepdims=True))
        dx2_ref[...] = dx2
        df_ref[...] = (dx2 * gt2).astype(bf16)
        s_ref[0:1, :] += jnp.sum(dy * xh, axis=0, keepdims=True)
        s_ref[1:2, :] += jnp.sum(dx2 * ff, axis=0, keepdims=True)
        s_ref[2:3, :] += jnp.sum(e * e, axis=0, keepdims=True)

        @pl.when(i == nt - 1)
        def _():
            tot = jnp.sum(s_ref[2:3, :], axis=-1, keepdims=True) * (0.5 / D)
            s_ref[3:4, :] = jnp.broadcast_to(tot, (1, D))

    row = lambda: pl.BlockSpec((tm, D), lambda i: (i, 0))
    return pl.pallas_call(
        body, name="loss_bwd", grid=(nt,),
        in_specs=[row(), row(), row(), _full((8, D)), _full((1, D))],
        out_specs=[row(), row(), _full((8, D))],
        out_shape=[jax.ShapeDtypeStruct((T, D), f32), jax.ShapeDtypeStruct((T, D), bf16), jax.ShapeDtypeStruct((8, D), f32)],
        compiler_params=_cp("arbitrary"),
    )(ffn, x1, tgt, modv, gf)


def _ffn_bwd(h2, dffn, w_up_r, cw_r, cb_r, w_down):
    tt, ct = FFN_TT, FFN_CT
    nt = T // tt
    wn = tt + 2 * HALO

    def body(hp, hc, hn, dp, dc, dn, w_ref, cw_ref, cb_ref, wd_ref,
             du_ref, dwu_ref, dwd_ref, dcw_ref, dcb_ref, hwin, dwin, uwin, d2win, dawin, actwin):
        c = pl.program_id(0)
        t = pl.program_id(1)
        first, last = t == 0, t == nt - 1
        zero = jnp.zeros((HALO, D), bf16)
        hwin[0:HALO, :] = jnp.where(first, zero, hp[...])
        hwin[HALO:HALO + tt, :] = hc[...]
        hwin[HALO + tt:, :] = jnp.where(last, zero, hn[...])
        dwin[0:HALO, :] = jnp.where(first, zero, dp[...])
        dwin[HALO:HALO + tt, :] = dc[...]
        dwin[HALO + tt:, :] = jnp.where(last, zero, dn[...])

        @pl.when(first)
        def _():
            dwu_ref[...] = jnp.zeros_like(dwu_ref)
            dwd_ref[...] = jnp.zeros_like(dwd_ref)
            dcw_ref[...] = jnp.zeros_like(dcw_ref)
            dcb_ref[...] = jnp.zeros_like(dcb_ref)

        uwin[...] = _dg(hwin[...], w_ref[...], NT)
        cw = cw_ref[...]
        m = wn - 2
        u2 = (uwin[pl.ds(0, m), :] * cw[0:1, :] + uwin[pl.ds(1, m), :] * cw[1:2, :]
              + uwin[pl.ds(2, m), :] * cw[2:3, :] + cb_ref[...])
        gate, val = u2[:, :ct], u2[:, ct:]
        sg = _sigmoid(gate)
        silu = gate * sg
        dawin[...] = _dg(dwin[...], wd_ref[...], NT)
        dact = dawin[pl.ds(1, m), :]
        d2win[pl.ds(1, m), :ct] = dact * val * (sg * (1.0 + gate * (1.0 - sg)))
        d2win[pl.ds(1, m), ct:] = dact * silu
        actwin[pl.ds(1, m), :] = silu * val
        act = actwin[pl.ds(HALO, tt), :].astype(bf16)
        dwd_ref[...] += _dg(act, dc[...], TN)
        d2c = d2win[pl.ds(HALO, tt), :]
        dcb_ref[...] += jnp.sum(d2c, axis=0, keepdims=True)
        for kk in range(3):
            dcw_ref[kk:kk + 1, :] += jnp.sum(uwin[pl.ds(HALO - 1 + kk, tt), :] * d2c, axis=0, keepdims=True)
        du = (d2win[pl.ds(HALO + 1, tt), :] * cw[0:1, :] + d2c * cw[1:2, :]
              + d2win[pl.ds(HALO - 1, tt), :] * cw[2:3, :]).astype(bf16)
        du_ref[...] = du
        dwu_ref[...] += _dg(du, hc[...], TN)

    per = tt // HALO
    halo = lambda: [pl.BlockSpec((HALO, D), lambda c, t: (jnp.maximum(t * per - 1, 0), 0)),
                    pl.BlockSpec((tt, D), lambda c, t: (t, 0)),
                    pl.BlockSpec((HALO, D), lambda c, t: (jnp.minimum((t + 1) * per, T // HALO - 1), 0))]
    return pl.pallas_call(
        body, name="ffn_bwd", grid=(FFN_NC, nt),
        in_specs=halo() + halo() + [
            pl.BlockSpec((2 * ct, D), lambda c, t: (c, 0)), pl.BlockSpec((3, 2 * ct), lambda c, t: (0, c)),
            pl.BlockSpec((1, 2 * ct), lambda c, t: (0, c)), pl.BlockSpec((ct, D), lambda c, t: (c, 0))],
        out_specs=[pl.BlockSpec((tt, 2 * ct), lambda c, t: (t, c)), pl.BlockSpec((2 * ct, D), lambda c, t: (c, 0)),
                   pl.BlockSpec((ct, D), lambda c, t: (c, 0)), pl.BlockSpec((3, 2 * ct), lambda c, t: (0, c)),
                   pl.BlockSpec((1, 2 * ct), lambda c, t: (0, c))],
        out_shape=[jax.ShapeDtypeStruct((T, F2), bf16), jax.ShapeDtypeStruct((F2, D), f32), jax.ShapeDtypeStruct((F, D), f32),
                   jax.ShapeDtypeStruct((3, F2), f32), jax.ShapeDtypeStruct((1, F2), f32)],
        scratch_shapes=[pltpu.VMEM((wn, D), bf16), pltpu.VMEM((wn, D), bf16),
                        pltpu.VMEM((wn, 2 * ct), f32), pltpu.VMEM((wn, 2 * ct), f32),
                        pltpu.VMEM((wn, ct), f32), pltpu.VMEM((wn, ct), f32)],
        compiler_params=_cp("parallel", "arbitrary"),
    )(h2, h2, h2, dffn, dffn, dffn, w_up_r, cw_r, cb_r, w_down)


def _norm_bwd(dh, xv, gain, sh_sc, rstd):
    xh = xv * rstd
    n = xh * gain
    dn = dh * (1.0 + sh_sc)
    dxh = dn * gain
    dx = rstd * (dxh - xh * jnp.mean(dxh * xh, axis=-1, keepdims=True))
    return (dx, jnp.sum(dh, axis=0, keepdims=True), jnp.sum(dh * n, axis=0, keepdims=True),
            jnp.sum(dn * xh, axis=0, keepdims=True))


def _norm2_bwd(du, w_up_r, x1, dx2, proj, w_out, modv, g2):
    tm = 256
    nt = T // tm

    def body(du_ref, w_ref, x1_ref, dx2_ref, pj_ref, wo_ref, mod_ref, g_ref, dx1_ref, dya_ref, dyc_ref, dpj_ref, s_ref):
        i = pl.program_id(0)

        @pl.when(i == 0)
        def _():
            s_ref[...] = jnp.zeros_like(s_ref)

        dh2 = _dot(du_ref[...], w_ref[...])
        x1 = x1_ref[...]
        rstd = lax.rsqrt(jnp.mean(x1 * x1, axis=-1, keepdims=True) + EPS)
        dxn, dsh, dsc, dgn = _norm_bwd(dh2, x1, g_ref[...], mod_ref[4:5, :], rstd)
        dx1 = dx2_ref[...] + dxn
        dx1_ref[...] = dx1
        dpj = (dx1 * mod_ref[2:3, :]).astype(bf16)
        dpj_ref[...] = dpj
        dyc = _dg(dpj, wo_ref[...], NT)
        dya_ref[...] = dyc[:, :DA].astype(bf16)
        dyc_ref[...] = dyc[:, DA:]
        s_ref[0:1, :] += dsh
        s_ref[1:2, :] += dsc
        s_ref[2:3, :] += dgn
        s_ref[3:4, :] += jnp.sum(dx1 * pj_ref[...].astype(f32), axis=0, keepdims=True)

    row = lambda w: pl.BlockSpec((tm, w), lambda i: (i, 0))
    return pl.pallas_call(
        body, name="norm2_bwd", grid=(nt,),
        in_specs=[row(F2), _full((F2, D)), row(D), row(D), row(D), _full((D, D)), _full((8, D)), _full((1, D))],
        out_specs=[row(D), row(DA), row(DA), row(D), _full((8, D))],
        out_shape=[jax.ShapeDtypeStruct((T, D), f32), jax.ShapeDtypeStruct((T, DA), bf16), jax.ShapeDtypeStruct((T, DA), f32),
                   jax.ShapeDtypeStruct((T, D), bf16), jax.ShapeDtypeStruct((8, D), f32)],
        compiler_params=_cp("arbitrary"),
    )(du, w_up_r, x1, dx2, proj, w_out, modv, g2)


def _conf_bwd(a, g, cv, dy, conv_w, ln_g, ln_b):
    tt = CONV_TT
    nt = T // tt
    sub = 32
    wn = tt + 2 * HALO

    def body(ap, ac, an, gp, gc, gn, cp_, cc, cn, dp, dc, dn, w_ref, lg_ref, lb_ref,
             da_ref, dg_ref, dcw_ref, s_ref, uwin, dwin, wacc):
        i = pl.program_id(0)
        first, last = i == 0, i == nt - 1

        @pl.when(first)
        def _():
            s_ref[...] = jnp.zeros_like(s_ref)
            wacc[...] = jnp.zeros_like(wacc)

        lg, lb = lg_ref[...], lb_ref[...]

        def ln_bwd(cvv, dyv):
            mu = jnp.mean(cvv, axis=-1, keepdims=True)
            xc = cvv - mu
            rstd = lax.rsqrt(jnp.mean(xc * xc, axis=-1, keepdims=True) + EPS)
            yn = xc * rstd
            z = yn * lg + lb
            sz = _sigmoid(z)
            dz = dyv * (sz * (1.0 + z * (1.0 - sz)))
            dyn = dz * lg
            dcv = rstd * (dyn - jnp.mean(dyn, axis=-1, keepdims=True) - yn * jnp.mean(dyn * yn, axis=-1, keepdims=True))
            return dcv, dz, yn

        uwin[0:HALO, :] = jnp.where(first, 0.0, ap[...] * _sigmoid(gp[...]))
        uwin[HALO + tt:, :] = jnp.where(last, 0.0, an[...] * _sigmoid(gn[...]))
        dwin[0:HALO, :] = jnp.where(first, 0.0, ln_bwd(cp_[...], dp[...])[0])
        dwin[HALO + tt:, :] = jnp.where(last, 0.0, ln_bwd(cn[...], dn[...])[0])
        for s in range(tt // sub):
            rr = pl.ds(s * sub, sub)
            uwin[pl.ds(HALO + s * sub, sub), :] = ac[rr, :] * _sigmoid(gc[rr, :])
            dcv, dz, yn = ln_bwd(cc[rr, :], dc[rr, :])
            dwin[pl.ds(HALO + s * sub, sub), :] = dcv
            s_ref[0:1, :] += jnp.sum(dcv, axis=0, keepdims=True)
            s_ref[1:2, :] += jnp.sum(dz * yn, axis=0, keepdims=True)
            s_ref[2:3, :] += jnp.sum(dz, axis=0, keepdims=True)
        w = w_ref[...]
        for s in range(tt // sub):
            rr = pl.ds(s * sub, sub)
            dcv = dwin[pl.ds(HALO + s * sub, sub), :]
            acc = jnp.zeros((sub, DA), f32)
            for j in range(CW):
                acc = acc + dwin[pl.ds(s * sub + 2 * HALO - 1 - j, sub), :] * w[j:j + 1, :]
                part = uwin[pl.ds(s * sub + 1 + j, sub), :] * dcv
                wacc[j] += part[0:8] + part[8:16] + part[16:24] + part[24:32]
            av, gv = ac[rr, :], gc[rr, :]
            sg = _sigmoid(gv)
            da_ref[rr, :] = (acc * sg).astype(bf16)
            dg_ref[rr, :] = (acc * av * sg * (1.0 - sg)).astype(bf16)

        @pl.when(last)
        def _():
            for j in range(CW):
                dcw_ref[j:j + 1, :] = jnp.sum(wacc[j], axis=0, keepdims=True)
            dcw_ref[CW:CW + 1, :] = jnp.zeros((1, DA), f32)

    hs = _halo_specs(tt, DA, T // HALO)
    return pl.pallas_call(
        body, name="conf_bwd", grid=(nt,),
        in_specs=hs * 4 + [_full((CW, DA)), _full((1, DA)), _full((1, DA))],
        out_specs=[pl.BlockSpec((tt, DA), lambda i: (i, 0)), pl.BlockSpec((tt, DA), lambda i: (i, 0)),
                   _full((CW + 1, DA)), _full((8, DA))],
        out_shape=[jax.ShapeDtypeStruct((T, DA), bf16), jax.ShapeDtypeStruct((T, DA), bf16),
                   jax.ShapeDtypeStruct((CW + 1, DA), f32), jax.ShapeDtypeStruct((8, DA), f32)],
        scratch_shapes=[pltpu.VMEM((wn, DA), f32), pltpu.VMEM((wn, DA), f32), pltpu.VMEM((CW, 8, DA), f32)],
        compiler_params=_cp("arbitrary"),
    )(a, a, a, g, g, g, cv, cv, cv, dy, dy, dy, conv_w, ln_g, ln_b)


def _attn_bwd(q, k, v, y, dy, lse, bias_tab):
    zr = 256

    def body(q_ref, k_ref, v_ref, y_ref, dy_ref, lse_ref, b_ref, dq_ref, dk_hbm, dv_hbm, db_ref, dk_s, dv_s, sem):
        r = pl.program_id(0)

        @pl.when(r == 0)
        def _():
            def z(i, _):
                rr = pl.ds(pl.multiple_of(i * zr, zr), zr)
                dk_s[rr, :] = jnp.zeros((zr, DA), f32)
                dv_s[rr, :] = jnp.zeros((zr, DA), f32)
                return 0
            lax.fori_loop(0, TA // zr, z, 0)

        @pl.when((r <= WR // 2) | (r > GW - WR // 2))
        def _():
            db_ref[...] = jnp.zeros_like(db_ref)

        ks = pl.multiple_of(_win_start(r) * GW, GW)
        win = pl.ds(ks, WR * GW)
        qq, yy, dyy, lse_v = q_ref[...], y_ref[...], dy_ref[...], lse_ref[...]
        for h in range(NH):
            hs = slice(h * HD, (h + 1) * HD)
            qh, doh = qq[:, hs], dyy[:, hs]
            delta = jnp.sum(doh.astype(f32) * yy[:, hs].astype(f32), axis=-1, keepdims=True)
            kw, vw = k_ref[win, hs], v_ref[win, hs]
            kc, vc = k_ref[T:TA, hs], v_ref[T:TA, hs]
            lh = lse_v[:, h:h + 1]
            pl_ = jnp.exp(_dg(qh, kw, NT) * SCALE + b_ref[0, h] - lh)
            pc = jnp.exp(_dg(qh, kc, NT) * SCALE - lh)
            dsl = pl_ * (_dg(doh, vw, NT) - delta)
            dsc = pc * (_dg(doh, vc, NT) - delta)
            db_ref[0, h] += dsl
            dslb, dscb = dsl.astype(bf16), dsc.astype(bf16)
            dq_ref[:, hs] = ((_dot(dslb, kw) + _dot(dscb, kc)) * SCALE).astype(bf16)
            dk_s[win, hs] += _dg(dslb, qh, TN) * SCALE
            dv_s[win, hs] += _dg(pl_.astype(bf16), doh, TN)
            dk_s[T:TA, hs] += _dg(dscb, qh, TN) * SCALE
            dv_s[T:TA, hs] += _dg(pc.astype(bf16), doh, TN)

        @pl.when(r == GW - 1)
        def _():
            c1 = pltpu.make_async_copy(dk_s, dk_hbm, sem.at[0])
            c2 = pltpu.make_async_copy(dv_s, dv_hbm, sem.at[1])
            c1.start()
            c2.start()
            c1.wait()
            c2.wait()

    rowq = lambda: pl.BlockSpec((GW, DA), lambda r: (r, 0))
    return pl.pallas_call(
        body, name="attn_bwd", grid=(GW,),
        in_specs=[rowq(), _full((TA, DA)), _full((TA, DA)), rowq(), rowq(), pl.BlockSpec((GW, NH), lambda r: (r, 0)),
                  pl.BlockSpec((1, NH, GW, WR * GW), lambda r: (_pattern(r), 0, 0, 0))],
        out_specs=[rowq(), pl.BlockSpec(memory_space=pl.ANY), pl.BlockSpec(memory_space=pl.ANY),
                   pl.BlockSpec((1, NH, GW, WR * GW), lambda r: (_pattern(r), 0, 0, 0))],
        out_shape=[jax.ShapeDtypeStruct((T, DA), bf16), jax.ShapeDtypeStruct((TA, DA), f32), jax.ShapeDtypeStruct((TA, DA), f32),
                   jax.ShapeDtypeStruct((8, NH, GW, WR * GW), f32)],
        scratch_shapes=[pltpu.VMEM((TA, DA), f32), pltpu.VMEM((TA, DA), f32), pltpu.SemaphoreType.DMA((2,))],
        compiler_params=_cp("arbitrary"),
    )(q, k, v, y, dy, lse, bias_tab)


def _rpb_reduce(dbias, cotab):
    seg = (np.arange(WR * GW)[:, None] // GW == np.arange(128)[None, :]).astype(np.float32)

    def body(d_ref, c_ref, seg_ref, o_ref, rows):
        dv = d_ref[0, 0]
        ct = c_ref[...]
        for co in range(2 * NCOL - 1):
            rows[co:co + 1, :] = jnp.sum(jnp.where(ct == co, dv, 0.0), axis=0, keepdims=True)
        rows[2 * NCOL - 1:2 * NCOL, :] = jnp.zeros((1, WR * GW), f32)
        rv = rows[...]
        r0 = rv.astype(bf16)
        e1 = rv - r0.astype(f32)
        r1 = e1.astype(bf16)
        r2 = (e1 - r1.astype(f32)).astype(bf16)
        sv = seg_ref[...]
        o_ref[0, 0] = _dot(r0, sv) + _dot(r1, sv) + _dot(r2, sv)

    return pl.pallas_call(
        body, name="rpb_reduce", grid=(8, NH),
        in_specs=[pl.BlockSpec((1, 1, GW, WR * GW), lambda p, h: (p, h, 0, 0)), _full((GW, WR * GW)), _full((WR * GW, 128))],
        out_specs=pl.BlockSpec((1, 1, 32, 128), lambda p, h: (p, h, 0, 0)),
        out_shape=jax.ShapeDtypeStruct((8, NH, 32, 128), f32),
        scratch_shapes=[pltpu.VMEM((32, WR * GW), f32)],
        compiler_params=_cp("parallel", "parallel"),
    )(dbias, cotab, jnp.asarray(seg, dtype=bf16))


def _norm1_bwd(dq, dk, dv, da, dg, w_inT, xa, dx1, modv, g1):
    tm = 256
    nt = TA // tm
    nx = T // tm

    def body(dq_ref, dk_ref, dv_ref, da_ref, dg_ref, w_ref, x_ref, dx1_ref, mod_ref, g_ref, dx_ref, s_ref):
        i = pl.program_id(0)
        is_ctx = i == nt - 1

        @pl.when(i == 0)
        def _():
            s_ref[...] = jnp.zeros_like(s_ref)

        dh = _dot(dk_ref[...].astype(bf16), w_ref[DA:2 * DA, :]) + _dot(dv_ref[...].astype(bf16), w_ref[2 * DA:3 * DA, :])
        lat = (_dot(dq_ref[...], w_ref[0:DA, :]) + _dot(da_ref[...], w_ref[3 * DA:4 * DA, :])
               + _dot(dg_ref[...], w_ref[4 * DA:5 * DA, :]))
        dh = dh + jnp.where(is_ctx, 0.0, lat)
        xv = x_ref[...]
        rstd = lax.rsqrt(jnp.mean(xv * xv, axis=-1, keepdims=True) + EPS)
        sc = jnp.where(is_ctx, mod_ref[7:8, :], mod_ref[1:2, :])
        dxn, dsh, dsc, dgn = _norm_bwd(dh, xv, g_ref[...], sc, rstd)
        dx_ref[...] = jnp.where(is_ctx, 0.0, dx1_ref[...]) + dxn
        s_ref[0:1, :] += jnp.where(is_ctx, 0.0, dsh)
        s_ref[1:2, :] += jnp.where(is_ctx, 0.0, dsc)
        s_ref[2:3, :] += dgn
        s_ref[3:4, :] += jnp.where(is_ctx, dsh, 0.0)
        s_ref[4:5, :] += jnp.where(is_ctx, dsc, 0.0)

    row = lambda w: pl.BlockSpec((tm, w), lambda i: (i, 0))
    lrow = lambda w: pl.BlockSpec((tm, w), lambda i: (jnp.minimum(i, nx - 1), 0))
    return pl.pallas_call(
        body, name="norm1_bwd", grid=(nt,),
        in_specs=[lrow(DA), row(DA), row(DA), lrow(DA), lrow(DA), _full((5 * DA, D)), row(D), lrow(D), _full((8, D)), _full((1, D))],
        out_specs=[row(D), _full((8, D))],
        out_shape=[jax.ShapeDtypeStruct((TA, D), f32), jax.ShapeDtypeStruct((8, D), f32)],
        compiler_params=_cp("arbitrary"),
    )(dq, dk, dv, da, dg, w_inT, xa, dx1, modv, g1)


def _tn_matmul(a, b, name):
    kk, m = a.shape
    n = b.shape[1]
    tk, tm = 256, 256

    def body(a_ref, b_ref, o_ref):
        @pl.when(pl.program_id(1) == 0)
        def _():
            o_ref[...] = jnp.zeros_like(o_ref)
        o_ref[...] += _dg(a_ref[...].astype(bf16), b_ref[...], TN)

    return pl.pallas_call(
        body, name=name, grid=(m // tm, kk // tk),
        in_specs=[pl.BlockSpec((tk, tm), lambda i, j: (j, i)), pl.BlockSpec((tk, n), lambda i, j: (j, 0))],
        out_specs=pl.BlockSpec((tm, n), lambda i, j: (i, 0)),
        out_shape=jax.ShapeDtypeStruct((m, n), f32),
        compiler_params=_cp("parallel", "arbitrary"),
    )(a, b)


def _adamw(w, g, m, v, name):
    r, c = w.shape
    tr = r
    for cand in (256, 128, 64, 32, 16, 8):
        if r % cand == 0 and r > cand:
            tr = cand
            break

    def body(w_ref, g_ref, m_ref, v_ref, d_ref, nm_ref, nv_ref):
        gv = g_ref[...]
        nm = ADAM_B1 * m_ref[...] + (1.0 - ADAM_B1) * gv
        nv = ADAM_B2 * v_ref[...] + (1.0 - ADAM_B2) * (gv * gv)
        m_hat = nm / (1.0 - ADAM_B1 ** ADAM_STEP)
        v_hat = nv / (1.0 - ADAM_B2 ** ADAM_STEP)
        d_ref[...] = -ADAM_LR * (m_hat / (jnp.sqrt(v_hat) + ADAM_EPS) + ADAM_WD * w_ref[...])
        nm_ref[...] = nm
        nv_ref[...] = nv

    spec = pl.BlockSpec((tr, c), lambda i: (i, 0))
    return pl.pallas_call(
        body, name=name, grid=(r // tr,),
        in_specs=[spec] * 4, out_specs=[spec] * 3,
        out_shape=[jax.ShapeDtypeStruct((r, c), f32)] * 3,
        compiler_params=_cp("parallel"),
    )(w, g, m, v)


def _pad_rows128(vec):
    n = vec.shape[0]
    rows = -(-n // 1024) * 8
    return jnp.pad(vec, (0, rows * 128 - n)).reshape(rows, 128)


def _tile_order(wt):
    rest = wt.shape[1:]
    return wt.reshape(2, FFN_NC, FFN_CT, *rest).swapaxes(0, 1).reshape(F2, *rest)


def _untile_order(wt):
    rest = wt.shape[1:]
    return wt.reshape(FFN_NC, 2, FFN_CT, *rest).swapaxes(0, 1).reshape(F2, *rest)


def _local(x0, ctx0, tgt0, modv, w_inT, w_out_f, w_up_r, w_down_f, conv_w_f, fcw_r, fcb_r,
           g_norm1, g_norm2, g_final, rpb0, conv_b, ln_g, ln_b):
    xa = jnp.concatenate([x0, ctx0], axis=0)
    h, q, k, v, a, g = _in_proj(xa, g_norm1, modv, w_inT)
    bias_tab = _bias_table(rpb0)
    y_na, lse = _attn_fwd(q, k, v, bias_tab)
    y_cv, cv = _conf_fwd(a, g, conv_w_f, conv_b, ln_g, ln_b)
    x1, proj, h2 = _out_proj(xa, y_na, y_cv, w_out_f, modv, g_norm2)
    ffn = _ffn_fwd(h2, w_up_r, fcw_r, fcb_r, w_down_f)
    dx2, dffn, s_loss = _loss_bwd(ffn, x1, tgt0, modv, g_final)

    du, dwu_r, dw_down, dfcw_r, dfcb_r = _ffn_bwd(h2, dffn, w_up_r, fcw_r, fcb_r, w_down_f)
    dx1, dy_na, dy_cv, dproj, s_n2 = _norm2_bwd(du, w_up_r, x1, dx2, proj, w_out_f, modv, g_norm2)
    dw_out = jnp.concatenate([_tn_matmul(y_na, dproj, "dw_out_a"), _tn_matmul(y_cv, dproj, "dw_out_c")], axis=0)
    da, dg, dcw, s_cf = _conf_bwd(a, g, cv, dy_cv, conv_w_f, ln_g, ln_b)
    dq, dk, dv, dbias = _attn_bwd(q, k, v, y_na, dy_na, lse, bias_tab)
    cotab, r_hot = _rpb_tables()
    drp = _rpb_reduce(dbias, jnp.asarray(cotab))
    grad_rpb_part = jnp.einsum("phci,pir->hrc", drp[:, :, :2 * NCOL - 1, :WR], jnp.asarray(r_hot),
                               precision=lax.Precision.HIGHEST)
    dxa, s_n1 = _norm1_bwd(dq, dk, dv, da, dg, w_inT, xa, dx1, modv, g_norm1)
    dw_inT = jnp.concatenate([_tn_matmul(dq, h, "dw_in_q"), _tn_matmul(dk, h, "dw_in_k"), _tn_matmul(dv, h, "dw_in_v"),
                              _tn_matmul(da, h, "dw_in_a"), _tn_matmul(dg, h, "dw_in_g")], axis=0)
    return dict(s_loss=s_loss, s_n1=s_n1, s_n2=s_n2, s_cf=s_cf, grad_x=dxa[:T], dw_inT=dw_inT, dw_out=dw_out,
                dw_down=dw_down, dwu_r=dwu_r, dfcw_r=dfcw_r, dfcb_r=dfcb_r, dcw=dcw, drpb=grad_rpb_part)


def kernel(x, c, ctx, c_ctx, w_mod, b_mod, g_norm1, w_in, rpb, conv_w, conv_b, ln_g, ln_b, w_out, g_norm2, w_up, ffn_conv_w, ffn_conv_b, w_down, g_final, loss_target, m_c_ctx, m_w_mod, m_b_mod, m_g_norm1, m_w_in, m_rpb, m_conv_w, m_conv_b, m_ln_g, m_ln_b, m_w_out, m_g_norm2, m_w_up, m_ffn_conv_w, m_ffn_conv_b, m_w_down, m_g_final, v_c_ctx, v_w_mod, v_b_mod, v_g_norm1, v_w_in, v_rpb, v_conv_w, v_conv_b, v_ln_g, v_ln_b, v_w_out, v_g_norm2, v_w_up, v_ffn_conv_w, v_ffn_conv_b, v_w_down, v_g_final):
    me = 4 * lax.axis_index("x") + 2 * lax.axis_index("y") + lax.axis_index("c")
    nmod = w_mod.shape[2]
    n_in = w_in.shape[2]
    n_out = w_out.shape[1]
    n_up = w_up.shape[2]
    n_dn = w_down.shape[1]
    n_cw = conv_w.shape[2]

    slab_a = jnp.concatenate([w_in[0].T, w_out[0]], axis=0).astype(bf16)
    slab_f = jnp.concatenate([w_up[0].T, w_down[0]], axis=0).astype(bf16)
    ga = _allgather2(slab_a, "ag_attn").reshape(NDEV, n_in + n_out, D)
    gf = _allgather2(slab_f, "ag_ffn").reshape(NDEV, n_up + n_dn, D)
    w_inT = ga[:, :n_in].reshape(5 * DA, D)
    w_out_f = ga[:, n_in:].reshape(D, D)
    w_up_r = _tile_order(gf[:, :n_up].reshape(F2, D))
    w_down_f = gf[:, n_up:].reshape(F, D)

    c_all = _small_allgather(c.reshape(8, 128), "ag_c").reshape(NDEV, D)
    cvec = jnp.concatenate([c_all, c_ctx[None, :], jnp.zeros((7, D), f32)], axis=0)
    b_sh = lax.dynamic_slice(b_mod, (0, me * nmod), (1, nmod))
    mod_sh = _mod_fwd(cvec, w_mod[0], b_sh)
    n_modp = 16 * nmod
    payload = jnp.concatenate([mod_sh.reshape(-1), conv_w[0].reshape(-1), ffn_conv_w[0].reshape(-1)])
    flat = _small_allgather(_pad_rows128(payload), "ag_mod").reshape(NDEV, -1)
    mod_all = flat[:, :n_modp].reshape(NDEV, 16, nmod).transpose(1, 0, 2).reshape(16, 6 * D)
    mod_me = lax.dynamic_index_in_dim(mod_all, me, 0, keepdims=False).reshape(6, D)
    mod_c = mod_all[8]
    modv = jnp.concatenate([mod_me, mod_c[None, 0:D], mod_c[None, D:2 * D]], axis=0)
    o1 = n_modp + CW * n_cw
    conv_w_f = flat[:, n_modp:o1].reshape(NDEV, CW, n_cw).transpose(1, 0, 2).reshape(CW, DA)
    fcw_f = flat[:, o1:o1 + 3 * n_up].reshape(NDEV, 3, n_up).transpose(1, 0, 2).reshape(3, F2)
    fcw_r = _tile_order(fcw_f.T).T
    fcb_r = _tile_order(ffn_conv_b[0])[None, :]

    L = _local(x[0], ctx[0], loss_target[0], modv, w_inT, w_out_f, w_up_r, w_down_f, conv_w_f, fcw_r, fcb_r,
               g_norm1, g_norm2, g_final[None, :], rpb[0], conv_b, ln_g, ln_b)
    s_loss, s_n1, s_n2, s_cf = L["s_loss"], L["s_n1"], L["s_n2"], L["s_cf"]
    loss = lax.psum(s_loss[3, 0], ("x", "y", "c"))
    grad_x = L["grad_x"][None]
    dw_inT, dw_out, dw_down, dcw, grad_rpb_part = L["dw_inT"], L["dw_out"], L["dw_down"], L["dcw"], L["drpb"]
    dw_upT = _untile_order(L["dwu_r"])
    dfcw = _untile_order(L["dfcw_r"].T).T
    dfcb = _untile_order(L["dfcb_r"][0])

    small = jnp.concatenate([dcw[:CW].reshape(CW, NDEV, n_cw).transpose(1, 0, 2).reshape(NDEV, CW * n_cw),
                             dfcw.reshape(3, NDEV, n_up).transpose(1, 0, 2).reshape(NDEV, 3 * n_up)], axis=1)
    small = jnp.pad(small.reshape(NDEV, 4, D), ((0, 0), (0, 12), (0, 0)))
    g_a = jnp.concatenate([dw_inT.reshape(NDEV, n_in, D), dw_out.reshape(NDEV, n_out, D)], axis=1).astype(bf16)
    g_f = jnp.concatenate([dw_upT.reshape(NDEV, n_up, D), dw_down.reshape(NDEV, n_dn, D), small], axis=1).astype(bf16)
    r_a = _reduce_scatter2(g_a, "rs_attn")
    r_f = _reduce_scatter2(g_f, "rs_ffn")
    g_w_in = r_a[:n_in].T
    g_w_out = r_a[n_in:]
    g_w_up = r_f[:n_up].T
    g_w_down = r_f[n_up:n_up + n_dn]
    sm = r_f[n_up + n_dn:n_up + n_dn + 4].reshape(-1)
    g_conv_w = sm[:CW * n_cw].reshape(CW, n_cw)
    g_fcw = sm[CW * n_cw:].reshape(3, n_up)

    dmod = jnp.concatenate([s_n1[0], s_n1[1], s_n2[3], s_n2[0], s_n2[1], s_loss[1]])
    dmodc = jnp.concatenate([s_n1[3], s_n1[4]])
    parts = [dmodc, s_n1[2], grad_rpb_part.reshape(-1), s_cf[0], s_cf[1], s_cf[2], s_n2[2], dfcb, s_loss[0]]
    sizes = [p.shape[0] for p in parts]
    pvec = _pad_rows128(jnp.concatenate([dmod] + parts))
    gath = _small_allgather(pvec, "ag_small")
    tot = _sum_rows8(gath, "sum_small").reshape(-1)
    dmod_all = gath.reshape(NDEV, -1)[:, :6 * D]
    offs = np.cumsum([6 * D] + sizes)
    pick = lambda j: tot[offs[j]:offs[j + 1]]
    dmodc_t = jnp.pad(pick(0), (0, 4 * D))
    g_b_mod = (tot[:6 * D] + dmodc_t)[None, :]
    g_g_norm1 = pick(1)[None, :]
    g_rpb = pick(2).reshape(1, NH, 2 * WR - 1, 2 * NCOL - 1)
    g_conv_b, g_ln_g, g_ln_b = pick(3)[None, :], pick(4)[None, :], pick(5)[None, :]
    g_g_norm2 = pick(6)[None, :]
    g_fcb = pick(7)[None, :]
    g_g_final = pick(8)
    dm_rows = jnp.concatenate([dmod_all, dmodc_t[None, :], jnp.zeros((7, 6 * D), f32)], axis=0)
    dm_sh = lax.dynamic_slice(dm_rows, (0, me * nmod), (16, nmod))
    g_w_mod, gc_part = _mod_bwd(cvec, dm_sh, w_mod[0])
    gc_sum = _sum_rows8(_small_allgather(gc_part[0].reshape(8, 128), "ag_cctx"), "sum_cctx").reshape(D)
    sg_c = _sigmoid(c_ctx)
    g_c_ctx = gc_sum * (sg_c * (1.0 + c_ctx * (1.0 - sg_c)))

    big = [("w_mod", w_mod[0], g_w_mod, m_w_mod[0], v_w_mod[0]), ("w_in", w_in[0], g_w_in, m_w_in[0], v_w_in[0]),
           ("w_out", w_out[0], g_w_out, m_w_out[0], v_w_out[0]), ("w_up", w_up[0], g_w_up, m_w_up[0], v_w_up[0]),
           ("w_down", w_down[0], g_w_down, m_w_down[0], v_w_down[0])]
    upd = {n: _adamw(wv, gv, mv, vv, "adamw_" + n) for n, wv, gv, mv, vv in big}
    smalls = [("c_ctx", c_ctx, g_c_ctx, m_c_ctx, v_c_ctx), ("b_mod", b_mod, g_b_mod, m_b_mod, v_b_mod),
              ("g_norm1", g_norm1, g_g_norm1, m_g_norm1, v_g_norm1), ("rpb", rpb, g_rpb, m_rpb, v_rpb),
              ("conv_w", conv_w, g_conv_w[None], m_conv_w, v_conv_w), ("conv_b", conv_b, g_conv_b, m_conv_b, v_conv_b),
              ("ln_g", ln_g, g_ln_g, m_ln_g, v_ln_g), ("ln_b", ln_b, g_ln_b, m_ln_b, v_ln_b),
              ("g_norm2", g_norm2, g_g_norm2, m_g_norm2, v_g_norm2),
              ("ffn_conv_w", ffn_conv_w, g_fcw[None], m_ffn_conv_w, v_ffn_conv_w),
              ("ffn_conv_b", ffn_conv_b, g_fcb, m_ffn_conv_b, v_ffn_conv_b), ("g_final", g_final, g_g_final, m_g_final, v_g_final)]
    packed = [_pad_rows128(jnp.concatenate([t[j].reshape(-1) for t in smalls])) for j in (1, 2, 3, 4)]
    sd, sm_, sv = _adamw(*packed, "adamw_small")
    so = np.cumsum([0] + [int(np.prod(t[1].shape)) for t in smalls])
    for j, t in enumerate(smalls):
        shp = t[1].shape
        upd[t[0]] = tuple(arr.reshape(-1)[so[j]:so[j + 1]].reshape(shp) for arr in (sd, sm_, sv))
    grads = {"c_ctx": g_c_ctx, "w_mod": g_w_mod[None], "b_mod": g_b_mod, "g_norm1": g_g_norm1, "w_in": g_w_in[None],
             "rpb": g_rpb, "conv_w": g_conv_w[None], "conv_b": g_conv_b, "ln_g": g_ln_g, "ln_b": g_ln_b,
             "w_out": g_w_out[None], "g_norm2": g_g_norm2, "w_up": g_w_up[None], "ffn_conv_w": g_fcw[None],
             "ffn_conv_b": g_fcb, "w_down": g_w_down[None], "g_final": g_g_final}
    names = ["c_ctx", "w_mod", "b_mod", "g_norm1", "w_in", "rpb", "conv_w", "conv_b", "ln_g", "ln_b", "w_out", "g_norm2",
             "w_up", "ffn_conv_w", "ffn_conv_b", "w_down", "g_final"]
    shapes = {n: grads[n].shape for n in names}
    outs = [loss, grad_x] + [grads[n] for n in names]
    for j in range(3):
        outs += [upd[n][j].reshape(shapes[n]) for n in names]
    return tuple(outs)
```

```python
import functools

import numpy as np
import jax
import jax.numpy as jnp
from jax import lax
from jax.experimental import pallas as pl
from jax.experimental.pallas import tpu as pltpu

f32 = jnp.float32
bf16 = jnp.bfloat16

D = 1024
T = 4096
TC = 256
TA = T + TC
DA = 512
NH = 8
HD = 64
GW = 64
WR = 8
NCOL = 16
F = 2816
F2 = 2 * F
CW = 31
NDEV = 8
EPS = 1e-6
SCALE = HD ** -0.5
NEG = -1e30
MESH = pl.DeviceIdType.MESH

NT = (((1,), (1,)), ((), ()))
TN = (((0,), (0,)), ((), ()))

ADAM_LR, ADAM_B1, ADAM_B2, ADAM_EPS, ADAM_WD, ADAM_STEP = 0.001, 0.9, 0.999, 1e-08, 0.01, 10

VMEM_LIMIT = 56 * 1024 * 1024


def _cp(*sem):
    return pltpu.CompilerParams(dimension_semantics=sem or None, vmem_limit_bytes=VMEM_LIMIT)


def _dot(a, b):
    return jnp.dot(a, b, preferred_element_type=f32)


def _dg(a, b, dims):
    return lax.dot_general(a, b, dims, preferred_element_type=f32)


def _sigmoid(x):
    return 1.0 / (1.0 + jnp.exp(-x))


def _full(shape):
    n = len(shape)
    return pl.BlockSpec(shape, lambda *_: (0,) * n)


def _my_pos():
    return lax.axis_index("x"), lax.axis_index("y"), lax.axis_index("c")


def _small_allgather(v, name):
    n = v.shape[0]

    def body(v_ref, out_ref, send_sems, recv_sems):
        x, y, c = _my_pos()
        me = 4 * x + 2 * y + c
        out_ref[me] = v_ref[...]
        peers = []
        for k in range(1, NDEV):
            kx, ky, kc = (k >> 2) & 1, (k >> 1) & 1, k & 1
            peers.append((x ^ kx, y ^ ky, c ^ kc))

        def copy(k, slot, to):
            return pltpu.make_async_remote_copy(
                src_ref=v_ref, dst_ref=out_ref.at[slot], send_sem=send_sems.at[k], recv_sem=recv_sems.at[k],
                device_id=to, device_id_type=MESH)

        sends = [copy(k, me, p) for k, p in enumerate(peers)]
        for cp in sends:
            cp.start()
        for k, (px, py, pc) in enumerate(peers):
            copy(k, 4 * px + 2 * py + pc, (x, y, c)).wait_recv()
        for cp in sends:
            cp.wait_send()

    return pl.pallas_call(
        body, name=name,
        out_shape=jax.ShapeDtypeStruct((NDEV, n, 128), f32),
        in_specs=[pl.BlockSpec(memory_space=pltpu.VMEM)],
        out_specs=pl.BlockSpec(memory_space=pltpu.VMEM),
        scratch_shapes=[pltpu.SemaphoreType.DMA((NDEV - 1,)), pltpu.SemaphoreType.DMA((NDEV - 1,))],
    )(v)


def _allgather2(shards, name):
    na = len(shards)

    def body(*refs):
        x_refs, out_refs = refs[:na], refs[na:2 * na]
        send_sems, recv_sems, local_sems = refs[2 * na:]
        x, y, c = _my_pos()
        me, sibling = (x, y, c), (x, y, 1 - c)
        chips = [(1 - x, y), (x, 1 - y), (1 - x, 1 - y)]

        def rows(i, px, py, pc):
            m_per = shards[i].shape[0]
            return out_refs[i].at[pl.ds(pl.multiple_of((4 * px + 2 * py + pc) * m_per, 16), m_per), :]

        def copies(k, block, to, from_shard=False):
            return [pltpu.make_async_remote_copy(
                src_ref=x_refs[i] if from_shard else rows(i, *block), dst_ref=rows(i, *block),
                send_sem=send_sems.at[k * na + i], recv_sem=recv_sems.at[k * na + i], device_id=to, device_id_type=MESH)
                for i in range(na)]

        mine = [pltpu.make_async_copy(x_refs[i], rows(i, *me), local_sems.at[i]) for i in range(na)]
        for cp in mine:
            cp.start()
        first = copies(0, me, sibling, True)
        for j, chip in enumerate(chips):
            first += copies(1 + j, me, (*chip, c), True)
        for cp in first:
            cp.start()
        passed = [copies(4 + j, (*chip, c), sibling) for j, chip in enumerate(chips)]
        for j, chip in enumerate(chips):
            for cp in copies(1 + j, (*chip, c), me):
                cp.wait_recv()
            for cp in passed[j]:
                cp.start()
        for cp in copies(0, sibling, me):
            cp.wait_recv()
        for j, chip in enumerate(chips):
            for cp in copies(4 + j, (*chip, 1 - c), me):
                cp.wait_recv()
        for cp in first + [cp for grp in passed for cp in grp]:
            cp.wait_send()
        for cp in mine:
            cp.wait()

    return pl.pallas_call(
        body, name=name,
        out_shape=[jax.ShapeDtypeStruct((NDEV * s.shape[0], s.shape[1]), s.dtype) for s in shards],
        in_specs=[pl.BlockSpec(memory_space=pltpu.VMEM)] * na,
        out_specs=[pl.BlockSpec(memory_space=pltpu.VMEM)] * na,
        scratch_shapes=[pltpu.SemaphoreType.DMA((7 * na,)), pltpu.SemaphoreType.DMA((7 * na,)), pltpu.SemaphoreType.DMA((na,))],
        compiler_params=pltpu.CompilerParams(vmem_limit_bytes=VMEM_LIMIT),
    )(*shards)


def _reduce_scatter2(g, name):
    _, r, n = g.shape
    ch = 16
    nch = r // ch

    def body(g_ref, out_ref, a_ref, h_ref, b_ref, s1_send, s1_recv, s2_send, s2_recv):
        x, y, c = _my_pos()
        sibling = (x, y, 1 - c)
        s1 = []
        for j in range(4):
            cp = pltpu.make_async_remote_copy(
                src_ref=g_ref.at[2 * j + (1 - c)], dst_ref=a_ref.at[j], send_sem=s1_send.at[j], recv_sem=s1_recv.at[j],
                device_id=sibling, device_id_type=MESH)
            cp.start()
            s1.append(cp)
        for cp in s1:
            cp.wait_recv()

        def add1(i, _):
            rr = pl.ds(pl.multiple_of(i * ch, ch), ch)
            for j in range(4):
                h_ref[j, rr, :] = (g_ref[2 * j + c, rr, :].astype(f32) + a_ref[j, rr, :].astype(f32)).astype(bf16)
            return 0
        lax.fori_loop(0, nch, add1, 0)
        mychip = 2 * x + y
        s2 = []
        for m in range(1, 4):
            mx, my_ = (m >> 1) & 1, m & 1
            px, py = x ^ mx, y ^ my_
            cp = pltpu.make_async_remote_copy(
                src_ref=h_ref.at[2 * px + py], dst_ref=b_ref.at[m - 1], send_sem=s2_send.at[m - 1], recv_sem=s2_recv.at[m - 1],
                device_id=(px, py, c), device_id_type=MESH)
            cp.start()
            s2.append(cp)
        for cp in s2:
            cp.wait_recv()

        def add2(i, _):
            rr = pl.ds(pl.multiple_of(i * ch, ch), ch)
            acc = h_ref[mychip, rr, :].astype(f32)
            for m in range(3):
                acc = acc + b_ref[m, rr, :].astype(f32)
            out_ref[rr, :] = acc
            return 0
        lax.fori_loop(0, nch, add2, 0)
        for cp in s1 + s2:
            cp.wait_send()

    return pl.pallas_call(
        body, name=name,
        out_shape=jax.ShapeDtypeStruct((r, n), f32),
        in_specs=[pl.BlockSpec(memory_space=pltpu.VMEM)],
        out_specs=pl.BlockSpec(memory_space=pltpu.VMEM),
        scratch_shapes=[pltpu.VMEM((4, r, n), bf16), pltpu.VMEM((4, r, n), bf16), pltpu.VMEM((3, r, n), bf16),
                        pltpu.SemaphoreType.DMA((4,)), pltpu.SemaphoreType.DMA((4,)),
                        pltpu.SemaphoreType.DMA((3,)), pltpu.SemaphoreType.DMA((3,))],
        compiler_params=pltpu.CompilerParams(vmem_limit_bytes=VMEM_LIMIT),
    )(g)


def _mod_fwd(cvec, w_sh, b_sh):
    def body(c_ref, w_ref, b_ref, o_ref):
        cv = c_ref[...]
        act = (cv * _sigmoid(cv)).astype(bf16)
        o_ref[...] = _dot(act, w_ref[...].astype(bf16)) + b_ref[...]
    return pl.pallas_call(body, name="mod_fwd", out_shape=jax.ShapeDtypeStruct((16, w_sh.shape[1]), f32))(cvec, w_sh, b_sh)


def _mod_bwd(cvec, dm_sh, w_sh):
    def body(c_ref, dm_ref, w_ref, gw_ref, gc_ref):
        cv = c_ref[...]
        act = (cv * _sigmoid(cv)).astype(bf16)
        gw_ref[...] = _dg(act, dm_ref[...].astype(bf16), TN)
        gc_ref[...] = _dg(dm_ref[8:16, :].astype(bf16), w_ref[...].astype(bf16), NT)
    return pl.pallas_call(
        body, name="mod_bwd",
        out_shape=(jax.ShapeDtypeStruct(w_sh.shape, f32), jax.ShapeDtypeStruct((8, D), f32)))(cvec, dm_sh, w_sh)


def _sum_rows8(a, name):
    n = a.shape[1]

    def body(a_ref, o_ref):
        acc = a_ref[0]
        for d in range(1, NDEV):
            acc = acc + a_ref[d]
        o_ref[...] = acc
    return pl.pallas_call(body, name=name, out_shape=jax.ShapeDtypeStruct((n, 128), f32))(a)


def _in_proj(x0, ctx0, g1, modv, w_inT):
    tm = 256
    nt = TA // tm
    nx = T // tm

    def body(x_ref, c_ref, g_ref, mod_ref, w_ref, h_ref, q_ref, k_ref, v_ref, a_ref, gg_ref):
        i = pl.program_id(0)
        is_ctx = i == nt - 1
        xv = jnp.where(is_ctx, c_ref[...], x_ref[...])
        rstd = lax.rsqrt(jnp.mean(xv * xv, axis=-1, keepdims=True) + EPS)
        sh = jnp.where(is_ctx, mod_ref[6:7, :], mod_ref[0:1, :])
        sc = jnp.where(is_ctx, mod_ref[7:8, :], mod_ref[1:2, :])
        h = ((xv * rstd * g_ref[...]) * (1.0 + sc) + sh).astype(bf16)
        h_ref[...] = h
        for j, o_ref in enumerate((q_ref, k_ref, v_ref, a_ref, gg_ref)):
            o_ref[...] = _dg(h, w_ref[j * DA:(j + 1) * DA, :], NT).astype(o_ref.dtype)

    row = lambda w: pl.BlockSpec((tm, w), lambda i: (i, 0))
    return pl.pallas_call(
        body, name="in_proj", grid=(nt,),
        in_specs=[pl.BlockSpec((tm, D), lambda i: (jnp.minimum(i, nx - 1), 0)), _full((TC, D)),
                  _full((1, D)), _full((8, D)), _full((5 * DA, D))],
        out_specs=[row(D), row(DA), row(DA), row(DA), row(DA), row(DA)],
        out_shape=[jax.ShapeDtypeStruct((TA, D), bf16)] + [jax.ShapeDtypeStruct((TA, DA), bf16)] * 3
                  + [jax.ShapeDtypeStruct((TA, DA), f32)] * 2,
        compiler_params=_cp("parallel"),
    )(x0, ctx0, g1, modv, w_inT)


def _win_start(r):
    return jnp.clip(r - WR // 2, 0, GW - WR)


def _pattern(r):
    return _win_start(r) - r + (WR - 1)


def _bias_table(rpb):
    p = np.arange(8)[:, None]
    i = np.arange(WR)[None, :]
    ro = (p - (WR - 1)) + i + (WR - 1)
    r_hot = (ro[:, :, None] == np.arange(2 * WR - 1)[None, None, :]).astype(np.float32)
    qc = np.arange(GW)[:, None]
    kc = np.arange(GW)[None, :]
    cs = np.clip(qc - NCOL // 2, 0, GW - NCOL)
    valid = (kc >= cs) & (kc < cs + NCOL)
    co = np.clip(kc - qc + (NCOL - 1), 0, 2 * NCOL - 2)
    c_hot = ((co[:, :, None] == np.arange(2 * NCOL - 1)[None, None, :]) & valid[:, :, None]).astype(np.float32)
    hi = lax.Precision.HIGHEST
    t = jnp.einsum("hrc,pir->hpic", rpb, jnp.asarray(r_hot), precision=hi)
    b = jnp.einsum("hpic,qkc->phqik", t, jnp.asarray(c_hot), precision=hi)
    b = jnp.where(jnp.asarray(valid)[None, None, :, None, :], b, NEG)
    return b.reshape(8, NH, GW, WR * GW)


def _rpb_tables():
    lane_map = np.zeros((WR * GW, WR, 2 * NCOL - 1), np.float32)
    for i in range(WR):
        for t in range(GW):
            if t >= GW - NCOL:
                lane_map[i * GW + t, i, (GW - 1 - t) + NCOL - 1] = 1.0
            elif t < NCOL - 1:
                lane_map[i * GW + t, (i - 1) % WR, NCOL - 2 - t] = 1.0
    p = np.arange(8)[:, None]
    i = np.arange(WR)[None, :]
    r_hot = ((p + i)[:, :, None] == np.arange(2 * WR - 1)[None, None, :]).astype(np.float32)
    return lane_map, r_hot


def _attn_fwd(q, k, v, bias_tab):
    def body(q_ref, k_ref, v_ref, b_ref, y_ref, lse_ref):
        r = pl.program_id(0)
        ks = pl.multiple_of(_win_start(r) * GW, GW)
        qq = q_ref[...]
        lo = lax.broadcasted_iota(jnp.int32, (GW, 2 * HD), 1) < HD
        for pr in range(NH // 2):
            ps = slice(pr * 2 * HD, (pr + 1) * 2 * HD)
            q2 = qq[:, ps]
            kw = k_ref[pl.ds(ks, WR * GW), ps]
            vw = v_ref[pl.ds(ks, WR * GW), ps]
            kc = k_ref[T:TA, ps]
            vc = v_ref[T:TA, ps]
            outs = []
            for s in range(2):
                h = 2 * pr + s
                qm = jnp.where(lo if s == 0 else ~lo, q2, jnp.zeros_like(q2))
                sl = _dg(qm, kw, NT) * SCALE + b_ref[0, h]
                sc = _dg(qm, kc, NT) * SCALE
                m = jnp.maximum(jnp.max(sl, axis=-1, keepdims=True), jnp.max(sc, axis=-1, keepdims=True))
                pl_ = jnp.exp(sl - m)
                pc = jnp.exp(sc - m)
                l = jnp.sum(pl_, axis=-1, keepdims=True) + jnp.sum(pc, axis=-1, keepdims=True)
                outs.append((_dot(pl_.astype(bf16), vw) + _dot(pc.astype(bf16), vc)) / l)
                lse_ref[:, h:h + 1] = m + jnp.log(l)
            y_ref[:, ps] = jnp.where(lo, outs[0], outs[1]).astype(bf16)

    return pl.pallas_call(
        body, name="attn_fwd", grid=(GW,),
        in_specs=[pl.BlockSpec((GW, DA), lambda r: (r, 0)), _full((TA, DA)), _full((TA, DA)),
                  pl.BlockSpec((1, NH, GW, WR * GW), lambda r: (_pattern(r), 0, 0, 0))],
        out_specs=[pl.BlockSpec((GW, DA), lambda r: (r, 0)), pl.BlockSpec((GW, NH), lambda r: (r, 0))],
        out_shape=[jax.ShapeDtypeStruct((T, DA), bf16), jax.ShapeDtypeStruct((T, NH), f32)],
        compiler_params=_cp("parallel"),
    )(q, k, v, bias_tab)


CONV_TT = 256
HALO = 16


def _halo_specs(tt, w, nrows_blocks):
    per = tt // HALO
    prev = pl.BlockSpec((HALO, w), lambda i: (jnp.maximum(i * per - 1, 0), 0))
    cur = pl.BlockSpec((tt, w), lambda i: (i, 0))
    nxt = pl.BlockSpec((HALO, w), lambda i: (jnp.minimum((i + 1) * per, nrows_blocks - 1), 0))
    return [prev, cur, nxt]


def _shifted_copies(rot, wn):
    for b in range(1, 8):
        rot[b, 0:wn - 8, :] = rot[0, pl.ds(b, wn - 8), :]


def _conf_fwd(a, g, conv_w, conv_b, ln_g, ln_b):
    tt = CONV_TT
    nt = T // tt
    sub = 32
    wn = tt + 2 * HALO

    def body(ap, ac, an, gp, gc, gn, w_ref, b_ref, lg_ref, lb_ref, y_ref, cv_ref, rot):
        i = pl.program_id(0)
        rot[0, 0:HALO, :] = jnp.where(i > 0, ap[...] * _sigmoid(gp[...]), 0.0)
        rot[0, HALO:HALO + tt, :] = ac[...] * _sigmoid(gc[...])
        rot[0, HALO + tt:, :] = jnp.where(i < nt - 1, an[...] * _sigmoid(gn[...]), 0.0)
        _shifted_copies(rot, wn)
        w = w_ref[...]
        for s in range(tt // sub):
            acc = jnp.zeros((sub, DA), f32)
            for j in range(CW):
                a8, b8 = divmod(1 + j, 8)
                acc = acc + rot[b8, pl.ds(s * sub + 8 * a8, sub), :] * w[j:j + 1, :]
            cv = acc + b_ref[...]
            cv_ref[pl.ds(s * sub, sub), :] = cv
            mu = jnp.mean(cv, axis=-1, keepdims=True)
            xc = cv - mu
            rstd = lax.rsqrt(jnp.mean(xc * xc, axis=-1, keepdims=True) + EPS)
            z = xc * rstd * lg_ref[...] + lb_ref[...]
            y_ref[pl.ds(s * sub, sub), :] = (z * _sigmoid(z)).astype(bf16)

    hs = _halo_specs(tt, DA, T // HALO)
    return pl.pallas_call(
        body, name="conf_fwd", grid=(nt,),
        in_specs=hs + hs + [_full((CW, DA)), _full((1, DA)), _full((1, DA)), _full((1, DA))],
        out_specs=[pl.BlockSpec((tt, DA), lambda i: (i, 0)), pl.BlockSpec((tt, DA), lambda i: (i, 0))],
        out_shape=[jax.ShapeDtypeStruct((T, DA), bf16), jax.ShapeDtypeStruct((T, DA), f32)],
        scratch_shapes=[pltpu.VMEM((8, wn, DA), f32)],
        compiler_params=_cp("parallel"),
    )(a, a, a, g, g, g, conv_w, conv_b, ln_g, ln_b)


def _out_proj(xa, y_na, y_cv, w_out, modv, g2):
    tm = 256

    def body(x_ref, ya_ref, yc_ref, w_ref, mod_ref, g_ref, x1_ref, pj_ref, h2_ref):
        proj = _dot(ya_ref[...], w_ref[0:DA, :]) + _dot(yc_ref[...], w_ref[DA:D, :])
        x1 = x_ref[...] + mod_ref[2:3, :] * proj
        x1_ref[...] = x1
        pj_ref[...] = proj.astype(bf16)
        rstd = lax.rsqrt(jnp.mean(x1 * x1, axis=-1, keepdims=True) + EPS)
        h2_ref[...] = ((x1 * rstd * g_ref[...]) * (1.0 + mod_ref[4:5, :]) + mod_ref[3:4, :]).astype(bf16)

    row = lambda w: pl.BlockSpec((tm, w), lambda i: (i, 0))
    return pl.pallas_call(
        body, name="out_proj", grid=(T // tm,),
        in_specs=[row(D), row(DA), row(DA), _full((D, D)), _full((8, D)), _full((1, D))],
        out_specs=[row(D), row(D), row(D)],
        out_shape=[jax.ShapeDtypeStruct((T, D), f32), jax.ShapeDtypeStruct((T, D), bf16), jax.ShapeDtypeStruct((T, D), bf16)],
        compiler_params=_cp("parallel"),
    )(xa, y_na, y_cv, w_out, modv, g2)


FFN_TT = 512
FFN_CT = 256
FFN_NC = F // FFN_CT


def _ffn_specs(tt, ct, by_token_first):
    tc = (lambda f: (lambda t, c: f(t, c))) if by_token_first else (lambda f: (lambda c, t: f(t, c)))
    per = tt // HALO
    halo = [pl.BlockSpec((HALO, D), tc(lambda t, c: (jnp.maximum(t * per - 1, 0), 0))),
            pl.BlockSpec((tt, D), tc(lambda t, c: (t, 0))),
            pl.BlockSpec((HALO, D), tc(lambda t, c: (jnp.minimum((t + 1) * per, T // HALO - 1), 0)))]
    weights = [pl.BlockSpec((ct, D), tc(lambda t, c: (c, 0))), pl.BlockSpec((ct, D), tc(lambda t, c: (c + FFN_NC, 0))),
               pl.BlockSpec((3, ct), tc(lambda t, c: (0, c))), pl.BlockSpec((3, ct), tc(lambda t, c: (0, c + FFN_NC))),
               pl.BlockSpec((1, ct), tc(lambda t, c: (0, c))), pl.BlockSpec((1, ct), tc(lambda t, c: (0, c + FFN_NC))),
               pl.BlockSpec((ct, D), tc(lambda t, c: (c, 0)))]
    return halo, weights


def _ffn_fwd(h2, w_upT, fcw, fcb, w_down):
    tt, ct = FFN_TT, FFN_CT
    nt = T // tt
    wn = tt + 2 * HALO
    half = tt // 2

    def body(hp, hc, hn, wg_ref, wv_ref, cwg_ref, cwv_ref, cbg_ref, cbv_ref, wd_ref, o_ref, hwin, uwin):
        t = pl.program_id(0)
        c = pl.program_id(1)

        @pl.when(c == 0)
        def _():
            hwin[0:HALO, :] = jnp.where(t > 0, hp[...], jnp.zeros_like(hp[...]))
            hwin[HALO:HALO + tt, :] = hc[...]
            hwin[HALO + tt:, :] = jnp.where(t < nt - 1, hn[...], jnp.zeros_like(hn[...]))
            o_ref[...] = jnp.zeros_like(o_ref)

        for r0, r1 in ((0, half + 2 * HALO), (half + 2 * HALO, wn)):
            hw = hwin[r0:r1, :]
            uwin[r0:r1, :ct] = _dg(hw, wg_ref[...], NT)
            uwin[r0:r1, ct:] = _dg(hw, wv_ref[...], NT)
        cw = jnp.concatenate([cwg_ref[...], cwv_ref[...]], axis=1)
        cb = jnp.concatenate([cbg_ref[...], cbv_ref[...]], axis=1)
        for p in range(2):
            base = HALO + p * half
            u2 = (uwin[pl.ds(base - 1, half), :] * cw[0:1, :] + uwin[pl.ds(base, half), :] * cw[1:2, :]
                  + uwin[pl.ds(base + 1, half), :] * cw[2:3, :] + cb)
            gate = u2[:, :ct]
            val = u2[:, ct:]
            act = (gate * _sigmoid(gate) * val).astype(bf16)
            o_ref[p * half:(p + 1) * half, :] += _dot(act, wd_ref[...])

    halo, weights = _ffn_specs(tt, ct, True)
    return pl.pallas_call(
        body, name="ffn_fwd", grid=(nt, FFN_NC),
        in_specs=halo + weights,
        out_specs=pl.BlockSpec((tt, D), lambda t, c: (t, 0)),
        out_shape=jax.ShapeDtypeStruct((T, D), f32),
        scratch_shapes=[pltpu.VMEM((wn, D), bf16), pltpu.VMEM((wn, 2 * ct), f32)],
        compiler_params=_cp("parallel", "arbitrary"),
    )(h2, h2, h2, w_upT, w_upT, fcw, fcw, fcb, fcb, w_down)


def _loss_bwd(ffn, x1, tgt, modv, gf):
    tm = 256
    nt = T // tm

    def body(f_ref, x1_ref, t_ref, mod_ref, g_ref, dx2_ref, df_ref, s_ref):
        i = pl.program_id(0)

        @pl.when(i == 0)
        def _():
            s_ref[...] = jnp.zeros_like(s_ref)

        ff = f_ref[...]
        gt2 = mod_ref[5:6, :]
        x2 = x1_ref[...] + gt2 * ff
        rstd = lax.rsqrt(jnp.mean(x2 * x2, axis=-1, keepdims=True) + EPS)
        xh = x2 * rstd
        gfv = g_ref[...]
        e = xh * gfv - t_ref[...]
        dy = e * (1.0 / D)
        dxh = dy * gfv
        dx2 = rstd * (dxh - xh * jnp.mean(dxh * xh, axis=-1, keepdims=True))
        dx2_ref[...] = dx2
        df_ref[...] = (dx2 * gt2).astype(bf16)
        s_ref[0:1, :] += jnp.sum(dy * xh, axis=0, keepdims=True)
        s_ref[1:2, :] += jnp.sum(dx2 * ff, axis=0, keepdims=True)
        s_ref[2:3, :] += jnp.sum(e * e, axis=0, keepdims=True)

        @pl.when(i == nt - 1)
        def _():
            tot = jnp.sum(s_ref[2:3, :], axis=-1, keepdims=True) * (0.5 / D)
            s_ref[3:4, :] = jnp.broadcast_to(tot, (1, D))

    row = lambda: pl.BlockSpec((tm, D), lambda i: (i, 0))
    return pl.pallas_call(
        body, name="loss_bwd", grid=(nt,),
        in_specs=[row(), row(), row(), _full((8, D)), _full((1, D))],
        out_specs=[row(), row(), _full((8, D))],
        out_shape=[jax.ShapeDtypeStruct((T, D), f32), jax.ShapeDtypeStruct((T, D), bf16), jax.ShapeDtypeStruct((8, D), f32)],
        compiler_params=_cp("arbitrary"),
    )(ffn, x1, tgt, modv, gf)


def _ffn_bwd(h2, dffn, w_upT, fcw, fcb, w_down):
    tt, ct = FFN_TT, FFN_CT
    nt = T // tt
    wn = tt + 2 * HALO
    half = tt // 2

    def body(hp, hc, hn, dp, dc, dn, wg_ref, wv_ref, cwg_ref, cwv_ref, cbg_ref, cbv_ref, wd_ref,
             dug_ref, duv_ref, dwg_ref, dwv_ref, dwd_ref, dcwg_ref, dcwv_ref, dcbg_ref, dcbv_ref,
             hwin, dwin, uwin, d2win, dawin, actwin):
        t = pl.program_id(1)
        first, last = t == 0, t == nt - 1
        zero = jnp.zeros((HALO, D), bf16)
        hwin[0:HALO, :] = jnp.where(first, zero, hp[...])
        hwin[HALO:HALO + tt, :] = hc[...]
        hwin[HALO + tt:, :] = jnp.where(last, zero, hn[...])
        dwin[0:HALO, :] = jnp.where(first, zero, dp[...])
        dwin[HALO:HALO + tt, :] = dc[...]
        dwin[HALO + tt:, :] = jnp.where(last, zero, dn[...])

        @pl.when(first)
        def _():
            for r in (dwg_ref, dwv_ref, dwd_ref, dcwg_ref, dcwv_ref, dcbg_ref, dcbv_ref):
                r[...] = jnp.zeros_like(r)

        cw = jnp.concatenate([cwg_ref[...], cwv_ref[...]], axis=1)
        cb = jnp.concatenate([cbg_ref[...], cbv_ref[...]], axis=1)
        split = half + 2 * HALO
        for p in range(2):
            r0, r1 = (0, split) if p == 0 else (split, wn)
            hw = hwin[r0:r1, :]
            uwin[r0:r1, :ct] = _dg(hw, wg_ref[...], NT)
            uwin[r0:r1, ct:] = _dg(hw, wv_ref[...], NT)
            dawin[r0:r1, :] = _dg(dwin[r0:r1, :], wd_ref[...], NT)
            e0, e1 = (1, split - 15) if p == 0 else (split - 15, wn - 1)
            m = e1 - e0
            u2 = (uwin[pl.ds(e0 - 1, m), :] * cw[0:1, :] + uwin[pl.ds(e0, m), :] * cw[1:2, :]
                  + uwin[pl.ds(e0 + 1, m), :] * cw[2:3, :] + cb)
            gate, val = u2[:, :ct], u2[:, ct:]
            sg = _sigmoid(gate)
            silu = gate * sg
            dact = dawin[pl.ds(e0, m), :]
            d2win[pl.ds(e0, m), :ct] = dact * val * (sg * (1.0 + gate * (1.0 - sg)))
            d2win[pl.ds(e0, m), ct:] = dact * silu
            actwin[pl.ds(e0, m), :] = silu * val
            b0 = HALO + p * half
            rows = slice(p * half, (p + 1) * half)
            act = actwin[pl.ds(b0, half), :].astype(bf16)
            dwd_ref[...] += _dg(act, dc[rows, :], TN)
            d2c = d2win[pl.ds(b0, half), :]
            dcb = jnp.sum(d2c, axis=0, keepdims=True)
            dcbg_ref[...] += dcb[:, :ct]
            dcbv_ref[...] += dcb[:, ct:]
            for kk in range(3):
                dck = jnp.sum(uwin[pl.ds(b0 - 1 + kk, half), :] * d2c, axis=0, keepdims=True)
                dcwg_ref[kk:kk + 1, :] += dck[:, :ct]
                dcwv_ref[kk:kk + 1, :] += dck[:, ct:]
            du = (d2win[pl.ds(b0 + 1, half), :] * cw[0:1, :] + d2c * cw[1:2, :]
                  + d2win[pl.ds(b0 - 1, half), :] * cw[2:3, :]).astype(bf16)
            dug_ref[rows, :] = du[:, :ct]
            duv_ref[rows, :] = du[:, ct:]
            dwu = _dg(du, hc[rows, :], TN)
            dwg_ref[...] += dwu[:ct, :]
            dwv_ref[...] += dwu[ct:, :]

    halo, weights = _ffn_specs(tt, ct, False)
    tile = lambda: pl.BlockSpec((ct, D), lambda c, t: (c, 0))
    lane = lambda r: pl.BlockSpec((r, ct), lambda c, t: (0, c))
    return pl.pallas_call(
        body, name="ffn_bwd", grid=(FFN_NC, nt),
        in_specs=halo + halo + weights,
        out_specs=[pl.BlockSpec((tt, ct), lambda c, t: (t, c)), pl.BlockSpec((tt, ct), lambda c, t: (t, c)),
                   tile(), tile(), tile(), lane(3), lane(3), lane(1), lane(1)],
        out_shape=[jax.ShapeDtypeStruct((T, F), bf16), jax.ShapeDtypeStruct((T, F), bf16),
                   jax.ShapeDtypeStruct((F, D), f32), jax.ShapeDtypeStruct((F, D), f32), jax.ShapeDtypeStruct((F, D), f32),
                   jax.ShapeDtypeStruct((3, F), f32), jax.ShapeDtypeStruct((3, F), f32),
                   jax.ShapeDtypeStruct((1, F), f32), jax.ShapeDtypeStruct((1, F), f32)],
        scratch_shapes=[pltpu.VMEM((wn, D), bf16), pltpu.VMEM((wn, D), bf16),
                        pltpu.VMEM((wn, 2 * ct), f32), pltpu.VMEM((wn, 2 * ct), f32),
                        pltpu.VMEM((wn, ct), f32), pltpu.VMEM((wn, ct), f32)],
        compiler_params=_cp("parallel", "arbitrary"),
    )(h2, h2, h2, dffn, dffn, dffn, w_upT, w_upT, fcw, fcw, fcb, fcb, w_down)


def _norm_bwd(dh, xv, gain, sh_sc, rstd):
    xh = xv * rstd
    n = xh * gain
    dn = dh * (1.0 + sh_sc)
    dxh = dn * gain
    dx = rstd * (dxh - xh * jnp.mean(dxh * xh, axis=-1, keepdims=True))
    return (dx, jnp.sum(dh, axis=0, keepdims=True), jnp.sum(dh * n, axis=0, keepdims=True),
            jnp.sum(dn * xh, axis=0, keepdims=True))


def _norm2_bwd(dug, duv, w_upT, x1, dx2, proj, w_out, y_na, y_cv, modv, g2):
    tm = 256
    nt = T // tm

    def body(dug_ref, duv_ref, w_ref, x1_ref, dx2_ref, pj_ref, wo_ref, ya_ref, yc_ref, mod_ref, g_ref,
             dx1_ref, dya_ref, dyc_ref, dwo_ref, s_ref):
        i = pl.program_id(0)

        @pl.when(i == 0)
        def _():
            s_ref[...] = jnp.zeros_like(s_ref)
            dwo_ref[...] = jnp.zeros_like(dwo_ref)

        dh2 = _dot(dug_ref[...], w_ref[0:F, :]) + _dot(duv_ref[...], w_ref[F:F2, :])
        x1 = x1_ref[...]
        rstd = lax.rsqrt(jnp.mean(x1 * x1, axis=-1, keepdims=True) + EPS)
        dxn, dsh, dsc, dgn = _norm_bwd(dh2, x1, g_ref[...], mod_ref[4:5, :], rstd)
        dx1 = dx2_ref[...] + dxn
        dx1_ref[...] = dx1
        dpj = (dx1 * mod_ref[2:3, :]).astype(bf16)
        dyc = _dg(dpj, wo_ref[...], NT)
        dya_ref[...] = dyc[:, :DA].astype(bf16)
        dyc_ref[...] = dyc[:, DA:]
        dwo_ref[0:DA, :] += _dg(ya_ref[...], dpj, TN)
        dwo_ref[DA:D, :] += _dg(yc_ref[...], dpj, TN)
        s_ref[0:1, :] += dsh
        s_ref[1:2, :] += dsc
        s_ref[2:3, :] += dgn
        s_ref[3:4, :] += jnp.sum(dx1 * pj_ref[...].astype(f32), axis=0, keepdims=True)

    row = lambda w: pl.BlockSpec((tm, w), lambda i: (i, 0))
    return pl.pallas_call(
        body, name="norm2_bwd", grid=(nt,),
        in_specs=[row(F), row(F), _full((F2, D)), row(D), row(D), row(D), _full((D, D)), row(DA), row(DA),
                  _full((8, D)), _full((1, D))],
        out_specs=[row(D), row(DA), row(DA), _full((D, D)), _full((8, D))],
        out_shape=[jax.ShapeDtypeStruct((T, D), f32), jax.ShapeDtypeStruct((T, DA), bf16), jax.ShapeDtypeStruct((T, DA), f32),
                   jax.ShapeDtypeStruct((D, D), f32), jax.ShapeDtypeStruct((8, D), f32)],
        compiler_params=_cp("arbitrary"),
    )(dug, duv, w_upT, x1, dx2, proj, w_out, y_na, y_cv, modv, g2)


def _conf_bwd(a, g, cv, dy, conv_w, ln_g, ln_b):
    tt = CONV_TT
    nt = T // tt
    sub = 32
    wn = tt + 2 * HALO

    def body(ap, ac, an, gp, gc, gn, cp_, cc, cn, dp, dc, dn, w_ref, lg_ref, lb_ref,
             da_ref, dg_ref, dcw_ref, s_ref, urot, drot, wacc):
        i = pl.program_id(0)
        first, last = i == 0, i == nt - 1

        @pl.when(first)
        def _():
            s_ref[...] = jnp.zeros_like(s_ref)
            wacc[...] = jnp.zeros_like(wacc)

        lg, lb = lg_ref[...], lb_ref[...]

        def ln_bwd(cvv, dyv):
            mu = jnp.mean(cvv, axis=-1, keepdims=True)
            xc = cvv - mu
            rstd = lax.rsqrt(jnp.mean(xc * xc, axis=-1, keepdims=True) + EPS)
            yn = xc * rstd
            z = yn * lg + lb
            sz = _sigmoid(z)
            dz = dyv * (sz * (1.0 + z * (1.0 - sz)))
            dyn = dz * lg
            dcv = rstd * (dyn - jnp.mean(dyn, axis=-1, keepdims=True) - yn * jnp.mean(dyn * yn, axis=-1, keepdims=True))
            return dcv, dz, yn

        urot[0, 0:HALO, :] = jnp.where(first, 0.0, ap[...] * _sigmoid(gp[...]))
        urot[0, HALO + tt:, :] = jnp.where(last, 0.0, an[...] * _sigmoid(gn[...]))
        drot[0, 0:HALO, :] = jnp.where(first, 0.0, ln_bwd(cp_[...], dp[...])[0])
        drot[0, HALO + tt:, :] = jnp.where(last, 0.0, ln_bwd(cn[...], dn[...])[0])
        for s in range(tt // sub):
            rr = pl.ds(s * sub, sub)
            urot[0, pl.ds(HALO + s * sub, sub), :] = ac[rr, :] * _sigmoid(gc[rr, :])
            dcv, dz, yn = ln_bwd(cc[rr, :], dc[rr, :])
            drot[0, pl.ds(HALO + s * sub, sub), :] = dcv
            s_ref[0:1, :] += jnp.sum(dcv, axis=0, keepdims=True)
            s_ref[1:2, :] += jnp.sum(dz * yn, axis=0, keepdims=True)
            s_ref[2:3, :] += jnp.sum(dz, axis=0, keepdims=True)
        _shifted_copies(urot, wn)
        _shifted_copies(drot, wn)
        w = w_ref[...]
        for s in range(tt // sub):
            rr = pl.ds(s * sub, sub)
            dcv = drot[0, pl.ds(HALO + s * sub, sub), :]
            acc = jnp.zeros((sub, DA), f32)
            for j in range(CW):
                ad, bd = divmod(2 * HALO - 1 - j, 8)
                au, bu = divmod(1 + j, 8)
                acc = acc + drot[bd, pl.ds(s * sub + 8 * ad, sub), :] * w[j:j + 1, :]
                part = urot[bu, pl.ds(s * sub + 8 * au, sub), :] * dcv
                wacc[j] += part[0:8] + part[8:16] + part[16:24] + part[24:32]
            av, gv = ac[rr, :], gc[rr, :]
            sg = _sigmoid(gv)
            da_ref[rr, :] = (acc * sg).astype(bf16)
            dg_ref[rr, :] = (acc * av * sg * (1.0 - sg)).astype(bf16)

        @pl.when(last)
        def _():
            for j in range(CW):
                dcw_ref[j:j + 1, :] = jnp.sum(wacc[j], axis=0, keepdims=True)
            dcw_ref[CW:CW + 1, :] = jnp.zeros((1, DA), f32)

    hs = _halo_specs(tt, DA, T // HALO)
    return pl.pallas_call(
        body, name="conf_bwd", grid=(nt,),
        in_specs=hs * 4 + [_full((CW, DA)), _full((1, DA)), _full((1, DA))],
        out_specs=[pl.BlockSpec((tt, DA), lambda i: (i, 0)), pl.BlockSpec((tt, DA), lambda i: (i, 0)),
                   _full((CW + 1, DA)), _full((8, DA))],
        out_shape=[jax.ShapeDtypeStruct((T, DA), bf16), jax.ShapeDtypeStruct((T, DA), bf16),
                   jax.ShapeDtypeStruct((CW + 1, DA), f32), jax.ShapeDtypeStruct((8, DA), f32)],
        scratch_shapes=[pltpu.VMEM((8, wn, DA), f32), pltpu.VMEM((8, wn, DA), f32), pltpu.VMEM((CW, 8, DA), f32)],
        compiler_params=_cp("arbitrary"),
    )(a, a, a, g, g, g, cv, cv, cv, dy, dy, dy, conv_w, ln_g, ln_b)


def _attn_bwd(q, k, v, y, dy, lse, bias_tab):
    zr = 256

    def body(q_ref, k_ref, v_ref, y_ref, dy_ref, lse_ref, b_ref, dq_ref, dk_hbm, dv_hbm, db_ref, dk_s, dv_s, sem):
        r = pl.program_id(0)

        @pl.when(r == 0)
        def _():
            def z(i, _):
                rr = pl.ds(pl.multiple_of(i * zr, zr), zr)
                dk_s[rr, :] = jnp.zeros((zr, DA), f32)
                dv_s[rr, :] = jnp.zeros((zr, DA), f32)
                return 0
            lax.fori_loop(0, TA // zr, z, 0)

        @pl.when((r <= WR // 2) | (r > GW - WR // 2))
        def _():
            db_ref[...] = jnp.zeros_like(db_ref)

        ks = pl.multiple_of(_win_start(r) * GW, GW)
        win = pl.ds(ks, WR * GW)
        qq, yy, dyy, lse_v = q_ref[...], y_ref[...], dy_ref[...], lse_ref[...]
        lo = lax.broadcasted_iota(jnp.int32, (GW, 2 * HD), 1) < HD
        for pr in range(NH // 2):
            ps = slice(pr * 2 * HD, (pr + 1) * 2 * HD)
            q2, do2 = qq[:, ps], dyy[:, ps]
            prod = do2.astype(f32) * yy[:, ps].astype(f32)
            kw, vw = k_ref[win, ps], v_ref[win, ps]
            kc, vc = k_ref[T:TA, ps], v_ref[T:TA, ps]
            dqs = []
            dkw = dvw = dkc = dvc = None
            for s in range(2):
                h = 2 * pr + s
                msk = lo if s == 0 else ~lo
                qm = jnp.where(msk, q2, jnp.zeros_like(q2))
                dom = jnp.where(msk, do2, jnp.zeros_like(do2))
                delta = jnp.sum(jnp.where(msk, prod, 0.0), axis=-1, keepdims=True)
                lh = lse_v[:, h:h + 1]
                pl_ = jnp.exp(_dg(qm, kw, NT) * SCALE + b_ref[0, h] - lh)
                pc = jnp.exp(_dg(qm, kc, NT) * SCALE - lh)
                dsl = pl_ * (_dg(dom, vw, NT) - delta)
                dsc = pc * (_dg(dom, vc, NT) - delta)
                db_ref[0, h] += dsl
                dslb, dscb = dsl.astype(bf16), dsc.astype(bf16)
                dqs.append(_dot(dslb, kw) + _dot(dscb, kc))
                parts = (_dg(dslb, qm, TN), _dg(pl_.astype(bf16), dom, TN), _dg(dscb, qm, TN), _dg(pc.astype(bf16), dom, TN))
                if s == 0:
                    dkw, dvw, dkc, dvc = parts
                else:
                    dkw, dvw, dkc, dvc = dkw + parts[0], dvw + parts[1], dkc + parts[2], dvc + parts[3]
            dq_ref[:, ps] = (jnp.where(lo, dqs[0], dqs[1]) * SCALE).astype(bf16)
            dk_s[win, ps] += dkw * SCALE
            dv_s[win, ps] += dvw
            dk_s[T:TA, ps] += dkc * SCALE
            dv_s[T:TA, ps] += dvc

        @pl.when(r == GW - 1)
        def _():
            c1 = pltpu.make_async_copy(dk_s, dk_hbm, sem.at[0])
            c2 = pltpu.make_async_copy(dv_s, dv_hbm, sem.at[1])
            c1.start()
            c2.start()
            c1.wait()
            c2.wait()

    rowq = lambda: pl.BlockSpec((GW, DA), lambda r: (r, 0))
    return pl.pallas_call(
        body, name="attn_bwd", grid=(GW,),
        in_specs=[rowq(), _full((TA, DA)), _full((TA, DA)), rowq(), rowq(), pl.BlockSpec((GW, NH), lambda r: (r, 0)),
                  pl.BlockSpec((1, NH, GW, WR * GW), lambda r: (_pattern(r), 0, 0, 0))],
        out_specs=[rowq(), pl.BlockSpec(memory_space=pl.ANY), pl.BlockSpec(memory_space=pl.ANY),
                   pl.BlockSpec((1, NH, GW, WR * GW), lambda r: (_pattern(r), 0, 0, 0))],
        out_shape=[jax.ShapeDtypeStruct((T, DA), bf16), jax.ShapeDtypeStruct((TA, DA), f32), jax.ShapeDtypeStruct((TA, DA), f32),
                   jax.ShapeDtypeStruct((8, NH, GW, WR * GW), f32)],
        scratch_shapes=[pltpu.VMEM((TA, DA), f32), pltpu.VMEM((TA, DA), f32), pltpu.SemaphoreType.DMA((2,))],
        compiler_params=_cp("arbitrary"),
    )(q, k, v, y, dy, lse, bias_tab)


def _rpb_reduce(dbias):
    rev = np.zeros((WR * GW, WR * GW), np.float32)
    for i in range(WR):
        for kk in range(GW):
            rev[i * GW + kk, i * GW + GW - 1 - kk] = 1.0

    def body(d_ref, rev_ref, o_ref):
        rv = rev_ref[...]
        for h in range(NH):
            dv = d_ref[0, h]
            r0 = dv.astype(bf16)
            e1 = dv - r0.astype(f32)
            r1 = e1.astype(bf16)
            r2 = (e1 - r1.astype(f32)).astype(bf16)
            rr = _dot(r0, rv) + _dot(r1, rv) + _dot(r2, rv)
            skew = pltpu.roll(rr, 0, 1, stride=1, stride_axis=0)
            o_ref[0, h:h + 1, :] = jnp.sum(skew, axis=0, keepdims=True)

    return pl.pallas_call(
        body, name="rpb_reduce", grid=(8,),
        in_specs=[pl.BlockSpec((1, NH, GW, WR * GW), lambda p: (p, 0, 0, 0)), _full((WR * GW, WR * GW))],
        out_specs=pl.BlockSpec((1, NH, WR * GW), lambda p: (p, 0, 0)),
        out_shape=jax.ShapeDtypeStruct((8, NH, WR * GW), f32),
        compiler_params=_cp("parallel"),
    )(dbias, jnp.asarray(rev, dtype=bf16))


def _norm1_bwd(dq, dk, dv, da, dg, w_inT, x0, ctx0, h, dx1, modv, g1):
    tm = 256
    nt = TA // tm
    nx = T // tm

    def body(dq_ref, dk_ref, dv_ref, da_ref, dg_ref, w_ref, x_ref, c_ref, h_ref, dx1_ref, mod_ref, g_ref,
             dx_ref, dw_ref, s_ref):
        i = pl.program_id(0)
        is_ctx = i == nt - 1

        @pl.when(i == 0)
        def _():
            s_ref[...] = jnp.zeros_like(s_ref)
            dw_ref[...] = jnp.zeros_like(dw_ref)

        hb = h_ref[...]
        dkb, dvb = dk_ref[...].astype(bf16), dv_ref[...].astype(bf16)
        dw_ref[DA:2 * DA, :] += _dg(dkb, hb, TN)
        dw_ref[2 * DA:3 * DA, :] += _dg(dvb, hb, TN)
        dh_kv = _dot(dkb, w_ref[DA:2 * DA, :]) + _dot(dvb, w_ref[2 * DA:3 * DA, :])
        gain = g_ref[...]

        @pl.when(is_ctx)
        def _():
            xv = c_ref[...]
            rstd = lax.rsqrt(jnp.mean(xv * xv, axis=-1, keepdims=True) + EPS)
            _, dsh, dsc, dgn = _norm_bwd(dh_kv, xv, gain, mod_ref[7:8, :], rstd)
            s_ref[2:3, :] += dgn
            s_ref[3:4, :] += dsh
            s_ref[4:5, :] += dsc

        @pl.when(jnp.logical_not(is_ctx))
        def _():
            dqb, dab, dgb = dq_ref[...], da_ref[...], dg_ref[...]
            dw_ref[0:DA, :] += _dg(dqb, hb, TN)
            dw_ref[3 * DA:4 * DA, :] += _dg(dab, hb, TN)
            dw_ref[4 * DA:5 * DA, :] += _dg(dgb, hb, TN)
            dh = (dh_kv + _dot(dqb, w_ref[0:DA, :]) + _dot(dab, w_ref[3 * DA:4 * DA, :])
                  + _dot(dgb, w_ref[4 * DA:5 * DA, :]))
            xv = x_ref[...]
            rstd = lax.rsqrt(jnp.mean(xv * xv, axis=-1, keepdims=True) + EPS)
            dxn, dsh, dsc, dgn = _norm_bwd(dh, xv, gain, mod_ref[1:2, :], rstd)
            dx_ref[...] = dx1_ref[...] + dxn
            s_ref[0:1, :] += dsh
            s_ref[1:2, :] += dsc
            s_ref[2:3, :] += dgn

    row = lambda w: pl.BlockSpec((tm, w), lambda i: (i, 0))
    lrow = lambda w: pl.BlockSpec((tm, w), lambda i: (jnp.minimum(i, nx - 1), 0))
    return pl.pallas_call(
        body, name="norm1_bwd", grid=(nt,),
        in_specs=[lrow(DA), row(DA), row(DA), lrow(DA), lrow(DA), _full((5 * DA, D)), lrow(D), _full((TC, D)), row(D),
                  lrow(D), _full((8, D)), _full((1, D))],
        out_specs=[lrow(D), _full((5 * DA, D)), _full((8, D))],
        out_shape=[jax.ShapeDtypeStruct((T, D), f32), jax.ShapeDtypeStruct((5 * DA, D), f32), jax.ShapeDtypeStruct((8, D), f32)],
        compiler_params=_cp("arbitrary"),
    )(dq, dk, dv, da, dg, w_inT, x0, ctx0, h, dx1, modv, g1)


def _adamw(w, g, m, v, name):
    r, c = w.shape
    tr = r
    for cand in (256, 128, 64, 32, 16, 8):
        if r % cand == 0 and r > cand:
            tr = cand
            break

    def body(w_ref, g_ref, m_ref, v_ref, d_ref, nm_ref, nv_ref):
        gv = g_ref[...]
        nm = ADAM_B1 * m_ref[...] + (1.0 - ADAM_B1) * gv
        nv = ADAM_B2 * v_ref[...] + (1.0 - ADAM_B2) * (gv * gv)
        m_hat = nm / (1.0 - ADAM_B1 ** ADAM_STEP)
        v_hat = nv / (1.0 - ADAM_B2 ** ADAM_STEP)
        d_ref[...] = -ADAM_LR * (m_hat / (jnp.sqrt(v_hat) + ADAM_EPS) + ADAM_WD * w_ref[...])
        nm_ref[...] = nm
        nv_ref[...] = nv

    spec = pl.BlockSpec((tr, c), lambda i: (i, 0))
    return pl.pallas_call(
        body, name=name, grid=(r // tr,),
        in_specs=[spec] * 4, out_specs=[spec] * 3,
        out_shape=[jax.ShapeDtypeStruct((r, c), f32)] * 3,
        compiler_params=_cp("parallel"),
    )(w, g, m, v)


def _pad_rows128(vec):
    n = vec.shape[0]
    rows = -(-n // 1024) * 8
    return jnp.pad(vec, (0, rows * 128 - n)).reshape(rows, 128)


def _local(x0, ctx0, tgt0, modv, w_inT, w_out_f, w_upT, w_down_f, conv_w_f, fcw, fcb,
           g_norm1, g_norm2, g_final, rpb0, conv_b, ln_g, ln_b):
    h, q, k, v, a, g = _in_proj(x0, ctx0, g_norm1, modv, w_inT)
    bias_tab = _bias_table(rpb0)
    y_na, lse = _attn_fwd(q, k, v, bias_tab)
    y_cv, cv = _conf_fwd(a, g, conv_w_f, conv_b, ln_g, ln_b)
    x1, proj, h2 = _out_proj(x0, y_na, y_cv, w_out_f, modv, g_norm2)
    ffn = _ffn_fwd(h2, w_upT, fcw, fcb, w_down_f)
    dx2, dffn, s_loss = _loss_bwd(ffn, x1, tgt0, modv, g_final)

    dug, duv, dwg, dwv, dw_down, dcwg, dcwv, dcbg, dcbv = _ffn_bwd(h2, dffn, w_upT, fcw, fcb, w_down_f)
    dx1, dy_na, dy_cv, dw_out, s_n2 = _norm2_bwd(dug, duv, w_upT, x1, dx2, proj, w_out_f, y_na, y_cv, modv, g_norm2)
    da, dg, dcw, s_cf = _conf_bwd(a, g, cv, dy_cv, conv_w_f, ln_g, ln_b)
    dq, dk, dv, dbias = _attn_bwd(q, k, v, y_na, dy_na, lse, bias_tab)
    lane_map, r_hot = _rpb_tables()
    grad_rpb_part = jnp.einsum("phl,lic,pir->hrc", _rpb_reduce(dbias), jnp.asarray(lane_map), jnp.asarray(r_hot),
                               precision=lax.Precision.HIGHEST)
    grad_x, dw_inT, s_n1 = _norm1_bwd(dq, dk, dv, da, dg, w_inT, x0, ctx0, h, dx1, modv, g_norm1)
    return dict(s_loss=s_loss, s_n1=s_n1, s_n2=s_n2, s_cf=s_cf, grad_x=grad_x, dw_inT=dw_inT, dw_out=dw_out,
                dw_down=dw_down, dw_upT=jnp.concatenate([dwg, dwv], axis=0), dfcw=jnp.concatenate([dcwg, dcwv], axis=1),
                dfcb=jnp.concatenate([dcbg[0], dcbv[0]]), dcw=dcw, drpb=grad_rpb_part)


def kernel(x, c, ctx, c_ctx, w_mod, b_mod, g_norm1, w_in, rpb, conv_w, conv_b, ln_g, ln_b, w_out, g_norm2, w_up, ffn_conv_w, ffn_conv_b, w_down, g_final, loss_target, m_c_ctx, m_w_mod, m_b_mod, m_g_norm1, m_w_in, m_rpb, m_conv_w, m_conv_b, m_ln_g, m_ln_b, m_w_out, m_g_norm2, m_w_up, m_ffn_conv_w, m_ffn_conv_b, m_w_down, m_g_final, v_c_ctx, v_w_mod, v_b_mod, v_g_norm1, v_w_in, v_rpb, v_conv_w, v_conv_b, v_ln_g, v_ln_b, v_w_out, v_g_norm2, v_w_up, v_ffn_conv_w, v_ffn_conv_b, v_w_down, v_g_final):
    me = 4 * lax.axis_index("x") + 2 * lax.axis_index("y") + lax.axis_index("c")
    nmod = w_mod.shape[2]
    n_in = w_in.shape[2]
    n_out = w_out.shape[1]
    n_up = w_up.shape[2]
    n_dn = w_down.shape[1]
    n_cw = conv_w.shape[2]

    w_inT, w_out_f = _allgather2([w_in[0].T.astype(bf16), w_out[0].astype(bf16)], "ag_attn")
    w_upT, w_down_f = _allgather2([w_up[0].T.astype(bf16), w_down[0].astype(bf16)], "ag_ffn")

    c_all = _small_allgather(c.reshape(8, 128), "ag_c").reshape(NDEV, D)
    cvec = jnp.concatenate([c_all, c_ctx[None, :], jnp.zeros((7, D), f32)], axis=0)
    b_sh = lax.dynamic_slice(b_mod, (0, me * nmod), (1, nmod))
    mod_sh = _mod_fwd(cvec, w_mod[0], b_sh)
    n_modp = 16 * nmod
    payload = jnp.concatenate([mod_sh.reshape(-1), conv_w[0].reshape(-1), ffn_conv_w[0].reshape(-1)])
    flat = _small_allgather(_pad_rows128(payload), "ag_mod").reshape(NDEV, -1)
    mod_all = flat[:, :n_modp].reshape(NDEV, 16, nmod).transpose(1, 0, 2).reshape(16, 6 * D)
    mod_me = lax.dynamic_index_in_dim(mod_all, me, 0, keepdims=False).reshape(6, D)
    mod_c = mod_all[8]
    modv = jnp.concatenate([mod_me, mod_c[None, 0:D], mod_c[None, D:2 * D]], axis=0)
    o1 = n_modp + CW * n_cw
    conv_w_f = flat[:, n_modp:o1].reshape(NDEV, CW, n_cw).transpose(1, 0, 2).reshape(CW, DA)
    fcw_f = flat[:, o1:o1 + 3 * n_up].reshape(NDEV, 3, n_up).transpose(1, 0, 2).reshape(3, F2)

    L = _local(x[0], ctx[0], loss_target[0], modv, w_inT, w_out_f, w_upT, w_down_f, conv_w_f, fcw_f, ffn_conv_b,
               g_norm1, g_norm2, g_final[None, :], rpb[0], conv_b, ln_g, ln_b)
    s_loss, s_n1, s_n2, s_cf = L["s_loss"], L["s_n1"], L["s_n2"], L["s_cf"]
    grad_x = L["grad_x"][None]
    dw_inT, dw_out, dw_down, dcw, grad_rpb_part = L["dw_inT"], L["dw_out"], L["dw_down"], L["dcw"], L["drpb"]
    dw_upT, dfcw, dfcb = L["dw_upT"], L["dfcw"], L["dfcb"]

    small = jnp.concatenate([dcw[:CW].reshape(CW, NDEV, n_cw).transpose(1, 0, 2).reshape(NDEV, CW * n_cw),
                             dfcw.reshape(3, NDEV, n_up).transpose(1, 0, 2).reshape(NDEV, 3 * n_up)], axis=1)
    small = jnp.pad(small.reshape(NDEV, 4, D), ((0, 0), (0, 12), (0, 0)))
    g_a = jnp.concatenate([dw_inT.reshape(NDEV, n_in, D), dw_out.reshape(NDEV, n_out, D)], axis=1).astype(bf16)
    g_f = jnp.concatenate([dw_upT.reshape(NDEV, n_up, D), dw_down.reshape(NDEV, n_dn, D), small], axis=1).astype(bf16)
    r_a = _reduce_scatter2(g_a, "rs_attn")
    r_f = _reduce_scatter2(g_f, "rs_ffn")
    g_w_in = r_a[:n_in].T
    g_w_out = r_a[n_in:]
    g_w_up = r_f[:n_up].T
    g_w_down = r_f[n_up:n_up + n_dn]
    sm = r_f[n_up + n_dn:n_up + n_dn + 4].reshape(-1)
    g_conv_w = sm[:CW * n_cw].reshape(CW, n_cw)
    g_fcw = sm[CW * n_cw:].reshape(3, n_up)

    dmod = jnp.concatenate([s_n1[0], s_n1[1], s_n2[3], s_n2[0], s_n2[1], s_loss[1]])
    dmodc = jnp.concatenate([s_n1[3], s_n1[4]])
    parts = [dmodc, s_n1[2], grad_rpb_part.reshape(-1), s_cf[0], s_cf[1], s_cf[2], s_n2[2], dfcb, s_loss[0], s_loss[3, 0:1]]
    sizes = [p.shape[0] for p in parts]
    pvec = _pad_rows128(jnp.concatenate([dmod] + parts))
    gath = _small_allgather(pvec, "ag_small")
    tot = _sum_rows8(gath, "sum_small").reshape(-1)
    dmod_all = gath.reshape(NDEV, -1)[:, :6 * D]
    offs = np.cumsum([6 * D] + sizes)
    pick = lambda j: tot[offs[j]:offs[j + 1]]
    dmodc_t = jnp.pad(pick(0), (0, 4 * D))
    g_b_mod = (tot[:6 * D] + dmodc_t)[None, :]
    g_g_norm1 = pick(1)[None, :]
    g_rpb = pick(2).reshape(1, NH, 2 * WR - 1, 2 * NCOL - 1)
    g_conv_b, g_ln_g, g_ln_b = pick(3)[None, :], pick(4)[None, :], pick(5)[None, :]
    g_g_norm2 = pick(6)[None, :]
    g_fcb = pick(7)[None, :]
    g_g_final = pick(8)
    loss = pick(9)[0]
    dm_rows = jnp.concatenate([dmod_all, dmodc_t[None, :], jnp.zeros((7, 6 * D), f32)], axis=0)
    dm_sh = lax.dynamic_slice(dm_rows, (0, me * nmod), (16, nmod))
    g_w_mod, gc_part = _mod_bwd(cvec, dm_sh, w_mod[0])
    gc_sum = _sum_rows8(_small_allgather(gc_part[0].reshape(8, 128), "ag_cctx"), "sum_cctx").reshape(D)
    sg_c = _sigmoid(c_ctx)
    g_c_ctx = gc_sum * (sg_c * (1.0 + c_ctx * (1.0 - sg_c)))

    big = [("w_mod", w_mod[0], g_w_mod, m_w_mod[0], v_w_mod[0]), ("w_in", w_in[0], g_w_in, m_w_in[0], v_w_in[0]),
           ("w_out", w_out[0], g_w_out, m_w_out[0], v_w_out[0]), ("w_up", w_up[0], g_w_up, m_w_up[0], v_w_up[0]),
           ("w_down", w_down[0], g_w_down, m_w_down[0], v_w_down[0])]
    upd = {n: _adamw(wv, gv, mv, vv, "adamw_" + n) for n, wv, gv, mv, vv in big}
    smalls = [("c_ctx", c_ctx, g_c_ctx, m_c_ctx, v_c_ctx), ("b_mod", b_mod, g_b_mod, m_b_mod, v_b_mod),
              ("g_norm1", g_norm1, g_g_norm1, m_g_norm1, v_g_norm1), ("rpb", rpb, g_rpb, m_rpb, v_rpb),
              ("conv_w", conv_w, g_conv_w[None], m_conv_w, v_conv_w), ("conv_b", conv_b, g_conv_b, m_conv_b, v_conv_b),
              ("ln_g", ln_g, g_ln_g, m_ln_g, v_ln_g), ("ln_b", ln_b, g_ln_b, m_ln_b, v_ln_b),
              ("g_norm2", g_norm2, g_g_norm2, m_g_norm2, v_g_norm2),
              ("ffn_conv_w", ffn_conv_w, g_fcw[None], m_ffn_conv_w, v_ffn_conv_w),
              ("ffn_conv_b", ffn_conv_b, g_fcb, m_ffn_conv_b, v_ffn_conv_b), ("g_final", g_final, g_g_final, m_g_final, v_g_final)]
    packed = [_pad_rows128(jnp.concatenate([t[j].reshape(-1) for t in smalls])) for j in (1, 2, 3, 4)]
    sd, sm_, sv = _adamw(*packed, "adamw_small")
    so = np.cumsum([0] + [int(np.prod(t[1].shape)) for t in smalls])
    for j, t in enumerate(smalls):
        shp = t[1].shape
        upd[t[0]] = tuple(arr.reshape(-1)[so[j]:so[j + 1]].reshape(shp) for arr in (sd, sm_, sv))
    grads = {"c_ctx": g_c_ctx, "w_mod": g_w_mod[None], "b_mod": g_b_mod, "g_norm1": g_g_norm1, "w_in": g_w_in[None],
             "rpb": g_rpb, "conv_w": g_conv_w[None], "conv_b": g_conv_b, "ln_g": g_ln_g, "ln_b": g_ln_b,
             "w_out": g_w_out[None], "g_norm2": g_g_norm2, "w_up": g_w_up[None], "ffn_conv_w": g_fcw[None],
             "ffn_conv_b": g_fcb, "w_down": g_w_down[None], "g_final": g_g_final}
    names = ["c_ctx", "w_mod", "b_mod", "g_norm1", "w_in", "rpb", "conv_w", "conv_b", "ln_g", "ln_b", "w_out", "g_norm2",
             "w_up", "ffn_conv_w", "ffn_conv_b", "w_down", "g_final"]
    shapes = {n: grads[n].shape for n in names}
    outs = [loss, grad_x] + [grads[n] for n in names]
    for j in range(3):
        outs += [upd[n][j].reshape(shapes[n]) for n in names]
    return tuple(outs)
```

```python
import functools

import numpy as np
import jax
import jax.numpy as jnp
from jax import lax
from jax.experimental import pallas as pl
from jax.experimental.pallas import tpu as pltpu

f32 = jnp.float32
bf16 = jnp.bfloat16

D = 1024
T = 4096
TC = 256
TA = T + TC
DA = 512
NH = 8
HD = 64
GW = 64
WR = 8
NCOL = 16
F = 2816
F2 = 2 * F
CW = 31
NDEV = 8
EPS = 1e-6
SCALE = HD ** -0.5
NEG = -1e30
MESH = pl.DeviceIdType.MESH

NT = (((1,), (1,)), ((), ()))
TN = (((0,), (0,)), ((), ()))

ADAM_LR, ADAM_B1, ADAM_B2, ADAM_EPS, ADAM_WD, ADAM_STEP = 0.001, 0.9, 0.999, 1e-08, 0.01, 10

VMEM_LIMIT = 56 * 1024 * 1024


def _cp(*sem):
    return pltpu.CompilerParams(dimension_semantics=sem or None, vmem_limit_bytes=VMEM_LIMIT)


def _dot(a, b):
    return jnp.dot(a, b, preferred_element_type=f32)


def _dg(a, b, dims):
    return lax.dot_general(a, b, dims, preferred_element_type=f32)


def _sigmoid(x):
    return 1.0 / (1.0 + jnp.exp(-x))


def _full(shape):
    n = len(shape)
    return pl.BlockSpec(shape, lambda *_: (0,) * n)


def _my_pos():
    return lax.axis_index("x"), lax.axis_index("y"), lax.axis_index("c")


def _small_allgather(v, name):
    n = v.shape[0]

    def body(v_ref, out_ref, send_sems, recv_sems):
        x, y, c = _my_pos()
        me = 4 * x + 2 * y + c
        out_ref[me] = v_ref[...]
        peers = []
        for k in range(1, NDEV):
            kx, ky, kc = (k >> 2) & 1, (k >> 1) & 1, k & 1
            peers.append((x ^ kx, y ^ ky, c ^ kc))

        def copy(k, slot, to):
            return pltpu.make_async_remote_copy(
                src_ref=v_ref, dst_ref=out_ref.at[slot], send_sem=send_sems.at[k], recv_sem=recv_sems.at[k],
                device_id=to, device_id_type=MESH)

        sends = [copy(k, me, p) for k, p in enumerate(peers)]
        for cp in sends:
            cp.start()
        for k, (px, py, pc) in enumerate(peers):
            copy(k, 4 * px + 2 * py + pc, (x, y, c)).wait_recv()
        for cp in sends:
            cp.wait_send()

    return pl.pallas_call(
        body, name=name,
        out_shape=jax.ShapeDtypeStruct((NDEV, n, 128), f32),
        in_specs=[pl.BlockSpec(memory_space=pltpu.VMEM)],
        out_specs=pl.BlockSpec(memory_space=pltpu.VMEM),
        scratch_shapes=[pltpu.SemaphoreType.DMA((NDEV - 1,)), pltpu.SemaphoreType.DMA((NDEV - 1,))],
    )(v)


def _ag2_plan(x_refs, out_refs, send_sems, recv_sems, local_sems):
    na = len(x_refs)
    x, y, c = _my_pos()
    me, sibling = (x, y, c), (x, y, 1 - c)
    chips = [(1 - x, y), (x, 1 - y), (1 - x, 1 - y)]

    def rows(i, px, py, pc):
        m_per = x_refs[i].shape[0]
        return out_refs[i].at[pl.ds(pl.multiple_of((4 * px + 2 * py + pc) * m_per, 16), m_per), :]

    def copies(k, block, to, from_shard=False):
        return [pltpu.make_async_remote_copy(
            src_ref=x_refs[i] if from_shard else rows(i, *block), dst_ref=rows(i, *block),
            send_sem=send_sems.at[k * na + i], recv_sem=recv_sems.at[k * na + i], device_id=to, device_id_type=MESH)
            for i in range(na)]

    def mine():
        return [pltpu.make_async_copy(x_refs[i], rows(i, *me), local_sems.at[i]) for i in range(na)]

    def first():
        cps = copies(0, me, sibling, True)
        for j, chip in enumerate(chips):
            cps += copies(1 + j, me, (*chip, c), True)
        return cps

    def start():
        for cp in mine() + first():
            cp.start()

    def forward():
        for j, chip in enumerate(chips):
            for cp in copies(1 + j, (*chip, c), me):
                cp.wait_recv()
            for cp in copies(4 + j, (*chip, c), sibling):
                cp.start()

    def finish():
        for cp in copies(0, sibling, me):
            cp.wait_recv()
        for j, chip in enumerate(chips):
            for cp in copies(4 + j, (*chip, 1 - c), me):
                cp.wait_recv()
        for cp in first():
            cp.wait_send()
        for j, chip in enumerate(chips):
            for cp in copies(4 + j, (*chip, c), sibling):
                cp.wait_send()
        for cp in mine():
            cp.wait()

    return start, forward, finish


def _ag2_scratch(na):
    return [pltpu.SemaphoreType.DMA((7 * na,)), pltpu.SemaphoreType.DMA((7 * na,)), pltpu.SemaphoreType.DMA((na,))]


def _a2a_plan(g_ref, recv_ref, send_sems, recv_sems, local_sem):
    x, y, c = _my_pos()
    me = 4 * x + 2 * y + c
    peers = []
    for k in range(1, NDEV):
        kx, ky, kc = (k >> 2) & 1, (k >> 1) & 1, k & 1
        peers.append((x ^ kx, y ^ ky, c ^ kc))

    def sends():
        return [pltpu.make_async_remote_copy(
            src_ref=g_ref.at[4 * px + 2 * py + pc], dst_ref=recv_ref.at[me], send_sem=send_sems.at[k], recv_sem=recv_sems.at[k],
            device_id=(px, py, pc), device_id_type=MESH) for k, (px, py, pc) in enumerate(peers)]

    def own():
        return pltpu.make_async_copy(g_ref.at[me], recv_ref.at[me], local_sem)

    def start():
        own().start()
        for cp in sends():
            cp.start()

    def finish():
        for k, (px, py, pc) in enumerate(peers):
            pltpu.make_async_remote_copy(
                src_ref=g_ref.at[me], dst_ref=recv_ref.at[4 * px + 2 * py + pc], send_sem=send_sems.at[k],
                recv_sem=recv_sems.at[k], device_id=(x, y, c), device_id_type=MESH).wait_recv()
        for cp in sends():
            cp.wait_send()
        own().wait()

    return start, finish


def _allgather2(shards, name):
    na = len(shards)

    def body(*refs):
        start, forward, finish = _ag2_plan(refs[:na], refs[na:2 * na], *refs[2 * na:])
        start()
        forward()
        finish()

    return pl.pallas_call(
        body, name=name,
        out_shape=[jax.ShapeDtypeStruct((NDEV * s.shape[0], s.shape[1]), s.dtype) for s in shards],
        in_specs=[pl.BlockSpec(memory_space=pltpu.VMEM)] * na,
        out_specs=[pl.BlockSpec(memory_space=pltpu.VMEM)] * na,
        scratch_shapes=_ag2_scratch(na),
        compiler_params=pltpu.CompilerParams(vmem_limit_bytes=VMEM_LIMIT),
    )(*shards)


def _reduce_scatter2(g, name):
    _, r, n = g.shape
    ch = 16
    nch = r // ch

    def body(g_ref, out_ref, a_ref, h_ref, b_ref, s1_send, s1_recv, s2_send, s2_recv):
        x, y, c = _my_pos()
        sibling = (x, y, 1 - c)
        s1 = []
        for j in range(4):
            cp = pltpu.make_async_remote_copy(
                src_ref=g_ref.at[2 * j + (1 - c)], dst_ref=a_ref.at[j], send_sem=s1_send.at[j], recv_sem=s1_recv.at[j],
                device_id=sibling, device_id_type=MESH)
            cp.start()
            s1.append(cp)
        for cp in s1:
            cp.wait_recv()

        def add1(i, _):
            rr = pl.ds(pl.multiple_of(i * ch, ch), ch)
            for j in range(4):
                h_ref[j, rr, :] = (g_ref[2 * j + c, rr, :].astype(f32) + a_ref[j, rr, :].astype(f32)).astype(bf16)
            return 0
        lax.fori_loop(0, nch, add1, 0)
        mychip = 2 * x + y
        s2 = []
        for m in range(1, 4):
            mx, my_ = (m >> 1) & 1, m & 1
            px, py = x ^ mx, y ^ my_
            cp = pltpu.make_async_remote_copy(
                src_ref=h_ref.at[2 * px + py], dst_ref=b_ref.at[m - 1], send_sem=s2_send.at[m - 1], recv_sem=s2_recv.at[m - 1],
                device_id=(px, py, c), device_id_type=MESH)
            cp.start()
            s2.append(cp)
        for cp in s2:
            cp.wait_recv()

        def add2(i, _):
            rr = pl.ds(pl.multiple_of(i * ch, ch), ch)
            acc = h_ref[mychip, rr, :].astype(f32)
            for m in range(3):
                acc = acc + b_ref[m, rr, :].astype(f32)
            out_ref[rr, :] = acc
            return 0
        lax.fori_loop(0, nch, add2, 0)
        for cp in s1 + s2:
            cp.wait_send()

    return pl.pallas_call(
        body, name=name,
        out_shape=jax.ShapeDtypeStruct((r, n), f32),
        in_specs=[pl.BlockSpec(memory_space=pltpu.VMEM)],
        out_specs=pl.BlockSpec(memory_space=pltpu.VMEM),
        scratch_shapes=[pltpu.VMEM((4, r, n), bf16), pltpu.VMEM((4, r, n), bf16), pltpu.VMEM((3, r, n), bf16),
                        pltpu.SemaphoreType.DMA((4,)), pltpu.SemaphoreType.DMA((4,)),
                        pltpu.SemaphoreType.DMA((3,)), pltpu.SemaphoreType.DMA((3,))],
        compiler_params=pltpu.CompilerParams(vmem_limit_bytes=VMEM_LIMIT),
    )(g)


def _mod_fwd(cvec, w_sh, b_sh):
    def body(c_ref, w_ref, b_ref, o_ref):
        cv = c_ref[...]
        act = (cv * _sigmoid(cv)).astype(bf16)
        o_ref[...] = _dot(act, w_ref[...].astype(bf16)) + b_ref[...]
    return pl.pallas_call(body, name="mod_fwd", out_shape=jax.ShapeDtypeStruct((16, w_sh.shape[1]), f32))(cvec, w_sh, b_sh)


def _mod_bwd(cvec, dm_sh, w_sh):
    def body(c_ref, dm_ref, w_ref, gw_ref, gc_ref):
        cv = c_ref[...]
        act = (cv * _sigmoid(cv)).astype(bf16)
        gw_ref[...] = _dg(act, dm_ref[...].astype(bf16), TN)
        gc_ref[...] = _dg(dm_ref[8:16, :].astype(bf16), w_ref[...].astype(bf16), NT)
    return pl.pallas_call(
        body, name="mod_bwd",
        out_shape=(jax.ShapeDtypeStruct(w_sh.shape, f32), jax.ShapeDtypeStruct((8, D), f32)))(cvec, dm_sh, w_sh)


def _sum_rows8(a, name):
    n = a.shape[1]

    def body(a_ref, o_ref):
        acc = a_ref[0]
        for d in range(1, NDEV):
            acc = acc + a_ref[d]
        o_ref[...] = acc
    return pl.pallas_call(body, name=name, out_shape=jax.ShapeDtypeStruct((n, 128), f32))(a)


def _in_proj(x0, ctx0, g1, modv, w_inT):
    tm = 256
    nt = TA // tm
    nx = T // tm

    def body(x_ref, c_ref, g_ref, mod_ref, w_ref, h_ref, q_ref, k_ref, v_ref, a_ref, gg_ref):
        i = pl.program_id(0)
        is_ctx = i == nt - 1
        xv = jnp.where(is_ctx, c_ref[...], x_ref[...])
        rstd = lax.rsqrt(jnp.mean(xv * xv, axis=-1, keepdims=True) + EPS)
        sh = jnp.where(is_ctx, mod_ref[6:7, :], mod_ref[0:1, :])
        sc = jnp.where(is_ctx, mod_ref[7:8, :], mod_ref[1:2, :])
        h = ((xv * rstd * g_ref[...]) * (1.0 + sc) + sh).astype(bf16)
        h_ref[...] = h
        for j, o_ref in enumerate((q_ref, k_ref, v_ref, a_ref, gg_ref)):
            o_ref[...] = _dg(h, w_ref[j * DA:(j + 1) * DA, :], NT).astype(o_ref.dtype)

    row = lambda w: pl.BlockSpec((tm, w), lambda i: (i, 0))
    return pl.pallas_call(
        body, name="in_proj", grid=(nt,),
        in_specs=[pl.BlockSpec((tm, D), lambda i: (jnp.minimum(i, nx - 1), 0)), _full((TC, D)),
                  _full((1, D)), _full((8, D)), _full((5 * DA, D))],
        out_specs=[row(D), row(DA), row(DA), row(DA), row(DA), row(DA)],
        out_shape=[jax.ShapeDtypeStruct((TA, D), bf16)] + [jax.ShapeDtypeStruct((TA, DA), bf16)] * 3
                  + [jax.ShapeDtypeStruct((TA, DA), f32)] * 2,
        compiler_params=_cp("parallel"),
    )(x0, ctx0, g1, modv, w_inT)


def _win_start(r):
    return jnp.clip(r - WR // 2, 0, GW - WR)


def _pattern(r):
    return _win_start(r) - r + (WR - 1)


def _bias_table(rpb):
    p = np.arange(8)[:, None]
    i = np.arange(WR)[None, :]
    ro = (p - (WR - 1)) + i + (WR - 1)
    r_hot = (ro[:, :, None] == np.arange(2 * WR - 1)[None, None, :]).astype(np.float32)
    qc = np.arange(GW)[:, None]
    kc = np.arange(GW)[None, :]
    cs = np.clip(qc - NCOL // 2, 0, GW - NCOL)
    valid = (kc >= cs) & (kc < cs + NCOL)
    co = np.clip(kc - qc + (NCOL - 1), 0, 2 * NCOL - 2)
    c_hot = ((co[:, :, None] == np.arange(2 * NCOL - 1)[None, None, :]) & valid[:, :, None]).astype(np.float32)
    hi = lax.Precision.HIGHEST
    t = jnp.einsum("hrc,pir->hpic", rpb, jnp.asarray(r_hot), precision=hi)
    b = jnp.einsum("hpic,qkc->phqik", t, jnp.asarray(c_hot), precision=hi)
    b = jnp.where(jnp.asarray(valid)[None, None, :, None, :], b, NEG)
    return b.reshape(8, NH, GW, WR * GW)


def _rpb_tables():
    lane_map = np.zeros((WR * GW, WR, 2 * NCOL - 1), np.float32)
    for i in range(WR):
        for t in range(GW):
            if t >= GW - NCOL:
                lane_map[i * GW + t, i, (GW - 1 - t) + NCOL - 1] = 1.0
            elif t < NCOL - 1:
                lane_map[i * GW + t, (i - 1) % WR, NCOL - 2 - t] = 1.0
    p = np.arange(8)[:, None]
    i = np.arange(WR)[None, :]
    r_hot = ((p + i)[:, :, None] == np.arange(2 * WR - 1)[None, None, :]).astype(np.float32)
    return lane_map, r_hot


AG_FORWARD_ROW = 44


def _attn_fwd(q, k, v, bias_tab, shards):
    na = len(shards)

    def body(q_ref, k_ref, v_ref, b_ref, *rest):
        x_refs, (y_ref, lse_ref), out_refs, sems = rest[:na], rest[na:na + 2], rest[na + 2:2 * na + 2], rest[2 * na + 2:]
        r = pl.program_id(0)
        if na:
            start, forward, finish = _ag2_plan(x_refs, out_refs, *sems)
            pl.when(r == 0)(start)
            pl.when(r == AG_FORWARD_ROW)(forward)
        ks = pl.multiple_of(_win_start(r) * GW, GW)
        qq = q_ref[...]
        lo = lax.broadcasted_iota(jnp.int32, (GW, 2 * HD), 1) < HD
        kv, scores = [], []
        for pr in range(NH // 2):
            ps = slice(pr * 2 * HD, (pr + 1) * 2 * HD)
            q2 = qq[:, ps]
            kw, kc = k_ref[pl.ds(ks, WR * GW), ps], k_ref[T:TA, ps]
            kv.append((v_ref[pl.ds(ks, WR * GW), ps], v_ref[T:TA, ps]))
            for s in range(2):
                qm = jnp.where(lo if s == 0 else ~lo, q2, jnp.zeros_like(q2))
                scores.append((_dg(qm, kw, NT) * SCALE + b_ref[0, 2 * pr + s], _dg(qm, kc, NT) * SCALE))
        probs = []
        for h, (sl, sc) in enumerate(scores):
            m = jnp.maximum(jnp.max(sl, axis=-1, keepdims=True), jnp.max(sc, axis=-1, keepdims=True))
            pl_ = jnp.exp(sl - m)
            pc = jnp.exp(sc - m)
            l = jnp.sum(pl_, axis=-1, keepdims=True) + jnp.sum(pc, axis=-1, keepdims=True)
            lse_ref[:, h:h + 1] = m + jnp.log(l)
            probs.append((pl_.astype(bf16), pc.astype(bf16), 1.0 / l))
        for pr in range(NH // 2):
            ps = slice(pr * 2 * HD, (pr + 1) * 2 * HD)
            vw, vc = kv[pr]
            outs = [(_dot(pb, vw) + _dot(cb, vc)) * rl for pb, cb, rl in probs[2 * pr:2 * pr + 2]]
            y_ref[:, ps] = jnp.where(lo, outs[0], outs[1]).astype(bf16)
        if na:
            pl.when(r == GW - 1)(finish)

    hbm = pl.BlockSpec(memory_space=pl.ANY)
    return pl.pallas_call(
        body, name="attn_fwd", grid=(GW,),
        in_specs=[pl.BlockSpec((GW, DA), lambda r: (r, 0)), _full((TA, DA)), _full((TA, DA)),
                  pl.BlockSpec((1, NH, GW, WR * GW), lambda r: (_pattern(r), 0, 0, 0))] + [hbm] * na,
        out_specs=[pl.BlockSpec((GW, DA), lambda r: (r, 0)), pl.BlockSpec((GW, NH), lambda r: (r, 0))] + [hbm] * na,
        out_shape=[jax.ShapeDtypeStruct((T, DA), bf16), jax.ShapeDtypeStruct((T, NH), f32)]
                  + [jax.ShapeDtypeStruct((NDEV * s.shape[0], s.shape[1]), s.dtype) for s in shards],
        scratch_shapes=_ag2_scratch(na) if na else [],
        compiler_params=_cp("arbitrary"),
    )(q, k, v, bias_tab, *shards)


CONV_TT = 256
HALO = 16


def _halo_specs(tt, w, nrows_blocks):
    per = tt // HALO
    prev = pl.BlockSpec((HALO, w), lambda i: (jnp.maximum(i * per - 1, 0), 0))
    cur = pl.BlockSpec((tt, w), lambda i: (i, 0))
    nxt = pl.BlockSpec((HALO, w), lambda i: (jnp.minimum((i + 1) * per, nrows_blocks - 1), 0))
    return [prev, cur, nxt]


def _shifted_copies(rot, wn):
    for b in range(1, 8):
        rot[b, 0:wn - 8, :] = rot[0, pl.ds(b, wn - 8), :]


def _conf_fwd(a, g, conv_w, conv_b, ln_g, ln_b):
    tt = CONV_TT
    nt = T // tt
    sub = 32
    wn = tt + 2 * HALO

    def body(ap, ac, an, gp, gc, gn, w_ref, b_ref, lg_ref, lb_ref, y_ref, cv_ref, rot):
        i = pl.program_id(0)
        rot[0, 0:HALO, :] = jnp.where(i > 0, ap[...] * _sigmoid(gp[...]), 0.0)
        rot[0, HALO:HALO + tt, :] = ac[...] * _sigmoid(gc[...])
        rot[0, HALO + tt:, :] = jnp.where(i < nt - 1, an[...] * _sigmoid(gn[...]), 0.0)
        _shifted_copies(rot, wn)
        w = w_ref[...]
        for s in range(tt // sub):
            acc = jnp.zeros((sub, DA), f32)
            for j in range(CW):
                a8, b8 = divmod(1 + j, 8)
                acc = acc + rot[b8, pl.ds(s * sub + 8 * a8, sub), :] * w[j:j + 1, :]
            cv = acc + b_ref[...]
            cv_ref[pl.ds(s * sub, sub), :] = cv
            mu = jnp.mean(cv, axis=-1, keepdims=True)
            xc = cv - mu
            rstd = lax.rsqrt(jnp.mean(xc * xc, axis=-1, keepdims=True) + EPS)
            z = xc * rstd * lg_ref[...] + lb_ref[...]
            y_ref[pl.ds(s * sub, sub), :] = (z * _sigmoid(z)).astype(bf16)

    hs = _halo_specs(tt, DA, T // HALO)
    return pl.pallas_call(
        body, name="conf_fwd", grid=(nt,),
        in_specs=hs + hs + [_full((CW, DA)), _full((1, DA)), _full((1, DA)), _full((1, DA))],
        out_specs=[pl.BlockSpec((tt, DA), lambda i: (i, 0)), pl.BlockSpec((tt, DA), lambda i: (i, 0))],
        out_shape=[jax.ShapeDtypeStruct((T, DA), bf16), jax.ShapeDtypeStruct((T, DA), f32)],
        scratch_shapes=[pltpu.VMEM((8, wn, DA), f32)],
        compiler_params=_cp("parallel"),
    )(a, a, a, g, g, g, conv_w, conv_b, ln_g, ln_b)


def _out_proj(xa, y_na, y_cv, w_out, modv, g2):
    tm = 256

    def body(x_ref, ya_ref, yc_ref, w_ref, mod_ref, g_ref, x1_ref, pj_ref, h2_ref):
        proj = _dot(ya_ref[...], w_ref[0:DA, :]) + _dot(yc_ref[...], w_ref[DA:D, :])
        x1 = x_ref[...] + mod_ref[2:3, :] * proj
        x1_ref[...] = x1
        pj_ref[...] = proj.astype(bf16)
        rstd = lax.rsqrt(jnp.mean(x1 * x1, axis=-1, keepdims=True) + EPS)
        h2_ref[...] = ((x1 * rstd * g_ref[...]) * (1.0 + mod_ref[4:5, :]) + mod_ref[3:4, :]).astype(bf16)

    row = lambda w: pl.BlockSpec((tm, w), lambda i: (i, 0))
    return pl.pallas_call(
        body, name="out_proj", grid=(T // tm,),
        in_specs=[row(D), row(DA), row(DA), _full((D, D)), _full((8, D)), _full((1, D))],
        out_specs=[row(D), row(D), row(D)],
        out_shape=[jax.ShapeDtypeStruct((T, D), f32), jax.ShapeDtypeStruct((T, D), bf16), jax.ShapeDtypeStruct((T, D), bf16)],
        compiler_params=_cp("parallel"),
    )(xa, y_na, y_cv, w_out, modv, g2)


FFN_TT = 512
FFN_CT = 256
FFN_NC = F // FFN_CT


def _ffn_specs(tt, ct, by_token_first):
    tc = (lambda f: (lambda t, c: f(t, c))) if by_token_first else (lambda f: (lambda c, t: f(t, c)))
    per = tt // HALO
    halo = [pl.BlockSpec((HALO, D), tc(lambda t, c: (jnp.maximum(t * per - 1, 0), 0))),
            pl.BlockSpec((tt, D), tc(lambda t, c: (t, 0))),
            pl.BlockSpec((HALO, D), tc(lambda t, c: (jnp.minimum((t + 1) * per, T // HALO - 1), 0)))]
    weights = [pl.BlockSpec((ct, D), tc(lambda t, c: (c, 0))), pl.BlockSpec((ct, D), tc(lambda t, c: (c + FFN_NC, 0))),
               pl.BlockSpec((3, ct), tc(lambda t, c: (0, c))), pl.BlockSpec((3, ct), tc(lambda t, c: (0, c + FFN_NC))),
               pl.BlockSpec((1, ct), tc(lambda t, c: (0, c))), pl.BlockSpec((1, ct), tc(lambda t, c: (0, c + FFN_NC))),
               pl.BlockSpec((ct, D), tc(lambda t, c: (c, 0)))]
    return halo, weights


def _ffn_fwd(h2, w_upT, fcw, fcb, w_down):
    tt, ct = FFN_TT, FFN_CT
    nt = T // tt
    wn = tt + 2 * HALO
    half = tt // 2

    def body(hp, hc, hn, wg_ref, wv_ref, cwg_ref, cwv_ref, cbg_ref, cbv_ref, wd_ref, o_ref, hwin, uwin):
        t = pl.program_id(0)
        c = pl.program_id(1)

        @pl.when(c == 0)
        def _():
            hwin[0:HALO, :] = jnp.where(t > 0, hp[...], jnp.zeros_like(hp[...]))
            hwin[HALO:HALO + tt, :] = hc[...]
            hwin[HALO + tt:, :] = jnp.where(t < nt - 1, hn[...], jnp.zeros_like(hn[...]))
            o_ref[...] = jnp.zeros_like(o_ref)

        for r0, r1 in ((0, half + 2 * HALO), (half + 2 * HALO, wn)):
            hw = hwin[r0:r1, :]
            uwin[r0:r1, :ct] = _dg(hw, wg_ref[...], NT)
            uwin[r0:r1, ct:] = _dg(hw, wv_ref[...], NT)
        cw = jnp.concatenate([cwg_ref[...], cwv_ref[...]], axis=1)
        cb = jnp.concatenate([cbg_ref[...], cbv_ref[...]], axis=1)
        for p in range(2):
            base = HALO + p * half
            u2 = (uwin[pl.ds(base - 1, half), :] * cw[0:1, :] + uwin[pl.ds(base, half), :] * cw[1:2, :]
                  + uwin[pl.ds(base + 1, half), :] * cw[2:3, :] + cb)
            gate = u2[:, :ct]
            val = u2[:, ct:]
            act = (gate * _sigmoid(gate) * val).astype(bf16)
            o_ref[p * half:(p + 1) * half, :] += _dot(act, wd_ref[...])

    halo, weights = _ffn_specs(tt, ct, True)
    return pl.pallas_call(
        body, name="ffn_fwd", grid=(nt, FFN_NC),
        in_specs=halo + weights,
        out_specs=pl.BlockSpec((tt, D), lambda t, c: (t, 0)),
        out_shape=jax.ShapeDtypeStruct((T, D), f32),
        scratch_shapes=[pltpu.VMEM((wn, D), bf16), pltpu.VMEM((wn, 2 * ct), f32)],
        compiler_params=_cp("parallel", "arbitrary"),
    )(h2, h2, h2, w_upT, w_upT, fcw, fcw, fcb, fcb, w_down)


def _loss_bwd(ffn, x1, tgt, modv, gf):
    tm = 256
    nt = T // tm

    def body(f_ref, x1_ref, t_ref, mod_ref, g_ref, dx2_ref, df_ref, s_ref):
        i = pl.program_id(0)

        @pl.when(i == 0)
        def _():
            s_ref[...] = jnp.zeros_like(s_ref)

        ff = f_ref[...]
        gt2 = mod_ref[5:6, :]
        x2 = x1_ref[...] + gt2 * ff
        rstd = lax.rsqrt(jnp.mean(x2 * x2, axis=-1, keepdims=True) + EPS)
        xh = x2 * rstd
        gfv = g_ref[...]
        e = xh * gfv - t_ref[...]
        dy = e * (1.0 / D)
        dxh = dy * gfv
        dx2 = rstd * (dxh - xh * jnp.mean(dxh * xh, axis=-1, keepdims=True))
        dx2_ref[...] = dx2
        df_ref[...] = (dx2 * gt2).astype(bf16)
        s_ref[0:1, :] += jnp.sum(dy * xh, axis=0, keepdims=True)
        s_ref[1:2, :] += jnp.sum(dx2 * ff, axis=0, keepdims=True)
        s_ref[2:3, :] += jnp.sum(e * e, axis=0, keepdims=True)

        @pl.when(i == nt - 1)
        def _():
            tot = jnp.sum(s_ref[2:3, :], axis=-1, keepdims=True) * (0.5 / D)
            s_ref[3:4, :] = jnp.broadcast_to(tot, (1, D))

    row = lambda: pl.BlockSpec((tm, D), lambda i: (i, 0))
    return pl.pallas_call(
        body, name="loss_bwd", grid=(nt,),
        in_specs=[row(), row(), row(), _full((8, D)), _full((1, D))],
        out_specs=[row(), row(), _full((8, D))],
        out_shape=[jax.ShapeDtypeStruct((T, D), f32), jax.ShapeDtypeStruct((T, D), bf16), jax.ShapeDtypeStruct((8, D), f32)],
        compiler_params=_cp("arbitrary"),
    )(ffn, x1, tgt, modv, gf)


def _ffn_bwd(h2, dffn, w_upT, fcw, fcb, w_down):
    tt, ct = FFN_TT, FFN_CT
    nt = T // tt
    wn = tt + 2 * HALO
    half = tt // 2

    def body(hp, hc, hn, dp, dc, dn, wg_ref, wv_ref, cwg_ref, cwv_ref, cbg_ref, cbv_ref, wd_ref,
             dug_ref, duv_ref, dwg_ref, dwv_ref, dwd_ref, dcwg_ref, dcwv_ref, dcbg_ref, dcbv_ref,
             hwin, dwin, uwin, d2win, dawin, actwin):
        t = pl.program_id(1)
        first, last = t == 0, t == nt - 1
        zero = jnp.zeros((HALO, D), bf16)
        hwin[0:HALO, :] = jnp.where(first, zero, hp[...])
        hwin[HALO:HALO + tt, :] = hc[...]
        hwin[HALO + tt:, :] = jnp.where(last, zero, hn[...])
        dwin[0:HALO, :] = jnp.where(first, zero, dp[...])
        dwin[HALO:HALO + tt, :] = dc[...]
        dwin[HALO + tt:, :] = jnp.where(last, zero, dn[...])

        @pl.when(first)
        def _():
            for r in (dwg_ref, dwv_ref, dwd_ref, dcwg_ref, dcwv_ref, dcbg_ref, dcbv_ref):
                r[...] = jnp.zeros_like(r)

        cw = jnp.concatenate([cwg_ref[...], cwv_ref[...]], axis=1)
        cb = jnp.concatenate([cbg_ref[...], cbv_ref[...]], axis=1)
        split = half + 2 * HALO
        for p in range(2):
            r0, r1 = (0, split) if p == 0 else (split, wn)
            hw = hwin[r0:r1, :]
            uwin[r0:r1, :ct] = _dg(hw, wg_ref[...], NT)
            uwin[r0:r1, ct:] = _dg(hw, wv_ref[...], NT)
            dawin[r0:r1, :] = _dg(dwin[r0:r1, :], wd_ref[...], NT)
            e0, e1 = (1, split - 15) if p == 0 else (split - 15, wn - 1)
            m = e1 - e0
            u2 = (uwin[pl.ds(e0 - 1, m), :] * cw[0:1, :] + uwin[pl.ds(e0, m), :] * cw[1:2, :]
                  + uwin[pl.ds(e0 + 1, m), :] * cw[2:3, :] + cb)
            gate, val = u2[:, :ct], u2[:, ct:]
            sg = _sigmoid(gate)
            silu = gate * sg
            dact = dawin[pl.ds(e0, m), :]
            d2win[pl.ds(e0, m), :ct] = dact * val * (sg * (1.0 + gate * (1.0 - sg)))
            d2win[pl.ds(e0, m), ct:] = dact * silu
            actwin[pl.ds(e0, m), :] = silu * val
            b0 = HALO + p * half
            rows = slice(p * half, (p + 1) * half)
            act = actwin[pl.ds(b0, half), :].astype(bf16)
            dwd_ref[...] += _dg(act, dc[rows, :], TN)
            d2c = d2win[pl.ds(b0, half), :]
            dcb = jnp.sum(d2c, axis=0, keepdims=True)
            dcbg_ref[...] += dcb[:, :ct]
            dcbv_ref[...] += dcb[:, ct:]
            for kk in range(3):
                dck = jnp.sum(uwin[pl.ds(b0 - 1 + kk, half), :] * d2c, axis=0, keepdims=True)
                dcwg_ref[kk:kk + 1, :] += dck[:, :ct]
                dcwv_ref[kk:kk + 1, :] += dck[:, ct:]
            du = (d2win[pl.ds(b0 + 1, half), :] * cw[0:1, :] + d2c * cw[1:2, :]
                  + d2win[pl.ds(b0 - 1, half), :] * cw[2:3, :]).astype(bf16)
            dug_ref[rows, :] = du[:, :ct]
            duv_ref[rows, :] = du[:, ct:]
            dwu = _dg(du, hc[rows, :], TN)
            dwg_ref[...] += dwu[:ct, :]
            dwv_ref[...] += dwu[ct:, :]

    halo, weights = _ffn_specs(tt, ct, False)
    tile = lambda: pl.BlockSpec((ct, D), lambda c, t: (c, 0))
    lane = lambda r: pl.BlockSpec((r, ct), lambda c, t: (0, c))
    return pl.pallas_call(
        body, name="ffn_bwd", grid=(FFN_NC, nt),
        in_specs=halo + halo + weights,
        out_specs=[pl.BlockSpec((tt, ct), lambda c, t: (t, c)), pl.BlockSpec((tt, ct), lambda c, t: (t, c)),
                   tile(), tile(), tile(), lane(3), lane(3), lane(1), lane(1)],
        out_shape=[jax.ShapeDtypeStruct((T, F), bf16), jax.ShapeDtypeStruct((T, F), bf16),
                   jax.ShapeDtypeStruct((F, D), f32), jax.ShapeDtypeStruct((F, D), f32), jax.ShapeDtypeStruct((F, D), f32),
                   jax.ShapeDtypeStruct((3, F), f32), jax.ShapeDtypeStruct((3, F), f32),
                   jax.ShapeDtypeStruct((1, F), f32), jax.ShapeDtypeStruct((1, F), f32)],
        scratch_shapes=[pltpu.VMEM((wn, D), bf16), pltpu.VMEM((wn, D), bf16),
                        pltpu.VMEM((wn, 2 * ct), f32), pltpu.VMEM((wn, 2 * ct), f32),
                        pltpu.VMEM((wn, ct), f32), pltpu.VMEM((wn, ct), f32)],
        compiler_params=_cp("parallel", "arbitrary"),
    )(h2, h2, h2, dffn, dffn, dffn, w_upT, w_upT, fcw, fcw, fcb, fcb, w_down)


def _norm_bwd(dh, xv, gain, sh_sc, rstd):
    xh = xv * rstd
    n = xh * gain
    dn = dh * (1.0 + sh_sc)
    dxh = dn * gain
    dx = rstd * (dxh - xh * jnp.mean(dxh * xh, axis=-1, keepdims=True))
    return (dx, jnp.sum(dh, axis=0, keepdims=True), jnp.sum(dh * n, axis=0, keepdims=True),
            jnp.sum(dn * xh, axis=0, keepdims=True))


def _norm2_bwd(dug, duv, w_upT, x1, dx2, proj, w_out, y_na, y_cv, modv, g2):
    tm = 256
    nt = T // tm

    def body(dug_ref, duv_ref, w_ref, x1_ref, dx2_ref, pj_ref, wo_ref, ya_ref, yc_ref, mod_ref, g_ref,
             dx1_ref, dya_ref, dyc_ref, dwo_ref, s_ref):
        i = pl.program_id(0)

        @pl.when(i == 0)
        def _():
            s_ref[...] = jnp.zeros_like(s_ref)
            dwo_ref[...] = jnp.zeros_like(dwo_ref)

        dh2 = _dot(dug_ref[...], w_ref[0:F, :]) + _dot(duv_ref[...], w_ref[F:F2, :])
        x1 = x1_ref[...]
        rstd = lax.rsqrt(jnp.mean(x1 * x1, axis=-1, keepdims=True) + EPS)
        dxn, dsh, dsc, dgn = _norm_bwd(dh2, x1, g_ref[...], mod_ref[4:5, :], rstd)
        dx1 = dx2_ref[...] + dxn
        dx1_ref[...] = dx1
        dpj = (dx1 * mod_ref[2:3, :]).astype(bf16)
        dyc = _dg(dpj, wo_ref[...], NT)
        dya_ref[...] = dyc[:, :DA].astype(bf16)
        dyc_ref[...] = dyc[:, DA:]
        dwo_ref[0:DA, :] += _dg(ya_ref[...], dpj, TN)
        dwo_ref[DA:D, :] += _dg(yc_ref[...], dpj, TN)
        s_ref[0:1, :] += dsh
        s_ref[1:2, :] += dsc
        s_ref[2:3, :] += dgn
        s_ref[3:4, :] += jnp.sum(dx1 * pj_ref[...].astype(f32), axis=0, keepdims=True)

    row = lambda w: pl.BlockSpec((tm, w), lambda i: (i, 0))
    return pl.pallas_call(
        body, name="norm2_bwd", grid=(nt,),
        in_specs=[row(F), row(F), _full((F2, D)), row(D), row(D), row(D), _full((D, D)), row(DA), row(DA),
                  _full((8, D)), _full((1, D))],
        out_specs=[row(D), row(DA), row(DA), _full((D, D)), _full((8, D))],
        out_shape=[jax.ShapeDtypeStruct((T, D), f32), jax.ShapeDtypeStruct((T, DA), bf16), jax.ShapeDtypeStruct((T, DA), f32),
                   jax.ShapeDtypeStruct((D, D), f32), jax.ShapeDtypeStruct((8, D), f32)],
        compiler_params=_cp("arbitrary"),
    )(dug, duv, w_upT, x1, dx2, proj, w_out, y_na, y_cv, modv, g2)


def _conf_bwd(a, g, cv, dy, conv_w, ln_g, ln_b):
    tt = CONV_TT
    nt = T // tt
    sub = 32
    wn = tt + 2 * HALO

    def body(ap, ac, an, gp, gc, gn, cp_, cc, cn, dp, dc, dn, w_ref, lg_ref, lb_ref,
             da_ref, dg_ref, dcw_ref, s_ref, urot, drot, wacc):
        i = pl.program_id(0)
        first, last = i == 0, i == nt - 1

        @pl.when(first)
        def _():
            s_ref[...] = jnp.zeros_like(s_ref)
            wacc[...] = jnp.zeros_like(wacc)

        lg, lb = lg_ref[...], lb_ref[...]

        def ln_bwd(cvv, dyv):
            mu = jnp.mean(cvv, axis=-1, keepdims=True)
            xc = cvv - mu
            rstd = lax.rsqrt(jnp.mean(xc * xc, axis=-1, keepdims=True) + EPS)
            yn = xc * rstd
            z = yn * lg + lb
            sz = _sigmoid(z)
            dz = dyv * (sz * (1.0 + z * (1.0 - sz)))
            dyn = dz * lg
            dcv = rstd * (dyn - jnp.mean(dyn, axis=-1, keepdims=True) - yn * jnp.mean(dyn * yn, axis=-1, keepdims=True))
            return dcv, dz, yn

        urot[0, 0:HALO, :] = jnp.where(first, 0.0, ap[...] * _sigmoid(gp[...]))
        urot[0, HALO + tt:, :] = jnp.where(last, 0.0, an[...] * _sigmoid(gn[...]))
        drot[0, 0:HALO, :] = jnp.where(first, 0.0, ln_bwd(cp_[...], dp[...])[0])
        drot[0, HALO + tt:, :] = jnp.where(last, 0.0, ln_bwd(cn[...], dn[...])[0])
        for s in range(tt // sub):
            rr = pl.ds(s * sub, sub)
            urot[0, pl.ds(HALO + s * sub, sub), :] = ac[rr, :] * _sigmoid(gc[rr, :])
            dcv, dz, yn = ln_bwd(cc[rr, :], dc[rr, :])
            drot[0, pl.ds(HALO + s * sub, sub), :] = dcv
            s_ref[0:1, :] += jnp.sum(dcv, axis=0, keepdims=True)
            s_ref[1:2, :] += jnp.sum(dz * yn, axis=0, keepdims=True)
            s_ref[2:3, :] += jnp.sum(dz, axis=0, keepdims=True)
        _shifted_copies(urot, wn)
        _shifted_copies(drot, wn)
        w = w_ref[...]
        for s in range(tt // sub):
            rr = pl.ds(s * sub, sub)
            dcv = drot[0, pl.ds(HALO + s * sub, sub), :]
            acc = jnp.zeros((sub, DA), f32)
            for j in range(CW):
                ad, bd = divmod(2 * HALO - 1 - j, 8)
                au, bu = divmod(1 + j, 8)
                acc = acc + drot[bd, pl.ds(s * sub + 8 * ad, sub), :] * w[j:j + 1, :]
                part = urot[bu, pl.ds(s * sub + 8 * au, sub), :] * dcv
                wacc[j] += part[0:8] + part[8:16] + part[16:24] + part[24:32]
            av, gv = ac[rr, :], gc[rr, :]
            sg = _sigmoid(gv)
            da_ref[rr, :] = (acc * sg).astype(bf16)
            dg_ref[rr, :] = (acc * av * sg * (1.0 - sg)).astype(bf16)

        @pl.when(last)
        def _():
            for j in range(CW):
                dcw_ref[j:j + 1, :] = jnp.sum(wacc[j], axis=0, keepdims=True)
            dcw_ref[CW:CW + 1, :] = jnp.zeros((1, DA), f32)

    hs = _halo_specs(tt, DA, T // HALO)
    return pl.pallas_call(
        body, name="conf_bwd", grid=(nt,),
        in_specs=hs * 4 + [_full((CW, DA)), _full((1, DA)), _full((1, DA))],
        out_specs=[pl.BlockSpec((tt, DA), lambda i: (i, 0)), pl.BlockSpec((tt, DA), lambda i: (i, 0)),
                   _full((CW + 1, DA)), _full((8, DA))],
        out_shape=[jax.ShapeDtypeStruct((T, DA), bf16), jax.ShapeDtypeStruct((T, DA), bf16),
                   jax.ShapeDtypeStruct((CW + 1, DA), f32), jax.ShapeDtypeStruct((8, DA), f32)],
        scratch_shapes=[pltpu.VMEM((8, wn, DA), f32), pltpu.VMEM((8, wn, DA), f32), pltpu.VMEM((CW, 8, DA), f32)],
        compiler_params=_cp("arbitrary"),
    )(a, a, a, g, g, g, cv, cv, cv, dy, dy, dy, conv_w, ln_g, ln_b)


def _attn_bwd(q, k, v, y, dy, lse, bias_tab, blocks):
    zr = 256
    nb = len(blocks)

    def body(q_ref, k_ref, v_ref, y_ref, dy_ref, lse_ref, b_ref, *rest):
        g_refs, (dq_ref, dk_hbm, dv_hbm, db_ref) = rest[:nb], rest[nb:nb + 4]
        recv_refs, (dk_s, dv_s, sem), a2a_sems = rest[nb + 4:2 * nb + 4], rest[2 * nb + 4:2 * nb + 7], rest[2 * nb + 7:]
        r = pl.program_id(0)
        plans = [_a2a_plan(g_refs[i], recv_refs[i], *a2a_sems[3 * i:3 * i + 3]) for i in range(nb)]
        for start, _ in plans:
            pl.when(r == 0)(start)

        @pl.when(r == 0)
        def _():
            def z(i, _):
                rr = pl.ds(pl.multiple_of(i * zr, zr), zr)
                dk_s[rr, :] = jnp.zeros((zr, DA), f32)
                dv_s[rr, :] = jnp.zeros((zr, DA), f32)
                return 0
            lax.fori_loop(0, TA // zr, z, 0)

        @pl.when((r <= WR // 2) | (r > GW - WR // 2))
        def _():
            db_ref[...] = jnp.zeros_like(db_ref)

        ks = pl.multiple_of(_win_start(r) * GW, GW)
        win = pl.ds(ks, WR * GW)
        qq, yy, dyy, lse_v = q_ref[...], y_ref[...], dy_ref[...], lse_ref[...]
        lo = lax.broadcasted_iota(jnp.int32, (GW, 2 * HD), 1) < HD
        for pr in range(NH // 2):
            ps = slice(pr * 2 * HD, (pr + 1) * 2 * HD)
            q2, do2 = qq[:, ps], dyy[:, ps]
            prod = do2.astype(f32) * yy[:, ps].astype(f32)
            kw, vw = k_ref[win, ps], v_ref[win, ps]
            kc, vc = k_ref[T:TA, ps], v_ref[T:TA, ps]
            dqs = []
            dkw = dvw = dkc = dvc = None
            for s in range(2):
                h = 2 * pr + s
                msk = lo if s == 0 else ~lo
                qm = jnp.where(msk, q2, jnp.zeros_like(q2))
                dom = jnp.where(msk, do2, jnp.zeros_like(do2))
                delta = jnp.sum(jnp.where(msk, prod, 0.0), axis=-1, keepdims=True)
                lh = lse_v[:, h:h + 1]
                pl_ = jnp.exp(_dg(qm, kw, NT) * SCALE + b_ref[0, h] - lh)
                pc = jnp.exp(_dg(qm, kc, NT) * SCALE - lh)
                dsl = pl_ * (_dg(dom, vw, NT) - delta)
                dsc = pc * (_dg(dom, vc, NT) - delta)
                db_ref[0, h] += dsl
                dslb, dscb = dsl.astype(bf16), dsc.astype(bf16)
                dqs.append(_dot(dslb, kw) + _dot(dscb, kc))
                parts = (_dg(dslb, qm, TN), _dg(pl_.astype(bf16), dom, TN), _dg(dscb, qm, TN), _dg(pc.astype(bf16), dom, TN))
                if s == 0:
                    dkw, dvw, dkc, dvc = parts
                else:
                    dkw, dvw, dkc, dvc = dkw + parts[0], dvw + parts[1], dkc + parts[2], dvc + parts[3]
            dq_ref[:, ps] = (jnp.where(lo, dqs[0], dqs[1]) * SCALE).astype(bf16)
            dk_s[win, ps] += dkw * SCALE
            dv_s[win, ps] += dvw
            dk_s[T:TA, ps] += dkc * SCALE
            dv_s[T:TA, ps] += dvc

        @pl.when(r == GW - 1)
        def _():
            c1 = pltpu.make_async_copy(dk_s, dk_hbm, sem.at[0])
            c2 = pltpu.make_async_copy(dv_s, dv_hbm, sem.at[1])
            c1.start()
            c2.start()
            c1.wait()
            c2.wait()

        for _, finish in plans:
            pl.when(r == GW - 1)(finish)

    rowq = lambda: pl.BlockSpec((GW, DA), lambda r: (r, 0))
    hbm = pl.BlockSpec(memory_space=pl.ANY)
    a2a_scratch = [pltpu.SemaphoreType.DMA((NDEV - 1,)), pltpu.SemaphoreType.DMA((NDEV - 1,)), pltpu.SemaphoreType.DMA]
    return pl.pallas_call(
        body, name="attn_bwd", grid=(GW,),
        in_specs=[rowq(), _full((TA, DA)), _full((TA, DA)), rowq(), rowq(), pl.BlockSpec((GW, NH), lambda r: (r, 0)),
                  pl.BlockSpec((1, NH, GW, WR * GW), lambda r: (_pattern(r), 0, 0, 0))] + [hbm] * nb,
        out_specs=[rowq(), hbm, hbm, pl.BlockSpec((1, NH, GW, WR * GW), lambda r: (_pattern(r), 0, 0, 0))] + [hbm] * nb,
        out_shape=[jax.ShapeDtypeStruct((T, DA), bf16), jax.ShapeDtypeStruct((TA, DA), f32), jax.ShapeDtypeStruct((TA, DA), f32),
                   jax.ShapeDtypeStruct((8, NH, GW, WR * GW), f32)] + [jax.ShapeDtypeStruct(b.shape, b.dtype) for b in blocks],
        scratch_shapes=[pltpu.VMEM((TA, DA), f32), pltpu.VMEM((TA, DA), f32), pltpu.SemaphoreType.DMA((2,))] + a2a_scratch * nb,
        compiler_params=_cp("arbitrary"),
    )(q, k, v, y, dy, lse, bias_tab, *blocks)


def _sum_blocks(recv, name):
    _, r, n = recv.shape
    tr = r // 5 if r % 80 == 0 else r

    def body(a_ref, o_ref):
        acc = a_ref[0].astype(f32)
        for d in range(1, NDEV):
            acc = acc + a_ref[d].astype(f32)
        o_ref[...] = acc

    return pl.pallas_call(
        body, name=name, grid=(r // tr,),
        in_specs=[pl.BlockSpec((NDEV, tr, n), lambda i: (0, i, 0))],
        out_specs=pl.BlockSpec((tr, n), lambda i: (i, 0)),
        out_shape=jax.ShapeDtypeStruct((r, n), f32),
        compiler_params=_cp("parallel"),
    )(recv)


def _rpb_reduce(dbias):
    rev = np.zeros((WR * GW, WR * GW), np.float32)
    for i in range(WR):
        for kk in range(GW):
            rev[i * GW + kk, i * GW + GW - 1 - kk] = 1.0

    def body(d_ref, rev_ref, o_ref):
        rv = rev_ref[...]
        for h in range(NH):
            dv = d_ref[0, h]
            r0 = dv.astype(bf16)
            e1 = dv - r0.astype(f32)
            r1 = e1.astype(bf16)
            r2 = (e1 - r1.astype(f32)).astype(bf16)
            rr = _dot(r0, rv) + _dot(r1, rv) + _dot(r2, rv)
            skew = pltpu.roll(rr, 0, 1, stride=1, stride_axis=0)
            o_ref[0, h:h + 1, :] = jnp.sum(skew, axis=0, keepdims=True)

    return pl.pallas_call(
        body, name="rpb_reduce", grid=(8,),
        in_specs=[pl.BlockSpec((1, NH, GW, WR * GW), lambda p: (p, 0, 0, 0)), _full((WR * GW, WR * GW))],
        out_specs=pl.BlockSpec((1, NH, WR * GW), lambda p: (p, 0, 0)),
        out_shape=jax.ShapeDtypeStruct((8, NH, WR * GW), f32),
        compiler_params=_cp("parallel"),
    )(dbias, jnp.asarray(rev, dtype=bf16))


def _norm1_bwd(dq, dk, dv, da, dg, w_inT, x0, ctx0, h, dx1, modv, g1):
    tm = 256
    nt = TA // tm
    nx = T // tm

    def body(dq_ref, dk_ref, dv_ref, da_ref, dg_ref, w_ref, x_ref, c_ref, h_ref, dx1_ref, mod_ref, g_ref,
             dx_ref, dw_ref, s_ref):
        i = pl.program_id(0)
        is_ctx = i == nt - 1

        @pl.when(i == 0)
        def _():
            s_ref[...] = jnp.zeros_like(s_ref)
            dw_ref[...] = jnp.zeros_like(dw_ref)

        hb = h_ref[...]
        dkb, dvb = dk_ref[...].astype(bf16), dv_ref[...].astype(bf16)
        dw_ref[DA:2 * DA, :] += _dg(dkb, hb, TN)
        dw_ref[2 * DA:3 * DA, :] += _dg(dvb, hb, TN)
        dh_kv = _dot(dkb, w_ref[DA:2 * DA, :]) + _dot(dvb, w_ref[2 * DA:3 * DA, :])
        gain = g_ref[...]

        @pl.when(is_ctx)
        def _():
            xv = c_ref[...]
            rstd = lax.rsqrt(jnp.mean(xv * xv, axis=-1, keepdims=True) + EPS)
            _, dsh, dsc, dgn = _norm_bwd(dh_kv, xv, gain, mod_ref[7:8, :], rstd)
            s_ref[2:3, :] += dgn
            s_ref[3:4, :] += dsh
            s_ref[4:5, :] += dsc

        @pl.when(jnp.logical_not(is_ctx))
        def _():
            dqb, dab, dgb = dq_ref[...], da_ref[...], dg_ref[...]
            dw_ref[0:DA, :] += _dg(dqb, hb, TN)
            dw_ref[3 * DA:4 * DA, :] += _dg(dab, hb, TN)
            dw_ref[4 * DA:5 * DA, :] += _dg(dgb, hb, TN)
            dh = (dh_kv + _dot(dqb, w_ref[0:DA, :]) + _dot(dab, w_ref[3 * DA:4 * DA, :])
                  + _dot(dgb, w_ref[4 * DA:5 * DA, :]))
            xv = x_ref[...]
            rstd = lax.rsqrt(jnp.mean(xv * xv, axis=-1, keepdims=True) + EPS)
            dxn, dsh, dsc, dgn = _norm_bwd(dh, xv, gain, mod_ref[1:2, :], rstd)
            dx_ref[...] = dx1_ref[...] + dxn
            s_ref[0:1, :] += dsh
            s_ref[1:2, :] += dsc
            s_ref[2:3, :] += dgn

    row = lambda w: pl.BlockSpec((tm, w), lambda i: (i, 0))
    lrow = lambda w: pl.BlockSpec((tm, w), lambda i: (jnp.minimum(i, nx - 1), 0))
    return pl.pallas_call(
        body, name="norm1_bwd", grid=(nt,),
        in_specs=[lrow(DA), row(DA), row(DA), lrow(DA), lrow(DA), _full((5 * DA, D)), lrow(D), _full((TC, D)), row(D),
                  lrow(D), _full((8, D)), _full((1, D))],
        out_specs=[lrow(D), _full((5 * DA, D)), _full((8, D))],
        out_shape=[jax.ShapeDtypeStruct((T, D), f32), jax.ShapeDtypeStruct((5 * DA, D), f32), jax.ShapeDtypeStruct((8, D), f32)],
        compiler_params=_cp("arbitrary"),
    )(dq, dk, dv, da, dg, w_inT, x0, ctx0, h, dx1, modv, g1)


def _adamw(w, g, m, v, name):
    r, c = w.shape
    tr = r
    for cand in (256, 128, 64, 32, 16, 8):
        if r % cand == 0 and r > cand:
            tr = cand
            break

    def body(w_ref, g_ref, m_ref, v_ref, d_ref, nm_ref, nv_ref):
        gv = g_ref[...]
        nm = ADAM_B1 * m_ref[...] + (1.0 - ADAM_B1) * gv
        nv = ADAM_B2 * v_ref[...] + (1.0 - ADAM_B2) * (gv * gv)
        m_hat = nm / (1.0 - ADAM_B1 ** ADAM_STEP)
        v_hat = nv / (1.0 - ADAM_B2 ** ADAM_STEP)
        d_ref[...] = -ADAM_LR * (m_hat / (jnp.sqrt(v_hat) + ADAM_EPS) + ADAM_WD * w_ref[...])
        nm_ref[...] = nm
        nv_ref[...] = nv

    spec = pl.BlockSpec((tr, c), lambda i: (i, 0))
    return pl.pallas_call(
        body, name=name, grid=(r // tr,),
        in_specs=[spec] * 4, out_specs=[spec] * 3,
        out_shape=[jax.ShapeDtypeStruct((r, c), f32)] * 3,
        compiler_params=_cp("parallel"),
    )(w, g, m, v)


def _pad_rows128(vec):
    n = vec.shape[0]
    rows = -(-n // 1024) * 8
    return jnp.pad(vec, (0, rows * 128 - n)).reshape(rows, 128)


def _grad_rpb(dbias):
    lane_map, r_hot = _rpb_tables()
    return jnp.einsum("phl,lic,pir->hrc", _rpb_reduce(dbias), jnp.asarray(lane_map), jnp.asarray(r_hot),
                      precision=lax.Precision.HIGHEST)


def kernel(x, c, ctx, c_ctx, w_mod, b_mod, g_norm1, w_in, rpb, conv_w, conv_b, ln_g, ln_b, w_out, g_norm2, w_up, ffn_conv_w, ffn_conv_b, w_down, g_final, loss_target, m_c_ctx, m_w_mod, m_b_mod, m_g_norm1, m_w_in, m_rpb, m_conv_w, m_conv_b, m_ln_g, m_ln_b, m_w_out, m_g_norm2, m_w_up, m_ffn_conv_w, m_ffn_conv_b, m_w_down, m_g_final, v_c_ctx, v_w_mod, v_b_mod, v_g_norm1, v_w_in, v_rpb, v_conv_w, v_conv_b, v_ln_g, v_ln_b, v_w_out, v_g_norm2, v_w_up, v_ffn_conv_w, v_ffn_conv_b, v_w_down, v_g_final):
    me = 4 * lax.axis_index("x") + 2 * lax.axis_index("y") + lax.axis_index("c")
    nmod = w_mod.shape[2]
    n_in = w_in.shape[2]
    n_out = w_out.shape[1]
    n_up = w_up.shape[2]
    n_dn = w_down.shape[1]
    n_cw = conv_w.shape[2]

    (w_inT,) = _allgather2([w_in[0].T.astype(bf16)], "ag_w_in")
    late_shards = [w_up[0].T.astype(bf16), w_down[0].astype(bf16), w_out[0].astype(bf16)]

    c_all = _small_allgather(c.reshape(8, 128), "ag_c").reshape(NDEV, D)
    cvec = jnp.concatenate([c_all, c_ctx[None, :], jnp.zeros((7, D), f32)], axis=0)
    b_sh = lax.dynamic_slice(b_mod, (0, me * nmod), (1, nmod))
    mod_sh = _mod_fwd(cvec, w_mod[0], b_sh)
    n_modp = 16 * nmod
    payload = jnp.concatenate([mod_sh.reshape(-1), conv_w[0].reshape(-1), ffn_conv_w[0].reshape(-1)])
    flat = _small_allgather(_pad_rows128(payload), "ag_mod").reshape(NDEV, -1)
    mod_all = flat[:, :n_modp].reshape(NDEV, 16, nmod).transpose(1, 0, 2).reshape(16, 6 * D)
    mod_me = lax.dynamic_index_in_dim(mod_all, me, 0, keepdims=False).reshape(6, D)
    mod_c = mod_all[8]
    modv = jnp.concatenate([mod_me, mod_c[None, 0:D], mod_c[None, D:2 * D]], axis=0)
    o1 = n_modp + CW * n_cw
    conv_w_f = flat[:, n_modp:o1].reshape(NDEV, CW, n_cw).transpose(1, 0, 2).reshape(CW, DA)
    fcw_f = flat[:, o1:o1 + 3 * n_up].reshape(NDEV, 3, n_up).transpose(1, 0, 2).reshape(3, F2)

    x0, ctx0 = x[0], ctx[0]
    h, q, k, v, a, g = _in_proj(x0, ctx0, g_norm1, modv, w_inT)
    bias_tab = _bias_table(rpb[0])
    y_na, lse, w_upT, w_down_f, w_out_f = _attn_fwd(q, k, v, bias_tab, late_shards)
    y_cv, cv = _conf_fwd(a, g, conv_w_f, conv_b, ln_g, ln_b)
    x1, proj, h2 = _out_proj(x0, y_na, y_cv, w_out_f, modv, g_norm2)
    ffn = _ffn_fwd(h2, w_upT, fcw_f, ffn_conv_b, w_down_f)
    dx2, dffn, s_loss = _loss_bwd(ffn, x1, loss_target[0], modv, g_final[None, :])

    dug, duv, dwg, dwv, dw_down, dcwg, dcwv, dcbg, dcbv = _ffn_bwd(h2, dffn, w_upT, fcw_f, ffn_conv_b, w_down_f)
    dx1, dy_na, dy_cv, dw_out, s_n2 = _norm2_bwd(dug, duv, w_upT, x1, dx2, proj, w_out_f, y_na, y_cv, modv, g_norm2)
    da, dg, dcw, s_cf = _conf_bwd(a, g, cv, dy_cv, conv_w_f, ln_g, ln_b)
    dfcw = jnp.concatenate([dcwg, dcwv], axis=1)
    dfcb = jnp.concatenate([dcbg[0], dcbv[0]])
    small = jnp.concatenate([dcw[:CW].reshape(CW, NDEV, n_cw).transpose(1, 0, 2).reshape(NDEV, CW * n_cw),
                             dfcw.reshape(3, NDEV, n_up).transpose(1, 0, 2).reshape(NDEV, 3 * n_up)], axis=1)
    small = jnp.pad(small.reshape(NDEV, 4, D), ((0, 0), (0, 12), (0, 0)))
    g_late = jnp.concatenate([dwg.reshape(NDEV // 2, n_up, D), dwv.reshape(NDEV // 2, n_up, D)], axis=0)
    g_late = jnp.concatenate([g_late, dw_down.reshape(NDEV, n_dn, D), small, dw_out.reshape(NDEV, n_out, D)],
                             axis=1).astype(bf16)
    dq, dk, dv, dbias, recv = _attn_bwd(q, k, v, y_na, dy_na, lse, bias_tab, [g_late])
    r_f = _sum_blocks(recv, "sum_late")
    grad_rpb_part = _grad_rpb(dbias)
    grad_x, dw_inT, s_n1 = _norm1_bwd(dq, dk, dv, da, dg, w_inT, x0, ctx0, h, dx1, modv, g_norm1)
    grad_x = grad_x[None]
    r_a = _reduce_scatter2(dw_inT.reshape(NDEV, n_in, D).astype(bf16), "rs_w_in")
    g_w_in = r_a.T
    g_w_up = r_f[:n_up].T
    g_w_down = r_f[n_up:n_up + n_dn]
    sm = r_f[n_up + n_dn:n_up + n_dn + 4].reshape(-1)
    g_conv_w = sm[:CW * n_cw].reshape(CW, n_cw)
    g_fcw = sm[CW * n_cw:].reshape(3, n_up)
    g_w_out = r_f[n_up + n_dn + 16:]

    dmod = jnp.concatenate([s_n1[0], s_n1[1], s_n2[3], s_n2[0], s_n2[1], s_loss[1]])
    dmodc = jnp.concatenate([s_n1[3], s_n1[4]])
    parts = [dmodc, s_n1[2], grad_rpb_part.reshape(-1), s_cf[0], s_cf[1], s_cf[2], s_n2[2], dfcb, s_loss[0], s_loss[3, 0:1]]
    sizes = [p.shape[0] for p in parts]
    pvec = _pad_rows128(jnp.concatenate([dmod] + parts))
    gath = _small_allgather(pvec, "ag_small")
    tot = _sum_rows8(gath, "sum_small").reshape(-1)
    dmod_all = gath.reshape(NDEV, -1)[:, :6 * D]
    offs = np.cumsum([6 * D] + sizes)
    pick = lambda j: tot[offs[j]:offs[j + 1]]
    dmodc_t = jnp.pad(pick(0), (0, 4 * D))
    g_b_mod = (tot[:6 * D] + dmodc_t)[None, :]
    g_g_norm1 = pick(1)[None, :]
    g_rpb = pick(2).reshape(1, NH, 2 * WR - 1, 2 * NCOL - 1)
    g_conv_b, g_ln_g, g_ln_b = pick(3)[None, :], pick(4)[None, :], pick(5)[None, :]
    g_g_norm2 = pick(6)[None, :]
    g_fcb = pick(7)[None, :]
    g_g_final = pick(8)
    loss = pick(9)[0]
    dm_rows = jnp.concatenate([dmod_all, dmodc_t[None, :], jnp.zeros((7, 6 * D), f32)], axis=0)
    dm_sh = lax.dynamic_slice(dm_rows, (0, me * nmod), (16, nmod))
    g_w_mod, gc_part = _mod_bwd(cvec, dm_sh, w_mod[0])
    gc_sum = _sum_rows8(_small_allgather(gc_part[0].reshape(8, 128), "ag_cctx"), "sum_cctx").reshape(D)
    sg_c = _sigmoid(c_ctx)
    g_c_ctx = gc_sum * (sg_c * (1.0 + c_ctx * (1.0 - sg_c)))

    big = [("w_mod", w_mod[0], g_w_mod, m_w_mod[0], v_w_mod[0]), ("w_in", w_in[0], g_w_in, m_w_in[0], v_w_in[0]),
           ("w_out", w_out[0], g_w_out, m_w_out[0], v_w_out[0]), ("w_up", w_up[0], g_w_up, m_w_up[0], v_w_up[0]),
           ("w_down", w_down[0], g_w_down, m_w_down[0], v_w_down[0])]
    upd = {n: _adamw(wv, gv, mv, vv, "adamw_" + n) for n, wv, gv, mv, vv in big}
    smalls = [("c_ctx", c_ctx, g_c_ctx, m_c_ctx, v_c_ctx), ("b_mod", b_mod, g_b_mod, m_b_mod, v_b_mod),
              ("g_norm1", g_norm1, g_g_norm1, m_g_norm1, v_g_norm1), ("rpb", rpb, g_rpb, m_rpb, v_rpb),
              ("conv_w", conv_w, g_conv_w[None], m_conv_w, v_conv_w), ("conv_b", conv_b, g_conv_b, m_conv_b, v_conv_b),
              ("ln_g", ln_g, g_ln_g, m_ln_g, v_ln_g), ("ln_b", ln_b, g_ln_b, m_ln_b, v_ln_b),
              ("g_norm2", g_norm2, g_g_norm2, m_g_norm2, v_g_norm2),
              ("ffn_conv_w", ffn_conv_w, g_fcw[None], m_ffn_conv_w, v_ffn_conv_w),
              ("ffn_conv_b", ffn_conv_b, g_fcb, m_ffn_conv_b, v_ffn_conv_b), ("g_final", g_final, g_g_final, m_g_final, v_g_final)]
    packed = [_pad_rows128(jnp.concatenate([t[j].reshape(-1) for t in smalls])) for j in (1, 2, 3, 4)]
    sd, sm_, sv = _adamw(*packed, "adamw_small")
    so = np.cumsum([0] + [int(np.prod(t[1].shape)) for t in smalls])
    for j, t in enumerate(smalls):
        shp = t[1].shape
        upd[t[0]] = tuple(arr.reshape(-1)[so[j]:so[j + 1]].reshape(shp) for arr in (sd, sm_, sv))
    grads = {"c_ctx": g_c_ctx, "w_mod": g_w_mod[None], "b_mod": g_b_mod, "g_norm1": g_g_norm1, "w_in": g_w_in[None],
             "rpb": g_rpb, "conv_w": g_conv_w[None], "conv_b": g_conv_b, "ln_g": g_ln_g, "ln_b": g_ln_b,
             "w_out": g_w_out[None], "g_norm2": g_g_norm2, "w_up": g_w_up[None], "ffn_conv_w": g_fcw[None],
             "ffn_conv_b": g_fcb, "w_down": g_w_down[None], "g_final": g_g_final}
    names = ["c_ctx", "w_mod", "b_mod", "g_norm1", "w_in", "rpb", "conv_w", "conv_b", "ln_g", "ln_b", "w_out", "g_norm2",
             "w_up", "ffn_conv_w", "ffn_conv_b", "w_down", "g_final"]
    shapes = {n: grads[n].shape for n in names}
    outs = [loss, grad_x] + [grads[n] for n in names]
    for j in range(3):
        outs += [upd[n][j].reshape(shapes[n]) for n in names]
    return tuple(outs)
```

```python
import functools

import numpy as np
import jax
import jax.numpy as jnp
from jax import lax
from jax.experimental import pallas as pl
from jax.experimental.pallas import tpu as pltpu

f32 = jnp.float32
bf16 = jnp.bfloat16

D = 1024
T = 4096
TC = 256
TA = T + TC
DA = 512
NH = 8
HD = 64
GW = 64
WR = 8
NCOL = 16
F = 2816
F2 = 2 * F
CW = 31
NDEV = 8
EPS = 1e-6
SCALE = HD ** -0.5
NEG = -1e30
MESH = pl.DeviceIdType.MESH

NT = (((1,), (1,)), ((), ()))
TN = (((0,), (0,)), ((), ()))

ADAM_LR, ADAM_B1, ADAM_B2, ADAM_EPS, ADAM_WD, ADAM_STEP = 0.001, 0.9, 0.999, 1e-08, 0.01, 10

VMEM_LIMIT = 56 * 1024 * 1024


def _cp(*sem):
    return pltpu.CompilerParams(dimension_semantics=sem or None, vmem_limit_bytes=VMEM_LIMIT)


def _dot(a, b):
    return jnp.dot(a, b, preferred_element_type=f32)


def _dg(a, b, dims):
    return lax.dot_general(a, b, dims, preferred_element_type=f32)


def _sigmoid(x):
    return 1.0 / (1.0 + jnp.exp(-x))


def _full(shape):
    n = len(shape)
    return pl.BlockSpec(shape, lambda *_: (0,) * n)


def _my_pos():
    return lax.axis_index("x"), lax.axis_index("y"), lax.axis_index("c")


def _small_allgather(v, name):
    n = v.shape[0]

    def body(v_ref, out_ref, send_sems, recv_sems):
        x, y, c = _my_pos()
        me = 4 * x + 2 * y + c
        out_ref[me] = v_ref[...]
        peers = []
        for k in range(1, NDEV):
            kx, ky, kc = (k >> 2) & 1, (k >> 1) & 1, k & 1
            peers.append((x ^ kx, y ^ ky, c ^ kc))

        def copy(k, slot, to):
            return pltpu.make_async_remote_copy(
                src_ref=v_ref, dst_ref=out_ref.at[slot], send_sem=send_sems.at[k], recv_sem=recv_sems.at[k],
                device_id=to, device_id_type=MESH)

        sends = [copy(k, me, p) for k, p in enumerate(peers)]
        for cp in sends:
            cp.start()
        for k, (px, py, pc) in enumerate(peers):
            copy(k, 4 * px + 2 * py + pc, (x, y, c)).wait_recv()
        for cp in sends:
            cp.wait_send()

    return pl.pallas_call(
        body, name=name,
        out_shape=jax.ShapeDtypeStruct((NDEV, n, 128), f32),
        in_specs=[pl.BlockSpec(memory_space=pltpu.VMEM)],
        out_specs=pl.BlockSpec(memory_space=pltpu.VMEM),
        scratch_shapes=[pltpu.SemaphoreType.DMA((NDEV - 1,)), pltpu.SemaphoreType.DMA((NDEV - 1,))],
    )(v)


def _ag2_plan(x_refs, out_refs, send_sems, recv_sems, local_sems):
    na = len(x_refs)
    x, y, c = _my_pos()
    me, sibling = (x, y, c), (x, y, 1 - c)
    chips = [(1 - x, y), (x, 1 - y), (1 - x, 1 - y)]

    def rows(i, px, py, pc):
        m_per = x_refs[i].shape[0]
        return out_refs[i].at[pl.ds(pl.multiple_of((4 * px + 2 * py + pc) * m_per, 16), m_per), :]

    def copies(k, block, to, from_shard=False):
        return [pltpu.make_async_remote_copy(
            src_ref=x_refs[i] if from_shard else rows(i, *block), dst_ref=rows(i, *block),
            send_sem=send_sems.at[k * na + i], recv_sem=recv_sems.at[k * na + i], device_id=to, device_id_type=MESH)
            for i in range(na)]

    def mine():
        return [pltpu.make_async_copy(x_refs[i], rows(i, *me), local_sems.at[i]) for i in range(na)]

    def first():
        cps = copies(0, me, sibling, True)
        for j, chip in enumerate(chips):
            cps += copies(1 + j, me, (*chip, c), True)
        return cps

    def start():
        for cp in mine() + first():
            cp.start()

    def forward():
        for j, chip in enumerate(chips):
            for cp in copies(1 + j, (*chip, c), me):
                cp.wait_recv()
            for cp in copies(4 + j, (*chip, c), sibling):
                cp.start()

    def finish():
        for cp in copies(0, sibling, me):
            cp.wait_recv()
        for j, chip in enumerate(chips):
            for cp in copies(4 + j, (*chip, 1 - c), me):
                cp.wait_recv()
        for cp in first():
            cp.wait_send()
        for j, chip in enumerate(chips):
            for cp in copies(4 + j, (*chip, c), sibling):
                cp.wait_send()
        for cp in mine():
            cp.wait()

    return start, forward, finish


def _ag2_scratch(na):
    return [pltpu.SemaphoreType.DMA((7 * na,)), pltpu.SemaphoreType.DMA((7 * na,)), pltpu.SemaphoreType.DMA((na,))]


def _a2a_plan(g_ref, recv_ref, send_sems, recv_sems, local_sem):
    x, y, c = _my_pos()
    me = 4 * x + 2 * y + c
    peers = []
    for k in range(1, NDEV):
        kx, ky, kc = (k >> 2) & 1, (k >> 1) & 1, k & 1
        peers.append((x ^ kx, y ^ ky, c ^ kc))

    def sends():
        return [pltpu.make_async_remote_copy(
            src_ref=g_ref.at[4 * px + 2 * py + pc], dst_ref=recv_ref.at[me], send_sem=send_sems.at[k], recv_sem=recv_sems.at[k],
            device_id=(px, py, pc), device_id_type=MESH) for k, (px, py, pc) in enumerate(peers)]

    def own():
        return pltpu.make_async_copy(g_ref.at[me], recv_ref.at[me], local_sem)

    def start():
        own().start()
        for cp in sends():
            cp.start()

    def finish():
        for k, (px, py, pc) in enumerate(peers):
            pltpu.make_async_remote_copy(
                src_ref=g_ref.at[me], dst_ref=recv_ref.at[4 * px + 2 * py + pc], send_sem=send_sems.at[k],
                recv_sem=recv_sems.at[k], device_id=(x, y, c), device_id_type=MESH).wait_recv()
        for cp in sends():
            cp.wait_send()
        own().wait()

    return start, finish


def _allgather2(shards, name):
    na = len(shards)

    def body(*refs):
        start, forward, finish = _ag2_plan(refs[:na], refs[na:2 * na], *refs[2 * na:])
        start()
        forward()
        finish()

    return pl.pallas_call(
        body, name=name,
        out_shape=[jax.ShapeDtypeStruct((NDEV * s.shape[0], s.shape[1]), s.dtype) for s in shards],
        in_specs=[pl.BlockSpec(memory_space=pltpu.VMEM)] * na,
        out_specs=[pl.BlockSpec(memory_space=pltpu.VMEM)] * na,
        scratch_shapes=_ag2_scratch(na),
        compiler_params=pltpu.CompilerParams(vmem_limit_bytes=VMEM_LIMIT),
    )(*shards)


def _reduce_scatter2(g, name):
    _, r, n = g.shape
    ch = 16
    nch = r // ch

    def body(g_ref, out_ref, a_ref, h_ref, b_ref, s1_send, s1_recv, s2_send, s2_recv):
        x, y, c = _my_pos()
        sibling = (x, y, 1 - c)
        s1 = []
        for j in range(4):
            cp = pltpu.make_async_remote_copy(
                src_ref=g_ref.at[2 * j + (1 - c)], dst_ref=a_ref.at[j], send_sem=s1_send.at[j], recv_sem=s1_recv.at[j],
                device_id=sibling, device_id_type=MESH)
            cp.start()
            s1.append(cp)
        for cp in s1:
            cp.wait_recv()

        def add1(i, _):
            rr = pl.ds(pl.multiple_of(i * ch, ch), ch)
            for j in range(4):
                h_ref[j, rr, :] = (g_ref[2 * j + c, rr, :].astype(f32) + a_ref[j, rr, :].astype(f32)).astype(bf16)
            return 0
        lax.fori_loop(0, nch, add1, 0)
        mychip = 2 * x + y
        s2 = []
        for m in range(1, 4):
            mx, my_ = (m >> 1) & 1, m & 1
            px, py = x ^ mx, y ^ my_
            cp = pltpu.make_async_remote_copy(
                src_ref=h_ref.at[2 * px + py], dst_ref=b_ref.at[m - 1], send_sem=s2_send.at[m - 1], recv_sem=s2_recv.at[m - 1],
                device_id=(px, py, c), device_id_type=MESH)
            cp.start()
            s2.append(cp)
        for cp in s2:
            cp.wait_recv()

        def add2(i, _):
            rr = pl.ds(pl.multiple_of(i * ch, ch), ch)
            acc = h_ref[mychip, rr, :].astype(f32)
            for m in range(3):
                acc = acc + b_ref[m, rr, :].astype(f32)
            out_ref[rr, :] = acc
            return 0
        lax.fori_loop(0, nch, add2, 0)
        for cp in s1 + s2:
            cp.wait_send()

    return pl.pallas_call(
        body, name=name,
        out_shape=jax.ShapeDtypeStruct((r, n), f32),
        in_specs=[pl.BlockSpec(memory_space=pltpu.VMEM)],
        out_specs=pl.BlockSpec(memory_space=pltpu.VMEM),
        scratch_shapes=[pltpu.VMEM((4, r, n), bf16), pltpu.VMEM((4, r, n), bf16), pltpu.VMEM((3, r, n), bf16),
                        pltpu.SemaphoreType.DMA((4,)), pltpu.SemaphoreType.DMA((4,)),
                        pltpu.SemaphoreType.DMA((3,)), pltpu.SemaphoreType.DMA((3,))],
        compiler_params=pltpu.CompilerParams(vmem_limit_bytes=VMEM_LIMIT),
    )(g)


def _mod_fwd(cvec, w_sh, b_sh):
    def body(c_ref, w_ref, b_ref, o_ref):
        cv = c_ref[...]
        act = (cv * _sigmoid(cv)).astype(bf16)
        o_ref[...] = _dot(act, w_ref[...].astype(bf16)) + b_ref[...]
    return pl.pallas_call(body, name="mod_fwd", out_shape=jax.ShapeDtypeStruct((16, w_sh.shape[1]), f32))(cvec, w_sh, b_sh)


def _mod_bwd(cvec, dm_sh, w_sh):
    def body(c_ref, dm_ref, w_ref, gw_ref, gc_ref):
        cv = c_ref[...]
        act = (cv * _sigmoid(cv)).astype(bf16)
        gw_ref[...] = _dg(act, dm_ref[...].astype(bf16), TN)
        gc_ref[...] = _dg(dm_ref[8:16, :].astype(bf16), w_ref[...].astype(bf16), NT)
    return pl.pallas_call(
        body, name="mod_bwd",
        out_shape=(jax.ShapeDtypeStruct(w_sh.shape, f32), jax.ShapeDtypeStruct((8, D), f32)))(cvec, dm_sh, w_sh)


def _sum_rows8(a, name):
    n = a.shape[1]

    def body(a_ref, o_ref):
        acc = a_ref[0]
        for d in range(1, NDEV):
            acc = acc + a_ref[d]
        o_ref[...] = acc
    return pl.pallas_call(body, name=name, out_shape=jax.ShapeDtypeStruct((n, 128), f32))(a)


def _in_proj(x0, ctx0, g1, modv, w_inT):
    tm = 256
    nt = TA // tm
    nx = T // tm

    def body(x_ref, c_ref, g_ref, mod_ref, w_ref, h_ref, q_ref, k_ref, v_ref, a_ref, gg_ref):
        i = pl.program_id(0)
        is_ctx = i == nt - 1
        xv = jnp.where(is_ctx, c_ref[...], x_ref[...])
        rstd = lax.rsqrt(jnp.mean(xv * xv, axis=-1, keepdims=True) + EPS)
        sh = jnp.where(is_ctx, mod_ref[6:7, :], mod_ref[0:1, :])
        sc = jnp.where(is_ctx, mod_ref[7:8, :], mod_ref[1:2, :])
        h = ((xv * rstd * g_ref[...]) * (1.0 + sc) + sh).astype(bf16)
        h_ref[...] = h
        for j, o_ref in enumerate((q_ref, k_ref, v_ref, a_ref, gg_ref)):
            o_ref[...] = _dg(h, w_ref[j * DA:(j + 1) * DA, :], NT).astype(o_ref.dtype)

    row = lambda w: pl.BlockSpec((tm, w), lambda i: (i, 0))
    return pl.pallas_call(
        body, name="in_proj", grid=(nt,),
        in_specs=[pl.BlockSpec((tm, D), lambda i: (jnp.minimum(i, nx - 1), 0)), _full((TC, D)),
                  _full((1, D)), _full((8, D)), _full((5 * DA, D))],
        out_specs=[row(D), row(DA), row(DA), row(DA), row(DA), row(DA)],
        out_shape=[jax.ShapeDtypeStruct((TA, D), bf16)] + [jax.ShapeDtypeStruct((TA, DA), bf16)] * 3
                  + [jax.ShapeDtypeStruct((TA, DA), f32)] * 2,
        compiler_params=_cp("parallel"),
    )(x0, ctx0, g1, modv, w_inT)


def _win_start(r):
    return jnp.clip(r - WR // 2, 0, GW - WR)


def _pattern(r):
    return _win_start(r) - r + (WR - 1)


def _bias_table(rpb):
    p = np.arange(8)[:, None]
    i = np.arange(WR)[None, :]
    ro = (p - (WR - 1)) + i + (WR - 1)
    r_hot = (ro[:, :, None] == np.arange(2 * WR - 1)[None, None, :]).astype(np.float32)
    qc = np.arange(GW)[:, None]
    kc = np.arange(GW)[None, :]
    cs = np.clip(qc - NCOL // 2, 0, GW - NCOL)
    valid = (kc >= cs) & (kc < cs + NCOL)
    co = np.clip(kc - qc + (NCOL - 1), 0, 2 * NCOL - 2)
    c_hot = ((co[:, :, None] == np.arange(2 * NCOL - 1)[None, None, :]) & valid[:, :, None]).astype(np.float32)
    hi = lax.Precision.HIGHEST
    t = jnp.einsum("hrc,pir->hpic", rpb, jnp.asarray(r_hot), precision=hi)
    b = jnp.einsum("hpic,qkc->phqik", t, jnp.asarray(c_hot), precision=hi)
    b = jnp.where(jnp.asarray(valid)[None, None, :, None, :], b, NEG)
    return b.reshape(8, NH, GW, WR * GW)


def _rpb_tables():
    lane_map = np.zeros((WR * GW, WR, 2 * NCOL - 1), np.float32)
    for i in range(WR):
        for t in range(GW):
            if t >= GW - NCOL:
                lane_map[i * GW + t, i, (GW - 1 - t) + NCOL - 1] = 1.0
            elif t < NCOL - 1:
                lane_map[i * GW + t, (i - 1) % WR, NCOL - 2 - t] = 1.0
    p = np.arange(8)[:, None]
    i = np.arange(WR)[None, :]
    r_hot = ((p + i)[:, :, None] == np.arange(2 * WR - 1)[None, None, :]).astype(np.float32)
    return lane_map, r_hot


AG_FORWARD_ROW = 44


def _attn_fwd(q, k, v, bias_tab, shards):
    na = len(shards)

    def body(q_ref, k_ref, v_ref, b_ref, *rest):
        x_refs, (y_ref, lse_ref), out_refs, sems = rest[:na], rest[na:na + 2], rest[na + 2:2 * na + 2], rest[2 * na + 2:]
        r = pl.program_id(0)
        if na:
            start, forward, finish = _ag2_plan(x_refs, out_refs, *sems)
            pl.when(r == 0)(start)
            pl.when(r == AG_FORWARD_ROW)(forward)
        ks = pl.multiple_of(_win_start(r) * GW, GW)
        qq = q_ref[...]
        lo = lax.broadcasted_iota(jnp.int32, (GW, 2 * HD), 1) < HD
        kv, scores = [], []
        for pr in range(NH // 2):
            ps = slice(pr * 2 * HD, (pr + 1) * 2 * HD)
            q2 = qq[:, ps]
            kw, kc = k_ref[pl.ds(ks, WR * GW), ps], k_ref[T:TA, ps]
            kv.append((v_ref[pl.ds(ks, WR * GW), ps], v_ref[T:TA, ps]))
            for s in range(2):
                qm = jnp.where(lo if s == 0 else ~lo, q2, jnp.zeros_like(q2))
                scores.append((_dg(qm, kw, NT) * SCALE + b_ref[0, 2 * pr + s], _dg(qm, kc, NT) * SCALE))
        probs = []
        for h, (sl, sc) in enumerate(scores):
            m = jnp.maximum(jnp.max(sl, axis=-1, keepdims=True), jnp.max(sc, axis=-1, keepdims=True))
            pl_ = jnp.exp(sl - m)
            pc = jnp.exp(sc - m)
            l = jnp.sum(pl_, axis=-1, keepdims=True) + jnp.sum(pc, axis=-1, keepdims=True)
            lse_ref[:, h:h + 1] = m + jnp.log(l)
            probs.append((pl_.astype(bf16), pc.astype(bf16), 1.0 / l))
        for pr in range(NH // 2):
            ps = slice(pr * 2 * HD, (pr + 1) * 2 * HD)
            vw, vc = kv[pr]
            outs = [(_dot(pb, vw) + _dot(cb, vc)) * rl for pb, cb, rl in probs[2 * pr:2 * pr + 2]]
            y_ref[:, ps] = jnp.where(lo, outs[0], outs[1]).astype(bf16)
        if na:
            pl.when(r == GW - 1)(finish)

    hbm = pl.BlockSpec(memory_space=pl.ANY)
    return pl.pallas_call(
        body, name="attn_fwd", grid=(GW,),
        in_specs=[pl.BlockSpec((GW, DA), lambda r: (r, 0)), _full((TA, DA)), _full((TA, DA)),
                  pl.BlockSpec((1, NH, GW, WR * GW), lambda r: (_pattern(r), 0, 0, 0))] + [hbm] * na,
        out_specs=[pl.BlockSpec((GW, DA), lambda r: (r, 0)), pl.BlockSpec((GW, NH), lambda r: (r, 0))] + [hbm] * na,
        out_shape=[jax.ShapeDtypeStruct((T, DA), bf16), jax.ShapeDtypeStruct((T, NH), f32)]
                  + [jax.ShapeDtypeStruct((NDEV * s.shape[0], s.shape[1]), s.dtype) for s in shards],
        scratch_shapes=_ag2_scratch(na) if na else [],
        compiler_params=_cp("arbitrary"),
    )(q, k, v, bias_tab, *shards)


CONV_TT = 256
HALO = 16


def _halo_specs(tt, w, nrows_blocks):
    per = tt // HALO
    prev = pl.BlockSpec((HALO, w), lambda i: (jnp.maximum(i * per - 1, 0), 0))
    cur = pl.BlockSpec((tt, w), lambda i: (i, 0))
    nxt = pl.BlockSpec((HALO, w), lambda i: (jnp.minimum((i + 1) * per, nrows_blocks - 1), 0))
    return [prev, cur, nxt]


def _shifted_copies(rot, wn):
    for b in range(1, 8):
        rot[b, 0:wn - 8, :] = rot[0, pl.ds(b, wn - 8), :]


def _conf_fwd(a, g, conv_w, conv_b, ln_g, ln_b):
    tt = CONV_TT
    nt = T // tt
    sub = 32
    wn = tt + 2 * HALO

    def body(ap, ac, an, gp, gc, gn, w_ref, b_ref, lg_ref, lb_ref, y_ref, cv_ref, rot):
        i = pl.program_id(0)
        rot[0, 0:HALO, :] = jnp.where(i > 0, ap[...] * _sigmoid(gp[...]), 0.0)
        rot[0, HALO:HALO + tt, :] = ac[...] * _sigmoid(gc[...])
        rot[0, HALO + tt:, :] = jnp.where(i < nt - 1, an[...] * _sigmoid(gn[...]), 0.0)
        _shifted_copies(rot, wn)
        w = w_ref[...]
        for s in range(tt // sub):
            acc = jnp.zeros((sub, DA), f32)
            for j in range(CW):
                a8, b8 = divmod(1 + j, 8)
                acc = acc + rot[b8, pl.ds(s * sub + 8 * a8, sub), :] * w[j:j + 1, :]
            cv = acc + b_ref[...]
            cv_ref[pl.ds(s * sub, sub), :] = cv
            mu = jnp.mean(cv, axis=-1, keepdims=True)
            xc = cv - mu
            rstd = lax.rsqrt(jnp.mean(xc * xc, axis=-1, keepdims=True) + EPS)
            z = xc * rstd * lg_ref[...] + lb_ref[...]
            y_ref[pl.ds(s * sub, sub), :] = (z * _sigmoid(z)).astype(bf16)

    hs = _halo_specs(tt, DA, T // HALO)
    return pl.pallas_call(
        body, name="conf_fwd", grid=(nt,),
        in_specs=hs + hs + [_full((CW, DA)), _full((1, DA)), _full((1, DA)), _full((1, DA))],
        out_specs=[pl.BlockSpec((tt, DA), lambda i: (i, 0)), pl.BlockSpec((tt, DA), lambda i: (i, 0))],
        out_shape=[jax.ShapeDtypeStruct((T, DA), bf16), jax.ShapeDtypeStruct((T, DA), f32)],
        scratch_shapes=[pltpu.VMEM((8, wn, DA), f32)],
        compiler_params=_cp("parallel"),
    )(a, a, a, g, g, g, conv_w, conv_b, ln_g, ln_b)


def _out_proj(xa, y_na, y_cv, w_out, modv, g2):
    tm = 256

    def body(x_ref, ya_ref, yc_ref, w_ref, mod_ref, g_ref, x1_ref, pj_ref, h2_ref):
        proj = _dot(ya_ref[...], w_ref[0:DA, :]) + _dot(yc_ref[...], w_ref[DA:D, :])
        x1 = x_ref[...] + mod_ref[2:3, :] * proj
        x1_ref[...] = x1
        pj_ref[...] = proj.astype(bf16)
        rstd = lax.rsqrt(jnp.mean(x1 * x1, axis=-1, keepdims=True) + EPS)
        h2_ref[...] = ((x1 * rstd * g_ref[...]) * (1.0 + mod_ref[4:5, :]) + mod_ref[3:4, :]).astype(bf16)

    row = lambda w: pl.BlockSpec((tm, w), lambda i: (i, 0))
    return pl.pallas_call(
        body, name="out_proj", grid=(T // tm,),
        in_specs=[row(D), row(DA), row(DA), _full((D, D)), _full((8, D)), _full((1, D))],
        out_specs=[row(D), row(D), row(D)],
        out_shape=[jax.ShapeDtypeStruct((T, D), f32), jax.ShapeDtypeStruct((T, D), bf16), jax.ShapeDtypeStruct((T, D), bf16)],
        compiler_params=_cp("parallel"),
    )(xa, y_na, y_cv, w_out, modv, g2)


FFN_TT = 512
FFN_CT = 256
FFN_NC = F // FFN_CT


def _ffn_specs(tt, ct, by_token_first):
    tc = (lambda f: (lambda t, c: f(t, c))) if by_token_first else (lambda f: (lambda c, t: f(t, c)))
    per = tt // HALO
    halo = [pl.BlockSpec((HALO, D), tc(lambda t, c: (jnp.maximum(t * per - 1, 0), 0))),
            pl.BlockSpec((tt, D), tc(lambda t, c: (t, 0))),
            pl.BlockSpec((HALO, D), tc(lambda t, c: (jnp.minimum((t + 1) * per, T // HALO - 1), 0)))]
    weights = [pl.BlockSpec((ct, D), tc(lambda t, c: (c, 0))), pl.BlockSpec((ct, D), tc(lambda t, c: (c + FFN_NC, 0))),
               pl.BlockSpec((3, ct), tc(lambda t, c: (0, c))), pl.BlockSpec((3, ct), tc(lambda t, c: (0, c + FFN_NC))),
               pl.BlockSpec((1, ct), tc(lambda t, c: (0, c))), pl.BlockSpec((1, ct), tc(lambda t, c: (0, c + FFN_NC))),
               pl.BlockSpec((ct, D), tc(lambda t, c: (c, 0)))]
    return halo, weights


def _ffn_fwd(h2, w_upT, fcw, fcb, w_down):
    tt, ct = FFN_TT, FFN_CT
    nt = T // tt
    wn = tt + 2 * HALO
    half = tt // 2

    def body(hp, hc, hn, wg_ref, wv_ref, cwg_ref, cwv_ref, cbg_ref, cbv_ref, wd_ref, o_ref, u_ref, u2_ref, hwin, uwin):
        t = pl.program_id(0)
        c = pl.program_id(1)

        @pl.when(c == 0)
        def _():
            hwin[0:HALO, :] = jnp.where(t > 0, hp[...], jnp.zeros_like(hp[...]))
            hwin[HALO:HALO + tt, :] = hc[...]
            hwin[HALO + tt:, :] = jnp.where(t < nt - 1, hn[...], jnp.zeros_like(hn[...]))
            o_ref[...] = jnp.zeros_like(o_ref)

        for r0, r1 in ((0, half + 2 * HALO), (half + 2 * HALO, wn)):
            hw = hwin[r0:r1, :]
            uwin[r0:r1, :ct] = _dg(hw, wg_ref[...], NT)
            uwin[r0:r1, ct:] = _dg(hw, wv_ref[...], NT)
        cw = jnp.concatenate([cwg_ref[...], cwv_ref[...]], axis=1)
        cb = jnp.concatenate([cbg_ref[...], cbv_ref[...]], axis=1)
        for p in range(2):
            base = HALO + p * half
            rows = slice(p * half, (p + 1) * half)
            uc = uwin[pl.ds(base, half), :]
            u2 = (uwin[pl.ds(base - 1, half), :] * cw[0:1, :] + uc * cw[1:2, :]
                  + uwin[pl.ds(base + 1, half), :] * cw[2:3, :] + cb)
            u_ref[rows, :] = uc.astype(bf16)
            u2_ref[rows, :] = u2
            gate = u2[:, :ct]
            val = u2[:, ct:]
            act = (gate * _sigmoid(gate) * val).astype(bf16)
            o_ref[rows, :] += _dot(act, wd_ref[...])

    halo, weights = _ffn_specs(tt, ct, True)
    pair = pl.BlockSpec((tt, 2 * ct), lambda t, c: (t, c))
    return pl.pallas_call(
        body, name="ffn_fwd", grid=(nt, FFN_NC),
        in_specs=halo + weights,
        out_specs=[pl.BlockSpec((tt, D), lambda t, c: (t, 0)), pair, pair],
        out_shape=[jax.ShapeDtypeStruct((T, D), f32), jax.ShapeDtypeStruct((T, F2), bf16), jax.ShapeDtypeStruct((T, F2), f32)],
        scratch_shapes=[pltpu.VMEM((wn, D), bf16), pltpu.VMEM((wn, 2 * ct), f32)],
        compiler_params=_cp("parallel", "arbitrary"),
    )(h2, h2, h2, w_upT, w_upT, fcw, fcw, fcb, fcb, w_down)


def _loss_bwd(ffn, x1, tgt, modv, gf):
    tm = 256
    nt = T // tm

    def body(f_ref, x1_ref, t_ref, mod_ref, g_ref, dx2_ref, df_ref, s_ref):
        i = pl.program_id(0)

        @pl.when(i == 0)
        def _():
            s_ref[...] = jnp.zeros_like(s_ref)

        ff = f_ref[...]
        gt2 = mod_ref[5:6, :]
        x2 = x1_ref[...] + gt2 * ff
        rstd = lax.rsqrt(jnp.mean(x2 * x2, axis=-1, keepdims=True) + EPS)
        xh = x2 * rstd
        gfv = g_ref[...]
        e = xh * gfv - t_ref[...]
        dy = e * (1.0 / D)
        dxh = dy * gfv
        dx2 = rstd * (dxh - xh * jnp.mean(dxh * xh, axis=-1, keepdims=True))
        dx2_ref[...] = dx2
        df_ref[...] = (dx2 * gt2).astype(bf16)
        s_ref[0:1, :] += jnp.sum(dy * xh, axis=0, keepdims=True)
        s_ref[1:2, :] += jnp.sum(dx2 * ff, axis=0, keepdims=True)
        s_ref[2:3, :] += jnp.sum(e * e, axis=0, keepdims=True)

        @pl.when(i == nt - 1)
        def _():
            tot = jnp.sum(s_ref[2:3, :], axis=-1, keepdims=True) * (0.5 / D)
            s_ref[3:4, :] = jnp.broadcast_to(tot, (1, D))

    row = lambda: pl.BlockSpec((tm, D), lambda i: (i, 0))
    return pl.pallas_call(
        body, name="loss_bwd", grid=(nt,),
        in_specs=[row(), row(), row(), _full((8, D)), _full((1, D))],
        out_specs=[row(), row(), _full((8, D))],
        out_shape=[jax.ShapeDtypeStruct((T, D), f32), jax.ShapeDtypeStruct((T, D), bf16), jax.ShapeDtypeStruct((8, D), f32)],
        compiler_params=_cp("arbitrary"),
    )(ffn, x1, tgt, modv, gf)


def _ffn_bwd(h2, dffn, u_t, u2_t, fcw, w_down):
    tt, ct = FFN_TT, FFN_CT
    nt = T // tt
    wn = tt + 2 * HALO
    half = tt // 2

    def body(dp, dc, dn, hc, uc_ref, u2p, u2c, u2n, cwg_ref, cwv_ref, wd_ref,
             dug_ref, duv_ref, dwg_ref, dwv_ref, dwd_ref, dcwg_ref, dcwv_ref, dcbg_ref, dcbv_ref,
             dwin, d2win, dawin):
        t = pl.program_id(1)
        first, last = t == 0, t == nt - 1
        zero = jnp.zeros((HALO, D), bf16)
        dwin[0:HALO, :] = jnp.where(first, zero, dp[...])
        dwin[HALO:HALO + tt, :] = dc[...]
        dwin[HALO + tt:, :] = jnp.where(last, zero, dn[...])

        @pl.when(first)
        def _():
            for r in (dwg_ref, dwv_ref, dwd_ref, dcwg_ref, dcwv_ref, dcbg_ref, dcbv_ref):
                r[...] = jnp.zeros_like(r)

        cw = jnp.concatenate([cwg_ref[...], cwv_ref[...]], axis=1)
        split = half + 2 * HALO
        for r0, r1 in ((0, split), (split, wn)):
            dawin[r0:r1, :] = _dg(dwin[r0:r1, :], wd_ref[...], NT)

        def grads(u2v, dact):
            gate, val = u2v[:, :ct], u2v[:, ct:]
            sg = _sigmoid(gate)
            silu = gate * sg
            return dact * val * (sg * (1.0 + gate * (1.0 - sg))), dact * silu, silu * val

        for blk, r0 in ((u2p, 0), (u2n, HALO + tt)):
            dgate, dval, _ = grads(blk[...], dawin[r0:r0 + HALO, :])
            d2win[r0:r0 + HALO, :ct] = dgate
            d2win[r0:r0 + HALO, ct:] = dval
        for p in range(2):
            b0 = HALO + p * half
            rows = slice(p * half, (p + 1) * half)
            dgate, dval, act = grads(u2c[rows, :], dawin[b0:b0 + half, :])
            d2win[b0:b0 + half, :ct] = dgate
            d2win[b0:b0 + half, ct:] = dval
            dwd_ref[...] += _dg(act.astype(bf16), dc[rows, :], TN)
        for p in range(2):
            b0 = HALO + p * half
            rows = slice(p * half, (p + 1) * half)
            shifted = (d2win[pl.ds(b0 + 1, half), :], d2win[pl.ds(b0, half), :], d2win[pl.ds(b0 - 1, half), :])
            ucur = uc_ref[rows, :].astype(f32)
            dcb = jnp.sum(shifted[1], axis=0, keepdims=True)
            dcbg_ref[...] += dcb[:, :ct]
            dcbv_ref[...] += dcb[:, ct:]
            for kk in range(3):
                dck = jnp.sum(ucur * shifted[kk], axis=0, keepdims=True)
                dcwg_ref[kk:kk + 1, :] += dck[:, :ct]
                dcwv_ref[kk:kk + 1, :] += dck[:, ct:]
            du = (shifted[0] * cw[0:1, :] + shifted[1] * cw[1:2, :] + shifted[2] * cw[2:3, :]).astype(bf16)
            dug_ref[rows, :] = du[:, :ct]
            duv_ref[rows, :] = du[:, ct:]
            dwu = _dg(du, hc[rows, :], TN)
            dwg_ref[...] += dwu[:ct, :]
            dwv_ref[...] += dwu[ct:, :]

    per = tt // HALO
    prev = lambda w: (lambda c, t: (jnp.maximum(t * per - 1, 0), c if w else 0))
    nxt = lambda w: (lambda c, t: (jnp.minimum((t + 1) * per, T // HALO - 1), c if w else 0))
    tile = lambda: pl.BlockSpec((ct, D), lambda c, t: (c, 0))
    lane = lambda r, off: pl.BlockSpec((r, ct), lambda c, t: (0, c + off))
    return pl.pallas_call(
        body, name="ffn_bwd", grid=(FFN_NC, nt),
        in_specs=[pl.BlockSpec((HALO, D), prev(False)), pl.BlockSpec((tt, D), lambda c, t: (t, 0)), pl.BlockSpec((HALO, D), nxt(False)),
                  pl.BlockSpec((tt, D), lambda c, t: (t, 0)), pl.BlockSpec((tt, 2 * ct), lambda c, t: (t, c)),
                  pl.BlockSpec((HALO, 2 * ct), prev(True)), pl.BlockSpec((tt, 2 * ct), lambda c, t: (t, c)),
                  pl.BlockSpec((HALO, 2 * ct), nxt(True)), lane(3, 0), lane(3, FFN_NC), tile()],
        out_specs=[pl.BlockSpec((tt, ct), lambda c, t: (t, c)), pl.BlockSpec((tt, ct), lambda c, t: (t, c)),
                   tile(), tile(), tile(), lane(3, 0), lane(3, 0), lane(1, 0), lane(1, 0)],
        out_shape=[jax.ShapeDtypeStruct((T, F), bf16), jax.ShapeDtypeStruct((T, F), bf16),
                   jax.ShapeDtypeStruct((F, D), f32), jax.ShapeDtypeStruct((F, D), f32), jax.ShapeDtypeStruct((F, D), f32),
                   jax.ShapeDtypeStruct((3, F), f32), jax.ShapeDtypeStruct((3, F), f32),
                   jax.ShapeDtypeStruct((1, F), f32), jax.ShapeDtypeStruct((1, F), f32)],
        scratch_shapes=[pltpu.VMEM((wn, D), bf16), pltpu.VMEM((wn, 2 * ct), f32), pltpu.VMEM((wn, ct), f32)],
        compiler_params=_cp("parallel", "arbitrary"),
    )(dffn, dffn, dffn, h2, u_t, u2_t, u2_t, u2_t, fcw, fcw, w_down)


def _norm_bwd(dh, xv, gain, sh_sc, rstd):
    xh = xv * rstd
    n = xh * gain
    dn = dh * (1.0 + sh_sc)
    dxh = dn * gain
    dx = rstd * (dxh - xh * jnp.mean(dxh * xh, axis=-1, keepdims=True))
    return (dx, jnp.sum(dh, axis=0, keepdims=True), jnp.sum(dh * n, axis=0, keepdims=True),
            jnp.sum(dn * xh, axis=0, keepdims=True))


def _norm2_bwd(dug, duv, w_upT, x1, dx2, proj, w_out, y_na, y_cv, modv, g2):
    tm = 256
    nt = T // tm

    def body(dug_ref, duv_ref, w_ref, x1_ref, dx2_ref, pj_ref, wo_ref, ya_ref, yc_ref, mod_ref, g_ref,
             dx1_ref, dya_ref, dyc_ref, dwo_ref, s_ref):
        i = pl.program_id(0)

        @pl.when(i == 0)
        def _():
            s_ref[...] = jnp.zeros_like(s_ref)
            dwo_ref[...] = jnp.zeros_like(dwo_ref)

        dh2 = _dot(dug_ref[...], w_ref[0:F, :]) + _dot(duv_ref[...], w_ref[F:F2, :])
        x1 = x1_ref[...]
        rstd = lax.rsqrt(jnp.mean(x1 * x1, axis=-1, keepdims=True) + EPS)
        dxn, dsh, dsc, dgn = _norm_bwd(dh2, x1, g_ref[...], mod_ref[4:5, :], rstd)
        dx1 = dx2_ref[...] + dxn
        dx1_ref[...] = dx1
        dpj = (dx1 * mod_ref[2:3, :]).astype(bf16)
        dyc = _dg(dpj, wo_ref[...], NT)
        dya_ref[...] = dyc[:, :DA].astype(bf16)
        dyc_ref[...] = dyc[:, DA:]
        dwo_ref[0:DA, :] += _dg(ya_ref[...], dpj, TN)
        dwo_ref[DA:D, :] += _dg(yc_ref[...], dpj, TN)
        s_ref[0:1, :] += dsh
        s_ref[1:2, :] += dsc
        s_ref[2:3, :] += dgn
        s_ref[3:4, :] += jnp.sum(dx1 * pj_ref[...].astype(f32), axis=0, keepdims=True)

    row = lambda w: pl.BlockSpec((tm, w), lambda i: (i, 0))
    return pl.pallas_call(
        body, name="norm2_bwd", grid=(nt,),
        in_specs=[row(F), row(F), _full((F2, D)), row(D), row(D), row(D), _full((D, D)), row(DA), row(DA),
                  _full((8, D)), _full((1, D))],
        out_specs=[row(D), row(DA), row(DA), _full((D, D)), _full((8, D))],
        out_shape=[jax.ShapeDtypeStruct((T, D), f32), jax.ShapeDtypeStruct((T, DA), bf16), jax.ShapeDtypeStruct((T, DA), f32),
                   jax.ShapeDtypeStruct((D, D), f32), jax.ShapeDtypeStruct((8, D), f32)],
        compiler_params=_cp("arbitrary"),
    )(dug, duv, w_upT, x1, dx2, proj, w_out, y_na, y_cv, modv, g2)


def _conf_bwd(a, g, cv, dy, conv_w, ln_g, ln_b):
    tt = CONV_TT
    nt = T // tt
    sub = 32
    wn = tt + 2 * HALO

    def body(ap, ac, an, gp, gc, gn, cp_, cc, cn, dp, dc, dn, w_ref, lg_ref, lb_ref,
             da_ref, dg_ref, dcw_ref, s_ref, urot, drot, wacc):
        i = pl.program_id(0)
        first, last = i == 0, i == nt - 1

        @pl.when(first)
        def _():
            s_ref[...] = jnp.zeros_like(s_ref)
            wacc[...] = jnp.zeros_like(wacc)

        lg, lb = lg_ref[...], lb_ref[...]

        def ln_bwd(cvv, dyv):
            mu = jnp.mean(cvv, axis=-1, keepdims=True)
            xc = cvv - mu
            rstd = lax.rsqrt(jnp.mean(xc * xc, axis=-1, keepdims=True) + EPS)
            yn = xc * rstd
            z = yn * lg + lb
            sz = _sigmoid(z)
            dz = dyv * (sz * (1.0 + z * (1.0 - sz)))
            dyn = dz * lg
            dcv = rstd * (dyn - jnp.mean(dyn, axis=-1, keepdims=True) - yn * jnp.mean(dyn * yn, axis=-1, keepdims=True))
            return dcv, dz, yn

        urot[0, 0:HALO, :] = jnp.where(first, 0.0, ap[...] * _sigmoid(gp[...]))
        urot[0, HALO + tt:, :] = jnp.where(last, 0.0, an[...] * _sigmoid(gn[...]))
        drot[0, 0:HALO, :] = jnp.where(first, 0.0, ln_bwd(cp_[...], dp[...])[0])
        drot[0, HALO + tt:, :] = jnp.where(last, 0.0, ln_bwd(cn[...], dn[...])[0])
        for s in range(tt // sub):
            rr = pl.ds(s * sub, sub)
            urot[0, pl.ds(HALO + s * sub, sub), :] = ac[rr, :] * _sigmoid(gc[rr, :])
            dcv, dz, yn = ln_bwd(cc[rr, :], dc[rr, :])
            drot[0, pl.ds(HALO + s * sub, sub), :] = dcv
            s_ref[0:1, :] += jnp.sum(dcv, axis=0, keepdims=True)
            s_ref[1:2, :] += jnp.sum(dz * yn, axis=0, keepdims=True)
            s_ref[2:3, :] += jnp.sum(dz, axis=0, keepdims=True)
        _shifted_copies(urot, wn)
        _shifted_copies(drot, wn)
        w = w_ref[...]
        for s in range(tt // sub):
            rr = pl.ds(s * sub, sub)
            dcv = drot[0, pl.ds(HALO + s * sub, sub), :]
            acc = jnp.zeros((sub, DA), f32)
            for j in range(CW):
                ad, bd = divmod(2 * HALO - 1 - j, 8)
                au, bu = divmod(1 + j, 8)
                acc = acc + drot[bd, pl.ds(s * sub + 8 * ad, sub), :] * w[j:j + 1, :]
                part = urot[bu, pl.ds(s * sub + 8 * au, sub), :] * dcv
                wacc[j] += part[0:8] + part[8:16] + part[16:24] + part[24:32]
            av, gv = ac[rr, :], gc[rr, :]
            sg = _sigmoid(gv)
            da_ref[rr, :] = (acc * sg).astype(bf16)
            dg_ref[rr, :] = (acc * av * sg * (1.0 - sg)).astype(bf16)

        @pl.when(last)
        def _():
            for j in range(CW):
                dcw_ref[j:j + 1, :] = jnp.sum(wacc[j], axis=0, keepdims=True)
            dcw_ref[CW:CW + 1, :] = jnp.zeros((1, DA), f32)

    hs = _halo_specs(tt, DA, T // HALO)
    return pl.pallas_call(
        body, name="conf_bwd", grid=(nt,),
        in_specs=hs * 4 + [_full((CW, DA)), _full((1, DA)), _full((1, DA))],
        out_specs=[pl.BlockSpec((tt, DA), lambda i: (i, 0)), pl.BlockSpec((tt, DA), lambda i: (i, 0)),
                   _full((CW + 1, DA)), _full((8, DA))],
        out_shape=[jax.ShapeDtypeStruct((T, DA), bf16), jax.ShapeDtypeStruct((T, DA), bf16),
                   jax.ShapeDtypeStruct((CW + 1, DA), f32), jax.ShapeDtypeStruct((8, DA), f32)],
        scratch_shapes=[pltpu.VMEM((8, wn, DA), f32), pltpu.VMEM((8, wn, DA), f32), pltpu.VMEM((CW, 8, DA), f32)],
        compiler_params=_cp("arbitrary"),
    )(a, a, a, g, g, g, cv, cv, cv, dy, dy, dy, conv_w, ln_g, ln_b)


def _attn_bwd(q, k, v, y, dy, lse, bias_tab, blocks):
    zr = 256
    nb = len(blocks)

    def body(q_ref, k_ref, v_ref, y_ref, dy_ref, lse_ref, b_ref, *rest):
        g_refs, (dq_ref, dk_hbm, dv_hbm, db_ref) = rest[:nb], rest[nb:nb + 4]
        recv_refs, (dk_s, dv_s, sem), a2a_sems = rest[nb + 4:2 * nb + 4], rest[2 * nb + 4:2 * nb + 7], rest[2 * nb + 7:]
        r = pl.program_id(0)
        plans = [_a2a_plan(g_refs[i], recv_refs[i], *a2a_sems[3 * i:3 * i + 3]) for i in range(nb)]
        for start, _ in plans:
            pl.when(r == 0)(start)

        @pl.when(r == 0)
        def _():
            def z(i, _):
                rr = pl.ds(pl.multiple_of(i * zr, zr), zr)
                dk_s[rr, :] = jnp.zeros((zr, DA), f32)
                dv_s[rr, :] = jnp.zeros((zr, DA), f32)
                return 0
            lax.fori_loop(0, TA // zr, z, 0)

        @pl.when((r <= WR // 2) | (r > GW - WR // 2))
        def _():
            db_ref[...] = jnp.zeros_like(db_ref)

        ks = pl.multiple_of(_win_start(r) * GW, GW)
        win = pl.ds(ks, WR * GW)
        qq, yy, dyy, lse_v = q_ref[...], y_ref[...], dy_ref[...], lse_ref[...]
        lo = lax.broadcasted_iota(jnp.int32, (GW, 2 * HD), 1) < HD
        ops, heads = [], []
        for pr in range(NH // 2):
            ps = slice(pr * 2 * HD, (pr + 1) * 2 * HD)
            q2, do2 = qq[:, ps], dyy[:, ps]
            prod = do2.astype(f32) * yy[:, ps].astype(f32)
            kw, vw = k_ref[win, ps], v_ref[win, ps]
            kc, vc = k_ref[T:TA, ps], v_ref[T:TA, ps]
            ops.append((kw, kc))
            for s in range(2):
                h = 2 * pr + s
                msk = lo if s == 0 else ~lo
                qm = jnp.where(msk, q2, jnp.zeros_like(q2))
                dom = jnp.where(msk, do2, jnp.zeros_like(do2))
                delta = jnp.sum(jnp.where(msk, prod, 0.0), axis=-1, keepdims=True)
                heads.append((qm, dom, delta, _dg(qm, kw, NT), _dg(qm, kc, NT), _dg(dom, vw, NT), _dg(dom, vc, NT)))
        grads = []
        for h, (qm, dom, delta, sl, sc, dpl, dpc) in enumerate(heads):
            lh = lse_v[:, h:h + 1]
            pl_ = jnp.exp(sl * SCALE + b_ref[0, h] - lh)
            pc = jnp.exp(sc * SCALE - lh)
            dsl = pl_ * (dpl - delta)
            dsc = pc * (dpc - delta)
            db_ref[0, h] += dsl
            grads.append((qm, dom, pl_.astype(bf16), pc.astype(bf16), dsl.astype(bf16), dsc.astype(bf16)))
        for pr in range(NH // 2):
            ps = slice(pr * 2 * HD, (pr + 1) * 2 * HD)
            kw, kc = ops[pr]
            dqs = []
            dkw = dvw = dkc = dvc = None
            for s in range(2):
                qm, dom, plb, pcb, dslb, dscb = grads[2 * pr + s]
                dqs.append(_dot(dslb, kw) + _dot(dscb, kc))
                parts = (_dg(dslb, qm, TN), _dg(plb, dom, TN), _dg(dscb, qm, TN), _dg(pcb, dom, TN))
                if s == 0:
                    dkw, dvw, dkc, dvc = parts
                else:
                    dkw, dvw, dkc, dvc = dkw + parts[0], dvw + parts[1], dkc + parts[2], dvc + parts[3]
            dq_ref[:, ps] = (jnp.where(lo, dqs[0], dqs[1]) * SCALE).astype(bf16)
            dk_s[win, ps] += dkw * SCALE
            dv_s[win, ps] += dvw
            dk_s[T:TA, ps] += dkc * SCALE
            dv_s[T:TA, ps] += dvc

        @pl.when(r == GW - 1)
        def _():
            c1 = pltpu.make_async_copy(dk_s, dk_hbm, sem.at[0])
            c2 = pltpu.make_async_copy(dv_s, dv_hbm, sem.at[1])
            c1.start()
            c2.start()
            c1.wait()
            c2.wait()

        for _, finish in plans:
            pl.when(r == GW - 1)(finish)

    rowq = lambda: pl.BlockSpec((GW, DA), lambda r: (r, 0))
    hbm = pl.BlockSpec(memory_space=pl.ANY)
    a2a_scratch = [pltpu.SemaphoreType.DMA((NDEV - 1,)), pltpu.SemaphoreType.DMA((NDEV - 1,)), pltpu.SemaphoreType.DMA]
    return pl.pallas_call(
        body, name="attn_bwd", grid=(GW,),
        in_specs=[rowq(), _full((TA, DA)), _full((TA, DA)), rowq(), rowq(), pl.BlockSpec((GW, NH), lambda r: (r, 0)),
                  pl.BlockSpec((1, NH, GW, WR * GW), lambda r: (_pattern(r), 0, 0, 0))] + [hbm] * nb,
        out_specs=[rowq(), hbm, hbm, pl.BlockSpec((1, NH, GW, WR * GW), lambda r: (_pattern(r), 0, 0, 0))] + [hbm] * nb,
        out_shape=[jax.ShapeDtypeStruct((T, DA), bf16), jax.ShapeDtypeStruct((TA, DA), f32), jax.ShapeDtypeStruct((TA, DA), f32),
                   jax.ShapeDtypeStruct((8, NH, GW, WR * GW), f32)] + [jax.ShapeDtypeStruct(b.shape, b.dtype) for b in blocks],
        scratch_shapes=[pltpu.VMEM((TA, DA), f32), pltpu.VMEM((TA, DA), f32), pltpu.SemaphoreType.DMA((2,))] + a2a_scratch * nb,
        compiler_params=_cp("arbitrary"),
    )(q, k, v, y, dy, lse, bias_tab, *blocks)


def _sum_blocks(recv, name):
    _, r, n = recv.shape
    tr = r // 5 if r % 80 == 0 else r

    def body(a_ref, o_ref):
        acc = a_ref[0].astype(f32)
        for d in range(1, NDEV):
            acc = acc + a_ref[d].astype(f32)
        o_ref[...] = acc

    return pl.pallas_call(
        body, name=name, grid=(r // tr,),
        in_specs=[pl.BlockSpec((NDEV, tr, n), lambda i: (0, i, 0))],
        out_specs=pl.BlockSpec((tr, n), lambda i: (i, 0)),
        out_shape=jax.ShapeDtypeStruct((r, n), f32),
        compiler_params=_cp("parallel"),
    )(recv)


def _rpb_reduce(dbias):
    rev = np.zeros((WR * GW, WR * GW), np.float32)
    for i in range(WR):
        for kk in range(GW):
            rev[i * GW + kk, i * GW + GW - 1 - kk] = 1.0

    def body(d_ref, rev_ref, o_ref):
        rv = rev_ref[...]
        for h in range(NH):
            dv = d_ref[0, h]
            r0 = dv.astype(bf16)
            e1 = dv - r0.astype(f32)
            r1 = e1.astype(bf16)
            r2 = (e1 - r1.astype(f32)).astype(bf16)
            rr = _dot(r0, rv) + _dot(r1, rv) + _dot(r2, rv)
            skew = pltpu.roll(rr, 0, 1, stride=1, stride_axis=0)
            o_ref[0, h:h + 1, :] = jnp.sum(skew, axis=0, keepdims=True)

    return pl.pallas_call(
        body, name="rpb_reduce", grid=(8,),
        in_specs=[pl.BlockSpec((1, NH, GW, WR * GW), lambda p: (p, 0, 0, 0)), _full((WR * GW, WR * GW))],
        out_specs=pl.BlockSpec((1, NH, WR * GW), lambda p: (p, 0, 0)),
        out_shape=jax.ShapeDtypeStruct((8, NH, WR * GW), f32),
        compiler_params=_cp("parallel"),
    )(dbias, jnp.asarray(rev, dtype=bf16))


def _norm1_bwd(dq, dk, dv, da, dg, w_inT, x0, ctx0, h, dx1, modv, g1):
    tm = 256
    nt = TA // tm
    nx = T // tm

    def body(dq_ref, dk_ref, dv_ref, da_ref, dg_ref, w_ref, x_ref, c_ref, h_ref, dx1_ref, mod_ref, g_ref,
             dx_ref, dw_ref, s_ref):
        i = pl.program_id(0)
        is_ctx = i == nt - 1

        @pl.when(i == 0)
        def _():
            s_ref[...] = jnp.zeros_like(s_ref)
            dw_ref[...] = jnp.zeros_like(dw_ref)

        hb = h_ref[...]
        dkb, dvb = dk_ref[...].astype(bf16), dv_ref[...].astype(bf16)
        dw_ref[DA:2 * DA, :] += _dg(dkb, hb, TN)
        dw_ref[2 * DA:3 * DA, :] += _dg(dvb, hb, TN)
        dh_kv = _dot(dkb, w_ref[DA:2 * DA, :]) + _dot(dvb, w_ref[2 * DA:3 * DA, :])
        gain = g_ref[...]

        @pl.when(is_ctx)
        def _():
            xv = c_ref[...]
            rstd = lax.rsqrt(jnp.mean(xv * xv, axis=-1, keepdims=True) + EPS)
            _, dsh, dsc, dgn = _norm_bwd(dh_kv, xv, gain, mod_ref[7:8, :], rstd)
            s_ref[2:3, :] += dgn
            s_ref[3:4, :] += dsh
            s_ref[4:5, :] += dsc

        @pl.when(jnp.logical_not(is_ctx))
        def _():
            dqb, dab, dgb = dq_ref[...], da_ref[...], dg_ref[...]
            dw_ref[0:DA, :] += _dg(dqb, hb, TN)
            dw_ref[3 * DA:4 * DA, :] += _dg(dab, hb, TN)
            dw_ref[4 * DA:5 * DA, :] += _dg(dgb, hb, TN)
            dh = (dh_kv + _dot(dqb, w_ref[0:DA, :]) + _dot(dab, w_ref[3 * DA:4 * DA, :])
                  + _dot(dgb, w_ref[4 * DA:5 * DA, :]))
            xv = x_ref[...]
            rstd = lax.rsqrt(jnp.mean(xv * xv, axis=-1, keepdims=True) + EPS)
            dxn, dsh, dsc, dgn = _norm_bwd(dh, xv, gain, mod_ref[1:2, :], rstd)
            dx_ref[...] = dx1_ref[...] + dxn
            s_ref[0:1, :] += dsh
            s_ref[1:2, :] += dsc
            s_ref[2:3, :] += dgn

    row = lambda w: pl.BlockSpec((tm, w), lambda i: (i, 0))
    lrow = lambda w: pl.BlockSpec((tm, w), lambda i: (jnp.minimum(i, nx - 1), 0))
    return pl.pallas_call(
        body, name="norm1_bwd", grid=(nt,),
        in_specs=[lrow(DA), row(DA), row(DA), lrow(DA), lrow(DA), _full((5 * DA, D)), lrow(D), _full((TC, D)), row(D),
                  lrow(D), _full((8, D)), _full((1, D))],
        out_specs=[lrow(D), _full((5 * DA, D)), _full((8, D))],
        out_shape=[jax.ShapeDtypeStruct((T, D), f32), jax.ShapeDtypeStruct((5 * DA, D), f32), jax.ShapeDtypeStruct((8, D), f32)],
        compiler_params=_cp("arbitrary"),
    )(dq, dk, dv, da, dg, w_inT, x0, ctx0, h, dx1, modv, g1)


def _adamw(w, g, m, v, name):
    r, c = w.shape
    tr = r
    for cand in (256, 128, 64, 32, 16, 8):
        if r % cand == 0 and r > cand:
            tr = cand
            break

    def body(w_ref, g_ref, m_ref, v_ref, d_ref, nm_ref, nv_ref):
        gv = g_ref[...]
        nm = ADAM_B1 * m_ref[...] + (1.0 - ADAM_B1) * gv
        nv = ADAM_B2 * v_ref[...] + (1.0 - ADAM_B2) * (gv * gv)
        m_hat = nm / (1.0 - ADAM_B1 ** ADAM_STEP)
        v_hat = nv / (1.0 - ADAM_B2 ** ADAM_STEP)
        d_ref[...] = -ADAM_LR * (m_hat / (jnp.sqrt(v_hat) + ADAM_EPS) + ADAM_WD * w_ref[...])
        nm_ref[...] = nm
        nv_ref[...] = nv

    spec = pl.BlockSpec((tr, c), lambda i: (i, 0))
    return pl.pallas_call(
        body, name=name, grid=(r // tr,),
        in_specs=[spec] * 4, out_specs=[spec] * 3,
        out_shape=[jax.ShapeDtypeStruct((r, c), f32)] * 3,
        compiler_params=_cp("parallel"),
    )(w, g, m, v)


def _pad_rows128(vec):
    n = vec.shape[0]
    rows = -(-n // 1024) * 8
    return jnp.pad(vec, (0, rows * 128 - n)).reshape(rows, 128)


def _grad_rpb(dbias):
    lane_map, r_hot = _rpb_tables()
    return jnp.einsum("phl,lic,pir->hrc", _rpb_reduce(dbias), jnp.asarray(lane_map), jnp.asarray(r_hot),
                      precision=lax.Precision.HIGHEST)


def kernel(x, c, ctx, c_ctx, w_mod, b_mod, g_norm1, w_in, rpb, conv_w, conv_b, ln_g, ln_b, w_out, g_norm2, w_up, ffn_conv_w, ffn_conv_b, w_down, g_final, loss_target, m_c_ctx, m_w_mod, m_b_mod, m_g_norm1, m_w_in, m_rpb, m_conv_w, m_conv_b, m_ln_g, m_ln_b, m_w_out, m_g_norm2, m_w_up, m_ffn_conv_w, m_ffn_conv_b, m_w_down, m_g_final, v_c_ctx, v_w_mod, v_b_mod, v_g_norm1, v_w_in, v_rpb, v_conv_w, v_conv_b, v_ln_g, v_ln_b, v_w_out, v_g_norm2, v_w_up, v_ffn_conv_w, v_ffn_conv_b, v_w_down, v_g_final):
    me = 4 * lax.axis_index("x") + 2 * lax.axis_index("y") + lax.axis_index("c")
    nmod = w_mod.shape[2]
    n_in = w_in.shape[2]
    n_out = w_out.shape[1]
    n_up = w_up.shape[2]
    n_dn = w_down.shape[1]
    n_cw = conv_w.shape[2]

    (w_inT,) = _allgather2([w_in[0].T.astype(bf16)], "ag_w_in")
    late_shards = [w_up[0].T.astype(bf16), w_down[0].astype(bf16), w_out[0].astype(bf16)]

    c_all = _small_allgather(c.reshape(8, 128), "ag_c").reshape(NDEV, D)
    cvec = jnp.concatenate([c_all, c_ctx[None, :], jnp.zeros((7, D), f32)], axis=0)
    b_sh = lax.dynamic_slice(b_mod, (0, me * nmod), (1, nmod))
    mod_sh = _mod_fwd(cvec, w_mod[0], b_sh)
    n_modp = 16 * nmod
    payload = jnp.concatenate([mod_sh.reshape(-1), conv_w[0].reshape(-1), ffn_conv_w[0].reshape(-1)])
    flat = _small_allgather(_pad_rows128(payload), "ag_mod").reshape(NDEV, -1)
    mod_all = flat[:, :n_modp].reshape(NDEV, 16, nmod).transpose(1, 0, 2).reshape(16, 6 * D)
    mod_me = lax.dynamic_index_in_dim(mod_all, me, 0, keepdims=False).reshape(6, D)
    mod_c = mod_all[8]
    modv = jnp.concatenate([mod_me, mod_c[None, 0:D], mod_c[None, D:2 * D]], axis=0)
    o1 = n_modp + CW * n_cw
    conv_w_f = flat[:, n_modp:o1].reshape(NDEV, CW, n_cw).transpose(1, 0, 2).reshape(CW, DA)
    fcw_f = flat[:, o1:o1 + 3 * n_up].reshape(NDEV, 3, n_up).transpose(1, 0, 2).reshape(3, F2)

    x0, ctx0 = x[0], ctx[0]
    h, q, k, v, a, g = _in_proj(x0, ctx0, g_norm1, modv, w_inT)
    bias_tab = _bias_table(rpb[0])
    y_na, lse, w_upT, w_down_f, w_out_f = _attn_fwd(q, k, v, bias_tab, late_shards)
    y_cv, cv = _conf_fwd(a, g, conv_w_f, conv_b, ln_g, ln_b)
    x1, proj, h2 = _out_proj(x0, y_na, y_cv, w_out_f, modv, g_norm2)
    ffn, u_t, u2_t = _ffn_fwd(h2, w_upT, fcw_f, ffn_conv_b, w_down_f)
    dx2, dffn, s_loss = _loss_bwd(ffn, x1, loss_target[0], modv, g_final[None, :])

    dug, duv, dwg, dwv, dw_down, dcwg, dcwv, dcbg, dcbv = _ffn_bwd(h2, dffn, u_t, u2_t, fcw_f, w_down_f)
    dx1, dy_na, dy_cv, dw_out, s_n2 = _norm2_bwd(dug, duv, w_upT, x1, dx2, proj, w_out_f, y_na, y_cv, modv, g_norm2)
    da, dg, dcw, s_cf = _conf_bwd(a, g, cv, dy_cv, conv_w_f, ln_g, ln_b)
    dfcw = jnp.concatenate([dcwg, dcwv], axis=1)
    dfcb = jnp.concatenate([dcbg[0], dcbv[0]])
    small = jnp.concatenate([dcw[:CW].reshape(CW, NDEV, n_cw).transpose(1, 0, 2).reshape(NDEV, CW * n_cw),
                             dfcw.reshape(3, NDEV, n_up).transpose(1, 0, 2).reshape(NDEV, 3 * n_up)], axis=1)
    small = jnp.pad(small.reshape(NDEV, 4, D), ((0, 0), (0, 12), (0, 0)))
    g_late = jnp.concatenate([dwg.reshape(NDEV // 2, n_up, D), dwv.reshape(NDEV // 2, n_up, D)], axis=0)
    g_late = jnp.concatenate([g_late, dw_down.reshape(NDEV, n_dn, D), small, dw_out.reshape(NDEV, n_out, D)],
                             axis=1).astype(bf16)
    dq, dk, dv, dbias, recv = _attn_bwd(q, k, v, y_na, dy_na, lse, bias_tab, [g_late])
    r_f = _sum_blocks(recv, "sum_late")
    grad_rpb_part = _grad_rpb(dbias)
    grad_x, dw_inT, s_n1 = _norm1_bwd(dq, dk, dv, da, dg, w_inT, x0, ctx0, h, dx1, modv, g_norm1)
    grad_x = grad_x[None]
    r_a = _reduce_scatter2(dw_inT.reshape(NDEV, n_in, D).astype(bf16), "rs_w_in")
    g_w_in = r_a.T
    g_w_up = r_f[:n_up].T
    g_w_down = r_f[n_up:n_up + n_dn]
    sm = r_f[n_up + n_dn:n_up + n_dn + 4].reshape(-1)
    g_conv_w = sm[:CW * n_cw].reshape(CW, n_cw)
    g_fcw = sm[CW * n_cw:].reshape(3, n_up)
    g_w_out = r_f[n_up + n_dn + 16:]

    dmod = jnp.concatenate([s_n1[0], s_n1[1], s_n2[3], s_n2[0], s_n2[1], s_loss[1]])
    dmodc = jnp.concatenate([s_n1[3], s_n1[4]])
    parts = [dmodc, s_n1[2], grad_rpb_part.reshape(-1), s_cf[0], s_cf[1], s_cf[2], s_n2[2], dfcb, s_loss[0], s_loss[3, 0:1]]
    sizes = [p.shape[0] for p in parts]
    pvec = _pad_rows128(jnp.concatenate([dmod] + parts))
    gath = _small_allgather(pvec, "ag_small")
    tot = _sum_rows8(gath, "sum_small").reshape(-1)
    dmod_all = gath.reshape(NDEV, -1)[:, :6 * D]
    offs = np.cumsum([6 * D] + sizes)
    pick = lambda j: tot[offs[j]:offs[j + 1]]
    dmodc_t = jnp.pad(pick(0), (0, 4 * D))
    g_b_mod = (tot[:6 * D] + dmodc_t)[None, :]
    g_g_norm1 = pick(1)[None, :]
    g_rpb = pick(2).reshape(1, NH, 2 * WR - 1, 2 * NCOL - 1)
    g_conv_b, g_ln_g, g_ln_b = pick(3)[None, :], pick(4)[None, :], pick(5)[None, :]
    g_g_norm2 = pick(6)[None, :]
    g_fcb = pick(7)[None, :]
    g_g_final = pick(8)
    loss = pick(9)[0]
    dm_rows = jnp.concatenate([dmod_all, dmodc_t[None, :], jnp.zeros((7, 6 * D), f32)], axis=0)
    dm_sh = lax.dynamic_slice(dm_rows, (0, me * nmod), (16, nmod))
    g_w_mod, gc_part = _mod_bwd(cvec, dm_sh, w_mod[0])
    gc_sum = _sum_rows8(_small_allgather(gc_part[0].reshape(8, 128), "ag_cctx"), "sum_cctx").reshape(D)
    sg_c = _sigmoid(c_ctx)
    g_c_ctx = gc_sum * (sg_c * (1.0 + c_ctx * (1.0 - sg_c)))

    big = [("w_mod", w_mod[0], g_w_mod, m_w_mod[0], v_w_mod[0]), ("w_in", w_in[0], g_w_in, m_w_in[0], v_w_in[0]),
           ("w_out", w_out[0], g_w_out, m_w_out[0], v_w_out[0]), ("w_up", w_up[0], g_w_up, m_w_up[0], v_w_up[0]),
           ("w_down", w_down[0], g_w_down, m_w_down[0], v_w_down[0])]
    upd = {n: _adamw(wv, gv, mv, vv, "adamw_" + n) for n, wv, gv, mv, vv in big}
    smalls = [("c_ctx", c_ctx, g_c_ctx, m_c_ctx, v_c_ctx), ("b_mod", b_mod, g_b_mod, m_b_mod, v_b_mod),
              ("g_norm1", g_norm1, g_g_norm1, m_g_norm1, v_g_norm1), ("rpb", rpb, g_rpb, m_rpb, v_rpb),
              ("conv_w", conv_w, g_conv_w[None], m_conv_w, v_conv_w), ("conv_b", conv_b, g_conv_b, m_conv_b, v_conv_b),
              ("ln_g", ln_g, g_ln_g, m_ln_g, v_ln_g), ("ln_b", ln_b, g_ln_b, m_ln_b, v_ln_b),
              ("g_norm2", g_norm2, g_g_norm2, m_g_norm2, v_g_norm2),
              ("ffn_conv_w", ffn_conv_w, g_fcw[None], m_ffn_conv_w, v_ffn_conv_w),
              ("ffn_conv_b", ffn_conv_b, g_fcb, m_ffn_conv_b, v_ffn_conv_b), ("g_final", g_final, g_g_final, m_g_final, v_g_final)]
    packed = [_pad_rows128(jnp.concatenate([t[j].reshape(-1) for t in smalls])) for j in (1, 2, 3, 4)]
    sd, sm_, sv = _adamw(*packed, "adamw_small")
    so = np.cumsum([0] + [int(np.prod(t[1].shape)) for t in smalls])
    for j, t in enumerate(smalls):
        shp = t[1].shape
        upd[t[0]] = tuple(arr.reshape(-1)[so[j]:so[j + 1]].reshape(shp) for arr in (sd, sm_, sv))
    grads = {"c_ctx": g_c_ctx, "w_mod": g_w_mod[None], "b_mod": g_b_mod, "g_norm1": g_g_norm1, "w_in": g_w_in[None],
             "rpb": g_rpb, "conv_w": g_conv_w[None], "conv_b": g_conv_b, "ln_g": g_ln_g, "ln_b": g_ln_b,
             "w_out": g_w_out[None], "g_norm2": g_g_norm2, "w_up": g_w_up[None], "ffn_conv_w": g_fcw[None],
             "ffn_conv_b": g_fcb, "w_down": g_w_down[None], "g_final": g_g_final}
    names = ["c_ctx", "w_mod", "b_mod", "g_norm1", "w_in", "rpb", "conv_w", "conv_b", "ln_g", "ln_b", "w_out", "g_norm2",
             "w_up", "ffn_conv_w", "ffn_conv_b", "w_down", "g_final"]
    shapes = {n: grads[n].shape for n in names}
    outs = [loss, grad_x] + [grads[n] for n in names]
    for j in range(3):
        outs += [upd[n][j].reshape(shapes[n]) for n in names]
    return tuple(outs)
```

```python
import functools

import numpy as np
import jax
import jax.numpy as jnp
from jax import lax
from jax.experimental import pallas as pl
from jax.experimental.pallas import tpu as pltpu

f32 = jnp.float32
bf16 = jnp.bfloat16

D = 1024
T = 4096
TC = 256
TA = T + TC
DA = 512
NH = 8
HD = 64
GW = 64
WR = 8
NCOL = 16
F = 2816
F2 = 2 * F
CW = 31
NDEV = 8
EPS = 1e-6
SCALE = HD ** -0.5
NEG = -1e30
MESH = pl.DeviceIdType.MESH

NT = (((1,), (1,)), ((), ()))
TN = (((0,), (0,)), ((), ()))

ADAM_LR, ADAM_B1, ADAM_B2, ADAM_EPS, ADAM_WD, ADAM_STEP = 0.001, 0.9, 0.999, 1e-08, 0.01, 10

VMEM_LIMIT = 56 * 1024 * 1024


def _cp(*sem):
    return pltpu.CompilerParams(dimension_semantics=sem or None, vmem_limit_bytes=VMEM_LIMIT)


def _dot(a, b):
    return jnp.dot(a, b, preferred_element_type=f32)


def _dg(a, b, dims):
    return lax.dot_general(a, b, dims, preferred_element_type=f32)


def _sigmoid(x):
    return 1.0 / (1.0 + jnp.exp(-x))


def _full(shape):
    n = len(shape)
    return pl.BlockSpec(shape, lambda *_: (0,) * n)


def _my_pos():
    return lax.axis_index("x"), lax.axis_index("y"), lax.axis_index("c")


def _small_allgather(v, name):
    n = v.shape[0]

    def body(v_ref, out_ref, send_sems, recv_sems):
        x, y, c = _my_pos()
        me = 4 * x + 2 * y + c
        out_ref[me] = v_ref[...]
        peers = []
        for k in range(1, NDEV):
            kx, ky, kc = (k >> 2) & 1, (k >> 1) & 1, k & 1
            peers.append((x ^ kx, y ^ ky, c ^ kc))

        def copy(k, slot, to):
            return pltpu.make_async_remote_copy(
                src_ref=v_ref, dst_ref=out_ref.at[slot], send_sem=send_sems.at[k], recv_sem=recv_sems.at[k],
                device_id=to, device_id_type=MESH)

        sends = [copy(k, me, p) for k, p in enumerate(peers)]
        for cp in sends:
            cp.start()
        for k, (px, py, pc) in enumerate(peers):
            copy(k, 4 * px + 2 * py + pc, (x, y, c)).wait_recv()
        for cp in sends:
            cp.wait_send()

    return pl.pallas_call(
        body, name=name,
        out_shape=jax.ShapeDtypeStruct((NDEV, n, 128), f32),
        in_specs=[pl.BlockSpec(memory_space=pltpu.VMEM)],
        out_specs=pl.BlockSpec(memory_space=pltpu.VMEM),
        scratch_shapes=[pltpu.SemaphoreType.DMA((NDEV - 1,)), pltpu.SemaphoreType.DMA((NDEV - 1,))],
    )(v)


def _ag2_plan(x_refs, out_refs, send_sems, recv_sems, local_sems):
    na = len(x_refs)
    x, y, c = _my_pos()
    me, sibling = (x, y, c), (x, y, 1 - c)
    chips = [(1 - x, y), (x, 1 - y), (1 - x, 1 - y)]

    def rows(i, px, py, pc):
        m_per = x_refs[i].shape[0]
        return out_refs[i].at[pl.ds(pl.multiple_of((4 * px + 2 * py + pc) * m_per, 16), m_per), :]

    def copies(k, block, to, from_shard=False):
        return [pltpu.make_async_remote_copy(
            src_ref=x_refs[i] if from_shard else rows(i, *block), dst_ref=rows(i, *block),
            send_sem=send_sems.at[k * na + i], recv_sem=recv_sems.at[k * na + i], device_id=to, device_id_type=MESH)
            for i in range(na)]

    def mine():
        return [pltpu.make_async_copy(x_refs[i], rows(i, *me), local_sems.at[i]) for i in range(na)]

    def first():
        cps = copies(0, me, sibling, True)
        for j, chip in enumerate(chips):
            cps += copies(1 + j, me, (*chip, c), True)
        return cps

    def start():
        for cp in mine() + first():
            cp.start()

    def forward():
        for j, chip in enumerate(chips):
            for cp in copies(1 + j, (*chip, c), me):
                cp.wait_recv()
            for cp in copies(4 + j, (*chip, c), sibling):
                cp.start()

    def finish():
        for cp in copies(0, sibling, me):
            cp.wait_recv()
        for j, chip in enumerate(chips):
            for cp in copies(4 + j, (*chip, 1 - c), me):
                cp.wait_recv()
        for cp in first():
            cp.wait_send()
        for j, chip in enumerate(chips):
            for cp in copies(4 + j, (*chip, c), sibling):
                cp.wait_send()
        for cp in mine():
            cp.wait()

    return start, forward, finish


def _ag2_scratch(na):
    return [pltpu.SemaphoreType.DMA((7 * na,)), pltpu.SemaphoreType.DMA((7 * na,)), pltpu.SemaphoreType.DMA((na,))]


def _a2a_plan(g_ref, recv_ref, send_sems, recv_sems, local_sem):
    x, y, c = _my_pos()
    me = 4 * x + 2 * y + c
    peers = []
    for k in range(1, NDEV):
        kx, ky, kc = (k >> 2) & 1, (k >> 1) & 1, k & 1
        peers.append((x ^ kx, y ^ ky, c ^ kc))

    def sends():
        return [pltpu.make_async_remote_copy(
            src_ref=g_ref.at[4 * px + 2 * py + pc], dst_ref=recv_ref.at[me], send_sem=send_sems.at[k], recv_sem=recv_sems.at[k],
            device_id=(px, py, pc), device_id_type=MESH) for k, (px, py, pc) in enumerate(peers)]

    def own():
        return pltpu.make_async_copy(g_ref.at[me], recv_ref.at[me], local_sem)

    def start():
        own().start()
        for cp in sends():
            cp.start()

    def finish():
        for k, (px, py, pc) in enumerate(peers):
            pltpu.make_async_remote_copy(
                src_ref=g_ref.at[me], dst_ref=recv_ref.at[4 * px + 2 * py + pc], send_sem=send_sems.at[k],
                recv_sem=recv_sems.at[k], device_id=(x, y, c), device_id_type=MESH).wait_recv()
        for cp in sends():
            cp.wait_send()
        own().wait()

    return start, finish


def _a2a_scratch():
    return [pltpu.SemaphoreType.DMA((NDEV - 1,)), pltpu.SemaphoreType.DMA((NDEV - 1,)), pltpu.SemaphoreType.DMA]


def _allgather2(shards, name):
    na = len(shards)

    def body(*refs):
        start, forward, finish = _ag2_plan(refs[:na], refs[na:2 * na], *refs[2 * na:])
        start()
        forward()
        finish()

    return pl.pallas_call(
        body, name=name,
        out_shape=[jax.ShapeDtypeStruct((NDEV * s.shape[0], s.shape[1]), s.dtype) for s in shards],
        in_specs=[pl.BlockSpec(memory_space=pltpu.VMEM)] * na,
        out_specs=[pl.BlockSpec(memory_space=pltpu.VMEM)] * na,
        scratch_shapes=_ag2_scratch(na),
        compiler_params=pltpu.CompilerParams(vmem_limit_bytes=VMEM_LIMIT),
    )(*shards)


def _reduce_scatter2(g, name):
    _, r, n = g.shape
    ch = 16
    nch = r // ch

    def body(g_ref, out_ref, a_ref, h_ref, b_ref, s1_send, s1_recv, s2_send, s2_recv):
        x, y, c = _my_pos()
        sibling = (x, y, 1 - c)
        s1 = []
        for j in range(4):
            cp = pltpu.make_async_remote_copy(
                src_ref=g_ref.at[2 * j + (1 - c)], dst_ref=a_ref.at[j], send_sem=s1_send.at[j], recv_sem=s1_recv.at[j],
                device_id=sibling, device_id_type=MESH)
            cp.start()
            s1.append(cp)
        for cp in s1:
            cp.wait_recv()

        def add1(i, _):
            rr = pl.ds(pl.multiple_of(i * ch, ch), ch)
            for j in range(4):
                h_ref[j, rr, :] = (g_ref[2 * j + c, rr, :].astype(f32) + a_ref[j, rr, :].astype(f32)).astype(bf16)
            return 0
        lax.fori_loop(0, nch, add1, 0)
        mychip = 2 * x + y
        s2 = []
        for m in range(1, 4):
            mx, my_ = (m >> 1) & 1, m & 1
            px, py = x ^ mx, y ^ my_
            cp = pltpu.make_async_remote_copy(
                src_ref=h_ref.at[2 * px + py], dst_ref=b_ref.at[m - 1], send_sem=s2_send.at[m - 1], recv_sem=s2_recv.at[m - 1],
                device_id=(px, py, c), device_id_type=MESH)
            cp.start()
            s2.append(cp)
        for cp in s2:
            cp.wait_recv()

        def add2(i, _):
            rr = pl.ds(pl.multiple_of(i * ch, ch), ch)
            acc = h_ref[mychip, rr, :].astype(f32)
            for m in range(3):
                acc = acc + b_ref[m, rr, :].astype(f32)
            out_ref[rr, :] = acc
            return 0
        lax.fori_loop(0, nch, add2, 0)
        for cp in s1 + s2:
            cp.wait_send()

    return pl.pallas_call(
        body, name=name,
        out_shape=jax.ShapeDtypeStruct((r, n), f32),
        in_specs=[pl.BlockSpec(memory_space=pltpu.VMEM)],
        out_specs=pl.BlockSpec(memory_space=pltpu.VMEM),
        scratch_shapes=[pltpu.VMEM((4, r, n), bf16), pltpu.VMEM((4, r, n), bf16), pltpu.VMEM((3, r, n), bf16),
                        pltpu.SemaphoreType.DMA((4,)), pltpu.SemaphoreType.DMA((4,)),
                        pltpu.SemaphoreType.DMA((3,)), pltpu.SemaphoreType.DMA((3,))],
        compiler_params=pltpu.CompilerParams(vmem_limit_bytes=VMEM_LIMIT),
    )(g)


def _mod_fwd(cvec, w_sh, b_sh):
    def body(c_ref, w_ref, b_ref, o_ref):
        cv = c_ref[...]
        act = (cv * _sigmoid(cv)).astype(bf16)
        o_ref[...] = _dot(act, w_ref[...].astype(bf16)) + b_ref[...]
    return pl.pallas_call(body, name="mod_fwd", out_shape=jax.ShapeDtypeStruct((16, w_sh.shape[1]), f32))(cvec, w_sh, b_sh)


def _mod_bwd(cvec, dm_sh, w_sh):
    def body(c_ref, dm_ref, w_ref, gw_ref, gc_ref):
        cv = c_ref[...]
        act = (cv * _sigmoid(cv)).astype(bf16)
        gw_ref[...] = _dg(act, dm_ref[...].astype(bf16), TN)
        gc_ref[...] = _dg(dm_ref[8:16, :].astype(bf16), w_ref[...].astype(bf16), NT)
    return pl.pallas_call(
        body, name="mod_bwd",
        out_shape=(jax.ShapeDtypeStruct(w_sh.shape, f32), jax.ShapeDtypeStruct((8, D), f32)))(cvec, dm_sh, w_sh)


def _sum_rows8(a, name):
    n = a.shape[1]

    def body(a_ref, o_ref):
        acc = a_ref[0]
        for d in range(1, NDEV):
            acc = acc + a_ref[d]
        o_ref[...] = acc
    return pl.pallas_call(body, name=name, out_shape=jax.ShapeDtypeStruct((n, 128), f32))(a)


def _in_proj(x0, ctx0, g1, modv, w_inT):
    tm = 256
    nt = TA // tm
    nx = T // tm

    def body(x_ref, c_ref, g_ref, mod_ref, w_ref, h_ref, q_ref, k_ref, v_ref, a_ref, gg_ref):
        i = pl.program_id(0)
        is_ctx = i == nt - 1
        xv = jnp.where(is_ctx, c_ref[...], x_ref[...])
        rstd = lax.rsqrt(jnp.mean(xv * xv, axis=-1, keepdims=True) + EPS)
        sh = jnp.where(is_ctx, mod_ref[6:7, :], mod_ref[0:1, :])
        sc = jnp.where(is_ctx, mod_ref[7:8, :], mod_ref[1:2, :])
        h = ((xv * rstd * g_ref[...]) * (1.0 + sc) + sh).astype(bf16)
        h_ref[...] = h
        for j, o_ref in enumerate((q_ref, k_ref, v_ref, a_ref, gg_ref)):
            o_ref[...] = _dg(h, w_ref[j * DA:(j + 1) * DA, :], NT).astype(o_ref.dtype)

    row = lambda w: pl.BlockSpec((tm, w), lambda i: (i, 0))
    return pl.pallas_call(
        body, name="in_proj", grid=(nt,),
        in_specs=[pl.BlockSpec((tm, D), lambda i: (jnp.minimum(i, nx - 1), 0)), _full((TC, D)),
                  _full((1, D)), _full((8, D)), _full((5 * DA, D))],
        out_specs=[row(D), row(DA), row(DA), row(DA), row(DA), row(DA)],
        out_shape=[jax.ShapeDtypeStruct((TA, D), bf16)] + [jax.ShapeDtypeStruct((TA, DA), bf16)] * 3
                  + [jax.ShapeDtypeStruct((TA, DA), f32)] * 2,
        compiler_params=_cp("parallel"),
    )(x0, ctx0, g1, modv, w_inT)


def _win_start(r):
    return jnp.clip(r - WR // 2, 0, GW - WR)


def _pattern(r):
    return _win_start(r) - r + (WR - 1)


def _bias_table(rpb):
    qc = np.arange(GW)[:, None]
    kc = np.arange(GW)[None, :]
    cs = np.clip(qc - NCOL // 2, 0, GW - NCOL)
    valid = np.tile(((kc >= cs) & (kc < cs + NCOL)).astype(np.int32), (1, WR))
    pad = jnp.pad(rpb, ((0, 0), (0, 0), (0, GW - (2 * NCOL - 1))))
    base = jnp.stack([pad[:, p:p + WR, :].reshape(NH, WR * GW) for p in range(8)])

    def body(base_ref, valid_ref, o_ref):
        ok = valid_ref[...] != 0
        for h in range(NH):
            row = jnp.broadcast_to(base_ref[0, h:h + 1, :], (GW, WR * GW))
            skew = pltpu.roll(row, WR * GW - (NCOL - 1), 1, stride=1, stride_axis=0)
            o_ref[0, h] = jnp.where(ok, skew, NEG)

    return pl.pallas_call(
        body, name="bias_table", grid=(8,),
        in_specs=[pl.BlockSpec((1, NH, WR * GW), lambda p: (p, 0, 0)), _full((GW, WR * GW))],
        out_specs=pl.BlockSpec((1, NH, GW, WR * GW), lambda p: (p, 0, 0, 0)),
        out_shape=jax.ShapeDtypeStruct((8, NH, GW, WR * GW), f32),
        compiler_params=_cp("parallel"),
    )(base, jnp.asarray(valid))


def _rpb_tables():
    lane_map = np.zeros((WR * GW, WR, 2 * NCOL - 1), np.float32)
    for i in range(WR):
        for t in range(GW):
            if t >= GW - NCOL:
                lane_map[i * GW + t, i, (GW - 1 - t) + NCOL - 1] = 1.0
            elif t < NCOL - 1:
                lane_map[i * GW + t, (i - 1) % WR, NCOL - 2 - t] = 1.0
    p = np.arange(8)[:, None]
    i = np.arange(WR)[None, :]
    r_hot = ((p + i)[:, :, None] == np.arange(2 * WR - 1)[None, None, :]).astype(np.float32)
    return lane_map, r_hot


AG_FORWARD_ROW = 44


def _attn_fwd(q, k, v, bias_tab, shards):
    na = len(shards)

    def body(q_ref, k_ref, v_ref, b_ref, *rest):
        x_refs, (y_ref, lse_ref), out_refs, sems = rest[:na], rest[na:na + 2], rest[na + 2:2 * na + 2], rest[2 * na + 2:]
        r = pl.program_id(0)
        if na:
            start, forward, finish = _ag2_plan(x_refs, out_refs, *sems)
            pl.when(r == 0)(start)
            pl.when(r == AG_FORWARD_ROW)(forward)
        ks = pl.multiple_of(_win_start(r) * GW, GW)
        qq = q_ref[...]
        lo = lax.broadcasted_iota(jnp.int32, (GW, 2 * HD), 1) < HD
        kv, scores = [], []
        for pr in range(NH // 2):
            ps = slice(pr * 2 * HD, (pr + 1) * 2 * HD)
            q2 = qq[:, ps]
            kw, kc = k_ref[pl.ds(ks, WR * GW), ps], k_ref[T:TA, ps]
            kv.append((v_ref[pl.ds(ks, WR * GW), ps], v_ref[T:TA, ps]))
            for s in range(2):
                qm = jnp.where(lo if s == 0 else ~lo, q2, jnp.zeros_like(q2))
                scores.append((_dg(qm, kw, NT) * SCALE + b_ref[0, 2 * pr + s], _dg(qm, kc, NT) * SCALE))
        probs = []
        for h, (sl, sc) in enumerate(scores):
            m = jnp.maximum(jnp.max(sl, axis=-1, keepdims=True), jnp.max(sc, axis=-1, keepdims=True))
            pl_ = jnp.exp(sl - m)
            pc = jnp.exp(sc - m)
            l = jnp.sum(pl_, axis=-1, keepdims=True) + jnp.sum(pc, axis=-1, keepdims=True)
            lse_ref[:, h:h + 1] = m + jnp.log(l)
            probs.append((pl_.astype(bf16), pc.astype(bf16), 1.0 / l))
        for pr in range(NH // 2):
            ps = slice(pr * 2 * HD, (pr + 1) * 2 * HD)
            vw, vc = kv[pr]
            outs = [(_dot(pb, vw) + _dot(cb, vc)) * rl for pb, cb, rl in probs[2 * pr:2 * pr + 2]]
            y_ref[:, ps] = jnp.where(lo, outs[0], outs[1]).astype(bf16)
        if na:
            pl.when(r == GW - 1)(finish)

    hbm = pl.BlockSpec(memory_space=pl.ANY)
    return pl.pallas_call(
        body, name="attn_fwd", grid=(GW,),
        in_specs=[pl.BlockSpec((GW, DA), lambda r: (r, 0)), _full((TA, DA)), _full((TA, DA)),
                  pl.BlockSpec((1, NH, GW, WR * GW), lambda r: (_pattern(r), 0, 0, 0))] + [hbm] * na,
        out_specs=[pl.BlockSpec((GW, DA), lambda r: (r, 0)), pl.BlockSpec((GW, NH), lambda r: (r, 0))] + [hbm] * na,
        out_shape=[jax.ShapeDtypeStruct((T, DA), bf16), jax.ShapeDtypeStruct((T, NH), f32)]
                  + [jax.ShapeDtypeStruct((NDEV * s.shape[0], s.shape[1]), s.dtype) for s in shards],
        scratch_shapes=_ag2_scratch(na) if na else [],
        compiler_params=_cp("arbitrary"),
    )(q, k, v, bias_tab, *shards)


CONV_TT = 256
HALO = 16


def _halo_specs(tt, w, nrows_blocks):
    per = tt // HALO
    prev = pl.BlockSpec((HALO, w), lambda i: (jnp.maximum(i * per - 1, 0), 0))
    cur = pl.BlockSpec((tt, w), lambda i: (i, 0))
    nxt = pl.BlockSpec((HALO, w), lambda i: (jnp.minimum((i + 1) * per, nrows_blocks - 1), 0))
    return [prev, cur, nxt]


def _shifted_copies(rot, wn):
    for b in range(1, 8):
        rot[b, 0:wn - 8, :] = rot[0, pl.ds(b, wn - 8), :]


def _conf_fwd(a, g, conv_w, conv_b, ln_g, ln_b, shards):
    tt = CONV_TT
    nt = T // tt
    sub = 32
    wn = tt + 2 * HALO
    na = len(shards)

    def body(ap, ac, an, gp, gc, gn, w_ref, b_ref, lg_ref, lb_ref, *rest):
        x_refs, (y_ref, cv_ref), out_refs, (rot, *sems) = rest[:na], rest[na:na + 2], rest[na + 2:2 * na + 2], rest[2 * na + 2:]
        i = pl.program_id(0)
        if na:
            start, forward, finish = _ag2_plan(x_refs, out_refs, *sems)
            pl.when(i == 0)(start)
            pl.when(i == nt // 2)(forward)
        rot[0, 0:HALO, :] = jnp.where(i > 0, ap[...] * _sigmoid(gp[...]), 0.0)
        rot[0, HALO:HALO + tt, :] = ac[...] * _sigmoid(gc[...])
        rot[0, HALO + tt:, :] = jnp.where(i < nt - 1, an[...] * _sigmoid(gn[...]), 0.0)
        _shifted_copies(rot, wn)
        w = w_ref[...]
        for s in range(tt // sub):
            acc = jnp.zeros((sub, DA), f32)
            for j in range(CW):
                a8, b8 = divmod(1 + j, 8)
                acc = acc + rot[b8, pl.ds(s * sub + 8 * a8, sub), :] * w[j:j + 1, :]
            cv = acc + b_ref[...]
            cv_ref[pl.ds(s * sub, sub), :] = cv
            mu = jnp.mean(cv, axis=-1, keepdims=True)
            xc = cv - mu
            rstd = lax.rsqrt(jnp.mean(xc * xc, axis=-1, keepdims=True) + EPS)
            z = xc * rstd * lg_ref[...] + lb_ref[...]
            y_ref[pl.ds(s * sub, sub), :] = (z * _sigmoid(z)).astype(bf16)
        if na:
            pl.when(i == nt - 1)(finish)

    hs = _halo_specs(tt, DA, T // HALO)
    hbm = pl.BlockSpec(memory_space=pl.ANY)
    return pl.pallas_call(
        body, name="conf_fwd", grid=(nt,),
        in_specs=hs + hs + [_full((CW, DA)), _full((1, DA)), _full((1, DA)), _full((1, DA))] + [hbm] * na,
        out_specs=[pl.BlockSpec((tt, DA), lambda i: (i, 0)), pl.BlockSpec((tt, DA), lambda i: (i, 0))] + [hbm] * na,
        out_shape=[jax.ShapeDtypeStruct((T, DA), bf16), jax.ShapeDtypeStruct((T, DA), f32)]
                  + [jax.ShapeDtypeStruct((NDEV * s.shape[0], s.shape[1]), s.dtype) for s in shards],
        scratch_shapes=[pltpu.VMEM((8, wn, DA), f32)] + (_ag2_scratch(na) if na else []),
        compiler_params=_cp("arbitrary"),
    )(a, a, a, g, g, g, conv_w, conv_b, ln_g, ln_b, *shards)


def _out_proj(xa, y_na, y_cv, w_out, modv, g2):
    tm = 256

    def body(x_ref, ya_ref, yc_ref, w_ref, mod_ref, g_ref, x1_ref, pj_ref, h2_ref):
        proj = _dot(ya_ref[...], w_ref[0:DA, :]) + _dot(yc_ref[...], w_ref[DA:D, :])
        x1 = x_ref[...] + mod_ref[2:3, :] * proj
        x1_ref[...] = x1
        pj_ref[...] = proj.astype(bf16)
        rstd = lax.rsqrt(jnp.mean(x1 * x1, axis=-1, keepdims=True) + EPS)
        h2_ref[...] = ((x1 * rstd * g_ref[...]) * (1.0 + mod_ref[4:5, :]) + mod_ref[3:4, :]).astype(bf16)

    row = lambda w: pl.BlockSpec((tm, w), lambda i: (i, 0))
    return pl.pallas_call(
        body, name="out_proj", grid=(T // tm,),
        in_specs=[row(D), row(DA), row(DA), _full((D, D)), _full((8, D)), _full((1, D))],
        out_specs=[row(D), row(D), row(D)],
        out_shape=[jax.ShapeDtypeStruct((T, D), f32), jax.ShapeDtypeStruct((T, D), bf16), jax.ShapeDtypeStruct((T, D), bf16)],
        compiler_params=_cp("parallel"),
    )(xa, y_na, y_cv, w_out, modv, g2)


FFN_TT = 512
FFN_CT = 256
FFN_NC = F // FFN_CT


def _ffn_specs(tt, ct, by_token_first):
    tc = (lambda f: (lambda t, c: f(t, c))) if by_token_first else (lambda f: (lambda c, t: f(t, c)))
    per = tt // HALO
    halo = [pl.BlockSpec((HALO, D), tc(lambda t, c: (jnp.maximum(t * per - 1, 0), 0))),
            pl.BlockSpec((tt, D), tc(lambda t, c: (t, 0))),
            pl.BlockSpec((HALO, D), tc(lambda t, c: (jnp.minimum((t + 1) * per, T // HALO - 1), 0)))]
    weights = [pl.BlockSpec((ct, D), tc(lambda t, c: (c, 0))), pl.BlockSpec((ct, D), tc(lambda t, c: (c + FFN_NC, 0))),
               pl.BlockSpec((3, ct), tc(lambda t, c: (0, c))), pl.BlockSpec((3, ct), tc(lambda t, c: (0, c + FFN_NC))),
               pl.BlockSpec((1, ct), tc(lambda t, c: (0, c))), pl.BlockSpec((1, ct), tc(lambda t, c: (0, c + FFN_NC))),
               pl.BlockSpec((ct, D), tc(lambda t, c: (c, 0)))]
    return halo, weights


def _ffn_fwd(h2, w_upT, fcw, fcb, w_down):
    tt, ct = FFN_TT, FFN_CT
    nt = T // tt
    wn = tt + 2 * HALO
    half = tt // 2

    def body(hp, hc, hn, wg_ref, wv_ref, cwg_ref, cwv_ref, cbg_ref, cbv_ref, wd_ref, o_ref, u_ref, u2_ref, hwin, uwin):
        t = pl.program_id(0)
        c = pl.program_id(1)

        @pl.when(c == 0)
        def _():
            hwin[0:HALO, :] = jnp.where(t > 0, hp[...], jnp.zeros_like(hp[...]))
            hwin[HALO:HALO + tt, :] = hc[...]
            hwin[HALO + tt:, :] = jnp.where(t < nt - 1, hn[...], jnp.zeros_like(hn[...]))
            o_ref[...] = jnp.zeros_like(o_ref)

        for r0, r1 in ((0, half + 2 * HALO), (half + 2 * HALO, wn)):
            hw = hwin[r0:r1, :]
            uwin[r0:r1, :ct] = _dg(hw, wg_ref[...], NT)
            uwin[r0:r1, ct:] = _dg(hw, wv_ref[...], NT)
        cw = jnp.concatenate([cwg_ref[...], cwv_ref[...]], axis=1)
        cb = jnp.concatenate([cbg_ref[...], cbv_ref[...]], axis=1)
        for p in range(2):
            base = HALO + p * half
            rows = slice(p * half, (p + 1) * half)
            uc = uwin[pl.ds(base, half), :]
            u2 = (uwin[pl.ds(base - 1, half), :] * cw[0:1, :] + uc * cw[1:2, :]
                  + uwin[pl.ds(base + 1, half), :] * cw[2:3, :] + cb)
            u_ref[rows, :] = uc.astype(bf16)
            u2_ref[rows, :] = u2
            gate = u2[:, :ct]
            val = u2[:, ct:]
            act = (gate * _sigmoid(gate) * val).astype(bf16)
            o_ref[rows, :] += _dot(act, wd_ref[...])

    halo, weights = _ffn_specs(tt, ct, True)
    pair = pl.BlockSpec((tt, 2 * ct), lambda t, c: (t, c))
    return pl.pallas_call(
        body, name="ffn_fwd", grid=(nt, FFN_NC),
        in_specs=halo + weights,
        out_specs=[pl.BlockSpec((tt, D), lambda t, c: (t, 0)), pair, pair],
        out_shape=[jax.ShapeDtypeStruct((T, D), f32), jax.ShapeDtypeStruct((T, F2), bf16), jax.ShapeDtypeStruct((T, F2), f32)],
        scratch_shapes=[pltpu.VMEM((wn, D), bf16), pltpu.VMEM((wn, 2 * ct), f32)],
        compiler_params=_cp("parallel", "arbitrary"),
    )(h2, h2, h2, w_upT, w_upT, fcw, fcw, fcb, fcb, w_down)


def _loss_bwd(ffn, x1, tgt, modv, gf):
    tm = 256
    nt = T // tm

    def body(f_ref, x1_ref, t_ref, mod_ref, g_ref, dx2_ref, df_ref, s_ref):
        i = pl.program_id(0)

        @pl.when(i == 0)
        def _():
            s_ref[...] = jnp.zeros_like(s_ref)

        ff = f_ref[...]
        gt2 = mod_ref[5:6, :]
        x2 = x1_ref[...] + gt2 * ff
        rstd = lax.rsqrt(jnp.mean(x2 * x2, axis=-1, keepdims=True) + EPS)
        xh = x2 * rstd
        gfv = g_ref[...]
        e = xh * gfv - t_ref[...]
        dy = e * (1.0 / D)
        dxh = dy * gfv
        dx2 = rstd * (dxh - xh * jnp.mean(dxh * xh, axis=-1, keepdims=True))
        dx2_ref[...] = dx2
        df_ref[...] = (dx2 * gt2).astype(bf16)
        s_ref[0:1, :] += jnp.sum(dy * xh, axis=0, keepdims=True)
        s_ref[1:2, :] += jnp.sum(dx2 * ff, axis=0, keepdims=True)
        s_ref[2:3, :] += jnp.sum(e * e, axis=0, keepdims=True)

        @pl.when(i == nt - 1)
        def _():
            tot = jnp.sum(s_ref[2:3, :], axis=-1, keepdims=True) * (0.5 / D)
            s_ref[3:4, :] = jnp.broadcast_to(tot, (1, D))

    row = lambda: pl.BlockSpec((tm, D), lambda i: (i, 0))
    return pl.pallas_call(
        body, name="loss_bwd", grid=(nt,),
        in_specs=[row(), row(), row(), _full((8, D)), _full((1, D))],
        out_specs=[row(), row(), _full((8, D))],
        out_shape=[jax.ShapeDtypeStruct((T, D), f32), jax.ShapeDtypeStruct((T, D), bf16), jax.ShapeDtypeStruct((8, D), f32)],
        compiler_params=_cp("arbitrary"),
    )(ffn, x1, tgt, modv, gf)


def _ffn_bwd(h2, dffn, u_t, u2_t, fcw, w_down):
    tt, ct = FFN_TT, FFN_CT
    nt = T // tt
    wn = tt + 2 * HALO
    half = tt // 2

    def body(dp, dc, dn, hc, uc_ref, u2p, u2c, u2n, cwg_ref, cwv_ref, wd_ref,
             dug_ref, duv_ref, dwu_ref, dwd_ref, dcwg_ref, dcwv_ref, dcbg_ref, dcbv_ref,
             dwin, d2win, dawin, accu, accd):
        t = pl.program_id(1)
        first, last = t == 0, t == nt - 1
        zero = jnp.zeros((HALO, D), bf16)
        dwin[0:HALO, :] = jnp.where(first, zero, dp[...])
        dwin[HALO:HALO + tt, :] = dc[...]
        dwin[HALO + tt:, :] = jnp.where(last, zero, dn[...])

        @pl.when(first)
        def _():
            for r in (accu, accd, dcwg_ref, dcwv_ref, dcbg_ref, dcbv_ref):
                r[...] = jnp.zeros_like(r)

        cw = jnp.concatenate([cwg_ref[...], cwv_ref[...]], axis=1)
        split = half + 2 * HALO
        for r0, r1 in ((0, split), (split, wn)):
            dawin[r0:r1, :] = _dg(dwin[r0:r1, :], wd_ref[...], NT)

        def grads(u2v, dact):
            gate, val = u2v[:, :ct], u2v[:, ct:]
            sg = _sigmoid(gate)
            silu = gate * sg
            return dact * val * (sg * (1.0 + gate * (1.0 - sg))), dact * silu, silu * val

        for blk, r0 in ((u2p, 0), (u2n, HALO + tt)):
            dgate, dval, _ = grads(blk[...], dawin[r0:r0 + HALO, :])
            d2win[r0:r0 + HALO, :ct] = dgate
            d2win[r0:r0 + HALO, ct:] = dval
        for p in range(2):
            b0 = HALO + p * half
            rows = slice(p * half, (p + 1) * half)
            dgate, dval, act = grads(u2c[rows, :], dawin[b0:b0 + half, :])
            d2win[b0:b0 + half, :ct] = dgate
            d2win[b0:b0 + half, ct:] = dval
            accd[...] += _dg(act.astype(bf16), dc[rows, :], TN)
        for p in range(2):
            b0 = HALO + p * half
            rows = slice(p * half, (p + 1) * half)
            shifted = (d2win[pl.ds(b0 + 1, half), :], d2win[pl.ds(b0, half), :], d2win[pl.ds(b0 - 1, half), :])
            ucur = uc_ref[rows, :].astype(f32)
            dcb = jnp.sum(shifted[1], axis=0, keepdims=True)
            dcbg_ref[...] += dcb[:, :ct]
            dcbv_ref[...] += dcb[:, ct:]
            for kk in range(3):
                dck = jnp.sum(ucur * shifted[kk], axis=0, keepdims=True)
                dcwg_ref[kk:kk + 1, :] += dck[:, :ct]
                dcwv_ref[kk:kk + 1, :] += dck[:, ct:]
            du = (shifted[0] * cw[0:1, :] + shifted[1] * cw[1:2, :] + shifted[2] * cw[2:3, :]).astype(bf16)
            dug_ref[rows, :] = du[:, :ct]
            duv_ref[rows, :] = du[:, ct:]
            accu[...] += _dg(du, hc[rows, :], TN)

        @pl.when(last)
        def _():
            dwu_ref[0] = accu[0:ct, :].astype(bf16)
            dwu_ref[1] = accu[ct:, :].astype(bf16)
            dwd_ref[...] = accd[...].astype(bf16)

    per = tt // HALO
    prev = lambda w: (lambda c, t: (jnp.maximum(t * per - 1, 0), c if w else 0))
    nxt = lambda w: (lambda c, t: (jnp.minimum((t + 1) * per, T // HALO - 1), c if w else 0))
    tile = lambda: pl.BlockSpec((ct, D), lambda c, t: (c, 0))
    lane = lambda r, off: pl.BlockSpec((r, ct), lambda c, t: (0, c + off))
    return pl.pallas_call(
        body, name="ffn_bwd", grid=(FFN_NC, nt),
        in_specs=[pl.BlockSpec((HALO, D), prev(False)), pl.BlockSpec((tt, D), lambda c, t: (t, 0)), pl.BlockSpec((HALO, D), nxt(False)),
                  pl.BlockSpec((tt, D), lambda c, t: (t, 0)), pl.BlockSpec((tt, 2 * ct), lambda c, t: (t, c)),
                  pl.BlockSpec((HALO, 2 * ct), prev(True)), pl.BlockSpec((tt, 2 * ct), lambda c, t: (t, c)),
                  pl.BlockSpec((HALO, 2 * ct), nxt(True)), lane(3, 0), lane(3, FFN_NC), tile()],
        out_specs=[pl.BlockSpec((tt, ct), lambda c, t: (t, c)), pl.BlockSpec((tt, ct), lambda c, t: (t, c)),
                   pl.BlockSpec((2, ct, D), lambda c, t: (0, c, 0)), tile(), lane(3, 0), lane(3, 0), lane(1, 0), lane(1, 0)],
        out_shape=[jax.ShapeDtypeStruct((T, F), bf16), jax.ShapeDtypeStruct((T, F), bf16),
                   jax.ShapeDtypeStruct((2, F, D), bf16), jax.ShapeDtypeStruct((F, D), bf16),
                   jax.ShapeDtypeStruct((3, F), f32), jax.ShapeDtypeStruct((3, F), f32),
                   jax.ShapeDtypeStruct((1, F), f32), jax.ShapeDtypeStruct((1, F), f32)],
        scratch_shapes=[pltpu.VMEM((wn, D), bf16), pltpu.VMEM((wn, 2 * ct), f32), pltpu.VMEM((wn, ct), f32),
                        pltpu.VMEM((2 * ct, D), f32), pltpu.VMEM((ct, D), f32)],
        compiler_params=_cp("parallel", "arbitrary"),
    )(dffn, dffn, dffn, h2, u_t, u2_t, u2_t, u2_t, fcw, fcw, w_down)


def _norm_bwd(dh, xv, gain, sh_sc, rstd):
    xh = xv * rstd
    n = xh * gain
    dn = dh * (1.0 + sh_sc)
    dxh = dn * gain
    dx = rstd * (dxh - xh * jnp.mean(dxh * xh, axis=-1, keepdims=True))
    return (dx, jnp.sum(dh, axis=0, keepdims=True), jnp.sum(dh * n, axis=0, keepdims=True),
            jnp.sum(dn * xh, axis=0, keepdims=True))


def _norm2_bwd(dug, duv, w_upT, x1, dx2, proj, w_out, y_na, y_cv, modv, g2):
    tm = 256
    nt = T // tm

    def body(dug_ref, duv_ref, w_ref, x1_ref, dx2_ref, pj_ref, wo_ref, ya_ref, yc_ref, mod_ref, g_ref,
             dx1_ref, dya_ref, dyc_ref, dwo_ref, s_ref, acc):
        i = pl.program_id(0)

        @pl.when(i == 0)
        def _():
            s_ref[...] = jnp.zeros_like(s_ref)
            acc[...] = jnp.zeros_like(acc)

        dh2 = _dot(dug_ref[...], w_ref[0:F, :]) + _dot(duv_ref[...], w_ref[F:F2, :])
        x1 = x1_ref[...]
        rstd = lax.rsqrt(jnp.mean(x1 * x1, axis=-1, keepdims=True) + EPS)
        dxn, dsh, dsc, dgn = _norm_bwd(dh2, x1, g_ref[...], mod_ref[4:5, :], rstd)
        dx1 = dx2_ref[...] + dxn
        dx1_ref[...] = dx1
        dpj = (dx1 * mod_ref[2:3, :]).astype(bf16)
        dyc = _dg(dpj, wo_ref[...], NT)
        dya_ref[...] = dyc[:, :DA].astype(bf16)
        dyc_ref[...] = dyc[:, DA:]
        acc[0:DA, :] += _dg(ya_ref[...], dpj, TN)
        acc[DA:D, :] += _dg(yc_ref[...], dpj, TN)

        @pl.when(i == nt - 1)
        def _():
            dwo_ref[...] = acc[...].astype(bf16)

        s_ref[0:1, :] += dsh
        s_ref[1:2, :] += dsc
        s_ref[2:3, :] += dgn
        s_ref[3:4, :] += jnp.sum(dx1 * pj_ref[...].astype(f32), axis=0, keepdims=True)

    row = lambda w: pl.BlockSpec((tm, w), lambda i: (i, 0))
    return pl.pallas_call(
        body, name="norm2_bwd", grid=(nt,),
        in_specs=[row(F), row(F), _full((F2, D)), row(D), row(D), row(D), _full((D, D)), row(DA), row(DA),
                  _full((8, D)), _full((1, D))],
        out_specs=[row(D), row(DA), row(DA), _full((D, D)), _full((8, D))],
        out_shape=[jax.ShapeDtypeStruct((T, D), f32), jax.ShapeDtypeStruct((T, DA), bf16), jax.ShapeDtypeStruct((T, DA), f32),
                   jax.ShapeDtypeStruct((D, D), bf16), jax.ShapeDtypeStruct((8, D), f32)],
        scratch_shapes=[pltpu.VMEM((D, D), f32)],
        compiler_params=_cp("arbitrary"),
    )(dug, duv, w_upT, x1, dx2, proj, w_out, y_na, y_cv, modv, g2)


def _conf_bwd(a, g, cv, dy, conv_w, ln_g, ln_b, blocks):
    tt = CONV_TT
    nt = T // tt
    sub = 32
    wn = tt + 2 * HALO
    nb = len(blocks)

    def body(ap, ac, an, gp, gc, gn, cp_, cc, cn, dp, dc, dn, w_ref, lg_ref, lb_ref, *rest):
        g_refs, (da_ref, dg_ref, dcw_ref, s_ref) = rest[:nb], rest[nb:nb + 4]
        recv_refs, (urot, drot, wacc), a2a_sems = rest[nb + 4:2 * nb + 4], rest[2 * nb + 4:2 * nb + 7], rest[2 * nb + 7:]
        i = pl.program_id(0)
        first, last = i == 0, i == nt - 1
        plans = [_a2a_plan(g_refs[k], recv_refs[k], *a2a_sems[3 * k:3 * k + 3]) for k in range(nb)]
        for start, _ in plans:
            pl.when(first)(start)

        @pl.when(first)
        def _():
            s_ref[...] = jnp.zeros_like(s_ref)
            wacc[...] = jnp.zeros_like(wacc)

        lg, lb = lg_ref[...], lb_ref[...]

        def ln_bwd(cvv, dyv):
            mu = jnp.mean(cvv, axis=-1, keepdims=True)
            xc = cvv - mu
            rstd = lax.rsqrt(jnp.mean(xc * xc, axis=-1, keepdims=True) + EPS)
            yn = xc * rstd
            z = yn * lg + lb
            sz = _sigmoid(z)
            dz = dyv * (sz * (1.0 + z * (1.0 - sz)))
            dyn = dz * lg
            dcv = rstd * (dyn - jnp.mean(dyn, axis=-1, keepdims=True) - yn * jnp.mean(dyn * yn, axis=-1, keepdims=True))
            return dcv, dz, yn

        urot[0, 0:HALO, :] = jnp.where(first, 0.0, ap[...] * _sigmoid(gp[...]))
        urot[0, HALO + tt:, :] = jnp.where(last, 0.0, an[...] * _sigmoid(gn[...]))
        drot[0, 0:HALO, :] = jnp.where(first, 0.0, ln_bwd(cp_[...], dp[...])[0])
        drot[0, HALO + tt:, :] = jnp.where(last, 0.0, ln_bwd(cn[...], dn[...])[0])
        for s in range(tt // sub):
            rr = pl.ds(s * sub, sub)
            urot[0, pl.ds(HALO + s * sub, sub), :] = ac[rr, :] * _sigmoid(gc[rr, :])
            dcv, dz, yn = ln_bwd(cc[rr, :], dc[rr, :])
            drot[0, pl.ds(HALO + s * sub, sub), :] = dcv
            s_ref[0:1, :] += jnp.sum(dcv, axis=0, keepdims=True)
            s_ref[1:2, :] += jnp.sum(dz * yn, axis=0, keepdims=True)
            s_ref[2:3, :] += jnp.sum(dz, axis=0, keepdims=True)
        _shifted_copies(urot, wn)
        _shifted_copies(drot, wn)
        w = w_ref[...]
        for s in range(tt // sub):
            rr = pl.ds(s * sub, sub)
            dcv = drot[0, pl.ds(HALO + s * sub, sub), :]
            acc = jnp.zeros((sub, DA), f32)
            for j in range(CW):
                ad, bd = divmod(2 * HALO - 1 - j, 8)
                au, bu = divmod(1 + j, 8)
                acc = acc + drot[bd, pl.ds(s * sub + 8 * ad, sub), :] * w[j:j + 1, :]
                part = urot[bu, pl.ds(s * sub + 8 * au, sub), :] * dcv
                wacc[j] += part[0:8] + part[8:16] + part[16:24] + part[24:32]
            av, gv = ac[rr, :], gc[rr, :]
            sg = _sigmoid(gv)
            da_ref[rr, :] = (acc * sg).astype(bf16)
            dg_ref[rr, :] = (acc * av * sg * (1.0 - sg)).astype(bf16)

        @pl.when(last)
        def _():
            for j in range(CW):
                dcw_ref[j:j + 1, :] = jnp.sum(wacc[j], axis=0, keepdims=True)
            dcw_ref[CW:CW + 1, :] = jnp.zeros((1, DA), f32)

        for _, finish in plans:
            pl.when(last)(finish)

    hs = _halo_specs(tt, DA, T // HALO)
    hbm = pl.BlockSpec(memory_space=pl.ANY)
    return pl.pallas_call(
        body, name="conf_bwd", grid=(nt,),
        in_specs=hs * 4 + [_full((CW, DA)), _full((1, DA)), _full((1, DA))] + [hbm] * nb,
        out_specs=[pl.BlockSpec((tt, DA), lambda i: (i, 0)), pl.BlockSpec((tt, DA), lambda i: (i, 0)),
                   _full((CW + 1, DA)), _full((8, DA))] + [hbm] * nb,
        out_shape=[jax.ShapeDtypeStruct((T, DA), bf16), jax.ShapeDtypeStruct((T, DA), bf16),
                   jax.ShapeDtypeStruct((CW + 1, DA), f32), jax.ShapeDtypeStruct((8, DA), f32)]
                  + [jax.ShapeDtypeStruct(b.shape, b.dtype) for b in blocks],
        scratch_shapes=[pltpu.VMEM((8, wn, DA), f32), pltpu.VMEM((8, wn, DA), f32), pltpu.VMEM((CW, 8, DA), f32)]
                       + _a2a_scratch() * nb,
        compiler_params=_cp("arbitrary"),
    )(a, a, a, g, g, g, cv, cv, cv, dy, dy, dy, conv_w, ln_g, ln_b, *blocks)


def _attn_bwd(q, k, v, y, dy, lse, bias_tab, blocks):
    zr = 256
    nb = len(blocks)

    def body(q_ref, k_ref, v_ref, y_ref, dy_ref, lse_ref, b_ref, *rest):
        g_refs, (dq_ref, dk_hbm, dv_hbm, db_ref) = rest[:nb], rest[nb:nb + 4]
        recv_refs, (dk_s, dv_s, sem), a2a_sems = rest[nb + 4:2 * nb + 4], rest[2 * nb + 4:2 * nb + 7], rest[2 * nb + 7:]
        r = pl.program_id(0)
        plans = [_a2a_plan(g_refs[i], recv_refs[i], *a2a_sems[3 * i:3 * i + 3]) for i in range(nb)]
        for start, _ in plans:
            pl.when(r == 0)(start)

        @pl.when(r == 0)
        def _():
            def z(i, _):
                rr = pl.ds(pl.multiple_of(i * zr, zr), zr)
                dk_s[rr, :] = jnp.zeros((zr, DA), f32)
                dv_s[rr, :] = jnp.zeros((zr, DA), f32)
                return 0
            lax.fori_loop(0, TA // zr, z, 0)

        @pl.when((r <= WR // 2) | (r > GW - WR // 2))
        def _():
            db_ref[...] = jnp.zeros_like(db_ref)

        ks = pl.multiple_of(_win_start(r) * GW, GW)
        win = pl.ds(ks, WR * GW)
        qq, yy, dyy, lse_v = q_ref[...], y_ref[...], dy_ref[...], lse_ref[...]
        lo = lax.broadcasted_iota(jnp.int32, (GW, 2 * HD), 1) < HD
        ops, heads = [], []
        for pr in range(NH // 2):
            ps = slice(pr * 2 * HD, (pr + 1) * 2 * HD)
            q2, do2 = qq[:, ps], dyy[:, ps]
            prod = do2.astype(f32) * yy[:, ps].astype(f32)
            kw, vw = k_ref[win, ps], v_ref[win, ps]
            kc, vc = k_ref[T:TA, ps], v_ref[T:TA, ps]
            ops.append((kw, kc))
            for s in range(2):
                h = 2 * pr + s
                msk = lo if s == 0 else ~lo
                qm = jnp.where(msk, q2, jnp.zeros_like(q2))
                dom = jnp.where(msk, do2, jnp.zeros_like(do2))
                delta = jnp.sum(jnp.where(msk, prod, 0.0), axis=-1, keepdims=True)
                heads.append((qm, dom, delta, _dg(qm, kw, NT), _dg(qm, kc, NT), _dg(dom, vw, NT), _dg(dom, vc, NT)))
        grads = []
        for h, (qm, dom, delta, sl, sc, dpl, dpc) in enumerate(heads):
            lh = lse_v[:, h:h + 1]
            pl_ = jnp.exp(sl * SCALE + b_ref[0, h] - lh)
            pc = jnp.exp(sc * SCALE - lh)
            dsl = pl_ * (dpl - delta)
            dsc = pc * (dpc - delta)
            db_ref[0, h] += dsl
            grads.append((qm, dom, pl_.astype(bf16), pc.astype(bf16), dsl.astype(bf16), dsc.astype(bf16)))
        for pr in range(NH // 2):
            ps = slice(pr * 2 * HD, (pr + 1) * 2 * HD)
            kw, kc = ops[pr]
            dqs = []
            dkw = dvw = dkc = dvc = None
            for s in range(2):
                qm, dom, plb, pcb, dslb, dscb = grads[2 * pr + s]
                dqs.append(_dot(dslb, kw) + _dot(dscb, kc))
                parts = (_dg(dslb, qm, TN), _dg(plb, dom, TN), _dg(dscb, qm, TN), _dg(pcb, dom, TN))
                if s == 0:
                    dkw, dvw, dkc, dvc = parts
                else:
                    dkw, dvw, dkc, dvc = dkw + parts[0], dvw + parts[1], dkc + parts[2], dvc + parts[3]
            dq_ref[:, ps] = (jnp.where(lo, dqs[0], dqs[1]) * SCALE).astype(bf16)
            dk_s[win, ps] += dkw * SCALE
            dv_s[win, ps] += dvw
            dk_s[T:TA, ps] += dkc * SCALE
            dv_s[T:TA, ps] += dvc

        @pl.when(r == GW - 1)
        def _():
            c1 = pltpu.make_async_copy(dk_s, dk_hbm, sem.at[0])
            c2 = pltpu.make_async_copy(dv_s, dv_hbm, sem.at[1])
            c1.start()
            c2.start()
            c1.wait()
            c2.wait()

        for _, finish in plans:
            pl.when(r == GW - 1)(finish)

    rowq = lambda: pl.BlockSpec((GW, DA), lambda r: (r, 0))
    hbm = pl.BlockSpec(memory_space=pl.ANY)
    return pl.pallas_call(
        body, name="attn_bwd", grid=(GW,),
        in_specs=[rowq(), _full((TA, DA)), _full((TA, DA)), rowq(), rowq(), pl.BlockSpec((GW, NH), lambda r: (r, 0)),
                  pl.BlockSpec((1, NH, GW, WR * GW), lambda r: (_pattern(r), 0, 0, 0))] + [hbm] * nb,
        out_specs=[rowq(), hbm, hbm, pl.BlockSpec((1, NH, GW, WR * GW), lambda r: (_pattern(r), 0, 0, 0))] + [hbm] * nb,
        out_shape=[jax.ShapeDtypeStruct((T, DA), bf16), jax.ShapeDtypeStruct((TA, DA), f32), jax.ShapeDtypeStruct((TA, DA), f32),
                   jax.ShapeDtypeStruct((8, NH, GW, WR * GW), f32)] + [jax.ShapeDtypeStruct(b.shape, b.dtype) for b in blocks],
        scratch_shapes=[pltpu.VMEM((TA, DA), f32), pltpu.VMEM((TA, DA), f32), pltpu.SemaphoreType.DMA((2,))] + _a2a_scratch() * nb,
        compiler_params=_cp("arbitrary"),
    )(q, k, v, y, dy, lse, bias_tab, *blocks)


def _sum_blocks(recv, name):
    _, r, n = recv.shape
    tr = next(cand for cand in (176, 128, 64, 32, 16) if r % cand == 0)

    def body(a_ref, o_ref):
        acc = a_ref[0].astype(f32)
        for d in range(1, NDEV):
            acc = acc + a_ref[d].astype(f32)
        o_ref[...] = acc

    return pl.pallas_call(
        body, name=name, grid=(r // tr,),
        in_specs=[pl.BlockSpec((NDEV, tr, n), lambda i: (0, i, 0))],
        out_specs=pl.BlockSpec((tr, n), lambda i: (i, 0)),
        out_shape=jax.ShapeDtypeStruct((r, n), f32),
        compiler_params=_cp("parallel"),
    )(recv)


def _rpb_reduce(dbias):
    rev = np.zeros((WR * GW, WR * GW), np.float32)
    for i in range(WR):
        for kk in range(GW):
            rev[i * GW + kk, i * GW + GW - 1 - kk] = 1.0

    def body(d_ref, rev_ref, o_ref):
        rv = rev_ref[...]
        for h in range(NH):
            dv = d_ref[0, h]
            r0 = dv.astype(bf16)
            e1 = dv - r0.astype(f32)
            r1 = e1.astype(bf16)
            r2 = (e1 - r1.astype(f32)).astype(bf16)
            rr = _dot(r0, rv) + _dot(r1, rv) + _dot(r2, rv)
            skew = pltpu.roll(rr, 0, 1, stride=1, stride_axis=0)
            o_ref[0, h:h + 1, :] = jnp.sum(skew, axis=0, keepdims=True)

    return pl.pallas_call(
        body, name="rpb_reduce", grid=(8,),
        in_specs=[pl.BlockSpec((1, NH, GW, WR * GW), lambda p: (p, 0, 0, 0)), _full((WR * GW, WR * GW))],
        out_specs=pl.BlockSpec((1, NH, WR * GW), lambda p: (p, 0, 0)),
        out_shape=jax.ShapeDtypeStruct((8, NH, WR * GW), f32),
        compiler_params=_cp("parallel"),
    )(dbias, jnp.asarray(rev, dtype=bf16))


def _norm1_bwd(dq, dk, dv, da, dg, w_inT, x0, ctx0, h, dx1, modv, g1):
    tm = 256
    nt = TA // tm
    nx = T // tm

    def body(dq_ref, dk_ref, dv_ref, da_ref, dg_ref, w_ref, x_ref, c_ref, h_ref, dx1_ref, mod_ref, g_ref,
             dx_ref, dwo_ref, s_ref, dw_ref):
        i = pl.program_id(0)
        is_ctx = i == nt - 1

        @pl.when(i == 0)
        def _():
            s_ref[...] = jnp.zeros_like(s_ref)
            dw_ref[...] = jnp.zeros_like(dw_ref)

        hb = h_ref[...]
        dkb, dvb = dk_ref[...].astype(bf16), dv_ref[...].astype(bf16)
        dw_ref[DA:2 * DA, :] += _dg(dkb, hb, TN)
        dw_ref[2 * DA:3 * DA, :] += _dg(dvb, hb, TN)
        dh_kv = _dot(dkb, w_ref[DA:2 * DA, :]) + _dot(dvb, w_ref[2 * DA:3 * DA, :])
        gain = g_ref[...]

        @pl.when(is_ctx)
        def _():
            xv = c_ref[...]
            rstd = lax.rsqrt(jnp.mean(xv * xv, axis=-1, keepdims=True) + EPS)
            _, dsh, dsc, dgn = _norm_bwd(dh_kv, xv, gain, mod_ref[7:8, :], rstd)
            s_ref[2:3, :] += dgn
            s_ref[3:4, :] += dsh
            s_ref[4:5, :] += dsc
            dwo_ref[...] = dw_ref[...].astype(bf16)

        @pl.when(jnp.logical_not(is_ctx))
        def _():
            dqb, dab, dgb = dq_ref[...], da_ref[...], dg_ref[...]
            dw_ref[0:DA, :] += _dg(dqb, hb, TN)
            dw_ref[3 * DA:4 * DA, :] += _dg(dab, hb, TN)
            dw_ref[4 * DA:5 * DA, :] += _dg(dgb, hb, TN)
            dh = (dh_kv + _dot(dqb, w_ref[0:DA, :]) + _dot(dab, w_ref[3 * DA:4 * DA, :])
                  + _dot(dgb, w_ref[4 * DA:5 * DA, :]))
            xv = x_ref[...]
            rstd = lax.rsqrt(jnp.mean(xv * xv, axis=-1, keepdims=True) + EPS)
            dxn, dsh, dsc, dgn = _norm_bwd(dh, xv, gain, mod_ref[1:2, :], rstd)
            dx_ref[...] = dx1_ref[...] + dxn
            s_ref[0:1, :] += dsh
            s_ref[1:2, :] += dsc
            s_ref[2:3, :] += dgn

    row = lambda w: pl.BlockSpec((tm, w), lambda i: (i, 0))
    lrow = lambda w: pl.BlockSpec((tm, w), lambda i: (jnp.minimum(i, nx - 1), 0))
    return pl.pallas_call(
        body, name="norm1_bwd", grid=(nt,),
        in_specs=[lrow(DA), row(DA), row(DA), lrow(DA), lrow(DA), _full((5 * DA, D)), lrow(D), _full((TC, D)), row(D),
                  lrow(D), _full((8, D)), _full((1, D))],
        out_specs=[lrow(D), _full((5 * DA, D)), _full((8, D))],
        out_shape=[jax.ShapeDtypeStruct((T, D), f32), jax.ShapeDtypeStruct((5 * DA, D), bf16), jax.ShapeDtypeStruct((8, D), f32)],
        scratch_shapes=[pltpu.VMEM((5 * DA, D), f32)],
        compiler_params=_cp("arbitrary"),
    )(dq, dk, dv, da, dg, w_inT, x0, ctx0, h, dx1, modv, g1)


def _adamw(w, g, m, v, name):
    r, c = w.shape
    tr = r
    for cand in (256, 128, 64, 32, 16, 8):
        if r % cand == 0 and r > cand:
            tr = cand
            break

    def body(w_ref, g_ref, m_ref, v_ref, d_ref, nm_ref, nv_ref):
        gv = g_ref[...]
        nm = ADAM_B1 * m_ref[...] + (1.0 - ADAM_B1) * gv
        nv = ADAM_B2 * v_ref[...] + (1.0 - ADAM_B2) * (gv * gv)
        m_hat = nm / (1.0 - ADAM_B1 ** ADAM_STEP)
        v_hat = nv / (1.0 - ADAM_B2 ** ADAM_STEP)
        d_ref[...] = -ADAM_LR * (m_hat / (jnp.sqrt(v_hat) + ADAM_EPS) + ADAM_WD * w_ref[...])
        nm_ref[...] = nm
        nv_ref[...] = nv

    spec = pl.BlockSpec((tr, c), lambda i: (i, 0))
    return pl.pallas_call(
        body, name=name, grid=(r // tr,),
        in_specs=[spec] * 4, out_specs=[spec] * 3,
        out_shape=[jax.ShapeDtypeStruct((r, c), f32)] * 3,
        compiler_params=_cp("parallel"),
    )(w, g, m, v)


def _pad_rows128(vec):
    n = vec.shape[0]
    rows = -(-n // 1024) * 8
    return jnp.pad(vec, (0, rows * 128 - n)).reshape(rows, 128)


def _grad_rpb(dbias):
    lane_map, r_hot = _rpb_tables()
    return jnp.einsum("phl,lic,pir->hrc", _rpb_reduce(dbias), jnp.asarray(lane_map), jnp.asarray(r_hot),
                      precision=lax.Precision.HIGHEST)


def kernel(x, c, ctx, c_ctx, w_mod, b_mod, g_norm1, w_in, rpb, conv_w, conv_b, ln_g, ln_b, w_out, g_norm2, w_up, ffn_conv_w, ffn_conv_b, w_down, g_final, loss_target, m_c_ctx, m_w_mod, m_b_mod, m_g_norm1, m_w_in, m_rpb, m_conv_w, m_conv_b, m_ln_g, m_ln_b, m_w_out, m_g_norm2, m_w_up, m_ffn_conv_w, m_ffn_conv_b, m_w_down, m_g_final, v_c_ctx, v_w_mod, v_b_mod, v_g_norm1, v_w_in, v_rpb, v_conv_w, v_conv_b, v_ln_g, v_ln_b, v_w_out, v_g_norm2, v_w_up, v_ffn_conv_w, v_ffn_conv_b, v_w_down, v_g_final):
    me = 4 * lax.axis_index("x") + 2 * lax.axis_index("y") + lax.axis_index("c")
    nmod = w_mod.shape[2]
    n_in = w_in.shape[2]
    n_out = w_out.shape[1]
    n_up = w_up.shape[2]
    n_dn = w_down.shape[1]
    n_cw = conv_w.shape[2]

    (w_inT,) = _allgather2([w_in[0].T.astype(bf16)], "ag_w_in")

    c_all = _small_allgather(c.reshape(8, 128), "ag_c").reshape(NDEV, D)
    cvec = jnp.concatenate([c_all, c_ctx[None, :], jnp.zeros((7, D), f32)], axis=0)
    b_sh = lax.dynamic_slice(b_mod, (0, me * nmod), (1, nmod))
    mod_sh = _mod_fwd(cvec, w_mod[0], b_sh)
    n_modp = 16 * nmod
    payload = jnp.concatenate([mod_sh.reshape(-1), conv_w[0].reshape(-1), ffn_conv_w[0].reshape(-1)])
    flat = _small_allgather(_pad_rows128(payload), "ag_mod").reshape(NDEV, -1)
    mod_all = flat[:, :n_modp].reshape(NDEV, 16, nmod).transpose(1, 0, 2).reshape(16, 6 * D)
    mod_me = lax.dynamic_index_in_dim(mod_all, me, 0, keepdims=False).reshape(6, D)
    mod_c = mod_all[8]
    modv = jnp.concatenate([mod_me, mod_c[None, 0:D], mod_c[None, D:2 * D]], axis=0)
    o1 = n_modp + CW * n_cw
    conv_w_f = flat[:, n_modp:o1].reshape(NDEV, CW, n_cw).transpose(1, 0, 2).reshape(CW, DA)
    fcw_f = flat[:, o1:o1 + 3 * n_up].reshape(NDEV, 3, n_up).transpose(1, 0, 2).reshape(3, F2)

    x0, ctx0 = x[0], ctx[0]
    h, q, k, v, a, g = _in_proj(x0, ctx0, g_norm1, modv, w_inT)
    bias_tab = _bias_table(rpb[0])
    y_cv, cv, w_down_f = _conf_fwd(a, g, conv_w_f, conv_b, ln_g, ln_b, [w_down[0].astype(bf16)])
    y_na, lse, w_upT, w_out_f = _attn_fwd(q, k, v, bias_tab, [w_up[0].T.astype(bf16), w_out[0].astype(bf16)])
    x1, proj, h2 = _out_proj(x0, y_na, y_cv, w_out_f, modv, g_norm2)
    ffn, u_t, u2_t = _ffn_fwd(h2, w_upT, fcw_f, ffn_conv_b, w_down_f)
    dx2, dffn, s_loss = _loss_bwd(ffn, x1, loss_target[0], modv, g_final[None, :])

    dug, duv, dw_up, dw_down, dcwg, dcwv, dcbg, dcbv = _ffn_bwd(h2, dffn, u_t, u2_t, fcw_f, w_down_f)
    dx1, dy_na, dy_cv, dw_out, s_n2 = _norm2_bwd(dug, duv, w_upT, x1, dx2, proj, w_out_f, y_na, y_cv, modv, g_norm2)
    da, dg, dcw, s_cf, rv_down, rv_out = _conf_bwd(a, g, cv, dy_cv, conv_w_f, ln_g, ln_b,
                                                   [dw_down.reshape(NDEV, n_dn, D), dw_out.reshape(NDEV, n_out, D)])
    dq, dk, dv, dbias, rv_up = _attn_bwd(q, k, v, y_na, dy_na, lse, bias_tab, [dw_up.reshape(NDEV, n_up, D)])
    g_w_down = _sum_blocks(rv_down, "sum_w_down")
    g_w_out = _sum_blocks(rv_out, "sum_w_out")
    g_w_up = _sum_blocks(rv_up, "sum_w_up").T
    grad_rpb_part = _grad_rpb(dbias)
    grad_x, dw_inT, s_n1 = _norm1_bwd(dq, dk, dv, da, dg, w_inT, x0, ctx0, h, dx1, modv, g_norm1)
    grad_x = grad_x[None]
    dfcw = jnp.concatenate([dcwg, dcwv], axis=1)
    dfcb = jnp.concatenate([dcbg[0], dcbv[0]])
    small = jnp.concatenate([dcw[:CW].reshape(CW, NDEV, n_cw).transpose(1, 0, 2).reshape(NDEV, CW * n_cw),
                             dfcw.reshape(3, NDEV, n_up).transpose(1, 0, 2).reshape(NDEV, 3 * n_up)], axis=1)
    small = jnp.pad(small.reshape(NDEV, 4, D), ((0, 0), (0, 12), (0, 0))).astype(bf16)
    r_a = _reduce_scatter2(jnp.concatenate([dw_inT.reshape(NDEV, n_in, D), small], axis=1), "rs_w_in")
    g_w_in = r_a[:n_in].T
    sm = r_a[n_in:n_in + 4].reshape(-1)
    g_conv_w = sm[:CW * n_cw].reshape(CW, n_cw)
    g_fcw = sm[CW * n_cw:].reshape(3, n_up)

    dmod = jnp.concatenate([s_n1[0], s_n1[1], s_n2[3], s_n2[0], s_n2[1], s_loss[1]])
    dmodc = jnp.concatenate([s_n1[3], s_n1[4]])
    parts = [dmodc, s_n1[2], grad_rpb_part.reshape(-1), s_cf[0], s_cf[1], s_cf[2], s_n2[2], dfcb, s_loss[0], s_loss[3, 0:1]]
    sizes = [p.shape[0] for p in parts]
    pvec = _pad_rows128(jnp.concatenate([dmod] + parts))
    gath = _small_allgather(pvec, "ag_small")
    tot = _sum_rows8(gath, "sum_small").reshape(-1)
    dmod_all = gath.reshape(NDEV, -1)[:, :6 * D]
    offs = np.cumsum([6 * D] + sizes)
    pick = lambda j: tot[offs[j]:offs[j + 1]]
    dmodc_t = jnp.pad(pick(0), (0, 4 * D))
    g_b_mod = (tot[:6 * D] + dmodc_t)[None, :]
    g_g_norm1 = pick(1)[None, :]
    g_rpb = pick(2).reshape(1, NH, 2 * WR - 1, 2 * NCOL - 1)
    g_conv_b, g_ln_g, g_ln_b = pick(3)[None, :], pick(4)[None, :], pick(5)[None, :]
    g_g_norm2 = pick(6)[None, :]
    g_fcb = pick(7)[None, :]
    g_g_final = pick(8)
    loss = pick(9)[0]
    dm_rows = jnp.concatenate([dmod_all, dmodc_t[None, :], jnp.zeros((7, 6 * D), f32)], axis=0)
    dm_sh = lax.dynamic_slice(dm_rows, (0, me * nmod), (16, nmod))
    g_w_mod, gc_part = _mod_bwd(cvec, dm_sh, w_mod[0])
    gc_sum = _sum_rows8(_small_allgather(gc_part[0].reshape(8, 128), "ag_cctx"), "sum_cctx").reshape(D)
    sg_c = _sigmoid(c_ctx)
    g_c_ctx = gc_sum * (sg_c * (1.0 + c_ctx * (1.0 - sg_c)))

    big = [("w_mod", w_mod[0], g_w_mod, m_w_mod[0], v_w_mod[0]), ("w_in", w_in[0], g_w_in, m_w_in[0], v_w_in[0]),
           ("w_out", w_out[0], g_w_out, m_w_out[0], v_w_out[0]), ("w_up", w_up[0], g_w_up, m_w_up[0], v_w_up[0]),
           ("w_down", w_down[0], g_w_down, m_w_down[0], v_w_down[0])]
    upd = {n: _adamw(wv, gv, mv, vv, "adamw_" + n) for n, wv, gv, mv, vv in big}
    smalls = [("c_ctx", c_ctx, g_c_ctx, m_c_ctx, v_c_ctx), ("b_mod", b_mod, g_b_mod, m_b_mod, v_b_mod),
              ("g_norm1", g_norm1, g_g_norm1, m_g_norm1, v_g_norm1), ("rpb", rpb, g_rpb, m_rpb, v_rpb),
              ("conv_w", conv_w, g_conv_w[None], m_conv_w, v_conv_w), ("conv_b", conv_b, g_conv_b, m_conv_b, v_conv_b),
              ("ln_g", ln_g, g_ln_g, m_ln_g, v_ln_g), ("ln_b", ln_b, g_ln_b, m_ln_b, v_ln_b),
              ("g_norm2", g_norm2, g_g_norm2, m_g_norm2, v_g_norm2),
              ("ffn_conv_w", ffn_conv_w, g_fcw[None], m_ffn_conv_w, v_ffn_conv_w),
              ("ffn_conv_b", ffn_conv_b, g_fcb, m_ffn_conv_b, v_ffn_conv_b), ("g_final", g_final, g_g_final, m_g_final, v_g_final)]
    packed = [_pad_rows128(jnp.concatenate([t[j].reshape(-1) for t in smalls])) for j in (1, 2, 3, 4)]
    sd, sm_, sv = _adamw(*packed, "adamw_small")
    so = np.cumsum([0] + [int(np.prod(t[1].shape)) for t in smalls])
    for j, t in enumerate(smalls):
        shp = t[1].shape
        upd[t[0]] = tuple(arr.reshape(-1)[so[j]:so[j + 1]].reshape(shp) for arr in (sd, sm_, sv))
    grads = {"c_ctx": g_c_ctx, "w_mod": g_w_mod[None], "b_mod": g_b_mod, "g_norm1": g_g_norm1, "w_in": g_w_in[None],
             "rpb": g_rpb, "conv_w": g_conv_w[None], "conv_b": g_conv_b, "ln_g": g_ln_g, "ln_b": g_ln_b,
             "w_out": g_w_out[None], "g_norm2": g_g_norm2, "w_up": g_w_up[None], "ffn_conv_w": g_fcw[None],
             "ffn_conv_b": g_fcb, "w_down": g_w_down[None], "g_final": g_g_final}
    names = ["c_ctx", "w_mod", "b_mod", "g_norm1", "w_in", "rpb", "conv_w", "conv_b", "ln_g", "ln_b", "w_out", "g_norm2",
             "w_up", "ffn_conv_w", "ffn_conv_b", "w_down", "g_final"]
    shapes = {n: grads[n].shape for n in names}
    outs = [loss, grad_x] + [grads[n] for n in names]
    for j in range(3):
        outs += [upd[n][j].reshape(shapes[n]) for n in names]
    return tuple(outs)
```

```python
import functools

import numpy as np
import jax
import jax.numpy as jnp
from jax import lax
from jax.experimental import pallas as pl
from jax.experimental.pallas import tpu as pltpu

f32 = jnp.float32
bf16 = jnp.bfloat16

D = 1024
T = 4096
TC = 256
TA = T + TC
DA = 512
NH = 8
HD = 64
GW = 64
WR = 8
NCOL = 16
F = 2816
F2 = 2 * F
CW = 31
NDEV = 8
EPS = 1e-6
SCALE = HD ** -0.5
NEG = -1e30
MESH = pl.DeviceIdType.MESH

NT = (((1,), (1,)), ((), ()))
TN = (((0,), (0,)), ((), ()))

ADAM_LR, ADAM_B1, ADAM_B2, ADAM_EPS, ADAM_WD, ADAM_STEP = 0.001, 0.9, 0.999, 1e-08, 0.01, 10

VMEM_LIMIT = 56 * 1024 * 1024


def _cp(*sem):
    return pltpu.CompilerParams(dimension_semantics=sem or None, vmem_limit_bytes=VMEM_LIMIT)


def _dot(a, b):
    return jnp.dot(a, b, preferred_element_type=f32)


def _dg(a, b, dims):
    return lax.dot_general(a, b, dims, preferred_element_type=f32)


def _sigmoid(x):
    return 1.0 / (1.0 + jnp.exp(-x))


def _full(shape):
    n = len(shape)
    return pl.BlockSpec(shape, lambda *_: (0,) * n)


def _my_pos():
    return lax.axis_index("x"), lax.axis_index("y"), lax.axis_index("c")


def _small_allgather(v, name):
    n = v.shape[0]

    def body(v_ref, out_ref, send_sems, recv_sems):
        x, y, c = _my_pos()
        me = 4 * x + 2 * y + c
        out_ref[me] = v_ref[...]
        peers = []
        for k in range(1, NDEV):
            kx, ky, kc = (k >> 2) & 1, (k >> 1) & 1, k & 1
            peers.append((x ^ kx, y ^ ky, c ^ kc))

        def copy(k, slot, to):
            return pltpu.make_async_remote_copy(
                src_ref=v_ref, dst_ref=out_ref.at[slot], send_sem=send_sems.at[k], recv_sem=recv_sems.at[k],
                device_id=to, device_id_type=MESH)

        sends = [copy(k, me, p) for k, p in enumerate(peers)]
        for cp in sends:
            cp.start()
        for k, (px, py, pc) in enumerate(peers):
            copy(k, 4 * px + 2 * py + pc, (x, y, c)).wait_recv()
        for cp in sends:
            cp.wait_send()

    return pl.pallas_call(
        body, name=name,
        out_shape=jax.ShapeDtypeStruct((NDEV, n, 128), f32),
        in_specs=[pl.BlockSpec(memory_space=pltpu.VMEM)],
        out_specs=pl.BlockSpec(memory_space=pltpu.VMEM),
        scratch_shapes=[pltpu.SemaphoreType.DMA((NDEV - 1,)), pltpu.SemaphoreType.DMA((NDEV - 1,))],
    )(v)


def _ag2_plan(x_refs, out_refs, send_sems, recv_sems, local_sems):
    na = len(x_refs)
    x, y, c = _my_pos()
    me, sibling = (x, y, c), (x, y, 1 - c)
    chips = [(1 - x, y), (x, 1 - y), (1 - x, 1 - y)]

    def rows(i, px, py, pc):
        m_per = x_refs[i].shape[0]
        return out_refs[i].at[pl.ds(pl.multiple_of((4 * px + 2 * py + pc) * m_per, 16 if m_per % 16 == 0 else 8), m_per), :]

    def copies(k, block, to, from_shard=False):
        return [pltpu.make_async_remote_copy(
            src_ref=x_refs[i] if from_shard else rows(i, *block), dst_ref=rows(i, *block),
            send_sem=send_sems.at[k * na + i], recv_sem=recv_sems.at[k * na + i], device_id=to, device_id_type=MESH)
            for i in range(na)]

    def mine():
        return [pltpu.make_async_copy(x_refs[i], rows(i, *me), local_sems.at[i]) for i in range(na)]

    def first():
        cps = copies(0, me, sibling, True)
        for j, chip in enumerate(chips):
            cps += copies(1 + j, me, (*chip, c), True)
        return cps

    def start():
        for cp in mine() + first():
            cp.start()

    def forward():
        for j, chip in enumerate(chips):
            for cp in copies(1 + j, (*chip, c), me):
                cp.wait_recv()
            for cp in copies(4 + j, (*chip, c), sibling):
                cp.start()

    def finish():
        for cp in copies(0, sibling, me):
            cp.wait_recv()
        for j, chip in enumerate(chips):
            for cp in copies(4 + j, (*chip, 1 - c), me):
                cp.wait_recv()
        for cp in first():
            cp.wait_send()
        for j, chip in enumerate(chips):
            for cp in copies(4 + j, (*chip, c), sibling):
                cp.wait_send()
        for cp in mine():
            cp.wait()

    return start, forward, finish


def _ag2_scratch(na):
    return [pltpu.SemaphoreType.DMA((7 * na,)), pltpu.SemaphoreType.DMA((7 * na,)), pltpu.SemaphoreType.DMA((na,))]


def _a2a_plan(g_ref, recv_ref, send_sems, recv_sems, local_sem):
    x, y, c = _my_pos()
    me = 4 * x + 2 * y + c
    peers = []
    for k in range(1, NDEV):
        kx, ky, kc = (k >> 2) & 1, (k >> 1) & 1, k & 1
        peers.append((x ^ kx, y ^ ky, c ^ kc))

    def sends():
        return [pltpu.make_async_remote_copy(
            src_ref=g_ref.at[4 * px + 2 * py + pc], dst_ref=recv_ref.at[me], send_sem=send_sems.at[k], recv_sem=recv_sems.at[k],
            device_id=(px, py, pc), device_id_type=MESH) for k, (px, py, pc) in enumerate(peers)]

    def own():
        return pltpu.make_async_copy(g_ref.at[me], recv_ref.at[me], local_sem)

    def start():
        own().start()
        for cp in sends():
            cp.start()

    def finish():
        for k, (px, py, pc) in enumerate(peers):
            pltpu.make_async_remote_copy(
                src_ref=g_ref.at[me], dst_ref=recv_ref.at[4 * px + 2 * py + pc], send_sem=send_sems.at[k],
                recv_sem=recv_sems.at[k], device_id=(x, y, c), device_id_type=MESH).wait_recv()
        for cp in sends():
            cp.wait_send()
        own().wait()

    return start, finish


def _a2a_scratch():
    return [pltpu.SemaphoreType.DMA((NDEV - 1,)), pltpu.SemaphoreType.DMA((NDEV - 1,)), pltpu.SemaphoreType.DMA]


def _allgather2(shards, name):
    na = len(shards)

    def body(*refs):
        start, forward, finish = _ag2_plan(refs[:na], refs[na:2 * na], *refs[2 * na:])
        start()
        forward()
        finish()

    return pl.pallas_call(
        body, name=name,
        out_shape=[jax.ShapeDtypeStruct((NDEV * s.shape[0], s.shape[1]), s.dtype) for s in shards],
        in_specs=[pl.BlockSpec(memory_space=pltpu.VMEM)] * na,
        out_specs=[pl.BlockSpec(memory_space=pltpu.VMEM)] * na,
        scratch_shapes=_ag2_scratch(na),
        compiler_params=pltpu.CompilerParams(vmem_limit_bytes=VMEM_LIMIT),
    )(*shards)


def _reduce_scatter2(g, name):
    _, r, n = g.shape
    ch = 16
    nch = r // ch

    def body(g_ref, out_ref, a_ref, h_ref, b_ref, s1_send, s1_recv, s2_send, s2_recv):
        x, y, c = _my_pos()
        sibling = (x, y, 1 - c)
        s1 = []
        for j in range(4):
            cp = pltpu.make_async_remote_copy(
                src_ref=g_ref.at[2 * j + (1 - c)], dst_ref=a_ref.at[j], send_sem=s1_send.at[j], recv_sem=s1_recv.at[j],
                device_id=sibling, device_id_type=MESH)
            cp.start()
            s1.append(cp)
        for cp in s1:
            cp.wait_recv()

        def add1(i, _):
            rr = pl.ds(pl.multiple_of(i * ch, ch), ch)
            for j in range(4):
                h_ref[j, rr, :] = (g_ref[2 * j + c, rr, :].astype(f32) + a_ref[j, rr, :].astype(f32)).astype(bf16)
            return 0
        lax.fori_loop(0, nch, add1, 0)
        mychip = 2 * x + y
        s2 = []
        for m in range(1, 4):
            mx, my_ = (m >> 1) & 1, m & 1
            px, py = x ^ mx, y ^ my_
            cp = pltpu.make_async_remote_copy(
                src_ref=h_ref.at[2 * px + py], dst_ref=b_ref.at[m - 1], send_sem=s2_send.at[m - 1], recv_sem=s2_recv.at[m - 1],
                device_id=(px, py, c), device_id_type=MESH)
            cp.start()
            s2.append(cp)
        for cp in s2:
            cp.wait_recv()

        def add2(i, _):
            rr = pl.ds(pl.multiple_of(i * ch, ch), ch)
            acc = h_ref[mychip, rr, :].astype(f32)
            for m in range(3):
                acc = acc + b_ref[m, rr, :].astype(f32)
            out_ref[rr, :] = acc
            return 0
        lax.fori_loop(0, nch, add2, 0)
        for cp in s1 + s2:
            cp.wait_send()

    return pl.pallas_call(
        body, name=name,
        out_shape=jax.ShapeDtypeStruct((r, n), f32),
        in_specs=[pl.BlockSpec(memory_space=pltpu.VMEM)],
        out_specs=pl.BlockSpec(memory_space=pltpu.VMEM),
        scratch_shapes=[pltpu.VMEM((4, r, n), bf16), pltpu.VMEM((4, r, n), bf16), pltpu.VMEM((3, r, n), bf16),
                        pltpu.SemaphoreType.DMA((4,)), pltpu.SemaphoreType.DMA((4,)),
                        pltpu.SemaphoreType.DMA((3,)), pltpu.SemaphoreType.DMA((3,))],
        compiler_params=pltpu.CompilerParams(vmem_limit_bytes=VMEM_LIMIT),
    )(g)


def _mod_fwd(cvec, w_sh, b_sh):
    def body(c_ref, w_ref, b_ref, o_ref):
        cv = c_ref[...]
        act = (cv * _sigmoid(cv)).astype(bf16)
        o_ref[...] = _dot(act, w_ref[...].astype(bf16)) + b_ref[...]
    return pl.pallas_call(body, name="mod_fwd", out_shape=jax.ShapeDtypeStruct((16, w_sh.shape[1]), f32))(cvec, w_sh, b_sh)


def _mod_bwd(cvec, dm_sh, w_sh):
    def body(c_ref, dm_ref, w_ref, gw_ref, gc_ref):
        cv = c_ref[...]
        act = (cv * _sigmoid(cv)).astype(bf16)
        gw_ref[...] = _dg(act, dm_ref[...].astype(bf16), TN)
        gc_ref[...] = _dg(dm_ref[8:16, :].astype(bf16), w_ref[...].astype(bf16), NT)
    return pl.pallas_call(
        body, name="mod_bwd",
        out_shape=(jax.ShapeDtypeStruct(w_sh.shape, f32), jax.ShapeDtypeStruct((8, D), f32)))(cvec, dm_sh, w_sh)


def _sum_rows8(a, name):
    n = a.shape[1]

    def body(a_ref, o_ref):
        acc = a_ref[0]
        for d in range(1, NDEV):
            acc = acc + a_ref[d]
        o_ref[...] = acc
    return pl.pallas_call(body, name=name, out_shape=jax.ShapeDtypeStruct((n, 128), f32))(a)


def _in_proj(x0, ctx0, g1, modv, w_inT):
    tm = 256
    nt = TA // tm
    nx = T // tm

    def body(x_ref, c_ref, g_ref, mod_ref, w_ref, h_ref, q_ref, k_ref, v_ref, a_ref, gg_ref):
        i = pl.program_id(0)
        is_ctx = i == nt - 1
        xv = jnp.where(is_ctx, c_ref[...], x_ref[...])
        rstd = lax.rsqrt(jnp.mean(xv * xv, axis=-1, keepdims=True) + EPS)
        sh = jnp.where(is_ctx, mod_ref[6:7, :], mod_ref[0:1, :])
        sc = jnp.where(is_ctx, mod_ref[7:8, :], mod_ref[1:2, :])
        h = ((xv * rstd * g_ref[...]) * (1.0 + sc) + sh).astype(bf16)
        h_ref[...] = h
        for j, o_ref in enumerate((q_ref, k_ref, v_ref, a_ref, gg_ref)):
            o_ref[...] = _dg(h, w_ref[j * DA:(j + 1) * DA, :], NT).astype(o_ref.dtype)

    row = lambda w: pl.BlockSpec((tm, w), lambda i: (i, 0))
    return pl.pallas_call(
        body, name="in_proj", grid=(nt,),
        in_specs=[pl.BlockSpec((tm, D), lambda i: (jnp.minimum(i, nx - 1), 0)), _full((TC, D)),
                  _full((1, D)), _full((8, D)), _full((5 * DA, D))],
        out_specs=[row(D), row(DA), row(DA), row(DA), row(DA), row(DA)],
        out_shape=[jax.ShapeDtypeStruct((TA, D), bf16)] + [jax.ShapeDtypeStruct((TA, DA), bf16)] * 3
                  + [jax.ShapeDtypeStruct((TA, DA), f32)] * 2,
        compiler_params=_cp("parallel"),
    )(x0, ctx0, g1, modv, w_inT)


def _win_start(r):
    return jnp.clip(r - WR // 2, 0, GW - WR)


def _pattern(r):
    return _win_start(r) - r + (WR - 1)


def _bias_table(rpb):
    qc = np.arange(GW)[:, None]
    kc = np.arange(GW)[None, :]
    cs = np.clip(qc - NCOL // 2, 0, GW - NCOL)
    valid = np.tile(((kc >= cs) & (kc < cs + NCOL)).astype(np.int32), (1, WR))
    pad = jnp.pad(rpb, ((0, 0), (0, 0), (0, GW - (2 * NCOL - 1))))
    base = jnp.stack([pad[:, p:p + WR, :].reshape(NH, WR * GW) for p in range(8)])

    def body(base_ref, valid_ref, o_ref):
        ok = valid_ref[...] != 0
        for h in range(NH):
            row = jnp.broadcast_to(base_ref[0, h:h + 1, :], (GW, WR * GW))
            skew = pltpu.roll(row, WR * GW - (NCOL - 1), 1, stride=1, stride_axis=0)
            o_ref[0, h] = jnp.where(ok, skew, NEG)

    return pl.pallas_call(
        body, name="bias_table", grid=(8,),
        in_specs=[pl.BlockSpec((1, NH, WR * GW), lambda p: (p, 0, 0)), _full((GW, WR * GW))],
        out_specs=pl.BlockSpec((1, NH, GW, WR * GW), lambda p: (p, 0, 0, 0)),
        out_shape=jax.ShapeDtypeStruct((8, NH, GW, WR * GW), f32),
        compiler_params=_cp("parallel"),
    )(base, jnp.asarray(valid))


def _rpb_tables():
    lane_map = np.zeros((WR * GW, WR, 2 * NCOL - 1), np.float32)
    for i in range(WR):
        for t in range(GW):
            if t >= GW - NCOL:
                lane_map[i * GW + t, i, (GW - 1 - t) + NCOL - 1] = 1.0
            elif t < NCOL - 1:
                lane_map[i * GW + t, (i - 1) % WR, NCOL - 2 - t] = 1.0
    p = np.arange(8)[:, None]
    i = np.arange(WR)[None, :]
    r_hot = ((p + i)[:, :, None] == np.arange(2 * WR - 1)[None, None, :]).astype(np.float32)
    return lane_map, r_hot


AG_FORWARD_ROW = 58


def _attn_fwd(q, k, v, bias_tab, shards):
    na = len(shards)

    def body(q_ref, k_ref, v_ref, b_ref, *rest):
        x_refs, (y_ref, lse_ref), out_refs, sems = rest[:na], rest[na:na + 2], rest[na + 2:2 * na + 2], rest[2 * na + 2:]
        r = pl.program_id(0)
        if na:
            start, forward, finish = _ag2_plan(x_refs, out_refs, *sems)
            pl.when(r == 0)(start)
            pl.when(r == AG_FORWARD_ROW)(forward)
        ks = pl.multiple_of(_win_start(r) * GW, GW)
        qq = q_ref[...]
        lo = lax.broadcasted_iota(jnp.int32, (GW, 2 * HD), 1) < HD
        kv, scores = [], []
        for pr in range(NH // 2):
            ps = slice(pr * 2 * HD, (pr + 1) * 2 * HD)
            q2 = qq[:, ps]
            kw, kc = k_ref[pl.ds(ks, WR * GW), ps], k_ref[T:TA, ps]
            kv.append((v_ref[pl.ds(ks, WR * GW), ps], v_ref[T:TA, ps]))
            for s in range(2):
                qm = jnp.where(lo if s == 0 else ~lo, q2, jnp.zeros_like(q2))
                scores.append((_dg(qm, kw, NT) * SCALE + b_ref[0, 2 * pr + s], _dg(qm, kc, NT) * SCALE))
        probs = []
        for h, (sl, sc) in enumerate(scores):
            m = jnp.maximum(jnp.max(sl, axis=-1, keepdims=True), jnp.max(sc, axis=-1, keepdims=True))
            pl_ = jnp.exp(sl - m)
            pc = jnp.exp(sc - m)
            l = jnp.sum(pl_, axis=-1, keepdims=True) + jnp.sum(pc, axis=-1, keepdims=True)
            lse_ref[:, h:h + 1] = m + jnp.log(l)
            probs.append((pl_.astype(bf16), pc.astype(bf16), 1.0 / l))
        for pr in range(NH // 2):
            ps = slice(pr * 2 * HD, (pr + 1) * 2 * HD)
            vw, vc = kv[pr]
            outs = [(_dot(pb, vw) + _dot(cb, vc)) * rl for pb, cb, rl in probs[2 * pr:2 * pr + 2]]
            y_ref[:, ps] = jnp.where(lo, outs[0], outs[1]).astype(bf16)
        if na:
            pl.when(r == GW - 1)(finish)

    hbm = pl.BlockSpec(memory_space=pl.ANY)
    return pl.pallas_call(
        body, name="attn_fwd", grid=(GW,),
        in_specs=[pl.BlockSpec((GW, DA), lambda r: (r, 0)), _full((TA, DA)), _full((TA, DA)),
                  pl.BlockSpec((1, NH, GW, WR * GW), lambda r: (_pattern(r), 0, 0, 0))] + [hbm] * na,
        out_specs=[pl.BlockSpec((GW, DA), lambda r: (r, 0)), pl.BlockSpec((GW, NH), lambda r: (r, 0))] + [hbm] * na,
        out_shape=[jax.ShapeDtypeStruct((T, DA), bf16), jax.ShapeDtypeStruct((T, NH), f32)]
                  + [jax.ShapeDtypeStruct((NDEV * s.shape[0], s.shape[1]), s.dtype) for s in shards],
        scratch_shapes=_ag2_scratch(na) if na else [],
        compiler_params=_cp("arbitrary"),
    )(q, k, v, bias_tab, *shards)


CONV_TT = 256
HALO = 16


def _halo_specs(tt, w, nrows_blocks):
    per = tt // HALO
    prev = pl.BlockSpec((HALO, w), lambda i: (jnp.maximum(i * per - 1, 0), 0))
    cur = pl.BlockSpec((tt, w), lambda i: (i, 0))
    nxt = pl.BlockSpec((HALO, w), lambda i: (jnp.minimum((i + 1) * per, nrows_blocks - 1), 0))
    return [prev, cur, nxt]


def _shifted_copies(rot, wn):
    for b in range(1, 8):
        rot[b, 0:wn - 8, :] = rot[0, pl.ds(b, wn - 8), :]


def _conf_fwd(a, g, conv_w, conv_b, ln_g, ln_b, shards):
    tt = CONV_TT
    nt = T // tt
    sub = 32
    wn = tt + 2 * HALO
    na = len(shards)

    def body(ap, ac, an, gp, gc, gn, w_ref, b_ref, lg_ref, lb_ref, *rest):
        x_refs, (y_ref, cv_ref), out_refs, (rot, *sems) = rest[:na], rest[na:na + 2], rest[na + 2:2 * na + 2], rest[2 * na + 2:]
        i = pl.program_id(0)
        if na:
            start, forward, finish = _ag2_plan(x_refs, out_refs, *sems)
            pl.when(i == 0)(start)
            pl.when(i == nt // 2)(forward)
        rot[0, 0:HALO, :] = jnp.where(i > 0, ap[...] * _sigmoid(gp[...]), 0.0)
        rot[0, HALO:HALO + tt, :] = ac[...] * _sigmoid(gc[...])
        rot[0, HALO + tt:, :] = jnp.where(i < nt - 1, an[...] * _sigmoid(gn[...]), 0.0)
        _shifted_copies(rot, wn)
        w = w_ref[...]
        for s in range(tt // sub):
            acc = jnp.zeros((sub, DA), f32)
            for j in range(CW):
                a8, b8 = divmod(1 + j, 8)
                acc = acc + rot[b8, pl.ds(s * sub + 8 * a8, sub), :] * w[j:j + 1, :]
            cv = acc + b_ref[...]
            cv_ref[pl.ds(s * sub, sub), :] = cv
            mu = jnp.mean(cv, axis=-1, keepdims=True)
            xc = cv - mu
            rstd = lax.rsqrt(jnp.mean(xc * xc, axis=-1, keepdims=True) + EPS)
            z = xc * rstd * lg_ref[...] + lb_ref[...]
            y_ref[pl.ds(s * sub, sub), :] = (z * _sigmoid(z)).astype(bf16)
        if na:
            pl.when(i == nt - 1)(finish)

    hs = _halo_specs(tt, DA, T // HALO)
    hbm = pl.BlockSpec(memory_space=pl.ANY)
    return pl.pallas_call(
        body, name="conf_fwd", grid=(nt,),
        in_specs=hs + hs + [_full((CW, DA)), _full((1, DA)), _full((1, DA)), _full((1, DA))] + [hbm] * na,
        out_specs=[pl.BlockSpec((tt, DA), lambda i: (i, 0)), pl.BlockSpec((tt, DA), lambda i: (i, 0))] + [hbm] * na,
        out_shape=[jax.ShapeDtypeStruct((T, DA), bf16), jax.ShapeDtypeStruct((T, DA), f32)]
                  + [jax.ShapeDtypeStruct((NDEV * s.shape[0], s.shape[1]), s.dtype) for s in shards],
        scratch_shapes=[pltpu.VMEM((8, wn, DA), f32)] + (_ag2_scratch(na) if na else []),
        compiler_params=_cp("arbitrary"),
    )(a, a, a, g, g, g, conv_w, conv_b, ln_g, ln_b, *shards)


def _out_proj(xa, y_na, y_cv, w_out, modv, g2):
    tm = 256

    def body(x_ref, ya_ref, yc_ref, w_ref, mod_ref, g_ref, x1_ref, pj_ref, h2_ref):
        proj = _dot(ya_ref[...], w_ref[0:DA, :]) + _dot(yc_ref[...], w_ref[DA:D, :])
        x1 = x_ref[...] + mod_ref[2:3, :] * proj
        x1_ref[...] = x1
        pj_ref[...] = proj.astype(bf16)
        rstd = lax.rsqrt(jnp.mean(x1 * x1, axis=-1, keepdims=True) + EPS)
        h2_ref[...] = ((x1 * rstd * g_ref[...]) * (1.0 + mod_ref[4:5, :]) + mod_ref[3:4, :]).astype(bf16)

    row = lambda w: pl.BlockSpec((tm, w), lambda i: (i, 0))
    return pl.pallas_call(
        body, name="out_proj", grid=(T // tm,),
        in_specs=[row(D), row(DA), row(DA), _full((D, D)), _full((8, D)), _full((1, D))],
        out_specs=[row(D), row(D), row(D)],
        out_shape=[jax.ShapeDtypeStruct((T, D), f32), jax.ShapeDtypeStruct((T, D), bf16), jax.ShapeDtypeStruct((T, D), bf16)],
        compiler_params=_cp("parallel"),
    )(xa, y_na, y_cv, w_out, modv, g2)


FFN_TT = 512
FFN_CT = 256
FFN_NC = F // FFN_CT


def _ffn_specs(tt, ct, by_token_first):
    tc = (lambda f: (lambda t, c: f(t, c))) if by_token_first else (lambda f: (lambda c, t: f(t, c)))
    per = tt // HALO
    halo = [pl.BlockSpec((HALO, D), tc(lambda t, c: (jnp.maximum(t * per - 1, 0), 0))),
            pl.BlockSpec((tt, D), tc(lambda t, c: (t, 0))),
            pl.BlockSpec((HALO, D), tc(lambda t, c: (jnp.minimum((t + 1) * per, T // HALO - 1), 0)))]
    weights = [pl.BlockSpec((ct, D), tc(lambda t, c: (c, 0))), pl.BlockSpec((ct, D), tc(lambda t, c: (c + FFN_NC, 0))),
               pl.BlockSpec((3, ct), tc(lambda t, c: (0, c))), pl.BlockSpec((3, ct), tc(lambda t, c: (0, c + FFN_NC))),
               pl.BlockSpec((1, ct), tc(lambda t, c: (0, c))), pl.BlockSpec((1, ct), tc(lambda t, c: (0, c + FFN_NC))),
               pl.BlockSpec((ct, D), tc(lambda t, c: (c, 0)))]
    return halo, weights


def _ffn_fwd(h2, w_upT, fcw, fcb, w_down):
    tt, ct = FFN_TT, FFN_CT
    nt = T // tt
    wn = tt + 2 * HALO
    half = tt // 2

    def body(hp, hc, hn, wg_ref, wv_ref, cwg_ref, cwv_ref, cbg_ref, cbv_ref, wd_ref, o_ref, u_ref, u2_ref, hwin, uwin):
        t = pl.program_id(0)
        c = pl.program_id(1)

        @pl.when(c == 0)
        def _():
            hwin[0:HALO, :] = jnp.where(t > 0, hp[...], jnp.zeros_like(hp[...]))
            hwin[HALO:HALO + tt, :] = hc[...]
            hwin[HALO + tt:, :] = jnp.where(t < nt - 1, hn[...], jnp.zeros_like(hn[...]))
            o_ref[...] = jnp.zeros_like(o_ref)

        for r0, r1 in ((0, half + 2 * HALO), (half + 2 * HALO, wn)):
            hw = hwin[r0:r1, :]
            uwin[r0:r1, :ct] = _dg(hw, wg_ref[...], NT)
            uwin[r0:r1, ct:] = _dg(hw, wv_ref[...], NT)
        cw = jnp.concatenate([cwg_ref[...], cwv_ref[...]], axis=1)
        cb = jnp.concatenate([cbg_ref[...], cbv_ref[...]], axis=1)
        for p in range(2):
            base = HALO + p * half
            rows = slice(p * half, (p + 1) * half)
            uc = uwin[pl.ds(base, half), :]
            u2 = (uwin[pl.ds(base - 1, half), :] * cw[0:1, :] + uc * cw[1:2, :]
                  + uwin[pl.ds(base + 1, half), :] * cw[2:3, :] + cb)
            u_ref[rows, :] = uc.astype(bf16)
            u2_ref[rows, :] = u2
            gate = u2[:, :ct]
            val = u2[:, ct:]
            act = (gate * _sigmoid(gate) * val).astype(bf16)
            o_ref[rows, :] += _dot(act, wd_ref[...])

    halo, weights = _ffn_specs(tt, ct, True)
    pair = pl.BlockSpec((tt, 2 * ct), lambda t, c: (t, c))
    return pl.pallas_call(
        body, name="ffn_fwd", grid=(nt, FFN_NC),
        in_specs=halo + weights,
        out_specs=[pl.BlockSpec((tt, D), lambda t, c: (t, 0)), pair, pair],
        out_shape=[jax.ShapeDtypeStruct((T, D), f32), jax.ShapeDtypeStruct((T, F2), bf16), jax.ShapeDtypeStruct((T, F2), f32)],
        scratch_shapes=[pltpu.VMEM((wn, D), bf16), pltpu.VMEM((wn, 2 * ct), f32)],
        compiler_params=_cp("parallel", "arbitrary"),
    )(h2, h2, h2, w_upT, w_upT, fcw, fcw, fcb, fcb, w_down)


def _loss_bwd(ffn, x1, tgt, modv, gf):
    tm = 256
    nt = T // tm

    def body(f_ref, x1_ref, t_ref, mod_ref, g_ref, dx2_ref, df_ref, s_ref):
        i = pl.program_id(0)

        @pl.when(i == 0)
        def _():
            s_ref[...] = jnp.zeros_like(s_ref)

        ff = f_ref[...]
        gt2 = mod_ref[5:6, :]
        x2 = x1_ref[...] + gt2 * ff
        rstd = lax.rsqrt(jnp.mean(x2 * x2, axis=-1, keepdims=True) + EPS)
        xh = x2 * rstd
        gfv = g_ref[...]
        e = xh * gfv - t_ref[...]
        dy = e * (1.0 / D)
        dxh = dy * gfv
        dx2 = rstd * (dxh - xh * jnp.mean(dxh * xh, axis=-1, keepdims=True))
        dx2_ref[...] = dx2
        df_ref[...] = (dx2 * gt2).astype(bf16)
        s_ref[0:1, :] += jnp.sum(dy * xh, axis=0, keepdims=True)
        s_ref[1:2, :] += jnp.sum(dx2 * ff, axis=0, keepdims=True)
        s_ref[2:3, :] += jnp.sum(e * e, axis=0, keepdims=True)

        @pl.when(i == nt - 1)
        def _():
            tot = jnp.sum(s_ref[2:3, :], axis=-1, keepdims=True) * (0.5 / D)
            s_ref[3:4, :] = jnp.broadcast_to(tot, (1, D))

    row = lambda: pl.BlockSpec((tm, D), lambda i: (i, 0))
    return pl.pallas_call(
        body, name="loss_bwd", grid=(nt,),
        in_specs=[row(), row(), row(), _full((8, D)), _full((1, D))],
        out_specs=[row(), row(), _full((8, D))],
        out_shape=[jax.ShapeDtypeStruct((T, D), f32), jax.ShapeDtypeStruct((T, D), bf16), jax.ShapeDtypeStruct((8, D), f32)],
        compiler_params=_cp("arbitrary"),
    )(ffn, x1, tgt, modv, gf)


def _ffn_bwd(h2, dffn, u_t, u2_t, fcw, w_down):
    tt, ct = FFN_TT, FFN_CT
    nt = T // tt
    wn = tt + 2 * HALO
    half = tt // 2

    def body(dp, dc, dn, hc, uc_ref, u2p, u2c, u2n, cwg_ref, cwv_ref, wd_ref,
             dug_ref, duv_ref, dwu_ref, dwd_ref, dcwg_ref, dcwv_ref, dcbg_ref, dcbv_ref,
             dwin, d2win, dawin, accu, accd):
        t = pl.program_id(1)
        first, last = t == 0, t == nt - 1
        zero = jnp.zeros((HALO, D), bf16)
        dwin[0:HALO, :] = jnp.where(first, zero, dp[...])
        dwin[HALO:HALO + tt, :] = dc[...]
        dwin[HALO + tt:, :] = jnp.where(last, zero, dn[...])

        @pl.when(first)
        def _():
            for r in (accu, accd, dcwg_ref, dcwv_ref, dcbg_ref, dcbv_ref):
                r[...] = jnp.zeros_like(r)

        cw = jnp.concatenate([cwg_ref[...], cwv_ref[...]], axis=1)
        split = half + 2 * HALO
        for r0, r1 in ((0, split), (split, wn)):
            dawin[r0:r1, :] = _dg(dwin[r0:r1, :], wd_ref[...], NT)

        def grads(u2v, dact):
            gate, val = u2v[:, :ct], u2v[:, ct:]
            sg = _sigmoid(gate)
            silu = gate * sg
            return dact * val * (sg * (1.0 + gate * (1.0 - sg))), dact * silu, silu * val

        for blk, r0 in ((u2p, 0), (u2n, HALO + tt)):
            dgate, dval, _ = grads(blk[...], dawin[r0:r0 + HALO, :])
            d2win[r0:r0 + HALO, :ct] = dgate
            d2win[r0:r0 + HALO, ct:] = dval
        for p in range(2):
            b0 = HALO + p * half
            rows = slice(p * half, (p + 1) * half)
            dgate, dval, act = grads(u2c[rows, :], dawin[b0:b0 + half, :])
            d2win[b0:b0 + half, :ct] = dgate
            d2win[b0:b0 + half, ct:] = dval
            accd[...] += _dg(act.astype(bf16), dc[rows, :], TN)
        for p in range(2):
            b0 = HALO + p * half
            rows = slice(p * half, (p + 1) * half)
            shifted = (d2win[pl.ds(b0 + 1, half), :], d2win[pl.ds(b0, half), :], d2win[pl.ds(b0 - 1, half), :])
            ucur = uc_ref[rows, :].astype(f32)
            dcb = jnp.sum(shifted[1], axis=0, keepdims=True)
            dcbg_ref[...] += dcb[:, :ct]
            dcbv_ref[...] += dcb[:, ct:]
            for kk in range(3):
                dck = jnp.sum(ucur * shifted[kk], axis=0, keepdims=True)
                dcwg_ref[kk:kk + 1, :] += dck[:, :ct]
                dcwv_ref[kk:kk + 1, :] += dck[:, ct:]
            du = (shifted[0] * cw[0:1, :] + shifted[1] * cw[1:2, :] + shifted[2] * cw[2:3, :]).astype(bf16)
            dug_ref[rows, :] = du[:, :ct]
            duv_ref[rows, :] = du[:, ct:]
            accu[...] += _dg(du, hc[rows, :], TN)

        @pl.when(last)
        def _():
            dwu_ref[0] = accu[0:ct, :].astype(bf16)
            dwu_ref[1] = accu[ct:, :].astype(bf16)
            dwd_ref[...] = accd[...].astype(bf16)

    per = tt // HALO
    prev = lambda w: (lambda c, t: (jnp.maximum(t * per - 1, 0), c if w else 0))
    nxt = lambda w: (lambda c, t: (jnp.minimum((t + 1) * per, T // HALO - 1), c if w else 0))
    tile = lambda: pl.BlockSpec((ct, D), lambda c, t: (c, 0))
    lane = lambda r, off: pl.BlockSpec((r, ct), lambda c, t: (0, c + off))
    return pl.pallas_call(
        body, name="ffn_bwd", grid=(FFN_NC, nt),
        in_specs=[pl.BlockSpec((HALO, D), prev(False)), pl.BlockSpec((tt, D), lambda c, t: (t, 0)), pl.BlockSpec((HALO, D), nxt(False)),
                  pl.BlockSpec((tt, D), lambda c, t: (t, 0)), pl.BlockSpec((tt, 2 * ct), lambda c, t: (t, c)),
                  pl.BlockSpec((HALO, 2 * ct), prev(True)), pl.BlockSpec((tt, 2 * ct), lambda c, t: (t, c)),
                  pl.BlockSpec((HALO, 2 * ct), nxt(True)), lane(3, 0), lane(3, FFN_NC), tile()],
        out_specs=[pl.BlockSpec((tt, ct), lambda c, t: (t, c)), pl.BlockSpec((tt, ct), lambda c, t: (t, c)),
                   pl.BlockSpec((2, ct, D), lambda c, t: (0, c, 0)), tile(), lane(3, 0), lane(3, 0), lane(1, 0), lane(1, 0)],
        out_shape=[jax.ShapeDtypeStruct((T, F), bf16), jax.ShapeDtypeStruct((T, F), bf16),
                   jax.ShapeDtypeStruct((2, F, D), bf16), jax.ShapeDtypeStruct((F, D), bf16),
                   jax.ShapeDtypeStruct((3, F), f32), jax.ShapeDtypeStruct((3, F), f32),
                   jax.ShapeDtypeStruct((1, F), f32), jax.ShapeDtypeStruct((1, F), f32)],
        scratch_shapes=[pltpu.VMEM((wn, D), bf16), pltpu.VMEM((wn, 2 * ct), f32), pltpu.VMEM((wn, ct), f32),
                        pltpu.VMEM((2 * ct, D), f32), pltpu.VMEM((ct, D), f32)],
        compiler_params=_cp("parallel", "arbitrary"),
    )(dffn, dffn, dffn, h2, u_t, u2_t, u2_t, u2_t, fcw, fcw, w_down)


def _norm_bwd(dh, xv, gain, sh_sc, rstd):
    xh = xv * rstd
    n = xh * gain
    dn = dh * (1.0 + sh_sc)
    dxh = dn * gain
    dx = rstd * (dxh - xh * jnp.mean(dxh * xh, axis=-1, keepdims=True))
    return (dx, jnp.sum(dh, axis=0, keepdims=True), jnp.sum(dh * n, axis=0, keepdims=True),
            jnp.sum(dn * xh, axis=0, keepdims=True))


def _norm2_bwd(dug, duv, w_upT, x1, dx2, proj, w_out, y_na, y_cv, modv, g2):
    tm = 256
    nt = T // tm

    def body(dug_ref, duv_ref, w_ref, x1_ref, dx2_ref, pj_ref, wo_ref, ya_ref, yc_ref, mod_ref, g_ref,
             dx1_ref, dya_ref, dyc_ref, dwo_ref, s_ref, acc):
        i = pl.program_id(0)

        @pl.when(i == 0)
        def _():
            s_ref[...] = jnp.zeros_like(s_ref)
            acc[...] = jnp.zeros_like(acc)

        dh2 = _dot(dug_ref[...], w_ref[0:F, :]) + _dot(duv_ref[...], w_ref[F:F2, :])
        x1 = x1_ref[...]
        rstd = lax.rsqrt(jnp.mean(x1 * x1, axis=-1, keepdims=True) + EPS)
        dxn, dsh, dsc, dgn = _norm_bwd(dh2, x1, g_ref[...], mod_ref[4:5, :], rstd)
        dx1 = dx2_ref[...] + dxn
        dx1_ref[...] = dx1
        dpj = (dx1 * mod_ref[2:3, :]).astype(bf16)
        dyc = _dg(dpj, wo_ref[...], NT)
        dya_ref[...] = dyc[:, :DA].astype(bf16)
        dyc_ref[...] = dyc[:, DA:]
        acc[0:DA, :] += _dg(ya_ref[...], dpj, TN)
        acc[DA:D, :] += _dg(yc_ref[...], dpj, TN)

        @pl.when(i == nt - 1)
        def _():
            dwo_ref[...] = acc[...].astype(bf16)

        s_ref[0:1, :] += dsh
        s_ref[1:2, :] += dsc
        s_ref[2:3, :] += dgn
        s_ref[3:4, :] += jnp.sum(dx1 * pj_ref[...].astype(f32), axis=0, keepdims=True)

    row = lambda w: pl.BlockSpec((tm, w), lambda i: (i, 0))
    return pl.pallas_call(
        body, name="norm2_bwd", grid=(nt,),
        in_specs=[row(F), row(F), _full((F2, D)), row(D), row(D), row(D), _full((D, D)), row(DA), row(DA),
                  _full((8, D)), _full((1, D))],
        out_specs=[row(D), row(DA), row(DA), _full((D, D)), _full((8, D))],
        out_shape=[jax.ShapeDtypeStruct((T, D), f32), jax.ShapeDtypeStruct((T, DA), bf16), jax.ShapeDtypeStruct((T, DA), f32),
                   jax.ShapeDtypeStruct((D, D), bf16), jax.ShapeDtypeStruct((8, D), f32)],
        scratch_shapes=[pltpu.VMEM((D, D), f32)],
        compiler_params=_cp("arbitrary"),
    )(dug, duv, w_upT, x1, dx2, proj, w_out, y_na, y_cv, modv, g2)


def _conf_bwd(a, g, cv, dy, conv_w, ln_g, ln_b, blocks):
    tt = CONV_TT
    nt = T // tt
    sub = 32
    wn = tt + 2 * HALO
    nb = len(blocks)

    def body(ap, ac, an, gp, gc, gn, cp_, cc, cn, dp, dc, dn, w_ref, lg_ref, lb_ref, *rest):
        g_refs, (da_ref, dg_ref, dcw_ref, s_ref) = rest[:nb], rest[nb:nb + 4]
        recv_refs, (urot, drot, wacc), a2a_sems = rest[nb + 4:2 * nb + 4], rest[2 * nb + 4:2 * nb + 7], rest[2 * nb + 7:]
        i = pl.program_id(0)
        first, last = i == 0, i == nt - 1
        plans = [_a2a_plan(g_refs[k], recv_refs[k], *a2a_sems[3 * k:3 * k + 3]) for k in range(nb)]
        for start, _ in plans:
            pl.when(first)(start)

        @pl.when(first)
        def _():
            s_ref[...] = jnp.zeros_like(s_ref)
            wacc[...] = jnp.zeros_like(wacc)

        lg, lb = lg_ref[...], lb_ref[...]

        def ln_bwd(cvv, dyv):
            mu = jnp.mean(cvv, axis=-1, keepdims=True)
            xc = cvv - mu
            rstd = lax.rsqrt(jnp.mean(xc * xc, axis=-1, keepdims=True) + EPS)
            yn = xc * rstd
            z = yn * lg + lb
            sz = _sigmoid(z)
            dz = dyv * (sz * (1.0 + z * (1.0 - sz)))
            dyn = dz * lg
            dcv = rstd * (dyn - jnp.mean(dyn, axis=-1, keepdims=True) - yn * jnp.mean(dyn * yn, axis=-1, keepdims=True))
            return dcv, dz, yn

        urot[0, 0:HALO, :] = jnp.where(first, 0.0, ap[...] * _sigmoid(gp[...]))
        urot[0, HALO + tt:, :] = jnp.where(last, 0.0, an[...] * _sigmoid(gn[...]))
        drot[0, 0:HALO, :] = jnp.where(first, 0.0, ln_bwd(cp_[...], dp[...])[0])
        drot[0, HALO + tt:, :] = jnp.where(last, 0.0, ln_bwd(cn[...], dn[...])[0])
        for s in range(tt // sub):
            rr = pl.ds(s * sub, sub)
            urot[0, pl.ds(HALO + s * sub, sub), :] = ac[rr, :] * _sigmoid(gc[rr, :])
            dcv, dz, yn = ln_bwd(cc[rr, :], dc[rr, :])
            drot[0, pl.ds(HALO + s * sub, sub), :] = dcv
            s_ref[0:1, :] += jnp.sum(dcv, axis=0, keepdims=True)
            s_ref[1:2, :] += jnp.sum(dz * yn, axis=0, keepdims=True)
            s_ref[2:3, :] += jnp.sum(dz, axis=0, keepdims=True)
        _shifted_copies(urot, wn)
        _shifted_copies(drot, wn)
        w = w_ref[...]
        for s in range(tt // sub):
            rr = pl.ds(s * sub, sub)
            dcv = drot[0, pl.ds(HALO + s * sub, sub), :]
            acc = jnp.zeros((sub, DA), f32)
            for j in range(CW):
                ad, bd = divmod(2 * HALO - 1 - j, 8)
                au, bu = divmod(1 + j, 8)
                acc = acc + drot[bd, pl.ds(s * sub + 8 * ad, sub), :] * w[j:j + 1, :]
                part = urot[bu, pl.ds(s * sub + 8 * au, sub), :] * dcv
                wacc[j] += part[0:8] + part[8:16] + part[16:24] + part[24:32]
            av, gv = ac[rr, :], gc[rr, :]
            sg = _sigmoid(gv)
            da_ref[rr, :] = (acc * sg).astype(bf16)
            dg_ref[rr, :] = (acc * av * sg * (1.0 - sg)).astype(bf16)

        @pl.when(last)
        def _():
            for j in range(CW):
                dcw_ref[j:j + 1, :] = jnp.sum(wacc[j], axis=0, keepdims=True)
            dcw_ref[CW:CW + 1, :] = jnp.zeros((1, DA), f32)

        for _, finish in plans:
            pl.when(last)(finish)

    hs = _halo_specs(tt, DA, T // HALO)
    hbm = pl.BlockSpec(memory_space=pl.ANY)
    return pl.pallas_call(
        body, name="conf_bwd", grid=(nt,),
        in_specs=hs * 4 + [_full((CW, DA)), _full((1, DA)), _full((1, DA))] + [hbm] * nb,
        out_specs=[pl.BlockSpec((tt, DA), lambda i: (i, 0)), pl.BlockSpec((tt, DA), lambda i: (i, 0)),
                   _full((CW + 1, DA)), _full((8, DA))] + [hbm] * nb,
        out_shape=[jax.ShapeDtypeStruct((T, DA), bf16), jax.ShapeDtypeStruct((T, DA), bf16),
                   jax.ShapeDtypeStruct((CW + 1, DA), f32), jax.ShapeDtypeStruct((8, DA), f32)]
                  + [jax.ShapeDtypeStruct(b.shape, b.dtype) for b in blocks],
        scratch_shapes=[pltpu.VMEM((8, wn, DA), f32), pltpu.VMEM((8, wn, DA), f32), pltpu.VMEM((CW, 8, DA), f32)]
                       + _a2a_scratch() * nb,
        compiler_params=_cp("arbitrary"),
    )(a, a, a, g, g, g, cv, cv, cv, dy, dy, dy, conv_w, ln_g, ln_b, *blocks)


def _attn_bwd(q, k, v, y, dy, lse, bias_tab, blocks):
    zr = 256
    nb = len(blocks)

    def body(q_ref, k_ref, v_ref, y_ref, dy_ref, lse_ref, b_ref, *rest):
        g_refs, (dq_ref, dk_hbm, dv_hbm, db_ref) = rest[:nb], rest[nb:nb + 4]
        recv_refs, (dk_s, dv_s, sem), a2a_sems = rest[nb + 4:2 * nb + 4], rest[2 * nb + 4:2 * nb + 7], rest[2 * nb + 7:]
        r = pl.program_id(0)
        plans = [_a2a_plan(g_refs[i], recv_refs[i], *a2a_sems[3 * i:3 * i + 3]) for i in range(nb)]
        for start, _ in plans:
            pl.when(r == 0)(start)

        @pl.when(r == 0)
        def _():
            def z(i, _):
                rr = pl.ds(pl.multiple_of(i * zr, zr), zr)
                dk_s[rr, :] = jnp.zeros((zr, DA), f32)
                dv_s[rr, :] = jnp.zeros((zr, DA), f32)
                return 0
            lax.fori_loop(0, TA // zr, z, 0)

        @pl.when((r <= WR // 2) | (r > GW - WR // 2))
        def _():
            db_ref[...] = jnp.zeros_like(db_ref)

        ks = pl.multiple_of(_win_start(r) * GW, GW)
        win = pl.ds(ks, WR * GW)
        qq, yy, dyy, lse_v = q_ref[...], y_ref[...], dy_ref[...], lse_ref[...]
        lo = lax.broadcasted_iota(jnp.int32, (GW, 2 * HD), 1) < HD
        ops, heads = [], []
        for pr in range(NH // 2):
            ps = slice(pr * 2 * HD, (pr + 1) * 2 * HD)
            q2, do2 = qq[:, ps], dyy[:, ps]
            prod = do2.astype(f32) * yy[:, ps].astype(f32)
            kw, vw = k_ref[win, ps], v_ref[win, ps]
            kc, vc = k_ref[T:TA, ps], v_ref[T:TA, ps]
            ops.append((kw, kc))
            for s in range(2):
                h = 2 * pr + s
                msk = lo if s == 0 else ~lo
                qm = jnp.where(msk, q2, jnp.zeros_like(q2))
                dom = jnp.where(msk, do2, jnp.zeros_like(do2))
                delta = jnp.sum(jnp.where(msk, prod, 0.0), axis=-1, keepdims=True)
                heads.append((qm, dom, delta, _dg(qm, kw, NT), _dg(qm, kc, NT), _dg(dom, vw, NT), _dg(dom, vc, NT)))
        grads = []
        for h, (qm, dom, delta, sl, sc, dpl, dpc) in enumerate(heads):
            lh = lse_v[:, h:h + 1]
            pl_ = jnp.exp(sl * SCALE + b_ref[0, h] - lh)
            pc = jnp.exp(sc * SCALE - lh)
            dsl = pl_ * (dpl - delta)
            dsc = pc * (dpc - delta)
            db_ref[0, h] += dsl
            grads.append((qm, dom, pl_.astype(bf16), pc.astype(bf16), dsl.astype(bf16), dsc.astype(bf16)))
        for pr in range(NH // 2):
            ps = slice(pr * 2 * HD, (pr + 1) * 2 * HD)
            kw, kc = ops[pr]
            dqs = []
            dkw = dvw = dkc = dvc = None
            for s in range(2):
                qm, dom, plb, pcb, dslb, dscb = grads[2 * pr + s]
                dqs.append(_dot(dslb, kw) + _dot(dscb, kc))
                parts = (_dg(dslb, qm, TN), _dg(plb, dom, TN), _dg(dscb, qm, TN), _dg(pcb, dom, TN))
                if s == 0:
                    dkw, dvw, dkc, dvc = parts
                else:
                    dkw, dvw, dkc, dvc = dkw + parts[0], dvw + parts[1], dkc + parts[2], dvc + parts[3]
            dq_ref[:, ps] = (jnp.where(lo, dqs[0], dqs[1]) * SCALE).astype(bf16)
            dk_s[win, ps] += dkw * SCALE
            dv_s[win, ps] += dvw
            dk_s[T:TA, ps] += dkc * SCALE
            dv_s[T:TA, ps] += dvc

        @pl.when(r == GW - 1)
        def _():
            c1 = pltpu.make_async_copy(dk_s, dk_hbm, sem.at[0])
            c2 = pltpu.make_async_copy(dv_s, dv_hbm, sem.at[1])
            c1.start()
            c2.start()
            c1.wait()
            c2.wait()

        for _, finish in plans:
            pl.when(r == GW - 1)(finish)

    rowq = lambda: pl.BlockSpec((GW, DA), lambda r: (r, 0))
    hbm = pl.BlockSpec(memory_space=pl.ANY)
    return pl.pallas_call(
        body, name="attn_bwd", grid=(GW,),
        in_specs=[rowq(), _full((TA, DA)), _full((TA, DA)), rowq(), rowq(), pl.BlockSpec((GW, NH), lambda r: (r, 0)),
                  pl.BlockSpec((1, NH, GW, WR * GW), lambda r: (_pattern(r), 0, 0, 0))] + [hbm] * nb,
        out_specs=[rowq(), hbm, hbm, pl.BlockSpec((1, NH, GW, WR * GW), lambda r: (_pattern(r), 0, 0, 0))] + [hbm] * nb,
        out_shape=[jax.ShapeDtypeStruct((T, DA), bf16), jax.ShapeDtypeStruct((TA, DA), f32), jax.ShapeDtypeStruct((TA, DA), f32),
                   jax.ShapeDtypeStruct((8, NH, GW, WR * GW), f32)] + [jax.ShapeDtypeStruct(b.shape, b.dtype) for b in blocks],
        scratch_shapes=[pltpu.VMEM((TA, DA), f32), pltpu.VMEM((TA, DA), f32), pltpu.SemaphoreType.DMA((2,))] + _a2a_scratch() * nb,
        compiler_params=_cp("arbitrary"),
    )(q, k, v, y, dy, lse, bias_tab, *blocks)


def _sum_blocks(recv, name):
    _, r, n = recv.shape
    tr = next(cand for cand in (176, 128, 64, 32, 16) if r % cand == 0)

    def body(a_ref, o_ref):
        acc = a_ref[0].astype(f32)
        for d in range(1, NDEV):
            acc = acc + a_ref[d].astype(f32)
        o_ref[...] = acc

    return pl.pallas_call(
        body, name=name, grid=(r // tr,),
        in_specs=[pl.BlockSpec((NDEV, tr, n), lambda i: (0, i, 0))],
        out_specs=pl.BlockSpec((tr, n), lambda i: (i, 0)),
        out_shape=jax.ShapeDtypeStruct((r, n), f32),
        compiler_params=_cp("parallel"),
    )(recv)


def _rpb_reduce(dbias):
    rev = np.zeros((WR * GW, WR * GW), np.float32)
    for i in range(WR):
        for kk in range(GW):
            rev[i * GW + kk, i * GW + GW - 1 - kk] = 1.0

    def body(d_ref, rev_ref, o_ref):
        rv = rev_ref[...]
        for h in range(NH):
            dv = d_ref[0, h]
            r0 = dv.astype(bf16)
            e1 = dv - r0.astype(f32)
            r1 = e1.astype(bf16)
            r2 = (e1 - r1.astype(f32)).astype(bf16)
            rr = _dot(r0, rv) + _dot(r1, rv) + _dot(r2, rv)
            skew = pltpu.roll(rr, 0, 1, stride=1, stride_axis=0)
            o_ref[0, h:h + 1, :] = jnp.sum(skew, axis=0, keepdims=True)

    return pl.pallas_call(
        body, name="rpb_reduce", grid=(8,),
        in_specs=[pl.BlockSpec((1, NH, GW, WR * GW), lambda p: (p, 0, 0, 0)), _full((WR * GW, WR * GW))],
        out_specs=pl.BlockSpec((1, NH, WR * GW), lambda p: (p, 0, 0)),
        out_shape=jax.ShapeDtypeStruct((8, NH, WR * GW), f32),
        compiler_params=_cp("parallel"),
    )(dbias, jnp.asarray(rev, dtype=bf16))


def _norm1_bwd(dq, dk, dv, da, dg, w_inT, x0, ctx0, h, dx1, modv, g1):
    tm = 256
    nt = TA // tm
    nx = T // tm

    def body(dq_ref, dk_ref, dv_ref, da_ref, dg_ref, w_ref, x_ref, c_ref, h_ref, dx1_ref, mod_ref, g_ref,
             dx_ref, dwo_ref, s_ref, dw_ref):
        i = pl.program_id(0)
        is_ctx = i == nt - 1

        @pl.when(i == 0)
        def _():
            s_ref[...] = jnp.zeros_like(s_ref)
            dw_ref[...] = jnp.zeros_like(dw_ref)

        hb = h_ref[...]
        dkb, dvb = dk_ref[...].astype(bf16), dv_ref[...].astype(bf16)
        dw_ref[DA:2 * DA, :] += _dg(dkb, hb, TN)
        dw_ref[2 * DA:3 * DA, :] += _dg(dvb, hb, TN)
        dh_kv = _dot(dkb, w_ref[DA:2 * DA, :]) + _dot(dvb, w_ref[2 * DA:3 * DA, :])
        gain = g_ref[...]

        @pl.when(is_ctx)
        def _():
            xv = c_ref[...]
            rstd = lax.rsqrt(jnp.mean(xv * xv, axis=-1, keepdims=True) + EPS)
            _, dsh, dsc, dgn = _norm_bwd(dh_kv, xv, gain, mod_ref[7:8, :], rstd)
            s_ref[2:3, :] += dgn
            s_ref[3:4, :] += dsh
            s_ref[4:5, :] += dsc
            dwo_ref[...] = dw_ref[...].astype(bf16)

        @pl.when(jnp.logical_not(is_ctx))
        def _():
            dqb, dab, dgb = dq_ref[...], da_ref[...], dg_ref[...]
            dw_ref[0:DA, :] += _dg(dqb, hb, TN)
            dw_ref[3 * DA:4 * DA, :] += _dg(dab, hb, TN)
            dw_ref[4 * DA:5 * DA, :] += _dg(dgb, hb, TN)
            dh = (dh_kv + _dot(dqb, w_ref[0:DA, :]) + _dot(dab, w_ref[3 * DA:4 * DA, :])
                  + _dot(dgb, w_ref[4 * DA:5 * DA, :]))
            xv = x_ref[...]
            rstd = lax.rsqrt(jnp.mean(xv * xv, axis=-1, keepdims=True) + EPS)
            dxn, dsh, dsc, dgn = _norm_bwd(dh, xv, gain, mod_ref[1:2, :], rstd)
            dx_ref[...] = dx1_ref[...] + dxn
            s_ref[0:1, :] += dsh
            s_ref[1:2, :] += dsc
            s_ref[2:3, :] += dgn

    row = lambda w: pl.BlockSpec((tm, w), lambda i: (i, 0))
    lrow = lambda w: pl.BlockSpec((tm, w), lambda i: (jnp.minimum(i, nx - 1), 0))
    return pl.pallas_call(
        body, name="norm1_bwd", grid=(nt,),
        in_specs=[lrow(DA), row(DA), row(DA), lrow(DA), lrow(DA), _full((5 * DA, D)), lrow(D), _full((TC, D)), row(D),
                  lrow(D), _full((8, D)), _full((1, D))],
        out_specs=[lrow(D), _full((5 * DA, D)), _full((8, D))],
        out_shape=[jax.ShapeDtypeStruct((T, D), f32), jax.ShapeDtypeStruct((5 * DA, D), bf16), jax.ShapeDtypeStruct((8, D), f32)],
        scratch_shapes=[pltpu.VMEM((5 * DA, D), f32)],
        compiler_params=_cp("arbitrary"),
    )(dq, dk, dv, da, dg, w_inT, x0, ctx0, h, dx1, modv, g1)


def _adamw(w, g, m, v, name):
    r, c = w.shape
    tr = r
    for cand in (256, 128, 64, 32, 16, 8):
        if r % cand == 0 and r > cand:
            tr = cand
            break

    def body(w_ref, g_ref, m_ref, v_ref, d_ref, nm_ref, nv_ref):
        gv = g_ref[...]
        nm = ADAM_B1 * m_ref[...] + (1.0 - ADAM_B1) * gv
        nv = ADAM_B2 * v_ref[...] + (1.0 - ADAM_B2) * (gv * gv)
        m_hat = nm / (1.0 - ADAM_B1 ** ADAM_STEP)
        v_hat = nv / (1.0 - ADAM_B2 ** ADAM_STEP)
        d_ref[...] = -ADAM_LR * (m_hat / (jnp.sqrt(v_hat) + ADAM_EPS) + ADAM_WD * w_ref[...])
        nm_ref[...] = nm
        nv_ref[...] = nv

    spec = pl.BlockSpec((tr, c), lambda i: (i, 0))
    return pl.pallas_call(
        body, name=name, grid=(r // tr,),
        in_specs=[spec] * 4, out_specs=[spec] * 3,
        out_shape=[jax.ShapeDtypeStruct((r, c), f32)] * 3,
        compiler_params=_cp("parallel"),
    )(w, g, m, v)


def _pad_rows128(vec):
    n = vec.shape[0]
    rows = -(-n // 1024) * 8
    return jnp.pad(vec, (0, rows * 128 - n)).reshape(rows, 128)


def _grad_rpb(dbias):
    lane_map, r_hot = _rpb_tables()
    return jnp.einsum("phl,lic,pir->hrc", _rpb_reduce(dbias), jnp.asarray(lane_map), jnp.asarray(r_hot),
                      precision=lax.Precision.HIGHEST)


def kernel(x, c, ctx, c_ctx, w_mod, b_mod, g_norm1, w_in, rpb, conv_w, conv_b, ln_g, ln_b, w_out, g_norm2, w_up, ffn_conv_w, ffn_conv_b, w_down, g_final, loss_target, m_c_ctx, m_w_mod, m_b_mod, m_g_norm1, m_w_in, m_rpb, m_conv_w, m_conv_b, m_ln_g, m_ln_b, m_w_out, m_g_norm2, m_w_up, m_ffn_conv_w, m_ffn_conv_b, m_w_down, m_g_final, v_c_ctx, v_w_mod, v_b_mod, v_g_norm1, v_w_in, v_rpb, v_conv_w, v_conv_b, v_ln_g, v_ln_b, v_w_out, v_g_norm2, v_w_up, v_ffn_conv_w, v_ffn_conv_b, v_w_down, v_g_final):
    me = 4 * lax.axis_index("x") + 2 * lax.axis_index("y") + lax.axis_index("c")
    nmod = w_mod.shape[2]
    n_in = w_in.shape[2]
    n_out = w_out.shape[1]
    n_up = w_up.shape[2]
    n_dn = w_down.shape[1]
    n_cw = conv_w.shape[2]

    w_inT, c_all = _allgather2([w_in[0].T.astype(bf16), c.reshape(8, 128)], "ag_w_in")

    c_all = c_all.reshape(NDEV, D)
    cvec = jnp.concatenate([c_all, c_ctx[None, :], jnp.zeros((7, D), f32)], axis=0)
    b_sh = lax.dynamic_slice(b_mod, (0, me * nmod), (1, nmod))
    mod_sh = _mod_fwd(cvec, w_mod[0], b_sh)
    n_modp = 16 * nmod
    payload = jnp.concatenate([mod_sh.reshape(-1), conv_w[0].reshape(-1), ffn_conv_w[0].reshape(-1)])
    flat = _small_allgather(_pad_rows128(payload), "ag_mod").reshape(NDEV, -1)
    mod_all = flat[:, :n_modp].reshape(NDEV, 16, nmod).transpose(1, 0, 2).reshape(16, 6 * D)
    mod_me = lax.dynamic_index_in_dim(mod_all, me, 0, keepdims=False).reshape(6, D)
    mod_c = mod_all[8]
    modv = jnp.concatenate([mod_me, mod_c[None, 0:D], mod_c[None, D:2 * D]], axis=0)
    o1 = n_modp + CW * n_cw
    conv_w_f = flat[:, n_modp:o1].reshape(NDEV, CW, n_cw).transpose(1, 0, 2).reshape(CW, DA)
    fcw_f = flat[:, o1:o1 + 3 * n_up].reshape(NDEV, 3, n_up).transpose(1, 0, 2).reshape(3, F2)

    x0, ctx0 = x[0], ctx[0]
    h, q, k, v, a, g = _in_proj(x0, ctx0, g_norm1, modv, w_inT)
    bias_tab = _bias_table(rpb[0])
    y_cv, cv, w_out_f = _conf_fwd(a, g, conv_w_f, conv_b, ln_g, ln_b, [w_out[0].astype(bf16)])
    y_na, lse, w_upT, w_down_f = _attn_fwd(q, k, v, bias_tab, [w_up[0].T.astype(bf16), w_down[0].astype(bf16)])
    x1, proj, h2 = _out_proj(x0, y_na, y_cv, w_out_f, modv, g_norm2)
    ffn, u_t, u2_t = _ffn_fwd(h2, w_upT, fcw_f, ffn_conv_b, w_down_f)
    dx2, dffn, s_loss = _loss_bwd(ffn, x1, loss_target[0], modv, g_final[None, :])

    dug, duv, dw_up, dw_down, dcwg, dcwv, dcbg, dcbv = _ffn_bwd(h2, dffn, u_t, u2_t, fcw_f, w_down_f)
    dx1, dy_na, dy_cv, dw_out, s_n2 = _norm2_bwd(dug, duv, w_upT, x1, dx2, proj, w_out_f, y_na, y_cv, modv, g_norm2)
    da, dg, dcw, s_cf, rv_down = _conf_bwd(a, g, cv, dy_cv, conv_w_f, ln_g, ln_b, [dw_down.reshape(NDEV, n_dn, D)])
    dq, dk, dv, dbias, rv_up, rv_out = _attn_bwd(q, k, v, y_na, dy_na, lse, bias_tab,
                                                 [dw_up.reshape(NDEV, n_up, D), dw_out.reshape(NDEV, n_out, D)])
    g_w_down = _sum_blocks(rv_down, "sum_w_down")
    g_w_out = _sum_blocks(rv_out, "sum_w_out")
    g_w_upT = _sum_blocks(rv_up, "sum_w_up")
    grad_rpb_part = _grad_rpb(dbias)
    grad_x, dw_inT, s_n1 = _norm1_bwd(dq, dk, dv, da, dg, w_inT, x0, ctx0, h, dx1, modv, g_norm1)
    grad_x = grad_x[None]
    dfcw = jnp.concatenate([dcwg, dcwv], axis=1)
    dfcb = jnp.concatenate([dcbg[0], dcbv[0]])
    small = jnp.concatenate([dcw[:CW].reshape(CW, NDEV, n_cw).transpose(1, 0, 2).reshape(NDEV, CW * n_cw),
                             dfcw.reshape(3, NDEV, n_up).transpose(1, 0, 2).reshape(NDEV, 3 * n_up)], axis=1)
    small = jnp.pad(small.reshape(NDEV, 4, D), ((0, 0), (0, 12), (0, 0))).astype(bf16)
    r_a = _reduce_scatter2(jnp.concatenate([dw_inT.reshape(NDEV, n_in, D), small], axis=1), "rs_w_in")
    g_w_inT = r_a[:n_in]
    sm = r_a[n_in:n_in + 4].reshape(-1)
    g_conv_w = sm[:CW * n_cw].reshape(CW, n_cw)
    g_fcw = sm[CW * n_cw:].reshape(3, n_up)

    dmod = jnp.concatenate([s_n1[0], s_n1[1], s_n2[3], s_n2[0], s_n2[1], s_loss[1]])
    dmodc = jnp.concatenate([s_n1[3], s_n1[4]])
    parts = [dmodc, s_n1[2], grad_rpb_part.reshape(-1), s_cf[0], s_cf[1], s_cf[2], s_n2[2], dfcb, s_loss[0], s_loss[3, 0:1]]
    sizes = [p.shape[0] for p in parts]
    pvec = _pad_rows128(jnp.concatenate([dmod] + parts))
    gath = _small_allgather(pvec, "ag_small")
    tot = _sum_rows8(gath, "sum_small").reshape(-1)
    dmod_all = gath.reshape(NDEV, -1)[:, :6 * D]
    offs = np.cumsum([6 * D] + sizes)
    pick = lambda j: tot[offs[j]:offs[j + 1]]
    dmodc_t = jnp.pad(pick(0), (0, 4 * D))
    g_b_mod = (tot[:6 * D] + dmodc_t)[None, :]
    g_g_norm1 = pick(1)[None, :]
    g_rpb = pick(2).reshape(1, NH, 2 * WR - 1, 2 * NCOL - 1)
    g_conv_b, g_ln_g, g_ln_b = pick(3)[None, :], pick(4)[None, :], pick(5)[None, :]
    g_g_norm2 = pick(6)[None, :]
    g_fcb = pick(7)[None, :]
    g_g_final = pick(8)
    loss = pick(9)[0]
    dm_rows = jnp.concatenate([dmod_all, dmodc_t[None, :], jnp.zeros((7, 6 * D), f32)], axis=0)
    dm_sh = lax.dynamic_slice(dm_rows, (0, me * nmod), (16, nmod))
    g_w_mod, gc_part = _mod_bwd(cvec, dm_sh, w_mod[0])
    gc_sum = _sum_rows8(_small_allgather(gc_part[0].reshape(8, 128), "ag_cctx"), "sum_cctx").reshape(D)
    sg_c = _sigmoid(c_ctx)
    g_c_ctx = gc_sum * (sg_c * (1.0 + c_ctx * (1.0 - sg_c)))

    big = [("w_mod", w_mod[0], g_w_mod, m_w_mod[0], v_w_mod[0]), ("w_in", w_in[0].T, g_w_inT, m_w_in[0].T, v_w_in[0].T),
           ("w_out", w_out[0], g_w_out, m_w_out[0], v_w_out[0]), ("w_up", w_up[0].T, g_w_upT, m_w_up[0].T, v_w_up[0].T),
           ("w_down", w_down[0], g_w_down, m_w_down[0], v_w_down[0])]
    upd = {n: _adamw(wv, gv, mv, vv, "adamw_" + n) for n, wv, gv, mv, vv in big}
    for n in ("w_in", "w_up"):
        upd[n] = tuple(arr.T for arr in upd[n])
    g_w_in, g_w_up = g_w_inT.T, g_w_upT.T
    smalls = [("c_ctx", c_ctx, g_c_ctx, m_c_ctx, v_c_ctx), ("b_mod", b_mod, g_b_mod, m_b_mod, v_b_mod),
              ("g_norm1", g_norm1, g_g_norm1, m_g_norm1, v_g_norm1), ("rpb", rpb, g_rpb, m_rpb, v_rpb),
              ("conv_w", conv_w, g_conv_w[None], m_conv_w, v_conv_w), ("conv_b", conv_b, g_conv_b, m_conv_b, v_conv_b),
              ("ln_g", ln_g, g_ln_g, m_ln_g, v_ln_g), ("ln_b", ln_b, g_ln_b, m_ln_b, v_ln_b),
              ("g_norm2", g_norm2, g_g_norm2, m_g_norm2, v_g_norm2),
              ("ffn_conv_w", ffn_conv_w, g_fcw[None], m_ffn_conv_w, v_ffn_conv_w),
              ("ffn_conv_b", ffn_conv_b, g_fcb, m_ffn_conv_b, v_ffn_conv_b), ("g_final", g_final, g_g_final, m_g_final, v_g_final)]
    packed = [_pad_rows128(jnp.concatenate([t[j].reshape(-1) for t in smalls])) for j in (1, 2, 3, 4)]
    sd, sm_, sv = _adamw(*packed, "adamw_small")
    so = np.cumsum([0] + [int(np.prod(t[1].shape)) for t in smalls])
    for j, t in enumerate(smalls):
        shp = t[1].shape
        upd[t[0]] = tuple(arr.reshape(-1)[so[j]:so[j + 1]].reshape(shp) for arr in (sd, sm_, sv))
    grads = {"c_ctx": g_c_ctx, "w_mod": g_w_mod[None], "b_mod": g_b_mod, "g_norm1": g_g_norm1, "w_in": g_w_in[None],
             "rpb": g_rpb, "conv_w": g_conv_w[None], "conv_b": g_conv_b, "ln_g": g_ln_g, "ln_b": g_ln_b,
             "w_out": g_w_out[None], "g_norm2": g_g_norm2, "w_up": g_w_up[None], "ffn_conv_w": g_fcw[None],
             "ffn_conv_b": g_fcb, "w_down": g_w_down[None], "g_final": g_g_final}
    names = ["c_ctx", "w_mod", "b_mod", "g_norm1", "w_in", "rpb", "conv_w", "conv_b", "ln_g", "ln_b", "w_out", "g_norm2",
             "w_up", "ffn_conv_w", "ffn_conv_b", "w_down", "g_final"]
    shapes = {n: grads[n].shape for n in names}
    outs = [loss, grad_x] + [grads[n] for n in names]
    for j in range(3):
        outs += [upd[n][j].reshape(shapes[n]) for n in names]
    return tuple(outs)
```

```python
import functools

import numpy as np
import jax
import jax.numpy as jnp
from jax import lax
from jax.experimental import pallas as pl
from jax.experimental.pallas import tpu as pltpu

f32 = jnp.float32
bf16 = jnp.bfloat16

D = 1024
T = 4096
TC = 256
TA = T + TC
DA = 512
NH = 8
HD = 64
GW = 64
WR = 8
NCOL = 16
F = 2816
F2 = 2 * F
CW = 31
NDEV = 8
EPS = 1e-6
SCALE = HD ** -0.5
NEG = -1e30
MESH = pl.DeviceIdType.MESH

NT = (((1,), (1,)), ((), ()))
TN = (((0,), (0,)), ((), ()))

ADAM_LR, ADAM_B1, ADAM_B2, ADAM_EPS, ADAM_WD, ADAM_STEP = 0.001, 0.9, 0.999, 1e-08, 0.01, 10

VMEM_LIMIT = 56 * 1024 * 1024


def _cp(*sem):
    return pltpu.CompilerParams(dimension_semantics=sem or None, vmem_limit_bytes=VMEM_LIMIT)


def _dot(a, b):
    return jnp.dot(a, b, preferred_element_type=f32)


def _dg(a, b, dims):
    return lax.dot_general(a, b, dims, preferred_element_type=f32)


def _sigmoid(x):
    return 1.0 / (1.0 + jnp.exp(-x))


def _full(shape):
    n = len(shape)
    return pl.BlockSpec(shape, lambda *_: (0,) * n)


def _my_pos():
    return lax.axis_index("x"), lax.axis_index("y"), lax.axis_index("c")


def _small_gather_plan(v_ref, out_ref, send_sems, recv_sems):
    x, y, c = _my_pos()
    me = 4 * x + 2 * y + c
    peers = []
    for k in range(1, NDEV):
        kx, ky, kc = (k >> 2) & 1, (k >> 1) & 1, k & 1
        peers.append((x ^ kx, y ^ ky, c ^ kc))

    def copy(k, slot, to):
        return pltpu.make_async_remote_copy(
            src_ref=v_ref, dst_ref=out_ref.at[slot], send_sem=send_sems.at[k], recv_sem=recv_sems.at[k],
            device_id=to, device_id_type=MESH)

    def start():
        out_ref[me] = v_ref[...]
        for k, p in enumerate(peers):
            copy(k, me, p).start()

    def finish():
        for k, (px, py, pc) in enumerate(peers):
            copy(k, 4 * px + 2 * py + pc, (x, y, c)).wait_recv()
        for k, p in enumerate(peers):
            copy(k, me, p).wait_send()

    return start, finish


def _small_gather_scratch():
    return [pltpu.SemaphoreType.DMA((NDEV - 1,)), pltpu.SemaphoreType.DMA((NDEV - 1,))]


def _small_allgather(v, name):
    n = v.shape[0]

    def body(v_ref, out_ref, send_sems, recv_sems):
        start, finish = _small_gather_plan(v_ref, out_ref, send_sems, recv_sems)
        start()
        finish()

    return pl.pallas_call(
        body, name=name,
        out_shape=jax.ShapeDtypeStruct((NDEV, n, 128), f32),
        in_specs=[pl.BlockSpec(memory_space=pltpu.VMEM)],
        out_specs=pl.BlockSpec(memory_space=pltpu.VMEM),
        scratch_shapes=_small_gather_scratch(),
    )(v)


def _ag2_plan(x_refs, out_refs, send_sems, recv_sems, local_sems):
    na = len(x_refs)
    x, y, c = _my_pos()
    me, sibling = (x, y, c), (x, y, 1 - c)
    chips = [(1 - x, y), (x, 1 - y), (1 - x, 1 - y)]

    def rows(i, px, py, pc):
        m_per = x_refs[i].shape[0]
        return out_refs[i].at[pl.ds(pl.multiple_of((4 * px + 2 * py + pc) * m_per, 16 if m_per % 16 == 0 else 8), m_per), :]

    def copies(k, block, to, from_shard=False):
        return [pltpu.make_async_remote_copy(
            src_ref=x_refs[i] if from_shard else rows(i, *block), dst_ref=rows(i, *block),
            send_sem=send_sems.at[k * na + i], recv_sem=recv_sems.at[k * na + i], device_id=to, device_id_type=MESH)
            for i in range(na)]

    def mine():
        return [pltpu.make_async_copy(x_refs[i], rows(i, *me), local_sems.at[i]) for i in range(na)]

    def first():
        cps = copies(0, me, sibling, True)
        for j, chip in enumerate(chips):
            cps += copies(1 + j, me, (*chip, c), True)
        return cps

    def start():
        for cp in mine() + first():
            cp.start()

    def forward():
        for j, chip in enumerate(chips):
            for cp in copies(1 + j, (*chip, c), me):
                cp.wait_recv()
            for cp in copies(4 + j, (*chip, c), sibling):
                cp.start()

    def finish():
        for cp in copies(0, sibling, me):
            cp.wait_recv()
        for j, chip in enumerate(chips):
            for cp in copies(4 + j, (*chip, 1 - c), me):
                cp.wait_recv()
        for cp in first():
            cp.wait_send()
        for j, chip in enumerate(chips):
            for cp in copies(4 + j, (*chip, c), sibling):
                cp.wait_send()
        for cp in mine():
            cp.wait()

    return start, forward, finish


def _ag2_scratch(na):
    return [pltpu.SemaphoreType.DMA((7 * na,)), pltpu.SemaphoreType.DMA((7 * na,)), pltpu.SemaphoreType.DMA((na,))]


def _a2a_plan(g_ref, recv_ref, send_sems, recv_sems, local_sem):
    x, y, c = _my_pos()
    me = 4 * x + 2 * y + c
    peers = []
    for k in range(1, NDEV):
        kx, ky, kc = (k >> 2) & 1, (k >> 1) & 1, k & 1
        peers.append((x ^ kx, y ^ ky, c ^ kc))

    def sends():
        return [pltpu.make_async_remote_copy(
            src_ref=g_ref.at[4 * px + 2 * py + pc], dst_ref=recv_ref.at[me], send_sem=send_sems.at[k], recv_sem=recv_sems.at[k],
            device_id=(px, py, pc), device_id_type=MESH) for k, (px, py, pc) in enumerate(peers)]

    def own():
        return pltpu.make_async_copy(g_ref.at[me], recv_ref.at[me], local_sem)

    def start():
        own().start()
        for cp in sends():
            cp.start()

    def finish():
        for k, (px, py, pc) in enumerate(peers):
            pltpu.make_async_remote_copy(
                src_ref=g_ref.at[me], dst_ref=recv_ref.at[4 * px + 2 * py + pc], send_sem=send_sems.at[k],
                recv_sem=recv_sems.at[k], device_id=(x, y, c), device_id_type=MESH).wait_recv()
        for cp in sends():
            cp.wait_send()
        own().wait()

    return start, finish


def _a2a_scratch():
    return [pltpu.SemaphoreType.DMA((NDEV - 1,)), pltpu.SemaphoreType.DMA((NDEV - 1,)), pltpu.SemaphoreType.DMA]


def _allgather2(shards, name):
    na = len(shards)

    def body(*refs):
        start, forward, finish = _ag2_plan(refs[:na], refs[na:2 * na], *refs[2 * na:])
        start()
        forward()
        finish()

    return pl.pallas_call(
        body, name=name,
        out_shape=[jax.ShapeDtypeStruct((NDEV * s.shape[0], s.shape[1]), s.dtype) for s in shards],
        in_specs=[pl.BlockSpec(memory_space=pltpu.VMEM)] * na,
        out_specs=[pl.BlockSpec(memory_space=pltpu.VMEM)] * na,
        scratch_shapes=_ag2_scratch(na),
        compiler_params=pltpu.CompilerParams(vmem_limit_bytes=VMEM_LIMIT),
    )(*shards)


def _reduce_scatter2(g, small, name):
    _, r, n = g.shape
    ch = 16
    nch = r // ch
    ns = len(small)

    def body(g_ref, *rest):
        v_refs, out_ref, vout_refs = rest[:ns], rest[ns], rest[ns + 1:2 * ns + 1]
        a_ref, h_ref, b_ref, s1_send, s1_recv, s2_send, s2_recv = rest[2 * ns + 1:2 * ns + 8]
        gather_sems = rest[2 * ns + 8:]
        gathers = [_small_gather_plan(v_refs[i], vout_refs[i], *gather_sems[2 * i:2 * i + 2]) for i in range(ns)]
        for start, _ in gathers:
            start()
        x, y, c = _my_pos()
        sibling = (x, y, 1 - c)
        s1 = []
        for j in range(4):
            cp = pltpu.make_async_remote_copy(
                src_ref=g_ref.at[2 * j + (1 - c)], dst_ref=a_ref.at[j], send_sem=s1_send.at[j], recv_sem=s1_recv.at[j],
                device_id=sibling, device_id_type=MESH)
            cp.start()
            s1.append(cp)
        for cp in s1:
            cp.wait_recv()

        def add1(i, _):
            rr = pl.ds(pl.multiple_of(i * ch, ch), ch)
            for j in range(4):
                h_ref[j, rr, :] = (g_ref[2 * j + c, rr, :].astype(f32) + a_ref[j, rr, :].astype(f32)).astype(bf16)
            return 0
        lax.fori_loop(0, nch, add1, 0)
        mychip = 2 * x + y
        s2 = []
        for m in range(1, 4):
            mx, my_ = (m >> 1) & 1, m & 1
            px, py = x ^ mx, y ^ my_
            cp = pltpu.make_async_remote_copy(
                src_ref=h_ref.at[2 * px + py], dst_ref=b_ref.at[m - 1], send_sem=s2_send.at[m - 1], recv_sem=s2_recv.at[m - 1],
                device_id=(px, py, c), device_id_type=MESH)
            cp.start()
            s2.append(cp)
        for cp in s2:
            cp.wait_recv()

        def add2(i, _):
            rr = pl.ds(pl.multiple_of(i * ch, ch), ch)
            acc = h_ref[mychip, rr, :].astype(f32)
            for m in range(3):
                acc = acc + b_ref[m, rr, :].astype(f32)
            out_ref[rr, :] = acc
            return 0
        lax.fori_loop(0, nch, add2, 0)
        for cp in s1 + s2:
            cp.wait_send()
        for _, finish in gathers:
            finish()

    vmem = pl.BlockSpec(memory_space=pltpu.VMEM)
    return pl.pallas_call(
        body, name=name,
        out_shape=[jax.ShapeDtypeStruct((r, n), f32)] + [jax.ShapeDtypeStruct((NDEV,) + v.shape, v.dtype) for v in small],
        in_specs=[vmem] * (1 + ns),
        out_specs=[vmem] * (1 + ns),
        scratch_shapes=[pltpu.VMEM((4, r, n), bf16), pltpu.VMEM((4, r, n), bf16), pltpu.VMEM((3, r, n), bf16),
                        pltpu.SemaphoreType.DMA((4,)), pltpu.SemaphoreType.DMA((4,)),
                        pltpu.SemaphoreType.DMA((3,)), pltpu.SemaphoreType.DMA((3,))] + _small_gather_scratch() * ns,
        compiler_params=pltpu.CompilerParams(vmem_limit_bytes=VMEM_LIMIT),
    )(g, *small)


def _mod_fwd(cvec, w_sh, b_sh):
    def body(c_ref, w_ref, b_ref, o_ref):
        cv = c_ref[...]
        act = (cv * _sigmoid(cv)).astype(bf16)
        o_ref[...] = _dot(act, w_ref[...].astype(bf16)) + b_ref[...]
    return pl.pallas_call(body, name="mod_fwd", out_shape=jax.ShapeDtypeStruct((16, w_sh.shape[1]), f32))(cvec, w_sh, b_sh)


def _mod_bwd(cvec, dm_sh, w_sh):
    def body(c_ref, dm_ref, w_ref, gw_ref, gc_ref):
        cv = c_ref[...]
        act = (cv * _sigmoid(cv)).astype(bf16)
        gw_ref[...] = _dg(act, dm_ref[...].astype(bf16), TN)
        gc_ref[...] = _dg(dm_ref[8:16, :].astype(bf16), w_ref[...].astype(bf16), NT)
    return pl.pallas_call(
        body, name="mod_bwd",
        out_shape=(jax.ShapeDtypeStruct(w_sh.shape, f32), jax.ShapeDtypeStruct((8, D), f32)))(cvec, dm_sh, w_sh)


def _sum_rows8(a, name):
    n = a.shape[1]

    def body(a_ref, o_ref):
        acc = a_ref[0]
        for d in range(1, NDEV):
            acc = acc + a_ref[d]
        o_ref[...] = acc
    return pl.pallas_call(body, name=name, out_shape=jax.ShapeDtypeStruct((n, 128), f32))(a)


def _in_proj(x0, ctx0, g1, modv, w_inT):
    tm = 256
    nt = TA // tm
    nx = T // tm

    def body(x_ref, c_ref, g_ref, mod_ref, w_ref, h_ref, q_ref, k_ref, v_ref, a_ref, gg_ref):
        i = pl.program_id(0)
        is_ctx = i == nt - 1
        xv = jnp.where(is_ctx, c_ref[...], x_ref[...])
        rstd = lax.rsqrt(jnp.mean(xv * xv, axis=-1, keepdims=True) + EPS)
        sh = jnp.where(is_ctx, mod_ref[6:7, :], mod_ref[0:1, :])
        sc = jnp.where(is_ctx, mod_ref[7:8, :], mod_ref[1:2, :])
        h = ((xv * rstd * g_ref[...]) * (1.0 + sc) + sh).astype(bf16)
        h_ref[...] = h
        for j, o_ref in enumerate((q_ref, k_ref, v_ref, a_ref, gg_ref)):
            o_ref[...] = _dg(h, w_ref[j * DA:(j + 1) * DA, :], NT).astype(o_ref.dtype)

    row = lambda w: pl.BlockSpec((tm, w), lambda i: (i, 0))
    return pl.pallas_call(
        body, name="in_proj", grid=(nt,),
        in_specs=[pl.BlockSpec((tm, D), lambda i: (jnp.minimum(i, nx - 1), 0)), _full((TC, D)),
                  _full((1, D)), _full((8, D)), _full((5 * DA, D))],
        out_specs=[row(D), row(DA), row(DA), row(DA), row(DA), row(DA)],
        out_shape=[jax.ShapeDtypeStruct((TA, D), bf16)] + [jax.ShapeDtypeStruct((TA, DA), bf16)] * 3
                  + [jax.ShapeDtypeStruct((TA, DA), f32)] * 2,
        compiler_params=_cp("parallel"),
    )(x0, ctx0, g1, modv, w_inT)


def _win_start(r):
    return jnp.clip(r - WR // 2, 0, GW - WR)


def _pattern(r):
    return _win_start(r) - r + (WR - 1)


def _bias_table(rpb):
    qc = np.arange(GW)[:, None]
    kc = np.arange(GW)[None, :]
    cs = np.clip(qc - NCOL // 2, 0, GW - NCOL)
    valid = np.tile(((kc >= cs) & (kc < cs + NCOL)).astype(np.int32), (1, WR))
    pad = jnp.pad(rpb, ((0, 0), (0, 0), (0, GW - (2 * NCOL - 1))))
    base = jnp.stack([pad[:, p:p + WR, :].reshape(NH, WR * GW) for p in range(8)])

    def body(base_ref, valid_ref, o_ref):
        ok = valid_ref[...] != 0
        for h in range(NH):
            row = jnp.broadcast_to(base_ref[0, h:h + 1, :], (GW, WR * GW))
            skew = pltpu.roll(row, WR * GW - (NCOL - 1), 1, stride=1, stride_axis=0)
            o_ref[0, h] = jnp.where(ok, skew, NEG)

    return pl.pallas_call(
        body, name="bias_table", grid=(8,),
        in_specs=[pl.BlockSpec((1, NH, WR * GW), lambda p: (p, 0, 0)), _full((GW, WR * GW))],
        out_specs=pl.BlockSpec((1, NH, GW, WR * GW), lambda p: (p, 0, 0, 0)),
        out_shape=jax.ShapeDtypeStruct((8, NH, GW, WR * GW), f32),
        compiler_params=_cp("parallel"),
    )(base, jnp.asarray(valid))


def _rpb_tables():
    lane_map = np.zeros((WR * GW, WR, 2 * NCOL - 1), np.float32)
    for i in range(WR):
        for t in range(GW):
            if t >= GW - NCOL:
                lane_map[i * GW + t, i, (GW - 1 - t) + NCOL - 1] = 1.0
            elif t < NCOL - 1:
                lane_map[i * GW + t, (i - 1) % WR, NCOL - 2 - t] = 1.0
    p = np.arange(8)[:, None]
    i = np.arange(WR)[None, :]
    r_hot = ((p + i)[:, :, None] == np.arange(2 * WR - 1)[None, None, :]).astype(np.float32)
    return lane_map, r_hot


AG_FORWARD_ROW = 58


def _attn_fwd(q, k, v, bias_tab, shards):
    na = len(shards)

    def body(q_ref, k_ref, v_ref, b_ref, *rest):
        x_refs, (y_ref, lse_ref), out_refs, sems = rest[:na], rest[na:na + 2], rest[na + 2:2 * na + 2], rest[2 * na + 2:]
        r = pl.program_id(0)
        if na:
            start, forward, finish = _ag2_plan(x_refs, out_refs, *sems)
            pl.when(r == 0)(start)
            pl.when(r == AG_FORWARD_ROW)(forward)
        ks = pl.multiple_of(_win_start(r) * GW, GW)
        qq = q_ref[...]
        lo = lax.broadcasted_iota(jnp.int32, (GW, 2 * HD), 1) < HD
        kv, scores = [], []
        for pr in range(NH // 2):
            ps = slice(pr * 2 * HD, (pr + 1) * 2 * HD)
            q2 = qq[:, ps]
            kw, kc = k_ref[pl.ds(ks, WR * GW), ps], k_ref[T:TA, ps]
            kv.append((v_ref[pl.ds(ks, WR * GW), ps], v_ref[T:TA, ps]))
            for s in range(2):
                qm = jnp.where(lo if s == 0 else ~lo, q2, jnp.zeros_like(q2))
                scores.append((_dg(qm, kw, NT) * SCALE + b_ref[0, 2 * pr + s], _dg(qm, kc, NT) * SCALE))
        probs = []
        for h, (sl, sc) in enumerate(scores):
            m = jnp.maximum(jnp.max(sl, axis=-1, keepdims=True), jnp.max(sc, axis=-1, keepdims=True))
            pl_ = jnp.exp(sl - m)
            pc = jnp.exp(sc - m)
            l = jnp.sum(pl_, axis=-1, keepdims=True) + jnp.sum(pc, axis=-1, keepdims=True)
            lse_ref[:, h:h + 1] = m + jnp.log(l)
            probs.append((pl_.astype(bf16), pc.astype(bf16), 1.0 / l))
        for pr in range(NH // 2):
            ps = slice(pr * 2 * HD, (pr + 1) * 2 * HD)
            vw, vc = kv[pr]
            outs = [(_dot(pb, vw) + _dot(cb, vc)) * rl for pb, cb, rl in probs[2 * pr:2 * pr + 2]]
            y_ref[:, ps] = jnp.where(lo, outs[0], outs[1]).astype(bf16)
        if na:
            pl.when(r == GW - 1)(finish)

    hbm = pl.BlockSpec(memory_space=pl.ANY)
    return pl.pallas_call(
        body, name="attn_fwd", grid=(GW,),
        in_specs=[pl.BlockSpec((GW, DA), lambda r: (r, 0)), _full((TA, DA)), _full((TA, DA)),
                  pl.BlockSpec((1, NH, GW, WR * GW), lambda r: (_pattern(r), 0, 0, 0))] + [hbm] * na,
        out_specs=[pl.BlockSpec((GW, DA), lambda r: (r, 0)), pl.BlockSpec((GW, NH), lambda r: (r, 0))] + [hbm] * na,
        out_shape=[jax.ShapeDtypeStruct((T, DA), bf16), jax.ShapeDtypeStruct((T, NH), f32)]
                  + [jax.ShapeDtypeStruct((NDEV * s.shape[0], s.shape[1]), s.dtype) for s in shards],
        scratch_shapes=_ag2_scratch(na) if na else [],
        compiler_params=_cp("arbitrary"),
    )(q, k, v, bias_tab, *shards)


CONV_TT = 256
HALO = 16


def _halo_specs(tt, w, nrows_blocks):
    per = tt // HALO
    prev = pl.BlockSpec((HALO, w), lambda i: (jnp.maximum(i * per - 1, 0), 0))
    cur = pl.BlockSpec((tt, w), lambda i: (i, 0))
    nxt = pl.BlockSpec((HALO, w), lambda i: (jnp.minimum((i + 1) * per, nrows_blocks - 1), 0))
    return [prev, cur, nxt]


def _shifted_copies(rot, wn):
    for b in range(1, 8):
        rot[b, 0:wn - 8, :] = rot[0, pl.ds(b, wn - 8), :]


def _conf_fwd(a, g, conv_w, conv_b, ln_g, ln_b, shards):
    tt = CONV_TT
    nt = T // tt
    sub = 32
    wn = tt + 2 * HALO
    na = len(shards)

    def body(ap, ac, an, gp, gc, gn, w_ref, b_ref, lg_ref, lb_ref, *rest):
        x_refs, (y_ref, cv_ref), out_refs, (rot, *sems) = rest[:na], rest[na:na + 2], rest[na + 2:2 * na + 2], rest[2 * na + 2:]
        i = pl.program_id(0)
        if na:
            start, forward, finish = _ag2_plan(x_refs, out_refs, *sems)
            pl.when(i == 0)(start)
            pl.when(i == nt // 2)(forward)
        rot[0, 0:HALO, :] = jnp.where(i > 0, ap[...] * _sigmoid(gp[...]), 0.0)
        rot[0, HALO:HALO + tt, :] = ac[...] * _sigmoid(gc[...])
        rot[0, HALO + tt:, :] = jnp.where(i < nt - 1, an[...] * _sigmoid(gn[...]), 0.0)
        _shifted_copies(rot, wn)
        w = w_ref[...]
        for s in range(tt // sub):
            acc = jnp.zeros((sub, DA), f32)
            for j in range(CW):
                a8, b8 = divmod(1 + j, 8)
                acc = acc + rot[b8, pl.ds(s * sub + 8 * a8, sub), :] * w[j:j + 1, :]
            cv = acc + b_ref[...]
            cv_ref[pl.ds(s * sub, sub), :] = cv
            mu = jnp.mean(cv, axis=-1, keepdims=True)
            xc = cv - mu
            rstd = lax.rsqrt(jnp.mean(xc * xc, axis=-1, keepdims=True) + EPS)
            z = xc * rstd * lg_ref[...] + lb_ref[...]
            y_ref[pl.ds(s * sub, sub), :] = (z * _sigmoid(z)).astype(bf16)
        if na:
            pl.when(i == nt - 1)(finish)

    hs = _halo_specs(tt, DA, T // HALO)
    hbm = pl.BlockSpec(memory_space=pl.ANY)
    return pl.pallas_call(
        body, name="conf_fwd", grid=(nt,),
        in_specs=hs + hs + [_full((CW, DA)), _full((1, DA)), _full((1, DA)), _full((1, DA))] + [hbm] * na,
        out_specs=[pl.BlockSpec((tt, DA), lambda i: (i, 0)), pl.BlockSpec((tt, DA), lambda i: (i, 0))] + [hbm] * na,
        out_shape=[jax.ShapeDtypeStruct((T, DA), bf16), jax.ShapeDtypeStruct((T, DA), f32)]
                  + [jax.ShapeDtypeStruct((NDEV * s.shape[0], s.shape[1]), s.dtype) for s in shards],
        scratch_shapes=[pltpu.VMEM((8, wn, DA), f32)] + (_ag2_scratch(na) if na else []),
        compiler_params=_cp("arbitrary"),
    )(a, a, a, g, g, g, conv_w, conv_b, ln_g, ln_b, *shards)


def _out_proj(xa, y_na, y_cv, w_out, modv, g2):
    tm = 512

    def body(x_ref, ya_ref, yc_ref, w_ref, mod_ref, g_ref, x1_ref, pj_ref, h2_ref):
        proj = _dot(ya_ref[...], w_ref[0:DA, :]) + _dot(yc_ref[...], w_ref[DA:D, :])
        x1 = x_ref[...] + mod_ref[2:3, :] * proj
        x1_ref[...] = x1
        pj_ref[...] = proj.astype(bf16)
        rstd = lax.rsqrt(jnp.mean(x1 * x1, axis=-1, keepdims=True) + EPS)
        h2_ref[...] = ((x1 * rstd * g_ref[...]) * (1.0 + mod_ref[4:5, :]) + mod_ref[3:4, :]).astype(bf16)

    row = lambda w: pl.BlockSpec((tm, w), lambda i: (i, 0))
    return pl.pallas_call(
        body, name="out_proj", grid=(T // tm,),
        in_specs=[row(D), row(DA), row(DA), _full((D, D)), _full((8, D)), _full((1, D))],
        out_specs=[row(D), row(D), row(D)],
        out_shape=[jax.ShapeDtypeStruct((T, D), f32), jax.ShapeDtypeStruct((T, D), bf16), jax.ShapeDtypeStruct((T, D), bf16)],
        compiler_params=_cp("parallel"),
    )(xa, y_na, y_cv, w_out, modv, g2)


FFN_TT = 1024
FFN_CT = 256
FFN_NC = F // FFN_CT
FFN_SUB = 32


def _row_neighbours(ref, r, n):
    blk = ref[pl.ds(r - 8, n + 16), :]
    return blk[8:8 + n, :], pltpu.roll(blk, 1, 0)[8:8 + n, :], pltpu.roll(blk, n + 15, 0)[8:8 + n, :]


def _ffn_specs(tt, ct, by_token_first):
    tc = (lambda f: (lambda t, c: f(t, c))) if by_token_first else (lambda f: (lambda c, t: f(t, c)))
    per = tt // HALO
    halo = [pl.BlockSpec((HALO, D), tc(lambda t, c: (jnp.maximum(t * per - 1, 0), 0))),
            pl.BlockSpec((tt, D), tc(lambda t, c: (t, 0))),
            pl.BlockSpec((HALO, D), tc(lambda t, c: (jnp.minimum((t + 1) * per, T // HALO - 1), 0)))]
    weights = [pl.BlockSpec((ct, D), tc(lambda t, c: (c, 0))), pl.BlockSpec((ct, D), tc(lambda t, c: (c + FFN_NC, 0))),
               pl.BlockSpec((3, ct), tc(lambda t, c: (0, c))), pl.BlockSpec((3, ct), tc(lambda t, c: (0, c + FFN_NC))),
               pl.BlockSpec((1, ct), tc(lambda t, c: (0, c))), pl.BlockSpec((1, ct), tc(lambda t, c: (0, c + FFN_NC))),
               pl.BlockSpec((ct, D), tc(lambda t, c: (c, 0)))]
    return halo, weights


def _ffn_fwd(h2, w_upT, fcw, fcb, w_down):
    tt, ct = FFN_TT, FFN_CT
    nt = T // tt
    wn = tt + 2 * HALO
    half = tt // 2

    def body(hp, hc, hn, wg_ref, wv_ref, cwg_ref, cwv_ref, cbg_ref, cbv_ref, wd_ref, o_ref, u_ref, u2_ref, hwin, uwin, act):
        t = pl.program_id(0)
        c = pl.program_id(1)

        @pl.when(c == 0)
        def _():
            hwin[0:HALO, :] = jnp.where(t > 0, hp[...], jnp.zeros_like(hp[...]))
            hwin[HALO:HALO + tt, :] = hc[...]
            hwin[HALO + tt:, :] = jnp.where(t < nt - 1, hn[...], jnp.zeros_like(hn[...]))
            o_ref[...] = jnp.zeros_like(o_ref)

        for r0, r1 in ((0, half + 2 * HALO), (half + 2 * HALO, wn)):
            hw = hwin[r0:r1, :]
            uwin[r0:r1, :ct] = _dg(hw, wg_ref[...], NT)
            uwin[r0:r1, ct:] = _dg(hw, wv_ref[...], NT)
        cw = jnp.concatenate([cwg_ref[...], cwv_ref[...]], axis=1)
        cb = jnp.concatenate([cbg_ref[...], cbv_ref[...]], axis=1)
        for p in range(2):
            for r in range(p * half, (p + 1) * half, FFN_SUB):
                uc, prev, nxt = _row_neighbours(uwin, HALO + r, FFN_SUB)
                u2 = prev * cw[0:1, :] + uc * cw[1:2, :] + nxt * cw[2:3, :] + cb
                u_ref[r:r + FFN_SUB, :] = uc.astype(bf16)
                u2_ref[r:r + FFN_SUB, :] = u2
                gate = u2[:, :ct]
                act[r:r + FFN_SUB, :] = (gate * _sigmoid(gate) * u2[:, ct:]).astype(bf16)
            rows = slice(p * half, (p + 1) * half)
            o_ref[rows, :] += _dot(act[rows, :], wd_ref[...])

    halo, weights = _ffn_specs(tt, ct, True)
    pair = pl.BlockSpec((tt, 2 * ct), lambda t, c: (t, c))
    return pl.pallas_call(
        body, name="ffn_fwd", grid=(nt, FFN_NC),
        in_specs=halo + weights,
        out_specs=[pl.BlockSpec((tt, D), lambda t, c: (t, 0)), pair, pair],
        out_shape=[jax.ShapeDtypeStruct((T, D), f32), jax.ShapeDtypeStruct((T, F2), bf16), jax.ShapeDtypeStruct((T, F2), f32)],
        scratch_shapes=[pltpu.VMEM((wn, D), bf16), pltpu.VMEM((wn, 2 * ct), f32), pltpu.VMEM((tt, ct), bf16)],
        compiler_params=_cp("parallel", "arbitrary"),
    )(h2, h2, h2, w_upT, w_upT, fcw, fcw, fcb, fcb, w_down)


def _loss_bwd(ffn, x1, tgt, modv, gf):
    tm = 512
    nt = T // tm

    def body(f_ref, x1_ref, t_ref, mod_ref, g_ref, dx2_ref, df_ref, s_ref):
        i = pl.program_id(0)

        @pl.when(i == 0)
        def _():
            s_ref[...] = jnp.zeros_like(s_ref)

        ff = f_ref[...]
        gt2 = mod_ref[5:6, :]
        x2 = x1_ref[...] + gt2 * ff
        rstd = lax.rsqrt(jnp.mean(x2 * x2, axis=-1, keepdims=True) + EPS)
        xh = x2 * rstd
        gfv = g_ref[...]
        e = xh * gfv - t_ref[...]
        dy = e * (1.0 / D)
        dxh = dy * gfv
        dx2 = rstd * (dxh - xh * jnp.mean(dxh * xh, axis=-1, keepdims=True))
        dx2_ref[...] = dx2
        df_ref[...] = (dx2 * gt2).astype(bf16)
        s_ref[0:1, :] += jnp.sum(dy * xh, axis=0, keepdims=True)
        s_ref[1:2, :] += jnp.sum(dx2 * ff, axis=0, keepdims=True)
        s_ref[2:3, :] += jnp.sum(e * e, axis=0, keepdims=True)

        @pl.when(i == nt - 1)
        def _():
            tot = jnp.sum(s_ref[2:3, :], axis=-1, keepdims=True) * (0.5 / D)
            s_ref[3:4, :] = jnp.broadcast_to(tot, (1, D))

    row = lambda: pl.BlockSpec((tm, D), lambda i: (i, 0))
    return pl.pallas_call(
        body, name="loss_bwd", grid=(nt,),
        in_specs=[row(), row(), row(), _full((8, D)), _full((1, D))],
        out_specs=[row(), row(), _full((8, D))],
        out_shape=[jax.ShapeDtypeStruct((T, D), f32), jax.ShapeDtypeStruct((T, D), bf16), jax.ShapeDtypeStruct((8, D), f32)],
        compiler_params=_cp("arbitrary"),
    )(ffn, x1, tgt, modv, gf)


def _ffn_bwd(h2, dffn, u_t, u2_t, fcw, w_down):
    tt, ct = FFN_TT, FFN_CT
    nt = T // tt
    wn = tt + 2 * HALO
    half = tt // 2

    def body(dp, dc, dn, hc, uc_ref, u2p, u2c, u2n, cwg_ref, cwv_ref, wd_ref,
             dug_ref, duv_ref, dwu_ref, dwd_ref, dcwg_ref, dcwv_ref, dcbg_ref, dcbv_ref,
             dwin, d2win, dawin, accu, accd, act, du):
        t = pl.program_id(1)
        first, last = t == 0, t == nt - 1
        zero = jnp.zeros((HALO, D), bf16)
        dwin[0:HALO, :] = jnp.where(first, zero, dp[...])
        dwin[HALO:HALO + tt, :] = dc[...]
        dwin[HALO + tt:, :] = jnp.where(last, zero, dn[...])

        @pl.when(first)
        def _():
            for r in (accu, accd, dcwg_ref, dcwv_ref, dcbg_ref, dcbv_ref):
                r[...] = jnp.zeros_like(r)

        cw = jnp.concatenate([cwg_ref[...], cwv_ref[...]], axis=1)
        split = half + 2 * HALO
        for r0, r1 in ((0, split), (split, wn)):
            dawin[r0:r1, :] = _dg(dwin[r0:r1, :], wd_ref[...], NT)

        def grads(u2v, dact):
            gate, val = u2v[:, :ct], u2v[:, ct:]
            sg = _sigmoid(gate)
            silu = gate * sg
            return dact * val * (sg * (1.0 + gate * (1.0 - sg))), dact * silu, silu * val

        for blk, r0 in ((u2p, 0), (u2n, HALO + tt)):
            dgate, dval, _ = grads(blk[...], dawin[r0:r0 + HALO, :])
            d2win[r0:r0 + HALO, :ct] = dgate
            d2win[r0:r0 + HALO, ct:] = dval
        for p in range(2):
            rows = slice(p * half, (p + 1) * half)
            for r in range(p * half, (p + 1) * half, FFN_SUB):
                dgate, dval, av = grads(u2c[r:r + FFN_SUB, :], dawin[HALO + r:HALO + r + FFN_SUB, :])
                d2win[HALO + r:HALO + r + FFN_SUB, :ct] = dgate
                d2win[HALO + r:HALO + r + FFN_SUB, ct:] = dval
                act[r:r + FFN_SUB, :] = av.astype(bf16)
            accd[...] += _dg(act[rows, :], dc[rows, :], TN)

        def fold8(x):
            out = x[0:8]
            for k in range(8, FFN_SUB, 8):
                out = out + x[k:k + 8]
            return out

        for p in range(2):
            rows = slice(p * half, (p + 1) * half)
            sums = [jnp.zeros((8, 2 * ct), f32) for _ in range(4)]
            for r in range(p * half, (p + 1) * half, FFN_SUB):
                d2c, d2m, d2p = _row_neighbours(d2win, HALO + r, FFN_SUB)
                ucur = uc_ref[r:r + FFN_SUB, :].astype(f32)
                sums[0] = sums[0] + fold8(d2c)
                for kk, dd in enumerate((d2p, d2c, d2m)):
                    sums[1 + kk] = sums[1 + kk] + fold8(ucur * dd)
                du[r:r + FFN_SUB, :] = (d2p * cw[0:1, :] + d2c * cw[1:2, :] + d2m * cw[2:3, :]).astype(bf16)
            dcb = jnp.sum(sums[0], axis=0, keepdims=True)
            dcbg_ref[...] += dcb[:, :ct]
            dcbv_ref[...] += dcb[:, ct:]
            for kk in range(3):
                dck = jnp.sum(sums[1 + kk], axis=0, keepdims=True)
                dcwg_ref[kk:kk + 1, :] += dck[:, :ct]
                dcwv_ref[kk:kk + 1, :] += dck[:, ct:]
            dug_ref[rows, :] = du[rows, :ct]
            duv_ref[rows, :] = du[rows, ct:]
            accu[...] += _dg(du[rows, :], hc[rows, :], TN)

        @pl.when(last)
        def _():
            dwu_ref[0] = accu[0:ct, :].astype(bf16)
            dwu_ref[1] = accu[ct:, :].astype(bf16)
            dwd_ref[...] = accd[...].astype(bf16)

    per = tt // HALO
    prev = lambda w: (lambda c, t: (jnp.maximum(t * per - 1, 0), c if w else 0))
    nxt = lambda w: (lambda c, t: (jnp.minimum((t + 1) * per, T // HALO - 1), c if w else 0))
    tile = lambda: pl.BlockSpec((ct, D), lambda c, t: (c, 0))
    lane = lambda r, off: pl.BlockSpec((r, ct), lambda c, t: (0, c + off))
    return pl.pallas_call(
        body, name="ffn_bwd", grid=(FFN_NC, nt),
        in_specs=[pl.BlockSpec((HALO, D), prev(False)), pl.BlockSpec((tt, D), lambda c, t: (t, 0)), pl.BlockSpec((HALO, D), nxt(False)),
                  pl.BlockSpec((tt, D), lambda c, t: (t, 0)), pl.BlockSpec((tt, 2 * ct), lambda c, t: (t, c)),
                  pl.BlockSpec((HALO, 2 * ct), prev(True)), pl.BlockSpec((tt, 2 * ct), lambda c, t: (t, c)),
                  pl.BlockSpec((HALO, 2 * ct), nxt(True)), lane(3, 0), lane(3, FFN_NC), tile()],
        out_specs=[pl.BlockSpec((tt, ct), lambda c, t: (t, c)), pl.BlockSpec((tt, ct), lambda c, t: (t, c)),
                   pl.BlockSpec((2, ct, D), lambda c, t: (0, c, 0)), tile(), lane(3, 0), lane(3, 0), lane(1, 0), lane(1, 0)],
        out_shape=[jax.ShapeDtypeStruct((T, F), bf16), jax.ShapeDtypeStruct((T, F), bf16),
                   jax.ShapeDtypeStruct((2, F, D), bf16), jax.ShapeDtypeStruct((F, D), bf16),
                   jax.ShapeDtypeStruct((3, F), f32), jax.ShapeDtypeStruct((3, F), f32),
                   jax.ShapeDtypeStruct((1, F), f32), jax.ShapeDtypeStruct((1, F), f32)],
        scratch_shapes=[pltpu.VMEM((wn, D), bf16), pltpu.VMEM((wn, 2 * ct), f32), pltpu.VMEM((wn, ct), f32),
                        pltpu.VMEM((2 * ct, D), f32), pltpu.VMEM((ct, D), f32),
                        pltpu.VMEM((tt, ct), bf16), pltpu.VMEM((tt, 2 * ct), bf16)],
        compiler_params=_cp("parallel", "arbitrary"),
    )(dffn, dffn, dffn, h2, u_t, u2_t, u2_t, u2_t, fcw, fcw, w_down)


def _norm_bwd(dh, xv, gain, sh_sc, rstd):
    xh = xv * rstd
    n = xh * gain
    dn = dh * (1.0 + sh_sc)
    dxh = dn * gain
    dx = rstd * (dxh - xh * jnp.mean(dxh * xh, axis=-1, keepdims=True))
    return (dx, jnp.sum(dh, axis=0, keepdims=True), jnp.sum(dh * n, axis=0, keepdims=True),
            jnp.sum(dn * xh, axis=0, keepdims=True))


def _norm2_bwd(dug, duv, w_upT, x1, dx2, proj, w_out, y_na, y_cv, modv, g2):
    tm = 256
    nt = T // tm

    def body(dug_ref, duv_ref, w_ref, x1_ref, dx2_ref, pj_ref, wo_ref, ya_ref, yc_ref, mod_ref, g_ref,
             dx1_ref, dya_ref, dyc_ref, dwo_ref, s_ref, acc):
        i = pl.program_id(0)

        @pl.when(i == 0)
        def _():
            s_ref[...] = jnp.zeros_like(s_ref)
            acc[...] = jnp.zeros_like(acc)

        dh2 = _dot(dug_ref[...], w_ref[0:F, :]) + _dot(duv_ref[...], w_ref[F:F2, :])
        x1 = x1_ref[...]
        rstd = lax.rsqrt(jnp.mean(x1 * x1, axis=-1, keepdims=True) + EPS)
        dxn, dsh, dsc, dgn = _norm_bwd(dh2, x1, g_ref[...], mod_ref[4:5, :], rstd)
        dx1 = dx2_ref[...] + dxn
        dx1_ref[...] = dx1
        dpj = (dx1 * mod_ref[2:3, :]).astype(bf16)
        dyc = _dg(dpj, wo_ref[...], NT)
        dya_ref[...] = dyc[:, :DA].astype(bf16)
        dyc_ref[...] = dyc[:, DA:]
        acc[0:DA, :] += _dg(ya_ref[...], dpj, TN)
        acc[DA:D, :] += _dg(yc_ref[...], dpj, TN)

        @pl.when(i == nt - 1)
        def _():
            dwo_ref[...] = acc[...].astype(bf16)

        s_ref[0:1, :] += dsh
        s_ref[1:2, :] += dsc
        s_ref[2:3, :] += dgn
        s_ref[3:4, :] += jnp.sum(dx1 * pj_ref[...].astype(f32), axis=0, keepdims=True)

    row = lambda w: pl.BlockSpec((tm, w), lambda i: (i, 0))
    return pl.pallas_call(
        body, name="norm2_bwd", grid=(nt,),
        in_specs=[row(F), row(F), _full((F2, D)), row(D), row(D), row(D), _full((D, D)), row(DA), row(DA),
                  _full((8, D)), _full((1, D))],
        out_specs=[row(D), row(DA), row(DA), _full((D, D)), _full((8, D))],
        out_shape=[jax.ShapeDtypeStruct((T, D), f32), jax.ShapeDtypeStruct((T, DA), bf16), jax.ShapeDtypeStruct((T, DA), f32),
                   jax.ShapeDtypeStruct((D, D), bf16), jax.ShapeDtypeStruct((8, D), f32)],
        scratch_shapes=[pltpu.VMEM((D, D), f32)],
        compiler_params=_cp("arbitrary"),
    )(dug, duv, w_upT, x1, dx2, proj, w_out, y_na, y_cv, modv, g2)


def _conf_bwd(a, g, cv, dy, conv_w, ln_g, ln_b, blocks):
    tt = CONV_TT
    nt = T // tt
    sub = 32
    wn = tt + 2 * HALO
    nb = len(blocks)

    def body(ap, ac, an, gp, gc, gn, cp_, cc, cn, dp, dc, dn, w_ref, lg_ref, lb_ref, *rest):
        g_refs, (da_ref, dg_ref, dcw_ref, s_ref) = rest[:nb], rest[nb:nb + 4]
        recv_refs, (urot, drot, wacc), a2a_sems = rest[nb + 4:2 * nb + 4], rest[2 * nb + 4:2 * nb + 7], rest[2 * nb + 7:]
        i = pl.program_id(0)
        first, last = i == 0, i == nt - 1
        plans = [_a2a_plan(g_refs[k], recv_refs[k], *a2a_sems[3 * k:3 * k + 3]) for k in range(nb)]
        for start, _ in plans:
            pl.when(first)(start)

        @pl.when(first)
        def _():
            s_ref[...] = jnp.zeros_like(s_ref)
            wacc[...] = jnp.zeros_like(wacc)

        lg, lb = lg_ref[...], lb_ref[...]

        def ln_bwd(cvv, dyv):
            mu = jnp.mean(cvv, axis=-1, keepdims=True)
            xc = cvv - mu
            rstd = lax.rsqrt(jnp.mean(xc * xc, axis=-1, keepdims=True) + EPS)
            yn = xc * rstd
            z = yn * lg + lb
            sz = _sigmoid(z)
            dz = dyv * (sz * (1.0 + z * (1.0 - sz)))
            dyn = dz * lg
            dcv = rstd * (dyn - jnp.mean(dyn, axis=-1, keepdims=True) - yn * jnp.mean(dyn * yn, axis=-1, keepdims=True))
            return dcv, dz, yn

        urot[0, 0:HALO, :] = jnp.where(first, 0.0, ap[...] * _sigmoid(gp[...]))
        urot[0, HALO + tt:, :] = jnp.where(last, 0.0, an[...] * _sigmoid(gn[...]))
        drot[0, 0:HALO, :] = jnp.where(first, 0.0, ln_bwd(cp_[...], dp[...])[0])
        drot[0, HALO + tt:, :] = jnp.where(last, 0.0, ln_bwd(cn[...], dn[...])[0])
        for s in range(tt // sub):
            rr = pl.ds(s * sub, sub)
            urot[0, pl.ds(HALO + s * sub, sub), :] = ac[rr, :] * _sigmoid(gc[rr, :])
            dcv, dz, yn = ln_bwd(cc[rr, :], dc[rr, :])
            drot[0, pl.ds(HALO + s * sub, sub), :] = dcv
            s_ref[0:1, :] += jnp.sum(dcv, axis=0, keepdims=True)
            s_ref[1:2, :] += jnp.sum(dz * yn, axis=0, keepdims=True)
            s_ref[2:3, :] += jnp.sum(dz, axis=0, keepdims=True)
        _shifted_copies(urot, wn)
        _shifted_copies(drot, wn)
        w = w_ref[...]
        for s in range(tt // sub):
            rr = pl.ds(s * sub, sub)
            dcv = drot[0, pl.ds(HALO + s * sub, sub), :]
            acc = jnp.zeros((sub, DA), f32)
            for j in range(CW):
                ad, bd = divmod(2 * HALO - 1 - j, 8)
                au, bu = divmod(1 + j, 8)
                acc = acc + drot[bd, pl.ds(s * sub + 8 * ad, sub), :] * w[j:j + 1, :]
                part = urot[bu, pl.ds(s * sub + 8 * au, sub), :] * dcv
                wacc[j] += part[0:8] + part[8:16] + part[16:24] + part[24:32]
            av, gv = ac[rr, :], gc[rr, :]
            sg = _sigmoid(gv)
            da_ref[rr, :] = (acc * sg).astype(bf16)
            dg_ref[rr, :] = (acc * av * sg * (1.0 - sg)).astype(bf16)

        @pl.when(last)
        def _():
            for j in range(CW):
                dcw_ref[j:j + 1, :] = jnp.sum(wacc[j], axis=0, keepdims=True)
            dcw_ref[CW:CW + 1, :] = jnp.zeros((1, DA), f32)

        for _, finish in plans:
            pl.when(last)(finish)

    hs = _halo_specs(tt, DA, T // HALO)
    hbm = pl.BlockSpec(memory_space=pl.ANY)
    return pl.pallas_call(
        body, name="conf_bwd", grid=(nt,),
        in_specs=hs * 4 + [_full((CW, DA)), _full((1, DA)), _full((1, DA))] + [hbm] * nb,
        out_specs=[pl.BlockSpec((tt, DA), lambda i: (i, 0)), pl.BlockSpec((tt, DA), lambda i: (i, 0)),
                   _full((CW + 1, DA)), _full((8, DA))] + [hbm] * nb,
        out_shape=[jax.ShapeDtypeStruct((T, DA), bf16), jax.ShapeDtypeStruct((T, DA), bf16),
                   jax.ShapeDtypeStruct((CW + 1, DA), f32), jax.ShapeDtypeStruct((8, DA), f32)]
                  + [jax.ShapeDtypeStruct(b.shape, b.dtype) for b in blocks],
        scratch_shapes=[pltpu.VMEM((8, wn, DA), f32), pltpu.VMEM((8, wn, DA), f32), pltpu.VMEM((CW, 8, DA), f32)]
                       + _a2a_scratch() * nb,
        compiler_params=_cp("arbitrary"),
    )(a, a, a, g, g, g, cv, cv, cv, dy, dy, dy, conv_w, ln_g, ln_b, *blocks)


def _attn_bwd(q, k, v, y, dy, lse, bias_tab, blocks):
    zr = 256
    nb = len(blocks)

    def body(q_ref, k_ref, v_ref, y_ref, dy_ref, lse_ref, b_ref, *rest):
        g_refs, (dq_ref, dk_hbm, dv_hbm, db_ref) = rest[:nb], rest[nb:nb + 4]
        recv_refs, (dk_s, dv_s, sem), a2a_sems = rest[nb + 4:2 * nb + 4], rest[2 * nb + 4:2 * nb + 7], rest[2 * nb + 7:]
        r = pl.program_id(0)
        plans = [_a2a_plan(g_refs[i], recv_refs[i], *a2a_sems[3 * i:3 * i + 3]) for i in range(nb)]
        for start, _ in plans:
            pl.when(r == 0)(start)

        @pl.when(r == 0)
        def _():
            def z(i, _):
                rr = pl.ds(pl.multiple_of(i * zr, zr), zr)
                dk_s[rr, :] = jnp.zeros((zr, DA), f32)
                dv_s[rr, :] = jnp.zeros((zr, DA), f32)
                return 0
            lax.fori_loop(0, TA // zr, z, 0)

        @pl.when((r <= WR // 2) | (r > GW - WR // 2))
        def _():
            db_ref[...] = jnp.zeros_like(db_ref)

        ks = pl.multiple_of(_win_start(r) * GW, GW)
        win = pl.ds(ks, WR * GW)
        qq, yy, dyy, lse_v = q_ref[...], y_ref[...], dy_ref[...], lse_ref[...]
        lo = lax.broadcasted_iota(jnp.int32, (GW, 2 * HD), 1) < HD
        ops, heads = [], []
        for pr in range(NH // 2):
            ps = slice(pr * 2 * HD, (pr + 1) * 2 * HD)
            q2, do2 = qq[:, ps], dyy[:, ps]
            prod = do2.astype(f32) * yy[:, ps].astype(f32)
            kw, vw = k_ref[win, ps], v_ref[win, ps]
            kc, vc = k_ref[T:TA, ps], v_ref[T:TA, ps]
            ops.append((kw, kc))
            for s in range(2):
                h = 2 * pr + s
                msk = lo if s == 0 else ~lo
                qm = jnp.where(msk, q2, jnp.zeros_like(q2))
                dom = jnp.where(msk, do2, jnp.zeros_like(do2))
                delta = jnp.sum(jnp.where(msk, prod, 0.0), axis=-1, keepdims=True)
                heads.append((qm, dom, delta, _dg(qm, kw, NT), _dg(qm, kc, NT), _dg(dom, vw, NT), _dg(dom, vc, NT)))
        grads = []
        for h, (qm, dom, delta, sl, sc, dpl, dpc) in enumerate(heads):
            lh = lse_v[:, h:h + 1]
            pl_ = jnp.exp(sl * SCALE + b_ref[0, h] - lh)
            pc = jnp.exp(sc * SCALE - lh)
            dsl = pl_ * (dpl - delta)
            dsc = pc * (dpc - delta)
            db_ref[0, h] += dsl
            grads.append((qm, dom, pl_.astype(bf16), pc.astype(bf16), dsl.astype(bf16), dsc.astype(bf16)))
        for pr in range(NH // 2):
            ps = slice(pr * 2 * HD, (pr + 1) * 2 * HD)
            kw, kc = ops[pr]
            dqs = []
            dkw = dvw = dkc = dvc = None
            for s in range(2):
                qm, dom, plb, pcb, dslb, dscb = grads[2 * pr + s]
                dqs.append(_dot(dslb, kw) + _dot(dscb, kc))
                parts = (_dg(dslb, qm, TN), _dg(plb, dom, TN), _dg(dscb, qm, TN), _dg(pcb, dom, TN))
                if s == 0:
                    dkw, dvw, dkc, dvc = parts
                else:
                    dkw, dvw, dkc, dvc = dkw + parts[0], dvw + parts[1], dkc + parts[2], dvc + parts[3]
            dq_ref[:, ps] = (jnp.where(lo, dqs[0], dqs[1]) * SCALE).astype(bf16)
            dk_s[win, ps] += dkw * SCALE
            dv_s[win, ps] += dvw
            dk_s[T:TA, ps] += dkc * SCALE
            dv_s[T:TA, ps] += dvc

        @pl.when(r == GW - 1)
        def _():
            c1 = pltpu.make_async_copy(dk_s, dk_hbm, sem.at[0])
            c2 = pltpu.make_async_copy(dv_s, dv_hbm, sem.at[1])
            c1.start()
            c2.start()
            c1.wait()
            c2.wait()

        for _, finish in plans:
            pl.when(r == GW - 1)(finish)

    rowq = lambda: pl.BlockSpec((GW, DA), lambda r: (r, 0))
    hbm = pl.BlockSpec(memory_space=pl.ANY)
    return pl.pallas_call(
        body, name="attn_bwd", grid=(GW,),
        in_specs=[rowq(), _full((TA, DA)), _full((TA, DA)), rowq(), rowq(), pl.BlockSpec((GW, NH), lambda r: (r, 0)),
                  pl.BlockSpec((1, NH, GW, WR * GW), lambda r: (_pattern(r), 0, 0, 0))] + [hbm] * nb,
        out_specs=[rowq(), hbm, hbm, pl.BlockSpec((1, NH, GW, WR * GW), lambda r: (_pattern(r), 0, 0, 0))] + [hbm] * nb,
        out_shape=[jax.ShapeDtypeStruct((T, DA), bf16), jax.ShapeDtypeStruct((TA, DA), f32), jax.ShapeDtypeStruct((TA, DA), f32),
                   jax.ShapeDtypeStruct((8, NH, GW, WR * GW), f32)] + [jax.ShapeDtypeStruct(b.shape, b.dtype) for b in blocks],
        scratch_shapes=[pltpu.VMEM((TA, DA), f32), pltpu.VMEM((TA, DA), f32), pltpu.SemaphoreType.DMA((2,))] + _a2a_scratch() * nb,
        compiler_params=_cp("arbitrary"),
    )(q, k, v, y, dy, lse, bias_tab, *blocks)


def _sum_blocks(recv, name):
    _, r, n = recv.shape
    tr = next(cand for cand in (176, 128, 64, 32, 16) if r % cand == 0)

    def body(a_ref, o_ref):
        acc = a_ref[0].astype(f32)
        for d in range(1, NDEV):
            acc = acc + a_ref[d].astype(f32)
        o_ref[...] = acc

    return pl.pallas_call(
        body, name=name, grid=(r // tr,),
        in_specs=[pl.BlockSpec((NDEV, tr, n), lambda i: (0, i, 0))],
        out_specs=pl.BlockSpec((tr, n), lambda i: (i, 0)),
        out_shape=jax.ShapeDtypeStruct((r, n), f32),
        compiler_params=_cp("parallel"),
    )(recv)


def _rpb_reduce(dbias):
    rev = np.zeros((WR * GW, WR * GW), np.float32)
    for i in range(WR):
        for kk in range(GW):
            rev[i * GW + kk, i * GW + GW - 1 - kk] = 1.0

    def body(d_ref, rev_ref, o_ref):
        rv = rev_ref[...]
        for h in range(NH):
            dv = d_ref[0, h]
            r0 = dv.astype(bf16)
            e1 = dv - r0.astype(f32)
            r1 = e1.astype(bf16)
            r2 = (e1 - r1.astype(f32)).astype(bf16)
            rr = _dot(r0, rv) + _dot(r1, rv) + _dot(r2, rv)
            skew = pltpu.roll(rr, 0, 1, stride=1, stride_axis=0)
            o_ref[0, h:h + 1, :] = jnp.sum(skew, axis=0, keepdims=True)

    return pl.pallas_call(
        body, name="rpb_reduce", grid=(8,),
        in_specs=[pl.BlockSpec((1, NH, GW, WR * GW), lambda p: (p, 0, 0, 0)), _full((WR * GW, WR * GW))],
        out_specs=pl.BlockSpec((1, NH, WR * GW), lambda p: (p, 0, 0)),
        out_shape=jax.ShapeDtypeStruct((8, NH, WR * GW), f32),
        compiler_params=_cp("parallel"),
    )(dbias, jnp.asarray(rev, dtype=bf16))


def _norm1_bwd(dq, dk, dv, da, dg, w_inT, x0, ctx0, h, dx1, modv, g1):
    tm = 256
    nt = TA // tm
    nx = T // tm

    def body(dq_ref, dk_ref, dv_ref, da_ref, dg_ref, w_ref, x_ref, c_ref, h_ref, dx1_ref, mod_ref, g_ref,
             dx_ref, dwo_ref, s_ref, dw_ref):
        i = pl.program_id(0)
        is_ctx = i == nt - 1

        @pl.when(i == 0)
        def _():
            s_ref[...] = jnp.zeros_like(s_ref)
            dw_ref[...] = jnp.zeros_like(dw_ref)

        hb = h_ref[...]
        dkb, dvb = dk_ref[...].astype(bf16), dv_ref[...].astype(bf16)
        dw_ref[DA:2 * DA, :] += _dg(dkb, hb, TN)
        dw_ref[2 * DA:3 * DA, :] += _dg(dvb, hb, TN)
        dh_kv = _dot(dkb, w_ref[DA:2 * DA, :]) + _dot(dvb, w_ref[2 * DA:3 * DA, :])
        gain = g_ref[...]

        @pl.when(is_ctx)
        def _():
            xv = c_ref[...]
            rstd = lax.rsqrt(jnp.mean(xv * xv, axis=-1, keepdims=True) + EPS)
            _, dsh, dsc, dgn = _norm_bwd(dh_kv, xv, gain, mod_ref[7:8, :], rstd)
            s_ref[2:3, :] += dgn
            s_ref[3:4, :] += dsh
            s_ref[4:5, :] += dsc
            dwo_ref[...] = dw_ref[...].astype(bf16)

        @pl.when(jnp.logical_not(is_ctx))
        def _():
            dqb, dab, dgb = dq_ref[...], da_ref[...], dg_ref[...]
            dw_ref[0:DA, :] += _dg(dqb, hb, TN)
            dw_ref[3 * DA:4 * DA, :] += _dg(dab, hb, TN)
            dw_ref[4 * DA:5 * DA, :] += _dg(dgb, hb, TN)
            dh = (dh_kv + _dot(dqb, w_ref[0:DA, :]) + _dot(dab, w_ref[3 * DA:4 * DA, :])
                  + _dot(dgb, w_ref[4 * DA:5 * DA, :]))
            xv = x_ref[...]
            rstd = lax.rsqrt(jnp.mean(xv * xv, axis=-1, keepdims=True) + EPS)
            dxn, dsh, dsc, dgn = _norm_bwd(dh, xv, gain, mod_ref[1:2, :], rstd)
            dx_ref[...] = dx1_ref[...] + dxn
            s_ref[0:1, :] += dsh
            s_ref[1:2, :] += dsc
            s_ref[2:3, :] += dgn

    row = lambda w: pl.BlockSpec((tm, w), lambda i: (i, 0))
    lrow = lambda w: pl.BlockSpec((tm, w), lambda i: (jnp.minimum(i, nx - 1), 0))
    return pl.pallas_call(
        body, name="norm1_bwd", grid=(nt,),
        in_specs=[lrow(DA), row(DA), row(DA), lrow(DA), lrow(DA), _full((5 * DA, D)), lrow(D), _full((TC, D)), row(D),
                  lrow(D), _full((8, D)), _full((1, D))],
        out_specs=[lrow(D), _full((5 * DA, D)), _full((8, D))],
        out_shape=[jax.ShapeDtypeStruct((T, D), f32), jax.ShapeDtypeStruct((5 * DA, D), bf16), jax.ShapeDtypeStruct((8, D), f32)],
        scratch_shapes=[pltpu.VMEM((5 * DA, D), f32)],
        compiler_params=_cp("arbitrary"),
    )(dq, dk, dv, da, dg, w_inT, x0, ctx0, h, dx1, modv, g1)


def _adamw(w, g, m, v, name):
    r, c = w.shape
    tr = r
    for cand in (256, 128, 64, 32, 16, 8):
        if r % cand == 0 and r > cand:
            tr = cand
            break

    def body(w_ref, g_ref, m_ref, v_ref, d_ref, nm_ref, nv_ref):
        gv = g_ref[...]
        nm = ADAM_B1 * m_ref[...] + (1.0 - ADAM_B1) * gv
        nv = ADAM_B2 * v_ref[...] + (1.0 - ADAM_B2) * (gv * gv)
        m_hat = nm / (1.0 - ADAM_B1 ** ADAM_STEP)
        v_hat = nv / (1.0 - ADAM_B2 ** ADAM_STEP)
        d_ref[...] = -ADAM_LR * (m_hat / (jnp.sqrt(v_hat) + ADAM_EPS) + ADAM_WD * w_ref[...])
        nm_ref[...] = nm
        nv_ref[...] = nv

    spec = pl.BlockSpec((tr, c), lambda i: (i, 0))
    return pl.pallas_call(
        body, name=name, grid=(r // tr,),
        in_specs=[spec] * 4, out_specs=[spec] * 3,
        out_shape=[jax.ShapeDtypeStruct((r, c), f32)] * 3,
        compiler_params=_cp("parallel"),
    )(w, g, m, v)


def _pad_rows128(vec):
    n = vec.shape[0]
    rows = -(-n // 1024) * 8
    return jnp.pad(vec, (0, rows * 128 - n)).reshape(rows, 128)


def _grad_rpb(dbias):
    lane_map, r_hot = _rpb_tables()
    return jnp.einsum("phl,lic,pir->hrc", _rpb_reduce(dbias), jnp.asarray(lane_map), jnp.asarray(r_hot),
                      precision=lax.Precision.HIGHEST)


def kernel(x, c, ctx, c_ctx, w_mod, b_mod, g_norm1, w_in, rpb, conv_w, conv_b, ln_g, ln_b, w_out, g_norm2, w_up, ffn_conv_w, ffn_conv_b, w_down, g_final, loss_target, m_c_ctx, m_w_mod, m_b_mod, m_g_norm1, m_w_in, m_rpb, m_conv_w, m_conv_b, m_ln_g, m_ln_b, m_w_out, m_g_norm2, m_w_up, m_ffn_conv_w, m_ffn_conv_b, m_w_down, m_g_final, v_c_ctx, v_w_mod, v_b_mod, v_g_norm1, v_w_in, v_rpb, v_conv_w, v_conv_b, v_ln_g, v_ln_b, v_w_out, v_g_norm2, v_w_up, v_ffn_conv_w, v_ffn_conv_b, v_w_down, v_g_final):
    me = 4 * lax.axis_index("x") + 2 * lax.axis_index("y") + lax.axis_index("c")
    nmod = w_mod.shape[2]
    n_in = w_in.shape[2]
    n_out = w_out.shape[1]
    n_up = w_up.shape[2]
    n_dn = w_down.shape[1]
    n_cw = conv_w.shape[2]

    w_inT, c_all = _allgather2([w_in[0].T.astype(bf16), c.reshape(8, 128)], "ag_w_in")

    c_all = c_all.reshape(NDEV, D)
    cvec = jnp.concatenate([c_all, c_ctx[None, :], jnp.zeros((7, D), f32)], axis=0)
    b_sh = lax.dynamic_slice(b_mod, (0, me * nmod), (1, nmod))
    mod_sh = _mod_fwd(cvec, w_mod[0], b_sh)
    n_modp = 16 * nmod
    payload = jnp.concatenate([mod_sh.reshape(-1), conv_w[0].reshape(-1), ffn_conv_w[0].reshape(-1)])
    flat = _small_allgather(_pad_rows128(payload), "ag_mod").reshape(NDEV, -1)
    mod_all = flat[:, :n_modp].reshape(NDEV, 16, nmod).transpose(1, 0, 2).reshape(16, 6 * D)
    mod_me = lax.dynamic_index_in_dim(mod_all, me, 0, keepdims=False).reshape(6, D)
    mod_c = mod_all[8]
    modv = jnp.concatenate([mod_me, mod_c[None, 0:D], mod_c[None, D:2 * D]], axis=0)
    o1 = n_modp + CW * n_cw
    conv_w_f = flat[:, n_modp:o1].reshape(NDEV, CW, n_cw).transpose(1, 0, 2).reshape(CW, DA)
    fcw_f = flat[:, o1:o1 + 3 * n_up].reshape(NDEV, 3, n_up).transpose(1, 0, 2).reshape(3, F2)

    x0, ctx0 = x[0], ctx[0]
    h, q, k, v, a, g = _in_proj(x0, ctx0, g_norm1, modv, w_inT)
    bias_tab = _bias_table(rpb[0])
    y_cv, cv, w_out_f = _conf_fwd(a, g, conv_w_f, conv_b, ln_g, ln_b, [w_out[0].astype(bf16)])
    y_na, lse, w_upT, w_down_f = _attn_fwd(q, k, v, bias_tab, [w_up[0].T.astype(bf16), w_down[0].astype(bf16)])
    x1, proj, h2 = _out_proj(x0, y_na, y_cv, w_out_f, modv, g_norm2)
    ffn, u_t, u2_t = _ffn_fwd(h2, w_upT, fcw_f, ffn_conv_b, w_down_f)
    dx2, dffn, s_loss = _loss_bwd(ffn, x1, loss_target[0], modv, g_final[None, :])

    dug, duv, dw_up, dw_down, dcwg, dcwv, dcbg, dcbv = _ffn_bwd(h2, dffn, u_t, u2_t, fcw_f, w_down_f)
    dx1, dy_na, dy_cv, dw_out, s_n2 = _norm2_bwd(dug, duv, w_upT, x1, dx2, proj, w_out_f, y_na, y_cv, modv, g_norm2)
    da, dg, dcw, s_cf, rv_down = _conf_bwd(a, g, cv, dy_cv, conv_w_f, ln_g, ln_b, [dw_down.reshape(NDEV, n_dn, D)])
    dq, dk, dv, dbias, rv_up, rv_out = _attn_bwd(q, k, v, y_na, dy_na, lse, bias_tab,
                                                 [dw_up.reshape(NDEV, n_up, D), dw_out.reshape(NDEV, n_out, D)])
    g_w_down = _sum_blocks(rv_down, "sum_w_down")
    g_w_out = _sum_blocks(rv_out, "sum_w_out")
    g_w_upT = _sum_blocks(rv_up, "sum_w_up")
    grad_rpb_part = _grad_rpb(dbias)
    grad_x, dw_inT, s_n1 = _norm1_bwd(dq, dk, dv, da, dg, w_inT, x0, ctx0, h, dx1, modv, g_norm1)
    grad_x = grad_x[None]
    dfcw = jnp.concatenate([dcwg, dcwv], axis=1)
    dfcb = jnp.concatenate([dcbg[0], dcbv[0]])
    small = jnp.concatenate([dcw[:CW].reshape(CW, NDEV, n_cw).transpose(1, 0, 2).reshape(NDEV, CW * n_cw),
                             dfcw.reshape(3, NDEV, n_up).transpose(1, 0, 2).reshape(NDEV, 3 * n_up)], axis=1)
    small = jnp.pad(small.reshape(NDEV, 4, D), ((0, 0), (0, 12), (0, 0))).astype(bf16)
    dmod = jnp.concatenate([s_n1[0], s_n1[1], s_n2[3], s_n2[0], s_n2[1], s_loss[1]])
    dmodc = jnp.concatenate([s_n1[3], s_n1[4]])
    parts = [dmodc, s_n1[2], grad_rpb_part.reshape(-1), s_cf[0], s_cf[1], s_cf[2], s_n2[2], dfcb, s_loss[0], s_loss[3, 0:1]]
    sizes = [p.shape[0] for p in parts]
    pvec = _pad_rows128(jnp.concatenate([dmod] + parts))
    r_a, gath = _reduce_scatter2(jnp.concatenate([dw_inT.reshape(NDEV, n_in, D), small], axis=1), [pvec], "rs_w_in")
    g_w_inT = r_a[:n_in]
    sm = r_a[n_in:n_in + 4].reshape(-1)
    g_conv_w = sm[:CW * n_cw].reshape(CW, n_cw)
    g_fcw = sm[CW * n_cw:].reshape(3, n_up)
    tot = _sum_rows8(gath, "sum_small").reshape(-1)
    dmod_all = gath.reshape(NDEV, -1)[:, :6 * D]
    offs = np.cumsum([6 * D] + sizes)
    pick = lambda j: tot[offs[j]:offs[j + 1]]
    dmodc_t = jnp.pad(pick(0), (0, 4 * D))
    g_b_mod = (tot[:6 * D] + dmodc_t)[None, :]
    g_g_norm1 = pick(1)[None, :]
    g_rpb = pick(2).reshape(1, NH, 2 * WR - 1, 2 * NCOL - 1)
    g_conv_b, g_ln_g, g_ln_b = pick(3)[None, :], pick(4)[None, :], pick(5)[None, :]
    g_g_norm2 = pick(6)[None, :]
    g_fcb = pick(7)[None, :]
    g_g_final = pick(8)
    loss = pick(9)[0]
    dm_rows = jnp.concatenate([dmod_all, dmodc_t[None, :], jnp.zeros((7, 6 * D), f32)], axis=0)
    dm_sh = lax.dynamic_slice(dm_rows, (0, me * nmod), (16, nmod))
    g_w_mod, gc_part = _mod_bwd(cvec, dm_sh, w_mod[0])
    gc_sum = _sum_rows8(_small_allgather(gc_part[0].reshape(8, 128), "ag_cctx"), "sum_cctx").reshape(D)
    sg_c = _sigmoid(c_ctx)
    g_c_ctx = gc_sum * (sg_c * (1.0 + c_ctx * (1.0 - sg_c)))

    big = [("w_mod", w_mod[0], g_w_mod, m_w_mod[0], v_w_mod[0]), ("w_in", w_in[0].T, g_w_inT, m_w_in[0].T, v_w_in[0].T),
           ("w_out", w_out[0], g_w_out, m_w_out[0], v_w_out[0]), ("w_up", w_up[0].T, g_w_upT, m_w_up[0].T, v_w_up[0].T),
           ("w_down", w_down[0], g_w_down, m_w_down[0], v_w_down[0])]
    upd = {n: _adamw(wv, gv, mv, vv, "adamw_" + n) for n, wv, gv, mv, vv in big}
    for n in ("w_in", "w_up"):
        upd[n] = tuple(arr.T for arr in upd[n])
    g_w_in, g_w_up = g_w_inT.T, g_w_upT.T
    smalls = [("c_ctx", c_ctx, g_c_ctx, m_c_ctx, v_c_ctx), ("b_mod", b_mod, g_b_mod, m_b_mod, v_b_mod),
              ("g_norm1", g_norm1, g_g_norm1, m_g_norm1, v_g_norm1), ("rpb", rpb, g_rpb, m_rpb, v_rpb),
              ("conv_w", conv_w, g_conv_w[None], m_conv_w, v_conv_w), ("conv_b", conv_b, g_conv_b, m_conv_b, v_conv_b),
              ("ln_g", ln_g, g_ln_g, m_ln_g, v_ln_g), ("ln_b", ln_b, g_ln_b, m_ln_b, v_ln_b),
              ("g_norm2", g_norm2, g_g_norm2, m_g_norm2, v_g_norm2),
              ("ffn_conv_w", ffn_conv_w, g_fcw[None], m_ffn_conv_w, v_ffn_conv_w),
              ("ffn_conv_b", ffn_conv_b, g_fcb, m_ffn_conv_b, v_ffn_conv_b), ("g_final", g_final, g_g_final, m_g_final, v_g_final)]
    packed = [_pad_rows128(jnp.concatenate([t[j].reshape(-1) for t in smalls])) for j in (1, 2, 3, 4)]
    sd, sm_, sv = _adamw(*packed, "adamw_small")
    so = np.cumsum([0] + [int(np.prod(t[1].shape)) for t in smalls])
    for j, t in enumerate(smalls):
        shp = t[1].shape
        upd[t[0]] = tuple(arr.reshape(-1)[so[j]:so[j + 1]].reshape(shp) for arr in (sd, sm_, sv))
    grads = {"c_ctx": g_c_ctx, "w_mod": g_w_mod[None], "b_mod": g_b_mod, "g_norm1": g_g_norm1, "w_in": g_w_in[None],
             "rpb": g_rpb, "conv_w": g_conv_w[None], "conv_b": g_conv_b, "ln_g": g_ln_g, "ln_b": g_ln_b,
             "w_out": g_w_out[None], "g_norm2": g_g_norm2, "w_up": g_w_up[None], "ffn_conv_w": g_fcw[None],
             "ffn_conv_b": g_fcb, "w_down": g_w_down[None], "g_final": g_g_final}
    names = ["c_ctx", "w_mod", "b_mod", "g_norm1", "w_in", "rpb", "conv_w", "conv_b", "ln_g", "ln_b", "w_out", "g_norm2",
             "w_up", "ffn_conv_w", "ffn_conv_b", "w_down", "g_final"]
    shapes = {n: grads[n].shape for n in names}
    outs = [loss, grad_x] + [grads[n] for n in names]
    for j in range(3):
        outs += [upd[n][j].reshape(shapes[n]) for n in names]
    return tuple(outs)
```

```python
import functools

import numpy as np
import jax
import jax.numpy as jnp
from jax import lax
from jax.experimental import pallas as pl
from jax.experimental.pallas import tpu as pltpu

f32 = jnp.float32
bf16 = jnp.bfloat16

D = 1024
T = 4096
TC = 256
TA = T + TC
DA = 512
NH = 8
HD = 64
GW = 64
WR = 8
NCOL = 16
F = 2816
F2 = 2 * F
CW = 31
NDEV = 8
EPS = 1e-6
SCALE = HD ** -0.5
NEG = -1e30
MESH = pl.DeviceIdType.MESH

NT = (((1,), (1,)), ((), ()))
TN = (((0,), (0,)), ((), ()))

ADAM_LR, ADAM_B1, ADAM_B2, ADAM_EPS, ADAM_WD, ADAM_STEP = 0.001, 0.9, 0.999, 1e-08, 0.01, 10

VMEM_LIMIT = 56 * 1024 * 1024


def _cp(*sem):
    return pltpu.CompilerParams(dimension_semantics=sem or None, vmem_limit_bytes=VMEM_LIMIT)


def _dot(a, b):
    return jnp.dot(a, b, preferred_element_type=f32)


def _dg(a, b, dims):
    return lax.dot_general(a, b, dims, preferred_element_type=f32)


def _sigmoid(x):
    return 1.0 / (1.0 + jnp.exp(-x))


def _full(shape):
    n = len(shape)
    return pl.BlockSpec(shape, lambda *_: (0,) * n)


def _my_pos():
    return lax.axis_index("x"), lax.axis_index("y"), lax.axis_index("c")


def _small_gather_plan(v_ref, out_ref, send_sems, recv_sems):
    x, y, c = _my_pos()
    me = 4 * x + 2 * y + c
    peers = []
    for k in range(1, NDEV):
        kx, ky, kc = (k >> 2) & 1, (k >> 1) & 1, k & 1
        peers.append((x ^ kx, y ^ ky, c ^ kc))

    def copy(k, slot, to):
        return pltpu.make_async_remote_copy(
            src_ref=v_ref, dst_ref=out_ref.at[slot], send_sem=send_sems.at[k], recv_sem=recv_sems.at[k],
            device_id=to, device_id_type=MESH)

    def start():
        out_ref[me] = v_ref[...]
        for k, p in enumerate(peers):
            copy(k, me, p).start()

    def finish():
        for k, (px, py, pc) in enumerate(peers):
            copy(k, 4 * px + 2 * py + pc, (x, y, c)).wait_recv()
        for k, p in enumerate(peers):
            copy(k, me, p).wait_send()

    return start, finish


def _small_gather_scratch():
    return [pltpu.SemaphoreType.DMA((NDEV - 1,)), pltpu.SemaphoreType.DMA((NDEV - 1,))]


def _small_allgather(v, name):
    n = v.shape[0]

    def body(v_ref, out_ref, send_sems, recv_sems):
        start, finish = _small_gather_plan(v_ref, out_ref, send_sems, recv_sems)
        start()
        finish()

    return pl.pallas_call(
        body, name=name,
        out_shape=jax.ShapeDtypeStruct((NDEV, n, 128), f32),
        in_specs=[pl.BlockSpec(memory_space=pltpu.VMEM)],
        out_specs=pl.BlockSpec(memory_space=pltpu.VMEM),
        scratch_shapes=_small_gather_scratch(),
    )(v)


def _ag2_plan(x_refs, out_refs, send_sems, recv_sems, local_sems):
    na = len(x_refs)
    x, y, c = _my_pos()
    me, sibling = (x, y, c), (x, y, 1 - c)
    chips = [(1 - x, y), (x, 1 - y), (1 - x, 1 - y)]

    def rows(i, px, py, pc):
        m_per = x_refs[i].shape[0]
        return out_refs[i].at[pl.ds(pl.multiple_of((4 * px + 2 * py + pc) * m_per, 16 if m_per % 16 == 0 else 8), m_per), :]

    def copies(k, block, to, from_shard=False):
        return [pltpu.make_async_remote_copy(
            src_ref=x_refs[i] if from_shard else rows(i, *block), dst_ref=rows(i, *block),
            send_sem=send_sems.at[k * na + i], recv_sem=recv_sems.at[k * na + i], device_id=to, device_id_type=MESH)
            for i in range(na)]

    def mine():
        return [pltpu.make_async_copy(x_refs[i], rows(i, *me), local_sems.at[i]) for i in range(na)]

    def first():
        cps = copies(0, me, sibling, True)
        for j, chip in enumerate(chips):
            cps += copies(1 + j, me, (*chip, c), True)
        return cps

    def start():
        for cp in mine() + first():
            cp.start()

    def forward():
        for j, chip in enumerate(chips):
            for cp in copies(1 + j, (*chip, c), me):
                cp.wait_recv()
            for cp in copies(4 + j, (*chip, c), sibling):
                cp.start()

    def finish():
        for cp in copies(0, sibling, me):
            cp.wait_recv()
        for j, chip in enumerate(chips):
            for cp in copies(4 + j, (*chip, 1 - c), me):
                cp.wait_recv()
        for cp in first():
            cp.wait_send()
        for j, chip in enumerate(chips):
            for cp in copies(4 + j, (*chip, c), sibling):
                cp.wait_send()
        for cp in mine():
            cp.wait()

    return start, forward, finish


def _ag2_scratch(na):
    return [pltpu.SemaphoreType.DMA((7 * na,)), pltpu.SemaphoreType.DMA((7 * na,)), pltpu.SemaphoreType.DMA((na,))]


def _a2a_plan(g_ref, recv_ref, send_sems, recv_sems, local_sem):
    x, y, c = _my_pos()
    me = 4 * x + 2 * y + c
    peers = []
    for k in range(1, NDEV):
        kx, ky, kc = (k >> 2) & 1, (k >> 1) & 1, k & 1
        peers.append((x ^ kx, y ^ ky, c ^ kc))

    def sends():
        return [pltpu.make_async_remote_copy(
            src_ref=g_ref.at[4 * px + 2 * py + pc], dst_ref=recv_ref.at[me], send_sem=send_sems.at[k], recv_sem=recv_sems.at[k],
            device_id=(px, py, pc), device_id_type=MESH) for k, (px, py, pc) in enumerate(peers)]

    def own():
        return pltpu.make_async_copy(g_ref.at[me], recv_ref.at[me], local_sem)

    def start():
        own().start()
        for cp in sends():
            cp.start()

    def finish():
        for k, (px, py, pc) in enumerate(peers):
            pltpu.make_async_remote_copy(
                src_ref=g_ref.at[me], dst_ref=recv_ref.at[4 * px + 2 * py + pc], send_sem=send_sems.at[k],
                recv_sem=recv_sems.at[k], device_id=(x, y, c), device_id_type=MESH).wait_recv()
        for cp in sends():
            cp.wait_send()
        own().wait()

    return start, finish


def _a2a_scratch():
    return [pltpu.SemaphoreType.DMA((NDEV - 1,)), pltpu.SemaphoreType.DMA((NDEV - 1,)), pltpu.SemaphoreType.DMA]


def _allgather2(shards, name):
    na = len(shards)

    def body(*refs):
        start, forward, finish = _ag2_plan(refs[:na], refs[na:2 * na], *refs[2 * na:])
        start()
        forward()
        finish()

    return pl.pallas_call(
        body, name=name,
        out_shape=[jax.ShapeDtypeStruct((NDEV * s.shape[0], s.shape[1]), s.dtype) for s in shards],
        in_specs=[pl.BlockSpec(memory_space=pltpu.VMEM)] * na,
        out_specs=[pl.BlockSpec(memory_space=pltpu.VMEM)] * na,
        scratch_shapes=_ag2_scratch(na),
        compiler_params=pltpu.CompilerParams(vmem_limit_bytes=VMEM_LIMIT),
    )(*shards)


def _reduce_scatter2(g, small, name):
    _, r, n = g.shape
    ch = 16
    nch = r // ch
    ns = len(small)

    def body(g_ref, *rest):
        v_refs, out_ref, vout_refs = rest[:ns], rest[ns], rest[ns + 1:2 * ns + 1]
        a_ref, h_ref, b_ref, s1_send, s1_recv, s2_send, s2_recv = rest[2 * ns + 1:2 * ns + 8]
        gather_sems = rest[2 * ns + 8:]
        gathers = [_small_gather_plan(v_refs[i], vout_refs[i], *gather_sems[2 * i:2 * i + 2]) for i in range(ns)]
        for start, _ in gathers:
            start()
        x, y, c = _my_pos()
        sibling = (x, y, 1 - c)
        s1 = []
        for j in range(4):
            cp = pltpu.make_async_remote_copy(
                src_ref=g_ref.at[2 * j + (1 - c)], dst_ref=a_ref.at[j], send_sem=s1_send.at[j], recv_sem=s1_recv.at[j],
                device_id=sibling, device_id_type=MESH)
            cp.start()
            s1.append(cp)
        for cp in s1:
            cp.wait_recv()

        def add1(i, _):
            rr = pl.ds(pl.multiple_of(i * ch, ch), ch)
            for j in range(4):
                h_ref[j, rr, :] = (g_ref[2 * j + c, rr, :].astype(f32) + a_ref[j, rr, :].astype(f32)).astype(bf16)
            return 0
        lax.fori_loop(0, nch, add1, 0)
        mychip = 2 * x + y
        s2 = []
        for m in range(1, 4):
            mx, my_ = (m >> 1) & 1, m & 1
            px, py = x ^ mx, y ^ my_
            cp = pltpu.make_async_remote_copy(
                src_ref=h_ref.at[2 * px + py], dst_ref=b_ref.at[m - 1], send_sem=s2_send.at[m - 1], recv_sem=s2_recv.at[m - 1],
                device_id=(px, py, c), device_id_type=MESH)
            cp.start()
            s2.append(cp)
        for cp in s2:
            cp.wait_recv()

        def add2(i, _):
            rr = pl.ds(pl.multiple_of(i * ch, ch), ch)
            acc = h_ref[mychip, rr, :].astype(f32)
            for m in range(3):
                acc = acc + b_ref[m, rr, :].astype(f32)
            out_ref[rr, :] = acc
            return 0
        lax.fori_loop(0, nch, add2, 0)
        for cp in s1 + s2:
            cp.wait_send()
        for _, finish in gathers:
            finish()

    vmem = pl.BlockSpec(memory_space=pltpu.VMEM)
    return pl.pallas_call(
        body, name=name,
        out_shape=[jax.ShapeDtypeStruct((r, n), f32)] + [jax.ShapeDtypeStruct((NDEV,) + v.shape, v.dtype) for v in small],
        in_specs=[vmem] * (1 + ns),
        out_specs=[vmem] * (1 + ns),
        scratch_shapes=[pltpu.VMEM((4, r, n), bf16), pltpu.VMEM((4, r, n), bf16), pltpu.VMEM((3, r, n), bf16),
                        pltpu.SemaphoreType.DMA((4,)), pltpu.SemaphoreType.DMA((4,)),
                        pltpu.SemaphoreType.DMA((3,)), pltpu.SemaphoreType.DMA((3,))] + _small_gather_scratch() * ns,
        compiler_params=pltpu.CompilerParams(vmem_limit_bytes=VMEM_LIMIT),
    )(g, *small)


def _mod_fwd(cvec, w_sh, b_sh):
    def body(c_ref, w_ref, b_ref, o_ref):
        cv = c_ref[...]
        act = (cv * _sigmoid(cv)).astype(bf16)
        o_ref[...] = _dot(act, w_ref[...].astype(bf16)) + b_ref[...]
    return pl.pallas_call(body, name="mod_fwd", out_shape=jax.ShapeDtypeStruct((16, w_sh.shape[1]), f32))(cvec, w_sh, b_sh)


def _mod_bwd(cvec, dm_sh, w_sh):
    def body(c_ref, dm_ref, w_ref, gw_ref, gc_ref):
        cv = c_ref[...]
        act = (cv * _sigmoid(cv)).astype(bf16)
        gw_ref[...] = _dg(act, dm_ref[...].astype(bf16), TN)
        gc_ref[...] = _dg(dm_ref[8:16, :].astype(bf16), w_ref[...].astype(bf16), NT)
    return pl.pallas_call(
        body, name="mod_bwd",
        out_shape=(jax.ShapeDtypeStruct(w_sh.shape, f32), jax.ShapeDtypeStruct((8, D), f32)))(cvec, dm_sh, w_sh)


def _sum_rows8(a, name):
    n = a.shape[1]

    def body(a_ref, o_ref):
        acc = a_ref[0]
        for d in range(1, NDEV):
            acc = acc + a_ref[d]
        o_ref[...] = acc
    return pl.pallas_call(body, name=name, out_shape=jax.ShapeDtypeStruct((n, 128), f32))(a)


def _in_proj(x0, ctx0, g1, modv, w_inT, shards):
    tm = 256
    nt = TA // tm
    nx = T // tm
    na = len(shards)

    def body(x_ref, c_ref, g_ref, mod_ref, w_ref, *rest):
        x_refs, (h_ref, q_ref, k_ref, v_ref, a_ref, gg_ref) = rest[:na], rest[na:na + 6]
        out_refs, sems = rest[na + 6:2 * na + 6], rest[2 * na + 6:]
        i = pl.program_id(0)
        if na:
            start, forward, finish = _ag2_plan(x_refs, out_refs, *sems)
            pl.when(i == 0)(start)
            pl.when(i == nt - 2)(forward)
        is_ctx = i == nt - 1
        xv = jnp.where(is_ctx, c_ref[...], x_ref[...])
        rstd = lax.rsqrt(jnp.mean(xv * xv, axis=-1, keepdims=True) + EPS)
        sh = jnp.where(is_ctx, mod_ref[6:7, :], mod_ref[0:1, :])
        sc = jnp.where(is_ctx, mod_ref[7:8, :], mod_ref[1:2, :])
        h = ((xv * rstd * g_ref[...]) * (1.0 + sc) + sh).astype(bf16)
        h_ref[...] = h
        for j, o_ref in enumerate((q_ref, k_ref, v_ref, a_ref, gg_ref)):
            o_ref[...] = _dg(h, w_ref[j * DA:(j + 1) * DA, :], NT).astype(o_ref.dtype)
        if na:
            pl.when(is_ctx)(finish)

    row = lambda w: pl.BlockSpec((tm, w), lambda i: (i, 0))
    hbm = pl.BlockSpec(memory_space=pl.ANY)
    return pl.pallas_call(
        body, name="in_proj", grid=(nt,),
        in_specs=[pl.BlockSpec((tm, D), lambda i: (jnp.minimum(i, nx - 1), 0)), _full((TC, D)),
                  _full((1, D)), _full((8, D)), _full((5 * DA, D))] + [hbm] * na,
        out_specs=[row(D), row(DA), row(DA), row(DA), row(DA), row(DA)] + [hbm] * na,
        out_shape=[jax.ShapeDtypeStruct((TA, D), bf16)] + [jax.ShapeDtypeStruct((TA, DA), bf16)] * 3
                  + [jax.ShapeDtypeStruct((TA, DA), f32)] * 2
                  + [jax.ShapeDtypeStruct((NDEV * sh.shape[0], sh.shape[1]), sh.dtype) for sh in shards],
        scratch_shapes=_ag2_scratch(na) if na else [],
        compiler_params=_cp("arbitrary"),
    )(x0, ctx0, g1, modv, w_inT, *shards)


def _win_start(r):
    return jnp.clip(r - WR // 2, 0, GW - WR)


def _pattern(r):
    return _win_start(r) - r + (WR - 1)


def _bias_table(rpb):
    qc = np.arange(GW)[:, None]
    kc = np.arange(GW)[None, :]
    cs = np.clip(qc - NCOL // 2, 0, GW - NCOL)
    valid = np.tile(((kc >= cs) & (kc < cs + NCOL)).astype(np.int32), (1, WR))
    pad = jnp.pad(rpb, ((0, 0), (0, 0), (0, GW - (2 * NCOL - 1))))
    base = jnp.stack([pad[:, p:p + WR, :].reshape(NH, WR * GW) for p in range(8)])

    def body(base_ref, valid_ref, o_ref):
        ok = valid_ref[...] != 0
        for h in range(NH):
            row = jnp.broadcast_to(base_ref[0, h:h + 1, :], (GW, WR * GW))
            skew = pltpu.roll(row, WR * GW - (NCOL - 1), 1, stride=1, stride_axis=0)
            o_ref[0, h] = jnp.where(ok, skew, NEG)

    return pl.pallas_call(
        body, name="bias_table", grid=(8,),
        in_specs=[pl.BlockSpec((1, NH, WR * GW), lambda p: (p, 0, 0)), _full((GW, WR * GW))],
        out_specs=pl.BlockSpec((1, NH, GW, WR * GW), lambda p: (p, 0, 0, 0)),
        out_shape=jax.ShapeDtypeStruct((8, NH, GW, WR * GW), f32),
        compiler_params=_cp("parallel"),
    )(base, jnp.asarray(valid))


def _rpb_tables():
    lane_map = np.zeros((WR * GW, WR, 2 * NCOL - 1), np.float32)
    for i in range(WR):
        for t in range(GW):
            if t >= GW - NCOL:
                lane_map[i * GW + t, i, (GW - 1 - t) + NCOL - 1] = 1.0
            elif t < NCOL - 1:
                lane_map[i * GW + t, (i - 1) % WR, NCOL - 2 - t] = 1.0
    p = np.arange(8)[:, None]
    i = np.arange(WR)[None, :]
    r_hot = ((p + i)[:, :, None] == np.arange(2 * WR - 1)[None, None, :]).astype(np.float32)
    return lane_map, r_hot


AG_FORWARD_ROW = 52


def _attn_fwd(q, k, v, bias_tab, shards):
    na = len(shards)

    def body(q_ref, k_ref, v_ref, b_ref, *rest):
        x_refs, (y_ref, lse_ref), out_refs, sems = rest[:na], rest[na:na + 2], rest[na + 2:2 * na + 2], rest[2 * na + 2:]
        r = pl.program_id(0)
        if na:
            start, forward, finish = _ag2_plan(x_refs, out_refs, *sems)
            pl.when(r == 0)(start)
            pl.when(r == AG_FORWARD_ROW)(forward)
        ks = pl.multiple_of(_win_start(r) * GW, GW)
        qq = q_ref[...]
        lo = lax.broadcasted_iota(jnp.int32, (GW, 2 * HD), 1) < HD
        kv, scores = [], []
        for pr in range(NH // 2):
            ps = slice(pr * 2 * HD, (pr + 1) * 2 * HD)
            q2 = qq[:, ps]
            kw, kc = k_ref[pl.ds(ks, WR * GW), ps], k_ref[T:TA, ps]
            kv.append((v_ref[pl.ds(ks, WR * GW), ps], v_ref[T:TA, ps]))
            for s in range(2):
                qm = jnp.where(lo if s == 0 else ~lo, q2, jnp.zeros_like(q2))
                scores.append((_dg(qm, kw, NT) * SCALE + b_ref[0, 2 * pr + s], _dg(qm, kc, NT) * SCALE))
        probs = []
        for h, (sl, sc) in enumerate(scores):
            m = jnp.maximum(jnp.max(sl, axis=-1, keepdims=True), jnp.max(sc, axis=-1, keepdims=True))
            pl_ = jnp.exp(sl - m)
            pc = jnp.exp(sc - m)
            l = jnp.sum(pl_, axis=-1, keepdims=True) + jnp.sum(pc, axis=-1, keepdims=True)
            lse_ref[:, h:h + 1] = m + jnp.log(l)
            probs.append((pl_.astype(bf16), pc.astype(bf16), 1.0 / l))
        for pr in range(NH // 2):
            ps = slice(pr * 2 * HD, (pr + 1) * 2 * HD)
            vw, vc = kv[pr]
            outs = [(_dot(pb, vw) + _dot(cb, vc)) * rl for pb, cb, rl in probs[2 * pr:2 * pr + 2]]
            y_ref[:, ps] = jnp.where(lo, outs[0], outs[1]).astype(bf16)
        if na:
            pl.when(r == GW - 1)(finish)

    hbm = pl.BlockSpec(memory_space=pl.ANY)
    return pl.pallas_call(
        body, name="attn_fwd", grid=(GW,),
        in_specs=[pl.BlockSpec((GW, DA), lambda r: (r, 0)), _full((TA, DA)), _full((TA, DA)),
                  pl.BlockSpec((1, NH, GW, WR * GW), lambda r: (_pattern(r), 0, 0, 0))] + [hbm] * na,
        out_specs=[pl.BlockSpec((GW, DA), lambda r: (r, 0)), pl.BlockSpec((GW, NH), lambda r: (r, 0))] + [hbm] * na,
        out_shape=[jax.ShapeDtypeStruct((T, DA), bf16), jax.ShapeDtypeStruct((T, NH), f32)]
                  + [jax.ShapeDtypeStruct((NDEV * s.shape[0], s.shape[1]), s.dtype) for s in shards],
        scratch_shapes=_ag2_scratch(na) if na else [],
        compiler_params=_cp("arbitrary"),
    )(q, k, v, bias_tab, *shards)


CONV_TT = 256
HALO = 16


def _halo_specs(tt, w, nrows_blocks):
    per = tt // HALO
    prev = pl.BlockSpec((HALO, w), lambda i: (jnp.maximum(i * per - 1, 0), 0))
    cur = pl.BlockSpec((tt, w), lambda i: (i, 0))
    nxt = pl.BlockSpec((HALO, w), lambda i: (jnp.minimum((i + 1) * per, nrows_blocks - 1), 0))
    return [prev, cur, nxt]


def _shifted_copies(rot, wn):
    for b in range(1, 8):
        rot[b, 0:wn - 8, :] = rot[0, pl.ds(b, wn - 8), :]


def _conf_fwd(a, g, conv_w, conv_b, ln_g, ln_b, shards):
    tt = CONV_TT
    nt = T // tt
    sub = 32
    wn = tt + 2 * HALO
    na = len(shards)

    def body(ap, ac, an, gp, gc, gn, w_ref, b_ref, lg_ref, lb_ref, *rest):
        x_refs, (y_ref, cv_ref), out_refs, (rot, *sems) = rest[:na], rest[na:na + 2], rest[na + 2:2 * na + 2], rest[2 * na + 2:]
        i = pl.program_id(0)
        if na:
            start, forward, finish = _ag2_plan(x_refs, out_refs, *sems)
            pl.when(i == 0)(start)
            pl.when(i == nt // 2)(forward)
        rot[0, 0:HALO, :] = jnp.where(i > 0, ap[...] * _sigmoid(gp[...]), 0.0)
        rot[0, HALO:HALO + tt, :] = ac[...] * _sigmoid(gc[...])
        rot[0, HALO + tt:, :] = jnp.where(i < nt - 1, an[...] * _sigmoid(gn[...]), 0.0)
        _shifted_copies(rot, wn)
        w = w_ref[...]
        for s in range(tt // sub):
            acc = jnp.zeros((sub, DA), f32)
            for j in range(CW):
                a8, b8 = divmod(1 + j, 8)
                acc = acc + rot[b8, pl.ds(s * sub + 8 * a8, sub), :] * w[j:j + 1, :]
            cv = acc + b_ref[...]
            cv_ref[pl.ds(s * sub, sub), :] = cv
            mu = jnp.mean(cv, axis=-1, keepdims=True)
            xc = cv - mu
            rstd = lax.rsqrt(jnp.mean(xc * xc, axis=-1, keepdims=True) + EPS)
            z = xc * rstd * lg_ref[...] + lb_ref[...]
            y_ref[pl.ds(s * sub, sub), :] = (z * _sigmoid(z)).astype(bf16)
        if na:
            pl.when(i == nt - 1)(finish)

    hs = _halo_specs(tt, DA, T // HALO)
    hbm = pl.BlockSpec(memory_space=pl.ANY)
    return pl.pallas_call(
        body, name="conf_fwd", grid=(nt,),
        in_specs=hs + hs + [_full((CW, DA)), _full((1, DA)), _full((1, DA)), _full((1, DA))] + [hbm] * na,
        out_specs=[pl.BlockSpec((tt, DA), lambda i: (i, 0)), pl.BlockSpec((tt, DA), lambda i: (i, 0))] + [hbm] * na,
        out_shape=[jax.ShapeDtypeStruct((T, DA), bf16), jax.ShapeDtypeStruct((T, DA), f32)]
                  + [jax.ShapeDtypeStruct((NDEV * s.shape[0], s.shape[1]), s.dtype) for s in shards],
        scratch_shapes=[pltpu.VMEM((8, wn, DA), f32)] + (_ag2_scratch(na) if na else []),
        compiler_params=_cp("arbitrary"),
    )(a, a, a, g, g, g, conv_w, conv_b, ln_g, ln_b, *shards)


def _out_proj(xa, y_na, y_cv, w_out, modv, g2):
    tm = 512

    def body(x_ref, ya_ref, yc_ref, w_ref, mod_ref, g_ref, x1_ref, pj_ref, h2_ref):
        proj = _dot(ya_ref[...], w_ref[0:DA, :]) + _dot(yc_ref[...], w_ref[DA:D, :])
        x1 = x_ref[...] + mod_ref[2:3, :] * proj
        x1_ref[...] = x1
        pj_ref[...] = proj.astype(bf16)
        rstd = lax.rsqrt(jnp.mean(x1 * x1, axis=-1, keepdims=True) + EPS)
        h2_ref[...] = ((x1 * rstd * g_ref[...]) * (1.0 + mod_ref[4:5, :]) + mod_ref[3:4, :]).astype(bf16)

    row = lambda w: pl.BlockSpec((tm, w), lambda i: (i, 0))
    return pl.pallas_call(
        body, name="out_proj", grid=(T // tm,),
        in_specs=[row(D), row(DA), row(DA), _full((D, D)), _full((8, D)), _full((1, D))],
        out_specs=[row(D), row(D), row(D)],
        out_shape=[jax.ShapeDtypeStruct((T, D), f32), jax.ShapeDtypeStruct((T, D), bf16), jax.ShapeDtypeStruct((T, D), bf16)],
        compiler_params=_cp("parallel"),
    )(xa, y_na, y_cv, w_out, modv, g2)


FFN_TT = 1024
FFN_TT_FWD = 2048
FFN_CT = 256
FFN_NC = F // FFN_CT
FFN_SUB = 32


def _row_neighbours(ref, r, n):
    blk = ref[pl.ds(r - 8, n + 16), :]
    return blk[8:8 + n, :], pltpu.roll(blk, 1, 0)[8:8 + n, :], pltpu.roll(blk, n + 15, 0)[8:8 + n, :]


def _ffn_specs(tt, ct, by_token_first):
    tc = (lambda f: (lambda t, c: f(t, c))) if by_token_first else (lambda f: (lambda c, t: f(t, c)))
    per = tt // HALO
    halo = [pl.BlockSpec((HALO, D), tc(lambda t, c: (jnp.maximum(t * per - 1, 0), 0))),
            pl.BlockSpec((tt, D), tc(lambda t, c: (t, 0))),
            pl.BlockSpec((HALO, D), tc(lambda t, c: (jnp.minimum((t + 1) * per, T // HALO - 1), 0)))]
    weights = [pl.BlockSpec((ct, D), tc(lambda t, c: (c, 0))), pl.BlockSpec((ct, D), tc(lambda t, c: (c + FFN_NC, 0))),
               pl.BlockSpec((3, ct), tc(lambda t, c: (0, c))), pl.BlockSpec((3, ct), tc(lambda t, c: (0, c + FFN_NC))),
               pl.BlockSpec((1, ct), tc(lambda t, c: (0, c))), pl.BlockSpec((1, ct), tc(lambda t, c: (0, c + FFN_NC))),
               pl.BlockSpec((ct, D), tc(lambda t, c: (c, 0)))]
    return halo, weights


def _ffn_fwd(h2, w_upT, fcw, fcb, w_down):
    tt, ct = FFN_TT_FWD, FFN_CT
    nt = T // tt
    wn = tt + 2 * HALO
    half = tt // 2

    def body(hp, hc, hn, wg_ref, wv_ref, cwg_ref, cwv_ref, cbg_ref, cbv_ref, wd_ref, o_ref, u_ref, u2_ref, hwin, uwin, act):
        t = pl.program_id(0)
        c = pl.program_id(1)

        @pl.when(c == 0)
        def _():
            hwin[0:HALO, :] = jnp.where(t > 0, hp[...], jnp.zeros_like(hp[...]))
            hwin[HALO:HALO + tt, :] = hc[...]
            hwin[HALO + tt:, :] = jnp.where(t < nt - 1, hn[...], jnp.zeros_like(hn[...]))
            o_ref[...] = jnp.zeros_like(o_ref)

        for r0, r1 in ((0, half + 2 * HALO), (half + 2 * HALO, wn)):
            hw = hwin[r0:r1, :]
            uwin[r0:r1, :ct] = _dg(hw, wg_ref[...], NT)
            uwin[r0:r1, ct:] = _dg(hw, wv_ref[...], NT)
        cw = jnp.concatenate([cwg_ref[...], cwv_ref[...]], axis=1)
        cb = jnp.concatenate([cbg_ref[...], cbv_ref[...]], axis=1)
        for p in range(2):
            for r in range(p * half, (p + 1) * half, FFN_SUB):
                uc, prev, nxt = _row_neighbours(uwin, HALO + r, FFN_SUB)
                u2 = prev * cw[0:1, :] + uc * cw[1:2, :] + nxt * cw[2:3, :] + cb
                u_ref[r:r + FFN_SUB, :] = uc.astype(bf16)
                u2_ref[r:r + FFN_SUB, :] = u2
                gate = u2[:, :ct]
                act[r:r + FFN_SUB, :] = (gate * _sigmoid(gate) * u2[:, ct:]).astype(bf16)
            rows = slice(p * half, (p + 1) * half)
            o_ref[rows, :] += _dot(act[rows, :], wd_ref[...])

    halo, weights = _ffn_specs(tt, ct, True)
    pair = pl.BlockSpec((tt, 2 * ct), lambda t, c: (t, c))
    return pl.pallas_call(
        body, name="ffn_fwd", grid=(nt, FFN_NC),
        in_specs=halo + weights,
        out_specs=[pl.BlockSpec((tt, D), lambda t, c: (t, 0)), pair, pair],
        out_shape=[jax.ShapeDtypeStruct((T, D), f32), jax.ShapeDtypeStruct((T, F2), bf16), jax.ShapeDtypeStruct((T, F2), f32)],
        scratch_shapes=[pltpu.VMEM((wn, D), bf16), pltpu.VMEM((wn, 2 * ct), f32), pltpu.VMEM((tt, ct), bf16)],
        compiler_params=_cp("parallel", "arbitrary"),
    )(h2, h2, h2, w_upT, w_upT, fcw, fcw, fcb, fcb, w_down)


def _loss_bwd(ffn, x1, tgt, modv, gf):
    tm = 512
    nt = T // tm

    def body(f_ref, x1_ref, t_ref, mod_ref, g_ref, dx2_ref, df_ref, s_ref):
        i = pl.program_id(0)

        @pl.when(i == 0)
        def _():
            s_ref[...] = jnp.zeros_like(s_ref)

        ff = f_ref[...]
        gt2 = mod_ref[5:6, :]
        x2 = x1_ref[...] + gt2 * ff
        rstd = lax.rsqrt(jnp.mean(x2 * x2, axis=-1, keepdims=True) + EPS)
        xh = x2 * rstd
        gfv = g_ref[...]
        e = xh * gfv - t_ref[...]
        dy = e * (1.0 / D)
        dxh = dy * gfv
        dx2 = rstd * (dxh - xh * jnp.mean(dxh * xh, axis=-1, keepdims=True))
        dx2_ref[...] = dx2
        df_ref[...] = (dx2 * gt2).astype(bf16)
        s_ref[0:1, :] += jnp.sum(dy * xh, axis=0, keepdims=True)
        s_ref[1:2, :] += jnp.sum(dx2 * ff, axis=0, keepdims=True)
        s_ref[2:3, :] += jnp.sum(e * e, axis=0, keepdims=True)

        @pl.when(i == nt - 1)
        def _():
            tot = jnp.sum(s_ref[2:3, :], axis=-1, keepdims=True) * (0.5 / D)
            s_ref[3:4, :] = jnp.broadcast_to(tot, (1, D))

    row = lambda: pl.BlockSpec((tm, D), lambda i: (i, 0))
    return pl.pallas_call(
        body, name="loss_bwd", grid=(nt,),
        in_specs=[row(), row(), row(), _full((8, D)), _full((1, D))],
        out_specs=[row(), row(), _full((8, D))],
        out_shape=[jax.ShapeDtypeStruct((T, D), f32), jax.ShapeDtypeStruct((T, D), bf16), jax.ShapeDtypeStruct((8, D), f32)],
        compiler_params=_cp("arbitrary"),
    )(ffn, x1, tgt, modv, gf)


def _ffn_bwd(h2, dffn, u_t, u2_t, fcw, w_down):
    tt, ct = FFN_TT, FFN_CT
    nt = T // tt
    wn = tt + 2 * HALO
    half = tt // 2

    def body(dp, dc, dn, hc, uc_ref, u2p, u2c, u2n, cwg_ref, cwv_ref, wd_ref,
             dug_ref, duv_ref, dwu_ref, dwd_ref, dcwg_ref, dcwv_ref, dcbg_ref, dcbv_ref,
             dwin, d2win, dawin, accu, accd, act, du):
        t = pl.program_id(1)
        first, last = t == 0, t == nt - 1
        zero = jnp.zeros((HALO, D), bf16)
        dwin[0:HALO, :] = jnp.where(first, zero, dp[...])
        dwin[HALO:HALO + tt, :] = dc[...]
        dwin[HALO + tt:, :] = jnp.where(last, zero, dn[...])

        @pl.when(first)
        def _():
            for r in (accu, accd, dcwg_ref, dcwv_ref, dcbg_ref, dcbv_ref):
                r[...] = jnp.zeros_like(r)

        cw = jnp.concatenate([cwg_ref[...], cwv_ref[...]], axis=1)
        split = half + 2 * HALO
        for r0, r1 in ((0, split), (split, wn)):
            dawin[r0:r1, :] = _dg(dwin[r0:r1, :], wd_ref[...], NT)

        def grads(u2v, dact):
            gate, val = u2v[:, :ct], u2v[:, ct:]
            sg = _sigmoid(gate)
            silu = gate * sg
            return dact * val * (sg * (1.0 + gate * (1.0 - sg))), dact * silu, silu * val

        for blk, r0 in ((u2p, 0), (u2n, HALO + tt)):
            dgate, dval, _ = grads(blk[...], dawin[r0:r0 + HALO, :])
            d2win[r0:r0 + HALO, :ct] = dgate
            d2win[r0:r0 + HALO, ct:] = dval
        for p in range(2):
            rows = slice(p * half, (p + 1) * half)
            for r in range(p * half, (p + 1) * half, FFN_SUB):
                dgate, dval, av = grads(u2c[r:r + FFN_SUB, :], dawin[HALO + r:HALO + r + FFN_SUB, :])
                d2win[HALO + r:HALO + r + FFN_SUB, :ct] = dgate
                d2win[HALO + r:HALO + r + FFN_SUB, ct:] = dval
                act[r:r + FFN_SUB, :] = av.astype(bf16)
            accd[...] += _dg(act[rows, :], dc[rows, :], TN)

        def fold8(x):
            out = x[0:8]
            for k in range(8, FFN_SUB, 8):
                out = out + x[k:k + 8]
            return out

        for p in range(2):
            rows = slice(p * half, (p + 1) * half)
            sums = [jnp.zeros((8, 2 * ct), f32) for _ in range(4)]
            for r in range(p * half, (p + 1) * half, FFN_SUB):
                d2c, d2m, d2p = _row_neighbours(d2win, HALO + r, FFN_SUB)
                ucur = uc_ref[r:r + FFN_SUB, :].astype(f32)
                sums[0] = sums[0] + fold8(d2c)
                for kk, dd in enumerate((d2p, d2c, d2m)):
                    sums[1 + kk] = sums[1 + kk] + fold8(ucur * dd)
                du[r:r + FFN_SUB, :] = (d2p * cw[0:1, :] + d2c * cw[1:2, :] + d2m * cw[2:3, :]).astype(bf16)
            dcb = jnp.sum(sums[0], axis=0, keepdims=True)
            dcbg_ref[...] += dcb[:, :ct]
            dcbv_ref[...] += dcb[:, ct:]
            for kk in range(3):
                dck = jnp.sum(sums[1 + kk], axis=0, keepdims=True)
                dcwg_ref[kk:kk + 1, :] += dck[:, :ct]
                dcwv_ref[kk:kk + 1, :] += dck[:, ct:]
            dug_ref[rows, :] = du[rows, :ct]
            duv_ref[rows, :] = du[rows, ct:]
            accu[...] += _dg(du[rows, :], hc[rows, :], TN)

        @pl.when(last)
        def _():
            dwu_ref[0] = accu[0:ct, :].astype(bf16)
            dwu_ref[1] = accu[ct:, :].astype(bf16)
            dwd_ref[...] = accd[...].astype(bf16)

    per = tt // HALO
    prev = lambda w: (lambda c, t: (jnp.maximum(t * per - 1, 0), c if w else 0))
    nxt = lambda w: (lambda c, t: (jnp.minimum((t + 1) * per, T // HALO - 1), c if w else 0))
    tile = lambda: pl.BlockSpec((ct, D), lambda c, t: (c, 0))
    lane = lambda r, off: pl.BlockSpec((r, ct), lambda c, t: (0, c + off))
    return pl.pallas_call(
        body, name="ffn_bwd", grid=(FFN_NC, nt),
        in_specs=[pl.BlockSpec((HALO, D), prev(False)), pl.BlockSpec((tt, D), lambda c, t: (t, 0)), pl.BlockSpec((HALO, D), nxt(False)),
                  pl.BlockSpec((tt, D), lambda c, t: (t, 0)), pl.BlockSpec((tt, 2 * ct), lambda c, t: (t, c)),
                  pl.BlockSpec((HALO, 2 * ct), prev(True)), pl.BlockSpec((tt, 2 * ct), lambda c, t: (t, c)),
                  pl.BlockSpec((HALO, 2 * ct), nxt(True)), lane(3, 0), lane(3, FFN_NC), tile()],
        out_specs=[pl.BlockSpec((tt, ct), lambda c, t: (t, c)), pl.BlockSpec((tt, ct), lambda c, t: (t, c)),
                   pl.BlockSpec((2, ct, D), lambda c, t: (0, c, 0)), tile(), lane(3, 0), lane(3, 0), lane(1, 0), lane(1, 0)],
        out_shape=[jax.ShapeDtypeStruct((T, F), bf16), jax.ShapeDtypeStruct((T, F), bf16),
                   jax.ShapeDtypeStruct((2, F, D), bf16), jax.ShapeDtypeStruct((F, D), bf16),
                   jax.ShapeDtypeStruct((3, F), f32), jax.ShapeDtypeStruct((3, F), f32),
                   jax.ShapeDtypeStruct((1, F), f32), jax.ShapeDtypeStruct((1, F), f32)],
        scratch_shapes=[pltpu.VMEM((wn, D), bf16), pltpu.VMEM((wn, 2 * ct), f32), pltpu.VMEM((wn, ct), f32),
                        pltpu.VMEM((2 * ct, D), f32), pltpu.VMEM((ct, D), f32),
                        pltpu.VMEM((tt, ct), bf16), pltpu.VMEM((tt, 2 * ct), bf16)],
        compiler_params=_cp("parallel", "arbitrary"),
    )(dffn, dffn, dffn, h2, u_t, u2_t, u2_t, u2_t, fcw, fcw, w_down)


def _norm_bwd(dh, xv, gain, sh_sc, rstd):
    xh = xv * rstd
    n = xh * gain
    dn = dh * (1.0 + sh_sc)
    dxh = dn * gain
    dx = rstd * (dxh - xh * jnp.mean(dxh * xh, axis=-1, keepdims=True))
    return (dx, jnp.sum(dh, axis=0, keepdims=True), jnp.sum(dh * n, axis=0, keepdims=True),
            jnp.sum(dn * xh, axis=0, keepdims=True))


def _norm2_bwd(dug, duv, w_upT, x1, dx2, proj, w_out, y_na, y_cv, modv, g2):
    tm = 256
    nt = T // tm

    def body(dug_ref, duv_ref, w_ref, x1_ref, dx2_ref, pj_ref, wo_ref, ya_ref, yc_ref, mod_ref, g_ref,
             dx1_ref, dya_ref, dyc_ref, dwo_ref, s_ref, acc):
        i = pl.program_id(0)

        @pl.when(i == 0)
        def _():
            s_ref[...] = jnp.zeros_like(s_ref)
            acc[...] = jnp.zeros_like(acc)

        dh2 = _dot(dug_ref[...], w_ref[0:F, :]) + _dot(duv_ref[...], w_ref[F:F2, :])
        x1 = x1_ref[...]
        rstd = lax.rsqrt(jnp.mean(x1 * x1, axis=-1, keepdims=True) + EPS)
        dxn, dsh, dsc, dgn = _norm_bwd(dh2, x1, g_ref[...], mod_ref[4:5, :], rstd)
        dx1 = dx2_ref[...] + dxn
        dx1_ref[...] = dx1
        dpj = (dx1 * mod_ref[2:3, :]).astype(bf16)
        dyc = _dg(dpj, wo_ref[...], NT)
        dya_ref[...] = dyc[:, :DA].astype(bf16)
        dyc_ref[...] = dyc[:, DA:]
        acc[0:DA, :] += _dg(ya_ref[...], dpj, TN)
        acc[DA:D, :] += _dg(yc_ref[...], dpj, TN)

        @pl.when(i == nt - 1)
        def _():
            dwo_ref[...] = acc[...].astype(bf16)

        s_ref[0:1, :] += dsh
        s_ref[1:2, :] += dsc
        s_ref[2:3, :] += dgn
        s_ref[3:4, :] += jnp.sum(dx1 * pj_ref[...].astype(f32), axis=0, keepdims=True)

    row = lambda w: pl.BlockSpec((tm, w), lambda i: (i, 0))
    return pl.pallas_call(
        body, name="norm2_bwd", grid=(nt,),
        in_specs=[row(F), row(F), _full((F2, D)), row(D), row(D), row(D), _full((D, D)), row(DA), row(DA),
                  _full((8, D)), _full((1, D))],
        out_specs=[row(D), row(DA), row(DA), _full((D, D)), _full((8, D))],
        out_shape=[jax.ShapeDtypeStruct((T, D), f32), jax.ShapeDtypeStruct((T, DA), bf16), jax.ShapeDtypeStruct((T, DA), f32),
                   jax.ShapeDtypeStruct((D, D), bf16), jax.ShapeDtypeStruct((8, D), f32)],
        scratch_shapes=[pltpu.VMEM((D, D), f32)],
        compiler_params=_cp("arbitrary"),
    )(dug, duv, w_upT, x1, dx2, proj, w_out, y_na, y_cv, modv, g2)


def _conf_bwd(a, g, cv, dy, conv_w, ln_g, ln_b, blocks):
    tt = CONV_TT
    nt = T // tt
    sub = 32
    wn = tt + 2 * HALO
    nb = len(blocks)

    def body(ap, ac, an, gp, gc, gn, cp_, cc, cn, dp, dc, dn, w_ref, lg_ref, lb_ref, *rest):
        g_refs, (da_ref, dg_ref, dcw_ref, s_ref) = rest[:nb], rest[nb:nb + 4]
        recv_refs, (urot, drot, wacc), a2a_sems = rest[nb + 4:2 * nb + 4], rest[2 * nb + 4:2 * nb + 7], rest[2 * nb + 7:]
        i = pl.program_id(0)
        first, last = i == 0, i == nt - 1
        plans = [_a2a_plan(g_refs[k], recv_refs[k], *a2a_sems[3 * k:3 * k + 3]) for k in range(nb)]
        for start, _ in plans:
            pl.when(first)(start)

        @pl.when(first)
        def _():
            s_ref[...] = jnp.zeros_like(s_ref)
            wacc[...] = jnp.zeros_like(wacc)

        lg, lb = lg_ref[...], lb_ref[...]

        def ln_bwd(cvv, dyv):
            mu = jnp.mean(cvv, axis=-1, keepdims=True)
            xc = cvv - mu
            rstd = lax.rsqrt(jnp.mean(xc * xc, axis=-1, keepdims=True) + EPS)
            yn = xc * rstd
            z = yn * lg + lb
            sz = _sigmoid(z)
            dz = dyv * (sz * (1.0 + z * (1.0 - sz)))
            dyn = dz * lg
            dcv = rstd * (dyn - jnp.mean(dyn, axis=-1, keepdims=True) - yn * jnp.mean(dyn * yn, axis=-1, keepdims=True))
            return dcv, dz, yn

        urot[0, 0:HALO, :] = jnp.where(first, 0.0, ap[...] * _sigmoid(gp[...]))
        urot[0, HALO + tt:, :] = jnp.where(last, 0.0, an[...] * _sigmoid(gn[...]))
        drot[0, 0:HALO, :] = jnp.where(first, 0.0, ln_bwd(cp_[...], dp[...])[0])
        drot[0, HALO + tt:, :] = jnp.where(last, 0.0, ln_bwd(cn[...], dn[...])[0])
        for s in range(tt // sub):
            rr = pl.ds(s * sub, sub)
            urot[0, pl.ds(HALO + s * sub, sub), :] = ac[rr, :] * _sigmoid(gc[rr, :])
            dcv, dz, yn = ln_bwd(cc[rr, :], dc[rr, :])
            drot[0, pl.ds(HALO + s * sub, sub), :] = dcv
            s_ref[0:1, :] += jnp.sum(dcv, axis=0, keepdims=True)
            s_ref[1:2, :] += jnp.sum(dz * yn, axis=0, keepdims=True)
            s_ref[2:3, :] += jnp.sum(dz, axis=0, keepdims=True)
        _shifted_copies(urot, wn)
        _shifted_copies(drot, wn)
        w = w_ref[...]
        for s in range(tt // sub):
            rr = pl.ds(s * sub, sub)
            dcv = drot[0, pl.ds(HALO + s * sub, sub), :]
            acc = jnp.zeros((sub, DA), f32)
            for j in range(CW):
                ad, bd = divmod(2 * HALO - 1 - j, 8)
                au, bu = divmod(1 + j, 8)
                acc = acc + drot[bd, pl.ds(s * sub + 8 * ad, sub), :] * w[j:j + 1, :]
                part = urot[bu, pl.ds(s * sub + 8 * au, sub), :] * dcv
                wacc[j] += part[0:8] + part[8:16] + part[16:24] + part[24:32]
            av, gv = ac[rr, :], gc[rr, :]
            sg = _sigmoid(gv)
            da_ref[rr, :] = (acc * sg).astype(bf16)
            dg_ref[rr, :] = (acc * av * sg * (1.0 - sg)).astype(bf16)

        @pl.when(last)
        def _():
            for j in range(CW):
                dcw_ref[j:j + 1, :] = jnp.sum(wacc[j], axis=0, keepdims=True)
            dcw_ref[CW:CW + 1, :] = jnp.zeros((1, DA), f32)

        for _, finish in plans:
            pl.when(last)(finish)

    hs = _halo_specs(tt, DA, T // HALO)
    hbm = pl.BlockSpec(memory_space=pl.ANY)
    return pl.pallas_call(
        body, name="conf_bwd", grid=(nt,),
        in_specs=hs * 4 + [_full((CW, DA)), _full((1, DA)), _full((1, DA))] + [hbm] * nb,
        out_specs=[pl.BlockSpec((tt, DA), lambda i: (i, 0)), pl.BlockSpec((tt, DA), lambda i: (i, 0)),
                   _full((CW + 1, DA)), _full((8, DA))] + [hbm] * nb,
        out_shape=[jax.ShapeDtypeStruct((T, DA), bf16), jax.ShapeDtypeStruct((T, DA), bf16),
                   jax.ShapeDtypeStruct((CW + 1, DA), f32), jax.ShapeDtypeStruct((8, DA), f32)]
                  + [jax.ShapeDtypeStruct(b.shape, b.dtype) for b in blocks],
        scratch_shapes=[pltpu.VMEM((8, wn, DA), f32), pltpu.VMEM((8, wn, DA), f32), pltpu.VMEM((CW, 8, DA), f32)]
                       + _a2a_scratch() * nb,
        compiler_params=_cp("arbitrary"),
    )(a, a, a, g, g, g, cv, cv, cv, dy, dy, dy, conv_w, ln_g, ln_b, *blocks)


def _attn_bwd(q, k, v, y, dy, lse, bias_tab, blocks):
    zr = 256
    nb = len(blocks)

    def body(q_ref, k_ref, v_ref, y_ref, dy_ref, lse_ref, b_ref, *rest):
        g_refs, (dq_ref, dk_hbm, dv_hbm, db_ref) = rest[:nb], rest[nb:nb + 4]
        recv_refs, (dk_s, dv_s, sem), a2a_sems = rest[nb + 4:2 * nb + 4], rest[2 * nb + 4:2 * nb + 7], rest[2 * nb + 7:]
        r = pl.program_id(0)
        plans = [_a2a_plan(g_refs[i], recv_refs[i], *a2a_sems[3 * i:3 * i + 3]) for i in range(nb)]
        for start, _ in plans:
            pl.when(r == 0)(start)

        @pl.when(r == 0)
        def _():
            def z(i, _):
                rr = pl.ds(pl.multiple_of(i * zr, zr), zr)
                dk_s[rr, :] = jnp.zeros((zr, DA), f32)
                dv_s[rr, :] = jnp.zeros((zr, DA), f32)
                return 0
            lax.fori_loop(0, TA // zr, z, 0)

        @pl.when((r <= WR // 2) | (r > GW - WR // 2))
        def _():
            db_ref[...] = jnp.zeros_like(db_ref)

        ks = pl.multiple_of(_win_start(r) * GW, GW)
        win = pl.ds(ks, WR * GW)
        qq, yy, dyy, lse_v = q_ref[...], y_ref[...], dy_ref[...], lse_ref[...]
        lo = lax.broadcasted_iota(jnp.int32, (GW, 2 * HD), 1) < HD
        ops, heads = [], []
        for pr in range(NH // 2):
            ps = slice(pr * 2 * HD, (pr + 1) * 2 * HD)
            q2, do2 = qq[:, ps], dyy[:, ps]
            prod = do2.astype(f32) * yy[:, ps].astype(f32)
            kw, vw = k_ref[win, ps], v_ref[win, ps]
            kc, vc = k_ref[T:TA, ps], v_ref[T:TA, ps]
            ops.append((kw, kc))
            for s in range(2):
                h = 2 * pr + s
                msk = lo if s == 0 else ~lo
                qm = jnp.where(msk, q2, jnp.zeros_like(q2))
                dom = jnp.where(msk, do2, jnp.zeros_like(do2))
                delta = jnp.sum(jnp.where(msk, prod, 0.0), axis=-1, keepdims=True)
                heads.append((qm, dom, delta, _dg(qm, kw, NT), _dg(qm, kc, NT), _dg(dom, vw, NT), _dg(dom, vc, NT)))
        grads = []
        for h, (qm, dom, delta, sl, sc, dpl, dpc) in enumerate(heads):
            lh = lse_v[:, h:h + 1]
            pl_ = jnp.exp(sl * SCALE + b_ref[0, h] - lh)
            pc = jnp.exp(sc * SCALE - lh)
            dsl = pl_ * (dpl - delta)
            dsc = pc * (dpc - delta)
            db_ref[0, h] += dsl
            grads.append((qm, dom, pl_.astype(bf16), pc.astype(bf16), dsl.astype(bf16), dsc.astype(bf16)))
        for pr in range(NH // 2):
            ps = slice(pr * 2 * HD, (pr + 1) * 2 * HD)
            kw, kc = ops[pr]
            dqs = []
            dkw = dvw = dkc = dvc = None
            for s in range(2):
                qm, dom, plb, pcb, dslb, dscb = grads[2 * pr + s]
                dqs.append(_dot(dslb, kw) + _dot(dscb, kc))
                parts = (_dg(dslb, qm, TN), _dg(plb, dom, TN), _dg(dscb, qm, TN), _dg(pcb, dom, TN))
                if s == 0:
                    dkw, dvw, dkc, dvc = parts
                else:
                    dkw, dvw, dkc, dvc = dkw + parts[0], dvw + parts[1], dkc + parts[2], dvc + parts[3]
            dq_ref[:, ps] = (jnp.where(lo, dqs[0], dqs[1]) * SCALE).astype(bf16)
            dk_s[win, ps] += dkw * SCALE
            dv_s[win, ps] += dvw
            dk_s[T:TA, ps] += dkc * SCALE
            dv_s[T:TA, ps] += dvc

        @pl.when(r == GW - 1)
        def _():
            c1 = pltpu.make_async_copy(dk_s, dk_hbm, sem.at[0])
            c2 = pltpu.make_async_copy(dv_s, dv_hbm, sem.at[1])
            c1.start()
            c2.start()
            c1.wait()
            c2.wait()

        for _, finish in plans:
            pl.when(r == GW - 1)(finish)

    rowq = lambda: pl.BlockSpec((GW, DA), lambda r: (r, 0))
    hbm = pl.BlockSpec(memory_space=pl.ANY)
    return pl.pallas_call(
        body, name="attn_bwd", grid=(GW,),
        in_specs=[rowq(), _full((TA, DA)), _full((TA, DA)), rowq(), rowq(), pl.BlockSpec((GW, NH), lambda r: (r, 0)),
                  pl.BlockSpec((1, NH, GW, WR * GW), lambda r: (_pattern(r), 0, 0, 0))] + [hbm] * nb,
        out_specs=[rowq(), hbm, hbm, pl.BlockSpec((1, NH, GW, WR * GW), lambda r: (_pattern(r), 0, 0, 0))] + [hbm] * nb,
        out_shape=[jax.ShapeDtypeStruct((T, DA), bf16), jax.ShapeDtypeStruct((TA, DA), f32), jax.ShapeDtypeStruct((TA, DA), f32),
                   jax.ShapeDtypeStruct((8, NH, GW, WR * GW), f32)] + [jax.ShapeDtypeStruct(b.shape, b.dtype) for b in blocks],
        scratch_shapes=[pltpu.VMEM((TA, DA), f32), pltpu.VMEM((TA, DA), f32), pltpu.SemaphoreType.DMA((2,))] + _a2a_scratch() * nb,
        compiler_params=_cp("arbitrary"),
    )(q, k, v, y, dy, lse, bias_tab, *blocks)


def _sum_blocks(recv, name):
    _, r, n = recv.shape
    tr = next(cand for cand in (176, 128, 64, 32, 16) if r % cand == 0)

    def body(a_ref, o_ref):
        acc = a_ref[0].astype(f32)
        for d in range(1, NDEV):
            acc = acc + a_ref[d].astype(f32)
        o_ref[...] = acc

    return pl.pallas_call(
        body, name=name, grid=(r // tr,),
        in_specs=[pl.BlockSpec((NDEV, tr, n), lambda i: (0, i, 0))],
        out_specs=pl.BlockSpec((tr, n), lambda i: (i, 0)),
        out_shape=jax.ShapeDtypeStruct((r, n), f32),
        compiler_params=_cp("parallel"),
    )(recv)


def _rpb_reduce(dbias):
    rev = np.zeros((WR * GW, WR * GW), np.float32)
    for i in range(WR):
        for kk in range(GW):
            rev[i * GW + kk, i * GW + GW - 1 - kk] = 1.0

    def body(d_ref, rev_ref, o_ref):
        rv = rev_ref[...]
        for h in range(NH):
            dv = d_ref[0, h]
            r0 = dv.astype(bf16)
            e1 = dv - r0.astype(f32)
            r1 = e1.astype(bf16)
            r2 = (e1 - r1.astype(f32)).astype(bf16)
            rr = _dot(r0, rv) + _dot(r1, rv) + _dot(r2, rv)
            skew = pltpu.roll(rr, 0, 1, stride=1, stride_axis=0)
            o_ref[0, h:h + 1, :] = jnp.sum(skew, axis=0, keepdims=True)

    return pl.pallas_call(
        body, name="rpb_reduce", grid=(8,),
        in_specs=[pl.BlockSpec((1, NH, GW, WR * GW), lambda p: (p, 0, 0, 0)), _full((WR * GW, WR * GW))],
        out_specs=pl.BlockSpec((1, NH, WR * GW), lambda p: (p, 0, 0)),
        out_shape=jax.ShapeDtypeStruct((8, NH, WR * GW), f32),
        compiler_params=_cp("parallel"),
    )(dbias, jnp.asarray(rev, dtype=bf16))


def _norm1_bwd(dq, dk, dv, da, dg, w_inT, x0, ctx0, h, dx1, modv, g1):
    tm = 256
    nt = TA // tm
    nx = T // tm

    def body(dq_ref, dk_ref, dv_ref, da_ref, dg_ref, w_ref, x_ref, c_ref, h_ref, dx1_ref, mod_ref, g_ref,
             dx_ref, dwo_ref, s_ref, dw_ref):
        i = pl.program_id(0)
        is_ctx = i == nt - 1

        @pl.when(i == 0)
        def _():
            s_ref[...] = jnp.zeros_like(s_ref)
            dw_ref[...] = jnp.zeros_like(dw_ref)

        hb = h_ref[...]
        dkb, dvb = dk_ref[...].astype(bf16), dv_ref[...].astype(bf16)
        dw_ref[DA:2 * DA, :] += _dg(dkb, hb, TN)
        dw_ref[2 * DA:3 * DA, :] += _dg(dvb, hb, TN)
        dh_kv = _dot(dkb, w_ref[DA:2 * DA, :]) + _dot(dvb, w_ref[2 * DA:3 * DA, :])
        gain = g_ref[...]

        @pl.when(is_ctx)
        def _():
            xv = c_ref[...]
            rstd = lax.rsqrt(jnp.mean(xv * xv, axis=-1, keepdims=True) + EPS)
            _, dsh, dsc, dgn = _norm_bwd(dh_kv, xv, gain, mod_ref[7:8, :], rstd)
            s_ref[2:3, :] += dgn
            s_ref[3:4, :] += dsh
            s_ref[4:5, :] += dsc
            dwo_ref[...] = dw_ref[...].astype(bf16)

        @pl.when(jnp.logical_not(is_ctx))
        def _():
            dqb, dab, dgb = dq_ref[...], da_ref[...], dg_ref[...]
            dw_ref[0:DA, :] += _dg(dqb, hb, TN)
            dw_ref[3 * DA:4 * DA, :] += _dg(dab, hb, TN)
            dw_ref[4 * DA:5 * DA, :] += _dg(dgb, hb, TN)
            dh = (dh_kv + _dot(dqb, w_ref[0:DA, :]) + _dot(dab, w_ref[3 * DA:4 * DA, :])
                  + _dot(dgb, w_ref[4 * DA:5 * DA, :]))
            xv = x_ref[...]
            rstd = lax.rsqrt(jnp.mean(xv * xv, axis=-1, keepdims=True) + EPS)
            dxn, dsh, dsc, dgn = _norm_bwd(dh, xv, gain, mod_ref[1:2, :], rstd)
            dx_ref[...] = dx1_ref[...] + dxn
            s_ref[0:1, :] += dsh
            s_ref[1:2, :] += dsc
            s_ref[2:3, :] += dgn

    row = lambda w: pl.BlockSpec((tm, w), lambda i: (i, 0))
    lrow = lambda w: pl.BlockSpec((tm, w), lambda i: (jnp.minimum(i, nx - 1), 0))
    return pl.pallas_call(
        body, name="norm1_bwd", grid=(nt,),
        in_specs=[lrow(DA), row(DA), row(DA), lrow(DA), lrow(DA), _full((5 * DA, D)), lrow(D), _full((TC, D)), row(D),
                  lrow(D), _full((8, D)), _full((1, D))],
        out_specs=[lrow(D), _full((5 * DA, D)), _full((8, D))],
        out_shape=[jax.ShapeDtypeStruct((T, D), f32), jax.ShapeDtypeStruct((5 * DA, D), bf16), jax.ShapeDtypeStruct((8, D), f32)],
        scratch_shapes=[pltpu.VMEM((5 * DA, D), f32)],
        compiler_params=_cp("arbitrary"),
    )(dq, dk, dv, da, dg, w_inT, x0, ctx0, h, dx1, modv, g1)


def _adamw(w, g, m, v, name):
    r, c = w.shape
    tr = r
    for cand in (256, 128, 64, 32, 16, 8):
        if r % cand == 0 and r > cand:
            tr = cand
            break

    def body(w_ref, g_ref, m_ref, v_ref, d_ref, nm_ref, nv_ref):
        gv = g_ref[...]
        nm = ADAM_B1 * m_ref[...] + (1.0 - ADAM_B1) * gv
        nv = ADAM_B2 * v_ref[...] + (1.0 - ADAM_B2) * (gv * gv)
        m_hat = nm * (1.0 / (1.0 - ADAM_B1 ** ADAM_STEP))
        v_hat = nv * (1.0 / (1.0 - ADAM_B2 ** ADAM_STEP))
        d_ref[...] = -ADAM_LR * (m_hat / (jnp.sqrt(v_hat) + ADAM_EPS) + ADAM_WD * w_ref[...])
        nm_ref[...] = nm
        nv_ref[...] = nv

    spec = pl.BlockSpec((tr, c), lambda i: (i, 0))
    return pl.pallas_call(
        body, name=name, grid=(r // tr,),
        in_specs=[spec] * 4, out_specs=[spec] * 3,
        out_shape=[jax.ShapeDtypeStruct((r, c), f32)] * 3,
        compiler_params=_cp("parallel"),
    )(w, g, m, v)


def _pad_rows128(vec):
    n = vec.shape[0]
    rows = -(-n // 1024) * 8
    return jnp.pad(vec, (0, rows * 128 - n)).reshape(rows, 128)


def _grad_rpb(dbias):
    lane_map, r_hot = _rpb_tables()
    return jnp.einsum("phl,lic,pir->hrc", _rpb_reduce(dbias), jnp.asarray(lane_map), jnp.asarray(r_hot),
                      precision=lax.Precision.HIGHEST)


def kernel(x, c, ctx, c_ctx, w_mod, b_mod, g_norm1, w_in, rpb, conv_w, conv_b, ln_g, ln_b, w_out, g_norm2, w_up, ffn_conv_w, ffn_conv_b, w_down, g_final, loss_target, m_c_ctx, m_w_mod, m_b_mod, m_g_norm1, m_w_in, m_rpb, m_conv_w, m_conv_b, m_ln_g, m_ln_b, m_w_out, m_g_norm2, m_w_up, m_ffn_conv_w, m_ffn_conv_b, m_w_down, m_g_final, v_c_ctx, v_w_mod, v_b_mod, v_g_norm1, v_w_in, v_rpb, v_conv_w, v_conv_b, v_ln_g, v_ln_b, v_w_out, v_g_norm2, v_w_up, v_ffn_conv_w, v_ffn_conv_b, v_w_down, v_g_final):
    me = 4 * lax.axis_index("x") + 2 * lax.axis_index("y") + lax.axis_index("c")
    nmod = w_mod.shape[2]
    n_in = w_in.shape[2]
    n_out = w_out.shape[1]
    n_up = w_up.shape[2]
    n_dn = w_down.shape[1]
    n_cw = conv_w.shape[2]

    w_inT, c_all = _allgather2([w_in[0].T.astype(bf16), c.reshape(8, 128)], "ag_w_in")

    c_all = c_all.reshape(NDEV, D)
    cvec = jnp.concatenate([c_all, c_ctx[None, :], jnp.zeros((7, D), f32)], axis=0)
    b_sh = lax.dynamic_slice(b_mod, (0, me * nmod), (1, nmod))
    mod_sh = _mod_fwd(cvec, w_mod[0], b_sh)
    n_modp = 16 * nmod
    payload = jnp.concatenate([mod_sh.reshape(-1), conv_w[0].reshape(-1), ffn_conv_w[0].reshape(-1)])
    flat = _small_allgather(_pad_rows128(payload), "ag_mod").reshape(NDEV, -1)
    mod_all = flat[:, :n_modp].reshape(NDEV, 16, nmod).transpose(1, 0, 2).reshape(16, 6 * D)
    mod_me = lax.dynamic_index_in_dim(mod_all, me, 0, keepdims=False).reshape(6, D)
    mod_c = mod_all[8]
    modv = jnp.concatenate([mod_me, mod_c[None, 0:D], mod_c[None, D:2 * D]], axis=0)
    o1 = n_modp + CW * n_cw
    conv_w_f = flat[:, n_modp:o1].reshape(NDEV, CW, n_cw).transpose(1, 0, 2).reshape(CW, DA)
    fcw_f = flat[:, o1:o1 + 3 * n_up].reshape(NDEV, 3, n_up).transpose(1, 0, 2).reshape(3, F2)

    x0, ctx0 = x[0], ctx[0]
    h, q, k, v, a, g, w_down_f = _in_proj(x0, ctx0, g_norm1, modv, w_inT, [w_down[0].astype(bf16)])
    bias_tab = _bias_table(rpb[0])
    y_cv, cv, w_out_f = _conf_fwd(a, g, conv_w_f, conv_b, ln_g, ln_b, [w_out[0].astype(bf16)])
    y_na, lse, w_upT = _attn_fwd(q, k, v, bias_tab, [w_up[0].T.astype(bf16)])
    x1, proj, h2 = _out_proj(x0, y_na, y_cv, w_out_f, modv, g_norm2)
    ffn, u_t, u2_t = _ffn_fwd(h2, w_upT, fcw_f, ffn_conv_b, w_down_f)
    dx2, dffn, s_loss = _loss_bwd(ffn, x1, loss_target[0], modv, g_final[None, :])

    dug, duv, dw_up, dw_down, dcwg, dcwv, dcbg, dcbv = _ffn_bwd(h2, dffn, u_t, u2_t, fcw_f, w_down_f)
    dx1, dy_na, dy_cv, dw_out, s_n2 = _norm2_bwd(dug, duv, w_upT, x1, dx2, proj, w_out_f, y_na, y_cv, modv, g_norm2)
    da, dg, dcw, s_cf, rv_down = _conf_bwd(a, g, cv, dy_cv, conv_w_f, ln_g, ln_b, [dw_down.reshape(NDEV, n_dn, D)])
    dq, dk, dv, dbias, rv_up, rv_out = _attn_bwd(q, k, v, y_na, dy_na, lse, bias_tab,
                                                 [dw_up.reshape(NDEV, n_up, D), dw_out.reshape(NDEV, n_out, D)])
    g_w_down = _sum_blocks(rv_down, "sum_w_down")
    g_w_out = _sum_blocks(rv_out, "sum_w_out")
    g_w_upT = _sum_blocks(rv_up, "sum_w_up")
    grad_rpb_part = _grad_rpb(dbias)
    grad_x, dw_inT, s_n1 = _norm1_bwd(dq, dk, dv, da, dg, w_inT, x0, ctx0, h, dx1, modv, g_norm1)
    grad_x = grad_x[None]
    dfcw = jnp.concatenate([dcwg, dcwv], axis=1)
    dfcb = jnp.concatenate([dcbg[0], dcbv[0]])
    small = jnp.concatenate([dcw[:CW].reshape(CW, NDEV, n_cw).transpose(1, 0, 2).reshape(NDEV, CW * n_cw),
                             dfcw.reshape(3, NDEV, n_up).transpose(1, 0, 2).reshape(NDEV, 3 * n_up)], axis=1)
    small = jnp.pad(small.reshape(NDEV, 4, D), ((0, 0), (0, 12), (0, 0))).astype(bf16)
    dmod = jnp.concatenate([s_n1[0], s_n1[1], s_n2[3], s_n2[0], s_n2[1], s_loss[1]])
    dmodc = jnp.concatenate([s_n1[3], s_n1[4]])
    parts = [dmodc, s_n1[2], grad_rpb_part.reshape(-1), s_cf[0], s_cf[1], s_cf[2], s_n2[2], dfcb, s_loss[0], s_loss[3, 0:1]]
    sizes = [p.shape[0] for p in parts]
    pvec = _pad_rows128(jnp.concatenate([dmod] + parts))
    r_a, gath = _reduce_scatter2(jnp.concatenate([dw_inT.reshape(NDEV, n_in, D), small], axis=1), [pvec], "rs_w_in")
    g_w_inT = r_a[:n_in]
    sm = r_a[n_in:n_in + 4].reshape(-1)
    g_conv_w = sm[:CW * n_cw].reshape(CW, n_cw)
    g_fcw = sm[CW * n_cw:].reshape(3, n_up)
    tot = _sum_rows8(gath, "sum_small").reshape(-1)
    dmod_all = gath.reshape(NDEV, -1)[:, :6 * D]
    offs = np.cumsum([6 * D] + sizes)
    pick = lambda j: tot[offs[j]:offs[j + 1]]
    dmodc_t = jnp.pad(pick(0), (0, 4 * D))
    g_b_mod = (tot[:6 * D] + dmodc_t)[None, :]
    g_g_norm1 = pick(1)[None, :]
    g_rpb = pick(2).reshape(1, NH, 2 * WR - 1, 2 * NCOL - 1)
    g_conv_b, g_ln_g, g_ln_b = pick(3)[None, :], pick(4)[None, :], pick(5)[None, :]
    g_g_norm2 = pick(6)[None, :]
    g_fcb = pick(7)[None, :]
    g_g_final = pick(8)
    loss = pick(9)[0]
    dm_rows = jnp.concatenate([dmod_all, dmodc_t[None, :], jnp.zeros((7, 6 * D), f32)], axis=0)
    dm_sh = lax.dynamic_slice(dm_rows, (0, me * nmod), (16, nmod))
    g_w_mod, gc_part = _mod_bwd(cvec, dm_sh, w_mod[0])
    gc_sum = _sum_rows8(_small_allgather(gc_part[0].reshape(8, 128), "ag_cctx"), "sum_cctx").reshape(D)
    sg_c = _sigmoid(c_ctx)
    g_c_ctx = gc_sum * (sg_c * (1.0 + c_ctx * (1.0 - sg_c)))

    big = [("w_mod", w_mod[0], g_w_mod, m_w_mod[0], v_w_mod[0]), ("w_in", w_in[0].T, g_w_inT, m_w_in[0].T, v_w_in[0].T),
           ("w_out", w_out[0], g_w_out, m_w_out[0], v_w_out[0]), ("w_up", w_up[0].T, g_w_upT, m_w_up[0].T, v_w_up[0].T),
           ("w_down", w_down[0], g_w_down, m_w_down[0], v_w_down[0])]
    upd = {n: _adamw(wv, gv, mv, vv, "adamw_" + n) for n, wv, gv, mv, vv in big}
    for n in ("w_in", "w_up"):
        upd[n] = tuple(arr.T for arr in upd[n])
    g_w_in, g_w_up = g_w_inT.T, g_w_upT.T
    smalls = [("c_ctx", c_ctx, g_c_ctx, m_c_ctx, v_c_ctx), ("b_mod", b_mod, g_b_mod, m_b_mod, v_b_mod),
              ("g_norm1", g_norm1, g_g_norm1, m_g_norm1, v_g_norm1), ("rpb", rpb, g_rpb, m_rpb, v_rpb),
              ("conv_w", conv_w, g_conv_w[None], m_conv_w, v_conv_w), ("conv_b", conv_b, g_conv_b, m_conv_b, v_conv_b),
              ("ln_g", ln_g, g_ln_g, m_ln_g, v_ln_g), ("ln_b", ln_b, g_ln_b, m_ln_b, v_ln_b),
              ("g_norm2", g_norm2, g_g_norm2, m_g_norm2, v_g_norm2),
              ("ffn_conv_w", ffn_conv_w, g_fcw[None], m_ffn_conv_w, v_ffn_conv_w),
              ("ffn_conv_b", ffn_conv_b, g_fcb, m_ffn_conv_b, v_ffn_conv_b), ("g_final", g_final, g_g_final, m_g_final, v_g_final)]
    packed = [_pad_rows128(jnp.concatenate([t[j].reshape(-1) for t in smalls])) for j in (1, 2, 3, 4)]
    sd, sm_, sv = _adamw(*packed, "adamw_small")
    so = np.cumsum([0] + [int(np.prod(t[1].shape)) for t in smalls])
    for j, t in enumerate(smalls):
        shp = t[1].shape
        upd[t[0]] = tuple(arr.reshape(-1)[so[j]:so[j + 1]].reshape(shp) for arr in (sd, sm_, sv))
    grads = {"c_ctx": g_c_ctx, "w_mod": g_w_mod[None], "b_mod": g_b_mod, "g_norm1": g_g_norm1, "w_in": g_w_in[None],
             "rpb": g_rpb, "conv_w": g_conv_w[None], "conv_b": g_conv_b, "ln_g": g_ln_g, "ln_b": g_ln_b,
             "w_out": g_w_out[None], "g_norm2": g_g_norm2, "w_up": g_w_up[None], "ffn_conv_w": g_fcw[None],
             "ffn_conv_b": g_fcb, "w_down": g_w_down[None], "g_final": g_g_final}
    names = ["c_ctx", "w_mod", "b_mod", "g_norm1", "w_in", "rpb", "conv_w", "conv_b", "ln_g", "ln_b", "w_out", "g_norm2",
             "w_up", "ffn_conv_w", "ffn_conv_b", "w_down", "g_final"]
    shapes = {n: grads[n].shape for n in names}
    outs = [loss, grad_x] + [grads[n] for n in names]
    for j in range(3):
        outs += [upd[n][j].reshape(shapes[n]) for n in names]
    return tuple(outs)
```

```python
import functools

import numpy as np
import jax
import jax.numpy as jnp
from jax import lax
from jax.experimental import pallas as pl
from jax.experimental.pallas import tpu as pltpu

f32 = jnp.float32
bf16 = jnp.bfloat16

D = 1024
T = 4096
TC = 256
TA = T + TC
DA = 512
NH = 8
HD = 64
GW = 64
WR = 8
NCOL = 16
F = 2816
F2 = 2 * F
CW = 31
NDEV = 8
EPS = 1e-6
SCALE = HD ** -0.5
NEG = -1e30
MESH = pl.DeviceIdType.MESH

NT = (((1,), (1,)), ((), ()))
TN = (((0,), (0,)), ((), ()))

ADAM_LR, ADAM_B1, ADAM_B2, ADAM_EPS, ADAM_WD, ADAM_STEP = 0.001, 0.9, 0.999, 1e-08, 0.01, 10

VMEM_LIMIT = 56 * 1024 * 1024


def _cp(*sem):
    return pltpu.CompilerParams(dimension_semantics=sem or None, vmem_limit_bytes=VMEM_LIMIT)


def _dot(a, b):
    return jnp.dot(a, b, preferred_element_type=f32)


def _dg(a, b, dims):
    return lax.dot_general(a, b, dims, preferred_element_type=f32)


def _sigmoid(x):
    return 1.0 / (1.0 + jnp.exp(-x))


def _full(shape):
    n = len(shape)
    return pl.BlockSpec(shape, lambda *_: (0,) * n)


def _resident(shape):
    n = len(shape)
    return pl.BlockSpec(shape, lambda *_: (0,) * n, pipeline_mode=pl.Buffered(1))


def _my_pos():
    return lax.axis_index("x"), lax.axis_index("y"), lax.axis_index("c")


def _small_gather_plan(v_ref, out_ref, send_sems, recv_sems):
    x, y, c = _my_pos()
    me = 4 * x + 2 * y + c
    peers = []
    for k in range(1, NDEV):
        kx, ky, kc = (k >> 2) & 1, (k >> 1) & 1, k & 1
        peers.append((x ^ kx, y ^ ky, c ^ kc))

    def copy(k, slot, to):
        return pltpu.make_async_remote_copy(
            src_ref=v_ref, dst_ref=out_ref.at[slot], send_sem=send_sems.at[k], recv_sem=recv_sems.at[k],
            device_id=to, device_id_type=MESH)

    def start():
        out_ref[me] = v_ref[...]
        for k, p in enumerate(peers):
            copy(k, me, p).start()

    def finish():
        for k, (px, py, pc) in enumerate(peers):
            copy(k, 4 * px + 2 * py + pc, (x, y, c)).wait_recv()
        for k, p in enumerate(peers):
            copy(k, me, p).wait_send()

    return start, finish


def _small_gather_scratch():
    return [pltpu.SemaphoreType.DMA((NDEV - 1,)), pltpu.SemaphoreType.DMA((NDEV - 1,))]


def _small_allgather(v, name):
    n = v.shape[0]

    def body(v_ref, out_ref, send_sems, recv_sems):
        start, finish = _small_gather_plan(v_ref, out_ref, send_sems, recv_sems)
        start()
        finish()

    return pl.pallas_call(
        body, name=name,
        out_shape=jax.ShapeDtypeStruct((NDEV, n, 128), f32),
        in_specs=[pl.BlockSpec(memory_space=pltpu.VMEM)],
        out_specs=pl.BlockSpec(memory_space=pltpu.VMEM),
        scratch_shapes=_small_gather_scratch(),
    )(v)


def _ag2_plan(x_refs, out_refs, send_sems, recv_sems, local_sems):
    na = len(x_refs)
    x, y, c = _my_pos()
    me, sibling = (x, y, c), (x, y, 1 - c)
    chips = [(1 - x, y), (x, 1 - y), (1 - x, 1 - y)]

    def rows(i, px, py, pc):
        m_per = x_refs[i].shape[0]
        return out_refs[i].at[pl.ds(pl.multiple_of((4 * px + 2 * py + pc) * m_per, 16 if m_per % 16 == 0 else 8), m_per), :]

    def copies(k, block, to, from_shard=False):
        return [pltpu.make_async_remote_copy(
            src_ref=x_refs[i] if from_shard else rows(i, *block), dst_ref=rows(i, *block),
            send_sem=send_sems.at[k * na + i], recv_sem=recv_sems.at[k * na + i], device_id=to, device_id_type=MESH)
            for i in range(na)]

    def mine():
        return [pltpu.make_async_copy(x_refs[i], rows(i, *me), local_sems.at[i]) for i in range(na)]

    def first():
        cps = copies(0, me, sibling, True)
        for j, chip in enumerate(chips):
            cps += copies(1 + j, me, (*chip, c), True)
        return cps

    def start():
        for cp in mine() + first():
            cp.start()

    def forward():
        for j, chip in enumerate(chips):
            for cp in copies(1 + j, (*chip, c), me):
                cp.wait_recv()
            for cp in copies(4 + j, (*chip, c), sibling):
                cp.start()

    def finish():
        for cp in copies(0, sibling, me):
            cp.wait_recv()
        for j, chip in enumerate(chips):
            for cp in copies(4 + j, (*chip, 1 - c), me):
                cp.wait_recv()
        for cp in first():
            cp.wait_send()
        for j, chip in enumerate(chips):
            for cp in copies(4 + j, (*chip, c), sibling):
                cp.wait_send()
        for cp in mine():
            cp.wait()

    return start, forward, finish


def _ag2_scratch(na):
    return [pltpu.SemaphoreType.DMA((7 * na,)), pltpu.SemaphoreType.DMA((7 * na,)), pltpu.SemaphoreType.DMA((na,))]


def _a2a_plan(g_ref, recv_ref, send_sems, recv_sems, local_sem):
    x, y, c = _my_pos()
    me = 4 * x + 2 * y + c
    peers = []
    for k in range(1, NDEV):
        kx, ky, kc = (k >> 2) & 1, (k >> 1) & 1, k & 1
        peers.append((x ^ kx, y ^ ky, c ^ kc))

    def sends():
        return [pltpu.make_async_remote_copy(
            src_ref=g_ref.at[4 * px + 2 * py + pc], dst_ref=recv_ref.at[me], send_sem=send_sems.at[k], recv_sem=recv_sems.at[k],
            device_id=(px, py, pc), device_id_type=MESH) for k, (px, py, pc) in enumerate(peers)]

    def own():
        return pltpu.make_async_copy(g_ref.at[me], recv_ref.at[me], local_sem)

    def start():
        own().start()
        for cp in sends():
            cp.start()

    def finish():
        for k, (px, py, pc) in enumerate(peers):
            pltpu.make_async_remote_copy(
                src_ref=g_ref.at[me], dst_ref=recv_ref.at[4 * px + 2 * py + pc], send_sem=send_sems.at[k],
                recv_sem=recv_sems.at[k], device_id=(x, y, c), device_id_type=MESH).wait_recv()
        for cp in sends():
            cp.wait_send()
        own().wait()

    return start, finish


def _a2a_scratch():
    return [pltpu.SemaphoreType.DMA((NDEV - 1,)), pltpu.SemaphoreType.DMA((NDEV - 1,)), pltpu.SemaphoreType.DMA]


def _allgather2(shards, name):
    na = len(shards)

    def body(*refs):
        start, forward, finish = _ag2_plan(refs[:na], refs[na:2 * na], *refs[2 * na:])
        start()
        forward()
        finish()

    return pl.pallas_call(
        body, name=name,
        out_shape=[jax.ShapeDtypeStruct((NDEV * s.shape[0], s.shape[1]), s.dtype) for s in shards],
        in_specs=[pl.BlockSpec(memory_space=pltpu.VMEM)] * na,
        out_specs=[pl.BlockSpec(memory_space=pltpu.VMEM)] * na,
        scratch_shapes=_ag2_scratch(na),
        compiler_params=pltpu.CompilerParams(vmem_limit_bytes=VMEM_LIMIT),
    )(*shards)


def _reduce_scatter2(g, small, name):
    _, r, n = g.shape
    ch = 16
    nch = r // ch
    ns = len(small)

    def body(g_ref, *rest):
        v_refs, out_ref, vout_refs = rest[:ns], rest[ns], rest[ns + 1:2 * ns + 1]
        a_ref, h_ref, b_ref, s1_send, s1_recv, s2_send, s2_recv = rest[2 * ns + 1:2 * ns + 8]
        gather_sems = rest[2 * ns + 8:]
        gathers = [_small_gather_plan(v_refs[i], vout_refs[i], *gather_sems[2 * i:2 * i + 2]) for i in range(ns)]
        for start, _ in gathers:
            start()
        x, y, c = _my_pos()
        sibling = (x, y, 1 - c)
        s1 = []
        for j in range(4):
            cp = pltpu.make_async_remote_copy(
                src_ref=g_ref.at[2 * j + (1 - c)], dst_ref=a_ref.at[j], send_sem=s1_send.at[j], recv_sem=s1_recv.at[j],
                device_id=sibling, device_id_type=MESH)
            cp.start()
            s1.append(cp)
        for cp in s1:
            cp.wait_recv()

        def add1(i, _):
            rr = pl.ds(pl.multiple_of(i * ch, ch), ch)
            for j in range(4):
                h_ref[j, rr, :] = (g_ref[2 * j + c, rr, :].astype(f32) + a_ref[j, rr, :].astype(f32)).astype(bf16)
            return 0
        lax.fori_loop(0, nch, add1, 0)
        mychip = 2 * x + y
        s2 = []
        for m in range(1, 4):
            mx, my_ = (m >> 1) & 1, m & 1
            px, py = x ^ mx, y ^ my_
            cp = pltpu.make_async_remote_copy(
                src_ref=h_ref.at[2 * px + py], dst_ref=b_ref.at[m - 1], send_sem=s2_send.at[m - 1], recv_sem=s2_recv.at[m - 1],
                device_id=(px, py, c), device_id_type=MESH)
            cp.start()
            s2.append(cp)
        for cp in s2:
            cp.wait_recv()

        def add2(i, _):
            rr = pl.ds(pl.multiple_of(i * ch, ch), ch)
            acc = h_ref[mychip, rr, :].astype(f32)
            for m in range(3):
                acc = acc + b_ref[m, rr, :].astype(f32)
            out_ref[rr, :] = acc
            return 0
        lax.fori_loop(0, nch, add2, 0)
        for cp in s1 + s2:
            cp.wait_send()
        for _, finish in gathers:
            finish()

    vmem = pl.BlockSpec(memory_space=pltpu.VMEM)
    return pl.pallas_call(
        body, name=name,
        out_shape=[jax.ShapeDtypeStruct((r, n), f32)] + [jax.ShapeDtypeStruct((NDEV,) + v.shape, v.dtype) for v in small],
        in_specs=[vmem] * (1 + ns),
        out_specs=[vmem] * (1 + ns),
        scratch_shapes=[pltpu.VMEM((4, r, n), bf16), pltpu.VMEM((4, r, n), bf16), pltpu.VMEM((3, r, n), bf16),
                        pltpu.SemaphoreType.DMA((4,)), pltpu.SemaphoreType.DMA((4,)),
                        pltpu.SemaphoreType.DMA((3,)), pltpu.SemaphoreType.DMA((3,))] + _small_gather_scratch() * ns,
        compiler_params=pltpu.CompilerParams(vmem_limit_bytes=VMEM_LIMIT),
    )(g, *small)


def _mod_fwd(cvec, w_sh, b_sh):
    def body(c_ref, w_ref, b_ref, o_ref):
        cv = c_ref[...]
        act = (cv * _sigmoid(cv)).astype(bf16)
        o_ref[...] = _dot(act, w_ref[...].astype(bf16)) + b_ref[...]
    return pl.pallas_call(body, name="mod_fwd", out_shape=jax.ShapeDtypeStruct((16, w_sh.shape[1]), f32))(cvec, w_sh, b_sh)


def _mod_bwd(cvec, dm_sh, w_sh):
    def body(c_ref, dm_ref, w_ref, gw_ref, gc_ref):
        cv = c_ref[...]
        act = (cv * _sigmoid(cv)).astype(bf16)
        gw_ref[...] = _dg(act, dm_ref[...].astype(bf16), TN)
        gc_ref[...] = _dg(dm_ref[8:16, :].astype(bf16), w_ref[...].astype(bf16), NT)
    return pl.pallas_call(
        body, name="mod_bwd",
        out_shape=(jax.ShapeDtypeStruct(w_sh.shape, f32), jax.ShapeDtypeStruct((8, D), f32)))(cvec, dm_sh, w_sh)


def _sum_rows8(a, name):
    n = a.shape[1]

    def body(a_ref, o_ref):
        acc = a_ref[0]
        for d in range(1, NDEV):
            acc = acc + a_ref[d]
        o_ref[...] = acc
    return pl.pallas_call(body, name=name, out_shape=jax.ShapeDtypeStruct((n, 128), f32))(a)


def _in_proj(x0, ctx0, g1, modv, w_inT, shards):
    tm = 256
    nt = TA // tm
    nx = T // tm
    na = len(shards)

    def body(x_ref, c_ref, g_ref, mod_ref, w_ref, *rest):
        x_refs, (h_ref, q_ref, k_ref, v_ref, a_ref, gg_ref) = rest[:na], rest[na:na + 6]
        out_refs, sems = rest[na + 6:2 * na + 6], rest[2 * na + 6:]
        i = pl.program_id(0)
        if na:
            start, forward, finish = _ag2_plan(x_refs, out_refs, *sems)
            pl.when(i == 0)(start)
            pl.when(i == nt - 2)(forward)
        is_ctx = i == nt - 1
        xv = jnp.where(is_ctx, c_ref[...], x_ref[...])
        rstd = lax.rsqrt(jnp.mean(xv * xv, axis=-1, keepdims=True) + EPS)
        sh = jnp.where(is_ctx, mod_ref[6:7, :], mod_ref[0:1, :])
        sc = jnp.where(is_ctx, mod_ref[7:8, :], mod_ref[1:2, :])
        h = ((xv * rstd * g_ref[...]) * (1.0 + sc) + sh).astype(bf16)
        h_ref[...] = h
        for j, o_ref in enumerate((q_ref, k_ref, v_ref, a_ref, gg_ref)):
            o_ref[...] = _dg(h, w_ref[j * DA:(j + 1) * DA, :], NT).astype(o_ref.dtype)
        if na:
            pl.when(is_ctx)(finish)

    row = lambda w: pl.BlockSpec((tm, w), lambda i: (i, 0))
    hbm = pl.BlockSpec(memory_space=pl.ANY)
    return pl.pallas_call(
        body, name="in_proj", grid=(nt,),
        in_specs=[pl.BlockSpec((tm, D), lambda i: (jnp.minimum(i, nx - 1), 0)), _full((TC, D)),
                  _full((1, D)), _full((8, D)), _full((5 * DA, D))] + [hbm] * na,
        out_specs=[row(D), row(DA), row(DA), row(DA), row(DA), row(DA)] + [hbm] * na,
        out_shape=[jax.ShapeDtypeStruct((TA, D), bf16)] + [jax.ShapeDtypeStruct((TA, DA), bf16)] * 3
                  + [jax.ShapeDtypeStruct((TA, DA), f32)] * 2
                  + [jax.ShapeDtypeStruct((NDEV * sh.shape[0], sh.shape[1]), sh.dtype) for sh in shards],
        scratch_shapes=_ag2_scratch(na) if na else [],
        compiler_params=_cp("arbitrary"),
    )(x0, ctx0, g1, modv, w_inT, *shards)


def _win_start(r):
    return jnp.clip(r - WR // 2, 0, GW - WR)


def _pattern(r):
    return _win_start(r) - r + (WR - 1)


def _bias_table(rpb):
    qc = np.arange(GW)[:, None]
    kc = np.arange(GW)[None, :]
    cs = np.clip(qc - NCOL // 2, 0, GW - NCOL)
    valid = np.tile(((kc >= cs) & (kc < cs + NCOL)).astype(np.int32), (1, WR))
    pad = jnp.pad(rpb, ((0, 0), (0, 0), (0, GW - (2 * NCOL - 1))))
    base = jnp.stack([pad[:, p:p + WR, :].reshape(NH, WR * GW) for p in range(8)])

    def body(base_ref, valid_ref, o_ref):
        ok = valid_ref[...] != 0
        for h in range(NH):
            row = jnp.broadcast_to(base_ref[0, h:h + 1, :], (GW, WR * GW))
            skew = pltpu.roll(row, WR * GW - (NCOL - 1), 1, stride=1, stride_axis=0)
            o_ref[0, h] = jnp.where(ok, skew, NEG)

    return pl.pallas_call(
        body, name="bias_table", grid=(8,),
        in_specs=[pl.BlockSpec((1, NH, WR * GW), lambda p: (p, 0, 0)), _full((GW, WR * GW))],
        out_specs=pl.BlockSpec((1, NH, GW, WR * GW), lambda p: (p, 0, 0, 0)),
        out_shape=jax.ShapeDtypeStruct((8, NH, GW, WR * GW), f32),
        compiler_params=_cp("parallel"),
    )(base, jnp.asarray(valid))


def _rpb_tables():
    lane_map = np.zeros((WR * GW, WR, 2 * NCOL - 1), np.float32)
    for i in range(WR):
        for t in range(GW):
            if t >= GW - NCOL:
                lane_map[i * GW + t, i, t - (GW - NCOL)] = 1.0
            elif t < NCOL - 1:
                lane_map[i * GW + t, (i - 1) % WR, t + NCOL] = 1.0
    p = np.arange(8)[:, None]
    i = np.arange(WR)[None, :]
    r_hot = ((p + i)[:, :, None] == np.arange(2 * WR - 1)[None, None, :]).astype(np.float32)
    return lane_map, r_hot


AG_FORWARD_ROW = 52


def _attn_fwd(q, k, v, bias_tab, shards):
    na = len(shards)

    def body(q_ref, k_ref, v_ref, b_ref, *rest):
        x_refs, (y_ref, lse_ref), out_refs, sems = rest[:na], rest[na:na + 2], rest[na + 2:2 * na + 2], rest[2 * na + 2:]
        r = pl.program_id(0)
        if na:
            start, forward, finish = _ag2_plan(x_refs, out_refs, *sems)
            pl.when(r == 0)(start)
            pl.when(r == AG_FORWARD_ROW)(forward)
        ks = pl.multiple_of(_win_start(r) * GW, GW)
        qq = q_ref[...]
        lo = lax.broadcasted_iota(jnp.int32, (GW, 2 * HD), 1) < HD
        kv, scores = [], []
        for pr in range(NH // 2):
            ps = slice(pr * 2 * HD, (pr + 1) * 2 * HD)
            q2 = qq[:, ps]
            kw, kc = k_ref[pl.ds(ks, WR * GW), ps], k_ref[T:TA, ps]
            kv.append((v_ref[pl.ds(ks, WR * GW), ps], v_ref[T:TA, ps]))
            for s in range(2):
                qm = jnp.where(lo if s == 0 else ~lo, q2, jnp.zeros_like(q2))
                scores.append((_dg(qm, kw, NT) * SCALE + b_ref[0, 2 * pr + s], _dg(qm, kc, NT) * SCALE))
        probs = []
        for h, (sl, sc) in enumerate(scores):
            m = jnp.maximum(jnp.max(sl, axis=-1, keepdims=True), jnp.max(sc, axis=-1, keepdims=True))
            pl_ = jnp.exp(sl - m)
            pc = jnp.exp(sc - m)
            l = jnp.sum(pl_, axis=-1, keepdims=True) + jnp.sum(pc, axis=-1, keepdims=True)
            lse_ref[:, h:h + 1] = m + jnp.log(l)
            probs.append((pl_.astype(bf16), pc.astype(bf16), 1.0 / l))
        for pr in range(NH // 2):
            ps = slice(pr * 2 * HD, (pr + 1) * 2 * HD)
            vw, vc = kv[pr]
            outs = [(_dot(pb, vw) + _dot(cb, vc)) * rl for pb, cb, rl in probs[2 * pr:2 * pr + 2]]
            y_ref[:, ps] = jnp.where(lo, outs[0], outs[1]).astype(bf16)
        if na:
            pl.when(r == GW - 1)(finish)

    hbm = pl.BlockSpec(memory_space=pl.ANY)
    return pl.pallas_call(
        body, name="attn_fwd", grid=(GW,),
        in_specs=[pl.BlockSpec((GW, DA), lambda r: (r, 0)), _full((TA, DA)), _full((TA, DA)),
                  pl.BlockSpec((1, NH, GW, WR * GW), lambda r: (_pattern(r), 0, 0, 0))] + [hbm] * na,
        out_specs=[pl.BlockSpec((GW, DA), lambda r: (r, 0)), pl.BlockSpec((GW, NH), lambda r: (r, 0))] + [hbm] * na,
        out_shape=[jax.ShapeDtypeStruct((T, DA), bf16), jax.ShapeDtypeStruct((T, NH), f32)]
                  + [jax.ShapeDtypeStruct((NDEV * s.shape[0], s.shape[1]), s.dtype) for s in shards],
        scratch_shapes=_ag2_scratch(na) if na else [],
        compiler_params=_cp("arbitrary"),
    )(q, k, v, bias_tab, *shards)


CONV_TT = 512
HALO = 16


def _halo_specs(tt, w, nrows_blocks):
    per = tt // HALO
    prev = pl.BlockSpec((HALO, w), lambda i: (jnp.maximum(i * per - 1, 0), 0))
    cur = pl.BlockSpec((tt, w), lambda i: (i, 0))
    nxt = pl.BlockSpec((HALO, w), lambda i: (jnp.minimum((i + 1) * per, nrows_blocks - 1), 0))
    return [prev, cur, nxt]


def _shifted_copies(rot, wn):
    for b in range(1, 8):
        rot[b, 0:wn - 8, :] = rot[0, pl.ds(b, wn - 8), :]


def _conf_fwd(a, g, conv_w, conv_b, ln_g, ln_b, shards):
    tt = CONV_TT
    nt = T // tt
    sub = 32
    wn = tt + 2 * HALO
    na = len(shards)

    def body(ap, ac, an, gp, gc, gn, w_ref, b_ref, lg_ref, lb_ref, *rest):
        x_refs, (y_ref, cv_ref), out_refs, (rot, *sems) = rest[:na], rest[na:na + 2], rest[na + 2:2 * na + 2], rest[2 * na + 2:]
        i = pl.program_id(0)
        if na:
            start, forward, finish = _ag2_plan(x_refs, out_refs, *sems)
            pl.when(i == 0)(start)
            pl.when(i == nt // 2)(forward)
        rot[0, 0:HALO, :] = jnp.where(i > 0, ap[...] * _sigmoid(gp[...]), 0.0)
        rot[0, HALO:HALO + tt, :] = ac[...] * _sigmoid(gc[...])
        rot[0, HALO + tt:, :] = jnp.where(i < nt - 1, an[...] * _sigmoid(gn[...]), 0.0)
        _shifted_copies(rot, wn)
        w = w_ref[...]
        for s in range(tt // sub):
            acc = jnp.zeros((sub, DA), f32)
            for j in range(CW):
                a8, b8 = divmod(1 + j, 8)
                acc = acc + rot[b8, pl.ds(s * sub + 8 * a8, sub), :] * w[j:j + 1, :]
            cv = acc + b_ref[...]
            cv_ref[pl.ds(s * sub, sub), :] = cv
            mu = jnp.mean(cv, axis=-1, keepdims=True)
            xc = cv - mu
            rstd = lax.rsqrt(jnp.mean(xc * xc, axis=-1, keepdims=True) + EPS)
            z = xc * rstd * lg_ref[...] + lb_ref[...]
            y_ref[pl.ds(s * sub, sub), :] = (z * _sigmoid(z)).astype(bf16)
        if na:
            pl.when(i == nt - 1)(finish)

    hs = _halo_specs(tt, DA, T // HALO)
    hbm = pl.BlockSpec(memory_space=pl.ANY)
    return pl.pallas_call(
        body, name="conf_fwd", grid=(nt,),
        in_specs=hs + hs + [_full((CW, DA)), _full((1, DA)), _full((1, DA)), _full((1, DA))] + [hbm] * na,
        out_specs=[pl.BlockSpec((tt, DA), lambda i: (i, 0)), pl.BlockSpec((tt, DA), lambda i: (i, 0))] + [hbm] * na,
        out_shape=[jax.ShapeDtypeStruct((T, DA), bf16), jax.ShapeDtypeStruct((T, DA), f32)]
                  + [jax.ShapeDtypeStruct((NDEV * s.shape[0], s.shape[1]), s.dtype) for s in shards],
        scratch_shapes=[pltpu.VMEM((8, wn, DA), f32)] + (_ag2_scratch(na) if na else []),
        compiler_params=_cp("arbitrary"),
    )(a, a, a, g, g, g, conv_w, conv_b, ln_g, ln_b, *shards)


def _out_proj(xa, y_na, y_cv, w_out, modv, g2):
    tm = 512

    def body(x_ref, ya_ref, yc_ref, w_ref, mod_ref, g_ref, x1_ref, pj_ref, h2_ref):
        proj = _dot(ya_ref[...], w_ref[0:DA, :]) + _dot(yc_ref[...], w_ref[DA:D, :])
        x1 = x_ref[...] + mod_ref[2:3, :] * proj
        x1_ref[...] = x1
        pj_ref[...] = proj.astype(bf16)
        rstd = lax.rsqrt(jnp.mean(x1 * x1, axis=-1, keepdims=True) + EPS)
        h2_ref[...] = ((x1 * rstd * g_ref[...]) * (1.0 + mod_ref[4:5, :]) + mod_ref[3:4, :]).astype(bf16)

    row = lambda w: pl.BlockSpec((tm, w), lambda i: (i, 0))
    return pl.pallas_call(
        body, name="out_proj", grid=(T // tm,),
        in_specs=[row(D), row(DA), row(DA), _full((D, D)), _full((8, D)), _full((1, D))],
        out_specs=[row(D), row(D), row(D)],
        out_shape=[jax.ShapeDtypeStruct((T, D), f32), jax.ShapeDtypeStruct((T, D), bf16), jax.ShapeDtypeStruct((T, D), bf16)],
        compiler_params=_cp("parallel"),
    )(xa, y_na, y_cv, w_out, modv, g2)


FFN_TT = 2048
FFN_CT = 256
FFN_NC = F // FFN_CT
FFN_SUB = 32


def _row_neighbours(ref, r, n):
    blk = ref[pl.ds(r - 8, n + 16), :]
    return blk[8:8 + n, :], pltpu.roll(blk, 1, 0)[8:8 + n, :], pltpu.roll(blk, n + 15, 0)[8:8 + n, :]


def _ffn_specs(tt, ct, by_token_first):
    tc = (lambda f: (lambda t, c: f(t, c))) if by_token_first else (lambda f: (lambda c, t: f(t, c)))
    per = tt // HALO
    halo = [pl.BlockSpec((HALO, D), tc(lambda t, c: (jnp.maximum(t * per - 1, 0), 0))),
            pl.BlockSpec((tt, D), tc(lambda t, c: (t, 0))),
            pl.BlockSpec((HALO, D), tc(lambda t, c: (jnp.minimum((t + 1) * per, T // HALO - 1), 0)))]
    weights = [pl.BlockSpec((ct, D), tc(lambda t, c: (c, 0))), pl.BlockSpec((ct, D), tc(lambda t, c: (c + FFN_NC, 0))),
               pl.BlockSpec((3, ct), tc(lambda t, c: (0, c))), pl.BlockSpec((3, ct), tc(lambda t, c: (0, c + FFN_NC))),
               pl.BlockSpec((1, ct), tc(lambda t, c: (0, c))), pl.BlockSpec((1, ct), tc(lambda t, c: (0, c + FFN_NC))),
               pl.BlockSpec((ct, D), tc(lambda t, c: (c, 0)))]
    return halo, weights


def _ffn_fwd(h2, w_upT, fcw, fcb, w_down):
    tt, ct = FFN_TT, FFN_CT
    nt = T // tt
    wn = tt + 2 * HALO
    half = tt // 2

    def body(hp, hc, hn, wg_ref, wv_ref, cwg_ref, cwv_ref, cbg_ref, cbv_ref, wd_ref, o_ref, u_ref, u2_ref, hwin, uwin, act):
        t = pl.program_id(0)
        c = pl.program_id(1)

        @pl.when(c == 0)
        def _():
            hwin[0:HALO, :] = jnp.where(t > 0, hp[...], jnp.zeros_like(hp[...]))
            hwin[HALO:HALO + tt, :] = hc[...]
            hwin[HALO + tt:, :] = jnp.where(t < nt - 1, hn[...], jnp.zeros_like(hn[...]))
            o_ref[...] = jnp.zeros_like(o_ref)

        for r0, r1 in ((0, half + 2 * HALO), (half + 2 * HALO, wn)):
            hw = hwin[r0:r1, :]
            uwin[r0:r1, :ct] = _dg(hw, wg_ref[...], NT)
            uwin[r0:r1, ct:] = _dg(hw, wv_ref[...], NT)
        cw = jnp.concatenate([cwg_ref[...], cwv_ref[...]], axis=1)
        cb = jnp.concatenate([cbg_ref[...], cbv_ref[...]], axis=1)
        for p in range(2):
            for r in range(p * half, (p + 1) * half, FFN_SUB):
                uc, prev, nxt = _row_neighbours(uwin, HALO + r, FFN_SUB)
                u2 = prev * cw[0:1, :] + uc * cw[1:2, :] + nxt * cw[2:3, :] + cb
                u_ref[r:r + FFN_SUB, :] = uc.astype(bf16)
                u2_ref[r:r + FFN_SUB, :] = u2
                gate = u2[:, :ct]
                act[r:r + FFN_SUB, :] = (gate * _sigmoid(gate) * u2[:, ct:]).astype(bf16)
            rows = slice(p * half, (p + 1) * half)
            o_ref[rows, :] += _dot(act[rows, :], wd_ref[...])

    halo, weights = _ffn_specs(tt, ct, True)
    pair = pl.BlockSpec((tt, 2 * ct), lambda t, c: (t, c))
    return pl.pallas_call(
        body, name="ffn_fwd", grid=(nt, FFN_NC),
        in_specs=halo + weights,
        out_specs=[pl.BlockSpec((tt, D), lambda t, c: (t, 0)), pair, pair],
        out_shape=[jax.ShapeDtypeStruct((T, D), f32), jax.ShapeDtypeStruct((T, F2), bf16), jax.ShapeDtypeStruct((T, F2), f32)],
        scratch_shapes=[pltpu.VMEM((wn, D), bf16), pltpu.VMEM((wn, 2 * ct), f32), pltpu.VMEM((tt, ct), bf16)],
        compiler_params=_cp("parallel", "arbitrary"),
    )(h2, h2, h2, w_upT, w_upT, fcw, fcw, fcb, fcb, w_down)


def _loss_bwd(ffn, x1, tgt, modv, gf):
    tm = 1024
    nt = T // tm

    def body(f_ref, x1_ref, t_ref, mod_ref, g_ref, dx2_ref, df_ref, s_ref):
        i = pl.program_id(0)

        @pl.when(i == 0)
        def _():
            s_ref[...] = jnp.zeros_like(s_ref)

        ff = f_ref[...]
        gt2 = mod_ref[5:6, :]
        x2 = x1_ref[...] + gt2 * ff
        rstd = lax.rsqrt(jnp.mean(x2 * x2, axis=-1, keepdims=True) + EPS)
        xh = x2 * rstd
        gfv = g_ref[...]
        e = xh * gfv - t_ref[...]
        dy = e * (1.0 / D)
        dxh = dy * gfv
        dx2 = rstd * (dxh - xh * jnp.mean(dxh * xh, axis=-1, keepdims=True))
        dx2_ref[...] = dx2
        df_ref[...] = (dx2 * gt2).astype(bf16)
        s_ref[0:1, :] += jnp.sum(dy * xh, axis=0, keepdims=True)
        s_ref[1:2, :] += jnp.sum(dx2 * ff, axis=0, keepdims=True)
        s_ref[2:3, :] += jnp.sum(e * e, axis=0, keepdims=True)

        @pl.when(i == nt - 1)
        def _():
            tot = jnp.sum(s_ref[2:3, :], axis=-1, keepdims=True) * (0.5 / D)
            s_ref[3:4, :] = jnp.broadcast_to(tot, (1, D))

    row = lambda: pl.BlockSpec((tm, D), lambda i: (i, 0))
    return pl.pallas_call(
        body, name="loss_bwd", grid=(nt,),
        in_specs=[row(), row(), row(), _full((8, D)), _full((1, D))],
        out_specs=[row(), row(), _full((8, D))],
        out_shape=[jax.ShapeDtypeStruct((T, D), f32), jax.ShapeDtypeStruct((T, D), bf16), jax.ShapeDtypeStruct((8, D), f32)],
        compiler_params=_cp("arbitrary"),
    )(ffn, x1, tgt, modv, gf)


def _ffn_bwd(h2, dffn, u_t, u2_t, fcw, w_down):
    tt, ct = FFN_TT, FFN_CT
    nt = T // tt
    wn = tt + 2 * HALO
    half = tt // 2

    def body(dp, dc, dn, hc, uc_ref, u2p, u2c, u2n, cwg_ref, cwv_ref, wd_ref,
             dug_ref, duv_ref, dwu_ref, dwd_ref, dcwg_ref, dcwv_ref, dcbg_ref, dcbv_ref,
             dwin, d2win, dawin, accu, accd, act, du):
        t = pl.program_id(1)
        first, last = t == 0, t == nt - 1
        zero = jnp.zeros((HALO, D), bf16)
        dwin[0:HALO, :] = jnp.where(first, zero, dp[...])
        dwin[HALO:HALO + tt, :] = dc[...]
        dwin[HALO + tt:, :] = jnp.where(last, zero, dn[...])

        @pl.when(first)
        def _():
            for r in (accu, accd, dcwg_ref, dcwv_ref, dcbg_ref, dcbv_ref):
                r[...] = jnp.zeros_like(r)

        cw = jnp.concatenate([cwg_ref[...], cwv_ref[...]], axis=1)
        split = half + 2 * HALO
        for r0, r1 in ((0, split), (split, wn)):
            dawin[r0:r1, :] = _dg(dwin[r0:r1, :], wd_ref[...], NT)

        def grads(u2v, dact):
            gate, val = u2v[:, :ct], u2v[:, ct:]
            sg = _sigmoid(gate)
            silu = gate * sg
            return dact * val * (sg * (1.0 + gate * (1.0 - sg))), dact * silu, silu * val

        for blk, r0 in ((u2p, 0), (u2n, HALO + tt)):
            dgate, dval, _ = grads(blk[...], dawin[r0:r0 + HALO, :])
            d2win[r0:r0 + HALO, :ct] = dgate
            d2win[r0:r0 + HALO, ct:] = dval
        for p in range(2):
            rows = slice(p * half, (p + 1) * half)
            for r in range(p * half, (p + 1) * half, FFN_SUB):
                dgate, dval, av = grads(u2c[r:r + FFN_SUB, :], dawin[HALO + r:HALO + r + FFN_SUB, :])
                d2win[HALO + r:HALO + r + FFN_SUB, :ct] = dgate
                d2win[HALO + r:HALO + r + FFN_SUB, ct:] = dval
                act[r:r + FFN_SUB, :] = av.astype(bf16)
            accd[...] += _dg(act[rows, :], dc[rows, :], TN)

        def fold8(x):
            out = x[0:8]
            for k in range(8, FFN_SUB, 8):
                out = out + x[k:k + 8]
            return out

        for p in range(2):
            rows = slice(p * half, (p + 1) * half)
            sums = [jnp.zeros((8, 2 * ct), f32) for _ in range(4)]
            for r in range(p * half, (p + 1) * half, FFN_SUB):
                d2c, d2m, d2p = _row_neighbours(d2win, HALO + r, FFN_SUB)
                ucur = uc_ref[r:r + FFN_SUB, :].astype(f32)
                sums[0] = sums[0] + fold8(d2c)
                for kk, dd in enumerate((d2p, d2c, d2m)):
                    sums[1 + kk] = sums[1 + kk] + fold8(ucur * dd)
                du[r:r + FFN_SUB, :] = (d2p * cw[0:1, :] + d2c * cw[1:2, :] + d2m * cw[2:3, :]).astype(bf16)
            dcb = jnp.sum(sums[0], axis=0, keepdims=True)
            dcbg_ref[...] += dcb[:, :ct]
            dcbv_ref[...] += dcb[:, ct:]
            for kk in range(3):
                dck = jnp.sum(sums[1 + kk], axis=0, keepdims=True)
                dcwg_ref[kk:kk + 1, :] += dck[:, :ct]
                dcwv_ref[kk:kk + 1, :] += dck[:, ct:]
            dug_ref[rows, :] = du[rows, :ct]
            duv_ref[rows, :] = du[rows, ct:]
            accu[...] += _dg(du[rows, :], hc[rows, :], TN)

        @pl.when(last)
        def _():
            dwu_ref[0] = accu[0:ct, :].astype(bf16)
            dwu_ref[1] = accu[ct:, :].astype(bf16)
            dwd_ref[...] = accd[...].astype(bf16)

    per = tt // HALO
    prev = lambda w: (lambda c, t: (jnp.maximum(t * per - 1, 0), c if w else 0))
    nxt = lambda w: (lambda c, t: (jnp.minimum((t + 1) * per, T // HALO - 1), c if w else 0))
    tile = lambda: pl.BlockSpec((ct, D), lambda c, t: (c, 0))
    lane = lambda r, off: pl.BlockSpec((r, ct), lambda c, t: (0, c + off))
    return pl.pallas_call(
        body, name="ffn_bwd", grid=(FFN_NC, nt),
        in_specs=[pl.BlockSpec((HALO, D), prev(False)), pl.BlockSpec((tt, D), lambda c, t: (t, 0)), pl.BlockSpec((HALO, D), nxt(False)),
                  pl.BlockSpec((tt, D), lambda c, t: (t, 0)), pl.BlockSpec((tt, 2 * ct), lambda c, t: (t, c)),
                  pl.BlockSpec((HALO, 2 * ct), prev(True)), pl.BlockSpec((tt, 2 * ct), lambda c, t: (t, c)),
                  pl.BlockSpec((HALO, 2 * ct), nxt(True)), lane(3, 0), lane(3, FFN_NC), tile()],
        out_specs=[pl.BlockSpec((tt, ct), lambda c, t: (t, c)), pl.BlockSpec((tt, ct), lambda c, t: (t, c)),
                   pl.BlockSpec((2, ct, D), lambda c, t: (0, c, 0)), tile(), lane(3, 0), lane(3, 0), lane(1, 0), lane(1, 0)],
        out_shape=[jax.ShapeDtypeStruct((T, F), bf16), jax.ShapeDtypeStruct((T, F), bf16),
                   jax.ShapeDtypeStruct((2, F, D), bf16), jax.ShapeDtypeStruct((F, D), bf16),
                   jax.ShapeDtypeStruct((3, F), f32), jax.ShapeDtypeStruct((3, F), f32),
                   jax.ShapeDtypeStruct((1, F), f32), jax.ShapeDtypeStruct((1, F), f32)],
        scratch_shapes=[pltpu.VMEM((wn, D), bf16), pltpu.VMEM((wn, 2 * ct), f32), pltpu.VMEM((wn, ct), f32),
                        pltpu.VMEM((2 * ct, D), f32), pltpu.VMEM((ct, D), f32),
                        pltpu.VMEM((tt, ct), bf16), pltpu.VMEM((tt, 2 * ct), bf16)],
        compiler_params=_cp("parallel", "arbitrary"),
    )(dffn, dffn, dffn, h2, u_t, u2_t, u2_t, u2_t, fcw, fcw, w_down)


def _norm_bwd(dh, xv, gain, sh_sc, rstd):
    xh = xv * rstd
    n = xh * gain
    dn = dh * (1.0 + sh_sc)
    dxh = dn * gain
    dx = rstd * (dxh - xh * jnp.mean(dxh * xh, axis=-1, keepdims=True))
    return (dx, jnp.sum(dh, axis=0, keepdims=True), jnp.sum(dh * n, axis=0, keepdims=True),
            jnp.sum(dn * xh, axis=0, keepdims=True))


def _norm2_bwd(dug, duv, w_upT, x1, dx2, proj, w_out, y_na, y_cv, modv, g2):
    tm = 512
    nt = T // tm

    def body(dug_ref, duv_ref, w_ref, x1_ref, dx2_ref, pj_ref, wo_ref, ya_ref, yc_ref, mod_ref, g_ref,
             dx1_ref, dya_ref, dyc_ref, dwo_ref, s_ref, acc):
        i = pl.program_id(0)

        @pl.when(i == 0)
        def _():
            s_ref[...] = jnp.zeros_like(s_ref)
            acc[...] = jnp.zeros_like(acc)

        dh2 = _dot(dug_ref[...], w_ref[0:F, :]) + _dot(duv_ref[...], w_ref[F:F2, :])
        x1 = x1_ref[...]
        rstd = lax.rsqrt(jnp.mean(x1 * x1, axis=-1, keepdims=True) + EPS)
        dxn, dsh, dsc, dgn = _norm_bwd(dh2, x1, g_ref[...], mod_ref[4:5, :], rstd)
        dx1 = dx2_ref[...] + dxn
        dx1_ref[...] = dx1
        dpj = (dx1 * mod_ref[2:3, :]).astype(bf16)
        dyc = _dg(dpj, wo_ref[...], NT)
        dya_ref[...] = dyc[:, :DA].astype(bf16)
        dyc_ref[...] = dyc[:, DA:]
        acc[0:DA, :] += _dg(ya_ref[...], dpj, TN)
        acc[DA:D, :] += _dg(yc_ref[...], dpj, TN)

        @pl.when(i == nt - 1)
        def _():
            dwo_ref[...] = acc[...].astype(bf16)

        s_ref[0:1, :] += dsh
        s_ref[1:2, :] += dsc
        s_ref[2:3, :] += dgn
        s_ref[3:4, :] += jnp.sum(dx1 * pj_ref[...].astype(f32), axis=0, keepdims=True)

    row = lambda w: pl.BlockSpec((tm, w), lambda i: (i, 0))
    return pl.pallas_call(
        body, name="norm2_bwd", grid=(nt,),
        in_specs=[row(F), row(F), _resident((F2, D)), row(D), row(D), row(D), _resident((D, D)), row(DA), row(DA),
                  _full((8, D)), _full((1, D))],
        out_specs=[row(D), row(DA), row(DA), _full((D, D)), _full((8, D))],
        out_shape=[jax.ShapeDtypeStruct((T, D), f32), jax.ShapeDtypeStruct((T, DA), bf16), jax.ShapeDtypeStruct((T, DA), f32),
                   jax.ShapeDtypeStruct((D, D), bf16), jax.ShapeDtypeStruct((8, D), f32)],
        scratch_shapes=[pltpu.VMEM((D, D), f32)],
        compiler_params=_cp("arbitrary"),
    )(dug, duv, w_upT, x1, dx2, proj, w_out, y_na, y_cv, modv, g2)


def _conf_bwd(a, g, cv, dy, conv_w, ln_g, ln_b, blocks):
    tt = CONV_TT
    nt = T // tt
    sub = 32
    wn = tt + 2 * HALO
    nb = len(blocks)

    def body(ap, ac, an, gp, gc, gn, cp_, cc, cn, dp, dc, dn, w_ref, lg_ref, lb_ref, *rest):
        g_refs, (da_ref, dg_ref, dcw_ref, s_ref) = rest[:nb], rest[nb:nb + 4]
        recv_refs, (urot, drot, wacc), a2a_sems = rest[nb + 4:2 * nb + 4], rest[2 * nb + 4:2 * nb + 7], rest[2 * nb + 7:]
        i = pl.program_id(0)
        first, last = i == 0, i == nt - 1
        plans = [_a2a_plan(g_refs[k], recv_refs[k], *a2a_sems[3 * k:3 * k + 3]) for k in range(nb)]
        for start, _ in plans:
            pl.when(first)(start)

        @pl.when(first)
        def _():
            s_ref[...] = jnp.zeros_like(s_ref)
            wacc[...] = jnp.zeros_like(wacc)

        lg, lb = lg_ref[...], lb_ref[...]

        def ln_bwd(cvv, dyv):
            mu = jnp.mean(cvv, axis=-1, keepdims=True)
            xc = cvv - mu
            rstd = lax.rsqrt(jnp.mean(xc * xc, axis=-1, keepdims=True) + EPS)
            yn = xc * rstd
            z = yn * lg + lb
            sz = _sigmoid(z)
            dz = dyv * (sz * (1.0 + z * (1.0 - sz)))
            dyn = dz * lg
            dcv = rstd * (dyn - jnp.mean(dyn, axis=-1, keepdims=True) - yn * jnp.mean(dyn * yn, axis=-1, keepdims=True))
            return dcv, dz, yn

        urot[0, 0:HALO, :] = jnp.where(first, 0.0, ap[...] * _sigmoid(gp[...]))
        urot[0, HALO + tt:, :] = jnp.where(last, 0.0, an[...] * _sigmoid(gn[...]))
        drot[0, 0:HALO, :] = jnp.where(first, 0.0, ln_bwd(cp_[...], dp[...])[0])
        drot[0, HALO + tt:, :] = jnp.where(last, 0.0, ln_bwd(cn[...], dn[...])[0])
        for s in range(tt // sub):
            rr = pl.ds(s * sub, sub)
            urot[0, pl.ds(HALO + s * sub, sub), :] = ac[rr, :] * _sigmoid(gc[rr, :])
            dcv, dz, yn = ln_bwd(cc[rr, :], dc[rr, :])
            drot[0, pl.ds(HALO + s * sub, sub), :] = dcv
            s_ref[0:1, :] += jnp.sum(dcv, axis=0, keepdims=True)
            s_ref[1:2, :] += jnp.sum(dz * yn, axis=0, keepdims=True)
            s_ref[2:3, :] += jnp.sum(dz, axis=0, keepdims=True)
        _shifted_copies(urot, wn)
        _shifted_copies(drot, wn)
        w = w_ref[...]
        for s in range(tt // sub):
            rr = pl.ds(s * sub, sub)
            dcv = drot[0, pl.ds(HALO + s * sub, sub), :]
            acc = jnp.zeros((sub, DA), f32)
            for j in range(CW):
                ad, bd = divmod(2 * HALO - 1 - j, 8)
                au, bu = divmod(1 + j, 8)
                acc = acc + drot[bd, pl.ds(s * sub + 8 * ad, sub), :] * w[j:j + 1, :]
                part = urot[bu, pl.ds(s * sub + 8 * au, sub), :] * dcv
                wacc[j] += part[0:8] + part[8:16] + part[16:24] + part[24:32]
            av, gv = ac[rr, :], gc[rr, :]
            sg = _sigmoid(gv)
            da_ref[rr, :] = (acc * sg).astype(bf16)
            dg_ref[rr, :] = (acc * av * sg * (1.0 - sg)).astype(bf16)

        @pl.when(last)
        def _():
            for j in range(CW):
                dcw_ref[j:j + 1, :] = jnp.sum(wacc[j], axis=0, keepdims=True)
            dcw_ref[CW:CW + 1, :] = jnp.zeros((1, DA), f32)

        for _, finish in plans:
            pl.when(last)(finish)

    hs = _halo_specs(tt, DA, T // HALO)
    hbm = pl.BlockSpec(memory_space=pl.ANY)
    return pl.pallas_call(
        body, name="conf_bwd", grid=(nt,),
        in_specs=hs * 4 + [_full((CW, DA)), _full((1, DA)), _full((1, DA))] + [hbm] * nb,
        out_specs=[pl.BlockSpec((tt, DA), lambda i: (i, 0)), pl.BlockSpec((tt, DA), lambda i: (i, 0)),
                   _full((CW + 1, DA)), _full((8, DA))] + [hbm] * nb,
        out_shape=[jax.ShapeDtypeStruct((T, DA), bf16), jax.ShapeDtypeStruct((T, DA), bf16),
                   jax.ShapeDtypeStruct((CW + 1, DA), f32), jax.ShapeDtypeStruct((8, DA), f32)]
                  + [jax.ShapeDtypeStruct(b.shape, b.dtype) for b in blocks],
        scratch_shapes=[pltpu.VMEM((8, wn, DA), f32), pltpu.VMEM((8, wn, DA), f32), pltpu.VMEM((CW, 8, DA), f32)]
                       + _a2a_scratch() * nb,
        compiler_params=_cp("arbitrary"),
    )(a, a, a, g, g, g, cv, cv, cv, dy, dy, dy, conv_w, ln_g, ln_b, *blocks)


def _attn_bwd(q, k, v, y, dy, lse, bias_tab, blocks):
    zr = 256
    nb = len(blocks)

    def body(q_ref, k_ref, v_ref, y_ref, dy_ref, lse_ref, b_ref, *rest):
        g_refs, (dq_ref, dk_hbm, dv_hbm, db_ref) = rest[:nb], rest[nb:nb + 4]
        recv_refs, (dk_s, dv_s, sem), a2a_sems = rest[nb + 4:2 * nb + 4], rest[2 * nb + 4:2 * nb + 7], rest[2 * nb + 7:]
        r = pl.program_id(0)
        plans = [_a2a_plan(g_refs[i], recv_refs[i], *a2a_sems[3 * i:3 * i + 3]) for i in range(nb)]
        for start, _ in plans:
            pl.when(r == 0)(start)

        @pl.when(r == 0)
        def _():
            def z(i, _):
                rr = pl.ds(pl.multiple_of(i * zr, zr), zr)
                dk_s[rr, :] = jnp.zeros((zr, DA), f32)
                dv_s[rr, :] = jnp.zeros((zr, DA), f32)
                return 0
            lax.fori_loop(0, TA // zr, z, 0)

        @pl.when((r <= WR // 2) | (r > GW - WR // 2))
        def _():
            db_ref[...] = jnp.zeros_like(db_ref)

        ks = pl.multiple_of(_win_start(r) * GW, GW)
        win = pl.ds(ks, WR * GW)
        qq, yy, dyy, lse_v = q_ref[...], y_ref[...], dy_ref[...], lse_ref[...]
        lo = lax.broadcasted_iota(jnp.int32, (GW, 2 * HD), 1) < HD
        ops, heads = [], []
        for pr in range(NH // 2):
            ps = slice(pr * 2 * HD, (pr + 1) * 2 * HD)
            q2, do2 = qq[:, ps], dyy[:, ps]
            prod = do2.astype(f32) * yy[:, ps].astype(f32)
            kw, vw = k_ref[win, ps], v_ref[win, ps]
            kc, vc = k_ref[T:TA, ps], v_ref[T:TA, ps]
            ops.append((kw, kc))
            for s in range(2):
                h = 2 * pr + s
                msk = lo if s == 0 else ~lo
                qm = jnp.where(msk, q2, jnp.zeros_like(q2))
                dom = jnp.where(msk, do2, jnp.zeros_like(do2))
                delta = jnp.sum(jnp.where(msk, prod, 0.0), axis=-1, keepdims=True)
                heads.append((qm, dom, delta, _dg(qm, kw, NT), _dg(qm, kc, NT), _dg(dom, vw, NT), _dg(dom, vc, NT)))
        grads = []
        for h, (qm, dom, delta, sl, sc, dpl, dpc) in enumerate(heads):
            lh = lse_v[:, h:h + 1]
            pl_ = jnp.exp(sl * SCALE + b_ref[0, h] - lh)
            pc = jnp.exp(sc * SCALE - lh)
            dsl = pl_ * (dpl - delta)
            dsc = pc * (dpc - delta)
            db_ref[0, h] += dsl
            grads.append((qm, dom, pl_.astype(bf16), pc.astype(bf16), dsl.astype(bf16), dsc.astype(bf16)))
        for pr in range(NH // 2):
            ps = slice(pr * 2 * HD, (pr + 1) * 2 * HD)
            kw, kc = ops[pr]
            dqs = []
            dkw = dvw = dkc = dvc = None
            for s in range(2):
                qm, dom, plb, pcb, dslb, dscb = grads[2 * pr + s]
                dqs.append(_dot(dslb, kw) + _dot(dscb, kc))
                parts = (_dg(dslb, qm, TN), _dg(plb, dom, TN), _dg(dscb, qm, TN), _dg(pcb, dom, TN))
                if s == 0:
                    dkw, dvw, dkc, dvc = parts
                else:
                    dkw, dvw, dkc, dvc = dkw + parts[0], dvw + parts[1], dkc + parts[2], dvc + parts[3]
            dq_ref[:, ps] = (jnp.where(lo, dqs[0], dqs[1]) * SCALE).astype(bf16)
            dk_s[win, ps] += dkw * SCALE
            dv_s[win, ps] += dvw
            dk_s[T:TA, ps] += dkc * SCALE
            dv_s[T:TA, ps] += dvc

        @pl.when(r == GW - 1)
        def _():
            c1 = pltpu.make_async_copy(dk_s, dk_hbm, sem.at[0])
            c2 = pltpu.make_async_copy(dv_s, dv_hbm, sem.at[1])
            c1.start()
            c2.start()
            c1.wait()
            c2.wait()

        for _, finish in plans:
            pl.when(r == GW - 1)(finish)

    rowq = lambda: pl.BlockSpec((GW, DA), lambda r: (r, 0))
    hbm = pl.BlockSpec(memory_space=pl.ANY)
    return pl.pallas_call(
        body, name="attn_bwd", grid=(GW,),
        in_specs=[rowq(), _full((TA, DA)), _full((TA, DA)), rowq(), rowq(), pl.BlockSpec((GW, NH), lambda r: (r, 0)),
                  pl.BlockSpec((1, NH, GW, WR * GW), lambda r: (_pattern(r), 0, 0, 0))] + [hbm] * nb,
        out_specs=[rowq(), hbm, hbm, pl.BlockSpec((1, NH, GW, WR * GW), lambda r: (_pattern(r), 0, 0, 0))] + [hbm] * nb,
        out_shape=[jax.ShapeDtypeStruct((T, DA), bf16), jax.ShapeDtypeStruct((TA, DA), f32), jax.ShapeDtypeStruct((TA, DA), f32),
                   jax.ShapeDtypeStruct((8, NH, GW, WR * GW), f32)] + [jax.ShapeDtypeStruct(b.shape, b.dtype) for b in blocks],
        scratch_shapes=[pltpu.VMEM((TA, DA), f32), pltpu.VMEM((TA, DA), f32), pltpu.SemaphoreType.DMA((2,))] + _a2a_scratch() * nb,
        compiler_params=_cp("arbitrary"),
    )(q, k, v, y, dy, lse, bias_tab, *blocks)


def _sum_blocks(recv, name):
    _, r, n = recv.shape
    tr = next(cand for cand in (176, 128, 64, 32, 16) if r % cand == 0)

    def body(a_ref, o_ref):
        acc = a_ref[0].astype(f32)
        for d in range(1, NDEV):
            acc = acc + a_ref[d].astype(f32)
        o_ref[...] = acc

    return pl.pallas_call(
        body, name=name, grid=(r // tr,),
        in_specs=[pl.BlockSpec((NDEV, tr, n), lambda i: (0, i, 0))],
        out_specs=pl.BlockSpec((tr, n), lambda i: (i, 0)),
        out_shape=jax.ShapeDtypeStruct((r, n), f32),
        compiler_params=_cp("parallel"),
    )(recv)


def _rpb_reduce(dbias):
    rev = np.eye(GW, dtype=np.float32)[::-1]

    def body(d_ref, rev_ref, o_ref):
        rv = rev_ref[...]
        for h in range(NH):
            dv = d_ref[0, h]
            r0 = dv.astype(bf16)
            e1 = dv - r0.astype(f32)
            r1 = e1.astype(bf16)
            r2 = (e1 - r1.astype(f32)).astype(bf16)
            rr = _dot(rv, r0) + _dot(rv, r1) + _dot(rv, r2)
            skew = pltpu.roll(rr, 0, 1, stride=1, stride_axis=0)
            o_ref[0, h:h + 1, :] = jnp.sum(skew, axis=0, keepdims=True)

    return pl.pallas_call(
        body, name="rpb_reduce", grid=(8,),
        in_specs=[pl.BlockSpec((1, NH, GW, WR * GW), lambda p: (p, 0, 0, 0)), _full((GW, GW))],
        out_specs=pl.BlockSpec((1, NH, WR * GW), lambda p: (p, 0, 0)),
        out_shape=jax.ShapeDtypeStruct((8, NH, WR * GW), f32),
        compiler_params=_cp("parallel"),
    )(dbias, jnp.asarray(rev, dtype=bf16))


def _norm1_bwd(dq, dk, dv, da, dg, w_inT, x0, ctx0, h, dx1, modv, g1):
    tm = 256
    nt = TA // tm
    nx = T // tm

    def body(dq_ref, dk_ref, dv_ref, da_ref, dg_ref, w_ref, x_ref, c_ref, h_ref, dx1_ref, mod_ref, g_ref,
             dx_ref, dwo_ref, s_ref, dw_ref):
        i = pl.program_id(0)
        is_ctx = i == nt - 1

        @pl.when(i == 0)
        def _():
            s_ref[...] = jnp.zeros_like(s_ref)
            dw_ref[...] = jnp.zeros_like(dw_ref)

        hb = h_ref[...]
        dkb, dvb = dk_ref[...].astype(bf16), dv_ref[...].astype(bf16)
        dw_ref[DA:2 * DA, :] += _dg(dkb, hb, TN)
        dw_ref[2 * DA:3 * DA, :] += _dg(dvb, hb, TN)
        dh_kv = _dot(dkb, w_ref[DA:2 * DA, :]) + _dot(dvb, w_ref[2 * DA:3 * DA, :])
        gain = g_ref[...]

        @pl.when(is_ctx)
        def _():
            xv = c_ref[...]
            rstd = lax.rsqrt(jnp.mean(xv * xv, axis=-1, keepdims=True) + EPS)
            _, dsh, dsc, dgn = _norm_bwd(dh_kv, xv, gain, mod_ref[7:8, :], rstd)
            s_ref[2:3, :] += dgn
            s_ref[3:4, :] += dsh
            s_ref[4:5, :] += dsc
            dwo_ref[...] = dw_ref[...].astype(bf16)

        @pl.when(jnp.logical_not(is_ctx))
        def _():
            dqb, dab, dgb = dq_ref[...], da_ref[...], dg_ref[...]
            dw_ref[0:DA, :] += _dg(dqb, hb, TN)
            dw_ref[3 * DA:4 * DA, :] += _dg(dab, hb, TN)
            dw_ref[4 * DA:5 * DA, :] += _dg(dgb, hb, TN)
            dh = (dh_kv + _dot(dqb, w_ref[0:DA, :]) + _dot(dab, w_ref[3 * DA:4 * DA, :])
                  + _dot(dgb, w_ref[4 * DA:5 * DA, :]))
            xv = x_ref[...]
            rstd = lax.rsqrt(jnp.mean(xv * xv, axis=-1, keepdims=True) + EPS)
            dxn, dsh, dsc, dgn = _norm_bwd(dh, xv, gain, mod_ref[1:2, :], rstd)
            dx_ref[...] = dx1_ref[...] + dxn
            s_ref[0:1, :] += dsh
            s_ref[1:2, :] += dsc
            s_ref[2:3, :] += dgn

    row = lambda w: pl.BlockSpec((tm, w), lambda i: (i, 0))
    lrow = lambda w: pl.BlockSpec((tm, w), lambda i: (jnp.minimum(i, nx - 1), 0))
    return pl.pallas_call(
        body, name="norm1_bwd", grid=(nt,),
        in_specs=[lrow(DA), row(DA), row(DA), lrow(DA), lrow(DA), _full((5 * DA, D)), lrow(D), _full((TC, D)), row(D),
                  lrow(D), _full((8, D)), _full((1, D))],
        out_specs=[lrow(D), _full((5 * DA, D)), _full((8, D))],
        out_shape=[jax.ShapeDtypeStruct((T, D), f32), jax.ShapeDtypeStruct((5 * DA, D), bf16), jax.ShapeDtypeStruct((8, D), f32)],
        scratch_shapes=[pltpu.VMEM((5 * DA, D), f32)],
        compiler_params=_cp("arbitrary"),
    )(dq, dk, dv, da, dg, w_inT, x0, ctx0, h, dx1, modv, g1)


def _adamw(w, g, m, v, name):
    r, c = w.shape
    tr = max(t for t in range(8, r + 1, 8) if r % t == 0 and t * c * 4 <= 2 * 1024 * 1024)

    def body(w_ref, g_ref, m_ref, v_ref, d_ref, nm_ref, nv_ref):
        gv = g_ref[...]
        nm = ADAM_B1 * m_ref[...] + (1.0 - ADAM_B1) * gv
        nv = ADAM_B2 * v_ref[...] + (1.0 - ADAM_B2) * (gv * gv)
        m_hat = nm * (1.0 / (1.0 - ADAM_B1 ** ADAM_STEP))
        v_hat = nv * (1.0 / (1.0 - ADAM_B2 ** ADAM_STEP))
        d_ref[...] = -ADAM_LR * (m_hat / (jnp.sqrt(v_hat) + ADAM_EPS) + ADAM_WD * w_ref[...])
        nm_ref[...] = nm
        nv_ref[...] = nv

    spec = pl.BlockSpec((tr, c), lambda i: (i, 0))
    return pl.pallas_call(
        body, name=name, grid=(r // tr,),
        in_specs=[spec] * 4, out_specs=[spec] * 3,
        out_shape=[jax.ShapeDtypeStruct((r, c), f32)] * 3,
        compiler_params=_cp("parallel"),
    )(w, g, m, v)


def _pad_rows128(vec):
    n = vec.shape[0]
    rows = -(-n // 1024) * 8
    return jnp.pad(vec, (0, rows * 128 - n)).reshape(rows, 128)


def _grad_rpb(dbias):
    lane_map, r_hot = _rpb_tables()
    return jnp.einsum("phl,lic,pir->hrc", _rpb_reduce(dbias), jnp.asarray(lane_map), jnp.asarray(r_hot),
                      precision=lax.Precision.HIGHEST)


def kernel(x, c, ctx, c_ctx, w_mod, b_mod, g_norm1, w_in, rpb, conv_w, conv_b, ln_g, ln_b, w_out, g_norm2, w_up, ffn_conv_w, ffn_conv_b, w_down, g_final, loss_target, m_c_ctx, m_w_mod, m_b_mod, m_g_norm1, m_w_in, m_rpb, m_conv_w, m_conv_b, m_ln_g, m_ln_b, m_w_out, m_g_norm2, m_w_up, m_ffn_conv_w, m_ffn_conv_b, m_w_down, m_g_final, v_c_ctx, v_w_mod, v_b_mod, v_g_norm1, v_w_in, v_rpb, v_conv_w, v_conv_b, v_ln_g, v_ln_b, v_w_out, v_g_norm2, v_w_up, v_ffn_conv_w, v_ffn_conv_b, v_w_down, v_g_final):
    me = 4 * lax.axis_index("x") + 2 * lax.axis_index("y") + lax.axis_index("c")
    nmod = w_mod.shape[2]
    n_in = w_in.shape[2]
    n_out = w_out.shape[1]
    n_up = w_up.shape[2]
    n_dn = w_down.shape[1]
    n_cw = conv_w.shape[2]

    w_inT, c_all = _allgather2([w_in[0].T.astype(bf16), c.reshape(8, 128)], "ag_w_in")

    c_all = c_all.reshape(NDEV, D)
    cvec = jnp.concatenate([c_all, c_ctx[None, :], jnp.zeros((7, D), f32)], axis=0)
    b_sh = lax.dynamic_slice(b_mod, (0, me * nmod), (1, nmod))
    mod_sh = _mod_fwd(cvec, w_mod[0], b_sh)
    n_modp = 16 * nmod
    payload = jnp.concatenate([mod_sh.reshape(-1), conv_w[0].reshape(-1), ffn_conv_w[0].reshape(-1)])
    flat = _small_allgather(_pad_rows128(payload), "ag_mod").reshape(NDEV, -1)
    mod_all = flat[:, :n_modp].reshape(NDEV, 16, nmod).transpose(1, 0, 2).reshape(16, 6 * D)
    mod_me = lax.dynamic_index_in_dim(mod_all, me, 0, keepdims=False).reshape(6, D)
    mod_c = mod_all[8]
    modv = jnp.concatenate([mod_me, mod_c[None, 0:D], mod_c[None, D:2 * D]], axis=0)
    o1 = n_modp + CW * n_cw
    conv_w_f = flat[:, n_modp:o1].reshape(NDEV, CW, n_cw).transpose(1, 0, 2).reshape(CW, DA)
    fcw_f = flat[:, o1:o1 + 3 * n_up].reshape(NDEV, 3, n_up).transpose(1, 0, 2).reshape(3, F2)

    x0, ctx0 = x[0], ctx[0]
    h, q, k, v, a, g, w_down_f = _in_proj(x0, ctx0, g_norm1, modv, w_inT, [w_down[0].astype(bf16)])
    bias_tab = _bias_table(rpb[0])
    y_cv, cv, w_out_f = _conf_fwd(a, g, conv_w_f, conv_b, ln_g, ln_b, [w_out[0].astype(bf16)])
    y_na, lse, w_upT = _attn_fwd(q, k, v, bias_tab, [w_up[0].T.astype(bf16)])
    x1, proj, h2 = _out_proj(x0, y_na, y_cv, w_out_f, modv, g_norm2)
    ffn, u_t, u2_t = _ffn_fwd(h2, w_upT, fcw_f, ffn_conv_b, w_down_f)
    dx2, dffn, s_loss = _loss_bwd(ffn, x1, loss_target[0], modv, g_final[None, :])

    dug, duv, dw_up, dw_down, dcwg, dcwv, dcbg, dcbv = _ffn_bwd(h2, dffn, u_t, u2_t, fcw_f, w_down_f)
    dx1, dy_na, dy_cv, dw_out, s_n2 = _norm2_bwd(dug, duv, w_upT, x1, dx2, proj, w_out_f, y_na, y_cv, modv, g_norm2)
    da, dg, dcw, s_cf, rv_down = _conf_bwd(a, g, cv, dy_cv, conv_w_f, ln_g, ln_b, [dw_down.reshape(NDEV, n_dn, D)])
    dq, dk, dv, dbias, rv_up, rv_out = _attn_bwd(q, k, v, y_na, dy_na, lse, bias_tab,
                                                 [dw_up.reshape(NDEV, n_up, D), dw_out.reshape(NDEV, n_out, D)])
    g_w_down = _sum_blocks(rv_down, "sum_w_down")
    g_w_out = _sum_blocks(rv_out, "sum_w_out")
    g_w_upT = _sum_blocks(rv_up, "sum_w_up")
    grad_rpb_part = _grad_rpb(dbias)
    grad_x, dw_inT, s_n1 = _norm1_bwd(dq, dk, dv, da, dg, w_inT, x0, ctx0, h, dx1, modv, g_norm1)
    grad_x = grad_x[None]
    dfcw = jnp.concatenate([dcwg, dcwv], axis=1)
    dfcb = jnp.concatenate([dcbg[0], dcbv[0]])
    small = jnp.concatenate([dcw[:CW].reshape(CW, NDEV, n_cw).transpose(1, 0, 2).reshape(NDEV, CW * n_cw),
                             dfcw.reshape(3, NDEV, n_up).transpose(1, 0, 2).reshape(NDEV, 3 * n_up)], axis=1)
    small = jnp.pad(small.reshape(NDEV, 4, D), ((0, 0), (0, 12), (0, 0))).astype(bf16)
    dmod = jnp.concatenate([s_n1[0], s_n1[1], s_n2[3], s_n2[0], s_n2[1], s_loss[1]])
    dmodc = jnp.concatenate([s_n1[3], s_n1[4]])
    parts = [dmodc, s_n1[2], grad_rpb_part.reshape(-1), s_cf[0], s_cf[1], s_cf[2], s_n2[2], dfcb, s_loss[0], s_loss[3, 0:1]]
    sizes = [p.shape[0] for p in parts]
    pvec = _pad_rows128(jnp.concatenate([dmod] + parts))
    r_a, gath = _reduce_scatter2(jnp.concatenate([dw_inT.reshape(NDEV, n_in, D), small], axis=1), [pvec], "rs_w_in")
    g_w_inT = r_a[:n_in]
    sm = r_a[n_in:n_in + 4].reshape(-1)
    g_conv_w = sm[:CW * n_cw].reshape(CW, n_cw)
    g_fcw = sm[CW * n_cw:].reshape(3, n_up)
    tot = _sum_rows8(gath, "sum_small").reshape(-1)
    dmod_all = gath.reshape(NDEV, -1)[:, :6 * D]
    offs = np.cumsum([6 * D] + sizes)
    pick = lambda j: tot[offs[j]:offs[j + 1]]
    dmodc_t = jnp.pad(pick(0), (0, 4 * D))
    g_b_mod = (tot[:6 * D] + dmodc_t)[None, :]
    g_g_norm1 = pick(1)[None, :]
    g_rpb = pick(2).reshape(1, NH, 2 * WR - 1, 2 * NCOL - 1)
    g_conv_b, g_ln_g, g_ln_b = pick(3)[None, :], pick(4)[None, :], pick(5)[None, :]
    g_g_norm2 = pick(6)[None, :]
    g_fcb = pick(7)[None, :]
    g_g_final = pick(8)
    loss = pick(9)[0]
    dm_rows = jnp.concatenate([dmod_all, dmodc_t[None, :], jnp.zeros((7, 6 * D), f32)], axis=0)
    dm_sh = lax.dynamic_slice(dm_rows, (0, me * nmod), (16, nmod))
    g_w_mod, gc_part = _mod_bwd(cvec, dm_sh, w_mod[0])
    gc_sum = _sum_rows8(_small_allgather(gc_part[0].reshape(8, 128), "ag_cctx"), "sum_cctx").reshape(D)
    sg_c = _sigmoid(c_ctx)
    g_c_ctx = gc_sum * (sg_c * (1.0 + c_ctx * (1.0 - sg_c)))

    big = [("w_mod", w_mod[0], g_w_mod, m_w_mod[0], v_w_mod[0]), ("w_in", w_in[0].T, g_w_inT, m_w_in[0].T, v_w_in[0].T),
           ("w_out", w_out[0], g_w_out, m_w_out[0], v_w_out[0]), ("w_up", w_up[0].T, g_w_upT, m_w_up[0].T, v_w_up[0].T),
           ("w_down", w_down[0], g_w_down, m_w_down[0], v_w_down[0])]
    upd = {n: _adamw(wv, gv, mv, vv, "adamw_" + n) for n, wv, gv, mv, vv in big}
    for n in ("w_in", "w_up"):
        upd[n] = tuple(arr.T for arr in upd[n])
    g_w_in, g_w_up = g_w_inT.T, g_w_upT.T
    smalls = [("c_ctx", c_ctx, g_c_ctx, m_c_ctx, v_c_ctx), ("b_mod", b_mod, g_b_mod, m_b_mod, v_b_mod),
              ("g_norm1", g_norm1, g_g_norm1, m_g_norm1, v_g_norm1), ("rpb", rpb, g_rpb, m_rpb, v_rpb),
              ("conv_w", conv_w, g_conv_w[None], m_conv_w, v_conv_w), ("conv_b", conv_b, g_conv_b, m_conv_b, v_conv_b),
              ("ln_g", ln_g, g_ln_g, m_ln_g, v_ln_g), ("ln_b", ln_b, g_ln_b, m_ln_b, v_ln_b),
              ("g_norm2", g_norm2, g_g_norm2, m_g_norm2, v_g_norm2),
              ("ffn_conv_w", ffn_conv_w, g_fcw[None], m_ffn_conv_w, v_ffn_conv_w),
              ("ffn_conv_b", ffn_conv_b, g_fcb, m_ffn_conv_b, v_ffn_conv_b), ("g_final", g_final, g_g_final, m_g_final, v_g_final)]
    packed = [_pad_rows128(jnp.concatenate([t[j].reshape(-1) for t in smalls])) for j in (1, 2, 3, 4)]
    sd, sm_, sv = _adamw(*packed, "adamw_small")
    so = np.cumsum([0] + [int(np.prod(t[1].shape)) for t in smalls])
    for j, t in enumerate(smalls):
        shp = t[1].shape
        upd[t[0]] = tuple(arr.reshape(-1)[so[j]:so[j + 1]].reshape(shp) for arr in (sd, sm_, sv))
    grads = {"c_ctx": g_c_ctx, "w_mod": g_w_mod[None], "b_mod": g_b_mod, "g_norm1": g_g_norm1, "w_in": g_w_in[None],
             "rpb": g_rpb, "conv_w": g_conv_w[None], "conv_b": g_conv_b, "ln_g": g_ln_g, "ln_b": g_ln_b,
             "w_out": g_w_out[None], "g_norm2": g_g_norm2, "w_up": g_w_up[None], "ffn_conv_w": g_fcw[None],
             "ffn_conv_b": g_fcb, "w_down": g_w_down[None], "g_final": g_g_final}
    names = ["c_ctx", "w_mod", "b_mod", "g_norm1", "w_in", "rpb", "conv_w", "conv_b", "ln_g", "ln_b", "w_out", "g_norm2",
             "w_up", "ffn_conv_w", "ffn_conv_b", "w_down", "g_final"]
    shapes = {n: grads[n].shape for n in names}
    outs = [loss, grad_x] + [grads[n] for n in names]
    for j in range(3):
        outs += [upd[n][j].reshape(shapes[n]) for n in names]
    return tuple(outs)
```

```python
import functools

import numpy as np
import jax
import jax.numpy as jnp
from jax import lax
from jax.experimental import pallas as pl
from jax.experimental.pallas import tpu as pltpu

f32 = jnp.float32
bf16 = jnp.bfloat16

D = 1024
T = 4096
TC = 256
TA = T + TC
DA = 512
NH = 8
HD = 64
GW = 64
WR = 8
NCOL = 16
F = 2816
F2 = 2 * F
CW = 31
NDEV = 8
EPS = 1e-6
SCALE = HD ** -0.5
NEG = -1e30
MESH = pl.DeviceIdType.MESH

NT = (((1,), (1,)), ((), ()))
TN = (((0,), (0,)), ((), ()))

ADAM_LR, ADAM_B1, ADAM_B2, ADAM_EPS, ADAM_WD, ADAM_STEP = 0.001, 0.9, 0.999, 1e-08, 0.01, 10

VMEM_LIMIT = 56 * 1024 * 1024


def _cp(*sem):
    return pltpu.CompilerParams(dimension_semantics=sem or None, vmem_limit_bytes=VMEM_LIMIT)


def _dot(a, b):
    return jnp.dot(a, b, preferred_element_type=f32)


def _dg(a, b, dims):
    return lax.dot_general(a, b, dims, preferred_element_type=f32)


def _sigmoid(x):
    return 1.0 / (1.0 + jnp.exp(-x))


def _full(shape):
    n = len(shape)
    return pl.BlockSpec(shape, lambda *_: (0,) * n)


def _resident(shape):
    n = len(shape)
    return pl.BlockSpec(shape, lambda *_: (0,) * n, pipeline_mode=pl.Buffered(1))


def _my_pos():
    return lax.axis_index("x"), lax.axis_index("y"), lax.axis_index("c")


def _small_gather_plan(v_ref, out_ref, send_sems, recv_sems):
    x, y, c = _my_pos()
    me = 4 * x + 2 * y + c
    peers = []
    for k in range(1, NDEV):
        kx, ky, kc = (k >> 2) & 1, (k >> 1) & 1, k & 1
        peers.append((x ^ kx, y ^ ky, c ^ kc))

    def copy(k, slot, to):
        return pltpu.make_async_remote_copy(
            src_ref=v_ref, dst_ref=out_ref.at[slot], send_sem=send_sems.at[k], recv_sem=recv_sems.at[k],
            device_id=to, device_id_type=MESH)

    def start():
        out_ref[me] = v_ref[...]
        for k, p in enumerate(peers):
            copy(k, me, p).start()

    def finish():
        for k, (px, py, pc) in enumerate(peers):
            copy(k, 4 * px + 2 * py + pc, (x, y, c)).wait_recv()
        for k, p in enumerate(peers):
            copy(k, me, p).wait_send()

    return start, finish


def _small_gather_scratch():
    return [pltpu.SemaphoreType.DMA((NDEV - 1,)), pltpu.SemaphoreType.DMA((NDEV - 1,))]


def _small_allgather(v, name):
    n = v.shape[0]

    def body(v_ref, out_ref, send_sems, recv_sems):
        start, finish = _small_gather_plan(v_ref, out_ref, send_sems, recv_sems)
        start()
        finish()

    return pl.pallas_call(
        body, name=name,
        out_shape=jax.ShapeDtypeStruct((NDEV, n, 128), f32),
        in_specs=[pl.BlockSpec(memory_space=pltpu.VMEM)],
        out_specs=pl.BlockSpec(memory_space=pltpu.VMEM),
        scratch_shapes=_small_gather_scratch(),
    )(v)


def _ag2_plan(x_refs, out_refs, send_sems, recv_sems, local_sems):
    na = len(x_refs)
    x, y, c = _my_pos()
    me, sibling = (x, y, c), (x, y, 1 - c)
    chips = [(1 - x, y), (x, 1 - y), (1 - x, 1 - y)]

    def rows(i, px, py, pc):
        m_per = x_refs[i].shape[0]
        return out_refs[i].at[pl.ds(pl.multiple_of((4 * px + 2 * py + pc) * m_per, 16 if m_per % 16 == 0 else 8), m_per), :]

    def copies(k, block, to, from_shard=False):
        return [pltpu.make_async_remote_copy(
            src_ref=x_refs[i] if from_shard else rows(i, *block), dst_ref=rows(i, *block),
            send_sem=send_sems.at[k * na + i], recv_sem=recv_sems.at[k * na + i], device_id=to, device_id_type=MESH)
            for i in range(na)]

    def mine():
        return [pltpu.make_async_copy(x_refs[i], rows(i, *me), local_sems.at[i]) for i in range(na)]

    def first():
        cps = copies(0, me, sibling, True)
        for j, chip in enumerate(chips):
            cps += copies(1 + j, me, (*chip, c), True)
        return cps

    def start():
        for cp in mine() + first():
            cp.start()

    def forward():
        for j, chip in enumerate(chips):
            for cp in copies(1 + j, (*chip, c), me):
                cp.wait_recv()
            for cp in copies(4 + j, (*chip, c), sibling):
                cp.start()

    def finish():
        for cp in copies(0, sibling, me):
            cp.wait_recv()
        for j, chip in enumerate(chips):
            for cp in copies(4 + j, (*chip, 1 - c), me):
                cp.wait_recv()
        for cp in first():
            cp.wait_send()
        for j, chip in enumerate(chips):
            for cp in copies(4 + j, (*chip, c), sibling):
                cp.wait_send()
        for cp in mine():
            cp.wait()

    return start, forward, finish


def _ag2_scratch(na):
    return [pltpu.SemaphoreType.DMA((7 * na,)), pltpu.SemaphoreType.DMA((7 * na,)), pltpu.SemaphoreType.DMA((na,))]


def _a2a_plan(g_ref, recv_ref, send_sems, recv_sems, local_sem):
    x, y, c = _my_pos()
    me = 4 * x + 2 * y + c
    peers = []
    for k in range(1, NDEV):
        kx, ky, kc = (k >> 2) & 1, (k >> 1) & 1, k & 1
        peers.append((x ^ kx, y ^ ky, c ^ kc))

    def sends():
        return [pltpu.make_async_remote_copy(
            src_ref=g_ref.at[4 * px + 2 * py + pc], dst_ref=recv_ref.at[me], send_sem=send_sems.at[k], recv_sem=recv_sems.at[k],
            device_id=(px, py, pc), device_id_type=MESH) for k, (px, py, pc) in enumerate(peers)]

    def own():
        return pltpu.make_async_copy(g_ref.at[me], recv_ref.at[me], local_sem)

    def start():
        own().start()
        for cp in sends():
            cp.start()

    def finish():
        for k, (px, py, pc) in enumerate(peers):
            pltpu.make_async_remote_copy(
                src_ref=g_ref.at[me], dst_ref=recv_ref.at[4 * px + 2 * py + pc], send_sem=send_sems.at[k],
                recv_sem=recv_sems.at[k], device_id=(x, y, c), device_id_type=MESH).wait_recv()
        for cp in sends():
            cp.wait_send()
        own().wait()

    return start, finish


def _a2a_scratch():
    return [pltpu.SemaphoreType.DMA((NDEV - 1,)), pltpu.SemaphoreType.DMA((NDEV - 1,)), pltpu.SemaphoreType.DMA]


def _allgather2(shards, name):
    na = len(shards)

    def body(*refs):
        start, forward, finish = _ag2_plan(refs[:na], refs[na:2 * na], *refs[2 * na:])
        start()
        forward()
        finish()

    return pl.pallas_call(
        body, name=name,
        out_shape=[jax.ShapeDtypeStruct((NDEV * s.shape[0], s.shape[1]), s.dtype) for s in shards],
        in_specs=[pl.BlockSpec(memory_space=pltpu.VMEM)] * na,
        out_specs=[pl.BlockSpec(memory_space=pltpu.VMEM)] * na,
        scratch_shapes=_ag2_scratch(na),
        compiler_params=pltpu.CompilerParams(vmem_limit_bytes=VMEM_LIMIT),
    )(*shards)


def _reduce_scatter2(g, small, name):
    _, r, n = g.shape
    ch = 16
    nch = r // ch
    ns = len(small)

    def body(g_ref, *rest):
        v_refs, out_ref, vout_refs = rest[:ns], rest[ns], rest[ns + 1:2 * ns + 1]
        a_ref, h_ref, b_ref, s1_send, s1_recv, s2_send, s2_recv = rest[2 * ns + 1:2 * ns + 8]
        gather_sems = rest[2 * ns + 8:]
        gathers = [_small_gather_plan(v_refs[i], vout_refs[i], *gather_sems[2 * i:2 * i + 2]) for i in range(ns)]
        for start, _ in gathers:
            start()
        x, y, c = _my_pos()
        sibling = (x, y, 1 - c)
        s1 = []
        for j in range(4):
            cp = pltpu.make_async_remote_copy(
                src_ref=g_ref.at[2 * j + (1 - c)], dst_ref=a_ref.at[j], send_sem=s1_send.at[j], recv_sem=s1_recv.at[j],
                device_id=sibling, device_id_type=MESH)
            cp.start()
            s1.append(cp)
        for cp in s1:
            cp.wait_recv()

        def add1(i, _):
            rr = pl.ds(pl.multiple_of(i * ch, ch), ch)
            for j in range(4):
                h_ref[j, rr, :] = (g_ref[2 * j + c, rr, :].astype(f32) + a_ref[j, rr, :].astype(f32)).astype(bf16)
            return 0
        lax.fori_loop(0, nch, add1, 0)
        mychip = 2 * x + y
        s2 = []
        for m in range(1, 4):
            mx, my_ = (m >> 1) & 1, m & 1
            px, py = x ^ mx, y ^ my_
            cp = pltpu.make_async_remote_copy(
                src_ref=h_ref.at[2 * px + py], dst_ref=b_ref.at[m - 1], send_sem=s2_send.at[m - 1], recv_sem=s2_recv.at[m - 1],
                device_id=(px, py, c), device_id_type=MESH)
            cp.start()
            s2.append(cp)
        for cp in s2:
            cp.wait_recv()

        def add2(i, _):
            rr = pl.ds(pl.multiple_of(i * ch, ch), ch)
            acc = h_ref[mychip, rr, :].astype(f32)
            for m in range(3):
                acc = acc + b_ref[m, rr, :].astype(f32)
            out_ref[rr, :] = acc
            return 0
        lax.fori_loop(0, nch, add2, 0)
        for cp in s1 + s2:
            cp.wait_send()
        for _, finish in gathers:
            finish()

    vmem = pl.BlockSpec(memory_space=pltpu.VMEM)
    return pl.pallas_call(
        body, name=name,
        out_shape=[jax.ShapeDtypeStruct((r, n), f32)] + [jax.ShapeDtypeStruct((NDEV,) + v.shape, v.dtype) for v in small],
        in_specs=[vmem] * (1 + ns),
        out_specs=[vmem] * (1 + ns),
        scratch_shapes=[pltpu.VMEM((4, r, n), bf16), pltpu.VMEM((4, r, n), bf16), pltpu.VMEM((3, r, n), bf16),
                        pltpu.SemaphoreType.DMA((4,)), pltpu.SemaphoreType.DMA((4,)),
                        pltpu.SemaphoreType.DMA((3,)), pltpu.SemaphoreType.DMA((3,))] + _small_gather_scratch() * ns,
        compiler_params=pltpu.CompilerParams(vmem_limit_bytes=VMEM_LIMIT),
    )(g, *small)


def _mod_fwd(cvec, w_sh, b_sh):
    def body(c_ref, w_ref, b_ref, o_ref):
        cv = c_ref[...]
        act = (cv * _sigmoid(cv)).astype(bf16)
        o_ref[...] = _dot(act, w_ref[...].astype(bf16)) + b_ref[...]
    return pl.pallas_call(body, name="mod_fwd", out_shape=jax.ShapeDtypeStruct((16, w_sh.shape[1]), f32))(cvec, w_sh, b_sh)


def _mod_bwd(cvec, dm_sh, w_sh):
    def body(c_ref, dm_ref, w_ref, gw_ref, gc_ref):
        cv = c_ref[...]
        act = (cv * _sigmoid(cv)).astype(bf16)
        gw_ref[...] = _dg(act, dm_ref[...].astype(bf16), TN)
        gc_ref[...] = _dg(dm_ref[8:16, :].astype(bf16), w_ref[...].astype(bf16), NT)
    return pl.pallas_call(
        body, name="mod_bwd",
        out_shape=(jax.ShapeDtypeStruct(w_sh.shape, f32), jax.ShapeDtypeStruct((8, D), f32)))(cvec, dm_sh, w_sh)


def _sum_rows8(a, name):
    n = a.shape[1]

    def body(a_ref, o_ref):
        acc = a_ref[0]
        for d in range(1, NDEV):
            acc = acc + a_ref[d]
        o_ref[...] = acc
    return pl.pallas_call(body, name=name, out_shape=jax.ShapeDtypeStruct((n, 128), f32))(a)


def _in_proj(x0, ctx0, g1, modv, w_inT, shards):
    tm = 256
    nt = TA // tm
    nx = T // tm
    na = len(shards)

    def body(x_ref, c_ref, g_ref, mod_ref, w_ref, *rest):
        x_refs, (h_ref, q_ref, k_ref, v_ref, a_ref, gg_ref) = rest[:na], rest[na:na + 6]
        out_refs, sems = rest[na + 6:2 * na + 6], rest[2 * na + 6:]
        i = pl.program_id(0)
        if na:
            start, forward, finish = _ag2_plan(x_refs, out_refs, *sems)
            pl.when(i == 0)(start)
            pl.when(i == nt - 2)(forward)
        is_ctx = i == nt - 1
        xv = jnp.where(is_ctx, c_ref[...], x_ref[...])
        rstd = lax.rsqrt(jnp.mean(xv * xv, axis=-1, keepdims=True) + EPS)
        sh = jnp.where(is_ctx, mod_ref[6:7, :], mod_ref[0:1, :])
        sc = jnp.where(is_ctx, mod_ref[7:8, :], mod_ref[1:2, :])
        h = ((xv * rstd * g_ref[...]) * (1.0 + sc) + sh).astype(bf16)
        h_ref[...] = h
        for j, o_ref in enumerate((q_ref, k_ref, v_ref, a_ref, gg_ref)):
            o_ref[...] = _dg(h, w_ref[j * DA:(j + 1) * DA, :], NT).astype(o_ref.dtype)
        if na:
            pl.when(is_ctx)(finish)

    row = lambda w: pl.BlockSpec((tm, w), lambda i: (i, 0))
    hbm = pl.BlockSpec(memory_space=pl.ANY)
    return pl.pallas_call(
        body, name="in_proj", grid=(nt,),
        in_specs=[pl.BlockSpec((tm, D), lambda i: (jnp.minimum(i, nx - 1), 0)), _full((TC, D)),
                  _full((1, D)), _full((8, D)), _full((5 * DA, D))] + [hbm] * na,
        out_specs=[row(D), row(DA), row(DA), row(DA), row(DA), row(DA)] + [hbm] * na,
        out_shape=[jax.ShapeDtypeStruct((TA, D), bf16)] + [jax.ShapeDtypeStruct((TA, DA), bf16)] * 3
                  + [jax.ShapeDtypeStruct((TA, DA), f32)] * 2
                  + [jax.ShapeDtypeStruct((NDEV * sh.shape[0], sh.shape[1]), sh.dtype) for sh in shards],
        scratch_shapes=_ag2_scratch(na) if na else [],
        compiler_params=_cp("arbitrary"),
    )(x0, ctx0, g1, modv, w_inT, *shards)


def _win_start(r):
    return jnp.clip(r - WR // 2, 0, GW - WR)


def _pattern(r):
    return _win_start(r) - r + (WR - 1)


def _bias_table(rpb):
    qc = np.arange(GW)[:, None]
    kc = np.arange(GW)[None, :]
    cs = np.clip(qc - NCOL // 2, 0, GW - NCOL)
    valid = np.tile(((kc >= cs) & (kc < cs + NCOL)).astype(np.int32), (1, WR))
    pad = jnp.pad(rpb, ((0, 0), (0, 0), (0, GW - (2 * NCOL - 1))))
    base = jnp.stack([pad[:, p:p + WR, :].reshape(NH, WR * GW) for p in range(8)])

    def body(base_ref, valid_ref, o_ref):
        ok = valid_ref[...] != 0
        for h in range(NH):
            row = jnp.broadcast_to(base_ref[0, h:h + 1, :], (GW, WR * GW))
            skew = pltpu.roll(row, WR * GW - (NCOL - 1), 1, stride=1, stride_axis=0)
            o_ref[0, h] = jnp.where(ok, skew, NEG)

    return pl.pallas_call(
        body, name="bias_table", grid=(8,),
        in_specs=[pl.BlockSpec((1, NH, WR * GW), lambda p: (p, 0, 0)), _full((GW, WR * GW))],
        out_specs=pl.BlockSpec((1, NH, GW, WR * GW), lambda p: (p, 0, 0, 0)),
        out_shape=jax.ShapeDtypeStruct((8, NH, GW, WR * GW), f32),
        compiler_params=_cp("parallel"),
    )(base, jnp.asarray(valid))


def _rpb_tables():
    lane_map = np.zeros((WR * GW, WR, 2 * NCOL - 1), np.float32)
    for i in range(WR):
        for t in range(GW):
            if t >= GW - NCOL:
                lane_map[i * GW + t, i, t - (GW - NCOL)] = 1.0
            elif t < NCOL - 1:
                lane_map[i * GW + t, (i - 1) % WR, t + NCOL] = 1.0
    p = np.arange(8)[:, None]
    i = np.arange(WR)[None, :]
    r_hot = ((p + i)[:, :, None] == np.arange(2 * WR - 1)[None, None, :]).astype(np.float32)
    return lane_map, r_hot


AG_FORWARD_ROW = 58


def _stack_pair(x2, lo):
    z = jnp.zeros_like(x2)
    return jnp.concatenate([jnp.where(lo, x2, z), jnp.where(lo, z, x2)], axis=0)


def _attn_fwd(q, k, v, bias_tab, shards):
    na = len(shards)

    def body(q_ref, k_ref, v_ref, b_ref, *rest):
        x_refs, (y_ref, lse_ref), out_refs, sems = rest[:na], rest[na:na + 2], rest[na + 2:2 * na + 2], rest[2 * na + 2:]
        r = pl.program_id(0)
        if na:
            start, forward, finish = _ag2_plan(x_refs, out_refs, *sems)
            pl.when(r == 0)(start)
            pl.when(r == AG_FORWARD_ROW)(forward)
        ks = pl.multiple_of(_win_start(r) * GW, GW)
        qq = q_ref[...]
        lo = lax.broadcasted_iota(jnp.int32, (GW, 2 * HD), 1) < HD
        kv, scores = [], []
        for pr in range(NH // 2):
            ps = slice(pr * 2 * HD, (pr + 1) * 2 * HD)
            qst = _stack_pair(qq[:, ps], lo)
            kw, kc = k_ref[pl.ds(ks, WR * GW), ps], k_ref[T:TA, ps]
            kv.append((v_ref[pl.ds(ks, WR * GW), ps], v_ref[T:TA, ps]))
            bias2 = b_ref[0, 2 * pr:2 * pr + 2].reshape(2 * GW, WR * GW)
            scores.append((_dg(qst, kw, NT) * SCALE + bias2, _dg(qst, kc, NT) * SCALE))
        probs = []
        for pr, (sl, sc) in enumerate(scores):
            m = jnp.maximum(jnp.max(sl, axis=-1, keepdims=True), jnp.max(sc, axis=-1, keepdims=True))
            pl_ = jnp.exp(sl - m)
            pc = jnp.exp(sc - m)
            l = jnp.sum(pl_, axis=-1, keepdims=True) + jnp.sum(pc, axis=-1, keepdims=True)
            lse = m + jnp.log(l)
            lse_ref[:, 2 * pr:2 * pr + 1] = lse[0:GW]
            lse_ref[:, 2 * pr + 1:2 * pr + 2] = lse[GW:]
            probs.append((pl_.astype(bf16), pc.astype(bf16), 1.0 / l))
        for pr in range(NH // 2):
            ps = slice(pr * 2 * HD, (pr + 1) * 2 * HD)
            vw, vc = kv[pr]
            pb, cb, rl = probs[pr]
            o = (_dot(pb, vw) + _dot(cb, vc)) * rl
            y_ref[:, ps] = jnp.where(lo, o[0:GW], o[GW:]).astype(bf16)
        if na:
            pl.when(r == GW - 1)(finish)

    hbm = pl.BlockSpec(memory_space=pl.ANY)
    return pl.pallas_call(
        body, name="attn_fwd", grid=(GW,),
        in_specs=[pl.BlockSpec((GW, DA), lambda r: (r, 0)), _full((TA, DA)), _full((TA, DA)),
                  pl.BlockSpec((1, NH, GW, WR * GW), lambda r: (_pattern(r), 0, 0, 0))] + [hbm] * na,
        out_specs=[pl.BlockSpec((GW, DA), lambda r: (r, 0)), pl.BlockSpec((GW, NH), lambda r: (r, 0))] + [hbm] * na,
        out_shape=[jax.ShapeDtypeStruct((T, DA), bf16), jax.ShapeDtypeStruct((T, NH), f32)]
                  + [jax.ShapeDtypeStruct((NDEV * s.shape[0], s.shape[1]), s.dtype) for s in shards],
        scratch_shapes=_ag2_scratch(na) if na else [],
        compiler_params=_cp("arbitrary"),
    )(q, k, v, bias_tab, *shards)


CONV_TT = 512
HALO = 16


def _halo_specs(tt, w, nrows_blocks):
    per = tt // HALO
    prev = pl.BlockSpec((HALO, w), lambda i: (jnp.maximum(i * per - 1, 0), 0))
    cur = pl.BlockSpec((tt, w), lambda i: (i, 0))
    nxt = pl.BlockSpec((HALO, w), lambda i: (jnp.minimum((i + 1) * per, nrows_blocks - 1), 0))
    return [prev, cur, nxt]


def _shifted_copies(rot, wn):
    for b in range(1, 8):
        rot[b, 0:wn - 8, :] = rot[0, pl.ds(b, wn - 8), :]


def _conf_fwd(a, g, conv_w, conv_b, ln_g, ln_b, shards):
    tt = CONV_TT
    nt = T // tt
    sub = 32
    wn = tt + 2 * HALO
    na = len(shards)

    def body(ap, ac, an, gp, gc, gn, w_ref, b_ref, lg_ref, lb_ref, *rest):
        x_refs, (y_ref, cv_ref), out_refs, (rot, *sems) = rest[:na], rest[na:na + 2], rest[na + 2:2 * na + 2], rest[2 * na + 2:]
        i = pl.program_id(0)
        if na:
            start, forward, finish = _ag2_plan(x_refs, out_refs, *sems)
            pl.when(i == 0)(start)
            pl.when(i == nt // 2)(forward)
        rot[0, 0:HALO, :] = jnp.where(i > 0, ap[...] * _sigmoid(gp[...]), 0.0)
        rot[0, HALO:HALO + tt, :] = ac[...] * _sigmoid(gc[...])
        rot[0, HALO + tt:, :] = jnp.where(i < nt - 1, an[...] * _sigmoid(gn[...]), 0.0)
        _shifted_copies(rot, wn)
        w = w_ref[...]
        for s in range(tt // sub):
            acc = jnp.zeros((sub, DA), f32)
            for j in range(CW):
                a8, b8 = divmod(1 + j, 8)
                acc = acc + rot[b8, pl.ds(s * sub + 8 * a8, sub), :] * w[j:j + 1, :]
            cv = acc + b_ref[...]
            cv_ref[pl.ds(s * sub, sub), :] = cv
            mu = jnp.mean(cv, axis=-1, keepdims=True)
            xc = cv - mu
            rstd = lax.rsqrt(jnp.mean(xc * xc, axis=-1, keepdims=True) + EPS)
            z = xc * rstd * lg_ref[...] + lb_ref[...]
            y_ref[pl.ds(s * sub, sub), :] = (z * _sigmoid(z)).astype(bf16)
        if na:
            pl.when(i == nt - 1)(finish)

    hs = _halo_specs(tt, DA, T // HALO)
    hbm = pl.BlockSpec(memory_space=pl.ANY)
    return pl.pallas_call(
        body, name="conf_fwd", grid=(nt,),
        in_specs=hs + hs + [_full((CW, DA)), _full((1, DA)), _full((1, DA)), _full((1, DA))] + [hbm] * na,
        out_specs=[pl.BlockSpec((tt, DA), lambda i: (i, 0)), pl.BlockSpec((tt, DA), lambda i: (i, 0))] + [hbm] * na,
        out_shape=[jax.ShapeDtypeStruct((T, DA), bf16), jax.ShapeDtypeStruct((T, DA), f32)]
                  + [jax.ShapeDtypeStruct((NDEV * s.shape[0], s.shape[1]), s.dtype) for s in shards],
        scratch_shapes=[pltpu.VMEM((8, wn, DA), f32)] + (_ag2_scratch(na) if na else []),
        compiler_params=_cp("arbitrary"),
    )(a, a, a, g, g, g, conv_w, conv_b, ln_g, ln_b, *shards)


def _out_proj(xa, y_na, y_cv, w_out, modv, g2):
    tm = 512

    def body(x_ref, ya_ref, yc_ref, w_ref, mod_ref, g_ref, x1_ref, pj_ref, h2_ref):
        proj = _dot(ya_ref[...], w_ref[0:DA, :]) + _dot(yc_ref[...], w_ref[DA:D, :])
        x1 = x_ref[...] + mod_ref[2:3, :] * proj
        x1_ref[...] = x1
        pj_ref[...] = proj.astype(bf16)
        rstd = lax.rsqrt(jnp.mean(x1 * x1, axis=-1, keepdims=True) + EPS)
        h2_ref[...] = ((x1 * rstd * g_ref[...]) * (1.0 + mod_ref[4:5, :]) + mod_ref[3:4, :]).astype(bf16)

    row = lambda w: pl.BlockSpec((tm, w), lambda i: (i, 0))
    return pl.pallas_call(
        body, name="out_proj", grid=(T // tm,),
        in_specs=[row(D), row(DA), row(DA), _full((D, D)), _full((8, D)), _full((1, D))],
        out_specs=[row(D), row(D), row(D)],
        out_shape=[jax.ShapeDtypeStruct((T, D), f32), jax.ShapeDtypeStruct((T, D), bf16), jax.ShapeDtypeStruct((T, D), bf16)],
        compiler_params=_cp("parallel"),
    )(xa, y_na, y_cv, w_out, modv, g2)


FFN_TT = 2048
FFN_CT = 256
FFN_NC = F // FFN_CT
FFN_SUB = 32


def _row_neighbours(ref, r, n):
    blk = ref[pl.ds(r - 8, n + 16), :]
    return blk[8:8 + n, :], pltpu.roll(blk, 1, 0)[8:8 + n, :], pltpu.roll(blk, n + 15, 0)[8:8 + n, :]


def _ffn_specs(tt, ct, by_token_first):
    tc = (lambda f: (lambda t, c: f(t, c))) if by_token_first else (lambda f: (lambda c, t: f(t, c)))
    per = tt // HALO
    halo = [pl.BlockSpec((HALO, D), tc(lambda t, c: (jnp.maximum(t * per - 1, 0), 0))),
            pl.BlockSpec((tt, D), tc(lambda t, c: (t, 0))),
            pl.BlockSpec((HALO, D), tc(lambda t, c: (jnp.minimum((t + 1) * per, T // HALO - 1), 0)))]
    weights = [pl.BlockSpec((ct, D), tc(lambda t, c: (c, 0))), pl.BlockSpec((ct, D), tc(lambda t, c: (c + FFN_NC, 0))),
               pl.BlockSpec((3, ct), tc(lambda t, c: (0, c))), pl.BlockSpec((3, ct), tc(lambda t, c: (0, c + FFN_NC))),
               pl.BlockSpec((1, ct), tc(lambda t, c: (0, c))), pl.BlockSpec((1, ct), tc(lambda t, c: (0, c + FFN_NC))),
               pl.BlockSpec((ct, D), tc(lambda t, c: (c, 0)))]
    return halo, weights


def _ffn_fwd(h2, w_upT, fcw, fcb, w_down):
    tt, ct = FFN_TT, FFN_CT
    nt = T // tt
    wn = tt + 2 * HALO
    half = tt // 2

    def body(hp, hc, hn, wg_ref, wv_ref, cwg_ref, cwv_ref, cbg_ref, cbv_ref, wd_ref, o_ref, u_ref, u2_ref, hwin, uwin, act):
        t = pl.program_id(0)
        c = pl.program_id(1)

        @pl.when(c == 0)
        def _():
            hwin[0:HALO, :] = jnp.where(t > 0, hp[...], jnp.zeros_like(hp[...]))
            hwin[HALO:HALO + tt, :] = hc[...]
            hwin[HALO + tt:, :] = jnp.where(t < nt - 1, hn[...], jnp.zeros_like(hn[...]))
            o_ref[...] = jnp.zeros_like(o_ref)

        for r0, r1 in ((0, half + 2 * HALO), (half + 2 * HALO, wn)):
            hw = hwin[r0:r1, :]
            uwin[r0:r1, :ct] = _dg(hw, wg_ref[...], NT)
            uwin[r0:r1, ct:] = _dg(hw, wv_ref[...], NT)
        cw = jnp.concatenate([cwg_ref[...], cwv_ref[...]], axis=1)
        cb = jnp.concatenate([cbg_ref[...], cbv_ref[...]], axis=1)
        for p in range(2):
            for r in range(p * half, (p + 1) * half, FFN_SUB):
                uc, prev, nxt = _row_neighbours(uwin, HALO + r, FFN_SUB)
                u2 = prev * cw[0:1, :] + uc * cw[1:2, :] + nxt * cw[2:3, :] + cb
                u_ref[r:r + FFN_SUB, :] = uc.astype(bf16)
                u2_ref[r:r + FFN_SUB, :] = u2
                gate = u2[:, :ct]
                act[r:r + FFN_SUB, :] = (gate * _sigmoid(gate) * u2[:, ct:]).astype(bf16)
            rows = slice(p * half, (p + 1) * half)
            o_ref[rows, :] += _dot(act[rows, :], wd_ref[...])

    halo, weights = _ffn_specs(tt, ct, True)
    pair = pl.BlockSpec((tt, 2 * ct), lambda t, c: (t, c))
    return pl.pallas_call(
        body, name="ffn_fwd", grid=(nt, FFN_NC),
        in_specs=halo + weights,
        out_specs=[pl.BlockSpec((tt, D), lambda t, c: (t, 0)), pair, pair],
        out_shape=[jax.ShapeDtypeStruct((T, D), f32), jax.ShapeDtypeStruct((T, F2), bf16), jax.ShapeDtypeStruct((T, F2), f32)],
        scratch_shapes=[pltpu.VMEM((wn, D), bf16), pltpu.VMEM((wn, 2 * ct), f32), pltpu.VMEM((tt, ct), bf16)],
        compiler_params=_cp("parallel", "arbitrary"),
    )(h2, h2, h2, w_upT, w_upT, fcw, fcw, fcb, fcb, w_down)


def _loss_bwd(ffn, x1, tgt, modv, gf):
    tm = 1024
    nt = T // tm

    def body(f_ref, x1_ref, t_ref, mod_ref, g_ref, dx2_ref, df_ref, s_ref):
        i = pl.program_id(0)

        @pl.when(i == 0)
        def _():
            s_ref[...] = jnp.zeros_like(s_ref)

        ff = f_ref[...]
        gt2 = mod_ref[5:6, :]
        x2 = x1_ref[...] + gt2 * ff
        rstd = lax.rsqrt(jnp.mean(x2 * x2, axis=-1, keepdims=True) + EPS)
        xh = x2 * rstd
        gfv = g_ref[...]
        e = xh * gfv - t_ref[...]
        dy = e * (1.0 / D)
        dxh = dy * gfv
        dx2 = rstd * (dxh - xh * jnp.mean(dxh * xh, axis=-1, keepdims=True))
        dx2_ref[...] = dx2
        df_ref[...] = (dx2 * gt2).astype(bf16)
        s_ref[0:1, :] += jnp.sum(dy * xh, axis=0, keepdims=True)
        s_ref[1:2, :] += jnp.sum(dx2 * ff, axis=0, keepdims=True)
        s_ref[2:3, :] += jnp.sum(e * e, axis=0, keepdims=True)

        @pl.when(i == nt - 1)
        def _():
            tot = jnp.sum(s_ref[2:3, :], axis=-1, keepdims=True) * (0.5 / D)
            s_ref[3:4, :] = jnp.broadcast_to(tot, (1, D))

    row = lambda: pl.BlockSpec((tm, D), lambda i: (i, 0))
    return pl.pallas_call(
        body, name="loss_bwd", grid=(nt,),
        in_specs=[row(), row(), row(), _full((8, D)), _full((1, D))],
        out_specs=[row(), row(), _full((8, D))],
        out_shape=[jax.ShapeDtypeStruct((T, D), f32), jax.ShapeDtypeStruct((T, D), bf16), jax.ShapeDtypeStruct((8, D), f32)],
        compiler_params=_cp("arbitrary"),
    )(ffn, x1, tgt, modv, gf)


def _ffn_bwd(h2, dffn, u_t, u2_t, fcw, w_down):
    tt, ct = FFN_TT, FFN_CT
    nt = T // tt
    wn = tt + 2 * HALO
    half = tt // 2

    def body(dp, dc, dn, hc, uc_ref, u2p, u2c, u2n, cwg_ref, cwv_ref, wd_ref,
             dug_ref, duv_ref, dwu_ref, dwd_ref, dcwg_ref, dcwv_ref, dcbg_ref, dcbv_ref,
             dwin, d2win, dawin, accu, accd, act, du):
        t = pl.program_id(1)
        first, last = t == 0, t == nt - 1
        zero = jnp.zeros((HALO, D), bf16)
        dwin[0:HALO, :] = jnp.where(first, zero, dp[...])
        dwin[HALO:HALO + tt, :] = dc[...]
        dwin[HALO + tt:, :] = jnp.where(last, zero, dn[...])

        @pl.when(first)
        def _():
            for r in (accu, accd, dcwg_ref, dcwv_ref, dcbg_ref, dcbv_ref):
                r[...] = jnp.zeros_like(r)

        cw = jnp.concatenate([cwg_ref[...], cwv_ref[...]], axis=1)
        split = half + 2 * HALO
        for r0, r1 in ((0, split), (split, wn)):
            dawin[r0:r1, :] = _dg(dwin[r0:r1, :], wd_ref[...], NT)

        def grads(u2v, dact):
            gate, val = u2v[:, :ct], u2v[:, ct:]
            sg = _sigmoid(gate)
            silu = gate * sg
            return dact * val * (sg * (1.0 + gate * (1.0 - sg))), dact * silu, silu * val

        for blk, r0 in ((u2p, 0), (u2n, HALO + tt)):
            dgate, dval, _ = grads(blk[...], dawin[r0:r0 + HALO, :])
            d2win[r0:r0 + HALO, :ct] = dgate
            d2win[r0:r0 + HALO, ct:] = dval
        for p in range(2):
            rows = slice(p * half, (p + 1) * half)
            for r in range(p * half, (p + 1) * half, FFN_SUB):
                dgate, dval, av = grads(u2c[r:r + FFN_SUB, :], dawin[HALO + r:HALO + r + FFN_SUB, :])
                d2win[HALO + r:HALO + r + FFN_SUB, :ct] = dgate
                d2win[HALO + r:HALO + r + FFN_SUB, ct:] = dval
                act[r:r + FFN_SUB, :] = av.astype(bf16)
            accd[...] += _dg(act[rows, :], dc[rows, :], TN)

        def fold8(x):
            out = x[0:8]
            for k in range(8, FFN_SUB, 8):
                out = out + x[k:k + 8]
            return out

        for p in range(2):
            rows = slice(p * half, (p + 1) * half)
            sums = [jnp.zeros((8, 2 * ct), f32) for _ in range(4)]
            for r in range(p * half, (p + 1) * half, FFN_SUB):
                d2c, d2m, d2p = _row_neighbours(d2win, HALO + r, FFN_SUB)
                ucur = uc_ref[r:r + FFN_SUB, :].astype(f32)
                sums[0] = sums[0] + fold8(d2c)
                for kk, dd in enumerate((d2p, d2c, d2m)):
                    sums[1 + kk] = sums[1 + kk] + fold8(ucur * dd)
                du[r:r + FFN_SUB, :] = (d2p * cw[0:1, :] + d2c * cw[1:2, :] + d2m * cw[2:3, :]).astype(bf16)
            dcb = jnp.sum(sums[0], axis=0, keepdims=True)
            dcbg_ref[...] += dcb[:, :ct]
            dcbv_ref[...] += dcb[:, ct:]
            for kk in range(3):
                dck = jnp.sum(sums[1 + kk], axis=0, keepdims=True)
                dcwg_ref[kk:kk + 1, :] += dck[:, :ct]
                dcwv_ref[kk:kk + 1, :] += dck[:, ct:]
            dug_ref[rows, :] = du[rows, :ct]
            duv_ref[rows, :] = du[rows, ct:]
            accu[...] += _dg(du[rows, :], hc[rows, :], TN)

        @pl.when(last)
        def _():
            dwu_ref[0] = accu[0:ct, :].astype(bf16)
            dwu_ref[1] = accu[ct:, :].astype(bf16)
            dwd_ref[...] = accd[...].astype(bf16)

    per = tt // HALO
    prev = lambda w: (lambda c, t: (jnp.maximum(t * per - 1, 0), c if w else 0))
    nxt = lambda w: (lambda c, t: (jnp.minimum((t + 1) * per, T // HALO - 1), c if w else 0))
    tile = lambda: pl.BlockSpec((ct, D), lambda c, t: (c, 0))
    lane = lambda r, off: pl.BlockSpec((r, ct), lambda c, t: (0, c + off))
    return pl.pallas_call(
        body, name="ffn_bwd", grid=(FFN_NC, nt),
        in_specs=[pl.BlockSpec((HALO, D), prev(False)), pl.BlockSpec((tt, D), lambda c, t: (t, 0)), pl.BlockSpec((HALO, D), nxt(False)),
                  pl.BlockSpec((tt, D), lambda c, t: (t, 0)), pl.BlockSpec((tt, 2 * ct), lambda c, t: (t, c)),
                  pl.BlockSpec((HALO, 2 * ct), prev(True)), pl.BlockSpec((tt, 2 * ct), lambda c, t: (t, c)),
                  pl.BlockSpec((HALO, 2 * ct), nxt(True)), lane(3, 0), lane(3, FFN_NC), tile()],
        out_specs=[pl.BlockSpec((tt, ct), lambda c, t: (t, c)), pl.BlockSpec((tt, ct), lambda c, t: (t, c)),
                   pl.BlockSpec((2, ct, D), lambda c, t: (0, c, 0)), tile(), lane(3, 0), lane(3, 0), lane(1, 0), lane(1, 0)],
        out_shape=[jax.ShapeDtypeStruct((T, F), bf16), jax.ShapeDtypeStruct((T, F), bf16),
                   jax.ShapeDtypeStruct((2, F, D), bf16), jax.ShapeDtypeStruct((F, D), bf16),
                   jax.ShapeDtypeStruct((3, F), f32), jax.ShapeDtypeStruct((3, F), f32),
                   jax.ShapeDtypeStruct((1, F), f32), jax.ShapeDtypeStruct((1, F), f32)],
        scratch_shapes=[pltpu.VMEM((wn, D), bf16), pltpu.VMEM((wn, 2 * ct), f32), pltpu.VMEM((wn, ct), f32),
                        pltpu.VMEM((2 * ct, D), f32), pltpu.VMEM((ct, D), f32),
                        pltpu.VMEM((tt, ct), bf16), pltpu.VMEM((tt, 2 * ct), bf16)],
        compiler_params=_cp("parallel", "arbitrary"),
    )(dffn, dffn, dffn, h2, u_t, u2_t, u2_t, u2_t, fcw, fcw, w_down)


def _norm_bwd(dh, xv, gain, sh_sc, rstd):
    xh = xv * rstd
    n = xh * gain
    dn = dh * (1.0 + sh_sc)
    dxh = dn * gain
    dx = rstd * (dxh - xh * jnp.mean(dxh * xh, axis=-1, keepdims=True))
    return (dx, jnp.sum(dh, axis=0, keepdims=True), jnp.sum(dh * n, axis=0, keepdims=True),
            jnp.sum(dn * xh, axis=0, keepdims=True))


def _norm2_bwd(dug, duv, w_upT, x1, dx2, proj, w_out, y_na, y_cv, modv, g2):
    tm = 512
    nt = T // tm

    def body(dug_ref, duv_ref, w_ref, x1_ref, dx2_ref, pj_ref, wo_ref, ya_ref, yc_ref, mod_ref, g_ref,
             dx1_ref, dya_ref, dyc_ref, dwo_ref, s_ref, acc):
        i = pl.program_id(0)

        @pl.when(i == 0)
        def _():
            s_ref[...] = jnp.zeros_like(s_ref)
            acc[...] = jnp.zeros_like(acc)

        dh2 = _dot(dug_ref[...], w_ref[0:F, :]) + _dot(duv_ref[...], w_ref[F:F2, :])
        x1 = x1_ref[...]
        rstd = lax.rsqrt(jnp.mean(x1 * x1, axis=-1, keepdims=True) + EPS)
        dxn, dsh, dsc, dgn = _norm_bwd(dh2, x1, g_ref[...], mod_ref[4:5, :], rstd)
        dx1 = dx2_ref[...] + dxn
        dx1_ref[...] = dx1
        dpj = (dx1 * mod_ref[2:3, :]).astype(bf16)
        dyc = _dg(dpj, wo_ref[...], NT)
        dya_ref[...] = dyc[:, :DA].astype(bf16)
        dyc_ref[...] = dyc[:, DA:]
        acc[0:DA, :] += _dg(ya_ref[...], dpj, TN)
        acc[DA:D, :] += _dg(yc_ref[...], dpj, TN)

        @pl.when(i == nt - 1)
        def _():
            dwo_ref[...] = acc[...].astype(bf16)

        s_ref[0:1, :] += dsh
        s_ref[1:2, :] += dsc
        s_ref[2:3, :] += dgn
        s_ref[3:4, :] += jnp.sum(dx1 * pj_ref[...].astype(f32), axis=0, keepdims=True)

    row = lambda w: pl.BlockSpec((tm, w), lambda i: (i, 0))
    return pl.pallas_call(
        body, name="norm2_bwd", grid=(nt,),
        in_specs=[row(F), row(F), _resident((F2, D)), row(D), row(D), row(D), _resident((D, D)), row(DA), row(DA),
                  _full((8, D)), _full((1, D))],
        out_specs=[row(D), row(DA), row(DA), _full((D, D)), _full((8, D))],
        out_shape=[jax.ShapeDtypeStruct((T, D), f32), jax.ShapeDtypeStruct((T, DA), bf16), jax.ShapeDtypeStruct((T, DA), f32),
                   jax.ShapeDtypeStruct((D, D), bf16), jax.ShapeDtypeStruct((8, D), f32)],
        scratch_shapes=[pltpu.VMEM((D, D), f32)],
        compiler_params=_cp("arbitrary"),
    )(dug, duv, w_upT, x1, dx2, proj, w_out, y_na, y_cv, modv, g2)


def _conf_bwd(a, g, cv, dy, conv_w, ln_g, ln_b, blocks):
    tt = CONV_TT
    nt = T // tt
    sub = 32
    wn = tt + 2 * HALO
    nb = len(blocks)

    def body(ap, ac, an, gp, gc, gn, cp_, cc, cn, dp, dc, dn, w_ref, lg_ref, lb_ref, *rest):
        g_refs, (da_ref, dg_ref, dcw_ref, s_ref) = rest[:nb], rest[nb:nb + 4]
        recv_refs, (urot, drot, wacc), a2a_sems = rest[nb + 4:2 * nb + 4], rest[2 * nb + 4:2 * nb + 7], rest[2 * nb + 7:]
        i = pl.program_id(0)
        first, last = i == 0, i == nt - 1
        plans = [_a2a_plan(g_refs[k], recv_refs[k], *a2a_sems[3 * k:3 * k + 3]) for k in range(nb)]
        for start, _ in plans:
            pl.when(first)(start)

        @pl.when(first)
        def _():
            s_ref[...] = jnp.zeros_like(s_ref)
            wacc[...] = jnp.zeros_like(wacc)

        lg, lb = lg_ref[...], lb_ref[...]

        def ln_bwd(cvv, dyv):
            mu = jnp.mean(cvv, axis=-1, keepdims=True)
            xc = cvv - mu
            rstd = lax.rsqrt(jnp.mean(xc * xc, axis=-1, keepdims=True) + EPS)
            yn = xc * rstd
            z = yn * lg + lb
            sz = _sigmoid(z)
            dz = dyv * (sz * (1.0 + z * (1.0 - sz)))
            dyn = dz * lg
            dcv = rstd * (dyn - jnp.mean(dyn, axis=-1, keepdims=True) - yn * jnp.mean(dyn * yn, axis=-1, keepdims=True))
            return dcv, dz, yn

        urot[0, 0:HALO, :] = jnp.where(first, 0.0, ap[...] * _sigmoid(gp[...]))
        urot[0, HALO + tt:, :] = jnp.where(last, 0.0, an[...] * _sigmoid(gn[...]))
        drot[0, 0:HALO, :] = jnp.where(first, 0.0, ln_bwd(cp_[...], dp[...])[0])
        drot[0, HALO + tt:, :] = jnp.where(last, 0.0, ln_bwd(cn[...], dn[...])[0])
        for s in range(tt // sub):
            rr = pl.ds(s * sub, sub)
            urot[0, pl.ds(HALO + s * sub, sub), :] = ac[rr, :] * _sigmoid(gc[rr, :])
            dcv, dz, yn = ln_bwd(cc[rr, :], dc[rr, :])
            drot[0, pl.ds(HALO + s * sub, sub), :] = dcv
            s_ref[0:1, :] += jnp.sum(dcv, axis=0, keepdims=True)
            s_ref[1:2, :] += jnp.sum(dz * yn, axis=0, keepdims=True)
            s_ref[2:3, :] += jnp.sum(dz, axis=0, keepdims=True)
        _shifted_copies(urot, wn)
        _shifted_copies(drot, wn)
        w = w_ref[...]
        for s in range(tt // sub):
            rr = pl.ds(s * sub, sub)
            dcv = drot[0, pl.ds(HALO + s * sub, sub), :]
            acc = jnp.zeros((sub, DA), f32)
            for j in range(CW):
                ad, bd = divmod(2 * HALO - 1 - j, 8)
                au, bu = divmod(1 + j, 8)
                acc = acc + drot[bd, pl.ds(s * sub + 8 * ad, sub), :] * w[j:j + 1, :]
                part = urot[bu, pl.ds(s * sub + 8 * au, sub), :] * dcv
                wacc[j] += part[0:8] + part[8:16] + part[16:24] + part[24:32]
            av, gv = ac[rr, :], gc[rr, :]
            sg = _sigmoid(gv)
            da_ref[rr, :] = (acc * sg).astype(bf16)
            dg_ref[rr, :] = (acc * av * sg * (1.0 - sg)).astype(bf16)

        @pl.when(last)
        def _():
            for j in range(CW):
                dcw_ref[j:j + 1, :] = jnp.sum(wacc[j], axis=0, keepdims=True)
            dcw_ref[CW:CW + 1, :] = jnp.zeros((1, DA), f32)

        for _, finish in plans:
            pl.when(last)(finish)

    hs = _halo_specs(tt, DA, T // HALO)
    hbm = pl.BlockSpec(memory_space=pl.ANY)
    return pl.pallas_call(
        body, name="conf_bwd", grid=(nt,),
        in_specs=hs * 4 + [_full((CW, DA)), _full((1, DA)), _full((1, DA))] + [hbm] * nb,
        out_specs=[pl.BlockSpec((tt, DA), lambda i: (i, 0)), pl.BlockSpec((tt, DA), lambda i: (i, 0)),
                   _full((CW + 1, DA)), _full((8, DA))] + [hbm] * nb,
        out_shape=[jax.ShapeDtypeStruct((T, DA), bf16), jax.ShapeDtypeStruct((T, DA), bf16),
                   jax.ShapeDtypeStruct((CW + 1, DA), f32), jax.ShapeDtypeStruct((8, DA), f32)]
                  + [jax.ShapeDtypeStruct(b.shape, b.dtype) for b in blocks],
        scratch_shapes=[pltpu.VMEM((8, wn, DA), f32), pltpu.VMEM((8, wn, DA), f32), pltpu.VMEM((CW, 8, DA), f32)]
                       + _a2a_scratch() * nb,
        compiler_params=_cp("arbitrary"),
    )(a, a, a, g, g, g, cv, cv, cv, dy, dy, dy, conv_w, ln_g, ln_b, *blocks)


def _attn_bwd(q, k, v, y, dy, lse, bias_tab, blocks):
    zr = 256
    nb = len(blocks)

    def body(q_ref, k_ref, v_ref, y_ref, dy_ref, lse_ref, b_ref, *rest):
        g_refs, (dq_ref, dk_hbm, dv_hbm, db_ref) = rest[:nb], rest[nb:nb + 4]
        recv_refs, (dk_s, dv_s, sem), a2a_sems = rest[nb + 4:2 * nb + 4], rest[2 * nb + 4:2 * nb + 7], rest[2 * nb + 7:]
        r = pl.program_id(0)
        plans = [_a2a_plan(g_refs[i], recv_refs[i], *a2a_sems[3 * i:3 * i + 3]) for i in range(nb)]
        for start, _ in plans:
            pl.when(r == 0)(start)

        @pl.when(r == 0)
        def _():
            def z(i, _):
                rr = pl.ds(pl.multiple_of(i * zr, zr), zr)
                dk_s[rr, :] = jnp.zeros((zr, DA), f32)
                dv_s[rr, :] = jnp.zeros((zr, DA), f32)
                return 0
            lax.fori_loop(0, TA // zr, z, 0)

        @pl.when((r <= WR // 2) | (r > GW - WR // 2))
        def _():
            db_ref[...] = jnp.zeros_like(db_ref)

        ks = pl.multiple_of(_win_start(r) * GW, GW)
        win = pl.ds(ks, WR * GW)
        qq, yy, dyy, lse_v = q_ref[...], y_ref[...], dy_ref[...], lse_ref[...]
        lo = lax.broadcasted_iota(jnp.int32, (GW, 2 * HD), 1) < HD
        ops, pairs = [], []
        for pr in range(NH // 2):
            ps = slice(pr * 2 * HD, (pr + 1) * 2 * HD)
            q2, do2 = qq[:, ps], dyy[:, ps]
            prod = do2.astype(f32) * yy[:, ps].astype(f32)
            delta = jnp.concatenate([jnp.sum(jnp.where(lo, prod, 0.0), axis=-1, keepdims=True),
                                     jnp.sum(jnp.where(lo, 0.0, prod), axis=-1, keepdims=True)], axis=0)
            qst, dost = _stack_pair(q2, lo), _stack_pair(do2, lo)
            kw, vw = k_ref[win, ps], v_ref[win, ps]
            kc, vc = k_ref[T:TA, ps], v_ref[T:TA, ps]
            ops.append((kw, kc))
            pairs.append((qst, dost, delta, _dg(qst, kw, NT), _dg(qst, kc, NT), _dg(dost, vw, NT), _dg(dost, vc, NT)))
        grads = []
        for pr, (qst, dost, delta, sl, sc, dpl, dpc) in enumerate(pairs):
            lh = jnp.concatenate([lse_v[:, 2 * pr:2 * pr + 1], lse_v[:, 2 * pr + 1:2 * pr + 2]], axis=0)
            bias2 = b_ref[0, 2 * pr:2 * pr + 2].reshape(2 * GW, WR * GW)
            pl_ = jnp.exp(sl * SCALE + bias2 - lh)
            pc = jnp.exp(sc * SCALE - lh)
            dsl = pl_ * (dpl - delta)
            dsc = pc * (dpc - delta)
            db_ref[0, 2 * pr:2 * pr + 2] += dsl.reshape(2, GW, WR * GW)
            grads.append((qst, dost, pl_.astype(bf16), pc.astype(bf16), dsl.astype(bf16), dsc.astype(bf16)))
        for pr in range(NH // 2):
            ps = slice(pr * 2 * HD, (pr + 1) * 2 * HD)
            kw, kc = ops[pr]
            qst, dost, plb, pcb, dslb, dscb = grads[pr]
            dqst = _dot(dslb, kw) + _dot(dscb, kc)
            dq_ref[:, ps] = (jnp.where(lo, dqst[0:GW], dqst[GW:]) * SCALE).astype(bf16)
            dk_s[win, ps] += _dg(dslb, qst, TN) * SCALE
            dv_s[win, ps] += _dg(plb, dost, TN)
            dk_s[T:TA, ps] += _dg(dscb, qst, TN) * SCALE
            dv_s[T:TA, ps] += _dg(pcb, dost, TN)

        @pl.when(r == GW - 1)
        def _():
            c1 = pltpu.make_async_copy(dk_s, dk_hbm, sem.at[0])
            c2 = pltpu.make_async_copy(dv_s, dv_hbm, sem.at[1])
            c1.start()
            c2.start()
            c1.wait()
            c2.wait()

        for _, finish in plans:
            pl.when(r == GW - 1)(finish)

    rowq = lambda: pl.BlockSpec((GW, DA), lambda r: (r, 0))
    hbm = pl.BlockSpec(memory_space=pl.ANY)
    return pl.pallas_call(
        body, name="attn_bwd", grid=(GW,),
        in_specs=[rowq(), _full((TA, DA)), _full((TA, DA)), rowq(), rowq(), pl.BlockSpec((GW, NH), lambda r: (r, 0)),
                  pl.BlockSpec((1, NH, GW, WR * GW), lambda r: (_pattern(r), 0, 0, 0))] + [hbm] * nb,
        out_specs=[rowq(), hbm, hbm, pl.BlockSpec((1, NH, GW, WR * GW), lambda r: (_pattern(r), 0, 0, 0))] + [hbm] * nb,
        out_shape=[jax.ShapeDtypeStruct((T, DA), bf16), jax.ShapeDtypeStruct((TA, DA), f32), jax.ShapeDtypeStruct((TA, DA), f32),
                   jax.ShapeDtypeStruct((8, NH, GW, WR * GW), f32)] + [jax.ShapeDtypeStruct(b.shape, b.dtype) for b in blocks],
        scratch_shapes=[pltpu.VMEM((TA, DA), f32), pltpu.VMEM((TA, DA), f32), pltpu.SemaphoreType.DMA((2,))] + _a2a_scratch() * nb,
        compiler_params=_cp("arbitrary"),
    )(q, k, v, y, dy, lse, bias_tab, *blocks)


def _sum_blocks(recv, name):
    _, r, n = recv.shape
    tr = next(cand for cand in (176, 128, 64, 32, 16) if r % cand == 0)

    def body(a_ref, o_ref):
        acc = a_ref[0].astype(f32)
        for d in range(1, NDEV):
            acc = acc + a_ref[d].astype(f32)
        o_ref[...] = acc

    return pl.pallas_call(
        body, name=name, grid=(r // tr,),
        in_specs=[pl.BlockSpec((NDEV, tr, n), lambda i: (0, i, 0))],
        out_specs=pl.BlockSpec((tr, n), lambda i: (i, 0)),
        out_shape=jax.ShapeDtypeStruct((r, n), f32),
        compiler_params=_cp("parallel"),
    )(recv)


def _rpb_reduce(dbias):
    rev = np.eye(GW, dtype=np.float32)[::-1]

    def body(d_ref, rev_ref, o_ref):
        rv = rev_ref[...]
        for h in range(NH):
            dv = d_ref[0, h]
            r0 = dv.astype(bf16)
            e1 = dv - r0.astype(f32)
            r1 = e1.astype(bf16)
            r2 = (e1 - r1.astype(f32)).astype(bf16)
            rr = _dot(rv, r0) + _dot(rv, r1) + _dot(rv, r2)
            skew = pltpu.roll(rr, 0, 1, stride=1, stride_axis=0)
            o_ref[0, h:h + 1, :] = jnp.sum(skew, axis=0, keepdims=True)

    return pl.pallas_call(
        body, name="rpb_reduce", grid=(8,),
        in_specs=[pl.BlockSpec((1, NH, GW, WR * GW), lambda p: (p, 0, 0, 0)), _full((GW, GW))],
        out_specs=pl.BlockSpec((1, NH, WR * GW), lambda p: (p, 0, 0)),
        out_shape=jax.ShapeDtypeStruct((8, NH, WR * GW), f32),
        compiler_params=_cp("parallel"),
    )(dbias, jnp.asarray(rev, dtype=bf16))


def _norm1_bwd(dq, dk, dv, da, dg, w_inT, x0, ctx0, h, dx1, modv, g1, blocks):
    tm = 256
    nt = TA // tm
    nx = T // tm
    nb = len(blocks)

    def body(dq_ref, dk_ref, dv_ref, da_ref, dg_ref, w_ref, x_ref, c_ref, h_ref, dx1_ref, mod_ref, g_ref, *rest):
        g_refs, (dx_ref, dwo_ref, s_ref) = rest[:nb], rest[nb:nb + 3]
        recv_refs, dw_ref, a2a_sems = rest[nb + 3:2 * nb + 3], rest[2 * nb + 3], rest[2 * nb + 4:]
        i = pl.program_id(0)
        is_ctx = i == nt - 1
        plans = [_a2a_plan(g_refs[k], recv_refs[k], *a2a_sems[3 * k:3 * k + 3]) for k in range(nb)]
        for start, _ in plans:
            pl.when(i == 0)(start)

        @pl.when(i == 0)
        def _():
            s_ref[...] = jnp.zeros_like(s_ref)
            dw_ref[...] = jnp.zeros_like(dw_ref)

        hb = h_ref[...]
        dkb, dvb = dk_ref[...].astype(bf16), dv_ref[...].astype(bf16)
        dw_ref[DA:2 * DA, :] += _dg(dkb, hb, TN)
        dw_ref[2 * DA:3 * DA, :] += _dg(dvb, hb, TN)
        dh_kv = _dot(dkb, w_ref[DA:2 * DA, :]) + _dot(dvb, w_ref[2 * DA:3 * DA, :])
        gain = g_ref[...]

        @pl.when(is_ctx)
        def _():
            xv = c_ref[...]
            rstd = lax.rsqrt(jnp.mean(xv * xv, axis=-1, keepdims=True) + EPS)
            _, dsh, dsc, dgn = _norm_bwd(dh_kv, xv, gain, mod_ref[7:8, :], rstd)
            s_ref[2:3, :] += dgn
            s_ref[3:4, :] += dsh
            s_ref[4:5, :] += dsc
            dwo_ref[...] = dw_ref[...].astype(bf16)

        @pl.when(jnp.logical_not(is_ctx))
        def _():
            dqb, dab, dgb = dq_ref[...], da_ref[...], dg_ref[...]
            dw_ref[0:DA, :] += _dg(dqb, hb, TN)
            dw_ref[3 * DA:4 * DA, :] += _dg(dab, hb, TN)
            dw_ref[4 * DA:5 * DA, :] += _dg(dgb, hb, TN)
            dh = (dh_kv + _dot(dqb, w_ref[0:DA, :]) + _dot(dab, w_ref[3 * DA:4 * DA, :])
                  + _dot(dgb, w_ref[4 * DA:5 * DA, :]))
            xv = x_ref[...]
            rstd = lax.rsqrt(jnp.mean(xv * xv, axis=-1, keepdims=True) + EPS)
            dxn, dsh, dsc, dgn = _norm_bwd(dh, xv, gain, mod_ref[1:2, :], rstd)
            dx_ref[...] = dx1_ref[...] + dxn
            s_ref[0:1, :] += dsh
            s_ref[1:2, :] += dsc
            s_ref[2:3, :] += dgn

        for _, finish in plans:
            pl.when(is_ctx)(finish)

    row = lambda w: pl.BlockSpec((tm, w), lambda i: (i, 0))
    lrow = lambda w: pl.BlockSpec((tm, w), lambda i: (jnp.minimum(i, nx - 1), 0))
    hbm = pl.BlockSpec(memory_space=pl.ANY)
    return pl.pallas_call(
        body, name="norm1_bwd", grid=(nt,),
        in_specs=[lrow(DA), row(DA), row(DA), lrow(DA), lrow(DA), _full((5 * DA, D)), lrow(D), _full((TC, D)), row(D),
                  lrow(D), _full((8, D)), _full((1, D))] + [hbm] * nb,
        out_specs=[lrow(D), _full((5 * DA, D)), _full((8, D))] + [hbm] * nb,
        out_shape=[jax.ShapeDtypeStruct((T, D), f32), jax.ShapeDtypeStruct((5 * DA, D), bf16), jax.ShapeDtypeStruct((8, D), f32)]
                  + [jax.ShapeDtypeStruct(b.shape, b.dtype) for b in blocks],
        scratch_shapes=[pltpu.VMEM((5 * DA, D), f32)] + _a2a_scratch() * nb,
        compiler_params=_cp("arbitrary"),
    )(dq, dk, dv, da, dg, w_inT, x0, ctx0, h, dx1, modv, g1, *blocks)


def _adamw(w, g, m, v, name):
    r, c = w.shape
    tr = max(t for t in range(8, r + 1, 8) if r % t == 0 and t * c * 4 <= 2 * 1024 * 1024)

    def body(w_ref, g_ref, m_ref, v_ref, d_ref, nm_ref, nv_ref):
        gv = g_ref[...]
        nm = ADAM_B1 * m_ref[...] + (1.0 - ADAM_B1) * gv
        nv = ADAM_B2 * v_ref[...] + (1.0 - ADAM_B2) * (gv * gv)
        m_hat = nm * (1.0 / (1.0 - ADAM_B1 ** ADAM_STEP))
        v_hat = nv * (1.0 / (1.0 - ADAM_B2 ** ADAM_STEP))
        d_ref[...] = -ADAM_LR * (m_hat / (jnp.sqrt(v_hat) + ADAM_EPS) + ADAM_WD * w_ref[...])
        nm_ref[...] = nm
        nv_ref[...] = nv

    spec = pl.BlockSpec((tr, c), lambda i: (i, 0))
    return pl.pallas_call(
        body, name=name, grid=(r // tr,),
        in_specs=[spec] * 4, out_specs=[spec] * 3,
        out_shape=[jax.ShapeDtypeStruct((r, c), f32)] * 3,
        compiler_params=_cp("parallel"),
    )(w, g, m, v)


def _pad_rows128(vec):
    n = vec.shape[0]
    rows = -(-n // 1024) * 8
    return jnp.pad(vec, (0, rows * 128 - n)).reshape(rows, 128)


def _grad_rpb(dbias):
    lane_map, r_hot = _rpb_tables()
    return jnp.einsum("phl,lic,pir->hrc", _rpb_reduce(dbias), jnp.asarray(lane_map), jnp.asarray(r_hot),
                      precision=lax.Precision.HIGHEST)


def kernel(x, c, ctx, c_ctx, w_mod, b_mod, g_norm1, w_in, rpb, conv_w, conv_b, ln_g, ln_b, w_out, g_norm2, w_up, ffn_conv_w, ffn_conv_b, w_down, g_final, loss_target, m_c_ctx, m_w_mod, m_b_mod, m_g_norm1, m_w_in, m_rpb, m_conv_w, m_conv_b, m_ln_g, m_ln_b, m_w_out, m_g_norm2, m_w_up, m_ffn_conv_w, m_ffn_conv_b, m_w_down, m_g_final, v_c_ctx, v_w_mod, v_b_mod, v_g_norm1, v_w_in, v_rpb, v_conv_w, v_conv_b, v_ln_g, v_ln_b, v_w_out, v_g_norm2, v_w_up, v_ffn_conv_w, v_ffn_conv_b, v_w_down, v_g_final):
    me = 4 * lax.axis_index("x") + 2 * lax.axis_index("y") + lax.axis_index("c")
    nmod = w_mod.shape[2]
    n_in = w_in.shape[2]
    n_out = w_out.shape[1]
    n_up = w_up.shape[2]
    n_dn = w_down.shape[1]
    n_cw = conv_w.shape[2]

    w_inT, c_all = _allgather2([w_in[0].T.astype(bf16), c.reshape(8, 128)], "ag_w_in")

    c_all = c_all.reshape(NDEV, D)
    cvec = jnp.concatenate([c_all, c_ctx[None, :], jnp.zeros((7, D), f32)], axis=0)
    b_sh = lax.dynamic_slice(b_mod, (0, me * nmod), (1, nmod))
    mod_sh = _mod_fwd(cvec, w_mod[0], b_sh)
    n_modp = 16 * nmod
    payload = jnp.concatenate([mod_sh.reshape(-1), conv_w[0].reshape(-1), ffn_conv_w[0].reshape(-1)])
    flat = _small_allgather(_pad_rows128(payload), "ag_mod").reshape(NDEV, -1)
    mod_all = flat[:, :n_modp].reshape(NDEV, 16, nmod).transpose(1, 0, 2).reshape(16, 6 * D)
    mod_me = lax.dynamic_index_in_dim(mod_all, me, 0, keepdims=False).reshape(6, D)
    mod_c = mod_all[8]
    modv = jnp.concatenate([mod_me, mod_c[None, 0:D], mod_c[None, D:2 * D]], axis=0)
    o1 = n_modp + CW * n_cw
    conv_w_f = flat[:, n_modp:o1].reshape(NDEV, CW, n_cw).transpose(1, 0, 2).reshape(CW, DA)
    fcw_f = flat[:, o1:o1 + 3 * n_up].reshape(NDEV, 3, n_up).transpose(1, 0, 2).reshape(3, F2)

    x0, ctx0 = x[0], ctx[0]
    h, q, k, v, a, g, w_down_f = _in_proj(x0, ctx0, g_norm1, modv, w_inT, [w_down[0].astype(bf16)])
    bias_tab = _bias_table(rpb[0])
    y_cv, cv, w_out_f = _conf_fwd(a, g, conv_w_f, conv_b, ln_g, ln_b, [w_out[0].astype(bf16)])
    y_na, lse, w_upT = _attn_fwd(q, k, v, bias_tab, [w_up[0].T.astype(bf16)])
    x1, proj, h2 = _out_proj(x0, y_na, y_cv, w_out_f, modv, g_norm2)
    ffn, u_t, u2_t = _ffn_fwd(h2, w_upT, fcw_f, ffn_conv_b, w_down_f)
    dx2, dffn, s_loss = _loss_bwd(ffn, x1, loss_target[0], modv, g_final[None, :])

    dug, duv, dw_up, dw_down, dcwg, dcwv, dcbg, dcbv = _ffn_bwd(h2, dffn, u_t, u2_t, fcw_f, w_down_f)
    dx1, dy_na, dy_cv, dw_out, s_n2 = _norm2_bwd(dug, duv, w_upT, x1, dx2, proj, w_out_f, y_na, y_cv, modv, g_norm2)
    da, dg, dcw, s_cf, rv_down = _conf_bwd(a, g, cv, dy_cv, conv_w_f, ln_g, ln_b, [dw_down.reshape(NDEV, n_dn, D)])
    dq, dk, dv, dbias, rv_up = _attn_bwd(q, k, v, y_na, dy_na, lse, bias_tab, [dw_up.reshape(NDEV, n_up, D)])
    g_w_down = _sum_blocks(rv_down, "sum_w_down")
    g_w_upT = _sum_blocks(rv_up, "sum_w_up")
    grad_rpb_part = _grad_rpb(dbias)
    grad_x, dw_inT, s_n1, rv_out = _norm1_bwd(dq, dk, dv, da, dg, w_inT, x0, ctx0, h, dx1, modv, g_norm1,
                                              [dw_out.reshape(NDEV, n_out, D)])
    g_w_out = _sum_blocks(rv_out, "sum_w_out")
    grad_x = grad_x[None]
    dfcw = jnp.concatenate([dcwg, dcwv], axis=1)
    dfcb = jnp.concatenate([dcbg[0], dcbv[0]])
    small = jnp.concatenate([dcw[:CW].reshape(CW, NDEV, n_cw).transpose(1, 0, 2).reshape(NDEV, CW * n_cw),
                             dfcw.reshape(3, NDEV, n_up).transpose(1, 0, 2).reshape(NDEV, 3 * n_up)], axis=1)
    small = jnp.pad(small.reshape(NDEV, 4, D), ((0, 0), (0, 12), (0, 0))).astype(bf16)
    dmod = jnp.concatenate([s_n1[0], s_n1[1], s_n2[3], s_n2[0], s_n2[1], s_loss[1]])
    dmodc = jnp.concatenate([s_n1[3], s_n1[4]])
    parts = [dmodc, s_n1[2], grad_rpb_part.reshape(-1), s_cf[0], s_cf[1], s_cf[2], s_n2[2], dfcb, s_loss[0], s_loss[3, 0:1]]
    sizes = [p.shape[0] for p in parts]
    pvec = _pad_rows128(jnp.concatenate([dmod] + parts))
    r_a, gath = _reduce_scatter2(jnp.concatenate([dw_inT.reshape(NDEV, n_in, D), small], axis=1), [pvec], "rs_w_in")
    g_w_inT = r_a[:n_in]
    sm = r_a[n_in:n_in + 4].reshape(-1)
    g_conv_w = sm[:CW * n_cw].reshape(CW, n_cw)
    g_fcw = sm[CW * n_cw:].reshape(3, n_up)
    tot = _sum_rows8(gath, "sum_small").reshape(-1)
    dmod_all = gath.reshape(NDEV, -1)[:, :6 * D]
    offs = np.cumsum([6 * D] + sizes)
    pick = lambda j: tot[offs[j]:offs[j + 1]]
    dmodc_t = jnp.pad(pick(0), (0, 4 * D))
    g_b_mod = (tot[:6 * D] + dmodc_t)[None, :]
    g_g_norm1 = pick(1)[None, :]
    g_rpb = pick(2).reshape(1, NH, 2 * WR - 1, 2 * NCOL - 1)
    g_conv_b, g_ln_g, g_ln_b = pick(3)[None, :], pick(4)[None, :], pick(5)[None, :]
    g_g_norm2 = pick(6)[None, :]
    g_fcb = pick(7)[None, :]
    g_g_final = pick(8)
    loss = pick(9)[0]
    dm_rows = jnp.concatenate([dmod_all, dmodc_t[None, :], jnp.zeros((7, 6 * D), f32)], axis=0)
    dm_sh = lax.dynamic_slice(dm_rows, (0, me * nmod), (16, nmod))
    g_w_mod, gc_part = _mod_bwd(cvec, dm_sh, w_mod[0])
    gc_sum = _sum_rows8(_small_allgather(gc_part[0].reshape(8, 128), "ag_cctx"), "sum_cctx").reshape(D)
    sg_c = _sigmoid(c_ctx)
    g_c_ctx = gc_sum * (sg_c * (1.0 + c_ctx * (1.0 - sg_c)))

    big = [("w_mod", w_mod[0], g_w_mod, m_w_mod[0], v_w_mod[0]), ("w_in", w_in[0].T, g_w_inT, m_w_in[0].T, v_w_in[0].T),
           ("w_out", w_out[0], g_w_out, m_w_out[0], v_w_out[0]), ("w_up", w_up[0].T, g_w_upT, m_w_up[0].T, v_w_up[0].T),
           ("w_down", w_down[0], g_w_down, m_w_down[0], v_w_down[0])]
    upd = {n: _adamw(wv, gv, mv, vv, "adamw_" + n) for n, wv, gv, mv, vv in big}
    for n in ("w_in", "w_up"):
        upd[n] = tuple(arr.T for arr in upd[n])
    g_w_in, g_w_up = g_w_inT.T, g_w_upT.T
    smalls = [("c_ctx", c_ctx, g_c_ctx, m_c_ctx, v_c_ctx), ("b_mod", b_mod, g_b_mod, m_b_mod, v_b_mod),
              ("g_norm1", g_norm1, g_g_norm1, m_g_norm1, v_g_norm1), ("rpb", rpb, g_rpb, m_rpb, v_rpb),
              ("conv_w", conv_w, g_conv_w[None], m_conv_w, v_conv_w), ("conv_b", conv_b, g_conv_b, m_conv_b, v_conv_b),
              ("ln_g", ln_g, g_ln_g, m_ln_g, v_ln_g), ("ln_b", ln_b, g_ln_b, m_ln_b, v_ln_b),
              ("g_norm2", g_norm2, g_g_norm2, m_g_norm2, v_g_norm2),
              ("ffn_conv_w", ffn_conv_w, g_fcw[None], m_ffn_conv_w, v_ffn_conv_w),
              ("ffn_conv_b", ffn_conv_b, g_fcb, m_ffn_conv_b, v_ffn_conv_b), ("g_final", g_final, g_g_final, m_g_final, v_g_final)]
    packed = [_pad_rows128(jnp.concatenate([t[j].reshape(-1) for t in smalls])) for j in (1, 2, 3, 4)]
    sd, sm_, sv = _adamw(*packed, "adamw_small")
    so = np.cumsum([0] + [int(np.prod(t[1].shape)) for t in smalls])
    for j, t in enumerate(smalls):
        shp = t[1].shape
        upd[t[0]] = tuple(arr.reshape(-1)[so[j]:so[j + 1]].reshape(shp) for arr in (sd, sm_, sv))
    grads = {"c_ctx": g_c_ctx, "w_mod": g_w_mod[None], "b_mod": g_b_mod, "g_norm1": g_g_norm1, "w_in": g_w_in[None],
             "rpb": g_rpb, "conv_w": g_conv_w[None], "conv_b": g_conv_b, "ln_g": g_ln_g, "ln_b": g_ln_b,
             "w_out": g_w_out[None], "g_norm2": g_g_norm2, "w_up": g_w_up[None], "ffn_conv_w": g_fcw[None],
             "ffn_conv_b": g_fcb, "w_down": g_w_down[None], "g_final": g_g_final}
    names = ["c_ctx", "w_mod", "b_mod", "g_norm1", "w_in", "rpb", "conv_w", "conv_b", "ln_g", "ln_b", "w_out", "g_norm2",
             "w_up", "ffn_conv_w", "ffn_conv_b", "w_down", "g_final"]
    shapes = {n: grads[n].shape for n in names}
    outs = [loss, grad_x] + [grads[n] for n in names]
    for j in range(3):
        outs += [upd[n][j].reshape(shapes[n]) for n in names]
    return tuple(outs)
```

```python
import functools

import numpy as np
import jax
import jax.numpy as jnp
from jax import lax
from jax.experimental import pallas as pl
from jax.experimental.pallas import tpu as pltpu

f32 = jnp.float32
bf16 = jnp.bfloat16

D = 1024
T = 4096
TC = 256
TA = T + TC
DA = 512
NH = 8
HD = 64
GW = 64
WR = 8
NCOL = 16
F = 2816
F2 = 2 * F
CW = 31
NDEV = 8
EPS = 1e-6
SCALE = HD ** -0.5
NEG = -1e30
MESH = pl.DeviceIdType.MESH

NT = (((1,), (1,)), ((), ()))
TN = (((0,), (0,)), ((), ()))

ADAM_LR, ADAM_B1, ADAM_B2, ADAM_EPS, ADAM_WD, ADAM_STEP = 0.001, 0.9, 0.999, 1e-08, 0.01, 10

VMEM_LIMIT = 56 * 1024 * 1024


def _cp(*sem):
    return pltpu.CompilerParams(dimension_semantics=sem or None, vmem_limit_bytes=VMEM_LIMIT)


def _dot(a, b):
    return jnp.dot(a, b, preferred_element_type=f32)


def _dg(a, b, dims):
    return lax.dot_general(a, b, dims, preferred_element_type=f32)


def _sigmoid(x):
    return 1.0 / (1.0 + jnp.exp(-x))


def _full(shape):
    n = len(shape)
    return pl.BlockSpec(shape, lambda *_: (0,) * n)


def _resident(shape):
    n = len(shape)
    return pl.BlockSpec(shape, lambda *_: (0,) * n, pipeline_mode=pl.Buffered(1))


def _my_pos():
    return lax.axis_index("x"), lax.axis_index("y"), lax.axis_index("c")


def _small_gather_plan(v_ref, out_ref, send_sems, recv_sems):
    x, y, c = _my_pos()
    me = 4 * x + 2 * y + c
    peers = []
    for k in range(1, NDEV):
        kx, ky, kc = (k >> 2) & 1, (k >> 1) & 1, k & 1
        peers.append((x ^ kx, y ^ ky, c ^ kc))

    def copy(k, slot, to):
        return pltpu.make_async_remote_copy(
            src_ref=v_ref, dst_ref=out_ref.at[slot], send_sem=send_sems.at[k], recv_sem=recv_sems.at[k],
            device_id=to, device_id_type=MESH)

    def start():
        out_ref[me] = v_ref[...]
        for k, p in enumerate(peers):
            copy(k, me, p).start()

    def finish():
        for k, (px, py, pc) in enumerate(peers):
            copy(k, 4 * px + 2 * py + pc, (x, y, c)).wait_recv()
        for k, p in enumerate(peers):
            copy(k, me, p).wait_send()

    return start, finish


def _small_gather_scratch():
    return [pltpu.SemaphoreType.DMA((NDEV - 1,)), pltpu.SemaphoreType.DMA((NDEV - 1,))]


def _small_allgather(v, name):
    n = v.shape[0]

    def body(v_ref, out_ref, send_sems, recv_sems):
        start, finish = _small_gather_plan(v_ref, out_ref, send_sems, recv_sems)
        start()
        finish()

    return pl.pallas_call(
        body, name=name,
        out_shape=jax.ShapeDtypeStruct((NDEV, n, 128), f32),
        in_specs=[pl.BlockSpec(memory_space=pltpu.VMEM)],
        out_specs=pl.BlockSpec(memory_space=pltpu.VMEM),
        scratch_shapes=_small_gather_scratch(),
    )(v)


def _ag2_plan(x_refs, out_refs, send_sems, recv_sems, local_sems):
    na = len(x_refs)
    x, y, c = _my_pos()
    me, sibling = (x, y, c), (x, y, 1 - c)
    chips = [(1 - x, y), (x, 1 - y), (1 - x, 1 - y)]

    def rows(i, px, py, pc):
        m_per = x_refs[i].shape[0]
        return out_refs[i].at[pl.ds(pl.multiple_of((4 * px + 2 * py + pc) * m_per, 16 if m_per % 16 == 0 else 8), m_per), :]

    def copies(k, block, to, from_shard=False):
        return [pltpu.make_async_remote_copy(
            src_ref=x_refs[i] if from_shard else rows(i, *block), dst_ref=rows(i, *block),
            send_sem=send_sems.at[k * na + i], recv_sem=recv_sems.at[k * na + i], device_id=to, device_id_type=MESH)
            for i in range(na)]

    def mine():
        return [pltpu.make_async_copy(x_refs[i], rows(i, *me), local_sems.at[i]) for i in range(na)]

    def first():
        cps = copies(0, me, sibling, True)
        for j, chip in enumerate(chips):
            cps += copies(1 + j, me, (*chip, c), True)
        return cps

    def start():
        for cp in mine() + first():
            cp.start()

    def forward():
        for j, chip in enumerate(chips):
            for cp in copies(1 + j, (*chip, c), me):
                cp.wait_recv()
            for cp in copies(4 + j, (*chip, c), sibling):
                cp.start()

    def finish():
        for cp in copies(0, sibling, me):
            cp.wait_recv()
        for j, chip in enumerate(chips):
            for cp in copies(4 + j, (*chip, 1 - c), me):
                cp.wait_recv()
        for cp in first():
            cp.wait_send()
        for j, chip in enumerate(chips):
            for cp in copies(4 + j, (*chip, c), sibling):
                cp.wait_send()
        for cp in mine():
            cp.wait()

    return start, forward, finish


def _ag2_scratch(na):
    return [pltpu.SemaphoreType.DMA((7 * na,)), pltpu.SemaphoreType.DMA((7 * na,)), pltpu.SemaphoreType.DMA((na,))]


def _a2a_plan(g_ref, recv_ref, send_sems, recv_sems, local_sem):
    x, y, c = _my_pos()
    me = 4 * x + 2 * y + c
    peers = []
    for k in range(1, NDEV):
        kx, ky, kc = (k >> 2) & 1, (k >> 1) & 1, k & 1
        peers.append((x ^ kx, y ^ ky, c ^ kc))

    def sends():
        return [pltpu.make_async_remote_copy(
            src_ref=g_ref.at[4 * px + 2 * py + pc], dst_ref=recv_ref.at[me], send_sem=send_sems.at[k], recv_sem=recv_sems.at[k],
            device_id=(px, py, pc), device_id_type=MESH) for k, (px, py, pc) in enumerate(peers)]

    def own():
        return pltpu.make_async_copy(g_ref.at[me], recv_ref.at[me], local_sem)

    def start():
        own().start()
        for cp in sends():
            cp.start()

    def finish():
        for k, (px, py, pc) in enumerate(peers):
            pltpu.make_async_remote_copy(
                src_ref=g_ref.at[me], dst_ref=recv_ref.at[4 * px + 2 * py + pc], send_sem=send_sems.at[k],
                recv_sem=recv_sems.at[k], device_id=(x, y, c), device_id_type=MESH).wait_recv()
        for cp in sends():
            cp.wait_send()
        own().wait()

    return start, finish


def _a2a_scratch():
    return [pltpu.SemaphoreType.DMA((NDEV - 1,)), pltpu.SemaphoreType.DMA((NDEV - 1,)), pltpu.SemaphoreType.DMA]


def _reduce_scatter2(g, small, blocks, name):
    _, r, n = g.shape
    ch = 16
    nch = r // ch
    ns, nb = len(small), len(blocks)

    def body(g_ref, *rest):
        ni = ns + nb
        v_refs, blk_refs, out_ref = rest[:ns], rest[ns:ni], rest[ni]
        vout_refs, recv_refs = rest[ni + 1:ni + 1 + ns], rest[ni + 1 + ns:2 * ni + 1]
        a_ref, h_ref, b_ref, s1_send, s1_recv, s2_send, s2_recv = rest[2 * ni + 1:2 * ni + 8]
        gather_sems, a2a_sems = rest[2 * ni + 8:2 * ni + 8 + 2 * ns], rest[2 * ni + 8 + 2 * ns:]
        gathers = [_small_gather_plan(v_refs[i], vout_refs[i], *gather_sems[2 * i:2 * i + 2]) for i in range(ns)]
        gathers += [_a2a_plan(blk_refs[i], recv_refs[i], *a2a_sems[3 * i:3 * i + 3]) for i in range(nb)]
        for start, _ in gathers:
            start()
        x, y, c = _my_pos()
        sibling = (x, y, 1 - c)
        s1 = []
        for j in range(4):
            cp = pltpu.make_async_remote_copy(
                src_ref=g_ref.at[2 * j + (1 - c)], dst_ref=a_ref.at[j], send_sem=s1_send.at[j], recv_sem=s1_recv.at[j],
                device_id=sibling, device_id_type=MESH)
            cp.start()
            s1.append(cp)
        for cp in s1:
            cp.wait_recv()

        def add1(i, _):
            rr = pl.ds(pl.multiple_of(i * ch, ch), ch)
            for j in range(4):
                h_ref[j, rr, :] = (g_ref[2 * j + c, rr, :].astype(f32) + a_ref[j, rr, :].astype(f32)).astype(bf16)
            return 0
        lax.fori_loop(0, nch, add1, 0)
        mychip = 2 * x + y
        s2 = []
        for m in range(1, 4):
            mx, my_ = (m >> 1) & 1, m & 1
            px, py = x ^ mx, y ^ my_
            cp = pltpu.make_async_remote_copy(
                src_ref=h_ref.at[2 * px + py], dst_ref=b_ref.at[m - 1], send_sem=s2_send.at[m - 1], recv_sem=s2_recv.at[m - 1],
                device_id=(px, py, c), device_id_type=MESH)
            cp.start()
            s2.append(cp)
        for cp in s2:
            cp.wait_recv()

        def add2(i, _):
            rr = pl.ds(pl.multiple_of(i * ch, ch), ch)
            acc = h_ref[mychip, rr, :].astype(f32)
            for m in range(3):
                acc = acc + b_ref[m, rr, :].astype(f32)
            out_ref[rr, :] = acc
            return 0
        lax.fori_loop(0, nch, add2, 0)
        for cp in s1 + s2:
            cp.wait_send()
        for _, finish in gathers:
            finish()

    vmem = pl.BlockSpec(memory_space=pltpu.VMEM)
    return pl.pallas_call(
        body, name=name,
        out_shape=[jax.ShapeDtypeStruct((r, n), f32)] + [jax.ShapeDtypeStruct((NDEV,) + v.shape, v.dtype) for v in small]
                  + [jax.ShapeDtypeStruct(b.shape, b.dtype) for b in blocks],
        in_specs=[vmem] * (1 + ns + nb),
        out_specs=[vmem] * (1 + ns + nb),
        scratch_shapes=[pltpu.VMEM((4, r, n), bf16), pltpu.VMEM((4, r, n), bf16), pltpu.VMEM((3, r, n), bf16),
                        pltpu.SemaphoreType.DMA((4,)), pltpu.SemaphoreType.DMA((4,)),
                        pltpu.SemaphoreType.DMA((3,)), pltpu.SemaphoreType.DMA((3,))]
                       + _small_gather_scratch() * ns + _a2a_scratch() * nb,
        compiler_params=pltpu.CompilerParams(vmem_limit_bytes=VMEM_LIMIT),
    )(g, *small, *blocks)


def _head(w_in_sh, c8, cctx8, w_mod_sh, b_sh, convpay):
    nmod = w_mod_sh.shape[1]
    npay = convpay.shape[0]

    def body(w_ref, c_ref, cc_ref, wm_ref, b_ref, pay_ref, wout_ref, call_ref, mall_ref, pall_ref, mod_s,
             ag_send, ag_recv, ag_local, c_send, c_recv, m_send, m_recv, p_send, p_recv):
        start, forward, finish = _ag2_plan([w_ref], [wout_ref], ag_send, ag_recv, ag_local)
        c_start, c_finish = _small_gather_plan(c_ref, call_ref, c_send, c_recv)
        m_start, m_finish = _small_gather_plan(mod_s, mall_ref, m_send, m_recv)
        p_start, p_finish = _small_gather_plan(pay_ref, pall_ref, p_send, p_recv)
        c_start()
        start()
        p_start()
        c_finish()
        acc = jnp.zeros((16, nmod), f32)
        for j in range(D // 128):
            rows = jnp.concatenate([call_ref[:, j, :], cc_ref[j:j + 1, :], jnp.zeros((7, 128), f32)], axis=0)
            act = (rows * _sigmoid(rows)).astype(bf16)
            acc = acc + _dot(act, wm_ref[j * 128:(j + 1) * 128, :].astype(bf16))
        mod_s[...] = acc + b_ref[...]
        m_start()
        forward()
        finish()
        m_finish()
        p_finish()

    vmem = pl.BlockSpec(memory_space=pltpu.VMEM)
    return pl.pallas_call(
        body, name="head",
        out_shape=[jax.ShapeDtypeStruct((NDEV * w_in_sh.shape[0], D), bf16), jax.ShapeDtypeStruct((NDEV, 8, 128), f32),
                   jax.ShapeDtypeStruct((NDEV, 16, nmod), f32), jax.ShapeDtypeStruct((NDEV, npay, 128), f32)],
        in_specs=[vmem] * 6, out_specs=[vmem] * 4,
        scratch_shapes=[pltpu.VMEM((16, nmod), f32)] + _ag2_scratch(1) + _small_gather_scratch() * 3,
        compiler_params=pltpu.CompilerParams(vmem_limit_bytes=VMEM_LIMIT),
    )(w_in_sh, c8, cctx8, w_mod_sh, b_sh, convpay)


def _mod_bwd(cvec, dm_sh, w_sh):
    def body(c_ref, dm_ref, w_ref, gw_ref, gc_ref):
        cv = c_ref[...]
        act = (cv * _sigmoid(cv)).astype(bf16)
        gw_ref[...] = _dg(act, dm_ref[...].astype(bf16), TN)
        gc_ref[...] = _dg(dm_ref[8:16, :].astype(bf16), w_ref[...].astype(bf16), NT)
    return pl.pallas_call(
        body, name="mod_bwd",
        out_shape=(jax.ShapeDtypeStruct(w_sh.shape, f32), jax.ShapeDtypeStruct((8, D), f32)))(cvec, dm_sh, w_sh)


def _sum_rows8(a, name):
    n = a.shape[1]

    def body(a_ref, o_ref):
        acc = a_ref[0]
        for d in range(1, NDEV):
            acc = acc + a_ref[d]
        o_ref[...] = acc
    return pl.pallas_call(body, name=name, out_shape=jax.ShapeDtypeStruct((n, 128), f32))(a)


def _in_proj(x0, ctx0, g1, modv, w_inT, shards):
    tm = 256
    nt = TA // tm
    nx = T // tm
    na = len(shards)

    def body(x_ref, c_ref, g_ref, mod_ref, w_ref, *rest):
        x_refs, (h_ref, q_ref, k_ref, v_ref, a_ref, gg_ref) = rest[:na], rest[na:na + 6]
        out_refs, sems = rest[na + 6:2 * na + 6], rest[2 * na + 6:]
        i = pl.program_id(0)
        if na:
            start, forward, finish = _ag2_plan(x_refs, out_refs, *sems)
            pl.when(i == 0)(start)
            pl.when(i == nt - 2)(forward)
        is_ctx = i == nt - 1
        xv = jnp.where(is_ctx, c_ref[...], x_ref[...])
        rstd = lax.rsqrt(jnp.mean(xv * xv, axis=-1, keepdims=True) + EPS)
        sh = jnp.where(is_ctx, mod_ref[6:7, :], mod_ref[0:1, :])
        sc = jnp.where(is_ctx, mod_ref[7:8, :], mod_ref[1:2, :])
        h = ((xv * rstd * g_ref[...]) * (1.0 + sc) + sh).astype(bf16)
        h_ref[...] = h
        for j, o_ref in enumerate((q_ref, k_ref, v_ref, a_ref, gg_ref)):
            o_ref[...] = _dg(h, w_ref[j * DA:(j + 1) * DA, :], NT).astype(o_ref.dtype)
        if na:
            pl.when(is_ctx)(finish)

    row = lambda w: pl.BlockSpec((tm, w), lambda i: (i, 0))
    hbm = pl.BlockSpec(memory_space=pl.ANY)
    return pl.pallas_call(
        body, name="in_proj", grid=(nt,),
        in_specs=[pl.BlockSpec((tm, D), lambda i: (jnp.minimum(i, nx - 1), 0)), _full((TC, D)),
                  _full((1, D)), _full((8, D)), _full((5 * DA, D))] + [hbm] * na,
        out_specs=[row(D), row(DA), row(DA), row(DA), row(DA), row(DA)] + [hbm] * na,
        out_shape=[jax.ShapeDtypeStruct((TA, D), bf16)] + [jax.ShapeDtypeStruct((TA, DA), bf16)] * 3
                  + [jax.ShapeDtypeStruct((TA, DA), f32)] * 2
                  + [jax.ShapeDtypeStruct((NDEV * sh.shape[0], sh.shape[1]), sh.dtype) for sh in shards],
        scratch_shapes=_ag2_scratch(na) if na else [],
        compiler_params=_cp("arbitrary"),
    )(x0, ctx0, g1, modv, w_inT, *shards)


def _win_start(r):
    return jnp.clip(r - WR // 2, 0, GW - WR)


def _pattern(r):
    return _win_start(r) - r + (WR - 1)


def _bias_table(rpb):
    qc = np.arange(GW)[:, None]
    kc = np.arange(GW)[None, :]
    cs = np.clip(qc - NCOL // 2, 0, GW - NCOL)
    valid = np.tile(((kc >= cs) & (kc < cs + NCOL)).astype(np.int32), (1, WR))
    pad = jnp.pad(rpb, ((0, 0), (0, 0), (0, GW - (2 * NCOL - 1))))
    base = jnp.stack([pad[:, p:p + WR, :].reshape(NH, WR * GW) for p in range(8)])

    def body(base_ref, valid_ref, o_ref):
        ok = valid_ref[...] != 0
        for h in range(NH):
            row = jnp.broadcast_to(base_ref[0, h:h + 1, :], (GW, WR * GW))
            skew = pltpu.roll(row, WR * GW - (NCOL - 1), 1, stride=1, stride_axis=0)
            o_ref[0, h] = jnp.where(ok, skew, NEG)

    return pl.pallas_call(
        body, name="bias_table", grid=(8,),
        in_specs=[pl.BlockSpec((1, NH, WR * GW), lambda p: (p, 0, 0)), _full((GW, WR * GW))],
        out_specs=pl.BlockSpec((1, NH, GW, WR * GW), lambda p: (p, 0, 0, 0)),
        out_shape=jax.ShapeDtypeStruct((8, NH, GW, WR * GW), f32),
        compiler_params=_cp("parallel"),
    )(base, jnp.asarray(valid))


def _rpb_tables():
    lane_map = np.zeros((WR * GW, WR, 2 * NCOL - 1), np.float32)
    for i in range(WR):
        for t in range(GW):
            if t >= GW - NCOL:
                lane_map[i * GW + t, i, t - (GW - NCOL)] = 1.0
            elif t < NCOL - 1:
                lane_map[i * GW + t, (i - 1) % WR, t + NCOL] = 1.0
    p = np.arange(8)[:, None]
    i = np.arange(WR)[None, :]
    r_hot = ((p + i)[:, :, None] == np.arange(2 * WR - 1)[None, None, :]).astype(np.float32)
    return lane_map, r_hot


AG_FORWARD_ROW = 58


def _stack_pair(x2, lo):
    z = jnp.zeros_like(x2)
    return jnp.concatenate([jnp.where(lo, x2, z), jnp.where(lo, z, x2)], axis=0)


def _attn_fwd(q, k, v, bias_tab, shards):
    na = len(shards)

    def body(q_ref, k_ref, v_ref, b_ref, *rest):
        x_refs, (y_ref, lse_ref), out_refs, sems = rest[:na], rest[na:na + 2], rest[na + 2:2 * na + 2], rest[2 * na + 2:]
        r = pl.program_id(0)
        if na:
            start, forward, finish = _ag2_plan(x_refs, out_refs, *sems)
            pl.when(r == 0)(start)
            pl.when(r == AG_FORWARD_ROW)(forward)
        ks = pl.multiple_of(_win_start(r) * GW, GW)
        qq = q_ref[...]
        lo = lax.broadcasted_iota(jnp.int32, (GW, 2 * HD), 1) < HD
        kv, scores = [], []
        for pr in range(NH // 2):
            ps = slice(pr * 2 * HD, (pr + 1) * 2 * HD)
            qst = _stack_pair(qq[:, ps], lo)
            kw, kc = k_ref[pl.ds(ks, WR * GW), ps], k_ref[T:TA, ps]
            kv.append((v_ref[pl.ds(ks, WR * GW), ps], v_ref[T:TA, ps]))
            bias2 = b_ref[0, 2 * pr:2 * pr + 2].reshape(2 * GW, WR * GW)
            scores.append((_dg(qst, kw, NT) * SCALE + bias2, _dg(qst, kc, NT) * SCALE))
        probs = []
        for pr, (sl, sc) in enumerate(scores):
            m = jnp.maximum(jnp.max(sl, axis=-1, keepdims=True), jnp.max(sc, axis=-1, keepdims=True))
            pl_ = jnp.exp(sl - m)
            pc = jnp.exp(sc - m)
            l = jnp.sum(pl_, axis=-1, keepdims=True) + jnp.sum(pc, axis=-1, keepdims=True)
            lse = m + jnp.log(l)
            lse_ref[:, 2 * pr:2 * pr + 1] = lse[0:GW]
            lse_ref[:, 2 * pr + 1:2 * pr + 2] = lse[GW:]
            probs.append((pl_.astype(bf16), pc.astype(bf16), 1.0 / l))
        for pr in range(NH // 2):
            ps = slice(pr * 2 * HD, (pr + 1) * 2 * HD)
            vw, vc = kv[pr]
            pb, cb, rl = probs[pr]
            o = (_dot(pb, vw) + _dot(cb, vc)) * rl
            y_ref[:, ps] = jnp.where(lo, o[0:GW], o[GW:]).astype(bf16)
        if na:
            pl.when(r == GW - 1)(finish)

    hbm = pl.BlockSpec(memory_space=pl.ANY)
    return pl.pallas_call(
        body, name="attn_fwd", grid=(GW,),
        in_specs=[pl.BlockSpec((GW, DA), lambda r: (r, 0)), _full((TA, DA)), _full((TA, DA)),
                  pl.BlockSpec((1, NH, GW, WR * GW), lambda r: (_pattern(r), 0, 0, 0))] + [hbm] * na,
        out_specs=[pl.BlockSpec((GW, DA), lambda r: (r, 0)), pl.BlockSpec((GW, NH), lambda r: (r, 0))] + [hbm] * na,
        out_shape=[jax.ShapeDtypeStruct((T, DA), bf16), jax.ShapeDtypeStruct((T, NH), f32)]
                  + [jax.ShapeDtypeStruct((NDEV * s.shape[0], s.shape[1]), s.dtype) for s in shards],
        scratch_shapes=_ag2_scratch(na) if na else [],
        compiler_params=_cp("arbitrary"),
    )(q, k, v, bias_tab, *shards)


CONV_TT = 512
HALO = 16


def _halo_specs(tt, w, nrows_blocks):
    per = tt // HALO
    prev = pl.BlockSpec((HALO, w), lambda i: (jnp.maximum(i * per - 1, 0), 0))
    cur = pl.BlockSpec((tt, w), lambda i: (i, 0))
    nxt = pl.BlockSpec((HALO, w), lambda i: (jnp.minimum((i + 1) * per, nrows_blocks - 1), 0))
    return [prev, cur, nxt]


def _shifted_copies(rot, wn):
    for b in range(1, 8):
        rot[b, 0:wn - 8, :] = rot[0, pl.ds(b, wn - 8), :]


def _conf_fwd(a, g, conv_w, conv_b, ln_g, ln_b, shards):
    tt = CONV_TT
    nt = T // tt
    sub = 32
    wn = tt + 2 * HALO
    na = len(shards)

    def body(ap, ac, an, gp, gc, gn, w_ref, b_ref, lg_ref, lb_ref, *rest):
        x_refs, (y_ref, cv_ref), out_refs, (rot, *sems) = rest[:na], rest[na:na + 2], rest[na + 2:2 * na + 2], rest[2 * na + 2:]
        i = pl.program_id(0)
        if na:
            start, forward, finish = _ag2_plan(x_refs, out_refs, *sems)
            pl.when(i == 0)(start)
            pl.when(i == nt // 2)(forward)
        rot[0, 0:HALO, :] = jnp.where(i > 0, ap[...] * _sigmoid(gp[...]), 0.0)
        rot[0, HALO:HALO + tt, :] = ac[...] * _sigmoid(gc[...])
        rot[0, HALO + tt:, :] = jnp.where(i < nt - 1, an[...] * _sigmoid(gn[...]), 0.0)
        _shifted_copies(rot, wn)
        w = w_ref[...]
        for s in range(tt // sub):
            acc = jnp.zeros((sub, DA), f32)
            for j in range(CW):
                a8, b8 = divmod(1 + j, 8)
                acc = acc + rot[b8, pl.ds(s * sub + 8 * a8, sub), :] * w[j:j + 1, :]
            cv = acc + b_ref[...]
            cv_ref[pl.ds(s * sub, sub), :] = cv
            mu = jnp.mean(cv, axis=-1, keepdims=True)
            xc = cv - mu
            rstd = lax.rsqrt(jnp.mean(xc * xc, axis=-1, keepdims=True) + EPS)
            z = xc * rstd * lg_ref[...] + lb_ref[...]
            y_ref[pl.ds(s * sub, sub), :] = (z * _sigmoid(z)).astype(bf16)
        if na:
            pl.when(i == nt - 1)(finish)

    hs = _halo_specs(tt, DA, T // HALO)
    hbm = pl.BlockSpec(memory_space=pl.ANY)
    return pl.pallas_call(
        body, name="conf_fwd", grid=(nt,),
        in_specs=hs + hs + [_full((CW, DA)), _full((1, DA)), _full((1, DA)), _full((1, DA))] + [hbm] * na,
        out_specs=[pl.BlockSpec((tt, DA), lambda i: (i, 0)), pl.BlockSpec((tt, DA), lambda i: (i, 0))] + [hbm] * na,
        out_shape=[jax.ShapeDtypeStruct((T, DA), bf16), jax.ShapeDtypeStruct((T, DA), f32)]
                  + [jax.ShapeDtypeStruct((NDEV * s.shape[0], s.shape[1]), s.dtype) for s in shards],
        scratch_shapes=[pltpu.VMEM((8, wn, DA), f32)] + (_ag2_scratch(na) if na else []),
        compiler_params=_cp("arbitrary"),
    )(a, a, a, g, g, g, conv_w, conv_b, ln_g, ln_b, *shards)


def _out_proj(xa, y_na, y_cv, w_out, modv, g2):
    tm = 512

    def body(x_ref, ya_ref, yc_ref, w_ref, mod_ref, g_ref, x1_ref, pj_ref, h2_ref):
        proj = _dot(ya_ref[...], w_ref[0:DA, :]) + _dot(yc_ref[...], w_ref[DA:D, :])
        x1 = x_ref[...] + mod_ref[2:3, :] * proj
        x1_ref[...] = x1
        pj_ref[...] = proj.astype(bf16)
        rstd = lax.rsqrt(jnp.mean(x1 * x1, axis=-1, keepdims=True) + EPS)
        h2_ref[...] = ((x1 * rstd * g_ref[...]) * (1.0 + mod_ref[4:5, :]) + mod_ref[3:4, :]).astype(bf16)

    row = lambda w: pl.BlockSpec((tm, w), lambda i: (i, 0))
    return pl.pallas_call(
        body, name="out_proj", grid=(T // tm,),
        in_specs=[row(D), row(DA), row(DA), _full((D, D)), _full((8, D)), _full((1, D))],
        out_specs=[row(D), row(D), row(D)],
        out_shape=[jax.ShapeDtypeStruct((T, D), f32), jax.ShapeDtypeStruct((T, D), bf16), jax.ShapeDtypeStruct((T, D), bf16)],
        compiler_params=_cp("parallel"),
    )(xa, y_na, y_cv, w_out, modv, g2)


FFN_TT = 2048
FFN_CT = 256
FFN_NC = F // FFN_CT
FFN_SUB = 32


def _row_neighbours(ref, r, n):
    blk = ref[pl.ds(r - 8, n + 16), :]
    return blk[8:8 + n, :], pltpu.roll(blk, 1, 0)[8:8 + n, :], pltpu.roll(blk, n + 15, 0)[8:8 + n, :]


def _ffn_specs(tt, ct, by_token_first):
    tc = (lambda f: (lambda t, c: f(t, c))) if by_token_first else (lambda f: (lambda c, t: f(t, c)))
    per = tt // HALO
    halo = [pl.BlockSpec((HALO, D), tc(lambda t, c: (jnp.maximum(t * per - 1, 0), 0))),
            pl.BlockSpec((tt, D), tc(lambda t, c: (t, 0))),
            pl.BlockSpec((HALO, D), tc(lambda t, c: (jnp.minimum((t + 1) * per, T // HALO - 1), 0)))]
    weights = [pl.BlockSpec((ct, D), tc(lambda t, c: (c, 0))), pl.BlockSpec((ct, D), tc(lambda t, c: (c + FFN_NC, 0))),
               pl.BlockSpec((3, ct), tc(lambda t, c: (0, c))), pl.BlockSpec((3, ct), tc(lambda t, c: (0, c + FFN_NC))),
               pl.BlockSpec((1, ct), tc(lambda t, c: (0, c))), pl.BlockSpec((1, ct), tc(lambda t, c: (0, c + FFN_NC))),
               pl.BlockSpec((ct, D), tc(lambda t, c: (c, 0)))]
    return halo, weights


def _ffn_fwd(h2, w_upT, fcw, fcb, w_down):
    tt, ct = FFN_TT, FFN_CT
    nt = T // tt
    wn = tt + 2 * HALO
    half = tt // 2

    def body(hp, hc, hn, wg_ref, wv_ref, cwg_ref, cwv_ref, cbg_ref, cbv_ref, wd_ref, o_ref, u_ref, u2_ref, hwin, uwin, act):
        t = pl.program_id(0)
        c = pl.program_id(1)

        @pl.when(c == 0)
        def _():
            hwin[0:HALO, :] = jnp.where(t > 0, hp[...], jnp.zeros_like(hp[...]))
            hwin[HALO:HALO + tt, :] = hc[...]
            hwin[HALO + tt:, :] = jnp.where(t < nt - 1, hn[...], jnp.zeros_like(hn[...]))
            o_ref[...] = jnp.zeros_like(o_ref)

        for r0, r1 in ((0, half + 2 * HALO), (half + 2 * HALO, wn)):
            hw = hwin[r0:r1, :]
            uwin[r0:r1, :ct] = _dg(hw, wg_ref[...], NT)
            uwin[r0:r1, ct:] = _dg(hw, wv_ref[...], NT)
        cw = jnp.concatenate([cwg_ref[...], cwv_ref[...]], axis=1)
        cb = jnp.concatenate([cbg_ref[...], cbv_ref[...]], axis=1)
        for p in range(2):
            for r in range(p * half, (p + 1) * half, FFN_SUB):
                uc, prev, nxt = _row_neighbours(uwin, HALO + r, FFN_SUB)
                u2 = prev * cw[0:1, :] + uc * cw[1:2, :] + nxt * cw[2:3, :] + cb
                u_ref[r:r + FFN_SUB, :] = uc.astype(bf16)
                u2_ref[r:r + FFN_SUB, :] = u2
                gate = u2[:, :ct]
                act[r:r + FFN_SUB, :] = (gate * _sigmoid(gate) * u2[:, ct:]).astype(bf16)
            rows = slice(p * half, (p + 1) * half)
            o_ref[rows, :] += _dot(act[rows, :], wd_ref[...])

    halo, weights = _ffn_specs(tt, ct, True)
    pair = pl.BlockSpec((tt, 2 * ct), lambda t, c: (t, c))
    return pl.pallas_call(
        body, name="ffn_fwd", grid=(nt, FFN_NC),
        in_specs=halo + weights,
        out_specs=[pl.BlockSpec((tt, D), lambda t, c: (t, 0)), pair, pair],
        out_shape=[jax.ShapeDtypeStruct((T, D), f32), jax.ShapeDtypeStruct((T, F2), bf16), jax.ShapeDtypeStruct((T, F2), f32)],
        scratch_shapes=[pltpu.VMEM((wn, D), bf16), pltpu.VMEM((wn, 2 * ct), f32), pltpu.VMEM((tt, ct), bf16)],
        compiler_params=_cp("parallel", "arbitrary"),
    )(h2, h2, h2, w_upT, w_upT, fcw, fcw, fcb, fcb, w_down)


def _loss_bwd(ffn, x1, tgt, modv, gf):
    tm = 1024
    nt = T // tm

    def body(f_ref, x1_ref, t_ref, mod_ref, g_ref, dx2_ref, df_ref, s_ref):
        i = pl.program_id(0)

        @pl.when(i == 0)
        def _():
            s_ref[...] = jnp.zeros_like(s_ref)

        ff = f_ref[...]
        gt2 = mod_ref[5:6, :]
        x2 = x1_ref[...] + gt2 * ff
        rstd = lax.rsqrt(jnp.mean(x2 * x2, axis=-1, keepdims=True) + EPS)
        xh = x2 * rstd
        gfv = g_ref[...]
        e = xh * gfv - t_ref[...]
        dy = e * (1.0 / D)
        dxh = dy * gfv
        dx2 = rstd * (dxh - xh * jnp.mean(dxh * xh, axis=-1, keepdims=True))
        dx2_ref[...] = dx2
        df_ref[...] = (dx2 * gt2).astype(bf16)
        s_ref[0:1, :] += jnp.sum(dy * xh, axis=0, keepdims=True)
        s_ref[1:2, :] += jnp.sum(dx2 * ff, axis=0, keepdims=True)
        s_ref[2:3, :] += jnp.sum(e * e, axis=0, keepdims=True)

        @pl.when(i == nt - 1)
        def _():
            tot = jnp.sum(s_ref[2:3, :], axis=-1, keepdims=True) * (0.5 / D)
            s_ref[3:4, :] = jnp.broadcast_to(tot, (1, D))

    row = lambda: pl.BlockSpec((tm, D), lambda i: (i, 0))
    return pl.pallas_call(
        body, name="loss_bwd", grid=(nt,),
        in_specs=[row(), row(), row(), _full((8, D)), _full((1, D))],
        out_specs=[row(), row(), _full((8, D))],
        out_shape=[jax.ShapeDtypeStruct((T, D), f32), jax.ShapeDtypeStruct((T, D), bf16), jax.ShapeDtypeStruct((8, D), f32)],
        compiler_params=_cp("arbitrary"),
    )(ffn, x1, tgt, modv, gf)


def _ffn_bwd(h2, dffn, u_t, u2_t, fcw, w_down):
    tt, ct = FFN_TT, FFN_CT
    nt = T // tt
    wn = tt + 2 * HALO
    half = tt // 2

    def body(dp, dc, dn, hc, uc_ref, u2p, u2c, u2n, cwg_ref, cwv_ref, wd_ref,
             dug_ref, duv_ref, dwu_ref, dwd_ref, dcwg_ref, dcwv_ref, dcbg_ref, dcbv_ref,
             dwin, d2win, dawin, accu, accd, act, du):
        t = pl.program_id(1)
        first, last = t == 0, t == nt - 1
        zero = jnp.zeros((HALO, D), bf16)
        dwin[0:HALO, :] = jnp.where(first, zero, dp[...])
        dwin[HALO:HALO + tt, :] = dc[...]
        dwin[HALO + tt:, :] = jnp.where(last, zero, dn[...])

        @pl.when(first)
        def _():
            for r in (accu, accd, dcwg_ref, dcwv_ref, dcbg_ref, dcbv_ref):
                r[...] = jnp.zeros_like(r)

        cw = jnp.concatenate([cwg_ref[...], cwv_ref[...]], axis=1)
        split = half + 2 * HALO
        for r0, r1 in ((0, split), (split, wn)):
            dawin[r0:r1, :] = _dg(dwin[r0:r1, :], wd_ref[...], NT)

        def grads(u2v, dact):
            gate, val = u2v[:, :ct], u2v[:, ct:]
            sg = _sigmoid(gate)
            silu = gate * sg
            return dact * val * (sg * (1.0 + gate * (1.0 - sg))), dact * silu, silu * val

        for blk, r0 in ((u2p, 0), (u2n, HALO + tt)):
            dgate, dval, _ = grads(blk[...], dawin[r0:r0 + HALO, :])
            d2win[r0:r0 + HALO, :ct] = dgate
            d2win[r0:r0 + HALO, ct:] = dval
        for p in range(2):
            rows = slice(p * half, (p + 1) * half)
            for r in range(p * half, (p + 1) * half, FFN_SUB):
                dgate, dval, av = grads(u2c[r:r + FFN_SUB, :], dawin[HALO + r:HALO + r + FFN_SUB, :])
                d2win[HALO + r:HALO + r + FFN_SUB, :ct] = dgate
                d2win[HALO + r:HALO + r + FFN_SUB, ct:] = dval
                act[r:r + FFN_SUB, :] = av.astype(bf16)
            accd[...] += _dg(act[rows, :], dc[rows, :], TN)

        def fold8(x):
            out = x[0:8]
            for k in range(8, FFN_SUB, 8):
                out = out + x[k:k + 8]
            return out

        for p in range(2):
            rows = slice(p * half, (p + 1) * half)
            sums = [jnp.zeros((8, 2 * ct), f32) for _ in range(4)]
            for r in range(p * half, (p + 1) * half, FFN_SUB):
                d2c, d2m, d2p = _row_neighbours(d2win, HALO + r, FFN_SUB)
                ucur = uc_ref[r:r + FFN_SUB, :].astype(f32)
                sums[0] = sums[0] + fold8(d2c)
                for kk, dd in enumerate((d2p, d2c, d2m)):
                    sums[1 + kk] = sums[1 + kk] + fold8(ucur * dd)
                du[r:r + FFN_SUB, :] = (d2p * cw[0:1, :] + d2c * cw[1:2, :] + d2m * cw[2:3, :]).astype(bf16)
            dcb = jnp.sum(sums[0], axis=0, keepdims=True)
            dcbg_ref[...] += dcb[:, :ct]
            dcbv_ref[...] += dcb[:, ct:]
            for kk in range(3):
                dck = jnp.sum(sums[1 + kk], axis=0, keepdims=True)
                dcwg_ref[kk:kk + 1, :] += dck[:, :ct]
                dcwv_ref[kk:kk + 1, :] += dck[:, ct:]
            dug_ref[rows, :] = du[rows, :ct]
            duv_ref[rows, :] = du[rows, ct:]
            accu[...] += _dg(du[rows, :], hc[rows, :], TN)

        @pl.when(last)
        def _():
            dwu_ref[0] = accu[0:ct, :].astype(bf16)
            dwu_ref[1] = accu[ct:, :].astype(bf16)
            dwd_ref[...] = accd[...].astype(bf16)

    per = tt // HALO
    prev = lambda w: (lambda c, t: (jnp.maximum(t * per - 1, 0), c if w else 0))
    nxt = lambda w: (lambda c, t: (jnp.minimum((t + 1) * per, T // HALO - 1), c if w else 0))
    tile = lambda: pl.BlockSpec((ct, D), lambda c, t: (c, 0))
    lane = lambda r, off: pl.BlockSpec((r, ct), lambda c, t: (0, c + off))
    return pl.pallas_call(
        body, name="ffn_bwd", grid=(FFN_NC, nt),
        in_specs=[pl.BlockSpec((HALO, D), prev(False)), pl.BlockSpec((tt, D), lambda c, t: (t, 0)), pl.BlockSpec((HALO, D), nxt(False)),
                  pl.BlockSpec((tt, D), lambda c, t: (t, 0)), pl.BlockSpec((tt, 2 * ct), lambda c, t: (t, c)),
                  pl.BlockSpec((HALO, 2 * ct), prev(True)), pl.BlockSpec((tt, 2 * ct), lambda c, t: (t, c)),
                  pl.BlockSpec((HALO, 2 * ct), nxt(True)), lane(3, 0), lane(3, FFN_NC), tile()],
        out_specs=[pl.BlockSpec((tt, ct), lambda c, t: (t, c)), pl.BlockSpec((tt, ct), lambda c, t: (t, c)),
                   pl.BlockSpec((2, ct, D), lambda c, t: (0, c, 0)), tile(), lane(3, 0), lane(3, 0), lane(1, 0), lane(1, 0)],
        out_shape=[jax.ShapeDtypeStruct((T, F), bf16), jax.ShapeDtypeStruct((T, F), bf16),
                   jax.ShapeDtypeStruct((2, F, D), bf16), jax.ShapeDtypeStruct((F, D), bf16),
                   jax.ShapeDtypeStruct((3, F), f32), jax.ShapeDtypeStruct((3, F), f32),
                   jax.ShapeDtypeStruct((1, F), f32), jax.ShapeDtypeStruct((1, F), f32)],
        scratch_shapes=[pltpu.VMEM((wn, D), bf16), pltpu.VMEM((wn, 2 * ct), f32), pltpu.VMEM((wn, ct), f32),
                        pltpu.VMEM((2 * ct, D), f32), pltpu.VMEM((ct, D), f32),
                        pltpu.VMEM((tt, ct), bf16), pltpu.VMEM((tt, 2 * ct), bf16)],
        compiler_params=_cp("parallel", "arbitrary"),
    )(dffn, dffn, dffn, h2, u_t, u2_t, u2_t, u2_t, fcw, fcw, w_down)


def _norm_bwd(dh, xv, gain, sh_sc, rstd):
    xh = xv * rstd
    n = xh * gain
    dn = dh * (1.0 + sh_sc)
    dxh = dn * gain
    dx = rstd * (dxh - xh * jnp.mean(dxh * xh, axis=-1, keepdims=True))
    return (dx, jnp.sum(dh, axis=0, keepdims=True), jnp.sum(dh * n, axis=0, keepdims=True),
            jnp.sum(dn * xh, axis=0, keepdims=True))


def _norm2_bwd(dug, duv, w_upT, x1, dx2, proj, w_out, y_na, y_cv, modv, g2):
    tm = 512
    nt = T // tm

    def body(dug_ref, duv_ref, w_ref, x1_ref, dx2_ref, pj_ref, wo_ref, ya_ref, yc_ref, mod_ref, g_ref,
             dx1_ref, dya_ref, dyc_ref, dwo_ref, s_ref, acc):
        i = pl.program_id(0)

        @pl.when(i == 0)
        def _():
            s_ref[...] = jnp.zeros_like(s_ref)
            acc[...] = jnp.zeros_like(acc)

        dh2 = _dot(dug_ref[...], w_ref[0:F, :]) + _dot(duv_ref[...], w_ref[F:F2, :])
        x1 = x1_ref[...]
        rstd = lax.rsqrt(jnp.mean(x1 * x1, axis=-1, keepdims=True) + EPS)
        dxn, dsh, dsc, dgn = _norm_bwd(dh2, x1, g_ref[...], mod_ref[4:5, :], rstd)
        dx1 = dx2_ref[...] + dxn
        dx1_ref[...] = dx1
        dpj = (dx1 * mod_ref[2:3, :]).astype(bf16)
        dyc = _dg(dpj, wo_ref[...], NT)
        dya_ref[...] = dyc[:, :DA].astype(bf16)
        dyc_ref[...] = dyc[:, DA:]
        acc[0:DA, :] += _dg(ya_ref[...], dpj, TN)
        acc[DA:D, :] += _dg(yc_ref[...], dpj, TN)

        @pl.when(i == nt - 1)
        def _():
            dwo_ref[...] = acc[...].astype(bf16)

        s_ref[0:1, :] += dsh
        s_ref[1:2, :] += dsc
        s_ref[2:3, :] += dgn
        s_ref[3:4, :] += jnp.sum(dx1 * pj_ref[...].astype(f32), axis=0, keepdims=True)

    row = lambda w: pl.BlockSpec((tm, w), lambda i: (i, 0))
    return pl.pallas_call(
        body, name="norm2_bwd", grid=(nt,),
        in_specs=[row(F), row(F), _resident((F2, D)), row(D), row(D), row(D), _resident((D, D)), row(DA), row(DA),
                  _full((8, D)), _full((1, D))],
        out_specs=[row(D), row(DA), row(DA), _full((D, D)), _full((8, D))],
        out_shape=[jax.ShapeDtypeStruct((T, D), f32), jax.ShapeDtypeStruct((T, DA), bf16), jax.ShapeDtypeStruct((T, DA), f32),
                   jax.ShapeDtypeStruct((D, D), bf16), jax.ShapeDtypeStruct((8, D), f32)],
        scratch_shapes=[pltpu.VMEM((D, D), f32)],
        compiler_params=_cp("arbitrary"),
    )(dug, duv, w_upT, x1, dx2, proj, w_out, y_na, y_cv, modv, g2)


def _conf_bwd(a, g, cv, dy, conv_w, ln_g, ln_b, blocks):
    tt = CONV_TT
    nt = T // tt
    sub = 32
    wn = tt + 2 * HALO
    nb = len(blocks)

    def body(ap, ac, an, gp, gc, gn, cp_, cc, cn, dp, dc, dn, w_ref, lg_ref, lb_ref, *rest):
        g_refs, (da_ref, dg_ref, dcw_ref, s_ref) = rest[:nb], rest[nb:nb + 4]
        recv_refs, (urot, drot, wacc), a2a_sems = rest[nb + 4:2 * nb + 4], rest[2 * nb + 4:2 * nb + 7], rest[2 * nb + 7:]
        i = pl.program_id(0)
        first, last = i == 0, i == nt - 1
        plans = [_a2a_plan(g_refs[k], recv_refs[k], *a2a_sems[3 * k:3 * k + 3]) for k in range(nb)]
        for start, _ in plans:
            pl.when(first)(start)

        @pl.when(first)
        def _():
            s_ref[...] = jnp.zeros_like(s_ref)
            wacc[...] = jnp.zeros_like(wacc)

        lg, lb = lg_ref[...], lb_ref[...]

        def ln_bwd(cvv, dyv):
            mu = jnp.mean(cvv, axis=-1, keepdims=True)
            xc = cvv - mu
            rstd = lax.rsqrt(jnp.mean(xc * xc, axis=-1, keepdims=True) + EPS)
            yn = xc * rstd
            z = yn * lg + lb
            sz = _sigmoid(z)
            dz = dyv * (sz * (1.0 + z * (1.0 - sz)))
            dyn = dz * lg
            dcv = rstd * (dyn - jnp.mean(dyn, axis=-1, keepdims=True) - yn * jnp.mean(dyn * yn, axis=-1, keepdims=True))
            return dcv, dz, yn

        urot[0, 0:HALO, :] = jnp.where(first, 0.0, ap[...] * _sigmoid(gp[...]))
        urot[0, HALO + tt:, :] = jnp.where(last, 0.0, an[...] * _sigmoid(gn[...]))
        drot[0, 0:HALO, :] = jnp.where(first, 0.0, ln_bwd(cp_[...], dp[...])[0])
        drot[0, HALO + tt:, :] = jnp.where(last, 0.0, ln_bwd(cn[...], dn[...])[0])
        for s in range(tt // sub):
            rr = pl.ds(s * sub, sub)
            urot[0, pl.ds(HALO + s * sub, sub), :] = ac[rr, :] * _sigmoid(gc[rr, :])
            dcv, dz, yn = ln_bwd(cc[rr, :], dc[rr, :])
            drot[0, pl.ds(HALO + s * sub, sub), :] = dcv
            s_ref[0:1, :] += jnp.sum(dcv, axis=0, keepdims=True)
            s_ref[1:2, :] += jnp.sum(dz * yn, axis=0, keepdims=True)
            s_ref[2:3, :] += jnp.sum(dz, axis=0, keepdims=True)
        _shifted_copies(urot, wn)
        _shifted_copies(drot, wn)
        w = w_ref[...]
        for s in range(tt // sub):
            rr = pl.ds(s * sub, sub)
            dcv = drot[0, pl.ds(HALO + s * sub, sub), :]
            acc = jnp.zeros((sub, DA), f32)
            for j in range(CW):
                ad, bd = divmod(2 * HALO - 1 - j, 8)
                au, bu = divmod(1 + j, 8)
                acc = acc + drot[bd, pl.ds(s * sub + 8 * ad, sub), :] * w[j:j + 1, :]
                part = urot[bu, pl.ds(s * sub + 8 * au, sub), :] * dcv
                wacc[j] += part[0:8] + part[8:16] + part[16:24] + part[24:32]
            av, gv = ac[rr, :], gc[rr, :]
            sg = _sigmoid(gv)
            da_ref[rr, :] = (acc * sg).astype(bf16)
            dg_ref[rr, :] = (acc * av * sg * (1.0 - sg)).astype(bf16)

        @pl.when(last)
        def _():
            for j in range(CW):
                dcw_ref[j:j + 1, :] = jnp.sum(wacc[j], axis=0, keepdims=True)
            dcw_ref[CW:CW + 1, :] = jnp.zeros((1, DA), f32)

        for _, finish in plans:
            pl.when(last)(finish)

    hs = _halo_specs(tt, DA, T // HALO)
    hbm = pl.BlockSpec(memory_space=pl.ANY)
    return pl.pallas_call(
        body, name="conf_bwd", grid=(nt,),
        in_specs=hs * 4 + [_full((CW, DA)), _full((1, DA)), _full((1, DA))] + [hbm] * nb,
        out_specs=[pl.BlockSpec((tt, DA), lambda i: (i, 0)), pl.BlockSpec((tt, DA), lambda i: (i, 0)),
                   _full((CW + 1, DA)), _full((8, DA))] + [hbm] * nb,
        out_shape=[jax.ShapeDtypeStruct((T, DA), bf16), jax.ShapeDtypeStruct((T, DA), bf16),
                   jax.ShapeDtypeStruct((CW + 1, DA), f32), jax.ShapeDtypeStruct((8, DA), f32)]
                  + [jax.ShapeDtypeStruct(b.shape, b.dtype) for b in blocks],
        scratch_shapes=[pltpu.VMEM((8, wn, DA), f32), pltpu.VMEM((8, wn, DA), f32), pltpu.VMEM((CW, 8, DA), f32)]
                       + _a2a_scratch() * nb,
        compiler_params=_cp("arbitrary"),
    )(a, a, a, g, g, g, cv, cv, cv, dy, dy, dy, conv_w, ln_g, ln_b, *blocks)


def _attn_bwd(q, k, v, y, dy, lse, bias_tab, blocks):
    zr = 256
    nb = len(blocks)

    def body(q_ref, k_ref, v_ref, y_ref, dy_ref, lse_ref, b_ref, *rest):
        g_refs, (dq_ref, dk_hbm, dv_hbm, db_ref) = rest[:nb], rest[nb:nb + 4]
        recv_refs, (dk_s, dv_s, sem), a2a_sems = rest[nb + 4:2 * nb + 4], rest[2 * nb + 4:2 * nb + 7], rest[2 * nb + 7:]
        r = pl.program_id(0)
        plans = [_a2a_plan(g_refs[i], recv_refs[i], *a2a_sems[3 * i:3 * i + 3]) for i in range(nb)]
        for start, _ in plans:
            pl.when(r == 0)(start)

        @pl.when(r == 0)
        def _():
            def z(i, _):
                rr = pl.ds(pl.multiple_of(i * zr, zr), zr)
                dk_s[rr, :] = jnp.zeros((zr, DA), f32)
                dv_s[rr, :] = jnp.zeros((zr, DA), f32)
                return 0
            lax.fori_loop(0, TA // zr, z, 0)

        @pl.when((r <= WR // 2) | (r > GW - WR // 2))
        def _():
            db_ref[...] = jnp.zeros_like(db_ref)

        ks = pl.multiple_of(_win_start(r) * GW, GW)
        win = pl.ds(ks, WR * GW)
        qq, yy, dyy, lse_v = q_ref[...], y_ref[...], dy_ref[...], lse_ref[...]
        lo = lax.broadcasted_iota(jnp.int32, (GW, 2 * HD), 1) < HD
        ops, pairs = [], []
        for pr in range(NH // 2):
            ps = slice(pr * 2 * HD, (pr + 1) * 2 * HD)
            q2, do2 = qq[:, ps], dyy[:, ps]
            prod = do2.astype(f32) * yy[:, ps].astype(f32)
            delta = jnp.concatenate([jnp.sum(jnp.where(lo, prod, 0.0), axis=-1, keepdims=True),
                                     jnp.sum(jnp.where(lo, 0.0, prod), axis=-1, keepdims=True)], axis=0)
            qst, dost = _stack_pair(q2, lo), _stack_pair(do2, lo)
            kw, vw = k_ref[win, ps], v_ref[win, ps]
            kc, vc = k_ref[T:TA, ps], v_ref[T:TA, ps]
            ops.append((kw, kc))
            pairs.append((qst, dost, delta, _dg(qst, kw, NT), _dg(qst, kc, NT), _dg(dost, vw, NT), _dg(dost, vc, NT)))
        grads = []
        for pr, (qst, dost, delta, sl, sc, dpl, dpc) in enumerate(pairs):
            lh = jnp.concatenate([lse_v[:, 2 * pr:2 * pr + 1], lse_v[:, 2 * pr + 1:2 * pr + 2]], axis=0)
            bias2 = b_ref[0, 2 * pr:2 * pr + 2].reshape(2 * GW, WR * GW)
            pl_ = jnp.exp(sl * SCALE + bias2 - lh)
            pc = jnp.exp(sc * SCALE - lh)
            dsl = pl_ * (dpl - delta)
            dsc = pc * (dpc - delta)
            db_ref[0, 2 * pr:2 * pr + 2] += dsl.reshape(2, GW, WR * GW)
            grads.append((qst, dost, pl_.astype(bf16), pc.astype(bf16), dsl.astype(bf16), dsc.astype(bf16)))
        for pr in range(NH // 2):
            ps = slice(pr * 2 * HD, (pr + 1) * 2 * HD)
            kw, kc = ops[pr]
            qst, dost, plb, pcb, dslb, dscb = grads[pr]
            dqst = _dot(dslb, kw) + _dot(dscb, kc)
            dq_ref[:, ps] = (jnp.where(lo, dqst[0:GW], dqst[GW:]) * SCALE).astype(bf16)
            dk_s[win, ps] += _dg(dslb, qst, TN) * SCALE
            dv_s[win, ps] += _dg(plb, dost, TN)
            dk_s[T:TA, ps] += _dg(dscb, qst, TN) * SCALE
            dv_s[T:TA, ps] += _dg(pcb, dost, TN)

        @pl.when(r == GW - 1)
        def _():
            c1 = pltpu.make_async_copy(dk_s, dk_hbm, sem.at[0])
            c2 = pltpu.make_async_copy(dv_s, dv_hbm, sem.at[1])
            c1.start()
            c2.start()
            c1.wait()
            c2.wait()

        for _, finish in plans:
            pl.when(r == GW - 1)(finish)

    rowq = lambda: pl.BlockSpec((GW, DA), lambda r: (r, 0))
    hbm = pl.BlockSpec(memory_space=pl.ANY)
    return pl.pallas_call(
        body, name="attn_bwd", grid=(GW,),
        in_specs=[rowq(), _full((TA, DA)), _full((TA, DA)), rowq(), rowq(), pl.BlockSpec((GW, NH), lambda r: (r, 0)),
                  pl.BlockSpec((1, NH, GW, WR * GW), lambda r: (_pattern(r), 0, 0, 0))] + [hbm] * nb,
        out_specs=[rowq(), hbm, hbm, pl.BlockSpec((1, NH, GW, WR * GW), lambda r: (_pattern(r), 0, 0, 0))] + [hbm] * nb,
        out_shape=[jax.ShapeDtypeStruct((T, DA), bf16), jax.ShapeDtypeStruct((TA, DA), f32), jax.ShapeDtypeStruct((TA, DA), f32),
                   jax.ShapeDtypeStruct((8, NH, GW, WR * GW), f32)] + [jax.ShapeDtypeStruct(b.shape, b.dtype) for b in blocks],
        scratch_shapes=[pltpu.VMEM((TA, DA), f32), pltpu.VMEM((TA, DA), f32), pltpu.SemaphoreType.DMA((2,))] + _a2a_scratch() * nb,
        compiler_params=_cp("arbitrary"),
    )(q, k, v, y, dy, lse, bias_tab, *blocks)


def _sum_blocks(recv, name):
    _, r, n = recv.shape
    tr = next(cand for cand in (176, 128, 64, 32, 16) if r % cand == 0)

    def body(a_ref, o_ref):
        acc = a_ref[0].astype(f32)
        for d in range(1, NDEV):
            acc = acc + a_ref[d].astype(f32)
        o_ref[...] = acc

    return pl.pallas_call(
        body, name=name, grid=(r // tr,),
        in_specs=[pl.BlockSpec((NDEV, tr, n), lambda i: (0, i, 0))],
        out_specs=pl.BlockSpec((tr, n), lambda i: (i, 0)),
        out_shape=jax.ShapeDtypeStruct((r, n), f32),
        compiler_params=_cp("parallel"),
    )(recv)


def _rpb_reduce(dbias):
    rev = np.eye(GW, dtype=np.float32)[::-1]

    def body(d_ref, rev_ref, o_ref):
        rv = rev_ref[...]
        for h in range(NH):
            dv = d_ref[0, h]
            r0 = dv.astype(bf16)
            e1 = dv - r0.astype(f32)
            r1 = e1.astype(bf16)
            r2 = (e1 - r1.astype(f32)).astype(bf16)
            rr = _dot(rv, r0) + _dot(rv, r1) + _dot(rv, r2)
            skew = pltpu.roll(rr, 0, 1, stride=1, stride_axis=0)
            o_ref[0, h:h + 1, :] = jnp.sum(skew, axis=0, keepdims=True)

    return pl.pallas_call(
        body, name="rpb_reduce", grid=(8,),
        in_specs=[pl.BlockSpec((1, NH, GW, WR * GW), lambda p: (p, 0, 0, 0)), _full((GW, GW))],
        out_specs=pl.BlockSpec((1, NH, WR * GW), lambda p: (p, 0, 0)),
        out_shape=jax.ShapeDtypeStruct((8, NH, WR * GW), f32),
        compiler_params=_cp("parallel"),
    )(dbias, jnp.asarray(rev, dtype=bf16))


def _norm1_bwd(dq, dk, dv, da, dg, w_inT, x0, ctx0, h, dx1, modv, g1):
    tm = 256
    nt = TA // tm
    nx = T // tm

    def body(dq_ref, dk_ref, dv_ref, da_ref, dg_ref, w_ref, x_ref, c_ref, h_ref, dx1_ref, mod_ref, g_ref,
             dx_ref, dwo_ref, s_ref, dw_ref):
        i = pl.program_id(0)
        is_ctx = i == nt - 1

        @pl.when(i == 0)
        def _():
            s_ref[...] = jnp.zeros_like(s_ref)
            dw_ref[...] = jnp.zeros_like(dw_ref)

        hb = h_ref[...]
        dkb, dvb = dk_ref[...].astype(bf16), dv_ref[...].astype(bf16)
        dw_ref[DA:2 * DA, :] += _dg(dkb, hb, TN)
        dw_ref[2 * DA:3 * DA, :] += _dg(dvb, hb, TN)
        dh_kv = _dot(dkb, w_ref[DA:2 * DA, :]) + _dot(dvb, w_ref[2 * DA:3 * DA, :])
        gain = g_ref[...]

        @pl.when(is_ctx)
        def _():
            xv = c_ref[...]
            rstd = lax.rsqrt(jnp.mean(xv * xv, axis=-1, keepdims=True) + EPS)
            _, dsh, dsc, dgn = _norm_bwd(dh_kv, xv, gain, mod_ref[7:8, :], rstd)
            s_ref[2:3, :] += dgn
            s_ref[3:4, :] += dsh
            s_ref[4:5, :] += dsc
            dwo_ref[...] = dw_ref[...].astype(bf16)

        @pl.when(jnp.logical_not(is_ctx))
        def _():
            dqb, dab, dgb = dq_ref[...], da_ref[...], dg_ref[...]
            dw_ref[0:DA, :] += _dg(dqb, hb, TN)
            dw_ref[3 * DA:4 * DA, :] += _dg(dab, hb, TN)
            dw_ref[4 * DA:5 * DA, :] += _dg(dgb, hb, TN)
            dh = (dh_kv + _dot(dqb, w_ref[0:DA, :]) + _dot(dab, w_ref[3 * DA:4 * DA, :])
                  + _dot(dgb, w_ref[4 * DA:5 * DA, :]))
            xv = x_ref[...]
            rstd = lax.rsqrt(jnp.mean(xv * xv, axis=-1, keepdims=True) + EPS)
            dxn, dsh, dsc, dgn = _norm_bwd(dh, xv, gain, mod_ref[1:2, :], rstd)
            dx_ref[...] = dx1_ref[...] + dxn
            s_ref[0:1, :] += dsh
            s_ref[1:2, :] += dsc
            s_ref[2:3, :] += dgn

    row = lambda w: pl.BlockSpec((tm, w), lambda i: (i, 0))
    lrow = lambda w: pl.BlockSpec((tm, w), lambda i: (jnp.minimum(i, nx - 1), 0))
    return pl.pallas_call(
        body, name="norm1_bwd", grid=(nt,),
        in_specs=[lrow(DA), row(DA), row(DA), lrow(DA), lrow(DA), _full((5 * DA, D)), lrow(D), _full((TC, D)), row(D),
                  lrow(D), _full((8, D)), _full((1, D))],
        out_specs=[lrow(D), _full((5 * DA, D)), _full((8, D))],
        out_shape=[jax.ShapeDtypeStruct((T, D), f32), jax.ShapeDtypeStruct((5 * DA, D), bf16), jax.ShapeDtypeStruct((8, D), f32)],
        scratch_shapes=[pltpu.VMEM((5 * DA, D), f32)],
        compiler_params=_cp("arbitrary"),
    )(dq, dk, dv, da, dg, w_inT, x0, ctx0, h, dx1, modv, g1)


def _adamw(w, g, m, v, name):
    r, c = w.shape
    tr = max(t for t in range(8, r + 1, 8) if r % t == 0 and t * c * 4 <= 2 * 1024 * 1024)

    def body(w_ref, g_ref, m_ref, v_ref, d_ref, nm_ref, nv_ref):
        gv = g_ref[...]
        nm = ADAM_B1 * m_ref[...] + (1.0 - ADAM_B1) * gv
        nv = ADAM_B2 * v_ref[...] + (1.0 - ADAM_B2) * (gv * gv)
        m_hat = nm * (1.0 / (1.0 - ADAM_B1 ** ADAM_STEP))
        v_hat = nv * (1.0 / (1.0 - ADAM_B2 ** ADAM_STEP))
        d_ref[...] = -ADAM_LR * (m_hat / (jnp.sqrt(v_hat) + ADAM_EPS) + ADAM_WD * w_ref[...])
        nm_ref[...] = nm
        nv_ref[...] = nv

    spec = pl.BlockSpec((tr, c), lambda i: (i, 0))
    return pl.pallas_call(
        body, name=name, grid=(r // tr,),
        in_specs=[spec] * 4, out_specs=[spec] * 3,
        out_shape=[jax.ShapeDtypeStruct((r, c), f32)] * 3,
        compiler_params=_cp("parallel"),
    )(w, g, m, v)


def _pad_rows128(vec):
    n = vec.shape[0]
    rows = -(-n // 1024) * 8
    return jnp.pad(vec, (0, rows * 128 - n)).reshape(rows, 128)


def _grad_rpb(dbias):
    lane_map, r_hot = _rpb_tables()
    return jnp.einsum("phl,lic,pir->hrc", _rpb_reduce(dbias), jnp.asarray(lane_map), jnp.asarray(r_hot),
                      precision=lax.Precision.HIGHEST)


def kernel(x, c, ctx, c_ctx, w_mod, b_mod, g_norm1, w_in, rpb, conv_w, conv_b, ln_g, ln_b, w_out, g_norm2, w_up, ffn_conv_w, ffn_conv_b, w_down, g_final, loss_target, m_c_ctx, m_w_mod, m_b_mod, m_g_norm1, m_w_in, m_rpb, m_conv_w, m_conv_b, m_ln_g, m_ln_b, m_w_out, m_g_norm2, m_w_up, m_ffn_conv_w, m_ffn_conv_b, m_w_down, m_g_final, v_c_ctx, v_w_mod, v_b_mod, v_g_norm1, v_w_in, v_rpb, v_conv_w, v_conv_b, v_ln_g, v_ln_b, v_w_out, v_g_norm2, v_w_up, v_ffn_conv_w, v_ffn_conv_b, v_w_down, v_g_final):
    me = 4 * lax.axis_index("x") + 2 * lax.axis_index("y") + lax.axis_index("c")
    nmod = w_mod.shape[2]
    n_in = w_in.shape[2]
    n_out = w_out.shape[1]
    n_up = w_up.shape[2]
    n_dn = w_down.shape[1]
    n_cw = conv_w.shape[2]

    b_sh = lax.dynamic_slice(b_mod, (0, me * nmod), (1, nmod))
    convpay = _pad_rows128(jnp.concatenate([conv_w[0].reshape(-1), ffn_conv_w[0].reshape(-1)]))
    w_inT, c_all, mod_all, flat = _head(w_in[0].T.astype(bf16), c.reshape(8, 128), c_ctx.reshape(8, 128), w_mod[0], b_sh, convpay)
    cvec = jnp.concatenate([c_all.reshape(NDEV, D), c_ctx[None, :], jnp.zeros((7, D), f32)], axis=0)
    mod_all = mod_all.transpose(1, 0, 2).reshape(16, 6 * D)
    mod_me = lax.dynamic_index_in_dim(mod_all, me, 0, keepdims=False).reshape(6, D)
    mod_c = mod_all[8]
    modv = jnp.concatenate([mod_me, mod_c[None, 0:D], mod_c[None, D:2 * D]], axis=0)
    flat = flat.reshape(NDEV, -1)
    o1 = CW * n_cw
    conv_w_f = flat[:, :o1].reshape(NDEV, CW, n_cw).transpose(1, 0, 2).reshape(CW, DA)
    fcw_f = flat[:, o1:o1 + 3 * n_up].reshape(NDEV, 3, n_up).transpose(1, 0, 2).reshape(3, F2)

    x0, ctx0 = x[0], ctx[0]
    h, q, k, v, a, g, w_down_f = _in_proj(x0, ctx0, g_norm1, modv, w_inT, [w_down[0].astype(bf16)])
    bias_tab = _bias_table(rpb[0])
    y_cv, cv, w_out_f = _conf_fwd(a, g, conv_w_f, conv_b, ln_g, ln_b, [w_out[0].astype(bf16)])
    y_na, lse, w_upT = _attn_fwd(q, k, v, bias_tab, [w_up[0].T.astype(bf16)])
    x1, proj, h2 = _out_proj(x0, y_na, y_cv, w_out_f, modv, g_norm2)
    ffn, u_t, u2_t = _ffn_fwd(h2, w_upT, fcw_f, ffn_conv_b, w_down_f)
    dx2, dffn, s_loss = _loss_bwd(ffn, x1, loss_target[0], modv, g_final[None, :])

    dug, duv, dw_up, dw_down, dcwg, dcwv, dcbg, dcbv = _ffn_bwd(h2, dffn, u_t, u2_t, fcw_f, w_down_f)
    dx1, dy_na, dy_cv, dw_out, s_n2 = _norm2_bwd(dug, duv, w_upT, x1, dx2, proj, w_out_f, y_na, y_cv, modv, g_norm2)
    da, dg, dcw, s_cf, rv_down = _conf_bwd(a, g, cv, dy_cv, conv_w_f, ln_g, ln_b, [dw_down.reshape(NDEV, n_dn, D)])
    dq, dk, dv, dbias, rv_up = _attn_bwd(q, k, v, y_na, dy_na, lse, bias_tab, [dw_up.reshape(NDEV, n_up, D)])
    g_w_down = _sum_blocks(rv_down, "sum_w_down")
    g_w_upT = _sum_blocks(rv_up, "sum_w_up")
    grad_rpb_part = _grad_rpb(dbias)
    grad_x, dw_inT, s_n1 = _norm1_bwd(dq, dk, dv, da, dg, w_inT, x0, ctx0, h, dx1, modv, g_norm1)
    grad_x = grad_x[None]
    dfcw = jnp.concatenate([dcwg, dcwv], axis=1)
    dfcb = jnp.concatenate([dcbg[0], dcbv[0]])
    small = jnp.concatenate([dcw[:CW].reshape(CW, NDEV, n_cw).transpose(1, 0, 2).reshape(NDEV, CW * n_cw),
                             dfcw.reshape(3, NDEV, n_up).transpose(1, 0, 2).reshape(NDEV, 3 * n_up)], axis=1)
    small = jnp.pad(small.reshape(NDEV, 4, D), ((0, 0), (0, 12), (0, 0))).astype(bf16)
    dmod = jnp.concatenate([s_n1[0], s_n1[1], s_n2[3], s_n2[0], s_n2[1], s_loss[1]])
    dmodc = jnp.concatenate([s_n1[3], s_n1[4]])
    parts = [dmodc, s_n1[2], grad_rpb_part.reshape(-1), s_cf[0], s_cf[1], s_cf[2], s_n2[2], dfcb, s_loss[0], s_loss[3, 0:1]]
    sizes = [p.shape[0] for p in parts]
    pvec = _pad_rows128(jnp.concatenate([dmod] + parts))
    r_a, gath, rv_out = _reduce_scatter2(jnp.concatenate([dw_inT.reshape(NDEV, n_in, D), small], axis=1), [pvec],
                                         [dw_out.reshape(NDEV, n_out, D)], "rs_w_in")
    g_w_out = _sum_blocks(rv_out, "sum_w_out")
    g_w_inT = r_a[:n_in]
    sm = r_a[n_in:n_in + 4].reshape(-1)
    g_conv_w = sm[:CW * n_cw].reshape(CW, n_cw)
    g_fcw = sm[CW * n_cw:].reshape(3, n_up)
    tot = _sum_rows8(gath, "sum_small").reshape(-1)
    dmod_all = gath.reshape(NDEV, -1)[:, :6 * D]
    offs = np.cumsum([6 * D] + sizes)
    pick = lambda j: tot[offs[j]:offs[j + 1]]
    dmodc_t = jnp.pad(pick(0), (0, 4 * D))
    g_b_mod = (tot[:6 * D] + dmodc_t)[None, :]
    g_g_norm1 = pick(1)[None, :]
    g_rpb = pick(2).reshape(1, NH, 2 * WR - 1, 2 * NCOL - 1)
    g_conv_b, g_ln_g, g_ln_b = pick(3)[None, :], pick(4)[None, :], pick(5)[None, :]
    g_g_norm2 = pick(6)[None, :]
    g_fcb = pick(7)[None, :]
    g_g_final = pick(8)
    loss = pick(9)[0]
    dm_rows = jnp.concatenate([dmod_all, dmodc_t[None, :], jnp.zeros((7, 6 * D), f32)], axis=0)
    dm_sh = lax.dynamic_slice(dm_rows, (0, me * nmod), (16, nmod))
    g_w_mod, gc_part = _mod_bwd(cvec, dm_sh, w_mod[0])
    gc_sum = _sum_rows8(_small_allgather(gc_part[0].reshape(8, 128), "ag_cctx"), "sum_cctx").reshape(D)
    sg_c = _sigmoid(c_ctx)
    g_c_ctx = gc_sum * (sg_c * (1.0 + c_ctx * (1.0 - sg_c)))

    big = [("w_mod", w_mod[0], g_w_mod, m_w_mod[0], v_w_mod[0]), ("w_in", w_in[0].T, g_w_inT, m_w_in[0].T, v_w_in[0].T),
           ("w_out", w_out[0], g_w_out, m_w_out[0], v_w_out[0]), ("w_up", w_up[0].T, g_w_upT, m_w_up[0].T, v_w_up[0].T),
           ("w_down", w_down[0], g_w_down, m_w_down[0], v_w_down[0])]
    upd = {n: _adamw(wv, gv, mv, vv, "adamw_" + n) for n, wv, gv, mv, vv in big}
    for n in ("w_in", "w_up"):
        upd[n] = tuple(arr.T for arr in upd[n])
    g_w_in, g_w_up = g_w_inT.T, g_w_upT.T
    smalls = [("c_ctx", c_ctx, g_c_ctx, m_c_ctx, v_c_ctx), ("b_mod", b_mod, g_b_mod, m_b_mod, v_b_mod),
              ("g_norm1", g_norm1, g_g_norm1, m_g_norm1, v_g_norm1), ("rpb", rpb, g_rpb, m_rpb, v_rpb),
              ("conv_w", conv_w, g_conv_w[None], m_conv_w, v_conv_w), ("conv_b", conv_b, g_conv_b, m_conv_b, v_conv_b),
              ("ln_g", ln_g, g_ln_g, m_ln_g, v_ln_g), ("ln_b", ln_b, g_ln_b, m_ln_b, v_ln_b),
              ("g_norm2", g_norm2, g_g_norm2, m_g_norm2, v_g_norm2),
              ("ffn_conv_w", ffn_conv_w, g_fcw[None], m_ffn_conv_w, v_ffn_conv_w),
              ("ffn_conv_b", ffn_conv_b, g_fcb, m_ffn_conv_b, v_ffn_conv_b), ("g_final", g_final, g_g_final, m_g_final, v_g_final)]
    packed = [_pad_rows128(jnp.concatenate([t[j].reshape(-1) for t in smalls])) for j in (1, 2, 3, 4)]
    sd, sm_, sv = _adamw(*packed, "adamw_small")
    so = np.cumsum([0] + [int(np.prod(t[1].shape)) for t in smalls])
    for j, t in enumerate(smalls):
        shp = t[1].shape
        upd[t[0]] = tuple(arr.reshape(-1)[so[j]:so[j + 1]].reshape(shp) for arr in (sd, sm_, sv))
    grads = {"c_ctx": g_c_ctx, "w_mod": g_w_mod[None], "b_mod": g_b_mod, "g_norm1": g_g_norm1, "w_in": g_w_in[None],
             "rpb": g_rpb, "conv_w": g_conv_w[None], "conv_b": g_conv_b, "ln_g": g_ln_g, "ln_b": g_ln_b,
             "w_out": g_w_out[None], "g_norm2": g_g_norm2, "w_up": g_w_up[None], "ffn_conv_w": g_fcw[None],
             "ffn_conv_b": g_fcb, "w_down": g_w_down[None], "g_final": g_g_final}
    names = ["c_ctx", "w_mod", "b_mod", "g_norm1", "w_in", "rpb", "conv_w", "conv_b", "ln_g", "ln_b", "w_out", "g_norm2",
             "w_up", "ffn_conv_w", "ffn_conv_b", "w_down", "g_final"]
    shapes = {n: grads[n].shape for n in names}
    outs = [loss, grad_x] + [grads[n] for n in names]
    for j in range(3):
        outs += [upd[n][j].reshape(shapes[n]) for n in names]
    return tuple(outs)
```

```python
import functools

import numpy as np
import jax
import jax.numpy as jnp
from jax import lax
from jax.experimental import pallas as pl
from jax.experimental.pallas import tpu as pltpu

f32 = jnp.float32
bf16 = jnp.bfloat16

D = 1024
T = 4096
TC = 256
TA = T + TC
DA = 512
NH = 8
HD = 64
GW = 64
WR = 8
NCOL = 16
F = 2816
F2 = 2 * F
CW = 31
NDEV = 8
EPS = 1e-6
SCALE = HD ** -0.5
NEG = -1e30
MESH = pl.DeviceIdType.MESH

NT = (((1,), (1,)), ((), ()))
TN = (((0,), (0,)), ((), ()))

ADAM_LR, ADAM_B1, ADAM_B2, ADAM_EPS, ADAM_WD, ADAM_STEP = 0.001, 0.9, 0.999, 1e-08, 0.01, 10

VMEM_LIMIT = 56 * 1024 * 1024


def _cp(*sem):
    return pltpu.CompilerParams(dimension_semantics=sem or None, vmem_limit_bytes=VMEM_LIMIT)


def _dot(a, b):
    return jnp.dot(a, b, preferred_element_type=f32)


def _dg(a, b, dims):
    return lax.dot_general(a, b, dims, preferred_element_type=f32)


def _sigmoid(x):
    return 1.0 / (1.0 + jnp.exp(-x))


def _full(shape):
    n = len(shape)
    return pl.BlockSpec(shape, lambda *_: (0,) * n)


def _resident(shape):
    n = len(shape)
    return pl.BlockSpec(shape, lambda *_: (0,) * n, pipeline_mode=pl.Buffered(1))


def _my_pos():
    return lax.axis_index("x"), lax.axis_index("y"), lax.axis_index("c")


def _small_gather_plan(v_ref, out_ref, send_sems, recv_sems):
    x, y, c = _my_pos()
    me = 4 * x + 2 * y + c
    peers = []
    for k in range(1, NDEV):
        kx, ky, kc = (k >> 2) & 1, (k >> 1) & 1, k & 1
        peers.append((x ^ kx, y ^ ky, c ^ kc))

    def copy(k, slot, to):
        return pltpu.make_async_remote_copy(
            src_ref=v_ref, dst_ref=out_ref.at[slot], send_sem=send_sems.at[k], recv_sem=recv_sems.at[k],
            device_id=to, device_id_type=MESH)

    def start():
        out_ref[me] = v_ref[...]
        for k, p in enumerate(peers):
            copy(k, me, p).start()

    def finish():
        for k, (px, py, pc) in enumerate(peers):
            copy(k, 4 * px + 2 * py + pc, (x, y, c)).wait_recv()
        for k, p in enumerate(peers):
            copy(k, me, p).wait_send()

    return start, finish


def _small_gather_scratch():
    return [pltpu.SemaphoreType.DMA((NDEV - 1,)), pltpu.SemaphoreType.DMA((NDEV - 1,))]


def _small_allgather(v, name):
    n = v.shape[0]

    def body(v_ref, out_ref, send_sems, recv_sems):
        start, finish = _small_gather_plan(v_ref, out_ref, send_sems, recv_sems)
        start()
        finish()

    return pl.pallas_call(
        body, name=name,
        out_shape=jax.ShapeDtypeStruct((NDEV, n, 128), f32),
        in_specs=[pl.BlockSpec(memory_space=pltpu.VMEM)],
        out_specs=pl.BlockSpec(memory_space=pltpu.VMEM),
        scratch_shapes=_small_gather_scratch(),
    )(v)


def _ag2_plan(x_refs, out_refs, send_sems, recv_sems, local_sems):
    na = len(x_refs)
    x, y, c = _my_pos()
    me, sibling = (x, y, c), (x, y, 1 - c)
    chips = [(1 - x, y), (x, 1 - y), (1 - x, 1 - y)]

    def rows(i, px, py, pc):
        m_per = x_refs[i].shape[0]
        return out_refs[i].at[pl.ds(pl.multiple_of((4 * px + 2 * py + pc) * m_per, 16 if m_per % 16 == 0 else 8), m_per), :]

    def copies(k, block, to, from_shard=False):
        return [pltpu.make_async_remote_copy(
            src_ref=x_refs[i] if from_shard else rows(i, *block), dst_ref=rows(i, *block),
            send_sem=send_sems.at[k * na + i], recv_sem=recv_sems.at[k * na + i], device_id=to, device_id_type=MESH)
            for i in range(na)]

    def mine():
        return [pltpu.make_async_copy(x_refs[i], rows(i, *me), local_sems.at[i]) for i in range(na)]

    def first():
        cps = copies(0, me, sibling, True)
        for j, chip in enumerate(chips):
            cps += copies(1 + j, me, (*chip, c), True)
        return cps

    def start():
        for cp in mine() + first():
            cp.start()

    def forward():
        for j, chip in enumerate(chips):
            for cp in copies(1 + j, (*chip, c), me):
                cp.wait_recv()
            for cp in copies(4 + j, (*chip, c), sibling):
                cp.start()

    def finish():
        for cp in copies(0, sibling, me):
            cp.wait_recv()
        for j, chip in enumerate(chips):
            for cp in copies(4 + j, (*chip, 1 - c), me):
                cp.wait_recv()
        for cp in first():
            cp.wait_send()
        for j, chip in enumerate(chips):
            for cp in copies(4 + j, (*chip, c), sibling):
                cp.wait_send()
        for cp in mine():
            cp.wait()

    return start, forward, finish


def _ag2_scratch(na):
    return [pltpu.SemaphoreType.DMA((7 * na,)), pltpu.SemaphoreType.DMA((7 * na,)), pltpu.SemaphoreType.DMA((na,))]


def _a2a_plan(g_ref, recv_ref, send_sems, recv_sems, local_sem):
    x, y, c = _my_pos()
    me = 4 * x + 2 * y + c
    peers = []
    for k in range(1, NDEV):
        kx, ky, kc = (k >> 2) & 1, (k >> 1) & 1, k & 1
        peers.append((x ^ kx, y ^ ky, c ^ kc))

    def sends():
        return [pltpu.make_async_remote_copy(
            src_ref=g_ref.at[4 * px + 2 * py + pc], dst_ref=recv_ref.at[me], send_sem=send_sems.at[k], recv_sem=recv_sems.at[k],
            device_id=(px, py, pc), device_id_type=MESH) for k, (px, py, pc) in enumerate(peers)]

    def own():
        return pltpu.make_async_copy(g_ref.at[me], recv_ref.at[me], local_sem)

    def start():
        own().start()
        for cp in sends():
            cp.start()

    def finish():
        for k, (px, py, pc) in enumerate(peers):
            pltpu.make_async_remote_copy(
                src_ref=g_ref.at[me], dst_ref=recv_ref.at[4 * px + 2 * py + pc], send_sem=send_sems.at[k],
                recv_sem=recv_sems.at[k], device_id=(x, y, c), device_id_type=MESH).wait_recv()
        for cp in sends():
            cp.wait_send()
        own().wait()

    return start, finish


def _a2a_scratch():
    return [pltpu.SemaphoreType.DMA((NDEV - 1,)), pltpu.SemaphoreType.DMA((NDEV - 1,)), pltpu.SemaphoreType.DMA]


def _reduce_scatter2(g, small, name):
    _, r, n = g.shape
    ch = 16
    nch = r // ch
    ns = len(small)

    def body(g_ref, *rest):
        v_refs, out_ref, vout_refs = rest[:ns], rest[ns], rest[ns + 1:2 * ns + 1]
        a_ref, h_ref, b_ref, s1_send, s1_recv, s2_send, s2_recv = rest[2 * ns + 1:2 * ns + 8]
        gather_sems = rest[2 * ns + 8:]
        gathers = [_small_gather_plan(v_refs[i], vout_refs[i], *gather_sems[2 * i:2 * i + 2]) for i in range(ns)]
        for start, _ in gathers:
            start()
        x, y, c = _my_pos()
        sibling = (x, y, 1 - c)
        s1 = []
        for j in range(4):
            cp = pltpu.make_async_remote_copy(
                src_ref=g_ref.at[2 * j + (1 - c)], dst_ref=a_ref.at[j], send_sem=s1_send.at[j], recv_sem=s1_recv.at[j],
                device_id=sibling, device_id_type=MESH)
            cp.start()
            s1.append(cp)
        for cp in s1:
            cp.wait_recv()

        def add1(i, _):
            rr = pl.ds(pl.multiple_of(i * ch, ch), ch)
            for j in range(4):
                h_ref[j, rr, :] = (g_ref[2 * j + c, rr, :].astype(f32) + a_ref[j, rr, :].astype(f32)).astype(bf16)
            return 0
        lax.fori_loop(0, nch, add1, 0)
        mychip = 2 * x + y
        s2 = []
        for m in range(1, 4):
            mx, my_ = (m >> 1) & 1, m & 1
            px, py = x ^ mx, y ^ my_
            cp = pltpu.make_async_remote_copy(
                src_ref=h_ref.at[2 * px + py], dst_ref=b_ref.at[m - 1], send_sem=s2_send.at[m - 1], recv_sem=s2_recv.at[m - 1],
                device_id=(px, py, c), device_id_type=MESH)
            cp.start()
            s2.append(cp)
        for cp in s2:
            cp.wait_recv()

        def add2(i, _):
            rr = pl.ds(pl.multiple_of(i * ch, ch), ch)
            acc = h_ref[mychip, rr, :].astype(f32)
            for m in range(3):
                acc = acc + b_ref[m, rr, :].astype(f32)
            out_ref[rr, :] = acc
            return 0
        lax.fori_loop(0, nch, add2, 0)
        for cp in s1 + s2:
            cp.wait_send()
        for _, finish in gathers:
            finish()

    vmem = pl.BlockSpec(memory_space=pltpu.VMEM)
    return pl.pallas_call(
        body, name=name,
        out_shape=[jax.ShapeDtypeStruct((r, n), f32)] + [jax.ShapeDtypeStruct((NDEV,) + v.shape, v.dtype) for v in small],
        in_specs=[vmem] * (1 + ns),
        out_specs=[vmem] * (1 + ns),
        scratch_shapes=[pltpu.VMEM((4, r, n), bf16), pltpu.VMEM((4, r, n), bf16), pltpu.VMEM((3, r, n), bf16),
                        pltpu.SemaphoreType.DMA((4,)), pltpu.SemaphoreType.DMA((4,)),
                        pltpu.SemaphoreType.DMA((3,)), pltpu.SemaphoreType.DMA((3,))] + _small_gather_scratch() * ns,
        compiler_params=pltpu.CompilerParams(vmem_limit_bytes=VMEM_LIMIT),
    )(g, *small)


def _head(w_in_sh, c8, cctx8, w_mod_sh, b_sh, convpay):
    nmod = w_mod_sh.shape[1]
    npay = convpay.shape[0]

    def body(w_ref, c_ref, cc_ref, wm_ref, b_ref, pay_ref, wout_ref, call_ref, mall_ref, pall_ref, mod_s,
             ag_send, ag_recv, ag_local, c_send, c_recv, m_send, m_recv, p_send, p_recv):
        start, forward, finish = _ag2_plan([w_ref], [wout_ref], ag_send, ag_recv, ag_local)
        c_start, c_finish = _small_gather_plan(c_ref, call_ref, c_send, c_recv)
        m_start, m_finish = _small_gather_plan(mod_s, mall_ref, m_send, m_recv)
        p_start, p_finish = _small_gather_plan(pay_ref, pall_ref, p_send, p_recv)
        c_start()
        start()
        p_start()
        c_finish()
        acc = jnp.zeros((16, nmod), f32)
        for j in range(D // 128):
            rows = jnp.concatenate([call_ref[:, j, :], cc_ref[j:j + 1, :], jnp.zeros((7, 128), f32)], axis=0)
            act = (rows * _sigmoid(rows)).astype(bf16)
            acc = acc + _dot(act, wm_ref[j * 128:(j + 1) * 128, :].astype(bf16))
        mod_s[...] = acc + b_ref[...]
        m_start()
        forward()
        finish()
        m_finish()
        p_finish()

    vmem = pl.BlockSpec(memory_space=pltpu.VMEM)
    return pl.pallas_call(
        body, name="head",
        out_shape=[jax.ShapeDtypeStruct((NDEV * w_in_sh.shape[0], D), bf16), jax.ShapeDtypeStruct((NDEV, 8, 128), f32),
                   jax.ShapeDtypeStruct((NDEV, 16, nmod), f32), jax.ShapeDtypeStruct((NDEV, npay, 128), f32)],
        in_specs=[vmem] * 6, out_specs=[vmem] * 4,
        scratch_shapes=[pltpu.VMEM((16, nmod), f32)] + _ag2_scratch(1) + _small_gather_scratch() * 3,
        compiler_params=pltpu.CompilerParams(vmem_limit_bytes=VMEM_LIMIT),
    )(w_in_sh, c8, cctx8, w_mod_sh, b_sh, convpay)


def _mod_bwd(cvec, dm_sh, w_sh):
    def body(c_ref, dm_ref, w_ref, gw_ref, gc_ref):
        cv = c_ref[...]
        act = (cv * _sigmoid(cv)).astype(bf16)
        gw_ref[...] = _dg(act, dm_ref[...].astype(bf16), TN)
        gc_ref[...] = _dg(dm_ref[8:16, :].astype(bf16), w_ref[...].astype(bf16), NT)
    return pl.pallas_call(
        body, name="mod_bwd",
        out_shape=(jax.ShapeDtypeStruct(w_sh.shape, f32), jax.ShapeDtypeStruct((8, D), f32)))(cvec, dm_sh, w_sh)


def _sum_rows8(a, name):
    n = a.shape[1]

    def body(a_ref, o_ref):
        acc = a_ref[0]
        for d in range(1, NDEV):
            acc = acc + a_ref[d]
        o_ref[...] = acc
    return pl.pallas_call(body, name=name, out_shape=jax.ShapeDtypeStruct((n, 128), f32))(a)


def _in_proj(x0, ctx0, g1, modv, w_inT, shards):
    tm = 256
    nt = TA // tm
    nx = T // tm
    na = len(shards)

    def body(x_ref, c_ref, g_ref, mod_ref, w_ref, *rest):
        x_refs, (h_ref, q_ref, k_ref, v_ref, a_ref, gg_ref) = rest[:na], rest[na:na + 6]
        out_refs, sems = rest[na + 6:2 * na + 6], rest[2 * na + 6:]
        i = pl.program_id(0)
        if na:
            start, forward, finish = _ag2_plan(x_refs, out_refs, *sems)
            pl.when(i == 0)(start)
            pl.when(i == nt - 2)(forward)
        is_ctx = i == nt - 1
        xv = jnp.where(is_ctx, c_ref[...], x_ref[...])
        rstd = lax.rsqrt(jnp.mean(xv * xv, axis=-1, keepdims=True) + EPS)
        sh = jnp.where(is_ctx, mod_ref[6:7, :], mod_ref[0:1, :])
        sc = jnp.where(is_ctx, mod_ref[7:8, :], mod_ref[1:2, :])
        h = ((xv * rstd * g_ref[...]) * (1.0 + sc) + sh).astype(bf16)
        h_ref[...] = h
        for j, o_ref in enumerate((q_ref, k_ref, v_ref, a_ref, gg_ref)):
            o_ref[...] = _dg(h, w_ref[j * DA:(j + 1) * DA, :], NT).astype(o_ref.dtype)
        if na:
            pl.when(is_ctx)(finish)

    row = lambda w: pl.BlockSpec((tm, w), lambda i: (i, 0))
    hbm = pl.BlockSpec(memory_space=pl.ANY)
    return pl.pallas_call(
        body, name="in_proj", grid=(nt,),
        in_specs=[pl.BlockSpec((tm, D), lambda i: (jnp.minimum(i, nx - 1), 0)), _full((TC, D)),
                  _full((1, D)), _full((8, D)), _full((5 * DA, D))] + [hbm] * na,
        out_specs=[row(D), row(DA), row(DA), row(DA), row(DA), row(DA)] + [hbm] * na,
        out_shape=[jax.ShapeDtypeStruct((TA, D), bf16)] + [jax.ShapeDtypeStruct((TA, DA), bf16)] * 3
                  + [jax.ShapeDtypeStruct((TA, DA), f32)] * 2
                  + [jax.ShapeDtypeStruct((NDEV * sh.shape[0], sh.shape[1]), sh.dtype) for sh in shards],
        scratch_shapes=_ag2_scratch(na) if na else [],
        compiler_params=_cp("arbitrary"),
    )(x0, ctx0, g1, modv, w_inT, *shards)


def _win_start(r):
    return jnp.clip(r - WR // 2, 0, GW - WR)


def _pattern(r):
    return _win_start(r) - r + (WR - 1)


def _bias_table(rpb):
    qc = np.arange(GW)[:, None]
    kc = np.arange(GW)[None, :]
    cs = np.clip(qc - NCOL // 2, 0, GW - NCOL)
    valid = np.tile(((kc >= cs) & (kc < cs + NCOL)).astype(np.int32), (1, WR))
    pad = jnp.pad(rpb, ((0, 0), (0, 0), (0, GW - (2 * NCOL - 1))))
    base = jnp.stack([pad[:, p:p + WR, :].reshape(NH, WR * GW) for p in range(8)])

    def body(base_ref, valid_ref, o_ref):
        ok = valid_ref[...] != 0
        for h in range(NH):
            row = jnp.broadcast_to(base_ref[0, h:h + 1, :], (GW, WR * GW))
            skew = pltpu.roll(row, WR * GW - (NCOL - 1), 1, stride=1, stride_axis=0)
            o_ref[0, h] = jnp.where(ok, skew, NEG)

    return pl.pallas_call(
        body, name="bias_table", grid=(8,),
        in_specs=[pl.BlockSpec((1, NH, WR * GW), lambda p: (p, 0, 0)), _full((GW, WR * GW))],
        out_specs=pl.BlockSpec((1, NH, GW, WR * GW), lambda p: (p, 0, 0, 0)),
        out_shape=jax.ShapeDtypeStruct((8, NH, GW, WR * GW), f32),
        compiler_params=_cp("parallel"),
    )(base, jnp.asarray(valid))


def _rpb_tables():
    lane_map = np.zeros((WR * GW, WR, 2 * NCOL - 1), np.float32)
    for i in range(WR):
        for t in range(GW):
            if t >= GW - NCOL:
                lane_map[i * GW + t, i, t - (GW - NCOL)] = 1.0
            elif t < NCOL - 1:
                lane_map[i * GW + t, (i - 1) % WR, t + NCOL] = 1.0
    p = np.arange(8)[:, None]
    i = np.arange(WR)[None, :]
    r_hot = ((p + i)[:, :, None] == np.arange(2 * WR - 1)[None, None, :]).astype(np.float32)
    return lane_map, r_hot


AG_FORWARD_ROW = 58


def _stack_pair(x2, lo):
    z = jnp.zeros_like(x2)
    return jnp.concatenate([jnp.where(lo, x2, z), jnp.where(lo, z, x2)], axis=0)


def _attn_fwd(q, k, v, bias_tab, shards):
    na = len(shards)

    def body(q_ref, k_ref, v_ref, b_ref, *rest):
        x_refs, (y_ref, lse_ref), out_refs, sems = rest[:na], rest[na:na + 2], rest[na + 2:2 * na + 2], rest[2 * na + 2:]
        r = pl.program_id(0)
        if na:
            start, forward, finish = _ag2_plan(x_refs, out_refs, *sems)
            pl.when(r == 0)(start)
            pl.when(r == AG_FORWARD_ROW)(forward)
        ks = pl.multiple_of(_win_start(r) * GW, GW)
        qq = q_ref[...]
        lo = lax.broadcasted_iota(jnp.int32, (GW, 2 * HD), 1) < HD
        kv, scores = [], []
        for pr in range(NH // 2):
            ps = slice(pr * 2 * HD, (pr + 1) * 2 * HD)
            qst = _stack_pair(qq[:, ps], lo)
            kw, kc = k_ref[pl.ds(ks, WR * GW), ps], k_ref[T:TA, ps]
            kv.append((v_ref[pl.ds(ks, WR * GW), ps], v_ref[T:TA, ps]))
            bias2 = b_ref[0, 2 * pr:2 * pr + 2].reshape(2 * GW, WR * GW)
            scores.append((_dg(qst, kw, NT) * SCALE + bias2, _dg(qst, kc, NT) * SCALE))
        probs = []
        for pr, (sl, sc) in enumerate(scores):
            m = jnp.maximum(jnp.max(sl, axis=-1, keepdims=True), jnp.max(sc, axis=-1, keepdims=True))
            pl_ = jnp.exp(sl - m)
            pc = jnp.exp(sc - m)
            l = jnp.sum(pl_, axis=-1, keepdims=True) + jnp.sum(pc, axis=-1, keepdims=True)
            lse = m + jnp.log(l)
            lse_ref[:, 2 * pr:2 * pr + 1] = lse[0:GW]
            lse_ref[:, 2 * pr + 1:2 * pr + 2] = lse[GW:]
            probs.append((pl_.astype(bf16), pc.astype(bf16), 1.0 / l))
        for pr in range(NH // 2):
            ps = slice(pr * 2 * HD, (pr + 1) * 2 * HD)
            vw, vc = kv[pr]
            pb, cb, rl = probs[pr]
            o = (_dot(pb, vw) + _dot(cb, vc)) * rl
            y_ref[:, ps] = jnp.where(lo, o[0:GW], o[GW:]).astype(bf16)
        if na:
            pl.when(r == GW - 1)(finish)

    hbm = pl.BlockSpec(memory_space=pl.ANY)
    return pl.pallas_call(
        body, name="attn_fwd", grid=(GW,),
        in_specs=[pl.BlockSpec((GW, DA), lambda r: (r, 0)), _full((TA, DA)), _full((TA, DA)),
                  pl.BlockSpec((1, NH, GW, WR * GW), lambda r: (_pattern(r), 0, 0, 0))] + [hbm] * na,
        out_specs=[pl.BlockSpec((GW, DA), lambda r: (r, 0)), pl.BlockSpec((GW, NH), lambda r: (r, 0))] + [hbm] * na,
        out_shape=[jax.ShapeDtypeStruct((T, DA), bf16), jax.ShapeDtypeStruct((T, NH), f32)]
                  + [jax.ShapeDtypeStruct((NDEV * s.shape[0], s.shape[1]), s.dtype) for s in shards],
        scratch_shapes=_ag2_scratch(na) if na else [],
        compiler_params=_cp("arbitrary"),
    )(q, k, v, bias_tab, *shards)


CONV_TT = 512
HALO = 16


def _halo_specs(tt, w, nrows_blocks):
    per = tt // HALO
    prev = pl.BlockSpec((HALO, w), lambda i: (jnp.maximum(i * per - 1, 0), 0))
    cur = pl.BlockSpec((tt, w), lambda i: (i, 0))
    nxt = pl.BlockSpec((HALO, w), lambda i: (jnp.minimum((i + 1) * per, nrows_blocks - 1), 0))
    return [prev, cur, nxt]


def _shifted_copies(rot, wn):
    for b in range(1, 8):
        rot[b, 0:wn - 8, :] = rot[0, pl.ds(b, wn - 8), :]


def _conf_fwd(a, g, conv_w, conv_b, ln_g, ln_b, shards):
    tt = CONV_TT
    nt = T // tt
    sub = 32
    wn = tt + 2 * HALO
    na = len(shards)

    def body(ap, ac, an, gp, gc, gn, w_ref, b_ref, lg_ref, lb_ref, *rest):
        x_refs, (y_ref, cv_ref), out_refs, (rot, *sems) = rest[:na], rest[na:na + 2], rest[na + 2:2 * na + 2], rest[2 * na + 2:]
        i = pl.program_id(0)
        if na:
            start, forward, finish = _ag2_plan(x_refs, out_refs, *sems)
            pl.when(i == 0)(start)
            pl.when(i == nt // 2)(forward)
        rot[0, 0:HALO, :] = jnp.where(i > 0, ap[...] * _sigmoid(gp[...]), 0.0)
        rot[0, HALO:HALO + tt, :] = ac[...] * _sigmoid(gc[...])
        rot[0, HALO + tt:, :] = jnp.where(i < nt - 1, an[...] * _sigmoid(gn[...]), 0.0)
        _shifted_copies(rot, wn)
        w = w_ref[...]
        for s in range(tt // sub):
            acc = jnp.zeros((sub, DA), f32)
            for j in range(CW):
                a8, b8 = divmod(1 + j, 8)
                acc = acc + rot[b8, pl.ds(s * sub + 8 * a8, sub), :] * w[j:j + 1, :]
            cv = acc + b_ref[...]
            cv_ref[pl.ds(s * sub, sub), :] = cv
            mu = jnp.mean(cv, axis=-1, keepdims=True)
            xc = cv - mu
            rstd = lax.rsqrt(jnp.mean(xc * xc, axis=-1, keepdims=True) + EPS)
            z = xc * rstd * lg_ref[...] + lb_ref[...]
            y_ref[pl.ds(s * sub, sub), :] = (z * _sigmoid(z)).astype(bf16)
        if na:
            pl.when(i == nt - 1)(finish)

    hs = _halo_specs(tt, DA, T // HALO)
    hbm = pl.BlockSpec(memory_space=pl.ANY)
    return pl.pallas_call(
        body, name="conf_fwd", grid=(nt,),
        in_specs=hs + hs + [_full((CW, DA)), _full((1, DA)), _full((1, DA)), _full((1, DA))] + [hbm] * na,
        out_specs=[pl.BlockSpec((tt, DA), lambda i: (i, 0)), pl.BlockSpec((tt, DA), lambda i: (i, 0))] + [hbm] * na,
        out_shape=[jax.ShapeDtypeStruct((T, DA), bf16), jax.ShapeDtypeStruct((T, DA), f32)]
                  + [jax.ShapeDtypeStruct((NDEV * s.shape[0], s.shape[1]), s.dtype) for s in shards],
        scratch_shapes=[pltpu.VMEM((8, wn, DA), f32)] + (_ag2_scratch(na) if na else []),
        compiler_params=_cp("arbitrary"),
    )(a, a, a, g, g, g, conv_w, conv_b, ln_g, ln_b, *shards)


def _out_proj(xa, y_na, y_cv, w_out, modv, g2):
    tm = 512

    def body(x_ref, ya_ref, yc_ref, w_ref, mod_ref, g_ref, x1_ref, pj_ref, h2_ref):
        proj = _dot(ya_ref[...], w_ref[0:DA, :]) + _dot(yc_ref[...], w_ref[DA:D, :])
        x1 = x_ref[...] + mod_ref[2:3, :] * proj
        x1_ref[...] = x1
        pj_ref[...] = proj.astype(bf16)
        rstd = lax.rsqrt(jnp.mean(x1 * x1, axis=-1, keepdims=True) + EPS)
        h2_ref[...] = ((x1 * rstd * g_ref[...]) * (1.0 + mod_ref[4:5, :]) + mod_ref[3:4, :]).astype(bf16)

    row = lambda w: pl.BlockSpec((tm, w), lambda i: (i, 0))
    return pl.pallas_call(
        body, name="out_proj", grid=(T // tm,),
        in_specs=[row(D), row(DA), row(DA), _full((D, D)), _full((8, D)), _full((1, D))],
        out_specs=[row(D), row(D), row(D)],
        out_shape=[jax.ShapeDtypeStruct((T, D), f32), jax.ShapeDtypeStruct((T, D), bf16), jax.ShapeDtypeStruct((T, D), bf16)],
        compiler_params=_cp("parallel"),
    )(xa, y_na, y_cv, w_out, modv, g2)


FFN_TT = 2048
FFN_CT = 256
FFN_NC = F // FFN_CT
FFN_SUB = 32


def _row_neighbours(ref, r, n):
    blk = ref[pl.ds(r - 8, n + 16), :]
    return blk[8:8 + n, :], pltpu.roll(blk, 1, 0)[8:8 + n, :], pltpu.roll(blk, n + 15, 0)[8:8 + n, :]


def _ffn_specs(tt, ct, by_token_first):
    tc = (lambda f: (lambda t, c: f(t, c))) if by_token_first else (lambda f: (lambda c, t: f(t, c)))
    per = tt // HALO
    halo = [pl.BlockSpec((HALO, D), tc(lambda t, c: (jnp.maximum(t * per - 1, 0), 0))),
            pl.BlockSpec((tt, D), tc(lambda t, c: (t, 0))),
            pl.BlockSpec((HALO, D), tc(lambda t, c: (jnp.minimum((t + 1) * per, T // HALO - 1), 0)))]
    weights = [pl.BlockSpec((ct, D), tc(lambda t, c: (c, 0))), pl.BlockSpec((ct, D), tc(lambda t, c: (c + FFN_NC, 0))),
               pl.BlockSpec((3, ct), tc(lambda t, c: (0, c))), pl.BlockSpec((3, ct), tc(lambda t, c: (0, c + FFN_NC))),
               pl.BlockSpec((1, ct), tc(lambda t, c: (0, c))), pl.BlockSpec((1, ct), tc(lambda t, c: (0, c + FFN_NC))),
               pl.BlockSpec((ct, D), tc(lambda t, c: (c, 0)))]
    return halo, weights


def _ffn_fwd(h2, w_upT, fcw, fcb, w_down):
    tt, ct = FFN_TT, FFN_CT
    nt = T // tt
    wn = tt + 2 * HALO
    half = tt // 2

    def body(hp, hc, hn, wg_ref, wv_ref, cwg_ref, cwv_ref, cbg_ref, cbv_ref, wd_ref, o_ref, u_ref, u2_ref, hwin, uwin, act):
        t = pl.program_id(0)
        c = pl.program_id(1)

        @pl.when(c == 0)
        def _():
            hwin[0:HALO, :] = jnp.where(t > 0, hp[...], jnp.zeros_like(hp[...]))
            hwin[HALO:HALO + tt, :] = hc[...]
            hwin[HALO + tt:, :] = jnp.where(t < nt - 1, hn[...], jnp.zeros_like(hn[...]))
            o_ref[...] = jnp.zeros_like(o_ref)

        for r0, r1 in ((0, half + 2 * HALO), (half + 2 * HALO, wn)):
            hw = hwin[r0:r1, :]
            uwin[r0:r1, :ct] = _dg(hw, wg_ref[...], NT)
            uwin[r0:r1, ct:] = _dg(hw, wv_ref[...], NT)
        cw = jnp.concatenate([cwg_ref[...], cwv_ref[...]], axis=1)
        cb = jnp.concatenate([cbg_ref[...], cbv_ref[...]], axis=1)
        for p in range(2):
            for r in range(p * half, (p + 1) * half, FFN_SUB):
                uc, prev, nxt = _row_neighbours(uwin, HALO + r, FFN_SUB)
                u2 = prev * cw[0:1, :] + uc * cw[1:2, :] + nxt * cw[2:3, :] + cb
                u_ref[r:r + FFN_SUB, :] = uc.astype(bf16)
                u2_ref[r:r + FFN_SUB, :] = u2
                gate = u2[:, :ct]
                act[r:r + FFN_SUB, :] = (gate * _sigmoid(gate) * u2[:, ct:]).astype(bf16)
            rows = slice(p * half, (p + 1) * half)
            o_ref[rows, :] += _dot(act[rows, :], wd_ref[...])

    halo, weights = _ffn_specs(tt, ct, True)
    pair = pl.BlockSpec((tt, 2 * ct), lambda t, c: (t, c))
    return pl.pallas_call(
        body, name="ffn_fwd", grid=(nt, FFN_NC),
        in_specs=halo + weights,
        out_specs=[pl.BlockSpec((tt, D), lambda t, c: (t, 0)), pair, pair],
        out_shape=[jax.ShapeDtypeStruct((T, D), f32), jax.ShapeDtypeStruct((T, F2), bf16), jax.ShapeDtypeStruct((T, F2), f32)],
        scratch_shapes=[pltpu.VMEM((wn, D), bf16), pltpu.VMEM((wn, 2 * ct), f32), pltpu.VMEM((tt, ct), bf16)],
        compiler_params=_cp("parallel", "arbitrary"),
    )(h2, h2, h2, w_upT, w_upT, fcw, fcw, fcb, fcb, w_down)


def _loss_bwd(ffn, x1, tgt, modv, gf):
    tm = 1024
    nt = T // tm

    def body(f_ref, x1_ref, t_ref, mod_ref, g_ref, dx2_ref, df_ref, s_ref):
        i = pl.program_id(0)

        @pl.when(i == 0)
        def _():
            s_ref[...] = jnp.zeros_like(s_ref)

        ff = f_ref[...]
        gt2 = mod_ref[5:6, :]
        x2 = x1_ref[...] + gt2 * ff
        rstd = lax.rsqrt(jnp.mean(x2 * x2, axis=-1, keepdims=True) + EPS)
        xh = x2 * rstd
        gfv = g_ref[...]
        e = xh * gfv - t_ref[...]
        dy = e * (1.0 / D)
        dxh = dy * gfv
        dx2 = rstd * (dxh - xh * jnp.mean(dxh * xh, axis=-1, keepdims=True))
        dx2_ref[...] = dx2
        df_ref[...] = (dx2 * gt2).astype(bf16)
        s_ref[0:1, :] += jnp.sum(dy * xh, axis=0, keepdims=True)
        s_ref[1:2, :] += jnp.sum(dx2 * ff, axis=0, keepdims=True)
        s_ref[2:3, :] += jnp.sum(e * e, axis=0, keepdims=True)

        @pl.when(i == nt - 1)
        def _():
            tot = jnp.sum(s_ref[2:3, :], axis=-1, keepdims=True) * (0.5 / D)
            s_ref[3:4, :] = jnp.broadcast_to(tot, (1, D))

    row = lambda: pl.BlockSpec((tm, D), lambda i: (i, 0))
    return pl.pallas_call(
        body, name="loss_bwd", grid=(nt,),
        in_specs=[row(), row(), row(), _full((8, D)), _full((1, D))],
        out_specs=[row(), row(), _full((8, D))],
        out_shape=[jax.ShapeDtypeStruct((T, D), f32), jax.ShapeDtypeStruct((T, D), bf16), jax.ShapeDtypeStruct((8, D), f32)],
        compiler_params=_cp("arbitrary"),
    )(ffn, x1, tgt, modv, gf)


def _ffn_bwd(h2, dffn, u_t, u2_t, fcw, w_down):
    tt, ct = FFN_TT, FFN_CT
    nt = T // tt
    wn = tt + 2 * HALO
    half = tt // 2

    def body(dp, dc, dn, hc, uc_ref, u2p, u2c, u2n, cwg_ref, cwv_ref, wd_ref,
             dug_ref, duv_ref, dwu_ref, dwd_ref, dcwg_ref, dcwv_ref, dcbg_ref, dcbv_ref,
             dwin, d2win, dawin, accu, accd, act, du):
        t = pl.program_id(1)
        first, last = t == 0, t == nt - 1
        zero = jnp.zeros((HALO, D), bf16)
        dwin[0:HALO, :] = jnp.where(first, zero, dp[...])
        dwin[HALO:HALO + tt, :] = dc[...]
        dwin[HALO + tt:, :] = jnp.where(last, zero, dn[...])

        @pl.when(first)
        def _():
            for r in (accu, accd, dcwg_ref, dcwv_ref, dcbg_ref, dcbv_ref):
                r[...] = jnp.zeros_like(r)

        cw = jnp.concatenate([cwg_ref[...], cwv_ref[...]], axis=1)
        split = half + 2 * HALO
        for r0, r1 in ((0, split), (split, wn)):
            dawin[r0:r1, :] = _dg(dwin[r0:r1, :], wd_ref[...], NT)

        def grads(u2v, dact):
            gate, val = u2v[:, :ct], u2v[:, ct:]
            sg = _sigmoid(gate)
            silu = gate * sg
            return dact * val * (sg * (1.0 + gate * (1.0 - sg))), dact * silu, silu * val

        for blk, r0 in ((u2p, 0), (u2n, HALO + tt)):
            dgate, dval, _ = grads(blk[...], dawin[r0:r0 + HALO, :])
            d2win[r0:r0 + HALO, :ct] = dgate
            d2win[r0:r0 + HALO, ct:] = dval
        for p in range(2):
            rows = slice(p * half, (p + 1) * half)
            for r in range(p * half, (p + 1) * half, FFN_SUB):
                dgate, dval, av = grads(u2c[r:r + FFN_SUB, :], dawin[HALO + r:HALO + r + FFN_SUB, :])
                d2win[HALO + r:HALO + r + FFN_SUB, :ct] = dgate
                d2win[HALO + r:HALO + r + FFN_SUB, ct:] = dval
                act[r:r + FFN_SUB, :] = av.astype(bf16)
            accd[...] += _dg(act[rows, :], dc[rows, :], TN)

        def fold8(x):
            out = x[0:8]
            for k in range(8, FFN_SUB, 8):
                out = out + x[k:k + 8]
            return out

        for p in range(2):
            rows = slice(p * half, (p + 1) * half)
            sums = [jnp.zeros((8, 2 * ct), f32) for _ in range(4)]
            for r in range(p * half, (p + 1) * half, FFN_SUB):
                d2c, d2m, d2p = _row_neighbours(d2win, HALO + r, FFN_SUB)
                ucur = uc_ref[r:r + FFN_SUB, :].astype(f32)
                sums[0] = sums[0] + fold8(d2c)
                for kk, dd in enumerate((d2p, d2c, d2m)):
                    sums[1 + kk] = sums[1 + kk] + fold8(ucur * dd)
                du[r:r + FFN_SUB, :] = (d2p * cw[0:1, :] + d2c * cw[1:2, :] + d2m * cw[2:3, :]).astype(bf16)
            dcb = jnp.sum(sums[0], axis=0, keepdims=True)
            dcbg_ref[...] += dcb[:, :ct]
            dcbv_ref[...] += dcb[:, ct:]
            for kk in range(3):
                dck = jnp.sum(sums[1 + kk], axis=0, keepdims=True)
                dcwg_ref[kk:kk + 1, :] += dck[:, :ct]
                dcwv_ref[kk:kk + 1, :] += dck[:, ct:]
            dug_ref[rows, :] = du[rows, :ct]
            duv_ref[rows, :] = du[rows, ct:]
            accu[...] += _dg(du[rows, :], hc[rows, :], TN)

        @pl.when(last)
        def _():
            dwu_ref[0] = accu[0:ct, :].astype(bf16)
            dwu_ref[1] = accu[ct:, :].astype(bf16)
            dwd_ref[...] = accd[...].astype(bf16)

    per = tt // HALO
    prev = lambda w: (lambda c, t: (jnp.maximum(t * per - 1, 0), c if w else 0))
    nxt = lambda w: (lambda c, t: (jnp.minimum((t + 1) * per, T // HALO - 1), c if w else 0))
    tile = lambda: pl.BlockSpec((ct, D), lambda c, t: (c, 0))
    lane = lambda r, off: pl.BlockSpec((r, ct), lambda c, t: (0, c + off))
    return pl.pallas_call(
        body, name="ffn_bwd", grid=(FFN_NC, nt),
        in_specs=[pl.BlockSpec((HALO, D), prev(False)), pl.BlockSpec((tt, D), lambda c, t: (t, 0)), pl.BlockSpec((HALO, D), nxt(False)),
                  pl.BlockSpec((tt, D), lambda c, t: (t, 0)), pl.BlockSpec((tt, 2 * ct), lambda c, t: (t, c)),
                  pl.BlockSpec((HALO, 2 * ct), prev(True)), pl.BlockSpec((tt, 2 * ct), lambda c, t: (t, c)),
                  pl.BlockSpec((HALO, 2 * ct), nxt(True)), lane(3, 0), lane(3, FFN_NC), tile()],
        out_specs=[pl.BlockSpec((tt, ct), lambda c, t: (t, c)), pl.BlockSpec((tt, ct), lambda c, t: (t, c)),
                   pl.BlockSpec((2, ct, D), lambda c, t: (0, c, 0)), tile(), lane(3, 0), lane(3, 0), lane(1, 0), lane(1, 0)],
        out_shape=[jax.ShapeDtypeStruct((T, F), bf16), jax.ShapeDtypeStruct((T, F), bf16),
                   jax.ShapeDtypeStruct((2, F, D), bf16), jax.ShapeDtypeStruct((F, D), bf16),
                   jax.ShapeDtypeStruct((3, F), f32), jax.ShapeDtypeStruct((3, F), f32),
                   jax.ShapeDtypeStruct((1, F), f32), jax.ShapeDtypeStruct((1, F), f32)],
        scratch_shapes=[pltpu.VMEM((wn, D), bf16), pltpu.VMEM((wn, 2 * ct), f32), pltpu.VMEM((wn, ct), f32),
                        pltpu.VMEM((2 * ct, D), f32), pltpu.VMEM((ct, D), f32),
                        pltpu.VMEM((tt, ct), bf16), pltpu.VMEM((tt, 2 * ct), bf16)],
        compiler_params=_cp("parallel", "arbitrary"),
    )(dffn, dffn, dffn, h2, u_t, u2_t, u2_t, u2_t, fcw, fcw, w_down)


def _norm_bwd(dh, xv, gain, sh_sc, rstd):
    xh = xv * rstd
    n = xh * gain
    dn = dh * (1.0 + sh_sc)
    dxh = dn * gain
    dx = rstd * (dxh - xh * jnp.mean(dxh * xh, axis=-1, keepdims=True))
    return (dx, jnp.sum(dh, axis=0, keepdims=True), jnp.sum(dh * n, axis=0, keepdims=True),
            jnp.sum(dn * xh, axis=0, keepdims=True))


def _norm2_bwd(dug, duv, w_upT, x1, dx2, proj, w_out, y_na, y_cv, modv, g2):
    tm = 512
    nt = T // tm

    def body(dug_ref, duv_ref, w_ref, x1_ref, dx2_ref, pj_ref, wo_ref, ya_ref, yc_ref, mod_ref, g_ref,
             dx1_ref, dya_ref, dyc_ref, dwo_ref, s_ref, acc):
        i = pl.program_id(0)

        @pl.when(i == 0)
        def _():
            s_ref[...] = jnp.zeros_like(s_ref)
            acc[...] = jnp.zeros_like(acc)

        dh2 = _dot(dug_ref[...], w_ref[0:F, :]) + _dot(duv_ref[...], w_ref[F:F2, :])
        x1 = x1_ref[...]
        rstd = lax.rsqrt(jnp.mean(x1 * x1, axis=-1, keepdims=True) + EPS)
        dxn, dsh, dsc, dgn = _norm_bwd(dh2, x1, g_ref[...], mod_ref[4:5, :], rstd)
        dx1 = dx2_ref[...] + dxn
        dx1_ref[...] = dx1
        dpj = (dx1 * mod_ref[2:3, :]).astype(bf16)
        dyc = _dg(dpj, wo_ref[...], NT)
        dya_ref[...] = dyc[:, :DA].astype(bf16)
        dyc_ref[...] = dyc[:, DA:]
        acc[0:DA, :] += _dg(ya_ref[...], dpj, TN)
        acc[DA:D, :] += _dg(yc_ref[...], dpj, TN)

        @pl.when(i == nt - 1)
        def _():
            dwo_ref[...] = acc[...].astype(bf16)

        s_ref[0:1, :] += dsh
        s_ref[1:2, :] += dsc
        s_ref[2:3, :] += dgn
        s_ref[3:4, :] += jnp.sum(dx1 * pj_ref[...].astype(f32), axis=0, keepdims=True)

    row = lambda w: pl.BlockSpec((tm, w), lambda i: (i, 0))
    return pl.pallas_call(
        body, name="norm2_bwd", grid=(nt,),
        in_specs=[row(F), row(F), _resident((F2, D)), row(D), row(D), row(D), _resident((D, D)), row(DA), row(DA),
                  _full((8, D)), _full((1, D))],
        out_specs=[row(D), row(DA), row(DA), _full((D, D)), _full((8, D))],
        out_shape=[jax.ShapeDtypeStruct((T, D), f32), jax.ShapeDtypeStruct((T, DA), bf16), jax.ShapeDtypeStruct((T, DA), f32),
                   jax.ShapeDtypeStruct((D, D), bf16), jax.ShapeDtypeStruct((8, D), f32)],
        scratch_shapes=[pltpu.VMEM((D, D), f32)],
        compiler_params=_cp("arbitrary"),
    )(dug, duv, w_upT, x1, dx2, proj, w_out, y_na, y_cv, modv, g2)


def _conf_bwd(a, g, cv, dy, conv_w, ln_g, ln_b, blocks):
    tt = CONV_TT
    nt = T // tt
    sub = 32
    wn = tt + 2 * HALO
    nb = len(blocks)

    def body(ap, ac, an, gp, gc, gn, cp_, cc, cn, dp, dc, dn, w_ref, lg_ref, lb_ref, *rest):
        g_refs, (da_ref, dg_ref, dcw_ref, s_ref) = rest[:nb], rest[nb:nb + 4]
        recv_refs, (urot, drot, wacc), a2a_sems = rest[nb + 4:2 * nb + 4], rest[2 * nb + 4:2 * nb + 7], rest[2 * nb + 7:]
        i = pl.program_id(0)
        first, last = i == 0, i == nt - 1
        plans = [_a2a_plan(g_refs[k], recv_refs[k], *a2a_sems[3 * k:3 * k + 3]) for k in range(nb)]
        for start, _ in plans:
            pl.when(first)(start)

        @pl.when(first)
        def _():
            s_ref[...] = jnp.zeros_like(s_ref)
            wacc[...] = jnp.zeros_like(wacc)

        lg, lb = lg_ref[...], lb_ref[...]

        def ln_bwd(cvv, dyv):
            mu = jnp.mean(cvv, axis=-1, keepdims=True)
            xc = cvv - mu
            rstd = lax.rsqrt(jnp.mean(xc * xc, axis=-1, keepdims=True) + EPS)
            yn = xc * rstd
            z = yn * lg + lb
            sz = _sigmoid(z)
            dz = dyv * (sz * (1.0 + z * (1.0 - sz)))
            dyn = dz * lg
            dcv = rstd * (dyn - jnp.mean(dyn, axis=-1, keepdims=True) - yn * jnp.mean(dyn * yn, axis=-1, keepdims=True))
            return dcv, dz, yn

        urot[0, 0:HALO, :] = jnp.where(first, 0.0, ap[...] * _sigmoid(gp[...]))
        urot[0, HALO + tt:, :] = jnp.where(last, 0.0, an[...] * _sigmoid(gn[...]))
        drot[0, 0:HALO, :] = jnp.where(first, 0.0, ln_bwd(cp_[...], dp[...])[0])
        drot[0, HALO + tt:, :] = jnp.where(last, 0.0, ln_bwd(cn[...], dn[...])[0])
        for s in range(tt // sub):
            rr = pl.ds(s * sub, sub)
            urot[0, pl.ds(HALO + s * sub, sub), :] = ac[rr, :] * _sigmoid(gc[rr, :])
            dcv, dz, yn = ln_bwd(cc[rr, :], dc[rr, :])
            drot[0, pl.ds(HALO + s * sub, sub), :] = dcv
            s_ref[0:1, :] += jnp.sum(dcv, axis=0, keepdims=True)
            s_ref[1:2, :] += jnp.sum(dz * yn, axis=0, keepdims=True)
            s_ref[2:3, :] += jnp.sum(dz, axis=0, keepdims=True)
        _shifted_copies(urot, wn)
        _shifted_copies(drot, wn)
        w = w_ref[...]
        for s in range(tt // sub):
            rr = pl.ds(s * sub, sub)
            dcv = drot[0, pl.ds(HALO + s * sub, sub), :]
            acc = jnp.zeros((sub, DA), f32)
            for j in range(CW):
                ad, bd = divmod(2 * HALO - 1 - j, 8)
                au, bu = divmod(1 + j, 8)
                acc = acc + drot[bd, pl.ds(s * sub + 8 * ad, sub), :] * w[j:j + 1, :]
                part = urot[bu, pl.ds(s * sub + 8 * au, sub), :] * dcv
                wacc[j] += part[0:8] + part[8:16] + part[16:24] + part[24:32]
            av, gv = ac[rr, :], gc[rr, :]
            sg = _sigmoid(gv)
            da_ref[rr, :] = (acc * sg).astype(bf16)
            dg_ref[rr, :] = (acc * av * sg * (1.0 - sg)).astype(bf16)

        @pl.when(last)
        def _():
            for j in range(CW):
                dcw_ref[j:j + 1, :] = jnp.sum(wacc[j], axis=0, keepdims=True)
            dcw_ref[CW:CW + 1, :] = jnp.zeros((1, DA), f32)

        for _, finish in plans:
            pl.when(last)(finish)

    hs = _halo_specs(tt, DA, T // HALO)
    hbm = pl.BlockSpec(memory_space=pl.ANY)
    return pl.pallas_call(
        body, name="conf_bwd", grid=(nt,),
        in_specs=hs * 4 + [_full((CW, DA)), _full((1, DA)), _full((1, DA))] + [hbm] * nb,
        out_specs=[pl.BlockSpec((tt, DA), lambda i: (i, 0)), pl.BlockSpec((tt, DA), lambda i: (i, 0)),
                   _full((CW + 1, DA)), _full((8, DA))] + [hbm] * nb,
        out_shape=[jax.ShapeDtypeStruct((T, DA), bf16), jax.ShapeDtypeStruct((T, DA), bf16),
                   jax.ShapeDtypeStruct((CW + 1, DA), f32), jax.ShapeDtypeStruct((8, DA), f32)]
                  + [jax.ShapeDtypeStruct(b.shape, b.dtype) for b in blocks],
        scratch_shapes=[pltpu.VMEM((8, wn, DA), f32), pltpu.VMEM((8, wn, DA), f32), pltpu.VMEM((CW, 8, DA), f32)]
                       + _a2a_scratch() * nb,
        compiler_params=_cp("arbitrary"),
    )(a, a, a, g, g, g, cv, cv, cv, dy, dy, dy, conv_w, ln_g, ln_b, *blocks)


def _attn_bwd(q, k, v, y, dy, lse, bias_tab, blocks):
    zr = 256
    nb = len(blocks)

    def body(q_ref, k_ref, v_ref, y_ref, dy_ref, lse_ref, b_ref, *rest):
        g_refs, (dq_ref, dk_hbm, dv_hbm, db_ref) = rest[:nb], rest[nb:nb + 4]
        recv_refs, (dk_s, dv_s, sem), a2a_sems = rest[nb + 4:2 * nb + 4], rest[2 * nb + 4:2 * nb + 7], rest[2 * nb + 7:]
        r = pl.program_id(0)
        plans = [_a2a_plan(g_refs[i], recv_refs[i], *a2a_sems[3 * i:3 * i + 3]) for i in range(nb)]
        for start, _ in plans:
            pl.when(r == 0)(start)

        @pl.when(r == 0)
        def _():
            def z(i, _):
                rr = pl.ds(pl.multiple_of(i * zr, zr), zr)
                dk_s[rr, :] = jnp.zeros((zr, DA), f32)
                dv_s[rr, :] = jnp.zeros((zr, DA), f32)
                return 0
            lax.fori_loop(0, TA // zr, z, 0)

        @pl.when((r <= WR // 2) | (r > GW - WR // 2))
        def _():
            db_ref[...] = jnp.zeros_like(db_ref)

        ks = pl.multiple_of(_win_start(r) * GW, GW)
        win = pl.ds(ks, WR * GW)
        qq, yy, dyy, lse_v = q_ref[...], y_ref[...], dy_ref[...], lse_ref[...]
        lo = lax.broadcasted_iota(jnp.int32, (GW, 2 * HD), 1) < HD
        ops, pairs = [], []
        for pr in range(NH // 2):
            ps = slice(pr * 2 * HD, (pr + 1) * 2 * HD)
            q2, do2 = qq[:, ps], dyy[:, ps]
            prod = do2.astype(f32) * yy[:, ps].astype(f32)
            delta = jnp.concatenate([jnp.sum(jnp.where(lo, prod, 0.0), axis=-1, keepdims=True),
                                     jnp.sum(jnp.where(lo, 0.0, prod), axis=-1, keepdims=True)], axis=0)
            qst, dost = _stack_pair(q2, lo), _stack_pair(do2, lo)
            kw, vw = k_ref[win, ps], v_ref[win, ps]
            kc, vc = k_ref[T:TA, ps], v_ref[T:TA, ps]
            ops.append((kw, kc))
            pairs.append((qst, dost, delta, _dg(qst, kw, NT), _dg(qst, kc, NT), _dg(dost, vw, NT), _dg(dost, vc, NT)))
        grads = []
        for pr, (qst, dost, delta, sl, sc, dpl, dpc) in enumerate(pairs):
            lh = jnp.concatenate([lse_v[:, 2 * pr:2 * pr + 1], lse_v[:, 2 * pr + 1:2 * pr + 2]], axis=0)
            bias2 = b_ref[0, 2 * pr:2 * pr + 2].reshape(2 * GW, WR * GW)
            pl_ = jnp.exp(sl * SCALE + bias2 - lh)
            pc = jnp.exp(sc * SCALE - lh)
            dsl = pl_ * (dpl - delta)
            dsc = pc * (dpc - delta)
            db_ref[0, 2 * pr:2 * pr + 2] += dsl.reshape(2, GW, WR * GW)
            grads.append((qst, dost, pl_.astype(bf16), pc.astype(bf16), dsl.astype(bf16), dsc.astype(bf16)))
        for pr in range(NH // 2):
            ps = slice(pr * 2 * HD, (pr + 1) * 2 * HD)
            kw, kc = ops[pr]
            qst, dost, plb, pcb, dslb, dscb = grads[pr]
            dqst = _dot(dslb, kw) + _dot(dscb, kc)
            dq_ref[:, ps] = (jnp.where(lo, dqst[0:GW], dqst[GW:]) * SCALE).astype(bf16)
            dk_s[win, ps] += _dg(dslb, qst, TN) * SCALE
            dv_s[win, ps] += _dg(plb, dost, TN)
            dk_s[T:TA, ps] += _dg(dscb, qst, TN) * SCALE
            dv_s[T:TA, ps] += _dg(pcb, dost, TN)

        @pl.when(r == GW - 1)
        def _():
            c1 = pltpu.make_async_copy(dk_s, dk_hbm, sem.at[0])
            c2 = pltpu.make_async_copy(dv_s, dv_hbm, sem.at[1])
            c1.start()
            c2.start()
            c1.wait()
            c2.wait()

        for _, finish in plans:
            pl.when(r == GW - 1)(finish)

    rowq = lambda: pl.BlockSpec((GW, DA), lambda r: (r, 0))
    hbm = pl.BlockSpec(memory_space=pl.ANY)
    return pl.pallas_call(
        body, name="attn_bwd", grid=(GW,),
        in_specs=[rowq(), _full((TA, DA)), _full((TA, DA)), rowq(), rowq(), pl.BlockSpec((GW, NH), lambda r: (r, 0)),
                  pl.BlockSpec((1, NH, GW, WR * GW), lambda r: (_pattern(r), 0, 0, 0))] + [hbm] * nb,
        out_specs=[rowq(), hbm, hbm, pl.BlockSpec((1, NH, GW, WR * GW), lambda r: (_pattern(r), 0, 0, 0))] + [hbm] * nb,
        out_shape=[jax.ShapeDtypeStruct((T, DA), bf16), jax.ShapeDtypeStruct((TA, DA), f32), jax.ShapeDtypeStruct((TA, DA), f32),
                   jax.ShapeDtypeStruct((8, NH, GW, WR * GW), f32)] + [jax.ShapeDtypeStruct(b.shape, b.dtype) for b in blocks],
        scratch_shapes=[pltpu.VMEM((TA, DA), f32), pltpu.VMEM((TA, DA), f32), pltpu.SemaphoreType.DMA((2,))] + _a2a_scratch() * nb,
        compiler_params=_cp("arbitrary"),
    )(q, k, v, y, dy, lse, bias_tab, *blocks)


def _sum_blocks(recv, name):
    _, r, n = recv.shape
    tr = next(cand for cand in (176, 128, 64, 32, 16) if r % cand == 0)

    def body(a_ref, o_ref):
        acc = a_ref[0].astype(f32)
        for d in range(1, NDEV):
            acc = acc + a_ref[d].astype(f32)
        o_ref[...] = acc

    return pl.pallas_call(
        body, name=name, grid=(r // tr,),
        in_specs=[pl.BlockSpec((NDEV, tr, n), lambda i: (0, i, 0))],
        out_specs=pl.BlockSpec((tr, n), lambda i: (i, 0)),
        out_shape=jax.ShapeDtypeStruct((r, n), f32),
        compiler_params=_cp("parallel"),
    )(recv)


def _rpb_reduce(dbias):
    rev = np.eye(GW, dtype=np.float32)[::-1]

    def body(d_ref, rev_ref, o_ref):
        rv = rev_ref[...]
        for h in range(NH):
            dv = d_ref[0, h]
            r0 = dv.astype(bf16)
            e1 = dv - r0.astype(f32)
            r1 = e1.astype(bf16)
            r2 = (e1 - r1.astype(f32)).astype(bf16)
            rr = _dot(rv, r0) + _dot(rv, r1) + _dot(rv, r2)
            skew = pltpu.roll(rr, 0, 1, stride=1, stride_axis=0)
            o_ref[0, h:h + 1, :] = jnp.sum(skew, axis=0, keepdims=True)

    return pl.pallas_call(
        body, name="rpb_reduce", grid=(8,),
        in_specs=[pl.BlockSpec((1, NH, GW, WR * GW), lambda p: (p, 0, 0, 0)), _full((GW, GW))],
        out_specs=pl.BlockSpec((1, NH, WR * GW), lambda p: (p, 0, 0)),
        out_shape=jax.ShapeDtypeStruct((8, NH, WR * GW), f32),
        compiler_params=_cp("parallel"),
    )(dbias, jnp.asarray(rev, dtype=bf16))


def _norm1_bwd(dq, dk, dv, da, dg, w_inT, x0, ctx0, h, dx1, modv, g1):
    tm = 256
    nt = TA // tm
    nx = T // tm

    def body(dq_ref, dk_ref, dv_ref, da_ref, dg_ref, w_ref, x_ref, c_ref, h_ref, dx1_ref, mod_ref, g_ref,
             dx_ref, dwo_ref, s_ref, dw_ref):
        i = pl.program_id(0)
        is_ctx = i == nt - 1

        @pl.when(i == 0)
        def _():
            s_ref[...] = jnp.zeros_like(s_ref)
            dw_ref[...] = jnp.zeros_like(dw_ref)

        hb = h_ref[...]
        dkb, dvb = dk_ref[...].astype(bf16), dv_ref[...].astype(bf16)
        dw_ref[DA:2 * DA, :] += _dg(dkb, hb, TN)
        dw_ref[2 * DA:3 * DA, :] += _dg(dvb, hb, TN)
        dh_kv = _dot(dkb, w_ref[DA:2 * DA, :]) + _dot(dvb, w_ref[2 * DA:3 * DA, :])
        gain = g_ref[...]

        @pl.when(is_ctx)
        def _():
            xv = c_ref[...]
            rstd = lax.rsqrt(jnp.mean(xv * xv, axis=-1, keepdims=True) + EPS)
            _, dsh, dsc, dgn = _norm_bwd(dh_kv, xv, gain, mod_ref[7:8, :], rstd)
            s_ref[2:3, :] += dgn
            s_ref[3:4, :] += dsh
            s_ref[4:5, :] += dsc
            dwo_ref[...] = dw_ref[...].astype(bf16)

        @pl.when(jnp.logical_not(is_ctx))
        def _():
            dqb, dab, dgb = dq_ref[...], da_ref[...], dg_ref[...]
            dw_ref[0:DA, :] += _dg(dqb, hb, TN)
            dw_ref[3 * DA:4 * DA, :] += _dg(dab, hb, TN)
            dw_ref[4 * DA:5 * DA, :] += _dg(dgb, hb, TN)
            dh = (dh_kv + _dot(dqb, w_ref[0:DA, :]) + _dot(dab, w_ref[3 * DA:4 * DA, :])
                  + _dot(dgb, w_ref[4 * DA:5 * DA, :]))
            xv = x_ref[...]
            rstd = lax.rsqrt(jnp.mean(xv * xv, axis=-1, keepdims=True) + EPS)
            dxn, dsh, dsc, dgn = _norm_bwd(dh, xv, gain, mod_ref[1:2, :], rstd)
            dx_ref[...] = dx1_ref[...] + dxn
            s_ref[0:1, :] += dsh
            s_ref[1:2, :] += dsc
            s_ref[2:3, :] += dgn

    row = lambda w: pl.BlockSpec((tm, w), lambda i: (i, 0))
    lrow = lambda w: pl.BlockSpec((tm, w), lambda i: (jnp.minimum(i, nx - 1), 0))
    return pl.pallas_call(
        body, name="norm1_bwd", grid=(nt,),
        in_specs=[lrow(DA), row(DA), row(DA), lrow(DA), lrow(DA), _full((5 * DA, D)), lrow(D), _full((TC, D)), row(D),
                  lrow(D), _full((8, D)), _full((1, D))],
        out_specs=[lrow(D), _full((5 * DA, D)), _full((8, D))],
        out_shape=[jax.ShapeDtypeStruct((T, D), f32), jax.ShapeDtypeStruct((5 * DA, D), bf16), jax.ShapeDtypeStruct((8, D), f32)],
        scratch_shapes=[pltpu.VMEM((5 * DA, D), f32)],
        compiler_params=_cp("arbitrary"),
    )(dq, dk, dv, da, dg, w_inT, x0, ctx0, h, dx1, modv, g1)


def _adamw(w, g, m, v, name):
    r, c = w.shape
    tr = max(t for t in range(8, r + 1, 8) if r % t == 0 and t * c * 4 <= 2 * 1024 * 1024)

    def body(w_ref, g_ref, m_ref, v_ref, d_ref, nm_ref, nv_ref):
        gv = g_ref[...]
        nm = ADAM_B1 * m_ref[...] + (1.0 - ADAM_B1) * gv
        nv = ADAM_B2 * v_ref[...] + (1.0 - ADAM_B2) * (gv * gv)
        m_hat = nm * (1.0 / (1.0 - ADAM_B1 ** ADAM_STEP))
        v_hat = nv * (1.0 / (1.0 - ADAM_B2 ** ADAM_STEP))
        d_ref[...] = -ADAM_LR * (m_hat / (jnp.sqrt(v_hat) + ADAM_EPS) + ADAM_WD * w_ref[...])
        nm_ref[...] = nm
        nv_ref[...] = nv

    spec = pl.BlockSpec((tr, c), lambda i: (i, 0))
    return pl.pallas_call(
        body, name=name, grid=(r // tr,),
        in_specs=[spec] * 4, out_specs=[spec] * 3,
        out_shape=[jax.ShapeDtypeStruct((r, c), f32)] * 3,
        compiler_params=_cp("parallel"),
    )(w, g, m, v)


def _pad_rows128(vec):
    n = vec.shape[0]
    rows = -(-n // 1024) * 8
    return jnp.pad(vec, (0, rows * 128 - n)).reshape(rows, 128)


def _grad_rpb(dbias):
    lane_map, r_hot = _rpb_tables()
    return jnp.einsum("phl,lic,pir->hrc", _rpb_reduce(dbias), jnp.asarray(lane_map), jnp.asarray(r_hot),
                      precision=lax.Precision.HIGHEST)


def kernel(x, c, ctx, c_ctx, w_mod, b_mod, g_norm1, w_in, rpb, conv_w, conv_b, ln_g, ln_b, w_out, g_norm2, w_up, ffn_conv_w, ffn_conv_b, w_down, g_final, loss_target, m_c_ctx, m_w_mod, m_b_mod, m_g_norm1, m_w_in, m_rpb, m_conv_w, m_conv_b, m_ln_g, m_ln_b, m_w_out, m_g_norm2, m_w_up, m_ffn_conv_w, m_ffn_conv_b, m_w_down, m_g_final, v_c_ctx, v_w_mod, v_b_mod, v_g_norm1, v_w_in, v_rpb, v_conv_w, v_conv_b, v_ln_g, v_ln_b, v_w_out, v_g_norm2, v_w_up, v_ffn_conv_w, v_ffn_conv_b, v_w_down, v_g_final):
    me = 4 * lax.axis_index("x") + 2 * lax.axis_index("y") + lax.axis_index("c")
    nmod = w_mod.shape[2]
    n_in = w_in.shape[2]
    n_out = w_out.shape[1]
    n_up = w_up.shape[2]
    n_dn = w_down.shape[1]
    n_cw = conv_w.shape[2]

    b_sh = lax.dynamic_slice(b_mod, (0, me * nmod), (1, nmod))
    convpay = _pad_rows128(jnp.concatenate([conv_w[0].reshape(-1), ffn_conv_w[0].reshape(-1)]))
    w_inT, c_all, mod_all, flat = _head(w_in[0].T.astype(bf16), c.reshape(8, 128), c_ctx.reshape(8, 128), w_mod[0], b_sh, convpay)
    cvec = jnp.concatenate([c_all.reshape(NDEV, D), c_ctx[None, :], jnp.zeros((7, D), f32)], axis=0)
    mod_all = mod_all.transpose(1, 0, 2).reshape(16, 6 * D)
    mod_me = lax.dynamic_index_in_dim(mod_all, me, 0, keepdims=False).reshape(6, D)
    mod_c = mod_all[8]
    modv = jnp.concatenate([mod_me, mod_c[None, 0:D], mod_c[None, D:2 * D]], axis=0)
    flat = flat.reshape(NDEV, -1)
    o1 = CW * n_cw
    conv_w_f = flat[:, :o1].reshape(NDEV, CW, n_cw).transpose(1, 0, 2).reshape(CW, DA)
    fcw_f = flat[:, o1:o1 + 3 * n_up].reshape(NDEV, 3, n_up).transpose(1, 0, 2).reshape(3, F2)

    x0, ctx0 = x[0], ctx[0]
    h, q, k, v, a, g, w_down_f = _in_proj(x0, ctx0, g_norm1, modv, w_inT, [w_down[0].astype(bf16)])
    bias_tab = _bias_table(rpb[0])
    y_cv, cv, w_out_f = _conf_fwd(a, g, conv_w_f, conv_b, ln_g, ln_b, [w_out[0].astype(bf16)])
    y_na, lse, w_upT = _attn_fwd(q, k, v, bias_tab, [w_up[0].T.astype(bf16)])
    x1, proj, h2 = _out_proj(x0, y_na, y_cv, w_out_f, modv, g_norm2)
    ffn, u_t, u2_t = _ffn_fwd(h2, w_upT, fcw_f, ffn_conv_b, w_down_f)
    dx2, dffn, s_loss = _loss_bwd(ffn, x1, loss_target[0], modv, g_final[None, :])

    dug, duv, dw_up, dw_down, dcwg, dcwv, dcbg, dcbv = _ffn_bwd(h2, dffn, u_t, u2_t, fcw_f, w_down_f)
    dx1, dy_na, dy_cv, dw_out, s_n2 = _norm2_bwd(dug, duv, w_upT, x1, dx2, proj, w_out_f, y_na, y_cv, modv, g_norm2)
    da, dg, dcw, s_cf, rv_down = _conf_bwd(a, g, cv, dy_cv, conv_w_f, ln_g, ln_b, [dw_down.reshape(NDEV, n_dn, D)])
    dq, dk, dv, dbias, rv_up = _attn_bwd(q, k, v, y_na, dy_na, lse, bias_tab, [dw_up.reshape(NDEV, n_up, D)])
    g_w_down = _sum_blocks(rv_down, "sum_w_down")
    g_w_upT = _sum_blocks(rv_up, "sum_w_up")
    grad_rpb_part = _grad_rpb(dbias)
    grad_x, dw_inT, s_n1 = _norm1_bwd(dq, dk, dv, da, dg, w_inT, x0, ctx0, h, dx1, modv, g_norm1)
    grad_x = grad_x[None]
    dfcw = jnp.concatenate([dcwg, dcwv], axis=1)
    dfcb = jnp.concatenate([dcbg[0], dcbv[0]])
    small = jnp.concatenate([dcw[:CW].reshape(CW, NDEV, n_cw).transpose(1, 0, 2).reshape(NDEV, CW * n_cw),
                             dfcw.reshape(3, NDEV, n_up).transpose(1, 0, 2).reshape(NDEV, 3 * n_up)], axis=1)
    small = jnp.pad(small.reshape(NDEV, 4, D), ((0, 0), (0, 12), (0, 0))).astype(bf16)
    dmod = jnp.concatenate([s_n1[0], s_n1[1], s_n2[3], s_n2[0], s_n2[1], s_loss[1]])
    dmodc = jnp.concatenate([s_n1[3], s_n1[4]])
    parts = [dmodc, s_n1[2], grad_rpb_part.reshape(-1), s_cf[0], s_cf[1], s_cf[2], s_n2[2], dfcb, s_loss[0], s_loss[3, 0:1]]
    sizes = [p.shape[0] for p in parts]
    pvec = _pad_rows128(jnp.concatenate([dmod] + parts))
    slab = jnp.concatenate([dw_inT.reshape(NDEV, n_in, D), small, dw_out.reshape(NDEV, n_out, D)], axis=1)
    r_a, gath = _reduce_scatter2(slab, [pvec], "rs_w_in")
    g_w_out = r_a[n_in + 16:]
    g_w_inT = r_a[:n_in]
    sm = r_a[n_in:n_in + 4].reshape(-1)
    g_conv_w = sm[:CW * n_cw].reshape(CW, n_cw)
    g_fcw = sm[CW * n_cw:].reshape(3, n_up)
    tot = _sum_rows8(gath, "sum_small").reshape(-1)
    dmod_all = gath.reshape(NDEV, -1)[:, :6 * D]
    offs = np.cumsum([6 * D] + sizes)
    pick = lambda j: tot[offs[j]:offs[j + 1]]
    dmodc_t = jnp.pad(pick(0), (0, 4 * D))
    g_b_mod = (tot[:6 * D] + dmodc_t)[None, :]
    g_g_norm1 = pick(1)[None, :]
    g_rpb = pick(2).reshape(1, NH, 2 * WR - 1, 2 * NCOL - 1)
    g_conv_b, g_ln_g, g_ln_b = pick(3)[None, :], pick(4)[None, :], pick(5)[None, :]
    g_g_norm2 = pick(6)[None, :]
    g_fcb = pick(7)[None, :]
    g_g_final = pick(8)
    loss = pick(9)[0]
    dm_rows = jnp.concatenate([dmod_all, dmodc_t[None, :], jnp.zeros((7, 6 * D), f32)], axis=0)
    dm_sh = lax.dynamic_slice(dm_rows, (0, me * nmod), (16, nmod))
    g_w_mod, gc_part = _mod_bwd(cvec, dm_sh, w_mod[0])
    gc_sum = _sum_rows8(_small_allgather(gc_part[0].reshape(8, 128), "ag_cctx"), "sum_cctx").reshape(D)
    sg_c = _sigmoid(c_ctx)
    g_c_ctx = gc_sum * (sg_c * (1.0 + c_ctx * (1.0 - sg_c)))

    big = [("w_mod", w_mod[0], g_w_mod, m_w_mod[0], v_w_mod[0]), ("w_in", w_in[0].T, g_w_inT, m_w_in[0].T, v_w_in[0].T),
           ("w_out", w_out[0], g_w_out, m_w_out[0], v_w_out[0]), ("w_up", w_up[0].T, g_w_upT, m_w_up[0].T, v_w_up[0].T),
           ("w_down", w_down[0], g_w_down, m_w_down[0], v_w_down[0])]
    upd = {n: _adamw(wv, gv, mv, vv, "adamw_" + n) for n, wv, gv, mv, vv in big}
    for n in ("w_in", "w_up"):
        upd[n] = tuple(arr.T for arr in upd[n])
    g_w_in, g_w_up = g_w_inT.T, g_w_upT.T
    smalls = [("c_ctx", c_ctx, g_c_ctx, m_c_ctx, v_c_ctx), ("b_mod", b_mod, g_b_mod, m_b_mod, v_b_mod),
              ("g_norm1", g_norm1, g_g_norm1, m_g_norm1, v_g_norm1), ("rpb", rpb, g_rpb, m_rpb, v_rpb),
              ("conv_w", conv_w, g_conv_w[None], m_conv_w, v_conv_w), ("conv_b", conv_b, g_conv_b, m_conv_b, v_conv_b),
              ("ln_g", ln_g, g_ln_g, m_ln_g, v_ln_g), ("ln_b", ln_b, g_ln_b, m_ln_b, v_ln_b),
              ("g_norm2", g_norm2, g_g_norm2, m_g_norm2, v_g_norm2),
              ("ffn_conv_w", ffn_conv_w, g_fcw[None], m_ffn_conv_w, v_ffn_conv_w),
              ("ffn_conv_b", ffn_conv_b, g_fcb, m_ffn_conv_b, v_ffn_conv_b), ("g_final", g_final, g_g_final, m_g_final, v_g_final)]
    packed = [_pad_rows128(jnp.concatenate([t[j].reshape(-1) for t in smalls])) for j in (1, 2, 3, 4)]
    sd, sm_, sv = _adamw(*packed, "adamw_small")
    so = np.cumsum([0] + [int(np.prod(t[1].shape)) for t in smalls])
    for j, t in enumerate(smalls):
        shp = t[1].shape
        upd[t[0]] = tuple(arr.reshape(-1)[so[j]:so[j + 1]].reshape(shp) for arr in (sd, sm_, sv))
    grads = {"c_ctx": g_c_ctx, "w_mod": g_w_mod[None], "b_mod": g_b_mod, "g_norm1": g_g_norm1, "w_in": g_w_in[None],
             "rpb": g_rpb, "conv_w": g_conv_w[None], "conv_b": g_conv_b, "ln_g": g_ln_g, "ln_b": g_ln_b,
             "w_out": g_w_out[None], "g_norm2": g_g_norm2, "w_up": g_w_up[None], "ffn_conv_w": g_fcw[None],
             "ffn_conv_b": g_fcb, "w_down": g_w_down[None], "g_final": g_g_final}
    names = ["c_ctx", "w_mod", "b_mod", "g_norm1", "w_in", "rpb", "conv_w", "conv_b", "ln_g", "ln_b", "w_out", "g_norm2",
             "w_up", "ffn_conv_w", "ffn_conv_b", "w_down", "g_final"]
    shapes = {n: grads[n].shape for n in names}
    outs = [loss, grad_x] + [grads[n] for n in names]
    for j in range(3):
        outs += [upd[n][j].reshape(shapes[n]) for n in names]
    return tuple(outs)
```

```python
import functools

import numpy as np
import jax
import jax.numpy as jnp
from jax import lax
from jax.experimental import pallas as pl
from jax.experimental.pallas import tpu as pltpu

f32 = jnp.float32
bf16 = jnp.bfloat16

D = 1024
T = 4096
TC = 256
TA = T + TC
DA = 512
NH = 8
HD = 64
GW = 64
WR = 8
NCOL = 16
F = 2816
F2 = 2 * F
CW = 31
NDEV = 8
EPS = 1e-6
SCALE = HD ** -0.5
NEG = -1e30
MESH = pl.DeviceIdType.MESH

NT = (((1,), (1,)), ((), ()))
TN = (((0,), (0,)), ((), ()))

ADAM_LR, ADAM_B1, ADAM_B2, ADAM_EPS, ADAM_WD, ADAM_STEP = 0.001, 0.9, 0.999, 1e-08, 0.01, 10

VMEM_LIMIT = 56 * 1024 * 1024


def _cp(*sem):
    return pltpu.CompilerParams(dimension_semantics=sem or None, vmem_limit_bytes=VMEM_LIMIT)


def _dot(a, b):
    return jnp.dot(a, b, preferred_element_type=f32)


def _dg(a, b, dims):
    return lax.dot_general(a, b, dims, preferred_element_type=f32)


def _sigmoid(x):
    return 1.0 / (1.0 + jnp.exp(-x))


def _full(shape):
    n = len(shape)
    return pl.BlockSpec(shape, lambda *_: (0,) * n)


def _resident(shape):
    n = len(shape)
    return pl.BlockSpec(shape, lambda *_: (0,) * n, pipeline_mode=pl.Buffered(1))


def _my_pos():
    return lax.axis_index("x"), lax.axis_index("y"), lax.axis_index("c")


def _small_gather_plan(v_ref, out_ref, send_sems, recv_sems):
    x, y, c = _my_pos()
    me = 4 * x + 2 * y + c
    peers = []
    for k in range(1, NDEV):
        kx, ky, kc = (k >> 2) & 1, (k >> 1) & 1, k & 1
        peers.append((x ^ kx, y ^ ky, c ^ kc))

    def copy(k, slot, to):
        return pltpu.make_async_remote_copy(
            src_ref=v_ref, dst_ref=out_ref.at[slot], send_sem=send_sems.at[k], recv_sem=recv_sems.at[k],
            device_id=to, device_id_type=MESH)

    def start():
        out_ref[me] = v_ref[...]
        for k, p in enumerate(peers):
            copy(k, me, p).start()

    def finish():
        for k, (px, py, pc) in enumerate(peers):
            copy(k, 4 * px + 2 * py + pc, (x, y, c)).wait_recv()
        for k, p in enumerate(peers):
            copy(k, me, p).wait_send()

    return start, finish


def _small_gather_scratch():
    return [pltpu.SemaphoreType.DMA((NDEV - 1,)), pltpu.SemaphoreType.DMA((NDEV - 1,))]


def _small_allgather(v, name):
    n = v.shape[0]

    def body(v_ref, out_ref, send_sems, recv_sems):
        start, finish = _small_gather_plan(v_ref, out_ref, send_sems, recv_sems)
        start()
        finish()

    return pl.pallas_call(
        body, name=name,
        out_shape=jax.ShapeDtypeStruct((NDEV, n, 128), f32),
        in_specs=[pl.BlockSpec(memory_space=pltpu.VMEM)],
        out_specs=pl.BlockSpec(memory_space=pltpu.VMEM),
        scratch_shapes=_small_gather_scratch(),
    )(v)


def _ag2_plan(x_refs, out_refs, send_sems, recv_sems, local_sems):
    na = len(x_refs)
    x, y, c = _my_pos()
    me, sibling = (x, y, c), (x, y, 1 - c)
    chips = [(1 - x, y), (x, 1 - y), (1 - x, 1 - y)]

    def rows(i, px, py, pc):
        m_per = x_refs[i].shape[0]
        return out_refs[i].at[pl.ds(pl.multiple_of((4 * px + 2 * py + pc) * m_per, 16 if m_per % 16 == 0 else 8), m_per), :]

    def copies(k, block, to, from_shard=False):
        return [pltpu.make_async_remote_copy(
            src_ref=x_refs[i] if from_shard else rows(i, *block), dst_ref=rows(i, *block),
            send_sem=send_sems.at[k * na + i], recv_sem=recv_sems.at[k * na + i], device_id=to, device_id_type=MESH)
            for i in range(na)]

    def mine():
        return [pltpu.make_async_copy(x_refs[i], rows(i, *me), local_sems.at[i]) for i in range(na)]

    def first():
        cps = copies(0, me, sibling, True)
        for j, chip in enumerate(chips):
            cps += copies(1 + j, me, (*chip, c), True)
        return cps

    def start():
        for cp in mine() + first():
            cp.start()

    def forward():
        for j, chip in enumerate(chips):
            for cp in copies(1 + j, (*chip, c), me):
                cp.wait_recv()
            for cp in copies(4 + j, (*chip, c), sibling):
                cp.start()

    def finish():
        for cp in copies(0, sibling, me):
            cp.wait_recv()
        for j, chip in enumerate(chips):
            for cp in copies(4 + j, (*chip, 1 - c), me):
                cp.wait_recv()
        for cp in first():
            cp.wait_send()
        for j, chip in enumerate(chips):
            for cp in copies(4 + j, (*chip, c), sibling):
                cp.wait_send()
        for cp in mine():
            cp.wait()

    return start, forward, finish


def _ag2_scratch(na):
    return [pltpu.SemaphoreType.DMA((7 * na,)), pltpu.SemaphoreType.DMA((7 * na,)), pltpu.SemaphoreType.DMA((na,))]


def _a2a_plan(g_ref, recv_ref, send_sems, recv_sems, local_sem):
    x, y, c = _my_pos()
    me = 4 * x + 2 * y + c
    peers = []
    for k in range(1, NDEV):
        kx, ky, kc = (k >> 2) & 1, (k >> 1) & 1, k & 1
        peers.append((x ^ kx, y ^ ky, c ^ kc))

    def sends():
        return [pltpu.make_async_remote_copy(
            src_ref=g_ref.at[4 * px + 2 * py + pc], dst_ref=recv_ref.at[me], send_sem=send_sems.at[k], recv_sem=recv_sems.at[k],
            device_id=(px, py, pc), device_id_type=MESH) for k, (px, py, pc) in enumerate(peers)]

    def own():
        return pltpu.make_async_copy(g_ref.at[me], recv_ref.at[me], local_sem)

    def start():
        own().start()
        for cp in sends():
            cp.start()

    def finish():
        for k, (px, py, pc) in enumerate(peers):
            pltpu.make_async_remote_copy(
                src_ref=g_ref.at[me], dst_ref=recv_ref.at[4 * px + 2 * py + pc], send_sem=send_sems.at[k],
                recv_sem=recv_sems.at[k], device_id=(x, y, c), device_id_type=MESH).wait_recv()
        for cp in sends():
            cp.wait_send()
        own().wait()

    return start, finish


def _a2a_scratch():
    return [pltpu.SemaphoreType.DMA((NDEV - 1,)), pltpu.SemaphoreType.DMA((NDEV - 1,)), pltpu.SemaphoreType.DMA]


def _reduce_scatter2(g, small, name):
    _, r, n = g.shape
    ch = 16
    nch = r // ch
    ns = len(small)

    def body(g_ref, *rest):
        v_refs, out_ref, vout_refs = rest[:ns], rest[ns], rest[ns + 1:2 * ns + 1]
        a_ref, h_ref, b_ref, s1_send, s1_recv, s2_send, s2_recv = rest[2 * ns + 1:2 * ns + 8]
        gather_sems = rest[2 * ns + 8:]
        gathers = [_small_gather_plan(v_refs[i], vout_refs[i], *gather_sems[2 * i:2 * i + 2]) for i in range(ns)]
        for start, _ in gathers:
            start()
        x, y, c = _my_pos()
        sibling = (x, y, 1 - c)
        s1 = []
        for j in range(4):
            cp = pltpu.make_async_remote_copy(
                src_ref=g_ref.at[2 * j + (1 - c)], dst_ref=a_ref.at[j], send_sem=s1_send.at[j], recv_sem=s1_recv.at[j],
                device_id=sibling, device_id_type=MESH)
            cp.start()
            s1.append(cp)
        for cp in s1:
            cp.wait_recv()

        def add1(i, _):
            rr = pl.ds(pl.multiple_of(i * ch, ch), ch)
            for j in range(4):
                h_ref[j, rr, :] = (g_ref[2 * j + c, rr, :].astype(f32) + a_ref[j, rr, :].astype(f32)).astype(bf16)
            return 0
        lax.fori_loop(0, nch, add1, 0)
        mychip = 2 * x + y
        s2 = []
        for m in range(1, 4):
            mx, my_ = (m >> 1) & 1, m & 1
            px, py = x ^ mx, y ^ my_
            cp = pltpu.make_async_remote_copy(
                src_ref=h_ref.at[2 * px + py], dst_ref=b_ref.at[m - 1], send_sem=s2_send.at[m - 1], recv_sem=s2_recv.at[m - 1],
                device_id=(px, py, c), device_id_type=MESH)
            cp.start()
            s2.append(cp)
        for cp in s2:
            cp.wait_recv()

        def add2(i, _):
            rr = pl.ds(pl.multiple_of(i * ch, ch), ch)
            acc = h_ref[mychip, rr, :].astype(f32)
            for m in range(3):
                acc = acc + b_ref[m, rr, :].astype(f32)
            out_ref[rr, :] = acc
            return 0
        lax.fori_loop(0, nch, add2, 0)
        for cp in s1 + s2:
            cp.wait_send()
        for _, finish in gathers:
            finish()

    vmem = pl.BlockSpec(memory_space=pltpu.VMEM)
    return pl.pallas_call(
        body, name=name,
        out_shape=[jax.ShapeDtypeStruct((r, n), f32)] + [jax.ShapeDtypeStruct((NDEV,) + v.shape, v.dtype) for v in small],
        in_specs=[vmem] * (1 + ns),
        out_specs=[vmem] * (1 + ns),
        scratch_shapes=[pltpu.VMEM((4, r, n), bf16), pltpu.VMEM((4, r, n), bf16), pltpu.VMEM((3, r, n), bf16),
                        pltpu.SemaphoreType.DMA((4,)), pltpu.SemaphoreType.DMA((4,)),
                        pltpu.SemaphoreType.DMA((3,)), pltpu.SemaphoreType.DMA((3,))] + _small_gather_scratch() * ns,
        compiler_params=pltpu.CompilerParams(vmem_limit_bytes=VMEM_LIMIT),
    )(g, *small)


def _head(w_in_sh, c8, cctx8, w_mod_sh, b_sh, convpay):
    nmod = w_mod_sh.shape[1]
    npay = convpay.shape[0]

    def body(w_ref, c_ref, cc_ref, wm_ref, b_ref, pay_ref, wout_ref, call_ref, mall_ref, pall_ref, mod_s,
             ag_send, ag_recv, ag_local, c_send, c_recv, m_send, m_recv, p_send, p_recv):
        start, forward, finish = _ag2_plan([w_ref], [wout_ref], ag_send, ag_recv, ag_local)
        c_start, c_finish = _small_gather_plan(c_ref, call_ref, c_send, c_recv)
        m_start, m_finish = _small_gather_plan(mod_s, mall_ref, m_send, m_recv)
        p_start, p_finish = _small_gather_plan(pay_ref, pall_ref, p_send, p_recv)
        c_start()
        start()
        p_start()
        c_finish()
        acc = jnp.zeros((16, nmod), f32)
        for j in range(D // 128):
            rows = jnp.concatenate([call_ref[:, j, :], cc_ref[j:j + 1, :], jnp.zeros((7, 128), f32)], axis=0)
            act = (rows * _sigmoid(rows)).astype(bf16)
            acc = acc + _dot(act, wm_ref[j * 128:(j + 1) * 128, :].astype(bf16))
        mod_s[...] = acc + b_ref[...]
        m_start()
        forward()
        finish()
        m_finish()
        p_finish()

    vmem = pl.BlockSpec(memory_space=pltpu.VMEM)
    return pl.pallas_call(
        body, name="head",
        out_shape=[jax.ShapeDtypeStruct((NDEV * w_in_sh.shape[0], D), bf16), jax.ShapeDtypeStruct((NDEV, 8, 128), f32),
                   jax.ShapeDtypeStruct((NDEV, 16, nmod), f32), jax.ShapeDtypeStruct((NDEV, npay, 128), f32)],
        in_specs=[vmem] * 6, out_specs=[vmem] * 4,
        scratch_shapes=[pltpu.VMEM((16, nmod), f32)] + _ag2_scratch(1) + _small_gather_scratch() * 3,
        compiler_params=pltpu.CompilerParams(vmem_limit_bytes=VMEM_LIMIT),
    )(w_in_sh, c8, cctx8, w_mod_sh, b_sh, convpay)


def _mod_bwd(cvec, dm_sh, w_sh):
    def body(c_ref, dm_ref, w_ref, gw_ref, gc_ref):
        cv = c_ref[...]
        act = (cv * _sigmoid(cv)).astype(bf16)
        gw_ref[...] = _dg(act, dm_ref[...].astype(bf16), TN)
        gc_ref[...] = _dg(dm_ref[8:16, :].astype(bf16), w_ref[...].astype(bf16), NT)
    return pl.pallas_call(
        body, name="mod_bwd",
        out_shape=(jax.ShapeDtypeStruct(w_sh.shape, f32), jax.ShapeDtypeStruct((8, D), f32)))(cvec, dm_sh, w_sh)


def _sum_rows8(a, name):
    n = a.shape[1]

    def body(a_ref, o_ref):
        acc = a_ref[0]
        for d in range(1, NDEV):
            acc = acc + a_ref[d]
        o_ref[...] = acc
    return pl.pallas_call(body, name=name, out_shape=jax.ShapeDtypeStruct((n, 128), f32))(a)


def _in_proj(x0, ctx0, g1, modv, w_inT, shards):
    tm = 256
    nt = TA // tm
    nx = T // tm
    na = len(shards)

    def body(x_ref, c_ref, g_ref, mod_ref, w_ref, *rest):
        x_refs, (h_ref, q_ref, k_ref, v_ref, a_ref, gg_ref) = rest[:na], rest[na:na + 6]
        out_refs, sems = rest[na + 6:2 * na + 6], rest[2 * na + 6:]
        i = pl.program_id(0)
        if na:
            start, forward, finish = _ag2_plan(x_refs, out_refs, *sems)
            pl.when(i == 0)(start)
            pl.when(i == nt - 2)(forward)
        is_ctx = i == nt - 1
        xv = jnp.where(is_ctx, c_ref[...], x_ref[...])
        rstd = lax.rsqrt(jnp.mean(xv * xv, axis=-1, keepdims=True) + EPS)
        sh = jnp.where(is_ctx, mod_ref[6:7, :], mod_ref[0:1, :])
        sc = jnp.where(is_ctx, mod_ref[7:8, :], mod_ref[1:2, :])
        h = ((xv * rstd * g_ref[...]) * (1.0 + sc) + sh).astype(bf16)
        h_ref[...] = h
        for j, o_ref in enumerate((q_ref, k_ref, v_ref, a_ref, gg_ref)):
            o_ref[...] = _dg(h, w_ref[j * DA:(j + 1) * DA, :], NT).astype(o_ref.dtype)
        if na:
            pl.when(is_ctx)(finish)

    row = lambda w: pl.BlockSpec((tm, w), lambda i: (i, 0))
    hbm = pl.BlockSpec(memory_space=pl.ANY)
    return pl.pallas_call(
        body, name="in_proj", grid=(nt,),
        in_specs=[pl.BlockSpec((tm, D), lambda i: (jnp.minimum(i, nx - 1), 0)), _full((TC, D)),
                  _full((1, D)), _full((8, D)), _full((5 * DA, D))] + [hbm] * na,
        out_specs=[row(D), row(DA), row(DA), row(DA), row(DA), row(DA)] + [hbm] * na,
        out_shape=[jax.ShapeDtypeStruct((TA, D), bf16)] + [jax.ShapeDtypeStruct((TA, DA), bf16)] * 3
                  + [jax.ShapeDtypeStruct((TA, DA), f32)] * 2
                  + [jax.ShapeDtypeStruct((NDEV * sh.shape[0], sh.shape[1]), sh.dtype) for sh in shards],
        scratch_shapes=_ag2_scratch(na) if na else [],
        compiler_params=_cp("arbitrary"),
    )(x0, ctx0, g1, modv, w_inT, *shards)


def _win_start(r):
    return jnp.clip(r - WR // 2, 0, GW - WR)


def _pattern(r):
    return _win_start(r) - r + (WR - 1)


def _bias_table(rpb):
    qc = np.arange(GW)[:, None]
    kc = np.arange(GW)[None, :]
    cs = np.clip(qc - NCOL // 2, 0, GW - NCOL)
    valid = np.tile(((kc >= cs) & (kc < cs + NCOL)).astype(np.int32), (1, WR))
    pad = jnp.pad(rpb, ((0, 0), (0, 0), (0, GW - (2 * NCOL - 1))))
    base = jnp.stack([pad[:, p:p + WR, :].reshape(NH, WR * GW) for p in range(8)])

    def body(base_ref, valid_ref, o_ref):
        ok = valid_ref[...] != 0
        for h in range(NH):
            row = jnp.broadcast_to(base_ref[0, h:h + 1, :], (GW, WR * GW))
            skew = pltpu.roll(row, WR * GW - (NCOL - 1), 1, stride=1, stride_axis=0)
            o_ref[0, h] = jnp.where(ok, skew, NEG)

    return pl.pallas_call(
        body, name="bias_table", grid=(8,),
        in_specs=[pl.BlockSpec((1, NH, WR * GW), lambda p: (p, 0, 0)), _full((GW, WR * GW))],
        out_specs=pl.BlockSpec((1, NH, GW, WR * GW), lambda p: (p, 0, 0, 0)),
        out_shape=jax.ShapeDtypeStruct((8, NH, GW, WR * GW), f32),
        compiler_params=_cp("parallel"),
    )(base, jnp.asarray(valid))


def _rpb_tables():
    lane_map = np.zeros((WR * GW, WR, 2 * NCOL - 1), np.float32)
    for i in range(WR):
        for t in range(GW):
            if t >= GW - NCOL:
                lane_map[i * GW + t, i, t - (GW - NCOL)] = 1.0
            elif t < NCOL - 1:
                lane_map[i * GW + t, (i - 1) % WR, t + NCOL] = 1.0
    p = np.arange(8)[:, None]
    i = np.arange(WR)[None, :]
    r_hot = ((p + i)[:, :, None] == np.arange(2 * WR - 1)[None, None, :]).astype(np.float32)
    return lane_map, r_hot


AG_FORWARD_ROW = 58


def _stack_pair(x2, lo):
    z = jnp.zeros_like(x2)
    return jnp.concatenate([jnp.where(lo, x2, z), jnp.where(lo, z, x2)], axis=0)


def _attn_fwd(q, k, v, bias_tab, shards):
    na = len(shards)

    def body(q_ref, k_ref, v_ref, b_ref, *rest):
        x_refs, (y_ref, lse_ref), out_refs, sems = rest[:na], rest[na:na + 2], rest[na + 2:2 * na + 2], rest[2 * na + 2:]
        r = pl.program_id(0)
        if na:
            start, forward, finish = _ag2_plan(x_refs, out_refs, *sems)
            pl.when(r == 0)(start)
            pl.when(r == AG_FORWARD_ROW)(forward)
        ks = pl.multiple_of(_win_start(r) * GW, GW)
        qq = q_ref[...]
        lo = lax.broadcasted_iota(jnp.int32, (GW, 2 * HD), 1) < HD
        kv, scores = [], []
        for pr in range(NH // 2):
            ps = slice(pr * 2 * HD, (pr + 1) * 2 * HD)
            qst = _stack_pair(qq[:, ps], lo)
            kw, kc = k_ref[pl.ds(ks, WR * GW), ps], k_ref[T:TA, ps]
            kv.append((v_ref[pl.ds(ks, WR * GW), ps], v_ref[T:TA, ps]))
            bias2 = b_ref[0, 2 * pr:2 * pr + 2].reshape(2 * GW, WR * GW)
            scores.append((_dg(qst, kw, NT) * SCALE + bias2, _dg(qst, kc, NT) * SCALE))
        probs = []
        for pr, (sl, sc) in enumerate(scores):
            m = jnp.maximum(jnp.max(sl, axis=-1, keepdims=True), jnp.max(sc, axis=-1, keepdims=True))
            pl_ = jnp.exp(sl - m)
            pc = jnp.exp(sc - m)
            l = jnp.sum(pl_, axis=-1, keepdims=True) + jnp.sum(pc, axis=-1, keepdims=True)
            lse = m + jnp.log(l)
            lse_ref[:, 2 * pr:2 * pr + 1] = lse[0:GW]
            lse_ref[:, 2 * pr + 1:2 * pr + 2] = lse[GW:]
            probs.append((pl_.astype(bf16), pc.astype(bf16), 1.0 / l))
        for pr in range(NH // 2):
            ps = slice(pr * 2 * HD, (pr + 1) * 2 * HD)
            vw, vc = kv[pr]
            pb, cb, rl = probs[pr]
            o = (_dot(pb, vw) + _dot(cb, vc)) * rl
            y_ref[:, ps] = jnp.where(lo, o[0:GW], o[GW:]).astype(bf16)
        if na:
            pl.when(r == GW - 1)(finish)

    hbm = pl.BlockSpec(memory_space=pl.ANY)
    return pl.pallas_call(
        body, name="attn_fwd", grid=(GW,),
        in_specs=[pl.BlockSpec((GW, DA), lambda r: (r, 0)), _full((TA, DA)), _full((TA, DA)),
                  pl.BlockSpec((1, NH, GW, WR * GW), lambda r: (_pattern(r), 0, 0, 0))] + [hbm] * na,
        out_specs=[pl.BlockSpec((GW, DA), lambda r: (r, 0)), pl.BlockSpec((GW, NH), lambda r: (r, 0))] + [hbm] * na,
        out_shape=[jax.ShapeDtypeStruct((T, DA), bf16), jax.ShapeDtypeStruct((T, NH), f32)]
                  + [jax.ShapeDtypeStruct((NDEV * s.shape[0], s.shape[1]), s.dtype) for s in shards],
        scratch_shapes=_ag2_scratch(na) if na else [],
        compiler_params=_cp("arbitrary"),
    )(q, k, v, bias_tab, *shards)


CONV_TT = 512
HALO = 16


def _halo_specs(tt, w, nrows_blocks):
    per = tt // HALO
    prev = pl.BlockSpec((HALO, w), lambda i: (jnp.maximum(i * per - 1, 0), 0))
    cur = pl.BlockSpec((tt, w), lambda i: (i, 0))
    nxt = pl.BlockSpec((HALO, w), lambda i: (jnp.minimum((i + 1) * per, nrows_blocks - 1), 0))
    return [prev, cur, nxt]


def _shifted_copies(rot, wn):
    for b in range(1, 8):
        rot[b, 0:wn - 8, :] = rot[0, pl.ds(b, wn - 8), :]


def _conf_fwd(a, g, conv_w, conv_b, ln_g, ln_b, shards):
    tt = CONV_TT
    nt = T // tt
    sub = 32
    wn = tt + 2 * HALO
    na = len(shards)

    def body(ap, ac, an, gp, gc, gn, w_ref, b_ref, lg_ref, lb_ref, *rest):
        x_refs, (y_ref, cv_ref), out_refs, (rot, *sems) = rest[:na], rest[na:na + 2], rest[na + 2:2 * na + 2], rest[2 * na + 2:]
        i = pl.program_id(0)
        if na:
            start, forward, finish = _ag2_plan(x_refs, out_refs, *sems)
            pl.when(i == 0)(start)
            pl.when(i == nt // 2)(forward)
        rot[0, 0:HALO, :] = jnp.where(i > 0, ap[...] * _sigmoid(gp[...]), 0.0)
        rot[0, HALO:HALO + tt, :] = ac[...] * _sigmoid(gc[...])
        rot[0, HALO + tt:, :] = jnp.where(i < nt - 1, an[...] * _sigmoid(gn[...]), 0.0)
        _shifted_copies(rot, wn)
        w = w_ref[...]
        for s in range(tt // sub):
            acc = jnp.zeros((sub, DA), f32)
            for j in range(CW):
                a8, b8 = divmod(1 + j, 8)
                acc = acc + rot[b8, pl.ds(s * sub + 8 * a8, sub), :] * w[j:j + 1, :]
            cv = acc + b_ref[...]
            cv_ref[pl.ds(s * sub, sub), :] = cv
            mu = jnp.mean(cv, axis=-1, keepdims=True)
            xc = cv - mu
            rstd = lax.rsqrt(jnp.mean(xc * xc, axis=-1, keepdims=True) + EPS)
            z = xc * rstd * lg_ref[...] + lb_ref[...]
            y_ref[pl.ds(s * sub, sub), :] = (z * _sigmoid(z)).astype(bf16)
        if na:
            pl.when(i == nt - 1)(finish)

    hs = _halo_specs(tt, DA, T // HALO)
    hbm = pl.BlockSpec(memory_space=pl.ANY)
    return pl.pallas_call(
        body, name="conf_fwd", grid=(nt,),
        in_specs=hs + hs + [_full((CW, DA)), _full((1, DA)), _full((1, DA)), _full((1, DA))] + [hbm] * na,
        out_specs=[pl.BlockSpec((tt, DA), lambda i: (i, 0)), pl.BlockSpec((tt, DA), lambda i: (i, 0))] + [hbm] * na,
        out_shape=[jax.ShapeDtypeStruct((T, DA), bf16), jax.ShapeDtypeStruct((T, DA), f32)]
                  + [jax.ShapeDtypeStruct((NDEV * s.shape[0], s.shape[1]), s.dtype) for s in shards],
        scratch_shapes=[pltpu.VMEM((8, wn, DA), f32)] + (_ag2_scratch(na) if na else []),
        compiler_params=_cp("arbitrary"),
    )(a, a, a, g, g, g, conv_w, conv_b, ln_g, ln_b, *shards)


def _out_proj(xa, y_na, y_cv, w_out, modv, g2):
    tm = 512

    def body(x_ref, ya_ref, yc_ref, w_ref, mod_ref, g_ref, x1_ref, pj_ref, h2_ref):
        proj = _dot(ya_ref[...], w_ref[0:DA, :]) + _dot(yc_ref[...], w_ref[DA:D, :])
        x1 = x_ref[...] + mod_ref[2:3, :] * proj
        x1_ref[...] = x1
        pj_ref[...] = proj.astype(bf16)
        rstd = lax.rsqrt(jnp.mean(x1 * x1, axis=-1, keepdims=True) + EPS)
        h2_ref[...] = ((x1 * rstd * g_ref[...]) * (1.0 + mod_ref[4:5, :]) + mod_ref[3:4, :]).astype(bf16)

    row = lambda w: pl.BlockSpec((tm, w), lambda i: (i, 0))
    return pl.pallas_call(
        body, name="out_proj", grid=(T // tm,),
        in_specs=[row(D), row(DA), row(DA), _full((D, D)), _full((8, D)), _full((1, D))],
        out_specs=[row(D), row(D), row(D)],
        out_shape=[jax.ShapeDtypeStruct((T, D), f32), jax.ShapeDtypeStruct((T, D), bf16), jax.ShapeDtypeStruct((T, D), bf16)],
        compiler_params=_cp("parallel"),
    )(xa, y_na, y_cv, w_out, modv, g2)


FFN_TT = 2048
FFN_CT = 256
FFN_NC = F // FFN_CT
FFN_SUB = 32


def _row_neighbours(ref, r, n):
    blk = ref[pl.ds(r - 8, n + 16), :]
    return blk[8:8 + n, :], pltpu.roll(blk, 1, 0)[8:8 + n, :], pltpu.roll(blk, n + 15, 0)[8:8 + n, :]


def _ffn_specs(tt, ct, by_token_first):
    tc = (lambda f: (lambda t, c: f(t, c))) if by_token_first else (lambda f: (lambda c, t: f(t, c)))
    per = tt // HALO
    halo = [pl.BlockSpec((HALO, D), tc(lambda t, c: (jnp.maximum(t * per - 1, 0), 0))),
            pl.BlockSpec((tt, D), tc(lambda t, c: (t, 0))),
            pl.BlockSpec((HALO, D), tc(lambda t, c: (jnp.minimum((t + 1) * per, T // HALO - 1), 0)))]
    weights = [pl.BlockSpec((ct, D), tc(lambda t, c: (c, 0))), pl.BlockSpec((ct, D), tc(lambda t, c: (c + FFN_NC, 0))),
               pl.BlockSpec((3, ct), tc(lambda t, c: (0, c))), pl.BlockSpec((3, ct), tc(lambda t, c: (0, c + FFN_NC))),
               pl.BlockSpec((1, ct), tc(lambda t, c: (0, c))), pl.BlockSpec((1, ct), tc(lambda t, c: (0, c + FFN_NC))),
               pl.BlockSpec((ct, D), tc(lambda t, c: (c, 0)))]
    return halo, weights


def _ffn_fwd(h2, w_upT, fcw, fcb, w_down):
    tt, ct = FFN_TT, FFN_CT
    nt = T // tt
    wn = tt + 2 * HALO
    half = tt // 2

    def body(hp, hc, hn, wg_ref, wv_ref, cwg_ref, cwv_ref, cbg_ref, cbv_ref, wd_ref, o_ref, u_ref, u2_ref, hwin, uwin, act):
        t = pl.program_id(0)
        c = pl.program_id(1)

        @pl.when(c == 0)
        def _():
            hwin[0:HALO, :] = jnp.where(t > 0, hp[...], jnp.zeros_like(hp[...]))
            hwin[HALO:HALO + tt, :] = hc[...]
            hwin[HALO + tt:, :] = jnp.where(t < nt - 1, hn[...], jnp.zeros_like(hn[...]))
            o_ref[...] = jnp.zeros_like(o_ref)

        for r0, r1 in ((0, half + 2 * HALO), (half + 2 * HALO, wn)):
            hw = hwin[r0:r1, :]
            uwin[r0:r1, :ct] = _dg(hw, wg_ref[...], NT)
            uwin[r0:r1, ct:] = _dg(hw, wv_ref[...], NT)
        cw = jnp.concatenate([cwg_ref[...], cwv_ref[...]], axis=1)
        cb = jnp.concatenate([cbg_ref[...], cbv_ref[...]], axis=1)
        for p in range(2):
            for r in range(p * half, (p + 1) * half, FFN_SUB):
                uc, prev, nxt = _row_neighbours(uwin, HALO + r, FFN_SUB)
                u2 = prev * cw[0:1, :] + uc * cw[1:2, :] + nxt * cw[2:3, :] + cb
                u_ref[r:r + FFN_SUB, :] = uc.astype(bf16)
                u2_ref[r:r + FFN_SUB, :] = u2
                gate = u2[:, :ct]
                act[r:r + FFN_SUB, :] = (gate * _sigmoid(gate) * u2[:, ct:]).astype(bf16)
            rows = slice(p * half, (p + 1) * half)
            o_ref[rows, :] += _dot(act[rows, :], wd_ref[...])

    halo, weights = _ffn_specs(tt, ct, True)
    pair = pl.BlockSpec((tt, 2 * ct), lambda t, c: (t, c))
    return pl.pallas_call(
        body, name="ffn_fwd", grid=(nt, FFN_NC),
        in_specs=halo + weights,
        out_specs=[pl.BlockSpec((tt, D), lambda t, c: (t, 0)), pair, pair],
        out_shape=[jax.ShapeDtypeStruct((T, D), f32), jax.ShapeDtypeStruct((T, F2), bf16), jax.ShapeDtypeStruct((T, F2), f32)],
        scratch_shapes=[pltpu.VMEM((wn, D), bf16), pltpu.VMEM((wn, 2 * ct), f32), pltpu.VMEM((tt, ct), bf16)],
        compiler_params=_cp("parallel", "arbitrary"),
    )(h2, h2, h2, w_upT, w_upT, fcw, fcw, fcb, fcb, w_down)


def _loss_bwd(ffn, x1, tgt, modv, gf):
    tm = 1024
    nt = T // tm

    def body(f_ref, x1_ref, t_ref, mod_ref, g_ref, dx2_ref, df_ref, s_ref):
        i = pl.program_id(0)

        @pl.when(i == 0)
        def _():
            s_ref[...] = jnp.zeros_like(s_ref)

        ff = f_ref[...]
        gt2 = mod_ref[5:6, :]
        x2 = x1_ref[...] + gt2 * ff
        rstd = lax.rsqrt(jnp.mean(x2 * x2, axis=-1, keepdims=True) + EPS)
        xh = x2 * rstd
        gfv = g_ref[...]
        e = xh * gfv - t_ref[...]
        dy = e * (1.0 / D)
        dxh = dy * gfv
        dx2 = rstd * (dxh - xh * jnp.mean(dxh * xh, axis=-1, keepdims=True))
        dx2_ref[...] = dx2
        df_ref[...] = (dx2 * gt2).astype(bf16)
        s_ref[0:1, :] += jnp.sum(dy * xh, axis=0, keepdims=True)
        s_ref[1:2, :] += jnp.sum(dx2 * ff, axis=0, keepdims=True)
        s_ref[2:3, :] += jnp.sum(e * e, axis=0, keepdims=True)

        @pl.when(i == nt - 1)
        def _():
            tot = jnp.sum(s_ref[2:3, :], axis=-1, keepdims=True) * (0.5 / D)
            s_ref[3:4, :] = jnp.broadcast_to(tot, (1, D))

    row = lambda: pl.BlockSpec((tm, D), lambda i: (i, 0))
    return pl.pallas_call(
        body, name="loss_bwd", grid=(nt,),
        in_specs=[row(), row(), row(), _full((8, D)), _full((1, D))],
        out_specs=[row(), row(), _full((8, D))],
        out_shape=[jax.ShapeDtypeStruct((T, D), f32), jax.ShapeDtypeStruct((T, D), bf16), jax.ShapeDtypeStruct((8, D), f32)],
        compiler_params=_cp("arbitrary"),
    )(ffn, x1, tgt, modv, gf)


def _ffn_bwd(h2, dffn, u_t, u2_t, fcw, w_down):
    tt, ct = FFN_TT, FFN_CT
    nt = T // tt
    wn = tt + 2 * HALO
    half = tt // 2

    def body(dp, dc, dn, hc, uc_ref, u2p, u2c, u2n, cwg_ref, cwv_ref, wd_ref,
             dug_ref, duv_ref, dwu_ref, dwd_ref, dcwg_ref, dcwv_ref, dcbg_ref, dcbv_ref,
             dwin, d2win, dawin, accu, accd, act, du):
        t = pl.program_id(1)
        first, last = t == 0, t == nt - 1
        zero = jnp.zeros((HALO, D), bf16)
        dwin[0:HALO, :] = jnp.where(first, zero, dp[...])
        dwin[HALO:HALO + tt, :] = dc[...]
        dwin[HALO + tt:, :] = jnp.where(last, zero, dn[...])

        @pl.when(first)
        def _():
            for r in (accu, accd, dcwg_ref, dcwv_ref, dcbg_ref, dcbv_ref):
                r[...] = jnp.zeros_like(r)

        cw = jnp.concatenate([cwg_ref[...], cwv_ref[...]], axis=1)
        split = half + 2 * HALO
        for r0, r1 in ((0, split), (split, wn)):
            dawin[r0:r1, :] = _dg(dwin[r0:r1, :], wd_ref[...], NT)

        def grads(u2v, dact):
            gate, val = u2v[:, :ct], u2v[:, ct:]
            sg = _sigmoid(gate)
            silu = gate * sg
            return dact * val * (sg * (1.0 + gate * (1.0 - sg))), dact * silu, silu * val

        for blk, r0 in ((u2p, 0), (u2n, HALO + tt)):
            dgate, dval, _ = grads(blk[...], dawin[r0:r0 + HALO, :])
            d2win[r0:r0 + HALO, :ct] = dgate
            d2win[r0:r0 + HALO, ct:] = dval
        for p in range(2):
            rows = slice(p * half, (p + 1) * half)
            for r in range(p * half, (p + 1) * half, FFN_SUB):
                dgate, dval, av = grads(u2c[r:r + FFN_SUB, :], dawin[HALO + r:HALO + r + FFN_SUB, :])
                d2win[HALO + r:HALO + r + FFN_SUB, :ct] = dgate
                d2win[HALO + r:HALO + r + FFN_SUB, ct:] = dval
                act[r:r + FFN_SUB, :] = av.astype(bf16)
            accd[...] += _dg(act[rows, :], dc[rows, :], TN)

        def fold8(x):
            out = x[0:8]
            for k in range(8, FFN_SUB, 8):
                out = out + x[k:k + 8]
            return out

        for p in range(2):
            rows = slice(p * half, (p + 1) * half)
            sums = [jnp.zeros((8, 2 * ct), f32) for _ in range(4)]
            for r in range(p * half, (p + 1) * half, FFN_SUB):
                d2c, d2m, d2p = _row_neighbours(d2win, HALO + r, FFN_SUB)
                ucur = uc_ref[r:r + FFN_SUB, :].astype(f32)
                sums[0] = sums[0] + fold8(d2c)
                for kk, dd in enumerate((d2p, d2c, d2m)):
                    sums[1 + kk] = sums[1 + kk] + fold8(ucur * dd)
                du[r:r + FFN_SUB, :] = (d2p * cw[0:1, :] + d2c * cw[1:2, :] + d2m * cw[2:3, :]).astype(bf16)
            dcb = jnp.sum(sums[0], axis=0, keepdims=True)
            dcbg_ref[...] += dcb[:, :ct]
            dcbv_ref[...] += dcb[:, ct:]
            for kk in range(3):
                dck = jnp.sum(sums[1 + kk], axis=0, keepdims=True)
                dcwg_ref[kk:kk + 1, :] += dck[:, :ct]
                dcwv_ref[kk:kk + 1, :] += dck[:, ct:]
            dug_ref[rows, :] = du[rows, :ct]
            duv_ref[rows, :] = du[rows, ct:]
            accu[...] += _dg(du[rows, :], hc[rows, :], TN)

        @pl.when(last)
        def _():
            dwu_ref[0] = accu[0:ct, :].astype(bf16)
            dwu_ref[1] = accu[ct:, :].astype(bf16)
            dwd_ref[...] = accd[...].astype(bf16)

    per = tt // HALO
    prev = lambda w: (lambda c, t: (jnp.maximum(t * per - 1, 0), c if w else 0))
    nxt = lambda w: (lambda c, t: (jnp.minimum((t + 1) * per, T // HALO - 1), c if w else 0))
    tile = lambda: pl.BlockSpec((ct, D), lambda c, t: (c, 0))
    lane = lambda r, off: pl.BlockSpec((r, ct), lambda c, t: (0, c + off))
    return pl.pallas_call(
        body, name="ffn_bwd", grid=(FFN_NC, nt),
        in_specs=[pl.BlockSpec((HALO, D), prev(False)), pl.BlockSpec((tt, D), lambda c, t: (t, 0)), pl.BlockSpec((HALO, D), nxt(False)),
                  pl.BlockSpec((tt, D), lambda c, t: (t, 0)), pl.BlockSpec((tt, 2 * ct), lambda c, t: (t, c)),
                  pl.BlockSpec((HALO, 2 * ct), prev(True)), pl.BlockSpec((tt, 2 * ct), lambda c, t: (t, c)),
                  pl.BlockSpec((HALO, 2 * ct), nxt(True)), lane(3, 0), lane(3, FFN_NC), tile()],
        out_specs=[pl.BlockSpec((tt, ct), lambda c, t: (t, c)), pl.BlockSpec((tt, ct), lambda c, t: (t, c)),
                   pl.BlockSpec((2, ct, D), lambda c, t: (0, c, 0)), tile(), lane(3, 0), lane(3, 0), lane(1, 0), lane(1, 0)],
        out_shape=[jax.ShapeDtypeStruct((T, F), bf16), jax.ShapeDtypeStruct((T, F), bf16),
                   jax.ShapeDtypeStruct((2, F, D), bf16), jax.ShapeDtypeStruct((F, D), bf16),
                   jax.ShapeDtypeStruct((3, F), f32), jax.ShapeDtypeStruct((3, F), f32),
                   jax.ShapeDtypeStruct((1, F), f32), jax.ShapeDtypeStruct((1, F), f32)],
        scratch_shapes=[pltpu.VMEM((wn, D), bf16), pltpu.VMEM((wn, 2 * ct), f32), pltpu.VMEM((wn, ct), f32),
                        pltpu.VMEM((2 * ct, D), f32), pltpu.VMEM((ct, D), f32),
                        pltpu.VMEM((tt, ct), bf16), pltpu.VMEM((tt, 2 * ct), bf16)],
        compiler_params=_cp("parallel", "arbitrary"),
    )(dffn, dffn, dffn, h2, u_t, u2_t, u2_t, u2_t, fcw, fcw, w_down)


def _norm_bwd(dh, xv, gain, sh_sc, rstd):
    xh = xv * rstd
    n = xh * gain
    dn = dh * (1.0 + sh_sc)
    dxh = dn * gain
    dx = rstd * (dxh - xh * jnp.mean(dxh * xh, axis=-1, keepdims=True))
    return (dx, jnp.sum(dh, axis=0, keepdims=True), jnp.sum(dh * n, axis=0, keepdims=True),
            jnp.sum(dn * xh, axis=0, keepdims=True))


def _norm2_bwd(dug, duv, w_upT, x1, dx2, proj, w_out, y_na, y_cv, modv, g2):
    tm = 512
    nt = T // tm

    def body(dug_ref, duv_ref, w_ref, x1_ref, dx2_ref, pj_ref, wo_ref, ya_ref, yc_ref, mod_ref, g_ref,
             dx1_ref, dya_ref, dyc_ref, dwo_ref, s_ref, acc):
        i = pl.program_id(0)

        @pl.when(i == 0)
        def _():
            s_ref[...] = jnp.zeros_like(s_ref)
            acc[...] = jnp.zeros_like(acc)

        dh2 = _dot(dug_ref[...], w_ref[0:F, :]) + _dot(duv_ref[...], w_ref[F:F2, :])
        x1 = x1_ref[...]
        rstd = lax.rsqrt(jnp.mean(x1 * x1, axis=-1, keepdims=True) + EPS)
        dxn, dsh, dsc, dgn = _norm_bwd(dh2, x1, g_ref[...], mod_ref[4:5, :], rstd)
        dx1 = dx2_ref[...] + dxn
        dx1_ref[...] = dx1
        dpj = (dx1 * mod_ref[2:3, :]).astype(bf16)
        dyc = _dg(dpj, wo_ref[...], NT)
        dya_ref[...] = dyc[:, :DA].astype(bf16)
        dyc_ref[...] = dyc[:, DA:]
        acc[0:DA, :] += _dg(ya_ref[...], dpj, TN)
        acc[DA:D, :] += _dg(yc_ref[...], dpj, TN)

        @pl.when(i == nt - 1)
        def _():
            dwo_ref[...] = acc[...].astype(bf16)

        s_ref[0:1, :] += dsh
        s_ref[1:2, :] += dsc
        s_ref[2:3, :] += dgn
        s_ref[3:4, :] += jnp.sum(dx1 * pj_ref[...].astype(f32), axis=0, keepdims=True)

    row = lambda w: pl.BlockSpec((tm, w), lambda i: (i, 0))
    return pl.pallas_call(
        body, name="norm2_bwd", grid=(nt,),
        in_specs=[row(F), row(F), _resident((F2, D)), row(D), row(D), row(D), _resident((D, D)), row(DA), row(DA),
                  _full((8, D)), _full((1, D))],
        out_specs=[row(D), row(DA), row(DA), _full((D, D)), _full((8, D))],
        out_shape=[jax.ShapeDtypeStruct((T, D), f32), jax.ShapeDtypeStruct((T, DA), bf16), jax.ShapeDtypeStruct((T, DA), f32),
                   jax.ShapeDtypeStruct((D, D), bf16), jax.ShapeDtypeStruct((8, D), f32)],
        scratch_shapes=[pltpu.VMEM((D, D), f32)],
        compiler_params=_cp("arbitrary"),
    )(dug, duv, w_upT, x1, dx2, proj, w_out, y_na, y_cv, modv, g2)


def _conf_bwd(a, g, cv, dy, conv_w, ln_g, ln_b, blocks):
    tt = CONV_TT
    nt = T // tt
    sub = 32
    wn = tt + 2 * HALO
    nb = len(blocks)

    def body(ap, ac, an, gp, gc, gn, cp_, cc, cn, dp, dc, dn, w_ref, lg_ref, lb_ref, *rest):
        g_refs, (da_ref, dg_ref, dcw_ref, s_ref) = rest[:nb], rest[nb:nb + 4]
        recv_refs, (urot, drot, wacc), a2a_sems = rest[nb + 4:2 * nb + 4], rest[2 * nb + 4:2 * nb + 7], rest[2 * nb + 7:]
        i = pl.program_id(0)
        first, last = i == 0, i == nt - 1
        plans = [_a2a_plan(g_refs[k], recv_refs[k], *a2a_sems[3 * k:3 * k + 3]) for k in range(nb)]
        for start, _ in plans:
            pl.when(first)(start)

        @pl.when(first)
        def _():
            s_ref[...] = jnp.zeros_like(s_ref)
            wacc[...] = jnp.zeros_like(wacc)

        lg, lb = lg_ref[...], lb_ref[...]

        def ln_bwd(cvv, dyv):
            mu = jnp.mean(cvv, axis=-1, keepdims=True)
            xc = cvv - mu
            rstd = lax.rsqrt(jnp.mean(xc * xc, axis=-1, keepdims=True) + EPS)
            yn = xc * rstd
            z = yn * lg + lb
            sz = _sigmoid(z)
            dz = dyv * (sz * (1.0 + z * (1.0 - sz)))
            dyn = dz * lg
            dcv = rstd * (dyn - jnp.mean(dyn, axis=-1, keepdims=True) - yn * jnp.mean(dyn * yn, axis=-1, keepdims=True))
            return dcv, dz, yn

        urot[0, 0:HALO, :] = jnp.where(first, 0.0, ap[...] * _sigmoid(gp[...]))
        urot[0, HALO + tt:, :] = jnp.where(last, 0.0, an[...] * _sigmoid(gn[...]))
        drot[0, 0:HALO, :] = jnp.where(first, 0.0, ln_bwd(cp_[...], dp[...])[0])
        drot[0, HALO + tt:, :] = jnp.where(last, 0.0, ln_bwd(cn[...], dn[...])[0])
        for s in range(tt // sub):
            rr = pl.ds(s * sub, sub)
            urot[0, pl.ds(HALO + s * sub, sub), :] = ac[rr, :] * _sigmoid(gc[rr, :])
            dcv, dz, yn = ln_bwd(cc[rr, :], dc[rr, :])
            drot[0, pl.ds(HALO + s * sub, sub), :] = dcv
            s_ref[0:1, :] += jnp.sum(dcv, axis=0, keepdims=True)
            s_ref[1:2, :] += jnp.sum(dz * yn, axis=0, keepdims=True)
            s_ref[2:3, :] += jnp.sum(dz, axis=0, keepdims=True)
        _shifted_copies(urot, wn)
        _shifted_copies(drot, wn)
        w = w_ref[...]
        for s in range(tt // sub):
            rr = pl.ds(s * sub, sub)
            dcv = drot[0, pl.ds(HALO + s * sub, sub), :]
            acc = jnp.zeros((sub, DA), f32)
            for j in range(CW):
                ad, bd = divmod(2 * HALO - 1 - j, 8)
                au, bu = divmod(1 + j, 8)
                acc = acc + drot[bd, pl.ds(s * sub + 8 * ad, sub), :] * w[j:j + 1, :]
                part = urot[bu, pl.ds(s * sub + 8 * au, sub), :] * dcv
                wacc[j] += part[0:8] + part[8:16] + part[16:24] + part[24:32]
            av, gv = ac[rr, :], gc[rr, :]
            sg = _sigmoid(gv)
            da_ref[rr, :] = (acc * sg).astype(bf16)
            dg_ref[rr, :] = (acc * av * sg * (1.0 - sg)).astype(bf16)

        @pl.when(last)
        def _():
            for j in range(CW):
                dcw_ref[j:j + 1, :] = jnp.sum(wacc[j], axis=0, keepdims=True)
            dcw_ref[CW:CW + 1, :] = jnp.zeros((1, DA), f32)

        for _, finish in plans:
            pl.when(last)(finish)

    hs = _halo_specs(tt, DA, T // HALO)
    hbm = pl.BlockSpec(memory_space=pl.ANY)
    return pl.pallas_call(
        body, name="conf_bwd", grid=(nt,),
        in_specs=hs * 4 + [_full((CW, DA)), _full((1, DA)), _full((1, DA))] + [hbm] * nb,
        out_specs=[pl.BlockSpec((tt, DA), lambda i: (i, 0)), pl.BlockSpec((tt, DA), lambda i: (i, 0)),
                   _full((CW + 1, DA)), _full((8, DA))] + [hbm] * nb,
        out_shape=[jax.ShapeDtypeStruct((T, DA), bf16), jax.ShapeDtypeStruct((T, DA), bf16),
                   jax.ShapeDtypeStruct((CW + 1, DA), f32), jax.ShapeDtypeStruct((8, DA), f32)]
                  + [jax.ShapeDtypeStruct(b.shape, b.dtype) for b in blocks],
        scratch_shapes=[pltpu.VMEM((8, wn, DA), f32), pltpu.VMEM((8, wn, DA), f32), pltpu.VMEM((CW, 8, DA), f32)]
                       + _a2a_scratch() * nb,
        compiler_params=_cp("arbitrary"),
    )(a, a, a, g, g, g, cv, cv, cv, dy, dy, dy, conv_w, ln_g, ln_b, *blocks)


def _attn_bwd(q, k, v, y, dy, lse, bias_tab, blocks):
    zr = 256
    nb = len(blocks)

    def body(q_ref, k_ref, v_ref, y_ref, dy_ref, lse_ref, b_ref, *rest):
        g_refs, (dq_ref, dk_hbm, dv_hbm, db_ref) = rest[:nb], rest[nb:nb + 4]
        recv_refs, (dk_s, dv_s, sem), a2a_sems = rest[nb + 4:2 * nb + 4], rest[2 * nb + 4:2 * nb + 7], rest[2 * nb + 7:]
        r = pl.program_id(0)
        plans = [_a2a_plan(g_refs[i], recv_refs[i], *a2a_sems[3 * i:3 * i + 3]) for i in range(nb)]
        for start, _ in plans:
            pl.when(r == 0)(start)

        @pl.when(r == 0)
        def _():
            def z(i, _):
                rr = pl.ds(pl.multiple_of(i * zr, zr), zr)
                dk_s[rr, :] = jnp.zeros((zr, DA), f32)
                dv_s[rr, :] = jnp.zeros((zr, DA), f32)
                return 0
            lax.fori_loop(0, TA // zr, z, 0)

        @pl.when((r <= WR // 2) | (r > GW - WR // 2))
        def _():
            db_ref[...] = jnp.zeros_like(db_ref)

        ks = pl.multiple_of(_win_start(r) * GW, GW)
        win = pl.ds(ks, WR * GW)
        qq, yy, dyy, lse_v = q_ref[...], y_ref[...], dy_ref[...], lse_ref[...]
        lo = lax.broadcasted_iota(jnp.int32, (GW, 2 * HD), 1) < HD
        ops, pairs = [], []
        for pr in range(NH // 2):
            ps = slice(pr * 2 * HD, (pr + 1) * 2 * HD)
            q2, do2 = qq[:, ps], dyy[:, ps]
            prod = do2.astype(f32) * yy[:, ps].astype(f32)
            delta = jnp.concatenate([jnp.sum(jnp.where(lo, prod, 0.0), axis=-1, keepdims=True),
                                     jnp.sum(jnp.where(lo, 0.0, prod), axis=-1, keepdims=True)], axis=0)
            qst, dost = _stack_pair(q2, lo), _stack_pair(do2, lo)
            kw, vw = k_ref[win, ps], v_ref[win, ps]
            kc, vc = k_ref[T:TA, ps], v_ref[T:TA, ps]
            ops.append((kw, kc))
            pairs.append((qst, dost, delta, _dg(qst, kw, NT), _dg(qst, kc, NT), _dg(dost, vw, NT), _dg(dost, vc, NT)))
        grads = []
        for pr, (qst, dost, delta, sl, sc, dpl, dpc) in enumerate(pairs):
            lh = jnp.concatenate([lse_v[:, 2 * pr:2 * pr + 1], lse_v[:, 2 * pr + 1:2 * pr + 2]], axis=0)
            bias2 = b_ref[0, 2 * pr:2 * pr + 2].reshape(2 * GW, WR * GW)
            pl_ = jnp.exp(sl * SCALE + bias2 - lh)
            pc = jnp.exp(sc * SCALE - lh)
            dsl = pl_ * (dpl - delta)
            dsc = pc * (dpc - delta)
            db_ref[0, 2 * pr:2 * pr + 2] += dsl.reshape(2, GW, WR * GW)
            grads.append((qst, dost, pl_.astype(bf16), pc.astype(bf16), dsl.astype(bf16), dsc.astype(bf16)))
        for pr in range(NH // 2):
            ps = slice(pr * 2 * HD, (pr + 1) * 2 * HD)
            kw, kc = ops[pr]
            qst, dost, plb, pcb, dslb, dscb = grads[pr]
            dqst = _dot(dslb, kw) + _dot(dscb, kc)
            dq_ref[:, ps] = (jnp.where(lo, dqst[0:GW], dqst[GW:]) * SCALE).astype(bf16)
            dk_s[win, ps] += _dg(dslb, qst, TN) * SCALE
            dv_s[win, ps] += _dg(plb, dost, TN)
            dk_s[T:TA, ps] += _dg(dscb, qst, TN) * SCALE
            dv_s[T:TA, ps] += _dg(pcb, dost, TN)

        @pl.when(r == GW - 1)
        def _():
            c1 = pltpu.make_async_copy(dk_s, dk_hbm, sem.at[0])
            c2 = pltpu.make_async_copy(dv_s, dv_hbm, sem.at[1])
            c1.start()
            c2.start()
            c1.wait()
            c2.wait()

        for _, finish in plans:
            pl.when(r == GW - 1)(finish)

    rowq = lambda: pl.BlockSpec((GW, DA), lambda r: (r, 0))
    hbm = pl.BlockSpec(memory_space=pl.ANY)
    return pl.pallas_call(
        body, name="attn_bwd", grid=(GW,),
        in_specs=[rowq(), _full((TA, DA)), _full((TA, DA)), rowq(), rowq(), pl.BlockSpec((GW, NH), lambda r: (r, 0)),
                  pl.BlockSpec((1, NH, GW, WR * GW), lambda r: (_pattern(r), 0, 0, 0))] + [hbm] * nb,
        out_specs=[rowq(), hbm, hbm, pl.BlockSpec((1, NH, GW, WR * GW), lambda r: (_pattern(r), 0, 0, 0))] + [hbm] * nb,
        out_shape=[jax.ShapeDtypeStruct((T, DA), bf16), jax.ShapeDtypeStruct((TA, DA), f32), jax.ShapeDtypeStruct((TA, DA), f32),
                   jax.ShapeDtypeStruct((8, NH, GW, WR * GW), f32)] + [jax.ShapeDtypeStruct(b.shape, b.dtype) for b in blocks],
        scratch_shapes=[pltpu.VMEM((TA, DA), f32), pltpu.VMEM((TA, DA), f32), pltpu.SemaphoreType.DMA((2,))] + _a2a_scratch() * nb,
        compiler_params=_cp("arbitrary"),
    )(q, k, v, y, dy, lse, bias_tab, *blocks)


def _rpb_reduce(dbias):
    rev = np.eye(GW, dtype=np.float32)[::-1]

    def body(d_ref, rev_ref, o_ref):
        rv = rev_ref[...]
        for h in range(NH):
            dv = d_ref[0, h]
            r0 = dv.astype(bf16)
            e1 = dv - r0.astype(f32)
            r1 = e1.astype(bf16)
            r2 = (e1 - r1.astype(f32)).astype(bf16)
            rr = _dot(rv, r0) + _dot(rv, r1) + _dot(rv, r2)
            skew = pltpu.roll(rr, 0, 1, stride=1, stride_axis=0)
            o_ref[0, h:h + 1, :] = jnp.sum(skew, axis=0, keepdims=True)

    return pl.pallas_call(
        body, name="rpb_reduce", grid=(8,),
        in_specs=[pl.BlockSpec((1, NH, GW, WR * GW), lambda p: (p, 0, 0, 0)), _full((GW, GW))],
        out_specs=pl.BlockSpec((1, NH, WR * GW), lambda p: (p, 0, 0)),
        out_shape=jax.ShapeDtypeStruct((8, NH, WR * GW), f32),
        compiler_params=_cp("parallel"),
    )(dbias, jnp.asarray(rev, dtype=bf16))


def _norm1_bwd(dq, dk, dv, da, dg, w_inT, x0, ctx0, h, dx1, modv, g1):
    tm = 256
    nt = TA // tm
    nx = T // tm

    def body(dq_ref, dk_ref, dv_ref, da_ref, dg_ref, w_ref, x_ref, c_ref, h_ref, dx1_ref, mod_ref, g_ref,
             dx_ref, dwo_ref, s_ref, dw_ref):
        i = pl.program_id(0)
        is_ctx = i == nt - 1

        @pl.when(i == 0)
        def _():
            s_ref[...] = jnp.zeros_like(s_ref)
            dw_ref[...] = jnp.zeros_like(dw_ref)

        hb = h_ref[...]
        dkb, dvb = dk_ref[...].astype(bf16), dv_ref[...].astype(bf16)
        dw_ref[DA:2 * DA, :] += _dg(dkb, hb, TN)
        dw_ref[2 * DA:3 * DA, :] += _dg(dvb, hb, TN)
        dh_kv = _dot(dkb, w_ref[DA:2 * DA, :]) + _dot(dvb, w_ref[2 * DA:3 * DA, :])
        gain = g_ref[...]

        @pl.when(is_ctx)
        def _():
            xv = c_ref[...]
            rstd = lax.rsqrt(jnp.mean(xv * xv, axis=-1, keepdims=True) + EPS)
            _, dsh, dsc, dgn = _norm_bwd(dh_kv, xv, gain, mod_ref[7:8, :], rstd)
            s_ref[2:3, :] += dgn
            s_ref[3:4, :] += dsh
            s_ref[4:5, :] += dsc
            dwo_ref[...] = dw_ref[...].astype(bf16)

        @pl.when(jnp.logical_not(is_ctx))
        def _():
            dqb, dab, dgb = dq_ref[...], da_ref[...], dg_ref[...]
            dw_ref[0:DA, :] += _dg(dqb, hb, TN)
            dw_ref[3 * DA:4 * DA, :] += _dg(dab, hb, TN)
            dw_ref[4 * DA:5 * DA, :] += _dg(dgb, hb, TN)
            dh = (dh_kv + _dot(dqb, w_ref[0:DA, :]) + _dot(dab, w_ref[3 * DA:4 * DA, :])
                  + _dot(dgb, w_ref[4 * DA:5 * DA, :]))
            xv = x_ref[...]
            rstd = lax.rsqrt(jnp.mean(xv * xv, axis=-1, keepdims=True) + EPS)
            dxn, dsh, dsc, dgn = _norm_bwd(dh, xv, gain, mod_ref[1:2, :], rstd)
            dx_ref[...] = dx1_ref[...] + dxn
            s_ref[0:1, :] += dsh
            s_ref[1:2, :] += dsc
            s_ref[2:3, :] += dgn

    row = lambda w: pl.BlockSpec((tm, w), lambda i: (i, 0))
    lrow = lambda w: pl.BlockSpec((tm, w), lambda i: (jnp.minimum(i, nx - 1), 0))
    return pl.pallas_call(
        body, name="norm1_bwd", grid=(nt,),
        in_specs=[lrow(DA), row(DA), row(DA), lrow(DA), lrow(DA), _full((5 * DA, D)), lrow(D), _full((TC, D)), row(D),
                  lrow(D), _full((8, D)), _full((1, D))],
        out_specs=[lrow(D), _full((5 * DA, D)), _full((8, D))],
        out_shape=[jax.ShapeDtypeStruct((T, D), f32), jax.ShapeDtypeStruct((5 * DA, D), bf16), jax.ShapeDtypeStruct((8, D), f32)],
        scratch_shapes=[pltpu.VMEM((5 * DA, D), f32)],
        compiler_params=_cp("arbitrary"),
    )(dq, dk, dv, da, dg, w_inT, x0, ctx0, h, dx1, modv, g1)


def _adam_tile(r, c):
    return max(t for t in range(8, r + 1, 8) if r % t == 0 and t * c * 4 <= 2 * 1024 * 1024)


def _adam_update(wv, gv, mv, vv):
    nm = ADAM_B1 * mv + (1.0 - ADAM_B1) * gv
    nv = ADAM_B2 * vv + (1.0 - ADAM_B2) * (gv * gv)
    m_hat = nm * (1.0 / (1.0 - ADAM_B1 ** ADAM_STEP))
    v_hat = nv * (1.0 / (1.0 - ADAM_B2 ** ADAM_STEP))
    return -ADAM_LR * (m_hat / (jnp.sqrt(v_hat) + ADAM_EPS) + ADAM_WD * wv), nm, nv


def _adamw(w, g, m, v, name):
    r, c = w.shape
    tr = _adam_tile(r, c)

    def body(w_ref, g_ref, m_ref, v_ref, d_ref, nm_ref, nv_ref):
        d_ref[...], nm_ref[...], nv_ref[...] = _adam_update(w_ref[...], g_ref[...], m_ref[...], v_ref[...])

    spec = pl.BlockSpec((tr, c), lambda i: (i, 0))
    return pl.pallas_call(
        body, name=name, grid=(r // tr,),
        in_specs=[spec] * 4, out_specs=[spec] * 3,
        out_shape=[jax.ShapeDtypeStruct((r, c), f32)] * 3,
        compiler_params=_cp("parallel"),
    )(w, g, m, v)


def _adamw_blocks(w, recv, m, v, name):
    r, c = w.shape
    tr = _adam_tile(r, c)

    def body(w_ref, a_ref, m_ref, v_ref, g_ref, d_ref, nm_ref, nv_ref):
        gv = a_ref[0].astype(f32)
        for d in range(1, NDEV):
            gv = gv + a_ref[d].astype(f32)
        g_ref[...] = gv
        d_ref[...], nm_ref[...], nv_ref[...] = _adam_update(w_ref[...], gv, m_ref[...], v_ref[...])

    spec = pl.BlockSpec((tr, c), lambda i: (i, 0))
    return pl.pallas_call(
        body, name=name, grid=(r // tr,),
        in_specs=[spec, pl.BlockSpec((NDEV, tr, c), lambda i: (0, i, 0)), spec, spec], out_specs=[spec] * 4,
        out_shape=[jax.ShapeDtypeStruct((r, c), f32)] * 4,
        compiler_params=_cp("parallel"),
    )(w, recv, m, v)


def _pad_rows128(vec):
    n = vec.shape[0]
    rows = -(-n // 1024) * 8
    return jnp.pad(vec, (0, rows * 128 - n)).reshape(rows, 128)


def _grad_rpb(dbias):
    lane_map, r_hot = _rpb_tables()
    return jnp.einsum("phl,lic,pir->hrc", _rpb_reduce(dbias), jnp.asarray(lane_map), jnp.asarray(r_hot),
                      precision=lax.Precision.HIGHEST)


def kernel(x, c, ctx, c_ctx, w_mod, b_mod, g_norm1, w_in, rpb, conv_w, conv_b, ln_g, ln_b, w_out, g_norm2, w_up, ffn_conv_w, ffn_conv_b, w_down, g_final, loss_target, m_c_ctx, m_w_mod, m_b_mod, m_g_norm1, m_w_in, m_rpb, m_conv_w, m_conv_b, m_ln_g, m_ln_b, m_w_out, m_g_norm2, m_w_up, m_ffn_conv_w, m_ffn_conv_b, m_w_down, m_g_final, v_c_ctx, v_w_mod, v_b_mod, v_g_norm1, v_w_in, v_rpb, v_conv_w, v_conv_b, v_ln_g, v_ln_b, v_w_out, v_g_norm2, v_w_up, v_ffn_conv_w, v_ffn_conv_b, v_w_down, v_g_final):
    me = 4 * lax.axis_index("x") + 2 * lax.axis_index("y") + lax.axis_index("c")
    nmod = w_mod.shape[2]
    n_in = w_in.shape[2]
    n_out = w_out.shape[1]
    n_up = w_up.shape[2]
    n_dn = w_down.shape[1]
    n_cw = conv_w.shape[2]

    b_sh = lax.dynamic_slice(b_mod, (0, me * nmod), (1, nmod))
    convpay = _pad_rows128(jnp.concatenate([conv_w[0].reshape(-1), ffn_conv_w[0].reshape(-1)]))
    w_inT, c_all, mod_all, flat = _head(w_in[0].T.astype(bf16), c.reshape(8, 128), c_ctx.reshape(8, 128), w_mod[0], b_sh, convpay)
    cvec = jnp.concatenate([c_all.reshape(NDEV, D), c_ctx[None, :], jnp.zeros((7, D), f32)], axis=0)
    mod_all = mod_all.transpose(1, 0, 2).reshape(16, 6 * D)
    mod_me = lax.dynamic_index_in_dim(mod_all, me, 0, keepdims=False).reshape(6, D)
    mod_c = mod_all[8]
    modv = jnp.concatenate([mod_me, mod_c[None, 0:D], mod_c[None, D:2 * D]], axis=0)
    flat = flat.reshape(NDEV, -1)
    o1 = CW * n_cw
    conv_w_f = flat[:, :o1].reshape(NDEV, CW, n_cw).transpose(1, 0, 2).reshape(CW, DA)
    fcw_f = flat[:, o1:o1 + 3 * n_up].reshape(NDEV, 3, n_up).transpose(1, 0, 2).reshape(3, F2)

    x0, ctx0 = x[0], ctx[0]
    h, q, k, v, a, g, w_down_f = _in_proj(x0, ctx0, g_norm1, modv, w_inT, [w_down[0].astype(bf16)])
    bias_tab = _bias_table(rpb[0])
    y_cv, cv, w_out_f = _conf_fwd(a, g, conv_w_f, conv_b, ln_g, ln_b, [w_out[0].astype(bf16)])
    y_na, lse, w_upT = _attn_fwd(q, k, v, bias_tab, [w_up[0].T.astype(bf16)])
    x1, proj, h2 = _out_proj(x0, y_na, y_cv, w_out_f, modv, g_norm2)
    ffn, u_t, u2_t = _ffn_fwd(h2, w_upT, fcw_f, ffn_conv_b, w_down_f)
    dx2, dffn, s_loss = _loss_bwd(ffn, x1, loss_target[0], modv, g_final[None, :])

    dug, duv, dw_up, dw_down, dcwg, dcwv, dcbg, dcbv = _ffn_bwd(h2, dffn, u_t, u2_t, fcw_f, w_down_f)
    dx1, dy_na, dy_cv, dw_out, s_n2 = _norm2_bwd(dug, duv, w_upT, x1, dx2, proj, w_out_f, y_na, y_cv, modv, g_norm2)
    da, dg, dcw, s_cf, rv_down = _conf_bwd(a, g, cv, dy_cv, conv_w_f, ln_g, ln_b, [dw_down.reshape(NDEV, n_dn, D)])
    dq, dk, dv, dbias, rv_up = _attn_bwd(q, k, v, y_na, dy_na, lse, bias_tab, [dw_up.reshape(NDEV, n_up, D)])
    grad_rpb_part = _grad_rpb(dbias)
    grad_x, dw_inT, s_n1 = _norm1_bwd(dq, dk, dv, da, dg, w_inT, x0, ctx0, h, dx1, modv, g_norm1)
    grad_x = grad_x[None]
    dfcw = jnp.concatenate([dcwg, dcwv], axis=1)
    dfcb = jnp.concatenate([dcbg[0], dcbv[0]])
    small = jnp.concatenate([dcw[:CW].reshape(CW, NDEV, n_cw).transpose(1, 0, 2).reshape(NDEV, CW * n_cw),
                             dfcw.reshape(3, NDEV, n_up).transpose(1, 0, 2).reshape(NDEV, 3 * n_up)], axis=1)
    small = jnp.pad(small.reshape(NDEV, 4, D), ((0, 0), (0, 12), (0, 0))).astype(bf16)
    dmod = jnp.concatenate([s_n1[0], s_n1[1], s_n2[3], s_n2[0], s_n2[1], s_loss[1]])
    dmodc = jnp.concatenate([s_n1[3], s_n1[4]])
    parts = [dmodc, s_n1[2], grad_rpb_part.reshape(-1), s_cf[0], s_cf[1], s_cf[2], s_n2[2], dfcb, s_loss[0], s_loss[3, 0:1]]
    sizes = [p.shape[0] for p in parts]
    pvec = _pad_rows128(jnp.concatenate([dmod] + parts))
    slab = jnp.concatenate([dw_inT.reshape(NDEV, n_in, D), small, dw_out.reshape(NDEV, n_out, D)], axis=1)
    r_a, gath = _reduce_scatter2(slab, [pvec], "rs_w_in")
    g_w_out = r_a[n_in + 16:]
    g_w_inT = r_a[:n_in]
    sm = r_a[n_in:n_in + 4].reshape(-1)
    g_conv_w = sm[:CW * n_cw].reshape(CW, n_cw)
    g_fcw = sm[CW * n_cw:].reshape(3, n_up)
    tot = _sum_rows8(gath, "sum_small").reshape(-1)
    dmod_all = gath.reshape(NDEV, -1)[:, :6 * D]
    offs = np.cumsum([6 * D] + sizes)
    pick = lambda j: tot[offs[j]:offs[j + 1]]
    dmodc_t = jnp.pad(pick(0), (0, 4 * D))
    g_b_mod = (tot[:6 * D] + dmodc_t)[None, :]
    g_g_norm1 = pick(1)[None, :]
    g_rpb = pick(2).reshape(1, NH, 2 * WR - 1, 2 * NCOL - 1)
    g_conv_b, g_ln_g, g_ln_b = pick(3)[None, :], pick(4)[None, :], pick(5)[None, :]
    g_g_norm2 = pick(6)[None, :]
    g_fcb = pick(7)[None, :]
    g_g_final = pick(8)
    loss = pick(9)[0]
    dm_rows = jnp.concatenate([dmod_all, dmodc_t[None, :], jnp.zeros((7, 6 * D), f32)], axis=0)
    dm_sh = lax.dynamic_slice(dm_rows, (0, me * nmod), (16, nmod))
    g_w_mod, gc_part = _mod_bwd(cvec, dm_sh, w_mod[0])
    gc_sum = _sum_rows8(_small_allgather(gc_part[0].reshape(8, 128), "ag_cctx"), "sum_cctx").reshape(D)
    sg_c = _sigmoid(c_ctx)
    g_c_ctx = gc_sum * (sg_c * (1.0 + c_ctx * (1.0 - sg_c)))

    big = [("w_mod", w_mod[0], g_w_mod, m_w_mod[0], v_w_mod[0]), ("w_in", w_in[0].T, g_w_inT, m_w_in[0].T, v_w_in[0].T),
           ("w_out", w_out[0], g_w_out, m_w_out[0], v_w_out[0])]
    upd = {n: _adamw(wv, gv, mv, vv, "adamw_" + n) for n, wv, gv, mv, vv in big}
    g_w_upT, *upd["w_up"] = _adamw_blocks(w_up[0].T, rv_up, m_w_up[0].T, v_w_up[0].T, "adamw_w_up")
    g_w_down, *upd["w_down"] = _adamw_blocks(w_down[0], rv_down, m_w_down[0], v_w_down[0], "adamw_w_down")
    for n in ("w_in", "w_up"):
        upd[n] = tuple(arr.T for arr in upd[n])
    g_w_in, g_w_up = g_w_inT.T, g_w_upT.T
    smalls = [("c_ctx", c_ctx, g_c_ctx, m_c_ctx, v_c_ctx), ("b_mod", b_mod, g_b_mod, m_b_mod, v_b_mod),
              ("g_norm1", g_norm1, g_g_norm1, m_g_norm1, v_g_norm1), ("rpb", rpb, g_rpb, m_rpb, v_rpb),
              ("conv_w", conv_w, g_conv_w[None], m_conv_w, v_conv_w), ("conv_b", conv_b, g_conv_b, m_conv_b, v_conv_b),
              ("ln_g", ln_g, g_ln_g, m_ln_g, v_ln_g), ("ln_b", ln_b, g_ln_b, m_ln_b, v_ln_b),
              ("g_norm2", g_norm2, g_g_norm2, m_g_norm2, v_g_norm2),
              ("ffn_conv_w", ffn_conv_w, g_fcw[None], m_ffn_conv_w, v_ffn_conv_w),
              ("ffn_conv_b", ffn_conv_b, g_fcb, m_ffn_conv_b, v_ffn_conv_b), ("g_final", g_final, g_g_final, m_g_final, v_g_final)]
    packed = [_pad_rows128(jnp.concatenate([t[j].reshape(-1) for t in smalls])) for j in (1, 2, 3, 4)]
    sd, sm_, sv = _adamw(*packed, "adamw_small")
    so = np.cumsum([0] + [int(np.prod(t[1].shape)) for t in smalls])
    for j, t in enumerate(smalls):
        shp = t[1].shape
        upd[t[0]] = tuple(arr.reshape(-1)[so[j]:so[j + 1]].reshape(shp) for arr in (sd, sm_, sv))
    grads = {"c_ctx": g_c_ctx, "w_mod": g_w_mod[None], "b_mod": g_b_mod, "g_norm1": g_g_norm1, "w_in": g_w_in[None],
             "rpb": g_rpb, "conv_w": g_conv_w[None], "conv_b": g_conv_b, "ln_g": g_ln_g, "ln_b": g_ln_b,
             "w_out": g_w_out[None], "g_norm2": g_g_norm2, "w_up": g_w_up[None], "ffn_conv_w": g_fcw[None],
             "ffn_conv_b": g_fcb, "w_down": g_w_down[None], "g_final": g_g_final}
    names = ["c_ctx", "w_mod", "b_mod", "g_norm1", "w_in", "rpb", "conv_w", "conv_b", "ln_g", "ln_b", "w_out", "g_norm2",
             "w_up", "ffn_conv_w", "ffn_conv_b", "w_down", "g_final"]
    shapes = {n: grads[n].shape for n in names}
    outs = [loss, grad_x] + [grads[n] for n in names]
    for j in range(3):
        outs += [upd[n][j].reshape(shapes[n]) for n in names]
    return tuple(outs)
```

```python
import numpy as np
import jax
import jax.numpy as jnp
from jax import lax
from jax.experimental import pallas as pl
from jax.experimental.pallas import tpu as pltpu

f32 = jnp.float32
bf16 = jnp.bfloat16

D = 1024
T = 4096
TC = 256
TA = T + TC
DA = 512
NH = 8
HD = 64
GW = 64
WR = 8
NCOL = 16
F = 2816
F2 = 2 * F
CW = 31
NDEV = 8
EPS = 1e-6
SCALE = HD ** -0.5
NEG = -1e30
MESH = pl.DeviceIdType.MESH

NT = (((1,), (1,)), ((), ()))
TN = (((0,), (0,)), ((), ()))

ADAM_LR, ADAM_B1, ADAM_B2, ADAM_EPS, ADAM_WD, ADAM_STEP = 0.001, 0.9, 0.999, 1e-08, 0.01, 10

VMEM_LIMIT = 56 * 1024 * 1024


def _cp(*sem):
    return pltpu.CompilerParams(dimension_semantics=sem or None, vmem_limit_bytes=VMEM_LIMIT)


def _dot(a, b):
    return jnp.dot(a, b, preferred_element_type=f32)


def _dg(a, b, dims):
    return lax.dot_general(a, b, dims, preferred_element_type=f32)


def _sigmoid(x):
    return 1.0 / (1.0 + jnp.exp(-x))


def _full(shape):
    n = len(shape)
    return pl.BlockSpec(shape, lambda *_: (0,) * n)


def _resident(shape):
    n = len(shape)
    return pl.BlockSpec(shape, lambda *_: (0,) * n, pipeline_mode=pl.Buffered(1))


def _my_pos():
    return lax.axis_index("x"), lax.axis_index("y"), lax.axis_index("c")


def _small_gather_plan(v_ref, out_ref, send_sems, recv_sems):
    x, y, c = _my_pos()
    me = 4 * x + 2 * y + c
    peers = []
    for k in range(1, NDEV):
        kx, ky, kc = (k >> 2) & 1, (k >> 1) & 1, k & 1
        peers.append((x ^ kx, y ^ ky, c ^ kc))

    def copy(k, slot, to):
        return pltpu.make_async_remote_copy(
            src_ref=v_ref, dst_ref=out_ref.at[slot], send_sem=send_sems.at[k], recv_sem=recv_sems.at[k],
            device_id=to, device_id_type=MESH)

    def start():
        out_ref[me] = v_ref[...]
        for k, p in enumerate(peers):
            copy(k, me, p).start()

    def finish():
        for k, (px, py, pc) in enumerate(peers):
            copy(k, 4 * px + 2 * py + pc, (x, y, c)).wait_recv()
        for k, p in enumerate(peers):
            copy(k, me, p).wait_send()

    return start, finish


def _small_gather_scratch():
    return [pltpu.SemaphoreType.DMA((NDEV - 1,)), pltpu.SemaphoreType.DMA((NDEV - 1,))]


def _small_allgather(v, name):
    n = v.shape[0]

    def body(v_ref, out_ref, send_sems, recv_sems):
        start, finish = _small_gather_plan(v_ref, out_ref, send_sems, recv_sems)
        start()
        finish()

    return pl.pallas_call(
        body, name=name,
        out_shape=jax.ShapeDtypeStruct((NDEV, n, 128), f32),
        in_specs=[pl.BlockSpec(memory_space=pltpu.VMEM)],
        out_specs=pl.BlockSpec(memory_space=pltpu.VMEM),
        scratch_shapes=_small_gather_scratch(),
    )(v)


def _ag2_plan(x_refs, out_refs, send_sems, recv_sems, local_sems):
    na = len(x_refs)
    x, y, c = _my_pos()
    me, sibling = (x, y, c), (x, y, 1 - c)
    chips = [(1 - x, y), (x, 1 - y), (1 - x, 1 - y)]

    def rows(i, px, py, pc):
        m_per = x_refs[i].shape[0]
        return out_refs[i].at[pl.ds(pl.multiple_of((4 * px + 2 * py + pc) * m_per, 16 if m_per % 16 == 0 else 8), m_per), :]

    def copies(k, block, to, from_shard=False):
        return [pltpu.make_async_remote_copy(
            src_ref=x_refs[i] if from_shard else rows(i, *block), dst_ref=rows(i, *block),
            send_sem=send_sems.at[k * na + i], recv_sem=recv_sems.at[k * na + i], device_id=to, device_id_type=MESH)
            for i in range(na)]

    def mine():
        return [pltpu.make_async_copy(x_refs[i], rows(i, *me), local_sems.at[i]) for i in range(na)]

    def first():
        cps = copies(0, me, sibling, True)
        for j, chip in enumerate(chips):
            cps += copies(1 + j, me, (*chip, c), True)
        return cps

    def start():
        for cp in mine() + first():
            cp.start()

    def forward():
        for j, chip in enumerate(chips):
            for cp in copies(1 + j, (*chip, c), me):
                cp.wait_recv()
            for cp in copies(4 + j, (*chip, c), sibling):
                cp.start()

    def finish():
        for cp in copies(0, sibling, me):
            cp.wait_recv()
        for j, chip in enumerate(chips):
            for cp in copies(4 + j, (*chip, 1 - c), me):
                cp.wait_recv()
        for cp in first():
            cp.wait_send()
        for j, chip in enumerate(chips):
            for cp in copies(4 + j, (*chip, c), sibling):
                cp.wait_send()
        for cp in mine():
            cp.wait()

    return start, forward, finish


def _ag2_scratch(na):
    return [pltpu.SemaphoreType.DMA((7 * na,)), pltpu.SemaphoreType.DMA((7 * na,)), pltpu.SemaphoreType.DMA((na,))]


def _a2a_plan(g_ref, recv_ref, send_sems, recv_sems, local_sem):
    x, y, c = _my_pos()
    me = 4 * x + 2 * y + c
    peers = []
    for k in range(1, NDEV):
        kx, ky, kc = (k >> 2) & 1, (k >> 1) & 1, k & 1
        peers.append((x ^ kx, y ^ ky, c ^ kc))

    def sends():
        return [pltpu.make_async_remote_copy(
            src_ref=g_ref.at[4 * px + 2 * py + pc], dst_ref=recv_ref.at[me], send_sem=send_sems.at[k], recv_sem=recv_sems.at[k],
            device_id=(px, py, pc), device_id_type=MESH) for k, (px, py, pc) in enumerate(peers)]

    def own():
        return pltpu.make_async_copy(g_ref.at[me], recv_ref.at[me], local_sem)

    def start():
        own().start()
        for cp in sends():
            cp.start()

    def finish():
        for k, (px, py, pc) in enumerate(peers):
            pltpu.make_async_remote_copy(
                src_ref=g_ref.at[me], dst_ref=recv_ref.at[4 * px + 2 * py + pc], send_sem=send_sems.at[k],
                recv_sem=recv_sems.at[k], device_id=(x, y, c), device_id_type=MESH).wait_recv()
        for cp in sends():
            cp.wait_send()
        own().wait()

    return start, finish


def _a2a_scratch():
    return [pltpu.SemaphoreType.DMA((NDEV - 1,)), pltpu.SemaphoreType.DMA((NDEV - 1,)), pltpu.SemaphoreType.DMA]


def _reduce_scatter2(g, small, name):
    _, r, n = g.shape
    ch = 16
    nch = r // ch
    ns = len(small)

    def body(g_ref, *rest):
        v_refs, out_ref, vout_refs = rest[:ns], rest[ns], rest[ns + 1:2 * ns + 1]
        a_ref, h_ref, b_ref, s1_send, s1_recv, s2_send, s2_recv = rest[2 * ns + 1:2 * ns + 8]
        gather_sems = rest[2 * ns + 8:]
        gathers = [_small_gather_plan(v_refs[i], vout_refs[i], *gather_sems[2 * i:2 * i + 2]) for i in range(ns)]
        for start, _ in gathers:
            start()
        x, y, c = _my_pos()
        sibling = (x, y, 1 - c)
        s1 = []
        for j in range(4):
            cp = pltpu.make_async_remote_copy(
                src_ref=g_ref.at[2 * j + (1 - c)], dst_ref=a_ref.at[j], send_sem=s1_send.at[j], recv_sem=s1_recv.at[j],
                device_id=sibling, device_id_type=MESH)
            cp.start()
            s1.append(cp)
        for cp in s1:
            cp.wait_recv()

        def add1(i, _):
            rr = pl.ds(pl.multiple_of(i * ch, ch), ch)
            for j in range(4):
                h_ref[j, rr, :] = (g_ref[2 * j + c, rr, :].astype(f32) + a_ref[j, rr, :].astype(f32)).astype(bf16)
            return 0
        lax.fori_loop(0, nch, add1, 0)
        mychip = 2 * x + y
        s2 = []
        for m in range(1, 4):
            mx, my_ = (m >> 1) & 1, m & 1
            px, py = x ^ mx, y ^ my_
            cp = pltpu.make_async_remote_copy(
                src_ref=h_ref.at[2 * px + py], dst_ref=b_ref.at[m - 1], send_sem=s2_send.at[m - 1], recv_sem=s2_recv.at[m - 1],
                device_id=(px, py, c), device_id_type=MESH)
            cp.start()
            s2.append(cp)
        for cp in s2:
            cp.wait_recv()

        def add2(i, _):
            rr = pl.ds(pl.multiple_of(i * ch, ch), ch)
            acc = h_ref[mychip, rr, :].astype(f32)
            for m in range(3):
                acc = acc + b_ref[m, rr, :].astype(f32)
            out_ref[rr, :] = acc
            return 0
        lax.fori_loop(0, nch, add2, 0)
        for cp in s1 + s2:
            cp.wait_send()
        for _, finish in gathers:
            finish()

    vmem = pl.BlockSpec(memory_space=pltpu.VMEM)
    return pl.pallas_call(
        body, name=name,
        out_shape=[jax.ShapeDtypeStruct((r, n), f32)] + [jax.ShapeDtypeStruct((NDEV,) + v.shape, v.dtype) for v in small],
        in_specs=[vmem] * (1 + ns),
        out_specs=[vmem] * (1 + ns),
        scratch_shapes=[pltpu.VMEM((4, r, n), bf16), pltpu.VMEM((4, r, n), bf16), pltpu.VMEM((3, r, n), bf16),
                        pltpu.SemaphoreType.DMA((4,)), pltpu.SemaphoreType.DMA((4,)),
                        pltpu.SemaphoreType.DMA((3,)), pltpu.SemaphoreType.DMA((3,))] + _small_gather_scratch() * ns,
        compiler_params=pltpu.CompilerParams(vmem_limit_bytes=VMEM_LIMIT),
    )(g, *small)


def _head(w_in_sh, c8, cctx8, w_mod_sh, b_sh, convpay, bias_base, bias_valid):
    nmod = w_mod_sh.shape[1]
    npay = convpay.shape[0]

    def body(w_ref, c_ref, cc_ref, wm_ref, b_ref, pay_ref, base_ref, valid_ref,
             wout_ref, call_ref, mall_ref, pall_ref, bias_ref, mod_s,
             ag_send, ag_recv, ag_local, c_send, c_recv, m_send, m_recv, p_send, p_recv):
        start, forward, finish = _ag2_plan([w_ref], [wout_ref], ag_send, ag_recv, ag_local)
        c_start, c_finish = _small_gather_plan(c_ref, call_ref, c_send, c_recv)
        m_start, m_finish = _small_gather_plan(mod_s, mall_ref, m_send, m_recv)
        p_start, p_finish = _small_gather_plan(pay_ref, pall_ref, p_send, p_recv)
        c_start()
        start()
        p_start()
        _write_bias_table(base_ref, valid_ref, bias_ref)
        c_finish()
        acc = jnp.zeros((16, nmod), f32)
        for j in range(D // 128):
            rows = jnp.concatenate([call_ref[:, j, :], cc_ref[j:j + 1, :], jnp.zeros((7, 128), f32)], axis=0)
            act = (rows * _sigmoid(rows)).astype(bf16)
            acc = acc + _dot(act, wm_ref[j * 128:(j + 1) * 128, :].astype(bf16))
        mod_s[...] = acc + b_ref[...]
        m_start()
        forward()
        finish()
        m_finish()
        p_finish()

    vmem = pl.BlockSpec(memory_space=pltpu.VMEM)
    return pl.pallas_call(
        body, name="head",
        out_shape=[jax.ShapeDtypeStruct((NDEV * w_in_sh.shape[0], D), bf16), jax.ShapeDtypeStruct((NDEV, 8, 128), f32),
                   jax.ShapeDtypeStruct((NDEV, 16, nmod), f32), jax.ShapeDtypeStruct((NDEV, npay, 128), f32),
                   jax.ShapeDtypeStruct((8, NH, GW, WR * GW), f32)],
        in_specs=[vmem] * 8, out_specs=[vmem] * 5,
        scratch_shapes=[pltpu.VMEM((16, nmod), f32)] + _ag2_scratch(1) + _small_gather_scratch() * 3,
        compiler_params=pltpu.CompilerParams(vmem_limit_bytes=VMEM_LIMIT),
    )(w_in_sh, c8, cctx8, w_mod_sh, b_sh, convpay, bias_base, bias_valid)


def _mod_bwd(cvec, dm_sh, w_sh):
    def body(c_ref, dm_ref, w_ref, gw_ref, gc_ref):
        cv = c_ref[...]
        act = (cv * _sigmoid(cv)).astype(bf16)
        gw_ref[...] = _dg(act, dm_ref[...].astype(bf16), TN)
        gc_ref[...] = _dg(dm_ref[8:16, :].astype(bf16), w_ref[...].astype(bf16), NT)
    return pl.pallas_call(
        body, name="mod_bwd",
        out_shape=(jax.ShapeDtypeStruct(w_sh.shape, f32), jax.ShapeDtypeStruct((8, D), f32)))(cvec, dm_sh, w_sh)


def _sum_rows8(a, name):
    n = a.shape[1]

    def body(a_ref, o_ref):
        acc = a_ref[0]
        for d in range(1, NDEV):
            acc = acc + a_ref[d]
        o_ref[...] = acc
    return pl.pallas_call(body, name=name, out_shape=jax.ShapeDtypeStruct((n, 128), f32))(a)


def _in_proj(x0, ctx0, g1, modv, w_inT, shards):
    tm = 256
    nt = TA // tm
    nx = T // tm
    na = len(shards)

    def body(x_ref, c_ref, g_ref, mod_ref, w_ref, *rest):
        x_refs, (h_ref, q_ref, k_ref, v_ref, a_ref, gg_ref) = rest[:na], rest[na:na + 6]
        out_refs, sems = rest[na + 6:2 * na + 6], rest[2 * na + 6:]
        i = pl.program_id(0)
        if na:
            start, forward, finish = _ag2_plan(x_refs, out_refs, *sems)
            pl.when(i == 0)(start)
            pl.when(i == nt - 2)(forward)
        is_ctx = i == nt - 1
        xv = jnp.where(is_ctx, c_ref[...], x_ref[...])
        rstd = lax.rsqrt(jnp.mean(xv * xv, axis=-1, keepdims=True) + EPS)
        sh = jnp.where(is_ctx, mod_ref[6:7, :], mod_ref[0:1, :])
        sc = jnp.where(is_ctx, mod_ref[7:8, :], mod_ref[1:2, :])
        h = ((xv * rstd * g_ref[...]) * (1.0 + sc) + sh).astype(bf16)
        h_ref[...] = h
        for j, o_ref in enumerate((q_ref, k_ref, v_ref, a_ref, gg_ref)):
            o_ref[...] = _dg(h, w_ref[j * DA:(j + 1) * DA, :], NT).astype(o_ref.dtype)
        if na:
            pl.when(is_ctx)(finish)

    row = lambda w: pl.BlockSpec((tm, w), lambda i: (i, 0))
    hbm = pl.BlockSpec(memory_space=pl.ANY)
    return pl.pallas_call(
        body, name="in_proj", grid=(nt,),
        in_specs=[pl.BlockSpec((tm, D), lambda i: (jnp.minimum(i, nx - 1), 0)), _full((TC, D)),
                  _full((1, D)), _full((8, D)), _full((5 * DA, D))] + [hbm] * na,
        out_specs=[row(D), row(DA), row(DA), row(DA), row(DA), row(DA)] + [hbm] * na,
        out_shape=[jax.ShapeDtypeStruct((TA, D), bf16)] + [jax.ShapeDtypeStruct((TA, DA), bf16)] * 3
                  + [jax.ShapeDtypeStruct((TA, DA), f32)] * 2
                  + [jax.ShapeDtypeStruct((NDEV * sh.shape[0], sh.shape[1]), sh.dtype) for sh in shards],
        scratch_shapes=_ag2_scratch(na) if na else [],
        compiler_params=_cp("arbitrary"),
    )(x0, ctx0, g1, modv, w_inT, *shards)


def _win_start(r):
    return jnp.clip(r - WR // 2, 0, GW - WR)


def _pattern(r):
    return _win_start(r) - r + (WR - 1)


def _bias_table_inputs(rpb):
    qc = np.arange(GW)[:, None]
    kc = np.arange(GW)[None, :]
    cs = np.clip(qc - NCOL // 2, 0, GW - NCOL)
    valid = np.tile(((kc >= cs) & (kc < cs + NCOL)).astype(np.int32), (1, WR))
    pad = jnp.pad(rpb, ((0, 0), (0, 0), (0, GW - (2 * NCOL - 1))))
    base = jnp.stack([pad[:, p:p + WR, :].reshape(NH, WR * GW) for p in range(8)])
    return base, jnp.asarray(valid)


def _write_bias_table(base_ref, valid_ref, o_ref):
    ok = valid_ref[...] != 0
    for p in range(8):
        for h in range(NH):
            row = jnp.broadcast_to(base_ref[p, h:h + 1, :], (GW, WR * GW))
            skew = pltpu.roll(row, WR * GW - (NCOL - 1), 1, stride=1, stride_axis=0)
            o_ref[p, h] = jnp.where(ok, skew, NEG)


def _rpb_tables():
    lane_map = np.zeros((WR * GW, WR, 2 * NCOL - 1), np.float32)
    for i in range(WR):
        for t in range(GW):
            if t >= GW - NCOL:
                lane_map[i * GW + t, i, t - (GW - NCOL)] = 1.0
            elif t < NCOL - 1:
                lane_map[i * GW + t, (i - 1) % WR, t + NCOL] = 1.0
    p = np.arange(8)[:, None]
    i = np.arange(WR)[None, :]
    r_hot = ((p + i)[:, :, None] == np.arange(2 * WR - 1)[None, None, :]).astype(np.float32)
    return lane_map, r_hot


AG_FORWARD_ROW = 58


def _stack_pair(x2, lo):
    z = jnp.zeros_like(x2)
    return jnp.concatenate([jnp.where(lo, x2, z), jnp.where(lo, z, x2)], axis=0)


def _attn_fwd(q, k, v, bias_tab, shards):
    na = len(shards)

    def body(q_ref, k_ref, v_ref, b_ref, *rest):
        x_refs, (y_ref, lse_ref), out_refs, sems = rest[:na], rest[na:na + 2], rest[na + 2:2 * na + 2], rest[2 * na + 2:]
        r = pl.program_id(0)
        if na:
            start, forward, finish = _ag2_plan(x_refs, out_refs, *sems)
            pl.when(r == 0)(start)
            pl.when(r == AG_FORWARD_ROW)(forward)
        ks = pl.multiple_of(_win_start(r) * GW, GW)
        qq = q_ref[...]
        lo = lax.broadcasted_iota(jnp.int32, (GW, 2 * HD), 1) < HD
        kv, scores = [], []
        for pr in range(NH // 2):
            ps = slice(pr * 2 * HD, (pr + 1) * 2 * HD)
            qst = _stack_pair(qq[:, ps], lo)
            kw, kc = k_ref[pl.ds(ks, WR * GW), ps], k_ref[T:TA, ps]
            kv.append((v_ref[pl.ds(ks, WR * GW), ps], v_ref[T:TA, ps]))
            bias2 = b_ref[0, 2 * pr:2 * pr + 2].reshape(2 * GW, WR * GW)
            scores.append((_dg(qst, kw, NT) * SCALE + bias2, _dg(qst, kc, NT) * SCALE))
        probs = []
        for pr, (sl, sc) in enumerate(scores):
            m = jnp.maximum(jnp.max(sl, axis=-1, keepdims=True), jnp.max(sc, axis=-1, keepdims=True))
            pl_ = jnp.exp(sl - m)
            pc = jnp.exp(sc - m)
            l = jnp.sum(pl_, axis=-1, keepdims=True) + jnp.sum(pc, axis=-1, keepdims=True)
            lse = m + jnp.log(l)
            lse_ref[:, 2 * pr:2 * pr + 1] = lse[0:GW]
            lse_ref[:, 2 * pr + 1:2 * pr + 2] = lse[GW:]
            probs.append((pl_.astype(bf16), pc.astype(bf16), 1.0 / l))
        for pr in range(NH // 2):
            ps = slice(pr * 2 * HD, (pr + 1) * 2 * HD)
            vw, vc = kv[pr]
            pb, cb, rl = probs[pr]
            o = (_dot(pb, vw) + _dot(cb, vc)) * rl
            y_ref[:, ps] = jnp.where(lo, o[0:GW], o[GW:]).astype(bf16)
        if na:
            pl.when(r == GW - 1)(finish)

    hbm = pl.BlockSpec(memory_space=pl.ANY)
    return pl.pallas_call(
        body, name="attn_fwd", grid=(GW,),
        in_specs=[pl.BlockSpec((GW, DA), lambda r: (r, 0)), _full((TA, DA)), _full((TA, DA)),
                  pl.BlockSpec((1, NH, GW, WR * GW), lambda r: (_pattern(r), 0, 0, 0))] + [hbm] * na,
        out_specs=[pl.BlockSpec((GW, DA), lambda r: (r, 0)), pl.BlockSpec((GW, NH), lambda r: (r, 0))] + [hbm] * na,
        out_shape=[jax.ShapeDtypeStruct((T, DA), bf16), jax.ShapeDtypeStruct((T, NH), f32)]
                  + [jax.ShapeDtypeStruct((NDEV * s.shape[0], s.shape[1]), s.dtype) for s in shards],
        scratch_shapes=_ag2_scratch(na) if na else [],
        compiler_params=_cp("arbitrary"),
    )(q, k, v, bias_tab, *shards)


CONV_TT = 512
HALO = 16


def _halo_specs(tt, w, nrows_blocks):
    per = tt // HALO
    prev = pl.BlockSpec((HALO, w), lambda i: (jnp.maximum(i * per - 1, 0), 0))
    cur = pl.BlockSpec((tt, w), lambda i: (i, 0))
    nxt = pl.BlockSpec((HALO, w), lambda i: (jnp.minimum((i + 1) * per, nrows_blocks - 1), 0))
    return [prev, cur, nxt]


def _shifted_copies(rot, wn):
    for b in range(1, 8):
        rot[b, 0:wn - 8, :] = rot[0, pl.ds(b, wn - 8), :]


def _conf_fwd(a, g, conv_w, conv_b, ln_g, ln_b, shards):
    tt = CONV_TT
    nt = T // tt
    sub = 32
    wn = tt + 2 * HALO
    na = len(shards)

    def body(ap, ac, an, gp, gc, gn, w_ref, b_ref, lg_ref, lb_ref, *rest):
        x_refs, (y_ref, cv_ref), out_refs, (rot, *sems) = rest[:na], rest[na:na + 2], rest[na + 2:2 * na + 2], rest[2 * na + 2:]
        i = pl.program_id(0)
        if na:
            start, forward, finish = _ag2_plan(x_refs, out_refs, *sems)
            pl.when(i == 0)(start)
            pl.when(i == nt // 2)(forward)
        rot[0, 0:HALO, :] = jnp.where(i > 0, ap[...] * _sigmoid(gp[...]), 0.0)
        rot[0, HALO:HALO + tt, :] = ac[...] * _sigmoid(gc[...])
        rot[0, HALO + tt:, :] = jnp.where(i < nt - 1, an[...] * _sigmoid(gn[...]), 0.0)
        _shifted_copies(rot, wn)
        w = w_ref[...]
        for s in range(tt // sub):
            acc = jnp.zeros((sub, DA), f32)
            for j in range(CW):
                a8, b8 = divmod(1 + j, 8)
                acc = acc + rot[b8, pl.ds(s * sub + 8 * a8, sub), :] * w[j:j + 1, :]
            cv = acc + b_ref[...]
            cv_ref[pl.ds(s * sub, sub), :] = cv
            mu = jnp.mean(cv, axis=-1, keepdims=True)
            xc = cv - mu
            rstd = lax.rsqrt(jnp.mean(xc * xc, axis=-1, keepdims=True) + EPS)
            z = xc * rstd * lg_ref[...] + lb_ref[...]
            y_ref[pl.ds(s * sub, sub), :] = (z * _sigmoid(z)).astype(bf16)
        if na:
            pl.when(i == nt - 1)(finish)

    hs = _halo_specs(tt, DA, T // HALO)
    hbm = pl.BlockSpec(memory_space=pl.ANY)
    return pl.pallas_call(
        body, name="conf_fwd", grid=(nt,),
        in_specs=hs + hs + [_full((CW, DA)), _full((1, DA)), _full((1, DA)), _full((1, DA))] + [hbm] * na,
        out_specs=[pl.BlockSpec((tt, DA), lambda i: (i, 0)), pl.BlockSpec((tt, DA), lambda i: (i, 0))] + [hbm] * na,
        out_shape=[jax.ShapeDtypeStruct((T, DA), bf16), jax.ShapeDtypeStruct((T, DA), f32)]
                  + [jax.ShapeDtypeStruct((NDEV * s.shape[0], s.shape[1]), s.dtype) for s in shards],
        scratch_shapes=[pltpu.VMEM((8, wn, DA), f32)] + (_ag2_scratch(na) if na else []),
        compiler_params=_cp("arbitrary"),
    )(a, a, a, g, g, g, conv_w, conv_b, ln_g, ln_b, *shards)


def _out_proj(xa, y_na, y_cv, w_out, modv, g2):
    tm = 512

    def body(x_ref, ya_ref, yc_ref, w_ref, mod_ref, g_ref, x1_ref, pj_ref, h2_ref):
        proj = _dot(ya_ref[...], w_ref[0:DA, :]) + _dot(yc_ref[...], w_ref[DA:D, :])
        x1 = x_ref[...] + mod_ref[2:3, :] * proj
        x1_ref[...] = x1
        pj_ref[...] = proj.astype(bf16)
        rstd = lax.rsqrt(jnp.mean(x1 * x1, axis=-1, keepdims=True) + EPS)
        h2_ref[...] = ((x1 * rstd * g_ref[...]) * (1.0 + mod_ref[4:5, :]) + mod_ref[3:4, :]).astype(bf16)

    row = lambda w: pl.BlockSpec((tm, w), lambda i: (i, 0))
    return pl.pallas_call(
        body, name="out_proj", grid=(T // tm,),
        in_specs=[row(D), row(DA), row(DA), _full((D, D)), _full((8, D)), _full((1, D))],
        out_specs=[row(D), row(D), row(D)],
        out_shape=[jax.ShapeDtypeStruct((T, D), f32), jax.ShapeDtypeStruct((T, D), bf16), jax.ShapeDtypeStruct((T, D), bf16)],
        compiler_params=_cp("parallel"),
    )(xa, y_na, y_cv, w_out, modv, g2)


FFN_TT = 2048
FFN_CT = 256
FFN_NC = F // FFN_CT
FFN_SUB = 32


def _row_neighbours(ref, r, n):
    blk = ref[pl.ds(r - 8, n + 16), :]
    return blk[8:8 + n, :], pltpu.roll(blk, 1, 0)[8:8 + n, :], pltpu.roll(blk, n + 15, 0)[8:8 + n, :]


def _ffn_specs(tt, ct, by_token_first):
    tc = (lambda f: (lambda t, c: f(t, c))) if by_token_first else (lambda f: (lambda c, t: f(t, c)))
    per = tt // HALO
    halo = [pl.BlockSpec((HALO, D), tc(lambda t, c: (jnp.maximum(t * per - 1, 0), 0))),
            pl.BlockSpec((tt, D), tc(lambda t, c: (t, 0))),
            pl.BlockSpec((HALO, D), tc(lambda t, c: (jnp.minimum((t + 1) * per, T // HALO - 1), 0)))]
    weights = [pl.BlockSpec((ct, D), tc(lambda t, c: (c, 0))), pl.BlockSpec((ct, D), tc(lambda t, c: (c + FFN_NC, 0))),
               pl.BlockSpec((3, ct), tc(lambda t, c: (0, c))), pl.BlockSpec((3, ct), tc(lambda t, c: (0, c + FFN_NC))),
               pl.BlockSpec((1, ct), tc(lambda t, c: (0, c))), pl.BlockSpec((1, ct), tc(lambda t, c: (0, c + FFN_NC))),
               pl.BlockSpec((ct, D), tc(lambda t, c: (c, 0)))]
    return halo, weights


def _ffn_fwd(h2, w_upT, fcw, fcb, w_down):
    tt, ct = FFN_TT, FFN_CT
    nt = T // tt
    wn = tt + 2 * HALO
    half = tt // 2

    def body(hp, hc, hn, wg_ref, wv_ref, cwg_ref, cwv_ref, cbg_ref, cbv_ref, wd_ref, o_ref, u_ref, u2_ref, hwin, uwin, act):
        t = pl.program_id(0)
        c = pl.program_id(1)

        @pl.when(c == 0)
        def _():
            hwin[0:HALO, :] = jnp.where(t > 0, hp[...], jnp.zeros_like(hp[...]))
            hwin[HALO:HALO + tt, :] = hc[...]
            hwin[HALO + tt:, :] = jnp.where(t < nt - 1, hn[...], jnp.zeros_like(hn[...]))
            o_ref[...] = jnp.zeros_like(o_ref)

        for r0, r1 in ((0, half + 2 * HALO), (half + 2 * HALO, wn)):
            hw = hwin[r0:r1, :]
            uwin[r0:r1, :ct] = _dg(hw, wg_ref[...], NT)
            uwin[r0:r1, ct:] = _dg(hw, wv_ref[...], NT)
        cw = jnp.concatenate([cwg_ref[...], cwv_ref[...]], axis=1)
        cb = jnp.concatenate([cbg_ref[...], cbv_ref[...]], axis=1)
        for p in range(2):
            for r in range(p * half, (p + 1) * half, FFN_SUB):
                uc, prev, nxt = _row_neighbours(uwin, HALO + r, FFN_SUB)
                u2 = prev * cw[0:1, :] + uc * cw[1:2, :] + nxt * cw[2:3, :] + cb
                u_ref[r:r + FFN_SUB, :] = uc.astype(bf16)
                u2_ref[r:r + FFN_SUB, :] = u2
                gate = u2[:, :ct]
                act[r:r + FFN_SUB, :] = (gate * _sigmoid(gate) * u2[:, ct:]).astype(bf16)
            rows = slice(p * half, (p + 1) * half)
            o_ref[rows, :] += _dot(act[rows, :], wd_ref[...])

    halo, weights = _ffn_specs(tt, ct, True)
    pair = pl.BlockSpec((tt, 2 * ct), lambda t, c: (t, c))
    return pl.pallas_call(
        body, name="ffn_fwd", grid=(nt, FFN_NC),
        in_specs=halo + weights,
        out_specs=[pl.BlockSpec((tt, D), lambda t, c: (t, 0)), pair, pair],
        out_shape=[jax.ShapeDtypeStruct((T, D), f32), jax.ShapeDtypeStruct((T, F2), bf16), jax.ShapeDtypeStruct((T, F2), f32)],
        scratch_shapes=[pltpu.VMEM((wn, D), bf16), pltpu.VMEM((wn, 2 * ct), f32), pltpu.VMEM((tt, ct), bf16)],
        compiler_params=_cp("parallel", "arbitrary"),
    )(h2, h2, h2, w_upT, w_upT, fcw, fcw, fcb, fcb, w_down)


def _loss_bwd(ffn, x1, tgt, modv, gf):
    tm = 1024
    nt = T // tm

    def body(f_ref, x1_ref, t_ref, mod_ref, g_ref, dx2_ref, df_ref, s_ref):
        i = pl.program_id(0)

        @pl.when(i == 0)
        def _():
            s_ref[...] = jnp.zeros_like(s_ref)

        ff = f_ref[...]
        gt2 = mod_ref[5:6, :]
        x2 = x1_ref[...] + gt2 * ff
        rstd = lax.rsqrt(jnp.mean(x2 * x2, axis=-1, keepdims=True) + EPS)
        xh = x2 * rstd
        gfv = g_ref[...]
        e = xh * gfv - t_ref[...]
        dy = e * (1.0 / D)
        dxh = dy * gfv
        dx2 = rstd * (dxh - xh * jnp.mean(dxh * xh, axis=-1, keepdims=True))
        dx2_ref[...] = dx2
        df_ref[...] = (dx2 * gt2).astype(bf16)
        s_ref[0:1, :] += jnp.sum(dy * xh, axis=0, keepdims=True)
        s_ref[1:2, :] += jnp.sum(dx2 * ff, axis=0, keepdims=True)
        s_ref[2:3, :] += jnp.sum(e * e, axis=0, keepdims=True)

        @pl.when(i == nt - 1)
        def _():
            tot = jnp.sum(s_ref[2:3, :], axis=-1, keepdims=True) * (0.5 / D)
            s_ref[3:4, :] = jnp.broadcast_to(tot, (1, D))

    row = lambda: pl.BlockSpec((tm, D), lambda i: (i, 0))
    return pl.pallas_call(
        body, name="loss_bwd", grid=(nt,),
        in_specs=[row(), row(), row(), _full((8, D)), _full((1, D))],
        out_specs=[row(), row(), _full((8, D))],
        out_shape=[jax.ShapeDtypeStruct((T, D), f32), jax.ShapeDtypeStruct((T, D), bf16), jax.ShapeDtypeStruct((8, D), f32)],
        compiler_params=_cp("arbitrary"),
    )(ffn, x1, tgt, modv, gf)


def _ffn_bwd(h2, dffn, u_t, u2_t, fcw, w_down):
    tt, ct = FFN_TT, FFN_CT
    nt = T // tt
    wn = tt + 2 * HALO
    half = tt // 2

    def body(dp, dc, dn, hc, uc_ref, u2p, u2c, u2n, cwg_ref, cwv_ref, wd_ref,
             dug_ref, duv_ref, dwu_ref, dwd_ref, dcwg_ref, dcwv_ref, dcbg_ref, dcbv_ref,
             dwin, d2win, dawin, accu, accd, act, du):
        t = pl.program_id(1)
        first, last = t == 0, t == nt - 1
        zero = jnp.zeros((HALO, D), bf16)
        dwin[0:HALO, :] = jnp.where(first, zero, dp[...])
        dwin[HALO:HALO + tt, :] = dc[...]
        dwin[HALO + tt:, :] = jnp.where(last, zero, dn[...])

        @pl.when(first)
        def _():
            for r in (accu, accd, dcwg_ref, dcwv_ref, dcbg_ref, dcbv_ref):
                r[...] = jnp.zeros_like(r)

        cw = jnp.concatenate([cwg_ref[...], cwv_ref[...]], axis=1)
        split = half + 2 * HALO
        for r0, r1 in ((0, split), (split, wn)):
            dawin[r0:r1, :] = _dg(dwin[r0:r1, :], wd_ref[...], NT)

        def grads(u2v, dact):
            gate, val = u2v[:, :ct], u2v[:, ct:]
            sg = _sigmoid(gate)
            silu = gate * sg
            return dact * val * (sg * (1.0 + gate * (1.0 - sg))), dact * silu, silu * val

        for blk, r0 in ((u2p, 0), (u2n, HALO + tt)):
            dgate, dval, _ = grads(blk[...], dawin[r0:r0 + HALO, :])
            d2win[r0:r0 + HALO, :ct] = dgate
            d2win[r0:r0 + HALO, ct:] = dval
        for p in range(2):
            rows = slice(p * half, (p + 1) * half)
            for r in range(p * half, (p + 1) * half, FFN_SUB):
                dgate, dval, av = grads(u2c[r:r + FFN_SUB, :], dawin[HALO + r:HALO + r + FFN_SUB, :])
                d2win[HALO + r:HALO + r + FFN_SUB, :ct] = dgate
                d2win[HALO + r:HALO + r + FFN_SUB, ct:] = dval
                act[r:r + FFN_SUB, :] = av.astype(bf16)
            accd[...] += _dg(act[rows, :], dc[rows, :], TN)

        def fold8(x):
            out = x[0:8]
            for k in range(8, FFN_SUB, 8):
                out = out + x[k:k + 8]
            return out

        for p in range(2):
            rows = slice(p * half, (p + 1) * half)
            sums = [jnp.zeros((8, 2 * ct), f32) for _ in range(4)]
            for r in range(p * half, (p + 1) * half, FFN_SUB):
                d2c, d2m, d2p = _row_neighbours(d2win, HALO + r, FFN_SUB)
                ucur = uc_ref[r:r + FFN_SUB, :].astype(f32)
                sums[0] = sums[0] + fold8(d2c)
                for kk, dd in enumerate((d2p, d2c, d2m)):
                    sums[1 + kk] = sums[1 + kk] + fold8(ucur * dd)
                du[r:r + FFN_SUB, :] = (d2p * cw[0:1, :] + d2c * cw[1:2, :] + d2m * cw[2:3, :]).astype(bf16)
            dcb = jnp.sum(sums[0], axis=0, keepdims=True)
            dcbg_ref[...] += dcb[:, :ct]
            dcbv_ref[...] += dcb[:, ct:]
            for kk in range(3):
                dck = jnp.sum(sums[1 + kk], axis=0, keepdims=True)
                dcwg_ref[kk:kk + 1, :] += dck[:, :ct]
                dcwv_ref[kk:kk + 1, :] += dck[:, ct:]
            dug_ref[rows, :] = du[rows, :ct]
            duv_ref[rows, :] = du[rows, ct:]
            accu[...] += _dg(du[rows, :], hc[rows, :], TN)

        @pl.when(last)
        def _():
            dwu_ref[0] = accu[0:ct, :].astype(bf16)
            dwu_ref[1] = accu[ct:, :].astype(bf16)
            dwd_ref[...] = accd[...].astype(bf16)

    per = tt // HALO
    prev = lambda w: (lambda c, t: (jnp.maximum(t * per - 1, 0), c if w else 0))
    nxt = lambda w: (lambda c, t: (jnp.minimum((t + 1) * per, T // HALO - 1), c if w else 0))
    tile = lambda: pl.BlockSpec((ct, D), lambda c, t: (c, 0))
    lane = lambda r, off: pl.BlockSpec((r, ct), lambda c, t: (0, c + off))
    return pl.pallas_call(
        body, name="ffn_bwd", grid=(FFN_NC, nt),
        in_specs=[pl.BlockSpec((HALO, D), prev(False)), pl.BlockSpec((tt, D), lambda c, t: (t, 0)), pl.BlockSpec((HALO, D), nxt(False)),
                  pl.BlockSpec((tt, D), lambda c, t: (t, 0)), pl.BlockSpec((tt, 2 * ct), lambda c, t: (t, c)),
                  pl.BlockSpec((HALO, 2 * ct), prev(True)), pl.BlockSpec((tt, 2 * ct), lambda c, t: (t, c)),
                  pl.BlockSpec((HALO, 2 * ct), nxt(True)), lane(3, 0), lane(3, FFN_NC), tile()],
        out_specs=[pl.BlockSpec((tt, ct), lambda c, t: (t, c)), pl.BlockSpec((tt, ct), lambda c, t: (t, c)),
                   pl.BlockSpec((2, ct, D), lambda c, t: (0, c, 0)), tile(), lane(3, 0), lane(3, 0), lane(1, 0), lane(1, 0)],
        out_shape=[jax.ShapeDtypeStruct((T, F), bf16), jax.ShapeDtypeStruct((T, F), bf16),
                   jax.ShapeDtypeStruct((2, F, D), bf16), jax.ShapeDtypeStruct((F, D), bf16),
                   jax.ShapeDtypeStruct((3, F), f32), jax.ShapeDtypeStruct((3, F), f32),
                   jax.ShapeDtypeStruct((1, F), f32), jax.ShapeDtypeStruct((1, F), f32)],
        scratch_shapes=[pltpu.VMEM((wn, D), bf16), pltpu.VMEM((wn, 2 * ct), f32), pltpu.VMEM((wn, ct), f32),
                        pltpu.VMEM((2 * ct, D), f32), pltpu.VMEM((ct, D), f32),
                        pltpu.VMEM((tt, ct), bf16), pltpu.VMEM((tt, 2 * ct), bf16)],
        compiler_params=_cp("parallel", "arbitrary"),
    )(dffn, dffn, dffn, h2, u_t, u2_t, u2_t, u2_t, fcw, fcw, w_down)


def _norm_bwd(dh, xv, gain, sh_sc, rstd):
    xh = xv * rstd
    n = xh * gain
    dn = dh * (1.0 + sh_sc)
    dxh = dn * gain
    dx = rstd * (dxh - xh * jnp.mean(dxh * xh, axis=-1, keepdims=True))
    return (dx, jnp.sum(dh, axis=0, keepdims=True), jnp.sum(dh * n, axis=0, keepdims=True),
            jnp.sum(dn * xh, axis=0, keepdims=True))


def _norm2_bwd(dug, duv, w_upT, x1, dx2, proj, w_out, y_na, y_cv, modv, g2):
    tm = 512
    nt = T // tm

    def body(dug_ref, duv_ref, w_ref, x1_ref, dx2_ref, pj_ref, wo_ref, ya_ref, yc_ref, mod_ref, g_ref,
             dx1_ref, dya_ref, dyc_ref, dwo_ref, s_ref, acc):
        i = pl.program_id(0)

        @pl.when(i == 0)
        def _():
            s_ref[...] = jnp.zeros_like(s_ref)
            acc[...] = jnp.zeros_like(acc)

        dh2 = _dot(dug_ref[...], w_ref[0:F, :]) + _dot(duv_ref[...], w_ref[F:F2, :])
        x1 = x1_ref[...]
        rstd = lax.rsqrt(jnp.mean(x1 * x1, axis=-1, keepdims=True) + EPS)
        dxn, dsh, dsc, dgn = _norm_bwd(dh2, x1, g_ref[...], mod_ref[4:5, :], rstd)
        dx1 = dx2_ref[...] + dxn
        dx1_ref[...] = dx1
        dpj = (dx1 * mod_ref[2:3, :]).astype(bf16)
        dyc = _dg(dpj, wo_ref[...], NT)
        dya_ref[...] = dyc[:, :DA].astype(bf16)
        dyc_ref[...] = dyc[:, DA:]
        acc[0:DA, :] += _dg(ya_ref[...], dpj, TN)
        acc[DA:D, :] += _dg(yc_ref[...], dpj, TN)

        @pl.when(i == nt - 1)
        def _():
            dwo_ref[...] = acc[...].astype(bf16)

        s_ref[0:1, :] += dsh
        s_ref[1:2, :] += dsc
        s_ref[2:3, :] += dgn
        s_ref[3:4, :] += jnp.sum(dx1 * pj_ref[...].astype(f32), axis=0, keepdims=True)

    row = lambda w: pl.BlockSpec((tm, w), lambda i: (i, 0))
    return pl.pallas_call(
        body, name="norm2_bwd", grid=(nt,),
        in_specs=[row(F), row(F), _resident((F2, D)), row(D), row(D), row(D), _resident((D, D)), row(DA), row(DA),
                  _full((8, D)), _full((1, D))],
        out_specs=[row(D), row(DA), row(DA), _full((D, D)), _full((8, D))],
        out_shape=[jax.ShapeDtypeStruct((T, D), f32), jax.ShapeDtypeStruct((T, DA), bf16), jax.ShapeDtypeStruct((T, DA), f32),
                   jax.ShapeDtypeStruct((D, D), bf16), jax.ShapeDtypeStruct((8, D), f32)],
        scratch_shapes=[pltpu.VMEM((D, D), f32)],
        compiler_params=_cp("arbitrary"),
    )(dug, duv, w_upT, x1, dx2, proj, w_out, y_na, y_cv, modv, g2)


def _conf_bwd(a, g, cv, dy, conv_w, ln_g, ln_b, blocks):
    tt = CONV_TT
    nt = T // tt
    sub = 32
    wn = tt + 2 * HALO
    nb = len(blocks)

    def body(ap, ac, an, gp, gc, gn, cp_, cc, cn, dp, dc, dn, w_ref, lg_ref, lb_ref, *rest):
        g_refs, (da_ref, dg_ref, dcw_ref, s_ref) = rest[:nb], rest[nb:nb + 4]
        recv_refs, (urot, drot, wacc), a2a_sems = rest[nb + 4:2 * nb + 4], rest[2 * nb + 4:2 * nb + 7], rest[2 * nb + 7:]
        i = pl.program_id(0)
        first, last = i == 0, i == nt - 1
        plans = [_a2a_plan(g_refs[k], recv_refs[k], *a2a_sems[3 * k:3 * k + 3]) for k in range(nb)]
        for start, _ in plans:
            pl.when(first)(start)

        @pl.when(first)
        def _():
            s_ref[...] = jnp.zeros_like(s_ref)
            wacc[...] = jnp.zeros_like(wacc)

        lg, lb = lg_ref[...], lb_ref[...]

        def ln_bwd(cvv, dyv):
            mu = jnp.mean(cvv, axis=-1, keepdims=True)
            xc = cvv - mu
            rstd = lax.rsqrt(jnp.mean(xc * xc, axis=-1, keepdims=True) + EPS)
            yn = xc * rstd
            z = yn * lg + lb
            sz = _sigmoid(z)
            dz = dyv * (sz * (1.0 + z * (1.0 - sz)))
            dyn = dz * lg
            dcv = rstd * (dyn - jnp.mean(dyn, axis=-1, keepdims=True) - yn * jnp.mean(dyn * yn, axis=-1, keepdims=True))
            return dcv, dz, yn

        urot[0, 0:HALO, :] = jnp.where(first, 0.0, ap[...] * _sigmoid(gp[...]))
        urot[0, HALO + tt:, :] = jnp.where(last, 0.0, an[...] * _sigmoid(gn[...]))
        drot[0, 0:HALO, :] = jnp.where(first, 0.0, ln_bwd(cp_[...], dp[...])[0])
        drot[0, HALO + tt:, :] = jnp.where(last, 0.0, ln_bwd(cn[...], dn[...])[0])
        for s in range(tt // sub):
            rr = pl.ds(s * sub, sub)
            urot[0, pl.ds(HALO + s * sub, sub), :] = ac[rr, :] * _sigmoid(gc[rr, :])
            dcv, dz, yn = ln_bwd(cc[rr, :], dc[rr, :])
            drot[0, pl.ds(HALO + s * sub, sub), :] = dcv
            s_ref[0:1, :] += jnp.sum(dcv, axis=0, keepdims=True)
            s_ref[1:2, :] += jnp.sum(dz * yn, axis=0, keepdims=True)
            s_ref[2:3, :] += jnp.sum(dz, axis=0, keepdims=True)
        _shifted_copies(urot, wn)
        _shifted_copies(drot, wn)
        w = w_ref[...]
        for s in range(tt // sub):
            rr = pl.ds(s * sub, sub)
            dcv = drot[0, pl.ds(HALO + s * sub, sub), :]
            acc = jnp.zeros((sub, DA), f32)
            for j in range(CW):
                ad, bd = divmod(2 * HALO - 1 - j, 8)
                au, bu = divmod(1 + j, 8)
                acc = acc + drot[bd, pl.ds(s * sub + 8 * ad, sub), :] * w[j:j + 1, :]
                part = urot[bu, pl.ds(s * sub + 8 * au, sub), :] * dcv
                wacc[j] += part[0:8] + part[8:16] + part[16:24] + part[24:32]
            av, gv = ac[rr, :], gc[rr, :]
            sg = _sigmoid(gv)
            da_ref[rr, :] = (acc * sg).astype(bf16)
            dg_ref[rr, :] = (acc * av * sg * (1.0 - sg)).astype(bf16)

        @pl.when(last)
        def _():
            for j in range(CW):
                dcw_ref[j:j + 1, :] = jnp.sum(wacc[j], axis=0, keepdims=True)
            dcw_ref[CW:CW + 1, :] = jnp.zeros((1, DA), f32)

        for _, finish in plans:
            pl.when(last)(finish)

    hs = _halo_specs(tt, DA, T // HALO)
    hbm = pl.BlockSpec(memory_space=pl.ANY)
    return pl.pallas_call(
        body, name="conf_bwd", grid=(nt,),
        in_specs=hs * 4 + [_full((CW, DA)), _full((1, DA)), _full((1, DA))] + [hbm] * nb,
        out_specs=[pl.BlockSpec((tt, DA), lambda i: (i, 0)), pl.BlockSpec((tt, DA), lambda i: (i, 0)),
                   _full((CW + 1, DA)), _full((8, DA))] + [hbm] * nb,
        out_shape=[jax.ShapeDtypeStruct((T, DA), bf16), jax.ShapeDtypeStruct((T, DA), bf16),
                   jax.ShapeDtypeStruct((CW + 1, DA), f32), jax.ShapeDtypeStruct((8, DA), f32)]
                  + [jax.ShapeDtypeStruct(b.shape, b.dtype) for b in blocks],
        scratch_shapes=[pltpu.VMEM((8, wn, DA), f32), pltpu.VMEM((8, wn, DA), f32), pltpu.VMEM((CW, 8, DA), f32)]
                       + _a2a_scratch() * nb,
        compiler_params=_cp("arbitrary"),
    )(a, a, a, g, g, g, cv, cv, cv, dy, dy, dy, conv_w, ln_g, ln_b, *blocks)


def _attn_bwd(q, k, v, y, dy, lse, bias_tab, blocks):
    zr = 256
    nb = len(blocks)

    def body(q_ref, k_ref, v_ref, y_ref, dy_ref, lse_ref, b_ref, *rest):
        g_refs, (dq_ref, dk_hbm, dv_hbm, db_ref) = rest[:nb], rest[nb:nb + 4]
        recv_refs, (dk_s, dv_s, sem), a2a_sems = rest[nb + 4:2 * nb + 4], rest[2 * nb + 4:2 * nb + 7], rest[2 * nb + 7:]
        r = pl.program_id(0)
        plans = [_a2a_plan(g_refs[i], recv_refs[i], *a2a_sems[3 * i:3 * i + 3]) for i in range(nb)]
        for start, _ in plans:
            pl.when(r == 0)(start)

        @pl.when(r == 0)
        def _():
            def z(i, _):
                rr = pl.ds(pl.multiple_of(i * zr, zr), zr)
                dk_s[rr, :] = jnp.zeros((zr, DA), f32)
                dv_s[rr, :] = jnp.zeros((zr, DA), f32)
                return 0
            lax.fori_loop(0, TA // zr, z, 0)

        @pl.when((r <= WR // 2) | (r > GW - WR // 2))
        def _():
            db_ref[...] = jnp.zeros_like(db_ref)

        ks = pl.multiple_of(_win_start(r) * GW, GW)
        win = pl.ds(ks, WR * GW)
        qq, yy, dyy, lse_v = q_ref[...], y_ref[...], dy_ref[...], lse_ref[...]
        lo = lax.broadcasted_iota(jnp.int32, (GW, 2 * HD), 1) < HD
        ops, pairs = [], []
        for pr in range(NH // 2):
            ps = slice(pr * 2 * HD, (pr + 1) * 2 * HD)
            q2, do2 = qq[:, ps], dyy[:, ps]
            prod = do2.astype(f32) * yy[:, ps].astype(f32)
            delta = jnp.concatenate([jnp.sum(jnp.where(lo, prod, 0.0), axis=-1, keepdims=True),
                                     jnp.sum(jnp.where(lo, 0.0, prod), axis=-1, keepdims=True)], axis=0)
            qst, dost = _stack_pair(q2, lo), _stack_pair(do2, lo)
            kw, vw = k_ref[win, ps], v_ref[win, ps]
            kc, vc = k_ref[T:TA, ps], v_ref[T:TA, ps]
            ops.append((kw, kc))
            pairs.append((qst, dost, delta, _dg(qst, kw, NT), _dg(qst, kc, NT), _dg(dost, vw, NT), _dg(dost, vc, NT)))
        grads = []
        for pr, (qst, dost, delta, sl, sc, dpl, dpc) in enumerate(pairs):
            lh = jnp.concatenate([lse_v[:, 2 * pr:2 * pr + 1], lse_v[:, 2 * pr + 1:2 * pr + 2]], axis=0)
            bias2 = b_ref[0, 2 * pr:2 * pr + 2].reshape(2 * GW, WR * GW)
            pl_ = jnp.exp(sl * SCALE + bias2 - lh)
            pc = jnp.exp(sc * SCALE - lh)
            dsl = pl_ * (dpl - delta)
            dsc = pc * (dpc - delta)
            db_ref[0, 2 * pr:2 * pr + 2] += dsl.reshape(2, GW, WR * GW)
            grads.append((qst, dost, pl_.astype(bf16), pc.astype(bf16), dsl.astype(bf16), dsc.astype(bf16)))
        for pr in range(NH // 2):
            ps = slice(pr * 2 * HD, (pr + 1) * 2 * HD)
            kw, kc = ops[pr]
            qst, dost, plb, pcb, dslb, dscb = grads[pr]
            dqst = _dot(dslb, kw) + _dot(dscb, kc)
            dq_ref[:, ps] = (jnp.where(lo, dqst[0:GW], dqst[GW:]) * SCALE).astype(bf16)
            dk_s[win, ps] += _dg(dslb, qst, TN) * SCALE
            dv_s[win, ps] += _dg(plb, dost, TN)
            dk_s[T:TA, ps] += _dg(dscb, qst, TN) * SCALE
            dv_s[T:TA, ps] += _dg(pcb, dost, TN)

        @pl.when(r == GW - 1)
        def _():
            c1 = pltpu.make_async_copy(dk_s, dk_hbm, sem.at[0])
            c2 = pltpu.make_async_copy(dv_s, dv_hbm, sem.at[1])
            c1.start()
            c2.start()
            c1.wait()
            c2.wait()

        for _, finish in plans:
            pl.when(r == GW - 1)(finish)

    rowq = lambda: pl.BlockSpec((GW, DA), lambda r: (r, 0))
    hbm = pl.BlockSpec(memory_space=pl.ANY)
    return pl.pallas_call(
        body, name="attn_bwd", grid=(GW,),
        in_specs=[rowq(), _full((TA, DA)), _full((TA, DA)), rowq(), rowq(), pl.BlockSpec((GW, NH), lambda r: (r, 0)),
                  pl.BlockSpec((1, NH, GW, WR * GW), lambda r: (_pattern(r), 0, 0, 0))] + [hbm] * nb,
        out_specs=[rowq(), hbm, hbm, pl.BlockSpec((1, NH, GW, WR * GW), lambda r: (_pattern(r), 0, 0, 0))] + [hbm] * nb,
        out_shape=[jax.ShapeDtypeStruct((T, DA), bf16), jax.ShapeDtypeStruct((TA, DA), f32), jax.ShapeDtypeStruct((TA, DA), f32),
                   jax.ShapeDtypeStruct((8, NH, GW, WR * GW), f32)] + [jax.ShapeDtypeStruct(b.shape, b.dtype) for b in blocks],
        scratch_shapes=[pltpu.VMEM((TA, DA), f32), pltpu.VMEM((TA, DA), f32), pltpu.SemaphoreType.DMA((2,))] + _a2a_scratch() * nb,
        compiler_params=_cp("arbitrary"),
    )(q, k, v, y, dy, lse, bias_tab, *blocks)


def _rpb_reduce(dbias):
    rev = np.eye(GW, dtype=np.float32)[::-1]

    def body(d_ref, rev_ref, o_ref):
        rv = rev_ref[...]
        for h in range(NH):
            dv = d_ref[0, h]
            r0 = dv.astype(bf16)
            e1 = dv - r0.astype(f32)
            r1 = e1.astype(bf16)
            r2 = (e1 - r1.astype(f32)).astype(bf16)
            rr = _dot(rv, r0) + _dot(rv, r1) + _dot(rv, r2)
            skew = pltpu.roll(rr, 0, 1, stride=1, stride_axis=0)
            o_ref[0, h:h + 1, :] = jnp.sum(skew, axis=0, keepdims=True)

    return pl.pallas_call(
        body, name="rpb_reduce", grid=(8,),
        in_specs=[pl.BlockSpec((1, NH, GW, WR * GW), lambda p: (p, 0, 0, 0)), _full((GW, GW))],
        out_specs=pl.BlockSpec((1, NH, WR * GW), lambda p: (p, 0, 0)),
        out_shape=jax.ShapeDtypeStruct((8, NH, WR * GW), f32),
        compiler_params=_cp("parallel"),
    )(dbias, jnp.asarray(rev, dtype=bf16))


def _norm1_bwd(dq, dk, dv, da, dg, w_inT, x0, ctx0, h, dx1, modv, g1):
    tm = 256
    nt = TA // tm
    nx = T // tm

    def body(dq_ref, dk_ref, dv_ref, da_ref, dg_ref, w_ref, x_ref, c_ref, h_ref, dx1_ref, mod_ref, g_ref,
             dx_ref, dwo_ref, s_ref, dw_ref):
        i = pl.program_id(0)
        is_ctx = i == nt - 1

        @pl.when(i == 0)
        def _():
            s_ref[...] = jnp.zeros_like(s_ref)
            dw_ref[...] = jnp.zeros_like(dw_ref)

        hb = h_ref[...]
        dkb, dvb = dk_ref[...].astype(bf16), dv_ref[...].astype(bf16)
        dw_ref[DA:2 * DA, :] += _dg(dkb, hb, TN)
        dw_ref[2 * DA:3 * DA, :] += _dg(dvb, hb, TN)
        dh_kv = _dot(dkb, w_ref[DA:2 * DA, :]) + _dot(dvb, w_ref[2 * DA:3 * DA, :])
        gain = g_ref[...]

        @pl.when(is_ctx)
        def _():
            xv = c_ref[...]
            rstd = lax.rsqrt(jnp.mean(xv * xv, axis=-1, keepdims=True) + EPS)
            _, dsh, dsc, dgn = _norm_bwd(dh_kv, xv, gain, mod_ref[7:8, :], rstd)
            s_ref[2:3, :] += dgn
            s_ref[3:4, :] += dsh
            s_ref[4:5, :] += dsc
            dwo_ref[...] = dw_ref[...].astype(bf16)

        @pl.when(jnp.logical_not(is_ctx))
        def _():
            dqb, dab, dgb = dq_ref[...], da_ref[...], dg_ref[...]
            dw_ref[0:DA, :] += _dg(dqb, hb, TN)
            dw_ref[3 * DA:4 * DA, :] += _dg(dab, hb, TN)
            dw_ref[4 * DA:5 * DA, :] += _dg(dgb, hb, TN)
            dh = (dh_kv + _dot(dqb, w_ref[0:DA, :]) + _dot(dab, w_ref[3 * DA:4 * DA, :])
                  + _dot(dgb, w_ref[4 * DA:5 * DA, :]))
            xv = x_ref[...]
            rstd = lax.rsqrt(jnp.mean(xv * xv, axis=-1, keepdims=True) + EPS)
            dxn, dsh, dsc, dgn = _norm_bwd(dh, xv, gain, mod_ref[1:2, :], rstd)
            dx_ref[...] = dx1_ref[...] + dxn
            s_ref[0:1, :] += dsh
            s_ref[1:2, :] += dsc
            s_ref[2:3, :] += dgn

    row = lambda w: pl.BlockSpec((tm, w), lambda i: (i, 0))
    lrow = lambda w: pl.BlockSpec((tm, w), lambda i: (jnp.minimum(i, nx - 1), 0))
    return pl.pallas_call(
        body, name="norm1_bwd", grid=(nt,),
        in_specs=[lrow(DA), row(DA), row(DA), lrow(DA), lrow(DA), _full((5 * DA, D)), lrow(D), _full((TC, D)), row(D),
                  lrow(D), _full((8, D)), _full((1, D))],
        out_specs=[lrow(D), _full((5 * DA, D)), _full((8, D))],
        out_shape=[jax.ShapeDtypeStruct((T, D), f32), jax.ShapeDtypeStruct((5 * DA, D), bf16), jax.ShapeDtypeStruct((8, D), f32)],
        scratch_shapes=[pltpu.VMEM((5 * DA, D), f32)],
        compiler_params=_cp("arbitrary"),
    )(dq, dk, dv, da, dg, w_inT, x0, ctx0, h, dx1, modv, g1)


def _adam_tile(r, c):
    return max(t for t in range(8, r + 1, 8) if r % t == 0 and t * c * 4 <= 2 * 1024 * 1024)


def _adam_update(wv, gv, mv, vv):
    nm = ADAM_B1 * mv + (1.0 - ADAM_B1) * gv
    nv = ADAM_B2 * vv + (1.0 - ADAM_B2) * (gv * gv)
    m_hat = nm * (1.0 / (1.0 - ADAM_B1 ** ADAM_STEP))
    v_hat = nv * (1.0 / (1.0 - ADAM_B2 ** ADAM_STEP))
    return -ADAM_LR * (m_hat / (jnp.sqrt(v_hat) + ADAM_EPS) + ADAM_WD * wv), nm, nv


def _adamw(w, g, m, v, name):
    r, c = w.shape
    tr = _adam_tile(r, c)

    def body(w_ref, g_ref, m_ref, v_ref, d_ref, nm_ref, nv_ref):
        d_ref[...], nm_ref[...], nv_ref[...] = _adam_update(w_ref[...], g_ref[...], m_ref[...], v_ref[...])

    spec = pl.BlockSpec((tr, c), lambda i: (i, 0))
    return pl.pallas_call(
        body, name=name, grid=(r // tr,),
        in_specs=[spec] * 4, out_specs=[spec] * 3,
        out_shape=[jax.ShapeDtypeStruct((r, c), f32)] * 3,
        compiler_params=_cp("parallel"),
    )(w, g, m, v)


def _adamw_blocks(w, recv, m, v, name):
    r, c = w.shape
    tr = _adam_tile(r, c)

    def body(w_ref, a_ref, m_ref, v_ref, g_ref, d_ref, nm_ref, nv_ref):
        gv = a_ref[0].astype(f32)
        for d in range(1, NDEV):
            gv = gv + a_ref[d].astype(f32)
        g_ref[...] = gv
        d_ref[...], nm_ref[...], nv_ref[...] = _adam_update(w_ref[...], gv, m_ref[...], v_ref[...])

    spec = pl.BlockSpec((tr, c), lambda i: (i, 0))
    return pl.pallas_call(
        body, name=name, grid=(r // tr,),
        in_specs=[spec, pl.BlockSpec((NDEV, tr, c), lambda i: (0, i, 0)), spec, spec], out_specs=[spec] * 4,
        out_shape=[jax.ShapeDtypeStruct((r, c), f32)] * 4,
        compiler_params=_cp("parallel"),
    )(w, recv, m, v)


def _pad_rows128(vec):
    n = vec.shape[0]
    rows = -(-n // 1024) * 8
    return jnp.pad(vec, (0, rows * 128 - n)).reshape(rows, 128)


def _grad_rpb(dbias):
    lane_map, r_hot = _rpb_tables()
    return jnp.einsum("phl,lic,pir->hrc", _rpb_reduce(dbias), jnp.asarray(lane_map), jnp.asarray(r_hot),
                      precision=lax.Precision.HIGHEST)


def kernel(x, c, ctx, c_ctx, w_mod, b_mod, g_norm1, w_in, rpb, conv_w, conv_b, ln_g, ln_b, w_out, g_norm2, w_up, ffn_conv_w, ffn_conv_b, w_down, g_final, loss_target, m_c_ctx, m_w_mod, m_b_mod, m_g_norm1, m_w_in, m_rpb, m_conv_w, m_conv_b, m_ln_g, m_ln_b, m_w_out, m_g_norm2, m_w_up, m_ffn_conv_w, m_ffn_conv_b, m_w_down, m_g_final, v_c_ctx, v_w_mod, v_b_mod, v_g_norm1, v_w_in, v_rpb, v_conv_w, v_conv_b, v_ln_g, v_ln_b, v_w_out, v_g_norm2, v_w_up, v_ffn_conv_w, v_ffn_conv_b, v_w_down, v_g_final):
    me = 4 * lax.axis_index("x") + 2 * lax.axis_index("y") + lax.axis_index("c")
    nmod = w_mod.shape[2]
    n_in = w_in.shape[2]
    n_out = w_out.shape[1]
    n_up = w_up.shape[2]
    n_dn = w_down.shape[1]
    n_cw = conv_w.shape[2]

    b_sh = lax.dynamic_slice(b_mod, (0, me * nmod), (1, nmod))
    convpay = _pad_rows128(jnp.concatenate([conv_w[0].reshape(-1), ffn_conv_w[0].reshape(-1)]))
    w_inT, c_all, mod_all, flat, bias_tab = _head(w_in[0].T.astype(bf16), c.reshape(8, 128), c_ctx.reshape(8, 128),
                                                  w_mod[0], b_sh, convpay, *_bias_table_inputs(rpb[0]))
    cvec = jnp.concatenate([c_all.reshape(NDEV, D), c_ctx[None, :], jnp.zeros((7, D), f32)], axis=0)
    mod_all = mod_all.transpose(1, 0, 2).reshape(16, 6 * D)
    mod_me = lax.dynamic_index_in_dim(mod_all, me, 0, keepdims=False).reshape(6, D)
    mod_c = mod_all[8]
    modv = jnp.concatenate([mod_me, mod_c[None, 0:D], mod_c[None, D:2 * D]], axis=0)
    flat = flat.reshape(NDEV, -1)
    o1 = CW * n_cw
    conv_w_f = flat[:, :o1].reshape(NDEV, CW, n_cw).transpose(1, 0, 2).reshape(CW, DA)
    fcw_f = flat[:, o1:o1 + 3 * n_up].reshape(NDEV, 3, n_up).transpose(1, 0, 2).reshape(3, F2)

    x0, ctx0 = x[0], ctx[0]
    h, q, k, v, a, g, w_down_f = _in_proj(x0, ctx0, g_norm1, modv, w_inT, [w_down[0].astype(bf16)])
    y_cv, cv, w_out_f = _conf_fwd(a, g, conv_w_f, conv_b, ln_g, ln_b, [w_out[0].astype(bf16)])
    y_na, lse, w_upT = _attn_fwd(q, k, v, bias_tab, [w_up[0].T.astype(bf16)])
    x1, proj, h2 = _out_proj(x0, y_na, y_cv, w_out_f, modv, g_norm2)
    ffn, u_t, u2_t = _ffn_fwd(h2, w_upT, fcw_f, ffn_conv_b, w_down_f)
    dx2, dffn, s_loss = _loss_bwd(ffn, x1, loss_target[0], modv, g_final[None, :])

    dug, duv, dw_up, dw_down, dcwg, dcwv, dcbg, dcbv = _ffn_bwd(h2, dffn, u_t, u2_t, fcw_f, w_down_f)
    dx1, dy_na, dy_cv, dw_out, s_n2 = _norm2_bwd(dug, duv, w_upT, x1, dx2, proj, w_out_f, y_na, y_cv, modv, g_norm2)
    da, dg, dcw, s_cf, rv_down = _conf_bwd(a, g, cv, dy_cv, conv_w_f, ln_g, ln_b, [dw_down.reshape(NDEV, n_dn, D)])
    dq, dk, dv, dbias, rv_up = _attn_bwd(q, k, v, y_na, dy_na, lse, bias_tab, [dw_up.reshape(NDEV, n_up, D)])
    grad_rpb_part = _grad_rpb(dbias)
    grad_x, dw_inT, s_n1 = _norm1_bwd(dq, dk, dv, da, dg, w_inT, x0, ctx0, h, dx1, modv, g_norm1)
    grad_x = grad_x[None]
    dfcw = jnp.concatenate([dcwg, dcwv], axis=1)
    dfcb = jnp.concatenate([dcbg[0], dcbv[0]])
    small = jnp.concatenate([dcw[:CW].reshape(CW, NDEV, n_cw).transpose(1, 0, 2).reshape(NDEV, CW * n_cw),
                             dfcw.reshape(3, NDEV, n_up).transpose(1, 0, 2).reshape(NDEV, 3 * n_up)], axis=1)
    n_sm = small.shape[1] // D
    pad_sm = -n_sm % 16
    small = jnp.pad(small.reshape(NDEV, n_sm, D), ((0, 0), (0, pad_sm), (0, 0))).astype(bf16)
    dmod = jnp.concatenate([s_n1[0], s_n1[1], s_n2[3], s_n2[0], s_n2[1], s_loss[1]])
    dmodc = jnp.concatenate([s_n1[3], s_n1[4]])
    parts = [dmodc, s_n1[2], grad_rpb_part.reshape(-1), s_cf[0], s_cf[1], s_cf[2], s_n2[2], dfcb, s_loss[0], s_loss[3, 0:1]]
    sizes = [p.shape[0] for p in parts]
    pvec = _pad_rows128(jnp.concatenate([dmod] + parts))
    slab = jnp.concatenate([dw_inT.reshape(NDEV, n_in, D), small, dw_out.reshape(NDEV, n_out, D)], axis=1)
    r_a, gath = _reduce_scatter2(slab, [pvec], "rs_w_in")
    g_w_out = r_a[n_in + n_sm + pad_sm:]
    g_w_inT = r_a[:n_in]
    sm = r_a[n_in:n_in + n_sm].reshape(-1)
    g_conv_w = sm[:CW * n_cw].reshape(CW, n_cw)
    g_fcw = sm[CW * n_cw:].reshape(3, n_up)
    tot = _sum_rows8(gath, "sum_small").reshape(-1)
    dmod_all = gath.reshape(NDEV, -1)[:, :6 * D]
    offs = np.cumsum([6 * D] + sizes)
    pick = lambda j: tot[offs[j]:offs[j + 1]]
    dmodc_t = jnp.pad(pick(0), (0, 4 * D))
    g_b_mod = (tot[:6 * D] + dmodc_t)[None, :]
    g_g_norm1 = pick(1)[None, :]
    g_rpb = pick(2).reshape(1, NH, 2 * WR - 1, 2 * NCOL - 1)
    g_conv_b, g_ln_g, g_ln_b = pick(3)[None, :], pick(4)[None, :], pick(5)[None, :]
    g_g_norm2 = pick(6)[None, :]
    g_fcb = pick(7)[None, :]
    g_g_final = pick(8)
    loss = pick(9)[0]
    dm_rows = jnp.concatenate([dmod_all, dmodc_t[None, :], jnp.zeros((7, 6 * D), f32)], axis=0)
    dm_sh = lax.dynamic_slice(dm_rows, (0, me * nmod), (16, nmod))
    g_w_mod, gc_part = _mod_bwd(cvec, dm_sh, w_mod[0])
    gc_sum = _sum_rows8(_small_allgather(gc_part[0].reshape(8, 128), "ag_cctx"), "sum_cctx").reshape(D)
    sg_c = _sigmoid(c_ctx)
    g_c_ctx = gc_sum * (sg_c * (1.0 + c_ctx * (1.0 - sg_c)))

    big = [("w_mod", w_mod[0], g_w_mod, m_w_mod[0], v_w_mod[0]), ("w_in", w_in[0].T, g_w_inT, m_w_in[0].T, v_w_in[0].T),
           ("w_out", w_out[0], g_w_out, m_w_out[0], v_w_out[0])]
    upd = {n: _adamw(wv, gv, mv, vv, "adamw_" + n) for n, wv, gv, mv, vv in big}
    g_w_upT, *upd["w_up"] = _adamw_blocks(w_up[0].T, rv_up, m_w_up[0].T, v_w_up[0].T, "adamw_w_up")
    g_w_down, *upd["w_down"] = _adamw_blocks(w_down[0], rv_down, m_w_down[0], v_w_down[0], "adamw_w_down")
    for n in ("w_in", "w_up"):
        upd[n] = tuple(arr.T for arr in upd[n])
    g_w_in, g_w_up = g_w_inT.T, g_w_upT.T
    smalls = [("c_ctx", c_ctx, g_c_ctx, m_c_ctx, v_c_ctx), ("b_mod", b_mod, g_b_mod, m_b_mod, v_b_mod),
              ("g_norm1", g_norm1, g_g_norm1, m_g_norm1, v_g_norm1), ("rpb", rpb, g_rpb, m_rpb, v_rpb),
              ("conv_w", conv_w, g_conv_w[None], m_conv_w, v_conv_w), ("conv_b", conv_b, g_conv_b, m_conv_b, v_conv_b),
              ("ln_g", ln_g, g_ln_g, m_ln_g, v_ln_g), ("ln_b", ln_b, g_ln_b, m_ln_b, v_ln_b),
              ("g_norm2", g_norm2, g_g_norm2, m_g_norm2, v_g_norm2),
              ("ffn_conv_w", ffn_conv_w, g_fcw[None], m_ffn_conv_w, v_ffn_conv_w),
              ("ffn_conv_b", ffn_conv_b, g_fcb, m_ffn_conv_b, v_ffn_conv_b), ("g_final", g_final, g_g_final, m_g_final, v_g_final)]
    packed = [_pad_rows128(jnp.concatenate([t[j].reshape(-1) for t in smalls])) for j in (1, 2, 3, 4)]
    sd, sm_, sv = _adamw(*packed, "adamw_small")
    so = np.cumsum([0] + [int(np.prod(t[1].shape)) for t in smalls])
    for j, t in enumerate(smalls):
        shp = t[1].shape
        upd[t[0]] = tuple(arr.reshape(-1)[so[j]:so[j + 1]].reshape(shp) for arr in (sd, sm_, sv))
    grads = {"c_ctx": g_c_ctx, "w_mod": g_w_mod[None], "b_mod": g_b_mod, "g_norm1": g_g_norm1, "w_in": g_w_in[None],
             "rpb": g_rpb, "conv_w": g_conv_w[None], "conv_b": g_conv_b, "ln_g": g_ln_g, "ln_b": g_ln_b,
             "w_out": g_w_out[None], "g_norm2": g_g_norm2, "w_up": g_w_up[None], "ffn_conv_w": g_fcw[None],
             "ffn_conv_b": g_fcb, "w_down": g_w_down[None], "g_final": g_g_final}
    names = ["c_ctx", "w_mod", "b_mod", "g_norm1", "w_in", "rpb", "conv_w", "conv_b", "ln_g", "ln_b", "w_out", "g_norm2",
             "w_up", "ffn_conv_w", "ffn_conv_b", "w_down", "g_final"]
    shapes = {n: grads[n].shape for n in names}
    outs = [loss, grad_x] + [grads[n] for n in names]
    for j in range(3):
        outs += [upd[n][j].reshape(shapes[n]) for n in names]
    return tuple(outs)
```

```python
import numpy as np
import jax
import jax.numpy as jnp
from jax import lax
from jax.experimental import pallas as pl
from jax.experimental.pallas import tpu as pltpu

f32 = jnp.float32
bf16 = jnp.bfloat16

D = 1024
T = 4096
TC = 256
TA = T + TC
DA = 512
NH = 8
HD = 64
GW = 64
WR = 8
NCOL = 16
F = 2816
F2 = 2 * F
CW = 31
NDEV = 8
EPS = 1e-6
SCALE = HD ** -0.5
NEG = -1e30
MESH = pl.DeviceIdType.MESH

NT = (((1,), (1,)), ((), ()))
TN = (((0,), (0,)), ((), ()))

ADAM_LR, ADAM_B1, ADAM_B2, ADAM_EPS, ADAM_WD, ADAM_STEP = 0.001, 0.9, 0.999, 1e-08, 0.01, 10

VMEM_LIMIT = 56 * 1024 * 1024


def _cp(*sem):
    return pltpu.CompilerParams(dimension_semantics=sem or None, vmem_limit_bytes=VMEM_LIMIT)


def _dot(a, b):
    return jnp.dot(a, b, preferred_element_type=f32)


def _dg(a, b, dims):
    return lax.dot_general(a, b, dims, preferred_element_type=f32)


def _sigmoid(x):
    return 1.0 / (1.0 + jnp.exp(-x))


def _full(shape):
    n = len(shape)
    return pl.BlockSpec(shape, lambda *_: (0,) * n)


def _resident(shape):
    n = len(shape)
    return pl.BlockSpec(shape, lambda *_: (0,) * n, pipeline_mode=pl.Buffered(1))


def _my_pos():
    return lax.axis_index("x"), lax.axis_index("y"), lax.axis_index("c")


def _small_gather_plan(v_ref, out_ref, send_sems, recv_sems):
    x, y, c = _my_pos()
    me = 4 * x + 2 * y + c
    peers = []
    for k in range(1, NDEV):
        kx, ky, kc = (k >> 2) & 1, (k >> 1) & 1, k & 1
        peers.append((x ^ kx, y ^ ky, c ^ kc))

    def copy(k, slot, to):
        return pltpu.make_async_remote_copy(
            src_ref=v_ref, dst_ref=out_ref.at[slot], send_sem=send_sems.at[k], recv_sem=recv_sems.at[k],
            device_id=to, device_id_type=MESH)

    def start():
        out_ref[me] = v_ref[...]
        for k, p in enumerate(peers):
            copy(k, me, p).start()

    def finish():
        for k, (px, py, pc) in enumerate(peers):
            copy(k, 4 * px + 2 * py + pc, (x, y, c)).wait_recv()
        for k, p in enumerate(peers):
            copy(k, me, p).wait_send()

    return start, finish


def _small_gather_scratch():
    return [pltpu.SemaphoreType.DMA((NDEV - 1,)), pltpu.SemaphoreType.DMA((NDEV - 1,))]


def _small_allgather(v, name):
    n = v.shape[0]

    def body(v_ref, out_ref, send_sems, recv_sems):
        start, finish = _small_gather_plan(v_ref, out_ref, send_sems, recv_sems)
        start()
        finish()

    return pl.pallas_call(
        body, name=name,
        out_shape=jax.ShapeDtypeStruct((NDEV, n, 128), f32),
        in_specs=[pl.BlockSpec(memory_space=pltpu.VMEM)],
        out_specs=pl.BlockSpec(memory_space=pltpu.VMEM),
        scratch_shapes=_small_gather_scratch(),
    )(v)


def _ag2_plan(x_refs, out_refs, send_sems, recv_sems, local_sems):
    na = len(x_refs)
    x, y, c = _my_pos()
    me, sibling = (x, y, c), (x, y, 1 - c)
    chips = [(1 - x, y), (x, 1 - y), (1 - x, 1 - y)]

    def rows(i, px, py, pc):
        m_per = x_refs[i].shape[0]
        return out_refs[i].at[pl.ds(pl.multiple_of((4 * px + 2 * py + pc) * m_per, 16 if m_per % 16 == 0 else 8), m_per), :]

    def copies(k, block, to, from_shard=False):
        return [pltpu.make_async_remote_copy(
            src_ref=x_refs[i] if from_shard else rows(i, *block), dst_ref=rows(i, *block),
            send_sem=send_sems.at[k * na + i], recv_sem=recv_sems.at[k * na + i], device_id=to, device_id_type=MESH)
            for i in range(na)]

    def mine():
        return [pltpu.make_async_copy(x_refs[i], rows(i, *me), local_sems.at[i]) for i in range(na)]

    def first():
        cps = copies(0, me, sibling, True)
        for j, chip in enumerate(chips):
            cps += copies(1 + j, me, (*chip, c), True)
        return cps

    def start():
        for cp in mine() + first():
            cp.start()

    def forward():
        for j, chip in enumerate(chips):
            for cp in copies(1 + j, (*chip, c), me):
                cp.wait_recv()
            for cp in copies(4 + j, (*chip, c), sibling):
                cp.start()

    def finish():
        for cp in copies(0, sibling, me):
            cp.wait_recv()
        for j, chip in enumerate(chips):
            for cp in copies(4 + j, (*chip, 1 - c), me):
                cp.wait_recv()
        for cp in first():
            cp.wait_send()
        for j, chip in enumerate(chips):
            for cp in copies(4 + j, (*chip, c), sibling):
                cp.wait_send()
        for cp in mine():
            cp.wait()

    return start, forward, finish


def _ag2_scratch(na):
    return [pltpu.SemaphoreType.DMA((7 * na,)), pltpu.SemaphoreType.DMA((7 * na,)), pltpu.SemaphoreType.DMA((na,))]


def _a2a_plan(g_ref, recv_ref, send_sems, recv_sems, local_sem):
    x, y, c = _my_pos()
    me = 4 * x + 2 * y + c
    peers = []
    for k in range(1, NDEV):
        kx, ky, kc = (k >> 2) & 1, (k >> 1) & 1, k & 1
        peers.append((x ^ kx, y ^ ky, c ^ kc))

    def sends():
        return [pltpu.make_async_remote_copy(
            src_ref=g_ref.at[4 * px + 2 * py + pc], dst_ref=recv_ref.at[me], send_sem=send_sems.at[k], recv_sem=recv_sems.at[k],
            device_id=(px, py, pc), device_id_type=MESH) for k, (px, py, pc) in enumerate(peers)]

    def own():
        return pltpu.make_async_copy(g_ref.at[me], recv_ref.at[me], local_sem)

    def start():
        own().start()
        for cp in sends():
            cp.start()

    def finish():
        for k, (px, py, pc) in enumerate(peers):
            pltpu.make_async_remote_copy(
                src_ref=g_ref.at[me], dst_ref=recv_ref.at[4 * px + 2 * py + pc], send_sem=send_sems.at[k],
                recv_sem=recv_sems.at[k], device_id=(x, y, c), device_id_type=MESH).wait_recv()
        for cp in sends():
            cp.wait_send()
        own().wait()

    return start, finish


def _a2a_scratch():
    return [pltpu.SemaphoreType.DMA((NDEV - 1,)), pltpu.SemaphoreType.DMA((NDEV - 1,)), pltpu.SemaphoreType.DMA]


def _reduce_scatter2(g, small, name):
    _, r, n = g.shape
    ch = 16
    nch = r // ch
    ns = len(small)

    def body(g_ref, *rest):
        v_refs, out_ref, vout_refs = rest[:ns], rest[ns], rest[ns + 1:2 * ns + 1]
        a_ref, h_ref, b_ref, s1_send, s1_recv, s2_send, s2_recv = rest[2 * ns + 1:2 * ns + 8]
        gather_sems = rest[2 * ns + 8:]
        gathers = [_small_gather_plan(v_refs[i], vout_refs[i], *gather_sems[2 * i:2 * i + 2]) for i in range(ns)]
        for start, _ in gathers:
            start()
        x, y, c = _my_pos()
        sibling = (x, y, 1 - c)
        s1 = []
        for j in range(4):
            cp = pltpu.make_async_remote_copy(
                src_ref=g_ref.at[2 * j + (1 - c)], dst_ref=a_ref.at[j], send_sem=s1_send.at[j], recv_sem=s1_recv.at[j],
                device_id=sibling, device_id_type=MESH)
            cp.start()
            s1.append(cp)
        for cp in s1:
            cp.wait_recv()

        def add1(i, _):
            rr = pl.ds(pl.multiple_of(i * ch, ch), ch)
            for j in range(4):
                h_ref[j, rr, :] = (g_ref[2 * j + c, rr, :].astype(f32) + a_ref[j, rr, :].astype(f32)).astype(bf16)
            return 0
        lax.fori_loop(0, nch, add1, 0)
        mychip = 2 * x + y
        s2 = []
        for m in range(1, 4):
            mx, my_ = (m >> 1) & 1, m & 1
            px, py = x ^ mx, y ^ my_
            cp = pltpu.make_async_remote_copy(
                src_ref=h_ref.at[2 * px + py], dst_ref=b_ref.at[m - 1], send_sem=s2_send.at[m - 1], recv_sem=s2_recv.at[m - 1],
                device_id=(px, py, c), device_id_type=MESH)
            cp.start()
            s2.append(cp)
        for cp in s2:
            cp.wait_recv()

        def add2(i, _):
            rr = pl.ds(pl.multiple_of(i * ch, ch), ch)
            acc = h_ref[mychip, rr, :].astype(f32)
            for m in range(3):
                acc = acc + b_ref[m, rr, :].astype(f32)
            out_ref[rr, :] = acc
            return 0
        lax.fori_loop(0, nch, add2, 0)
        for cp in s1 + s2:
            cp.wait_send()
        for _, finish in gathers:
            finish()

    vmem = pl.BlockSpec(memory_space=pltpu.VMEM)
    return pl.pallas_call(
        body, name=name,
        out_shape=[jax.ShapeDtypeStruct((r, n), f32)] + [jax.ShapeDtypeStruct((NDEV,) + v.shape, v.dtype) for v in small],
        in_specs=[vmem] * (1 + ns),
        out_specs=[vmem] * (1 + ns),
        scratch_shapes=[pltpu.VMEM((4, r, n), bf16), pltpu.VMEM((4, r, n), bf16), pltpu.VMEM((3, r, n), bf16),
                        pltpu.SemaphoreType.DMA((4,)), pltpu.SemaphoreType.DMA((4,)),
                        pltpu.SemaphoreType.DMA((3,)), pltpu.SemaphoreType.DMA((3,))] + _small_gather_scratch() * ns,
        compiler_params=pltpu.CompilerParams(vmem_limit_bytes=VMEM_LIMIT),
    )(g, *small)


def _head(w_in_sh, c8, cctx8, w_mod_sh, b_sh, convpay, bias_base, bias_valid):
    nmod = w_mod_sh.shape[1]
    npay = convpay.shape[0]

    def body(w_ref, c_ref, cc_ref, wm_ref, b_ref, pay_ref, base_ref, valid_ref,
             wout_ref, call_ref, mall_ref, pall_ref, bias_ref, mod_s,
             ag_send, ag_recv, ag_local, c_send, c_recv, m_send, m_recv, p_send, p_recv):
        start, forward, finish = _ag2_plan([w_ref], [wout_ref], ag_send, ag_recv, ag_local)
        c_start, c_finish = _small_gather_plan(c_ref, call_ref, c_send, c_recv)
        m_start, m_finish = _small_gather_plan(mod_s, mall_ref, m_send, m_recv)
        p_start, p_finish = _small_gather_plan(pay_ref, pall_ref, p_send, p_recv)
        c_start()
        start()
        p_start()
        _write_bias_table(base_ref, valid_ref, bias_ref)
        c_finish()
        acc = jnp.zeros((16, nmod), f32)
        for j in range(D // 128):
            rows = jnp.concatenate([call_ref[:, j, :], cc_ref[j:j + 1, :], jnp.zeros((7, 128), f32)], axis=0)
            act = (rows * _sigmoid(rows)).astype(bf16)
            acc = acc + _dot(act, wm_ref[j * 128:(j + 1) * 128, :].astype(bf16))
        mod_s[...] = acc + b_ref[...]
        m_start()
        forward()
        finish()
        m_finish()
        p_finish()

    vmem = pl.BlockSpec(memory_space=pltpu.VMEM)
    return pl.pallas_call(
        body, name="head",
        out_shape=[jax.ShapeDtypeStruct((NDEV * w_in_sh.shape[0], D), bf16), jax.ShapeDtypeStruct((NDEV, 8, 128), f32),
                   jax.ShapeDtypeStruct((NDEV, 16, nmod), f32), jax.ShapeDtypeStruct((NDEV, npay, 128), f32),
                   jax.ShapeDtypeStruct((8, NH, GW, WR * GW), f32)],
        in_specs=[vmem] * 8, out_specs=[vmem] * 5,
        scratch_shapes=[pltpu.VMEM((16, nmod), f32)] + _ag2_scratch(1) + _small_gather_scratch() * 3,
        compiler_params=pltpu.CompilerParams(vmem_limit_bytes=VMEM_LIMIT),
    )(w_in_sh, c8, cctx8, w_mod_sh, b_sh, convpay, bias_base, bias_valid)


def _mod_bwd(cvec, dm_sh, w_sh):
    def body(c_ref, dm_ref, w_ref, gw_ref, gc_ref):
        cv = c_ref[...]
        act = (cv * _sigmoid(cv)).astype(bf16)
        gw_ref[...] = _dg(act, dm_ref[...].astype(bf16), TN)
        gc_ref[...] = _dg(dm_ref[8:16, :].astype(bf16), w_ref[...].astype(bf16), NT)
    return pl.pallas_call(
        body, name="mod_bwd",
        out_shape=(jax.ShapeDtypeStruct(w_sh.shape, f32), jax.ShapeDtypeStruct((8, D), f32)))(cvec, dm_sh, w_sh)


def _sum_rows8(a, name):
    n = a.shape[1]

    def body(a_ref, o_ref):
        acc = a_ref[0]
        for d in range(1, NDEV):
            acc = acc + a_ref[d]
        o_ref[...] = acc
    return pl.pallas_call(body, name=name, out_shape=jax.ShapeDtypeStruct((n, 128), f32))(a)


def _in_proj(x0, ctx0, g1, modv, w_inT, shards):
    tm = 256
    nt = TA // tm
    nx = T // tm
    na = len(shards)

    def body(x_ref, c_ref, g_ref, mod_ref, w_ref, *rest):
        x_refs, (h_ref, q_ref, k_ref, v_ref, a_ref, gg_ref) = rest[:na], rest[na:na + 6]
        out_refs, sems = rest[na + 6:2 * na + 6], rest[2 * na + 6:]
        i = pl.program_id(0)
        if na:
            start, forward, finish = _ag2_plan(x_refs, out_refs, *sems)
            pl.when(i == 0)(start)
            pl.when(i == nt - 2)(forward)
        is_ctx = i == nt - 1
        xv = jnp.where(is_ctx, c_ref[...], x_ref[...])
        rstd = lax.rsqrt(jnp.mean(xv * xv, axis=-1, keepdims=True) + EPS)
        sh = jnp.where(is_ctx, mod_ref[6:7, :], mod_ref[0:1, :])
        sc = jnp.where(is_ctx, mod_ref[7:8, :], mod_ref[1:2, :])
        h = ((xv * rstd * g_ref[...]) * (1.0 + sc) + sh).astype(bf16)
        h_ref[...] = h
        for j, o_ref in enumerate((q_ref, k_ref, v_ref, a_ref, gg_ref)):
            o_ref[...] = _dg(h, w_ref[j * DA:(j + 1) * DA, :], NT).astype(o_ref.dtype)
        if na:
            pl.when(is_ctx)(finish)

    row = lambda w: pl.BlockSpec((tm, w), lambda i: (i, 0))
    hbm = pl.BlockSpec(memory_space=pl.ANY)
    return pl.pallas_call(
        body, name="in_proj", grid=(nt,),
        in_specs=[pl.BlockSpec((tm, D), lambda i: (jnp.minimum(i, nx - 1), 0)), _full((TC, D)),
                  _full((1, D)), _full((8, D)), _full((5 * DA, D))] + [hbm] * na,
        out_specs=[row(D), row(DA), row(DA), row(DA), row(DA), row(DA)] + [hbm] * na,
        out_shape=[jax.ShapeDtypeStruct((TA, D), bf16)] + [jax.ShapeDtypeStruct((TA, DA), bf16)] * 3
                  + [jax.ShapeDtypeStruct((TA, DA), f32)] * 2
                  + [jax.ShapeDtypeStruct((NDEV * sh.shape[0], sh.shape[1]), sh.dtype) for sh in shards],
        scratch_shapes=_ag2_scratch(na) if na else [],
        compiler_params=_cp("arbitrary"),
    )(x0, ctx0, g1, modv, w_inT, *shards)


def _win_start(r):
    return jnp.clip(r - WR // 2, 0, GW - WR)


def _pattern(r):
    return _win_start(r) - r + (WR - 1)


def _bias_table_inputs(rpb):
    qc = np.arange(GW)[:, None]
    kc = np.arange(GW)[None, :]
    cs = np.clip(qc - NCOL // 2, 0, GW - NCOL)
    valid = np.tile(((kc >= cs) & (kc < cs + NCOL)).astype(np.int32), (1, WR))
    pad = jnp.pad(rpb, ((0, 0), (0, 0), (0, GW - (2 * NCOL - 1))))
    base = jnp.stack([pad[:, p:p + WR, :].reshape(NH, WR * GW) for p in range(8)])
    return base, jnp.asarray(valid)


def _write_bias_table(base_ref, valid_ref, o_ref):
    ok = valid_ref[...] != 0
    for p in range(8):
        for h in range(NH):
            row = jnp.broadcast_to(base_ref[p, h:h + 1, :], (GW, WR * GW))
            skew = pltpu.roll(row, WR * GW - (NCOL - 1), 1, stride=1, stride_axis=0)
            o_ref[p, h] = jnp.where(ok, skew, NEG)


def _rpb_tables():
    lane_map = np.zeros((WR * GW, WR, 2 * NCOL - 1), np.float32)
    for i in range(WR):
        for t in range(GW):
            if t >= GW - NCOL:
                lane_map[i * GW + t, i, t - (GW - NCOL)] = 1.0
            elif t < NCOL - 1:
                lane_map[i * GW + t, (i - 1) % WR, t + NCOL] = 1.0
    p = np.arange(8)[:, None]
    i = np.arange(WR)[None, :]
    r_hot = ((p + i)[:, :, None] == np.arange(2 * WR - 1)[None, None, :]).astype(np.float32)
    return lane_map, r_hot


ATTN_RPS = 4
AG_FORWARD_STEP = 13


def _stack_pair(x2, lo):
    z = jnp.zeros_like(x2)
    return jnp.concatenate([jnp.where(lo, x2, z), jnp.where(lo, z, x2)], axis=0)


def _attn_fwd(q, k, v, bias_tab, shards):
    na = len(shards)
    rps = ATTN_RPS
    nsteps = GW // rps

    def body(q_ref, k_ref, v_ref, *rest):
        b_refs, rest = rest[:rps], rest[rps:]
        x_refs, (y_ref, lse_ref), out_refs, sems = rest[:na], rest[na:na + 2], rest[na + 2:2 * na + 2], rest[2 * na + 2:]
        i = pl.program_id(0)
        if na:
            start, forward, finish = _ag2_plan(x_refs, out_refs, *sems)
            pl.when(i == 0)(start)
            pl.when(i == AG_FORWARD_STEP)(forward)
        lo = lax.broadcasted_iota(jnp.int32, (GW, 2 * HD), 1) < HD
        for rr in range(rps):
            rows = slice(rr * GW, (rr + 1) * GW)
            ks = pl.multiple_of(_win_start(i * rps + rr) * GW, GW)
            qq = q_ref[rows, :]
            kv, scores = [], []
            for pr in range(NH // 2):
                ps = slice(pr * 2 * HD, (pr + 1) * 2 * HD)
                qst = _stack_pair(qq[:, ps], lo)
                kw, kc = k_ref[pl.ds(ks, WR * GW), ps], k_ref[T:TA, ps]
                kv.append((v_ref[pl.ds(ks, WR * GW), ps], v_ref[T:TA, ps]))
                bias2 = b_refs[rr][0, 2 * pr:2 * pr + 2].reshape(2 * GW, WR * GW)
                scores.append((_dg(qst, kw, NT) * SCALE + bias2, _dg(qst, kc, NT) * SCALE))
            probs = []
            for pr, (sl, sc) in enumerate(scores):
                m = jnp.maximum(jnp.max(sl, axis=-1, keepdims=True), jnp.max(sc, axis=-1, keepdims=True))
                pl_ = jnp.exp(sl - m)
                pc = jnp.exp(sc - m)
                l = jnp.sum(pl_, axis=-1, keepdims=True) + jnp.sum(pc, axis=-1, keepdims=True)
                lse = m + jnp.log(l)
                lse_ref[rows, 2 * pr:2 * pr + 1] = lse[0:GW]
                lse_ref[rows, 2 * pr + 1:2 * pr + 2] = lse[GW:]
                probs.append((pl_.astype(bf16), pc.astype(bf16), 1.0 / l))
            for pr in range(NH // 2):
                ps = slice(pr * 2 * HD, (pr + 1) * 2 * HD)
                vw, vc = kv[pr]
                pb, cb, rl = probs[pr]
                o = (_dot(pb, vw) + _dot(cb, vc)) * rl
                y_ref[rows, ps] = jnp.where(lo, o[0:GW], o[GW:]).astype(bf16)
        if na:
            pl.when(i == nsteps - 1)(finish)

    hbm = pl.BlockSpec(memory_space=pl.ANY)
    rowq = lambda w: pl.BlockSpec((rps * GW, w), lambda i: (i, 0))
    bias = [pl.BlockSpec((1, NH, GW, WR * GW), lambda i, rr=rr: (_pattern(i * rps + rr), 0, 0, 0)) for rr in range(rps)]
    return pl.pallas_call(
        body, name="attn_fwd", grid=(nsteps,),
        in_specs=[rowq(DA), _full((TA, DA)), _full((TA, DA))] + bias + [hbm] * na,
        out_specs=[rowq(DA), rowq(NH)] + [hbm] * na,
        out_shape=[jax.ShapeDtypeStruct((T, DA), bf16), jax.ShapeDtypeStruct((T, NH), f32)]
                  + [jax.ShapeDtypeStruct((NDEV * s.shape[0], s.shape[1]), s.dtype) for s in shards],
        scratch_shapes=_ag2_scratch(na) if na else [],
        compiler_params=_cp("arbitrary"),
    )(q, k, v, *([bias_tab] * rps), *shards)


CONV_TT = 512
HALO = 16


def _halo_specs(tt, w, nrows_blocks):
    per = tt // HALO
    prev = pl.BlockSpec((HALO, w), lambda i: (jnp.maximum(i * per - 1, 0), 0))
    cur = pl.BlockSpec((tt, w), lambda i: (i, 0))
    nxt = pl.BlockSpec((HALO, w), lambda i: (jnp.minimum((i + 1) * per, nrows_blocks - 1), 0))
    return [prev, cur, nxt]


def _shifted_copies(rot, wn):
    for b in range(1, 8):
        rot[b, 0:wn - 8, :] = rot[0, pl.ds(b, wn - 8), :]


def _conf_fwd(a, g, conv_w, conv_b, ln_g, ln_b, shards):
    tt = CONV_TT
    nt = T // tt
    sub = 32
    wn = tt + 2 * HALO
    na = len(shards)

    def body(ap, ac, an, gp, gc, gn, w_ref, b_ref, lg_ref, lb_ref, *rest):
        x_refs, (y_ref, cv_ref), out_refs, (rot, *sems) = rest[:na], rest[na:na + 2], rest[na + 2:2 * na + 2], rest[2 * na + 2:]
        i = pl.program_id(0)
        if na:
            start, forward, finish = _ag2_plan(x_refs, out_refs, *sems)
            pl.when(i == 0)(start)
            pl.when(i == nt // 2)(forward)
        rot[0, 0:HALO, :] = jnp.where(i > 0, ap[...] * _sigmoid(gp[...]), 0.0)
        rot[0, HALO:HALO + tt, :] = ac[...] * _sigmoid(gc[...])
        rot[0, HALO + tt:, :] = jnp.where(i < nt - 1, an[...] * _sigmoid(gn[...]), 0.0)
        _shifted_copies(rot, wn)
        w = w_ref[...]
        for s in range(tt // sub):
            acc = jnp.zeros((sub, DA), f32)
            for j in range(CW):
                a8, b8 = divmod(1 + j, 8)
                acc = acc + rot[b8, pl.ds(s * sub + 8 * a8, sub), :] * w[j:j + 1, :]
            cv = acc + b_ref[...]
            cv_ref[pl.ds(s * sub, sub), :] = cv
            mu = jnp.mean(cv, axis=-1, keepdims=True)
            xc = cv - mu
            rstd = lax.rsqrt(jnp.mean(xc * xc, axis=-1, keepdims=True) + EPS)
            z = xc * rstd * lg_ref[...] + lb_ref[...]
            y_ref[pl.ds(s * sub, sub), :] = (z * _sigmoid(z)).astype(bf16)
        if na:
            pl.when(i == nt - 1)(finish)

    hs = _halo_specs(tt, DA, T // HALO)
    hbm = pl.BlockSpec(memory_space=pl.ANY)
    return pl.pallas_call(
        body, name="conf_fwd", grid=(nt,),
        in_specs=hs + hs + [_full((CW, DA)), _full((1, DA)), _full((1, DA)), _full((1, DA))] + [hbm] * na,
        out_specs=[pl.BlockSpec((tt, DA), lambda i: (i, 0)), pl.BlockSpec((tt, DA), lambda i: (i, 0))] + [hbm] * na,
        out_shape=[jax.ShapeDtypeStruct((T, DA), bf16), jax.ShapeDtypeStruct((T, DA), f32)]
                  + [jax.ShapeDtypeStruct((NDEV * s.shape[0], s.shape[1]), s.dtype) for s in shards],
        scratch_shapes=[pltpu.VMEM((8, wn, DA), f32)] + (_ag2_scratch(na) if na else []),
        compiler_params=_cp("arbitrary"),
    )(a, a, a, g, g, g, conv_w, conv_b, ln_g, ln_b, *shards)


def _out_proj(xa, y_na, y_cv, w_out, modv, g2):
    tm = 512

    def body(x_ref, ya_ref, yc_ref, w_ref, mod_ref, g_ref, x1_ref, pj_ref, h2_ref):
        proj = _dot(ya_ref[...], w_ref[0:DA, :]) + _dot(yc_ref[...], w_ref[DA:D, :])
        x1 = x_ref[...] + mod_ref[2:3, :] * proj
        x1_ref[...] = x1
        pj_ref[...] = proj.astype(bf16)
        rstd = lax.rsqrt(jnp.mean(x1 * x1, axis=-1, keepdims=True) + EPS)
        h2_ref[...] = ((x1 * rstd * g_ref[...]) * (1.0 + mod_ref[4:5, :]) + mod_ref[3:4, :]).astype(bf16)

    row = lambda w: pl.BlockSpec((tm, w), lambda i: (i, 0))
    return pl.pallas_call(
        body, name="out_proj", grid=(T // tm,),
        in_specs=[row(D), row(DA), row(DA), _full((D, D)), _full((8, D)), _full((1, D))],
        out_specs=[row(D), row(D), row(D)],
        out_shape=[jax.ShapeDtypeStruct((T, D), f32), jax.ShapeDtypeStruct((T, D), bf16), jax.ShapeDtypeStruct((T, D), bf16)],
        compiler_params=_cp("parallel"),
    )(xa, y_na, y_cv, w_out, modv, g2)


FFN_TT = 2048
FFN_CT = 256
FFN_NC = F // FFN_CT
FFN_SUB = 32


def _row_neighbours(ref, r, n):
    blk = ref[pl.ds(r - 8, n + 16), :]
    return blk[8:8 + n, :], pltpu.roll(blk, 1, 0)[8:8 + n, :], pltpu.roll(blk, n + 15, 0)[8:8 + n, :]


def _ffn_specs(tt, ct, by_token_first):
    tc = (lambda f: (lambda t, c: f(t, c))) if by_token_first else (lambda f: (lambda c, t: f(t, c)))
    per = tt // HALO
    halo = [pl.BlockSpec((HALO, D), tc(lambda t, c: (jnp.maximum(t * per - 1, 0), 0))),
            pl.BlockSpec((tt, D), tc(lambda t, c: (t, 0))),
            pl.BlockSpec((HALO, D), tc(lambda t, c: (jnp.minimum((t + 1) * per, T // HALO - 1), 0)))]
    weights = [pl.BlockSpec((ct, D), tc(lambda t, c: (c, 0))), pl.BlockSpec((ct, D), tc(lambda t, c: (c + FFN_NC, 0))),
               pl.BlockSpec((3, ct), tc(lambda t, c: (0, c))), pl.BlockSpec((3, ct), tc(lambda t, c: (0, c + FFN_NC))),
               pl.BlockSpec((1, ct), tc(lambda t, c: (0, c))), pl.BlockSpec((1, ct), tc(lambda t, c: (0, c + FFN_NC))),
               pl.BlockSpec((ct, D), tc(lambda t, c: (c, 0)))]
    return halo, weights


def _ffn_fwd(h2, w_upT, fcw, fcb, w_down):
    tt, ct = FFN_TT, FFN_CT
    nt = T // tt
    wn = tt + 2 * HALO
    half = tt // 2

    def body(hp, hc, hn, wg_ref, wv_ref, cwg_ref, cwv_ref, cbg_ref, cbv_ref, wd_ref, o_ref, u_ref, u2_ref, hwin, uwin, act):
        t = pl.program_id(0)
        c = pl.program_id(1)

        @pl.when(c == 0)
        def _():
            hwin[0:HALO, :] = jnp.where(t > 0, hp[...], jnp.zeros_like(hp[...]))
            hwin[HALO:HALO + tt, :] = hc[...]
            hwin[HALO + tt:, :] = jnp.where(t < nt - 1, hn[...], jnp.zeros_like(hn[...]))
            o_ref[...] = jnp.zeros_like(o_ref)

        for r0, r1 in ((0, half + 2 * HALO), (half + 2 * HALO, wn)):
            hw = hwin[r0:r1, :]
            uwin[r0:r1, :ct] = _dg(hw, wg_ref[...], NT)
            uwin[r0:r1, ct:] = _dg(hw, wv_ref[...], NT)
        cw = jnp.concatenate([cwg_ref[...], cwv_ref[...]], axis=1)
        cb = jnp.concatenate([cbg_ref[...], cbv_ref[...]], axis=1)
        for p in range(2):
            for r in range(p * half, (p + 1) * half, FFN_SUB):
                uc, prev, nxt = _row_neighbours(uwin, HALO + r, FFN_SUB)
                u2 = prev * cw[0:1, :] + uc * cw[1:2, :] + nxt * cw[2:3, :] + cb
                u_ref[r:r + FFN_SUB, :] = uc.astype(bf16)
                u2_ref[r:r + FFN_SUB, :] = u2
                gate = u2[:, :ct]
                act[r:r + FFN_SUB, :] = (gate * _sigmoid(gate) * u2[:, ct:]).astype(bf16)
            rows = slice(p * half, (p + 1) * half)
            o_ref[rows, :] += _dot(act[rows, :], wd_ref[...])

    halo, weights = _ffn_specs(tt, ct, True)
    pair = pl.BlockSpec((tt, 2 * ct), lambda t, c: (t, c))
    return pl.pallas_call(
        body, name="ffn_fwd", grid=(nt, FFN_NC),
        in_specs=halo + weights,
        out_specs=[pl.BlockSpec((tt, D), lambda t, c: (t, 0)), pair, pair],
        out_shape=[jax.ShapeDtypeStruct((T, D), f32), jax.ShapeDtypeStruct((T, F2), bf16), jax.ShapeDtypeStruct((T, F2), f32)],
        scratch_shapes=[pltpu.VMEM((wn, D), bf16), pltpu.VMEM((wn, 2 * ct), f32), pltpu.VMEM((tt, ct), bf16)],
        compiler_params=_cp("parallel", "arbitrary"),
    )(h2, h2, h2, w_upT, w_upT, fcw, fcw, fcb, fcb, w_down)


def _loss_bwd(ffn, x1, tgt, modv, gf):
    tm = 1024
    nt = T // tm

    def body(f_ref, x1_ref, t_ref, mod_ref, g_ref, dx2_ref, df_ref, s_ref):
        i = pl.program_id(0)

        @pl.when(i == 0)
        def _():
            s_ref[...] = jnp.zeros_like(s_ref)

        ff = f_ref[...]
        gt2 = mod_ref[5:6, :]
        x2 = x1_ref[...] + gt2 * ff
        rstd = lax.rsqrt(jnp.mean(x2 * x2, axis=-1, keepdims=True) + EPS)
        xh = x2 * rstd
        gfv = g_ref[...]
        e = xh * gfv - t_ref[...]
        dy = e * (1.0 / D)
        dxh = dy * gfv
        dx2 = rstd * (dxh - xh * jnp.mean(dxh * xh, axis=-1, keepdims=True))
        dx2_ref[...] = dx2
        df_ref[...] = (dx2 * gt2).astype(bf16)
        s_ref[0:1, :] += jnp.sum(dy * xh, axis=0, keepdims=True)
        s_ref[1:2, :] += jnp.sum(dx2 * ff, axis=0, keepdims=True)
        s_ref[2:3, :] += jnp.sum(e * e, axis=0, keepdims=True)

        @pl.when(i == nt - 1)
        def _():
            tot = jnp.sum(s_ref[2:3, :], axis=-1, keepdims=True) * (0.5 / D)
            s_ref[3:4, :] = jnp.broadcast_to(tot, (1, D))

    row = lambda: pl.BlockSpec((tm, D), lambda i: (i, 0))
    return pl.pallas_call(
        body, name="loss_bwd", grid=(nt,),
        in_specs=[row(), row(), row(), _full((8, D)), _full((1, D))],
        out_specs=[row(), row(), _full((8, D))],
        out_shape=[jax.ShapeDtypeStruct((T, D), f32), jax.ShapeDtypeStruct((T, D), bf16), jax.ShapeDtypeStruct((8, D), f32)],
        compiler_params=_cp("arbitrary"),
    )(ffn, x1, tgt, modv, gf)


def _ffn_bwd(h2, dffn, u_t, u2_t, fcw, w_down):
    tt, ct = FFN_TT, FFN_CT
    nt = T // tt
    wn = tt + 2 * HALO
    half = tt // 2

    def body(dp, dc, dn, hc, uc_ref, u2p, u2c, u2n, cwg_ref, cwv_ref, wd_ref,
             dug_ref, duv_ref, dwu_ref, dwd_ref, dcwg_ref, dcwv_ref, dcbg_ref, dcbv_ref,
             dwin, d2win, dawin, accu, accd, act, du):
        t = pl.program_id(1)
        first, last = t == 0, t == nt - 1
        zero = jnp.zeros((HALO, D), bf16)
        dwin[0:HALO, :] = jnp.where(first, zero, dp[...])
        dwin[HALO:HALO + tt, :] = dc[...]
        dwin[HALO + tt:, :] = jnp.where(last, zero, dn[...])

        @pl.when(first)
        def _():
            for r in (accu, accd, dcwg_ref, dcwv_ref, dcbg_ref, dcbv_ref):
                r[...] = jnp.zeros_like(r)

        cw = jnp.concatenate([cwg_ref[...], cwv_ref[...]], axis=1)
        split = half + 2 * HALO
        for r0, r1 in ((0, split), (split, wn)):
            dawin[r0:r1, :] = _dg(dwin[r0:r1, :], wd_ref[...], NT)

        def grads(u2v, dact):
            gate, val = u2v[:, :ct], u2v[:, ct:]
            sg = _sigmoid(gate)
            silu = gate * sg
            return dact * val * (sg * (1.0 + gate * (1.0 - sg))), dact * silu, silu * val

        for blk, r0 in ((u2p, 0), (u2n, HALO + tt)):
            dgate, dval, _ = grads(blk[...], dawin[r0:r0 + HALO, :])
            d2win[r0:r0 + HALO, :ct] = dgate
            d2win[r0:r0 + HALO, ct:] = dval
        for p in range(2):
            rows = slice(p * half, (p + 1) * half)
            for r in range(p * half, (p + 1) * half, FFN_SUB):
                dgate, dval, av = grads(u2c[r:r + FFN_SUB, :], dawin[HALO + r:HALO + r + FFN_SUB, :])
                d2win[HALO + r:HALO + r + FFN_SUB, :ct] = dgate
                d2win[HALO + r:HALO + r + FFN_SUB, ct:] = dval
                act[r:r + FFN_SUB, :] = av.astype(bf16)
            accd[...] += _dg(act[rows, :], dc[rows, :], TN)

        def fold8(x):
            out = x[0:8]
            for k in range(8, FFN_SUB, 8):
                out = out + x[k:k + 8]
            return out

        for p in range(2):
            rows = slice(p * half, (p + 1) * half)
            sums = [jnp.zeros((8, 2 * ct), f32) for _ in range(4)]
            for r in range(p * half, (p + 1) * half, FFN_SUB):
                d2c, d2m, d2p = _row_neighbours(d2win, HALO + r, FFN_SUB)
                ucur = uc_ref[r:r + FFN_SUB, :].astype(f32)
                sums[0] = sums[0] + fold8(d2c)
                for kk, dd in enumerate((d2p, d2c, d2m)):
                    sums[1 + kk] = sums[1 + kk] + fold8(ucur * dd)
                du[r:r + FFN_SUB, :] = (d2p * cw[0:1, :] + d2c * cw[1:2, :] + d2m * cw[2:3, :]).astype(bf16)
            dcb = jnp.sum(sums[0], axis=0, keepdims=True)
            dcbg_ref[...] += dcb[:, :ct]
            dcbv_ref[...] += dcb[:, ct:]
            for kk in range(3):
                dck = jnp.sum(sums[1 + kk], axis=0, keepdims=True)
                dcwg_ref[kk:kk + 1, :] += dck[:, :ct]
                dcwv_ref[kk:kk + 1, :] += dck[:, ct:]
            dug_ref[rows, :] = du[rows, :ct]
            duv_ref[rows, :] = du[rows, ct:]
            accu[...] += _dg(du[rows, :], hc[rows, :], TN)

        @pl.when(last)
        def _():
            dwu_ref[0] = accu[0:ct, :].astype(bf16)
            dwu_ref[1] = accu[ct:, :].astype(bf16)
            dwd_ref[...] = accd[...].astype(bf16)

    per = tt // HALO
    prev = lambda w: (lambda c, t: (jnp.maximum(t * per - 1, 0), c if w else 0))
    nxt = lambda w: (lambda c, t: (jnp.minimum((t + 1) * per, T // HALO - 1), c if w else 0))
    tile = lambda: pl.BlockSpec((ct, D), lambda c, t: (c, 0))
    lane = lambda r, off: pl.BlockSpec((r, ct), lambda c, t: (0, c + off))
    return pl.pallas_call(
        body, name="ffn_bwd", grid=(FFN_NC, nt),
        in_specs=[pl.BlockSpec((HALO, D), prev(False)), pl.BlockSpec((tt, D), lambda c, t: (t, 0)), pl.BlockSpec((HALO, D), nxt(False)),
                  pl.BlockSpec((tt, D), lambda c, t: (t, 0)), pl.BlockSpec((tt, 2 * ct), lambda c, t: (t, c)),
                  pl.BlockSpec((HALO, 2 * ct), prev(True)), pl.BlockSpec((tt, 2 * ct), lambda c, t: (t, c)),
                  pl.BlockSpec((HALO, 2 * ct), nxt(True)), lane(3, 0), lane(3, FFN_NC), tile()],
        out_specs=[pl.BlockSpec((tt, ct), lambda c, t: (t, c)), pl.BlockSpec((tt, ct), lambda c, t: (t, c)),
                   pl.BlockSpec((2, ct, D), lambda c, t: (0, c, 0)), tile(), lane(3, 0), lane(3, 0), lane(1, 0), lane(1, 0)],
        out_shape=[jax.ShapeDtypeStruct((T, F), bf16), jax.ShapeDtypeStruct((T, F), bf16),
                   jax.ShapeDtypeStruct((2, F, D), bf16), jax.ShapeDtypeStruct((F, D), bf16),
                   jax.ShapeDtypeStruct((3, F), f32), jax.ShapeDtypeStruct((3, F), f32),
                   jax.ShapeDtypeStruct((1, F), f32), jax.ShapeDtypeStruct((1, F), f32)],
        scratch_shapes=[pltpu.VMEM((wn, D), bf16), pltpu.VMEM((wn, 2 * ct), f32), pltpu.VMEM((wn, ct), f32),
                        pltpu.VMEM((2 * ct, D), f32), pltpu.VMEM((ct, D), f32),
                        pltpu.VMEM((tt, ct), bf16), pltpu.VMEM((tt, 2 * ct), bf16)],
        compiler_params=_cp("parallel", "arbitrary"),
    )(dffn, dffn, dffn, h2, u_t, u2_t, u2_t, u2_t, fcw, fcw, w_down)


def _norm_bwd(dh, xv, gain, sh_sc, rstd):
    xh = xv * rstd
    n = xh * gain
    dn = dh * (1.0 + sh_sc)
    dxh = dn * gain
    dx = rstd * (dxh - xh * jnp.mean(dxh * xh, axis=-1, keepdims=True))
    return (dx, jnp.sum(dh, axis=0, keepdims=True), jnp.sum(dh * n, axis=0, keepdims=True),
            jnp.sum(dn * xh, axis=0, keepdims=True))


def _norm2_bwd(dug, duv, w_upT, x1, dx2, proj, w_out, y_na, y_cv, modv, g2, blocks):
    tm = 512
    nt = T // tm
    nb = len(blocks)

    def body(dug_ref, duv_ref, w_ref, x1_ref, dx2_ref, pj_ref, wo_ref, ya_ref, yc_ref, mod_ref, g_ref, *rest):
        g_refs, (dx1_ref, dya_ref, dyc_ref, dwo_ref, s_ref) = rest[:nb], rest[nb:nb + 5]
        recv_refs, acc, a2a_sems = rest[nb + 5:2 * nb + 5], rest[2 * nb + 5], rest[2 * nb + 6:]
        i = pl.program_id(0)
        plans = [_a2a_plan(g_refs[k], recv_refs[k], *a2a_sems[3 * k:3 * k + 3]) for k in range(nb)]
        for start, _ in plans:
            pl.when(i == 0)(start)

        @pl.when(i == 0)
        def _():
            s_ref[...] = jnp.zeros_like(s_ref)
            acc[...] = jnp.zeros_like(acc)

        dh2 = _dot(dug_ref[...], w_ref[0:F, :]) + _dot(duv_ref[...], w_ref[F:F2, :])
        x1 = x1_ref[...]
        rstd = lax.rsqrt(jnp.mean(x1 * x1, axis=-1, keepdims=True) + EPS)
        dxn, dsh, dsc, dgn = _norm_bwd(dh2, x1, g_ref[...], mod_ref[4:5, :], rstd)
        dx1 = dx2_ref[...] + dxn
        dx1_ref[...] = dx1
        dpj = (dx1 * mod_ref[2:3, :]).astype(bf16)
        dyc = _dg(dpj, wo_ref[...], NT)
        dya_ref[...] = dyc[:, :DA].astype(bf16)
        dyc_ref[...] = dyc[:, DA:]
        acc[0:DA, :] += _dg(ya_ref[...], dpj, TN)
        acc[DA:D, :] += _dg(yc_ref[...], dpj, TN)

        @pl.when(i == nt - 1)
        def _():
            dwo_ref[...] = acc[...].astype(bf16)

        s_ref[0:1, :] += dsh
        s_ref[1:2, :] += dsc
        s_ref[2:3, :] += dgn
        s_ref[3:4, :] += jnp.sum(dx1 * pj_ref[...].astype(f32), axis=0, keepdims=True)
        for _, finish in plans:
            pl.when(i == nt - 1)(finish)

    row = lambda w: pl.BlockSpec((tm, w), lambda i: (i, 0))
    hbm = pl.BlockSpec(memory_space=pl.ANY)
    return pl.pallas_call(
        body, name="norm2_bwd", grid=(nt,),
        in_specs=[row(F), row(F), _resident((F2, D)), row(D), row(D), row(D), _resident((D, D)), row(DA), row(DA),
                  _full((8, D)), _full((1, D))] + [hbm] * nb,
        out_specs=[row(D), row(DA), row(DA), _full((D, D)), _full((8, D))] + [hbm] * nb,
        out_shape=[jax.ShapeDtypeStruct((T, D), f32), jax.ShapeDtypeStruct((T, DA), bf16), jax.ShapeDtypeStruct((T, DA), f32),
                   jax.ShapeDtypeStruct((D, D), bf16), jax.ShapeDtypeStruct((8, D), f32)]
                  + [jax.ShapeDtypeStruct(b.shape, b.dtype) for b in blocks],
        scratch_shapes=[pltpu.VMEM((D, D), f32)] + _a2a_scratch() * nb,
        compiler_params=_cp("arbitrary"),
    )(dug, duv, w_upT, x1, dx2, proj, w_out, y_na, y_cv, modv, g2, *blocks)


def _conf_bwd(a, g, cv, dy, conv_w, ln_g, ln_b, blocks):
    tt = CONV_TT
    nt = T // tt
    sub = 32
    wn = tt + 2 * HALO
    nb = len(blocks)

    def body(ap, ac, an, gp, gc, gn, cp_, cc, cn, dp, dc, dn, w_ref, lg_ref, lb_ref, *rest):
        g_refs, (da_ref, dg_ref, dcw_ref, s_ref) = rest[:nb], rest[nb:nb + 4]
        recv_refs, (urot, drot, wacc), a2a_sems = rest[nb + 4:2 * nb + 4], rest[2 * nb + 4:2 * nb + 7], rest[2 * nb + 7:]
        i = pl.program_id(0)
        first, last = i == 0, i == nt - 1
        plans = [_a2a_plan(g_refs[k], recv_refs[k], *a2a_sems[3 * k:3 * k + 3]) for k in range(nb)]
        for start, _ in plans:
            pl.when(first)(start)

        @pl.when(first)
        def _():
            s_ref[...] = jnp.zeros_like(s_ref)
            wacc[...] = jnp.zeros_like(wacc)

        lg, lb = lg_ref[...], lb_ref[...]

        def ln_bwd(cvv, dyv):
            mu = jnp.mean(cvv, axis=-1, keepdims=True)
            xc = cvv - mu
            rstd = lax.rsqrt(jnp.mean(xc * xc, axis=-1, keepdims=True) + EPS)
            yn = xc * rstd
            z = yn * lg + lb
            sz = _sigmoid(z)
            dz = dyv * (sz * (1.0 + z * (1.0 - sz)))
            dyn = dz * lg
            dcv = rstd * (dyn - jnp.mean(dyn, axis=-1, keepdims=True) - yn * jnp.mean(dyn * yn, axis=-1, keepdims=True))
            return dcv, dz, yn

        urot[0, 0:HALO, :] = jnp.where(first, 0.0, ap[...] * _sigmoid(gp[...]))
        urot[0, HALO + tt:, :] = jnp.where(last, 0.0, an[...] * _sigmoid(gn[...]))
        drot[0, 0:HALO, :] = jnp.where(first, 0.0, ln_bwd(cp_[...], dp[...])[0])
        drot[0, HALO + tt:, :] = jnp.where(last, 0.0, ln_bwd(cn[...], dn[...])[0])
        for s in range(tt // sub):
            rr = pl.ds(s * sub, sub)
            urot[0, pl.ds(HALO + s * sub, sub), :] = ac[rr, :] * _sigmoid(gc[rr, :])
            dcv, dz, yn = ln_bwd(cc[rr, :], dc[rr, :])
            drot[0, pl.ds(HALO + s * sub, sub), :] = dcv
            s_ref[0:1, :] += jnp.sum(dcv, axis=0, keepdims=True)
            s_ref[1:2, :] += jnp.sum(dz * yn, axis=0, keepdims=True)
            s_ref[2:3, :] += jnp.sum(dz, axis=0, keepdims=True)
        _shifted_copies(urot, wn)
        _shifted_copies(drot, wn)
        w = w_ref[...]
        for s in range(tt // sub):
            rr = pl.ds(s * sub, sub)
            dcv = drot[0, pl.ds(HALO + s * sub, sub), :]
            acc = jnp.zeros((sub, DA), f32)
            for j in range(CW):
                ad, bd = divmod(2 * HALO - 1 - j, 8)
                au, bu = divmod(1 + j, 8)
                acc = acc + drot[bd, pl.ds(s * sub + 8 * ad, sub), :] * w[j:j + 1, :]
                part = urot[bu, pl.ds(s * sub + 8 * au, sub), :] * dcv
                wacc[j] += part[0:8] + part[8:16] + part[16:24] + part[24:32]
            av, gv = ac[rr, :], gc[rr, :]
            sg = _sigmoid(gv)
            da_ref[rr, :] = (acc * sg).astype(bf16)
            dg_ref[rr, :] = (acc * av * sg * (1.0 - sg)).astype(bf16)

        @pl.when(last)
        def _():
            for j in range(CW):
                dcw_ref[j:j + 1, :] = jnp.sum(wacc[j], axis=0, keepdims=True)
            dcw_ref[CW:CW + 1, :] = jnp.zeros((1, DA), f32)

        for _, finish in plans:
            pl.when(last)(finish)

    hs = _halo_specs(tt, DA, T // HALO)
    hbm = pl.BlockSpec(memory_space=pl.ANY)
    return pl.pallas_call(
        body, name="conf_bwd", grid=(nt,),
        in_specs=hs * 4 + [_full((CW, DA)), _full((1, DA)), _full((1, DA))] + [hbm] * nb,
        out_specs=[pl.BlockSpec((tt, DA), lambda i: (i, 0)), pl.BlockSpec((tt, DA), lambda i: (i, 0)),
                   _full((CW + 1, DA)), _full((8, DA))] + [hbm] * nb,
        out_shape=[jax.ShapeDtypeStruct((T, DA), bf16), jax.ShapeDtypeStruct((T, DA), bf16),
                   jax.ShapeDtypeStruct((CW + 1, DA), f32), jax.ShapeDtypeStruct((8, DA), f32)]
                  + [jax.ShapeDtypeStruct(b.shape, b.dtype) for b in blocks],
        scratch_shapes=[pltpu.VMEM((8, wn, DA), f32), pltpu.VMEM((8, wn, DA), f32), pltpu.VMEM((CW, 8, DA), f32)]
                       + _a2a_scratch() * nb,
        compiler_params=_cp("arbitrary"),
    )(a, a, a, g, g, g, cv, cv, cv, dy, dy, dy, conv_w, ln_g, ln_b, *blocks)


def _attn_bwd(q, k, v, y, dy, lse, bias_tab, blocks):
    zr = 256
    nb = len(blocks)
    rps = ATTN_RPS
    nsteps = GW // rps

    def body(q_ref, k_ref, v_ref, y_ref, dy_ref, lse_ref, *rest):
        b_refs, rest = rest[:rps], rest[rps:]
        g_refs, (dq_ref, dk_hbm, dv_hbm, db_hbm) = rest[:nb], rest[nb:nb + 4]
        recv_refs, (dk_s, dv_s, db_s, sem), a2a_sems = rest[nb + 4:2 * nb + 4], rest[2 * nb + 4:2 * nb + 8], rest[2 * nb + 8:]
        i = pl.program_id(0)
        plans = [_a2a_plan(g_refs[j], recv_refs[j], *a2a_sems[3 * j:3 * j + 3]) for j in range(nb)]
        for start, _ in plans:
            pl.when(i == 0)(start)

        @pl.when(i == 0)
        def _():
            def z(j, _):
                rr = pl.ds(pl.multiple_of(j * zr, zr), zr)
                dk_s[rr, :] = jnp.zeros((zr, DA), f32)
                dv_s[rr, :] = jnp.zeros((zr, DA), f32)
                return 0
            lax.fori_loop(0, TA // zr, z, 0)
            for p in range(8):
                db_s[p] = jnp.zeros((NH, GW, WR * GW), f32)

        lo = lax.broadcasted_iota(jnp.int32, (GW, 2 * HD), 1) < HD
        for rr in range(rps):
            r = i * rps + rr
            rows = slice(rr * GW, (rr + 1) * GW)
            ks = pl.multiple_of(_win_start(r) * GW, GW)
            pat = _pattern(r)
            win = pl.ds(ks, WR * GW)
            qq, yy, dyy, lse_v = q_ref[rows, :], y_ref[rows, :], dy_ref[rows, :], lse_ref[rows, :]
            ops, pairs = [], []
            for pr in range(NH // 2):
                ps = slice(pr * 2 * HD, (pr + 1) * 2 * HD)
                q2, do2 = qq[:, ps], dyy[:, ps]
                prod = do2.astype(f32) * yy[:, ps].astype(f32)
                delta = jnp.concatenate([jnp.sum(jnp.where(lo, prod, 0.0), axis=-1, keepdims=True),
                                         jnp.sum(jnp.where(lo, 0.0, prod), axis=-1, keepdims=True)], axis=0)
                qst, dost = _stack_pair(q2, lo), _stack_pair(do2, lo)
                kw, vw = k_ref[win, ps], v_ref[win, ps]
                kc, vc = k_ref[T:TA, ps], v_ref[T:TA, ps]
                ops.append((kw, kc))
                pairs.append((qst, dost, delta, _dg(qst, kw, NT), _dg(qst, kc, NT), _dg(dost, vw, NT), _dg(dost, vc, NT)))
            grads = []
            for pr, (qst, dost, delta, sl, sc, dpl, dpc) in enumerate(pairs):
                lh = jnp.concatenate([lse_v[:, 2 * pr:2 * pr + 1], lse_v[:, 2 * pr + 1:2 * pr + 2]], axis=0)
                bias2 = b_refs[rr][0, 2 * pr:2 * pr + 2].reshape(2 * GW, WR * GW)
                pl_ = jnp.exp(sl * SCALE + bias2 - lh)
                pc = jnp.exp(sc * SCALE - lh)
                dsl = pl_ * (dpl - delta)
                dsc = pc * (dpc - delta)
                db_s[pat, 2 * pr:2 * pr + 2] += dsl.reshape(2, GW, WR * GW)
                grads.append((qst, dost, pl_.astype(bf16), pc.astype(bf16), dsl.astype(bf16), dsc.astype(bf16)))
            for pr in range(NH // 2):
                ps = slice(pr * 2 * HD, (pr + 1) * 2 * HD)
                kw, kc = ops[pr]
                qst, dost, plb, pcb, dslb, dscb = grads[pr]
                dqst = _dot(dslb, kw) + _dot(dscb, kc)
                dq_ref[rows, ps] = (jnp.where(lo, dqst[0:GW], dqst[GW:]) * SCALE).astype(bf16)
                dk_s[win, ps] += _dg(dslb, qst, TN) * SCALE
                dv_s[win, ps] += _dg(plb, dost, TN)
                dk_s[T:TA, ps] += _dg(dscb, qst, TN) * SCALE
                dv_s[T:TA, ps] += _dg(pcb, dost, TN)

        @pl.when(i == nsteps - 1)
        def _():
            cps = [pltpu.make_async_copy(src, dst, sem.at[j])
                   for j, (src, dst) in enumerate(((dk_s, dk_hbm), (dv_s, dv_hbm), (db_s, db_hbm)))]
            for cp in cps:
                cp.start()
            for cp in cps:
                cp.wait()

        for _, finish in plans:
            pl.when(i == nsteps - 1)(finish)

    rowq = lambda w: pl.BlockSpec((rps * GW, w), lambda i: (i, 0))
    hbm = pl.BlockSpec(memory_space=pl.ANY)
    bias = [pl.BlockSpec((1, NH, GW, WR * GW), lambda i, rr=rr: (_pattern(i * rps + rr), 0, 0, 0)) for rr in range(rps)]
    return pl.pallas_call(
        body, name="attn_bwd", grid=(nsteps,),
        in_specs=[rowq(DA), _full((TA, DA)), _full((TA, DA)), rowq(DA), rowq(DA), rowq(NH)] + bias + [hbm] * nb,
        out_specs=[rowq(DA), hbm, hbm, hbm] + [hbm] * nb,
        out_shape=[jax.ShapeDtypeStruct((T, DA), bf16), jax.ShapeDtypeStruct((TA, DA), f32), jax.ShapeDtypeStruct((TA, DA), f32),
                   jax.ShapeDtypeStruct((8, NH, GW, WR * GW), f32)] + [jax.ShapeDtypeStruct(b.shape, b.dtype) for b in blocks],
        scratch_shapes=[pltpu.VMEM((TA, DA), f32), pltpu.VMEM((TA, DA), f32), pltpu.VMEM((8, NH, GW, WR * GW), f32),
                        pltpu.SemaphoreType.DMA((3,))] + _a2a_scratch() * nb,
        compiler_params=_cp("arbitrary"),
    )(q, k, v, y, dy, lse, *([bias_tab] * rps), *blocks)


def _rpb_reduce(dbias):
    rev = np.eye(GW, dtype=np.float32)[::-1]

    def body(d_ref, rev_ref, o_ref):
        rv = rev_ref[...]
        for h in range(NH):
            dv = d_ref[0, h]
            r0 = dv.astype(bf16)
            e1 = dv - r0.astype(f32)
            r1 = e1.astype(bf16)
            r2 = (e1 - r1.astype(f32)).astype(bf16)
            rr = _dot(rv, r0) + _dot(rv, r1) + _dot(rv, r2)
            skew = pltpu.roll(rr, 0, 1, stride=1, stride_axis=0)
            o_ref[0, h:h + 1, :] = jnp.sum(skew, axis=0, keepdims=True)

    return pl.pallas_call(
        body, name="rpb_reduce", grid=(8,),
        in_specs=[pl.BlockSpec((1, NH, GW, WR * GW), lambda p: (p, 0, 0, 0)), _full((GW, GW))],
        out_specs=pl.BlockSpec((1, NH, WR * GW), lambda p: (p, 0, 0)),
        out_shape=jax.ShapeDtypeStruct((8, NH, WR * GW), f32),
        compiler_params=_cp("parallel"),
    )(dbias, jnp.asarray(rev, dtype=bf16))


def _norm1_bwd(dq, dk, dv, da, dg, w_inT, x0, ctx0, h, dx1, modv, g1):
    tm = 256
    nt = TA // tm
    nx = T // tm

    def body(dq_ref, dk_ref, dv_ref, da_ref, dg_ref, w_ref, x_ref, c_ref, h_ref, dx1_ref, mod_ref, g_ref,
             dx_ref, dwo_ref, s_ref, dw_ref):
        i = pl.program_id(0)
        is_ctx = i == nt - 1

        @pl.when(i == 0)
        def _():
            s_ref[...] = jnp.zeros_like(s_ref)
            dw_ref[...] = jnp.zeros_like(dw_ref)

        hb = h_ref[...]
        dkb, dvb = dk_ref[...].astype(bf16), dv_ref[...].astype(bf16)
        dw_ref[DA:2 * DA, :] += _dg(dkb, hb, TN)
        dw_ref[2 * DA:3 * DA, :] += _dg(dvb, hb, TN)
        dh_kv = _dot(dkb, w_ref[DA:2 * DA, :]) + _dot(dvb, w_ref[2 * DA:3 * DA, :])
        gain = g_ref[...]

        @pl.when(is_ctx)
        def _():
            xv = c_ref[...]
            rstd = lax.rsqrt(jnp.mean(xv * xv, axis=-1, keepdims=True) + EPS)
            _, dsh, dsc, dgn = _norm_bwd(dh_kv, xv, gain, mod_ref[7:8, :], rstd)
            s_ref[2:3, :] += dgn
            s_ref[3:4, :] += dsh
            s_ref[4:5, :] += dsc
            dwo_ref[...] = dw_ref[...].astype(bf16)

        @pl.when(jnp.logical_not(is_ctx))
        def _():
            dqb, dab, dgb = dq_ref[...], da_ref[...], dg_ref[...]
            dw_ref[0:DA, :] += _dg(dqb, hb, TN)
            dw_ref[3 * DA:4 * DA, :] += _dg(dab, hb, TN)
            dw_ref[4 * DA:5 * DA, :] += _dg(dgb, hb, TN)
            dh = (dh_kv + _dot(dqb, w_ref[0:DA, :]) + _dot(dab, w_ref[3 * DA:4 * DA, :])
                  + _dot(dgb, w_ref[4 * DA:5 * DA, :]))
            xv = x_ref[...]
            rstd = lax.rsqrt(jnp.mean(xv * xv, axis=-1, keepdims=True) + EPS)
            dxn, dsh, dsc, dgn = _norm_bwd(dh, xv, gain, mod_ref[1:2, :], rstd)
            dx_ref[...] = dx1_ref[...] + dxn
            s_ref[0:1, :] += dsh
            s_ref[1:2, :] += dsc
            s_ref[2:3, :] += dgn

    row = lambda w: pl.BlockSpec((tm, w), lambda i: (i, 0))
    lrow = lambda w: pl.BlockSpec((tm, w), lambda i: (jnp.minimum(i, nx - 1), 0))
    return pl.pallas_call(
        body, name="norm1_bwd", grid=(nt,),
        in_specs=[lrow(DA), row(DA), row(DA), lrow(DA), lrow(DA), _full((5 * DA, D)), lrow(D), _full((TC, D)), row(D),
                  lrow(D), _full((8, D)), _full((1, D))],
        out_specs=[lrow(D), _full((5 * DA, D)), _full((8, D))],
        out_shape=[jax.ShapeDtypeStruct((T, D), f32), jax.ShapeDtypeStruct((5 * DA, D), bf16), jax.ShapeDtypeStruct((8, D), f32)],
        scratch_shapes=[pltpu.VMEM((5 * DA, D), f32)],
        compiler_params=_cp("arbitrary"),
    )(dq, dk, dv, da, dg, w_inT, x0, ctx0, h, dx1, modv, g1)


def _adam_tile(r, c):
    return max(t for t in range(8, r + 1, 8) if r % t == 0 and t * c * 4 <= 2 * 1024 * 1024)


def _adam_update(wv, gv, mv, vv):
    nm = ADAM_B1 * mv + (1.0 - ADAM_B1) * gv
    nv = ADAM_B2 * vv + (1.0 - ADAM_B2) * (gv * gv)
    m_hat = nm * (1.0 / (1.0 - ADAM_B1 ** ADAM_STEP))
    v_hat = nv * (1.0 / (1.0 - ADAM_B2 ** ADAM_STEP))
    return -ADAM_LR * (m_hat / (jnp.sqrt(v_hat) + ADAM_EPS) + ADAM_WD * wv), nm, nv


def _adamw(w, g, m, v, name):
    r, c = w.shape
    tr = _adam_tile(r, c)

    def body(w_ref, g_ref, m_ref, v_ref, d_ref, nm_ref, nv_ref):
        d_ref[...], nm_ref[...], nv_ref[...] = _adam_update(w_ref[...], g_ref[...], m_ref[...], v_ref[...])

    spec = pl.BlockSpec((tr, c), lambda i: (i, 0))
    return pl.pallas_call(
        body, name=name, grid=(r // tr,),
        in_specs=[spec] * 4, out_specs=[spec] * 3,
        out_shape=[jax.ShapeDtypeStruct((r, c), f32)] * 3,
        compiler_params=_cp("parallel"),
    )(w, g, m, v)


def _adamw_blocks(w, recvs, m, v, name):
    r, c = w.shape
    nk = len(recvs)
    tr = r // nk
    assert all(a.shape == (NDEV, tr, c) for a in recvs)

    def body(w_ref, *rest):
        a_refs, (m_ref, v_ref, g_ref, d_ref, nm_ref, nv_ref) = rest[:nk], rest[nk:]
        for k in range(nk):
            @pl.when(pl.program_id(0) == k)
            def _(a_ref=a_refs[k]):
                gv = a_ref[0].astype(f32)
                for d in range(1, NDEV):
                    gv = gv + a_ref[d].astype(f32)
                g_ref[...] = gv
                d_ref[...], nm_ref[...], nv_ref[...] = _adam_update(w_ref[...], gv, m_ref[...], v_ref[...])

    spec = pl.BlockSpec((tr, c), lambda i: (i, 0))
    return pl.pallas_call(
        body, name=name, grid=(nk,),
        in_specs=[spec] + [_full((NDEV, tr, c))] * nk + [spec, spec], out_specs=[spec] * 4,
        out_shape=[jax.ShapeDtypeStruct((r, c), f32)] * 4,
        compiler_params=_cp("arbitrary"),
    )(w, *recvs, m, v)


def _pad_rows128(vec):
    n = vec.shape[0]
    rows = -(-n // 1024) * 8
    return jnp.pad(vec, (0, rows * 128 - n)).reshape(rows, 128)


def _grad_rpb(dbias):
    lane_map, r_hot = _rpb_tables()
    return jnp.einsum("phl,lic,pir->hrc", _rpb_reduce(dbias), jnp.asarray(lane_map), jnp.asarray(r_hot),
                      precision=lax.Precision.HIGHEST)


def kernel(x, c, ctx, c_ctx, w_mod, b_mod, g_norm1, w_in, rpb, conv_w, conv_b, ln_g, ln_b, w_out, g_norm2, w_up, ffn_conv_w, ffn_conv_b, w_down, g_final, loss_target, m_c_ctx, m_w_mod, m_b_mod, m_g_norm1, m_w_in, m_rpb, m_conv_w, m_conv_b, m_ln_g, m_ln_b, m_w_out, m_g_norm2, m_w_up, m_ffn_conv_w, m_ffn_conv_b, m_w_down, m_g_final, v_c_ctx, v_w_mod, v_b_mod, v_g_norm1, v_w_in, v_rpb, v_conv_w, v_conv_b, v_ln_g, v_ln_b, v_w_out, v_g_norm2, v_w_up, v_ffn_conv_w, v_ffn_conv_b, v_w_down, v_g_final):
    me = 4 * lax.axis_index("x") + 2 * lax.axis_index("y") + lax.axis_index("c")
    nmod = w_mod.shape[2]
    n_in = w_in.shape[2]
    n_out = w_out.shape[1]
    n_up = w_up.shape[2]
    n_dn = w_down.shape[1]
    n_cw = conv_w.shape[2]

    b_sh = lax.dynamic_slice(b_mod, (0, me * nmod), (1, nmod))
    convpay = _pad_rows128(jnp.concatenate([conv_w[0].reshape(-1), ffn_conv_w[0].reshape(-1)]))
    w_inT, c_all, mod_all, flat, bias_tab = _head(w_in[0].T.astype(bf16), c.reshape(8, 128), c_ctx.reshape(8, 128),
                                                  w_mod[0], b_sh, convpay, *_bias_table_inputs(rpb[0]))
    cvec = jnp.concatenate([c_all.reshape(NDEV, D), c_ctx[None, :], jnp.zeros((7, D), f32)], axis=0)
    mod_all = mod_all.transpose(1, 0, 2).reshape(16, 6 * D)
    mod_me = lax.dynamic_index_in_dim(mod_all, me, 0, keepdims=False).reshape(6, D)
    mod_c = mod_all[8]
    modv = jnp.concatenate([mod_me, mod_c[None, 0:D], mod_c[None, D:2 * D]], axis=0)
    flat = flat.reshape(NDEV, -1)
    o1 = CW * n_cw
    conv_w_f = flat[:, :o1].reshape(NDEV, CW, n_cw).transpose(1, 0, 2).reshape(CW, DA)
    fcw_f = flat[:, o1:o1 + 3 * n_up].reshape(NDEV, 3, n_up).transpose(1, 0, 2).reshape(3, F2)

    x0, ctx0 = x[0], ctx[0]
    h, q, k, v, a, g, w_down_f = _in_proj(x0, ctx0, g_norm1, modv, w_inT, [w_down[0].astype(bf16)])
    w_upT_sh = w_up[0].T.astype(bf16)
    h_up = n_up // 2
    y_cv, cv, w_out_f, w_up_a = _conf_fwd(a, g, conv_w_f, conv_b, ln_g, ln_b, [w_out[0].astype(bf16), w_upT_sh[:h_up]])
    y_na, lse, w_up_b = _attn_fwd(q, k, v, bias_tab, [w_upT_sh[h_up:]])
    w_upT = jnp.concatenate([w_up_a.reshape(NDEV, h_up, D), w_up_b.reshape(NDEV, h_up, D)], axis=1).reshape(F2, D)
    x1, proj, h2 = _out_proj(x0, y_na, y_cv, w_out_f, modv, g_norm2)
    ffn, u_t, u2_t = _ffn_fwd(h2, w_upT, fcw_f, ffn_conv_b, w_down_f)
    dx2, dffn, s_loss = _loss_bwd(ffn, x1, loss_target[0], modv, g_final[None, :])

    dug, duv, dw_up, dw_down, dcwg, dcwv, dcbg, dcbv = _ffn_bwd(h2, dffn, u_t, u2_t, fcw_f, w_down_f)
    dw_up = dw_up.reshape(NDEV, n_up, D)
    dx1, dy_na, dy_cv, dw_out, s_n2, rv_up_a = _norm2_bwd(dug, duv, w_upT, x1, dx2, proj, w_out_f, y_na, y_cv, modv, g_norm2,
                                                          [dw_up[:, :h_up]])
    da, dg, dcw, s_cf, rv_down = _conf_bwd(a, g, cv, dy_cv, conv_w_f, ln_g, ln_b, [dw_down.reshape(NDEV, n_dn, D)])
    dq, dk, dv, dbias, rv_up_b = _attn_bwd(q, k, v, y_na, dy_na, lse, bias_tab, [dw_up[:, h_up:]])
    grad_rpb_part = _grad_rpb(dbias)
    grad_x, dw_inT, s_n1 = _norm1_bwd(dq, dk, dv, da, dg, w_inT, x0, ctx0, h, dx1, modv, g_norm1)
    grad_x = grad_x[None]
    dfcw = jnp.concatenate([dcwg, dcwv], axis=1)
    dfcb = jnp.concatenate([dcbg[0], dcbv[0]])
    small = jnp.concatenate([dcw[:CW].reshape(CW, NDEV, n_cw).transpose(1, 0, 2).reshape(NDEV, CW * n_cw),
                             dfcw.reshape(3, NDEV, n_up).transpose(1, 0, 2).reshape(NDEV, 3 * n_up)], axis=1)
    n_sm = small.shape[1] // D
    pad_sm = -n_sm % 16
    small = jnp.pad(small.reshape(NDEV, n_sm, D), ((0, 0), (0, pad_sm), (0, 0))).astype(bf16)
    dmod = jnp.concatenate([s_n1[0], s_n1[1], s_n2[3], s_n2[0], s_n2[1], s_loss[1]])
    dmodc = jnp.concatenate([s_n1[3], s_n1[4]])
    parts = [dmodc, s_n1[2], grad_rpb_part.reshape(-1), s_cf[0], s_cf[1], s_cf[2], s_n2[2], dfcb, s_loss[0], s_loss[3, 0:1]]
    sizes = [p.shape[0] for p in parts]
    pvec = _pad_rows128(jnp.concatenate([dmod] + parts))
    slab = jnp.concatenate([dw_inT.reshape(NDEV, n_in, D), small, dw_out.reshape(NDEV, n_out, D)], axis=1)
    r_a, gath = _reduce_scatter2(slab, [pvec], "rs_w_in")
    g_w_out = r_a[n_in + n_sm + pad_sm:]
    g_w_inT = r_a[:n_in]
    sm = r_a[n_in:n_in + n_sm].reshape(-1)
    g_conv_w = sm[:CW * n_cw].reshape(CW, n_cw)
    g_fcw = sm[CW * n_cw:].reshape(3, n_up)
    tot = _sum_rows8(gath, "sum_small").reshape(-1)
    dmod_all = gath.reshape(NDEV, -1)[:, :6 * D]
    offs = np.cumsum([6 * D] + sizes)
    pick = lambda j: tot[offs[j]:offs[j + 1]]
    dmodc_t = jnp.pad(pick(0), (0, 4 * D))
    g_b_mod = (tot[:6 * D] + dmodc_t)[None, :]
    g_g_norm1 = pick(1)[None, :]
    g_rpb = pick(2).reshape(1, NH, 2 * WR - 1, 2 * NCOL - 1)
    g_conv_b, g_ln_g, g_ln_b = pick(3)[None, :], pick(4)[None, :], pick(5)[None, :]
    g_g_norm2 = pick(6)[None, :]
    g_fcb = pick(7)[None, :]
    g_g_final = pick(8)
    loss = pick(9)[0]
    dm_rows = jnp.concatenate([dmod_all, dmodc_t[None, :], jnp.zeros((7, 6 * D), f32)], axis=0)
    dm_sh = lax.dynamic_slice(dm_rows, (0, me * nmod), (16, nmod))
    g_w_mod, gc_part = _mod_bwd(cvec, dm_sh, w_mod[0])
    gc_sum = _sum_rows8(_small_allgather(gc_part[0].reshape(8, 128), "ag_cctx"), "sum_cctx").reshape(D)
    sg_c = _sigmoid(c_ctx)
    g_c_ctx = gc_sum * (sg_c * (1.0 + c_ctx * (1.0 - sg_c)))

    big = [("w_mod", w_mod[0], g_w_mod, m_w_mod[0], v_w_mod[0]), ("w_in", w_in[0].T, g_w_inT, m_w_in[0].T, v_w_in[0].T),
           ("w_out", w_out[0], g_w_out, m_w_out[0], v_w_out[0])]
    upd = {n: _adamw(wv, gv, mv, vv, "adamw_" + n) for n, wv, gv, mv, vv in big}
    g_w_upT, *upd["w_up"] = _adamw_blocks(w_up[0].T, [rv_up_a, rv_up_b], m_w_up[0].T, v_w_up[0].T, "adamw_w_up")
    g_w_down, *upd["w_down"] = _adamw_blocks(w_down[0], [rv_down], m_w_down[0], v_w_down[0], "adamw_w_down")
    for n in ("w_in", "w_up"):
        upd[n] = tuple(arr.T for arr in upd[n])
    g_w_in, g_w_up = g_w_inT.T, g_w_upT.T
    smalls = [("c_ctx", c_ctx, g_c_ctx, m_c_ctx, v_c_ctx), ("b_mod", b_mod, g_b_mod, m_b_mod, v_b_mod),
              ("g_norm1", g_norm1, g_g_norm1, m_g_norm1, v_g_norm1), ("rpb", rpb, g_rpb, m_rpb, v_rpb),
              ("conv_w", conv_w, g_conv_w[None], m_conv_w, v_conv_w), ("conv_b", conv_b, g_conv_b, m_conv_b, v_conv_b),
              ("ln_g", ln_g, g_ln_g, m_ln_g, v_ln_g), ("ln_b", ln_b, g_ln_b, m_ln_b, v_ln_b),
              ("g_norm2", g_norm2, g_g_norm2, m_g_norm2, v_g_norm2),
              ("ffn_conv_w", ffn_conv_w, g_fcw[None], m_ffn_conv_w, v_ffn_conv_w),
              ("ffn_conv_b", ffn_conv_b, g_fcb, m_ffn_conv_b, v_ffn_conv_b), ("g_final", g_final, g_g_final, m_g_final, v_g_final)]
    packed = [_pad_rows128(jnp.concatenate([t[j].reshape(-1) for t in smalls])) for j in (1, 2, 3, 4)]
    sd, sm_, sv = _adamw(*packed, "adamw_small")
    so = np.cumsum([0] + [int(np.prod(t[1].shape)) for t in smalls])
    for j, t in enumerate(smalls):
        shp = t[1].shape
        upd[t[0]] = tuple(arr.reshape(-1)[so[j]:so[j + 1]].reshape(shp) for arr in (sd, sm_, sv))
    grads = {"c_ctx": g_c_ctx, "w_mod": g_w_mod[None], "b_mod": g_b_mod, "g_norm1": g_g_norm1, "w_in": g_w_in[None],
             "rpb": g_rpb, "conv_w": g_conv_w[None], "conv_b": g_conv_b, "ln_g": g_ln_g, "ln_b": g_ln_b,
             "w_out": g_w_out[None], "g_norm2": g_g_norm2, "w_up": g_w_up[None], "ffn_conv_w": g_fcw[None],
             "ffn_conv_b": g_fcb, "w_down": g_w_down[None], "g_final": g_g_final}
    names = ["c_ctx", "w_mod", "b_mod", "g_norm1", "w_in", "rpb", "conv_w", "conv_b", "ln_g", "ln_b", "w_out", "g_norm2",
             "w_up", "ffn_conv_w", "ffn_conv_b", "w_down", "g_final"]
    shapes = {n: grads[n].shape for n in names}
    outs = [loss, grad_x] + [grads[n] for n in names]
    for j in range(3):
        outs += [upd[n][j].reshape(shapes[n]) for n in names]
    return tuple(outs)
```

```python
import numpy as np
import jax
import jax.numpy as jnp
from jax import lax
from jax.experimental import pallas as pl
from jax.experimental.pallas import tpu as pltpu

f32 = jnp.float32
bf16 = jnp.bfloat16

D = 1024
T = 4096
TC = 256
TA = T + TC
DA = 512
NH = 8
HD = 64
GW = 64
WR = 8
NCOL = 16
F = 2816
F2 = 2 * F
CW = 31
NDEV = 8
EPS = 1e-6
SCALE = HD ** -0.5
NEG = -1e30
MESH = pl.DeviceIdType.MESH

NT = (((1,), (1,)), ((), ()))
TN = (((0,), (0,)), ((), ()))

ADAM_LR, ADAM_B1, ADAM_B2, ADAM_EPS, ADAM_WD, ADAM_STEP = 0.001, 0.9, 0.999, 1e-08, 0.01, 10

VMEM_LIMIT = 56 * 1024 * 1024


def _cp(*sem):
    return pltpu.CompilerParams(dimension_semantics=sem or None, vmem_limit_bytes=VMEM_LIMIT)


def _dot(a, b):
    return jnp.dot(a, b, preferred_element_type=f32)


def _dg(a, b, dims):
    return lax.dot_general(a, b, dims, preferred_element_type=f32)


def _sigmoid(x):
    return 1.0 / (1.0 + jnp.exp(-x))


def _full(shape):
    n = len(shape)
    return pl.BlockSpec(shape, lambda *_: (0,) * n)


def _resident(shape):
    n = len(shape)
    return pl.BlockSpec(shape, lambda *_: (0,) * n, pipeline_mode=pl.Buffered(1))


def _my_pos():
    return lax.axis_index("x"), lax.axis_index("y"), lax.axis_index("c")


def _small_gather_plan(v_ref, out_ref, send_sems, recv_sems):
    x, y, c = _my_pos()
    me = 4 * x + 2 * y + c
    peers = []
    for k in range(1, NDEV):
        kx, ky, kc = (k >> 2) & 1, (k >> 1) & 1, k & 1
        peers.append((x ^ kx, y ^ ky, c ^ kc))

    def copy(k, slot, to):
        return pltpu.make_async_remote_copy(
            src_ref=v_ref, dst_ref=out_ref.at[slot], send_sem=send_sems.at[k], recv_sem=recv_sems.at[k],
            device_id=to, device_id_type=MESH)

    def start():
        out_ref[me] = v_ref[...]
        for k, p in enumerate(peers):
            copy(k, me, p).start()

    def finish():
        for k, (px, py, pc) in enumerate(peers):
            copy(k, 4 * px + 2 * py + pc, (x, y, c)).wait_recv()
        for k, p in enumerate(peers):
            copy(k, me, p).wait_send()

    return start, finish


def _small_gather_scratch():
    return [pltpu.SemaphoreType.DMA((NDEV - 1,)), pltpu.SemaphoreType.DMA((NDEV - 1,))]


def _small_allgather(v, name):
    n = v.shape[0]

    def body(v_ref, out_ref, send_sems, recv_sems):
        start, finish = _small_gather_plan(v_ref, out_ref, send_sems, recv_sems)
        start()
        finish()

    return pl.pallas_call(
        body, name=name,
        out_shape=jax.ShapeDtypeStruct((NDEV, n, 128), f32),
        in_specs=[pl.BlockSpec(memory_space=pltpu.VMEM)],
        out_specs=pl.BlockSpec(memory_space=pltpu.VMEM),
        scratch_shapes=_small_gather_scratch(),
    )(v)


def _ag2_plan(x_refs, out_refs, send_sems, recv_sems, local_sems):
    na = len(x_refs)
    x, y, c = _my_pos()
    me, sibling = (x, y, c), (x, y, 1 - c)
    chips = [(1 - x, y), (x, 1 - y), (1 - x, 1 - y)]

    def rows(i, px, py, pc):
        m_per = x_refs[i].shape[0]
        return out_refs[i].at[pl.ds(pl.multiple_of((4 * px + 2 * py + pc) * m_per, 16 if m_per % 16 == 0 else 8), m_per), :]

    def copies(k, block, to, from_shard=False):
        return [pltpu.make_async_remote_copy(
            src_ref=x_refs[i] if from_shard else rows(i, *block), dst_ref=rows(i, *block),
            send_sem=send_sems.at[k * na + i], recv_sem=recv_sems.at[k * na + i], device_id=to, device_id_type=MESH)
            for i in range(na)]

    def mine():
        return [pltpu.make_async_copy(x_refs[i], rows(i, *me), local_sems.at[i]) for i in range(na)]

    def first():
        cps = copies(0, me, sibling, True)
        for j, chip in enumerate(chips):
            cps += copies(1 + j, me, (*chip, c), True)
        return cps

    def start():
        for cp in mine() + first():
            cp.start()

    def forward():
        for j, chip in enumerate(chips):
            for cp in copies(1 + j, (*chip, c), me):
                cp.wait_recv()
            for cp in copies(4 + j, (*chip, c), sibling):
                cp.start()

    def finish():
        for cp in copies(0, sibling, me):
            cp.wait_recv()
        for j, chip in enumerate(chips):
            for cp in copies(4 + j, (*chip, 1 - c), me):
                cp.wait_recv()
        for cp in first():
            cp.wait_send()
        for j, chip in enumerate(chips):
            for cp in copies(4 + j, (*chip, c), sibling):
                cp.wait_send()
        for cp in mine():
            cp.wait()

    return start, forward, finish


def _ag2_scratch(na):
    return [pltpu.SemaphoreType.DMA((7 * na,)), pltpu.SemaphoreType.DMA((7 * na,)), pltpu.SemaphoreType.DMA((na,))]


def _a2a_plan(g_ref, recv_ref, send_sems, recv_sems, local_sem):
    x, y, c = _my_pos()
    me = 4 * x + 2 * y + c
    peers = []
    for k in range(1, NDEV):
        kx, ky, kc = (k >> 2) & 1, (k >> 1) & 1, k & 1
        peers.append((x ^ kx, y ^ ky, c ^ kc))

    def sends():
        return [pltpu.make_async_remote_copy(
            src_ref=g_ref.at[4 * px + 2 * py + pc], dst_ref=recv_ref.at[me], send_sem=send_sems.at[k], recv_sem=recv_sems.at[k],
            device_id=(px, py, pc), device_id_type=MESH) for k, (px, py, pc) in enumerate(peers)]

    def own():
        return pltpu.make_async_copy(g_ref.at[me], recv_ref.at[me], local_sem)

    def start():
        own().start()
        for cp in sends():
            cp.start()

    def finish():
        for k, (px, py, pc) in enumerate(peers):
            pltpu.make_async_remote_copy(
                src_ref=g_ref.at[me], dst_ref=recv_ref.at[4 * px + 2 * py + pc], send_sem=send_sems.at[k],
                recv_sem=recv_sems.at[k], device_id=(x, y, c), device_id_type=MESH).wait_recv()
        for cp in sends():
            cp.wait_send()
        own().wait()

    return start, finish


def _a2a_scratch():
    return [pltpu.SemaphoreType.DMA((NDEV - 1,)), pltpu.SemaphoreType.DMA((NDEV - 1,)), pltpu.SemaphoreType.DMA]


def _reduce_scatter2(g, small, name):
    _, r, n = g.shape
    ch = 16
    nch = r // ch
    ns = len(small)

    def body(g_ref, *rest):
        v_refs, out_ref, vout_refs = rest[:ns], rest[ns], rest[ns + 1:2 * ns + 1]
        a_ref, h_ref, b_ref, s1_send, s1_recv, s2_send, s2_recv = rest[2 * ns + 1:2 * ns + 8]
        gather_sems = rest[2 * ns + 8:]
        gathers = [_small_gather_plan(v_refs[i], vout_refs[i], *gather_sems[2 * i:2 * i + 2]) for i in range(ns)]
        for start, _ in gathers:
            start()
        x, y, c = _my_pos()
        sibling = (x, y, 1 - c)
        s1 = []
        for j in range(4):
            cp = pltpu.make_async_remote_copy(
                src_ref=g_ref.at[2 * j + (1 - c)], dst_ref=a_ref.at[j], send_sem=s1_send.at[j], recv_sem=s1_recv.at[j],
                device_id=sibling, device_id_type=MESH)
            cp.start()
            s1.append(cp)
        for cp in s1:
            cp.wait_recv()

        def add1(i, _):
            rr = pl.ds(pl.multiple_of(i * ch, ch), ch)
            for j in range(4):
                h_ref[j, rr, :] = (g_ref[2 * j + c, rr, :].astype(f32) + a_ref[j, rr, :].astype(f32)).astype(bf16)
            return 0
        lax.fori_loop(0, nch, add1, 0)
        mychip = 2 * x + y
        s2 = []
        for m in range(1, 4):
            mx, my_ = (m >> 1) & 1, m & 1
            px, py = x ^ mx, y ^ my_
            cp = pltpu.make_async_remote_copy(
                src_ref=h_ref.at[2 * px + py], dst_ref=b_ref.at[m - 1], send_sem=s2_send.at[m - 1], recv_sem=s2_recv.at[m - 1],
                device_id=(px, py, c), device_id_type=MESH)
            cp.start()
            s2.append(cp)
        for cp in s2:
            cp.wait_recv()

        def add2(i, _):
            rr = pl.ds(pl.multiple_of(i * ch, ch), ch)
            acc = h_ref[mychip, rr, :].astype(f32)
            for m in range(3):
                acc = acc + b_ref[m, rr, :].astype(f32)
            out_ref[rr, :] = acc
            return 0
        lax.fori_loop(0, nch, add2, 0)
        for cp in s1 + s2:
            cp.wait_send()
        for _, finish in gathers:
            finish()

    vmem = pl.BlockSpec(memory_space=pltpu.VMEM)
    return pl.pallas_call(
        body, name=name,
        out_shape=[jax.ShapeDtypeStruct((r, n), f32)] + [jax.ShapeDtypeStruct((NDEV,) + v.shape, v.dtype) for v in small],
        in_specs=[vmem] * (1 + ns),
        out_specs=[vmem] * (1 + ns),
        scratch_shapes=[pltpu.VMEM((4, r, n), bf16), pltpu.VMEM((4, r, n), bf16), pltpu.VMEM((3, r, n), bf16),
                        pltpu.SemaphoreType.DMA((4,)), pltpu.SemaphoreType.DMA((4,)),
                        pltpu.SemaphoreType.DMA((3,)), pltpu.SemaphoreType.DMA((3,))] + _small_gather_scratch() * ns,
        compiler_params=pltpu.CompilerParams(vmem_limit_bytes=VMEM_LIMIT),
    )(g, *small)


def _head(w_in_sh, c8, cctx8, w_mod_sh, b_sh, convpay, bias_base, bias_valid):
    nmod = w_mod_sh.shape[1]
    npay = convpay.shape[0]

    def body(w_ref, c_ref, cc_ref, wm_ref, b_ref, pay_ref, base_ref, valid_ref,
             wout_ref, call_ref, mall_ref, pall_ref, bias_ref, mod_s,
             ag_send, ag_recv, ag_local, c_send, c_recv, m_send, m_recv, p_send, p_recv):
        start, forward, finish = _ag2_plan([w_ref], [wout_ref], ag_send, ag_recv, ag_local)
        c_start, c_finish = _small_gather_plan(c_ref, call_ref, c_send, c_recv)
        m_start, m_finish = _small_gather_plan(mod_s, mall_ref, m_send, m_recv)
        p_start, p_finish = _small_gather_plan(pay_ref, pall_ref, p_send, p_recv)
        c_start()
        start()
        p_start()
        _write_bias_table(base_ref, valid_ref, bias_ref)
        c_finish()
        acc = jnp.zeros((16, nmod), f32)
        for j in range(D // 128):
            rows = jnp.concatenate([call_ref[:, j, :], cc_ref[j:j + 1, :], jnp.zeros((7, 128), f32)], axis=0)
            act = (rows * _sigmoid(rows)).astype(bf16)
            acc = acc + _dot(act, wm_ref[j * 128:(j + 1) * 128, :].astype(bf16))
        mod_s[...] = acc + b_ref[...]
        m_start()
        forward()
        finish()
        m_finish()
        p_finish()

    vmem = pl.BlockSpec(memory_space=pltpu.VMEM)
    return pl.pallas_call(
        body, name="head",
        out_shape=[jax.ShapeDtypeStruct((NDEV * w_in_sh.shape[0], D), bf16), jax.ShapeDtypeStruct((NDEV, 8, 128), f32),
                   jax.ShapeDtypeStruct((NDEV, 16, nmod), f32), jax.ShapeDtypeStruct((NDEV, npay, 128), f32),
                   jax.ShapeDtypeStruct((8, NH, GW, WR * GW), f32)],
        in_specs=[vmem] * 8, out_specs=[vmem] * 5,
        scratch_shapes=[pltpu.VMEM((16, nmod), f32)] + _ag2_scratch(1) + _small_gather_scratch() * 3,
        compiler_params=pltpu.CompilerParams(vmem_limit_bytes=VMEM_LIMIT),
    )(w_in_sh, c8, cctx8, w_mod_sh, b_sh, convpay, bias_base, bias_valid)


def _mod_bwd(cvec, dm_sh, w_sh):
    def body(c_ref, dm_ref, w_ref, gw_ref, gc_ref):
        cv = c_ref[...]
        act = (cv * _sigmoid(cv)).astype(bf16)
        gw_ref[...] = _dg(act, dm_ref[...].astype(bf16), TN)
        gc_ref[...] = _dg(dm_ref[8:16, :].astype(bf16), w_ref[...].astype(bf16), NT)
    return pl.pallas_call(
        body, name="mod_bwd",
        out_shape=(jax.ShapeDtypeStruct(w_sh.shape, f32), jax.ShapeDtypeStruct((8, D), f32)))(cvec, dm_sh, w_sh)


def _sum_rows8(a, name):
    n = a.shape[1]

    def body(a_ref, o_ref):
        acc = a_ref[0]
        for d in range(1, NDEV):
            acc = acc + a_ref[d]
        o_ref[...] = acc
    return pl.pallas_call(body, name=name, out_shape=jax.ShapeDtypeStruct((n, 128), f32))(a)


def _in_proj(x0, ctx0, g1, modv, w_inT, shards):
    tm = 256
    nt = TA // tm
    nx = T // tm
    na = len(shards)

    def body(x_ref, c_ref, g_ref, mod_ref, w_ref, *rest):
        x_refs, (h_ref, q_ref, k_ref, v_ref, a_ref, gg_ref) = rest[:na], rest[na:na + 6]
        out_refs, sems = rest[na + 6:2 * na + 6], rest[2 * na + 6:]
        i = pl.program_id(0)
        if na:
            start, forward, finish = _ag2_plan(x_refs, out_refs, *sems)
            pl.when(i == 0)(start)
            pl.when(i == nt - 2)(forward)
        is_ctx = i == nt - 1
        xv = jnp.where(is_ctx, c_ref[...], x_ref[...])
        rstd = lax.rsqrt(jnp.mean(xv * xv, axis=-1, keepdims=True) + EPS)
        sh = jnp.where(is_ctx, mod_ref[6:7, :], mod_ref[0:1, :])
        sc = jnp.where(is_ctx, mod_ref[7:8, :], mod_ref[1:2, :])
        h = ((xv * rstd * g_ref[...]) * (1.0 + sc) + sh).astype(bf16)
        h_ref[...] = h
        for j, o_ref in enumerate((q_ref, k_ref, v_ref, a_ref, gg_ref)):
            o_ref[...] = _dg(h, w_ref[j * DA:(j + 1) * DA, :], NT).astype(o_ref.dtype)
        if na:
            pl.when(is_ctx)(finish)

    row = lambda w: pl.BlockSpec((tm, w), lambda i: (i, 0))
    hbm = pl.BlockSpec(memory_space=pl.ANY)
    return pl.pallas_call(
        body, name="in_proj", grid=(nt,),
        in_specs=[pl.BlockSpec((tm, D), lambda i: (jnp.minimum(i, nx - 1), 0)), _full((TC, D)),
                  _full((1, D)), _full((8, D)), _full((5 * DA, D))] + [hbm] * na,
        out_specs=[row(D), row(DA), row(DA), row(DA), row(DA), row(DA)] + [hbm] * na,
        out_shape=[jax.ShapeDtypeStruct((TA, D), bf16)] + [jax.ShapeDtypeStruct((TA, DA), bf16)] * 3
                  + [jax.ShapeDtypeStruct((TA, DA), f32)] * 2
                  + [jax.ShapeDtypeStruct((NDEV * sh.shape[0], sh.shape[1]), sh.dtype) for sh in shards],
        scratch_shapes=_ag2_scratch(na) if na else [],
        compiler_params=_cp("arbitrary"),
    )(x0, ctx0, g1, modv, w_inT, *shards)


def _win_start(r):
    return jnp.clip(r - WR // 2, 0, GW - WR)


def _pattern(r):
    return _win_start(r) - r + (WR - 1)


def _bias_table_inputs(rpb):
    qc = np.arange(GW)[:, None]
    kc = np.arange(GW)[None, :]
    cs = np.clip(qc - NCOL // 2, 0, GW - NCOL)
    valid = np.tile(((kc >= cs) & (kc < cs + NCOL)).astype(np.int32), (1, WR))
    pad = jnp.pad(rpb, ((0, 0), (0, 0), (0, GW - (2 * NCOL - 1))))
    base = jnp.stack([pad[:, p:p + WR, :].reshape(NH, WR * GW) for p in range(8)])
    return base, jnp.asarray(valid)


def _write_bias_table(base_ref, valid_ref, o_ref):
    ok = valid_ref[...] != 0
    for p in range(8):
        for h in range(NH):
            row = jnp.broadcast_to(base_ref[p, h:h + 1, :], (GW, WR * GW))
            skew = pltpu.roll(row, WR * GW - (NCOL - 1), 1, stride=1, stride_axis=0)
            o_ref[p, h] = jnp.where(ok, skew, NEG)


def _rpb_tables():
    lane_map = np.zeros((WR * GW, WR, 2 * NCOL - 1), np.float32)
    for i in range(WR):
        for t in range(GW):
            if t >= GW - NCOL:
                lane_map[i * GW + t, i, t - (GW - NCOL)] = 1.0
            elif t < NCOL - 1:
                lane_map[i * GW + t, (i - 1) % WR, t + NCOL] = 1.0
    p = np.arange(8)[:, None]
    i = np.arange(WR)[None, :]
    r_hot = ((p + i)[:, :, None] == np.arange(2 * WR - 1)[None, None, :]).astype(np.float32)
    return lane_map, r_hot


ATTN_RPS = 4
AG_FORWARD_STEP = 15


def _stack_pair(x2, lo):
    z = jnp.zeros_like(x2)
    return jnp.concatenate([jnp.where(lo, x2, z), jnp.where(lo, z, x2)], axis=0)


def _attn_fwd(q, k, v, bias_tab, shards):
    na = len(shards)
    rps = ATTN_RPS
    nsteps = GW // rps

    def body(q_ref, k_ref, v_ref, *rest):
        b_refs, rest = rest[:rps], rest[rps:]
        x_refs, (y_ref, lse_ref), out_refs, sems = rest[:na], rest[na:na + 2], rest[na + 2:2 * na + 2], rest[2 * na + 2:]
        i = pl.program_id(0)
        if na:
            start, forward, finish = _ag2_plan(x_refs, out_refs, *sems)
            pl.when(i == 0)(start)
            pl.when(i == AG_FORWARD_STEP)(forward)
        lo = lax.broadcasted_iota(jnp.int32, (GW, 2 * HD), 1) < HD
        for rr in range(rps):
            rows = slice(rr * GW, (rr + 1) * GW)
            ks = pl.multiple_of(_win_start(i * rps + rr) * GW, GW)
            qq = q_ref[rows, :]
            kv, scores = [], []
            for pr in range(NH // 2):
                ps = slice(pr * 2 * HD, (pr + 1) * 2 * HD)
                qst = _stack_pair(qq[:, ps], lo)
                kw, kc = k_ref[pl.ds(ks, WR * GW), ps], k_ref[T:TA, ps]
                kv.append((v_ref[pl.ds(ks, WR * GW), ps], v_ref[T:TA, ps]))
                bias2 = b_refs[rr][0, 2 * pr:2 * pr + 2].reshape(2 * GW, WR * GW)
                scores.append((_dg(qst, kw, NT) * SCALE + bias2, _dg(qst, kc, NT) * SCALE))
            probs = []
            for pr, (sl, sc) in enumerate(scores):
                m = jnp.maximum(jnp.max(sl, axis=-1, keepdims=True), jnp.max(sc, axis=-1, keepdims=True))
                pl_ = jnp.exp(sl - m)
                pc = jnp.exp(sc - m)
                l = jnp.sum(pl_, axis=-1, keepdims=True) + jnp.sum(pc, axis=-1, keepdims=True)
                lse = m + jnp.log(l)
                lse_ref[rows, 2 * pr:2 * pr + 1] = lse[0:GW]
                lse_ref[rows, 2 * pr + 1:2 * pr + 2] = lse[GW:]
                probs.append((pl_.astype(bf16), pc.astype(bf16), 1.0 / l))
            for pr in range(NH // 2):
                ps = slice(pr * 2 * HD, (pr + 1) * 2 * HD)
                vw, vc = kv[pr]
                pb, cb, rl = probs[pr]
                o = (_dot(pb, vw) + _dot(cb, vc)) * rl
                y_ref[rows, ps] = jnp.where(lo, o[0:GW], o[GW:]).astype(bf16)
        if na:
            pl.when(i == nsteps - 1)(finish)

    hbm = pl.BlockSpec(memory_space=pl.ANY)
    rowq = lambda w: pl.BlockSpec((rps * GW, w), lambda i: (i, 0))
    bias = [pl.BlockSpec((1, NH, GW, WR * GW), lambda i, rr=rr: (_pattern(i * rps + rr), 0, 0, 0)) for rr in range(rps)]
    return pl.pallas_call(
        body, name="attn_fwd", grid=(nsteps,),
        in_specs=[rowq(DA), _full((TA, DA)), _full((TA, DA))] + bias + [hbm] * na,
        out_specs=[rowq(DA), rowq(NH)] + [hbm] * na,
        out_shape=[jax.ShapeDtypeStruct((T, DA), bf16), jax.ShapeDtypeStruct((T, NH), f32)]
                  + [jax.ShapeDtypeStruct((NDEV * s.shape[0], s.shape[1]), s.dtype) for s in shards],
        scratch_shapes=_ag2_scratch(na) if na else [],
        compiler_params=_cp("arbitrary"),
    )(q, k, v, *([bias_tab] * rps), *shards)


CONV_TT = 512
HALO = 16


def _halo_specs(tt, w, nrows_blocks):
    per = tt // HALO
    prev = pl.BlockSpec((HALO, w), lambda i: (jnp.maximum(i * per - 1, 0), 0))
    cur = pl.BlockSpec((tt, w), lambda i: (i, 0))
    nxt = pl.BlockSpec((HALO, w), lambda i: (jnp.minimum((i + 1) * per, nrows_blocks - 1), 0))
    return [prev, cur, nxt]


def _shifted_copies(rot, wn):
    for b in range(1, 8):
        rot[b, 0:wn - 8, :] = rot[0, pl.ds(b, wn - 8), :]


def _conf_fwd(a, g, conv_w, conv_b, ln_g, ln_b, shards):
    tt = CONV_TT
    nt = T // tt
    sub = 32
    wn = tt + 2 * HALO
    na = len(shards)

    def body(ap, ac, an, gp, gc, gn, w_ref, b_ref, lg_ref, lb_ref, *rest):
        x_refs, (y_ref, cv_ref), out_refs, (rot, *sems) = rest[:na], rest[na:na + 2], rest[na + 2:2 * na + 2], rest[2 * na + 2:]
        i = pl.program_id(0)
        if na:
            start, forward, finish = _ag2_plan(x_refs, out_refs, *sems)
            pl.when(i == 0)(start)
            pl.when(i == nt // 2)(forward)
        rot[0, 0:HALO, :] = jnp.where(i > 0, ap[...] * _sigmoid(gp[...]), 0.0)
        rot[0, HALO:HALO + tt, :] = ac[...] * _sigmoid(gc[...])
        rot[0, HALO + tt:, :] = jnp.where(i < nt - 1, an[...] * _sigmoid(gn[...]), 0.0)
        _shifted_copies(rot, wn)
        w = w_ref[...]
        for s in range(tt // sub):
            acc = jnp.zeros((sub, DA), f32)
            for j in range(CW):
                a8, b8 = divmod(1 + j, 8)
                acc = acc + rot[b8, pl.ds(s * sub + 8 * a8, sub), :] * w[j:j + 1, :]
            cv = acc + b_ref[...]
            cv_ref[pl.ds(s * sub, sub), :] = cv
            mu = jnp.mean(cv, axis=-1, keepdims=True)
            xc = cv - mu
            rstd = lax.rsqrt(jnp.mean(xc * xc, axis=-1, keepdims=True) + EPS)
            z = xc * rstd * lg_ref[...] + lb_ref[...]
            y_ref[pl.ds(s * sub, sub), :] = (z * _sigmoid(z)).astype(bf16)
        if na:
            pl.when(i == nt - 1)(finish)

    hs = _halo_specs(tt, DA, T // HALO)
    hbm = pl.BlockSpec(memory_space=pl.ANY)
    return pl.pallas_call(
        body, name="conf_fwd", grid=(nt,),
        in_specs=hs + hs + [_full((CW, DA)), _full((1, DA)), _full((1, DA)), _full((1, DA))] + [hbm] * na,
        out_specs=[pl.BlockSpec((tt, DA), lambda i: (i, 0)), pl.BlockSpec((tt, DA), lambda i: (i, 0))] + [hbm] * na,
        out_shape=[jax.ShapeDtypeStruct((T, DA), bf16), jax.ShapeDtypeStruct((T, DA), f32)]
                  + [jax.ShapeDtypeStruct((NDEV * s.shape[0], s.shape[1]), s.dtype) for s in shards],
        scratch_shapes=[pltpu.VMEM((8, wn, DA), f32)] + (_ag2_scratch(na) if na else []),
        compiler_params=_cp("arbitrary"),
    )(a, a, a, g, g, g, conv_w, conv_b, ln_g, ln_b, *shards)


def _out_proj(xa, y_na, y_cv, w_out, modv, g2):
    tm = 512

    def body(x_ref, ya_ref, yc_ref, w_ref, mod_ref, g_ref, x1_ref, pj_ref, h2_ref):
        proj = _dot(ya_ref[...], w_ref[0:DA, :]) + _dot(yc_ref[...], w_ref[DA:D, :])
        x1 = x_ref[...] + mod_ref[2:3, :] * proj
        x1_ref[...] = x1
        pj_ref[...] = proj.astype(bf16)
        rstd = lax.rsqrt(jnp.mean(x1 * x1, axis=-1, keepdims=True) + EPS)
        h2_ref[...] = ((x1 * rstd * g_ref[...]) * (1.0 + mod_ref[4:5, :]) + mod_ref[3:4, :]).astype(bf16)

    row = lambda w: pl.BlockSpec((tm, w), lambda i: (i, 0))
    return pl.pallas_call(
        body, name="out_proj", grid=(T // tm,),
        in_specs=[row(D), row(DA), row(DA), _full((D, D)), _full((8, D)), _full((1, D))],
        out_specs=[row(D), row(D), row(D)],
        out_shape=[jax.ShapeDtypeStruct((T, D), f32), jax.ShapeDtypeStruct((T, D), bf16), jax.ShapeDtypeStruct((T, D), bf16)],
        compiler_params=_cp("parallel"),
    )(xa, y_na, y_cv, w_out, modv, g2)


FFN_TT = 2048
FFN_CT = 256
FFN_NC = F // FFN_CT
FFN_SUB = 32


def _row_neighbours(ref, r, n):
    blk = ref[pl.ds(r - 8, n + 16), :]
    return blk[8:8 + n, :], pltpu.roll(blk, 1, 0)[8:8 + n, :], pltpu.roll(blk, n + 15, 0)[8:8 + n, :]


def _ffn_specs(tt, ct, by_token_first):
    tc = (lambda f: (lambda t, c: f(t, c))) if by_token_first else (lambda f: (lambda c, t: f(t, c)))
    per = tt // HALO
    halo = [pl.BlockSpec((HALO, D), tc(lambda t, c: (jnp.maximum(t * per - 1, 0), 0))),
            pl.BlockSpec((tt, D), tc(lambda t, c: (t, 0))),
            pl.BlockSpec((HALO, D), tc(lambda t, c: (jnp.minimum((t + 1) * per, T // HALO - 1), 0)))]
    weights = [pl.BlockSpec((ct, D), tc(lambda t, c: (c, 0))), pl.BlockSpec((ct, D), tc(lambda t, c: (c + FFN_NC, 0))),
               pl.BlockSpec((3, ct), tc(lambda t, c: (0, c))), pl.BlockSpec((3, ct), tc(lambda t, c: (0, c + FFN_NC))),
               pl.BlockSpec((1, ct), tc(lambda t, c: (0, c))), pl.BlockSpec((1, ct), tc(lambda t, c: (0, c + FFN_NC))),
               pl.BlockSpec((ct, D), tc(lambda t, c: (c, 0)))]
    return halo, weights


def _ffn_fwd(h2, w_upT, fcw, fcb, w_down):
    tt, ct = FFN_TT, FFN_CT
    nt = T // tt
    wn = tt + 2 * HALO
    half = tt // 2

    def body(hp, hc, hn, wg_ref, wv_ref, cwg_ref, cwv_ref, cbg_ref, cbv_ref, wd_ref, o_ref, u_ref, u2_ref, hwin, uwin, act):
        t = pl.program_id(0)
        c = pl.program_id(1)

        @pl.when(c == 0)
        def _():
            hwin[0:HALO, :] = jnp.where(t > 0, hp[...], jnp.zeros_like(hp[...]))
            hwin[HALO:HALO + tt, :] = hc[...]
            hwin[HALO + tt:, :] = jnp.where(t < nt - 1, hn[...], jnp.zeros_like(hn[...]))
            o_ref[...] = jnp.zeros_like(o_ref)

        for r0, r1 in ((0, half + 2 * HALO), (half + 2 * HALO, wn)):
            hw = hwin[r0:r1, :]
            uwin[r0:r1, :ct] = _dg(hw, wg_ref[...], NT)
            uwin[r0:r1, ct:] = _dg(hw, wv_ref[...], NT)
        cw = jnp.concatenate([cwg_ref[...], cwv_ref[...]], axis=1)
        cb = jnp.concatenate([cbg_ref[...], cbv_ref[...]], axis=1)
        for p in range(2):
            for r in range(p * half, (p + 1) * half, FFN_SUB):
                uc, prev, nxt = _row_neighbours(uwin, HALO + r, FFN_SUB)
                u2 = prev * cw[0:1, :] + uc * cw[1:2, :] + nxt * cw[2:3, :] + cb
                u_ref[r:r + FFN_SUB, :] = uc.astype(bf16)
                u2_ref[r:r + FFN_SUB, :] = u2
                gate = u2[:, :ct]
                act[r:r + FFN_SUB, :] = (gate * _sigmoid(gate) * u2[:, ct:]).astype(bf16)
            rows = slice(p * half, (p + 1) * half)
            o_ref[rows, :] += _dot(act[rows, :], wd_ref[...])

    halo, weights = _ffn_specs(tt, ct, True)
    pair = pl.BlockSpec((tt, 2 * ct), lambda t, c: (t, c))
    return pl.pallas_call(
        body, name="ffn_fwd", grid=(nt, FFN_NC),
        in_specs=halo + weights,
        out_specs=[pl.BlockSpec((tt, D), lambda t, c: (t, 0)), pair, pair],
        out_shape=[jax.ShapeDtypeStruct((T, D), f32), jax.ShapeDtypeStruct((T, F2), bf16), jax.ShapeDtypeStruct((T, F2), f32)],
        scratch_shapes=[pltpu.VMEM((wn, D), bf16), pltpu.VMEM((wn, 2 * ct), f32), pltpu.VMEM((tt, ct), bf16)],
        compiler_params=_cp("parallel", "arbitrary"),
    )(h2, h2, h2, w_upT, w_upT, fcw, fcw, fcb, fcb, w_down)


def _loss_bwd(ffn, x1, tgt, modv, gf):
    tm = 1024
    nt = T // tm

    def body(f_ref, x1_ref, t_ref, mod_ref, g_ref, dx2_ref, df_ref, s_ref):
        i = pl.program_id(0)

        @pl.when(i == 0)
        def _():
            s_ref[...] = jnp.zeros_like(s_ref)

        ff = f_ref[...]
        gt2 = mod_ref[5:6, :]
        x2 = x1_ref[...] + gt2 * ff
        rstd = lax.rsqrt(jnp.mean(x2 * x2, axis=-1, keepdims=True) + EPS)
        xh = x2 * rstd
        gfv = g_ref[...]
        e = xh * gfv - t_ref[...]
        dy = e * (1.0 / D)
        dxh = dy * gfv
        dx2 = rstd * (dxh - xh * jnp.mean(dxh * xh, axis=-1, keepdims=True))
        dx2_ref[...] = dx2
        df_ref[...] = (dx2 * gt2).astype(bf16)
        s_ref[0:1, :] += jnp.sum(dy * xh, axis=0, keepdims=True)
        s_ref[1:2, :] += jnp.sum(dx2 * ff, axis=0, keepdims=True)
        s_ref[2:3, :] += jnp.sum(e * e, axis=0, keepdims=True)

        @pl.when(i == nt - 1)
        def _():
            tot = jnp.sum(s_ref[2:3, :], axis=-1, keepdims=True) * (0.5 / D)
            s_ref[3:4, :] = jnp.broadcast_to(tot, (1, D))

    row = lambda: pl.BlockSpec((tm, D), lambda i: (i, 0))
    return pl.pallas_call(
        body, name="loss_bwd", grid=(nt,),
        in_specs=[row(), row(), row(), _full((8, D)), _full((1, D))],
        out_specs=[row(), row(), _full((8, D))],
        out_shape=[jax.ShapeDtypeStruct((T, D), f32), jax.ShapeDtypeStruct((T, D), bf16), jax.ShapeDtypeStruct((8, D), f32)],
        compiler_params=_cp("arbitrary"),
    )(ffn, x1, tgt, modv, gf)


def _ffn_bwd(h2, dffn, u_t, u2_t, fcw, w_down):
    tt, ct = FFN_TT, FFN_CT
    nt = T // tt
    wn = tt + 2 * HALO
    half = tt // 2

    def body(dp, dc, dn, hc, uc_ref, u2p, u2c, u2n, cwg_ref, cwv_ref, wd_ref,
             dug_ref, duv_ref, dwu_ref, dwd_ref, dcwg_ref, dcwv_ref, dcbg_ref, dcbv_ref,
             dwin, d2win, dawin, accu, accd, act, du):
        t = pl.program_id(1)
        first, last = t == 0, t == nt - 1
        zero = jnp.zeros((HALO, D), bf16)
        dwin[0:HALO, :] = jnp.where(first, zero, dp[...])
        dwin[HALO:HALO + tt, :] = dc[...]
        dwin[HALO + tt:, :] = jnp.where(last, zero, dn[...])

        @pl.when(first)
        def _():
            for r in (accu, accd, dcwg_ref, dcwv_ref, dcbg_ref, dcbv_ref):
                r[...] = jnp.zeros_like(r)

        cw = jnp.concatenate([cwg_ref[...], cwv_ref[...]], axis=1)
        split = half + 2 * HALO
        for r0, r1 in ((0, split), (split, wn)):
            dawin[r0:r1, :] = _dg(dwin[r0:r1, :], wd_ref[...], NT)

        def grads(u2v, dact):
            gate, val = u2v[:, :ct], u2v[:, ct:]
            sg = _sigmoid(gate)
            silu = gate * sg
            return dact * val * (sg * (1.0 + gate * (1.0 - sg))), dact * silu, silu * val

        for blk, r0 in ((u2p, 0), (u2n, HALO + tt)):
            dgate, dval, _ = grads(blk[...], dawin[r0:r0 + HALO, :])
            d2win[r0:r0 + HALO, :ct] = dgate
            d2win[r0:r0 + HALO, ct:] = dval
        for p in range(2):
            rows = slice(p * half, (p + 1) * half)
            for r in range(p * half, (p + 1) * half, FFN_SUB):
                dgate, dval, av = grads(u2c[r:r + FFN_SUB, :], dawin[HALO + r:HALO + r + FFN_SUB, :])
                d2win[HALO + r:HALO + r + FFN_SUB, :ct] = dgate
                d2win[HALO + r:HALO + r + FFN_SUB, ct:] = dval
                act[r:r + FFN_SUB, :] = av.astype(bf16)
            accd[...] += _dg(act[rows, :], dc[rows, :], TN)

        def fold8(x):
            out = x[0:8]
            for k in range(8, FFN_SUB, 8):
                out = out + x[k:k + 8]
            return out

        for p in range(2):
            rows = slice(p * half, (p + 1) * half)
            sums = [jnp.zeros((8, 2 * ct), f32) for _ in range(4)]
            for r in range(p * half, (p + 1) * half, FFN_SUB):
                d2c, d2m, d2p = _row_neighbours(d2win, HALO + r, FFN_SUB)
                ucur = uc_ref[r:r + FFN_SUB, :].astype(f32)
                sums[0] = sums[0] + fold8(d2c)
                for kk, dd in enumerate((d2p, d2c, d2m)):
                    sums[1 + kk] = sums[1 + kk] + fold8(ucur * dd)
                du[r:r + FFN_SUB, :] = (d2p * cw[0:1, :] + d2c * cw[1:2, :] + d2m * cw[2:3, :]).astype(bf16)
            dcb = jnp.sum(sums[0], axis=0, keepdims=True)
            dcbg_ref[...] += dcb[:, :ct]
            dcbv_ref[...] += dcb[:, ct:]
            for kk in range(3):
                dck = jnp.sum(sums[1 + kk], axis=0, keepdims=True)
                dcwg_ref[kk:kk + 1, :] += dck[:, :ct]
                dcwv_ref[kk:kk + 1, :] += dck[:, ct:]
            dug_ref[rows, :] = du[rows, :ct]
            duv_ref[rows, :] = du[rows, ct:]
            accu[...] += _dg(du[rows, :], hc[rows, :], TN)

        @pl.when(last)
        def _():
            dwu_ref[0] = accu[0:ct, :].astype(bf16)
            dwu_ref[1] = accu[ct:, :].astype(bf16)
            dwd_ref[...] = accd[...].astype(bf16)

    per = tt // HALO
    prev = lambda w: (lambda c, t: (jnp.maximum(t * per - 1, 0), c if w else 0))
    nxt = lambda w: (lambda c, t: (jnp.minimum((t + 1) * per, T // HALO - 1), c if w else 0))
    tile = lambda: pl.BlockSpec((ct, D), lambda c, t: (c, 0))
    lane = lambda r, off: pl.BlockSpec((r, ct), lambda c, t: (0, c + off))
    return pl.pallas_call(
        body, name="ffn_bwd", grid=(FFN_NC, nt),
        in_specs=[pl.BlockSpec((HALO, D), prev(False)), pl.BlockSpec((tt, D), lambda c, t: (t, 0)), pl.BlockSpec((HALO, D), nxt(False)),
                  pl.BlockSpec((tt, D), lambda c, t: (t, 0)), pl.BlockSpec((tt, 2 * ct), lambda c, t: (t, c)),
                  pl.BlockSpec((HALO, 2 * ct), prev(True)), pl.BlockSpec((tt, 2 * ct), lambda c, t: (t, c)),
                  pl.BlockSpec((HALO, 2 * ct), nxt(True)), lane(3, 0), lane(3, FFN_NC), tile()],
        out_specs=[pl.BlockSpec((tt, ct), lambda c, t: (t, c)), pl.BlockSpec((tt, ct), lambda c, t: (t, c)),
                   pl.BlockSpec((2, ct, D), lambda c, t: (0, c, 0)), tile(), lane(3, 0), lane(3, 0), lane(1, 0), lane(1, 0)],
        out_shape=[jax.ShapeDtypeStruct((T, F), bf16), jax.ShapeDtypeStruct((T, F), bf16),
                   jax.ShapeDtypeStruct((2, F, D), bf16), jax.ShapeDtypeStruct((F, D), bf16),
                   jax.ShapeDtypeStruct((3, F), f32), jax.ShapeDtypeStruct((3, F), f32),
                   jax.ShapeDtypeStruct((1, F), f32), jax.ShapeDtypeStruct((1, F), f32)],
        scratch_shapes=[pltpu.VMEM((wn, D), bf16), pltpu.VMEM((wn, 2 * ct), f32), pltpu.VMEM((wn, ct), f32),
                        pltpu.VMEM((2 * ct, D), f32), pltpu.VMEM((ct, D), f32),
                        pltpu.VMEM((tt, ct), bf16), pltpu.VMEM((tt, 2 * ct), bf16)],
        compiler_params=_cp("parallel", "arbitrary"),
    )(dffn, dffn, dffn, h2, u_t, u2_t, u2_t, u2_t, fcw, fcw, w_down)


def _norm_bwd(dh, xv, gain, sh_sc, rstd):
    xh = xv * rstd
    n = xh * gain
    dn = dh * (1.0 + sh_sc)
    dxh = dn * gain
    dx = rstd * (dxh - xh * jnp.mean(dxh * xh, axis=-1, keepdims=True))
    return (dx, jnp.sum(dh, axis=0, keepdims=True), jnp.sum(dh * n, axis=0, keepdims=True),
            jnp.sum(dn * xh, axis=0, keepdims=True))


def _norm2_bwd(dug, duv, w_upT, x1, dx2, proj, w_out, y_na, y_cv, modv, g2):
    tm = 512
    nt = T // tm

    def body(dug_ref, duv_ref, w_ref, x1_ref, dx2_ref, pj_ref, wo_ref, ya_ref, yc_ref, mod_ref, g_ref,
             dx1_ref, dya_ref, dyc_ref, dwo_ref, s_ref, acc):
        i = pl.program_id(0)

        @pl.when(i == 0)
        def _():
            s_ref[...] = jnp.zeros_like(s_ref)
            acc[...] = jnp.zeros_like(acc)

        dh2 = _dot(dug_ref[...], w_ref[0:F, :]) + _dot(duv_ref[...], w_ref[F:F2, :])
        x1 = x1_ref[...]
        rstd = lax.rsqrt(jnp.mean(x1 * x1, axis=-1, keepdims=True) + EPS)
        dxn, dsh, dsc, dgn = _norm_bwd(dh2, x1, g_ref[...], mod_ref[4:5, :], rstd)
        dx1 = dx2_ref[...] + dxn
        dx1_ref[...] = dx1
        dpj = (dx1 * mod_ref[2:3, :]).astype(bf16)
        dyc = _dg(dpj, wo_ref[...], NT)
        dya_ref[...] = dyc[:, :DA].astype(bf16)
        dyc_ref[...] = dyc[:, DA:]
        acc[0:DA, :] += _dg(ya_ref[...], dpj, TN)
        acc[DA:D, :] += _dg(yc_ref[...], dpj, TN)

        @pl.when(i == nt - 1)
        def _():
            dwo_ref[...] = acc[...].astype(bf16)

        s_ref[0:1, :] += dsh
        s_ref[1:2, :] += dsc
        s_ref[2:3, :] += dgn
        s_ref[3:4, :] += jnp.sum(dx1 * pj_ref[...].astype(f32), axis=0, keepdims=True)

    row = lambda w: pl.BlockSpec((tm, w), lambda i: (i, 0))
    return pl.pallas_call(
        body, name="norm2_bwd", grid=(nt,),
        in_specs=[row(F), row(F), _resident((F2, D)), row(D), row(D), row(D), _resident((D, D)), row(DA), row(DA),
                  _full((8, D)), _full((1, D))],
        out_specs=[row(D), row(DA), row(DA), _full((D, D)), _full((8, D))],
        out_shape=[jax.ShapeDtypeStruct((T, D), f32), jax.ShapeDtypeStruct((T, DA), bf16), jax.ShapeDtypeStruct((T, DA), f32),
                   jax.ShapeDtypeStruct((D, D), bf16), jax.ShapeDtypeStruct((8, D), f32)],
        scratch_shapes=[pltpu.VMEM((D, D), f32)],
        compiler_params=_cp("arbitrary"),
    )(dug, duv, w_upT, x1, dx2, proj, w_out, y_na, y_cv, modv, g2)


def _conf_bwd(a, g, cv, dy, conv_w, ln_g, ln_b, blocks):
    tt = CONV_TT
    nt = T // tt
    sub = 32
    wn = tt + 2 * HALO
    nb = len(blocks)

    def body(ap, ac, an, gp, gc, gn, cp_, cc, cn, dp, dc, dn, w_ref, lg_ref, lb_ref, *rest):
        g_refs, (da_ref, dg_ref, dcw_ref, s_ref) = rest[:nb], rest[nb:nb + 4]
        recv_refs, (urot, drot, wacc), a2a_sems = rest[nb + 4:2 * nb + 4], rest[2 * nb + 4:2 * nb + 7], rest[2 * nb + 7:]
        i = pl.program_id(0)
        first, last = i == 0, i == nt - 1
        plans = [_a2a_plan(g_refs[k], recv_refs[k], *a2a_sems[3 * k:3 * k + 3]) for k in range(nb)]
        for start, _ in plans:
            pl.when(first)(start)

        @pl.when(first)
        def _():
            s_ref[...] = jnp.zeros_like(s_ref)
            wacc[...] = jnp.zeros_like(wacc)

        lg, lb = lg_ref[...], lb_ref[...]

        def ln_bwd(cvv, dyv):
            mu = jnp.mean(cvv, axis=-1, keepdims=True)
            xc = cvv - mu
            rstd = lax.rsqrt(jnp.mean(xc * xc, axis=-1, keepdims=True) + EPS)
            yn = xc * rstd
            z = yn * lg + lb
            sz = _sigmoid(z)
            dz = dyv * (sz * (1.0 + z * (1.0 - sz)))
            dyn = dz * lg
            dcv = rstd * (dyn - jnp.mean(dyn, axis=-1, keepdims=True) - yn * jnp.mean(dyn * yn, axis=-1, keepdims=True))
            return dcv, dz, yn

        urot[0, 0:HALO, :] = jnp.where(first, 0.0, ap[...] * _sigmoid(gp[...]))
        urot[0, HALO + tt:, :] = jnp.where(last, 0.0, an[...] * _sigmoid(gn[...]))
        drot[0, 0:HALO, :] = jnp.where(first, 0.0, ln_bwd(cp_[...], dp[...])[0])
        drot[0, HALO + tt:, :] = jnp.where(last, 0.0, ln_bwd(cn[...], dn[...])[0])
        for s in range(tt // sub):
            rr = pl.ds(s * sub, sub)
            urot[0, pl.ds(HALO + s * sub, sub), :] = ac[rr, :] * _sigmoid(gc[rr, :])
            dcv, dz, yn = ln_bwd(cc[rr, :], dc[rr, :])
            drot[0, pl.ds(HALO + s * sub, sub), :] = dcv
            s_ref[0:1, :] += jnp.sum(dcv, axis=0, keepdims=True)
            s_ref[1:2, :] += jnp.sum(dz * yn, axis=0, keepdims=True)
            s_ref[2:3, :] += jnp.sum(dz, axis=0, keepdims=True)
        _shifted_copies(urot, wn)
        _shifted_copies(drot, wn)
        w = w_ref[...]
        for s in range(tt // sub):
            rr = pl.ds(s * sub, sub)
            dcv = drot[0, pl.ds(HALO + s * sub, sub), :]
            acc = jnp.zeros((sub, DA), f32)
            for j in range(CW):
                ad, bd = divmod(2 * HALO - 1 - j, 8)
                au, bu = divmod(1 + j, 8)
                acc = acc + drot[bd, pl.ds(s * sub + 8 * ad, sub), :] * w[j:j + 1, :]
                part = urot[bu, pl.ds(s * sub + 8 * au, sub), :] * dcv
                wacc[j] += part[0:8] + part[8:16] + part[16:24] + part[24:32]
            av, gv = ac[rr, :], gc[rr, :]
            sg = _sigmoid(gv)
            da_ref[rr, :] = (acc * sg).astype(bf16)
            dg_ref[rr, :] = (acc * av * sg * (1.0 - sg)).astype(bf16)

        @pl.when(last)
        def _():
            for j in range(CW):
                dcw_ref[j:j + 1, :] = jnp.sum(wacc[j], axis=0, keepdims=True)
            dcw_ref[CW:CW + 1, :] = jnp.zeros((1, DA), f32)

        for _, finish in plans:
            pl.when(last)(finish)

    hs = _halo_specs(tt, DA, T // HALO)
    hbm = pl.BlockSpec(memory_space=pl.ANY)
    return pl.pallas_call(
        body, name="conf_bwd", grid=(nt,),
        in_specs=hs * 4 + [_full((CW, DA)), _full((1, DA)), _full((1, DA))] + [hbm] * nb,
        out_specs=[pl.BlockSpec((tt, DA), lambda i: (i, 0)), pl.BlockSpec((tt, DA), lambda i: (i, 0)),
                   _full((CW + 1, DA)), _full((8, DA))] + [hbm] * nb,
        out_shape=[jax.ShapeDtypeStruct((T, DA), bf16), jax.ShapeDtypeStruct((T, DA), bf16),
                   jax.ShapeDtypeStruct((CW + 1, DA), f32), jax.ShapeDtypeStruct((8, DA), f32)]
                  + [jax.ShapeDtypeStruct(b.shape, b.dtype) for b in blocks],
        scratch_shapes=[pltpu.VMEM((8, wn, DA), f32), pltpu.VMEM((8, wn, DA), f32), pltpu.VMEM((CW, 8, DA), f32)]
                       + _a2a_scratch() * nb,
        compiler_params=_cp("arbitrary"),
    )(a, a, a, g, g, g, cv, cv, cv, dy, dy, dy, conv_w, ln_g, ln_b, *blocks)


def _attn_bwd(q, k, v, y, dy, lse, bias_tab, blocks):
    zr = 256
    nb = len(blocks)
    rps = ATTN_RPS
    nsteps = GW // rps

    def body(q_ref, k_ref, v_ref, y_ref, dy_ref, lse_ref, *rest):
        b_refs, rest = rest[:rps], rest[rps:]
        g_refs, (dq_ref, dk_hbm, dv_hbm, db_hbm) = rest[:nb], rest[nb:nb + 4]
        recv_refs, (dk_s, dv_s, db_s, sem), a2a_sems = rest[nb + 4:2 * nb + 4], rest[2 * nb + 4:2 * nb + 8], rest[2 * nb + 8:]
        i = pl.program_id(0)
        plans = [_a2a_plan(g_refs[j], recv_refs[j], *a2a_sems[3 * j:3 * j + 3]) for j in range(nb)]
        for start, _ in plans:
            pl.when(i == 0)(start)

        @pl.when(i == 0)
        def _():
            def z(j, _):
                rr = pl.ds(pl.multiple_of(j * zr, zr), zr)
                dk_s[rr, :] = jnp.zeros((zr, DA), f32)
                dv_s[rr, :] = jnp.zeros((zr, DA), f32)
                return 0
            lax.fori_loop(0, TA // zr, z, 0)
            for p in range(8):
                db_s[p] = jnp.zeros((NH, GW, WR * GW), f32)

        lo = lax.broadcasted_iota(jnp.int32, (GW, 2 * HD), 1) < HD
        for rr in range(rps):
            r = i * rps + rr
            rows = slice(rr * GW, (rr + 1) * GW)
            ks = pl.multiple_of(_win_start(r) * GW, GW)
            pat = _pattern(r)
            win = pl.ds(ks, WR * GW)
            qq, yy, dyy, lse_v = q_ref[rows, :], y_ref[rows, :], dy_ref[rows, :], lse_ref[rows, :]
            ops, pairs = [], []
            for pr in range(NH // 2):
                ps = slice(pr * 2 * HD, (pr + 1) * 2 * HD)
                q2, do2 = qq[:, ps], dyy[:, ps]
                prod = do2.astype(f32) * yy[:, ps].astype(f32)
                delta = jnp.concatenate([jnp.sum(jnp.where(lo, prod, 0.0), axis=-1, keepdims=True),
                                         jnp.sum(jnp.where(lo, 0.0, prod), axis=-1, keepdims=True)], axis=0)
                qst, dost = _stack_pair(q2, lo), _stack_pair(do2, lo)
                kw, vw = k_ref[win, ps], v_ref[win, ps]
                kc, vc = k_ref[T:TA, ps], v_ref[T:TA, ps]
                ops.append((kw, kc))
                pairs.append((qst, dost, delta, _dg(qst, kw, NT), _dg(qst, kc, NT), _dg(dost, vw, NT), _dg(dost, vc, NT)))
            grads = []
            for pr, (qst, dost, delta, sl, sc, dpl, dpc) in enumerate(pairs):
                lh = jnp.concatenate([lse_v[:, 2 * pr:2 * pr + 1], lse_v[:, 2 * pr + 1:2 * pr + 2]], axis=0)
                bias2 = b_refs[rr][0, 2 * pr:2 * pr + 2].reshape(2 * GW, WR * GW)
                pl_ = jnp.exp(sl * SCALE + bias2 - lh)
                pc = jnp.exp(sc * SCALE - lh)
                dsl = pl_ * (dpl - delta)
                dsc = pc * (dpc - delta)
                db_s[pat, 2 * pr:2 * pr + 2] += dsl.reshape(2, GW, WR * GW)
                grads.append((qst, dost, pl_.astype(bf16), pc.astype(bf16), dsl.astype(bf16), dsc.astype(bf16)))
            for pr in range(NH // 2):
                ps = slice(pr * 2 * HD, (pr + 1) * 2 * HD)
                kw, kc = ops[pr]
                qst, dost, plb, pcb, dslb, dscb = grads[pr]
                dqst = _dot(dslb, kw) + _dot(dscb, kc)
                dq_ref[rows, ps] = (jnp.where(lo, dqst[0:GW], dqst[GW:]) * SCALE).astype(bf16)
                dk_s[win, ps] += _dg(dslb, qst, TN) * SCALE
                dv_s[win, ps] += _dg(plb, dost, TN)
                dk_s[T:TA, ps] += _dg(dscb, qst, TN) * SCALE
                dv_s[T:TA, ps] += _dg(pcb, dost, TN)

        @pl.when(i == nsteps - 1)
        def _():
            cps = [pltpu.make_async_copy(src, dst, sem.at[j])
                   for j, (src, dst) in enumerate(((dk_s, dk_hbm), (dv_s, dv_hbm), (db_s, db_hbm)))]
            for cp in cps:
                cp.start()
            for cp in cps:
                cp.wait()

        for _, finish in plans:
            pl.when(i == nsteps - 1)(finish)

    rowq = lambda w: pl.BlockSpec((rps * GW, w), lambda i: (i, 0))
    hbm = pl.BlockSpec(memory_space=pl.ANY)
    bias = [pl.BlockSpec((1, NH, GW, WR * GW), lambda i, rr=rr: (_pattern(i * rps + rr), 0, 0, 0)) for rr in range(rps)]
    return pl.pallas_call(
        body, name="attn_bwd", grid=(nsteps,),
        in_specs=[rowq(DA), _full((TA, DA)), _full((TA, DA)), rowq(DA), rowq(DA), rowq(NH)] + bias + [hbm] * nb,
        out_specs=[rowq(DA), hbm, hbm, hbm] + [hbm] * nb,
        out_shape=[jax.ShapeDtypeStruct((T, DA), bf16), jax.ShapeDtypeStruct((TA, DA), f32), jax.ShapeDtypeStruct((TA, DA), f32),
                   jax.ShapeDtypeStruct((8, NH, GW, WR * GW), f32)] + [jax.ShapeDtypeStruct(b.shape, b.dtype) for b in blocks],
        scratch_shapes=[pltpu.VMEM((TA, DA), f32), pltpu.VMEM((TA, DA), f32), pltpu.VMEM((8, NH, GW, WR * GW), f32),
                        pltpu.SemaphoreType.DMA((3,))] + _a2a_scratch() * nb,
        compiler_params=_cp("arbitrary"),
    )(q, k, v, y, dy, lse, *([bias_tab] * rps), *blocks)


def _rpb_reduce(dbias):
    rev = np.eye(GW, dtype=np.float32)[::-1]

    def body(d_ref, rev_ref, o_ref):
        rv = rev_ref[...]
        for h in range(NH):
            dv = d_ref[0, h]
            r0 = dv.astype(bf16)
            e1 = dv - r0.astype(f32)
            r1 = e1.astype(bf16)
            r2 = (e1 - r1.astype(f32)).astype(bf16)
            rr = _dot(rv, r0) + _dot(rv, r1) + _dot(rv, r2)
            skew = pltpu.roll(rr, 0, 1, stride=1, stride_axis=0)
            o_ref[0, h:h + 1, :] = jnp.sum(skew, axis=0, keepdims=True)

    return pl.pallas_call(
        body, name="rpb_reduce", grid=(8,),
        in_specs=[pl.BlockSpec((1, NH, GW, WR * GW), lambda p: (p, 0, 0, 0)), _full((GW, GW))],
        out_specs=pl.BlockSpec((1, NH, WR * GW), lambda p: (p, 0, 0)),
        out_shape=jax.ShapeDtypeStruct((8, NH, WR * GW), f32),
        compiler_params=_cp("parallel"),
    )(dbias, jnp.asarray(rev, dtype=bf16))


def _norm1_bwd(dq, dk, dv, da, dg, w_inT, x0, ctx0, h, dx1, modv, g1):
    tm = 256
    nt = TA // tm
    nx = T // tm

    def body(dq_ref, dk_ref, dv_ref, da_ref, dg_ref, w_ref, x_ref, c_ref, h_ref, dx1_ref, mod_ref, g_ref,
             dx_ref, dwo_ref, s_ref, dw_ref):
        i = pl.program_id(0)
        is_ctx = i == nt - 1

        @pl.when(i == 0)
        def _():
            s_ref[...] = jnp.zeros_like(s_ref)
            dw_ref[...] = jnp.zeros_like(dw_ref)

        hb = h_ref[...]
        dkb, dvb = dk_ref[...].astype(bf16), dv_ref[...].astype(bf16)
        dw_ref[DA:2 * DA, :] += _dg(dkb, hb, TN)
        dw_ref[2 * DA:3 * DA, :] += _dg(dvb, hb, TN)
        dh_kv = _dot(dkb, w_ref[DA:2 * DA, :]) + _dot(dvb, w_ref[2 * DA:3 * DA, :])
        gain = g_ref[...]

        @pl.when(is_ctx)
        def _():
            xv = c_ref[...]
            rstd = lax.rsqrt(jnp.mean(xv * xv, axis=-1, keepdims=True) + EPS)
            _, dsh, dsc, dgn = _norm_bwd(dh_kv, xv, gain, mod_ref[7:8, :], rstd)
            s_ref[2:3, :] += dgn
            s_ref[3:4, :] += dsh
            s_ref[4:5, :] += dsc
            dwo_ref[...] = dw_ref[...].astype(bf16)

        @pl.when(jnp.logical_not(is_ctx))
        def _():
            dqb, dab, dgb = dq_ref[...], da_ref[...], dg_ref[...]
            dw_ref[0:DA, :] += _dg(dqb, hb, TN)
            dw_ref[3 * DA:4 * DA, :] += _dg(dab, hb, TN)
            dw_ref[4 * DA:5 * DA, :] += _dg(dgb, hb, TN)
            dh = (dh_kv + _dot(dqb, w_ref[0:DA, :]) + _dot(dab, w_ref[3 * DA:4 * DA, :])
                  + _dot(dgb, w_ref[4 * DA:5 * DA, :]))
            xv = x_ref[...]
            rstd = lax.rsqrt(jnp.mean(xv * xv, axis=-1, keepdims=True) + EPS)
            dxn, dsh, dsc, dgn = _norm_bwd(dh, xv, gain, mod_ref[1:2, :], rstd)
            dx_ref[...] = dx1_ref[...] + dxn
            s_ref[0:1, :] += dsh
            s_ref[1:2, :] += dsc
            s_ref[2:3, :] += dgn

    row = lambda w: pl.BlockSpec((tm, w), lambda i: (i, 0))
    lrow = lambda w: pl.BlockSpec((tm, w), lambda i: (jnp.minimum(i, nx - 1), 0))
    return pl.pallas_call(
        body, name="norm1_bwd", grid=(nt,),
        in_specs=[lrow(DA), row(DA), row(DA), lrow(DA), lrow(DA), _full((5 * DA, D)), lrow(D), _full((TC, D)), row(D),
                  lrow(D), _full((8, D)), _full((1, D))],
        out_specs=[lrow(D), _full((5 * DA, D)), _full((8, D))],
        out_shape=[jax.ShapeDtypeStruct((T, D), f32), jax.ShapeDtypeStruct((5 * DA, D), bf16), jax.ShapeDtypeStruct((8, D), f32)],
        scratch_shapes=[pltpu.VMEM((5 * DA, D), f32)],
        compiler_params=_cp("arbitrary"),
    )(dq, dk, dv, da, dg, w_inT, x0, ctx0, h, dx1, modv, g1)


def _adam_tile(r, c):
    return max(t for t in range(8, r + 1, 8) if r % t == 0 and t * c * 4 <= 2 * 1024 * 1024)


def _adam_update(wv, gv, mv, vv):
    nm = ADAM_B1 * mv + (1.0 - ADAM_B1) * gv
    nv = ADAM_B2 * vv + (1.0 - ADAM_B2) * (gv * gv)
    m_hat = nm * (1.0 / (1.0 - ADAM_B1 ** ADAM_STEP))
    v_hat = nv * (1.0 / (1.0 - ADAM_B2 ** ADAM_STEP))
    return -ADAM_LR * (m_hat / (jnp.sqrt(v_hat) + ADAM_EPS) + ADAM_WD * wv), nm, nv


def _adamw(w, g, m, v, name):
    r, c = w.shape
    tr = _adam_tile(r, c)

    def body(w_ref, g_ref, m_ref, v_ref, d_ref, nm_ref, nv_ref):
        d_ref[...], nm_ref[...], nv_ref[...] = _adam_update(w_ref[...], g_ref[...], m_ref[...], v_ref[...])

    spec = pl.BlockSpec((tr, c), lambda i: (i, 0))
    return pl.pallas_call(
        body, name=name, grid=(r // tr,),
        in_specs=[spec] * 4, out_specs=[spec] * 3,
        out_shape=[jax.ShapeDtypeStruct((r, c), f32)] * 3,
        compiler_params=_cp("parallel"),
    )(w, g, m, v)


def _adamw_blocks(w, recv, m, v, name):
    r, c = w.shape
    tr = _adam_tile(r, c)

    def body(w_ref, a_ref, m_ref, v_ref, g_ref, d_ref, nm_ref, nv_ref):
        gv = a_ref[0].astype(f32)
        for d in range(1, NDEV):
            gv = gv + a_ref[d].astype(f32)
        g_ref[...] = gv
        d_ref[...], nm_ref[...], nv_ref[...] = _adam_update(w_ref[...], gv, m_ref[...], v_ref[...])

    spec = pl.BlockSpec((tr, c), lambda i: (i, 0))
    return pl.pallas_call(
        body, name=name, grid=(r // tr,),
        in_specs=[spec, pl.BlockSpec((NDEV, tr, c), lambda i: (0, i, 0)), spec, spec], out_specs=[spec] * 4,
        out_shape=[jax.ShapeDtypeStruct((r, c), f32)] * 4,
        compiler_params=_cp("parallel"),
    )(w, recv, m, v)


def _pad_rows128(vec):
    n = vec.shape[0]
    rows = -(-n // 1024) * 8
    return jnp.pad(vec, (0, rows * 128 - n)).reshape(rows, 128)


def _grad_rpb(dbias):
    lane_map, r_hot = _rpb_tables()
    return jnp.einsum("phl,lic,pir->hrc", _rpb_reduce(dbias), jnp.asarray(lane_map), jnp.asarray(r_hot),
                      precision=lax.Precision.HIGHEST)


def kernel(x, c, ctx, c_ctx, w_mod, b_mod, g_norm1, w_in, rpb, conv_w, conv_b, ln_g, ln_b, w_out, g_norm2, w_up, ffn_conv_w, ffn_conv_b, w_down, g_final, loss_target, m_c_ctx, m_w_mod, m_b_mod, m_g_norm1, m_w_in, m_rpb, m_conv_w, m_conv_b, m_ln_g, m_ln_b, m_w_out, m_g_norm2, m_w_up, m_ffn_conv_w, m_ffn_conv_b, m_w_down, m_g_final, v_c_ctx, v_w_mod, v_b_mod, v_g_norm1, v_w_in, v_rpb, v_conv_w, v_conv_b, v_ln_g, v_ln_b, v_w_out, v_g_norm2, v_w_up, v_ffn_conv_w, v_ffn_conv_b, v_w_down, v_g_final):
    me = 4 * lax.axis_index("x") + 2 * lax.axis_index("y") + lax.axis_index("c")
    nmod = w_mod.shape[2]
    n_in = w_in.shape[2]
    n_out = w_out.shape[1]
    n_up = w_up.shape[2]
    n_dn = w_down.shape[1]
    n_cw = conv_w.shape[2]

    b_sh = lax.dynamic_slice(b_mod, (0, me * nmod), (1, nmod))
    convpay = _pad_rows128(jnp.concatenate([conv_w[0].reshape(-1), ffn_conv_w[0].reshape(-1)]))
    w_inT, c_all, mod_all, flat, bias_tab = _head(w_in[0].T.astype(bf16), c.reshape(8, 128), c_ctx.reshape(8, 128),
                                                  w_mod[0], b_sh, convpay, *_bias_table_inputs(rpb[0]))
    cvec = jnp.concatenate([c_all.reshape(NDEV, D), c_ctx[None, :], jnp.zeros((7, D), f32)], axis=0)
    mod_all = mod_all.transpose(1, 0, 2).reshape(16, 6 * D)
    mod_me = lax.dynamic_index_in_dim(mod_all, me, 0, keepdims=False).reshape(6, D)
    mod_c = mod_all[8]
    modv = jnp.concatenate([mod_me, mod_c[None, 0:D], mod_c[None, D:2 * D]], axis=0)
    flat = flat.reshape(NDEV, -1)
    o1 = CW * n_cw
    conv_w_f = flat[:, :o1].reshape(NDEV, CW, n_cw).transpose(1, 0, 2).reshape(CW, DA)
    fcw_f = flat[:, o1:o1 + 3 * n_up].reshape(NDEV, 3, n_up).transpose(1, 0, 2).reshape(3, F2)

    x0, ctx0 = x[0], ctx[0]
    h, q, k, v, a, g, w_down_f = _in_proj(x0, ctx0, g_norm1, modv, w_inT, [w_down[0].astype(bf16)])
    y_cv, cv, w_out_f = _conf_fwd(a, g, conv_w_f, conv_b, ln_g, ln_b, [w_out[0].astype(bf16)])
    y_na, lse, w_upT = _attn_fwd(q, k, v, bias_tab, [w_up[0].T.astype(bf16)])
    x1, proj, h2 = _out_proj(x0, y_na, y_cv, w_out_f, modv, g_norm2)
    ffn, u_t, u2_t = _ffn_fwd(h2, w_upT, fcw_f, ffn_conv_b, w_down_f)
    dx2, dffn, s_loss = _loss_bwd(ffn, x1, loss_target[0], modv, g_final[None, :])

    dug, duv, dw_up, dw_down, dcwg, dcwv, dcbg, dcbv = _ffn_bwd(h2, dffn, u_t, u2_t, fcw_f, w_down_f)
    dx1, dy_na, dy_cv, dw_out, s_n2 = _norm2_bwd(dug, duv, w_upT, x1, dx2, proj, w_out_f, y_na, y_cv, modv, g_norm2)
    da, dg, dcw, s_cf, rv_down = _conf_bwd(a, g, cv, dy_cv, conv_w_f, ln_g, ln_b, [dw_down.reshape(NDEV, n_dn, D)])
    dq, dk, dv, dbias, rv_up = _attn_bwd(q, k, v, y_na, dy_na, lse, bias_tab, [dw_up.reshape(NDEV, n_up, D)])
    grad_rpb_part = _grad_rpb(dbias)
    grad_x, dw_inT, s_n1 = _norm1_bwd(dq, dk, dv, da, dg, w_inT, x0, ctx0, h, dx1, modv, g_norm1)
    grad_x = grad_x[None]
    dfcw = jnp.concatenate([dcwg, dcwv], axis=1)
    dfcb = jnp.concatenate([dcbg[0], dcbv[0]])
    small = jnp.concatenate([dcw[:CW].reshape(CW, NDEV, n_cw).transpose(1, 0, 2).reshape(NDEV, CW * n_cw),
                             dfcw.reshape(3, NDEV, n_up).transpose(1, 0, 2).reshape(NDEV, 3 * n_up)], axis=1)
    n_sm = small.shape[1] // D
    pad_sm = -n_sm % 16
    small = jnp.pad(small.reshape(NDEV, n_sm, D), ((0, 0), (0, pad_sm), (0, 0))).astype(bf16)
    dmod = jnp.concatenate([s_n1[0], s_n1[1], s_n2[3], s_n2[0], s_n2[1], s_loss[1]])
    dmodc = jnp.concatenate([s_n1[3], s_n1[4]])
    parts = [dmodc, s_n1[2], grad_rpb_part.reshape(-1), s_cf[0], s_cf[1], s_cf[2], s_n2[2], dfcb, s_loss[0], s_loss[3, 0:1]]
    sizes = [p.shape[0] for p in parts]
    pvec = _pad_rows128(jnp.concatenate([dmod] + parts))
    slab = jnp.concatenate([dw_inT.reshape(NDEV, n_in, D), small, dw_out.reshape(NDEV, n_out, D)], axis=1)
    r_a, gath = _reduce_scatter2(slab, [pvec], "rs_w_in")
    g_w_out = r_a[n_in + n_sm + pad_sm:]
    g_w_inT = r_a[:n_in]
    sm = r_a[n_in:n_in + n_sm].reshape(-1)
    g_conv_w = sm[:CW * n_cw].reshape(CW, n_cw)
    g_fcw = sm[CW * n_cw:].reshape(3, n_up)
    tot = _sum_rows8(gath, "sum_small").reshape(-1)
    dmod_all = gath.reshape(NDEV, -1)[:, :6 * D]
    offs = np.cumsum([6 * D] + sizes)
    pick = lambda j: tot[offs[j]:offs[j + 1]]
    dmodc_t = jnp.pad(pick(0), (0, 4 * D))
    g_b_mod = (tot[:6 * D] + dmodc_t)[None, :]
    g_g_norm1 = pick(1)[None, :]
    g_rpb = pick(2).reshape(1, NH, 2 * WR - 1, 2 * NCOL - 1)
    g_conv_b, g_ln_g, g_ln_b = pick(3)[None, :], pick(4)[None, :], pick(5)[None, :]
    g_g_norm2 = pick(6)[None, :]
    g_fcb = pick(7)[None, :]
    g_g_final = pick(8)
    loss = pick(9)[0]
    dm_rows = jnp.concatenate([dmod_all, dmodc_t[None, :], jnp.zeros((7, 6 * D), f32)], axis=0)
    dm_sh = lax.dynamic_slice(dm_rows, (0, me * nmod), (16, nmod))
    g_w_mod, gc_part = _mod_bwd(cvec, dm_sh, w_mod[0])
    gc_sum = _sum_rows8(_small_allgather(gc_part[0].reshape(8, 128), "ag_cctx"), "sum_cctx").reshape(D)
    sg_c = _sigmoid(c_ctx)
    g_c_ctx = gc_sum * (sg_c * (1.0 + c_ctx * (1.0 - sg_c)))

    big = [("w_mod", w_mod[0], g_w_mod, m_w_mod[0], v_w_mod[0]), ("w_in", w_in[0].T, g_w_inT, m_w_in[0].T, v_w_in[0].T),
           ("w_out", w_out[0], g_w_out, m_w_out[0], v_w_out[0])]
    upd = {n: _adamw(wv, gv, mv, vv, "adamw_" + n) for n, wv, gv, mv, vv in big}
    g_w_upT, *upd["w_up"] = _adamw_blocks(w_up[0].T, rv_up, m_w_up[0].T, v_w_up[0].T, "adamw_w_up")
    g_w_down, *upd["w_down"] = _adamw_blocks(w_down[0], rv_down, m_w_down[0], v_w_down[0], "adamw_w_down")
    for n in ("w_in", "w_up"):
        upd[n] = tuple(arr.T for arr in upd[n])
    g_w_in, g_w_up = g_w_inT.T, g_w_upT.T
    smalls = [("c_ctx", c_ctx, g_c_ctx, m_c_ctx, v_c_ctx), ("b_mod", b_mod, g_b_mod, m_b_mod, v_b_mod),
              ("g_norm1", g_norm1, g_g_norm1, m_g_norm1, v_g_norm1), ("rpb", rpb, g_rpb, m_rpb, v_rpb),
              ("conv_w", conv_w, g_conv_w[None], m_conv_w, v_conv_w), ("conv_b", conv_b, g_conv_b, m_conv_b, v_conv_b),
              ("ln_g", ln_g, g_ln_g, m_ln_g, v_ln_g), ("ln_b", ln_b, g_ln_b, m_ln_b, v_ln_b),
              ("g_norm2", g_norm2, g_g_norm2, m_g_norm2, v_g_norm2),
              ("ffn_conv_w", ffn_conv_w, g_fcw[None], m_ffn_conv_w, v_ffn_conv_w),
              ("ffn_conv_b", ffn_conv_b, g_fcb, m_ffn_conv_b, v_ffn_conv_b), ("g_final", g_final, g_g_final, m_g_final, v_g_final)]
    packed = [_pad_rows128(jnp.concatenate([t[j].reshape(-1) for t in smalls])) for j in (1, 2, 3, 4)]
    sd, sm_, sv = _adamw(*packed, "adamw_small")
    so = np.cumsum([0] + [int(np.prod(t[1].shape)) for t in smalls])
    for j, t in enumerate(smalls):
        shp = t[1].shape
        upd[t[0]] = tuple(arr.reshape(-1)[so[j]:so[j + 1]].reshape(shp) for arr in (sd, sm_, sv))
    grads = {"c_ctx": g_c_ctx, "w_mod": g_w_mod[None], "b_mod": g_b_mod, "g_norm1": g_g_norm1, "w_in": g_w_in[None],
             "rpb": g_rpb, "conv_w": g_conv_w[None], "conv_b": g_conv_b, "ln_g": g_ln_g, "ln_b": g_ln_b,
             "w_out": g_w_out[None], "g_norm2": g_g_norm2, "w_up": g_w_up[None], "ffn_conv_w": g_fcw[None],
             "ffn_conv_b": g_fcb, "w_down": g_w_down[None], "g_final": g_g_final}
    names = ["c_ctx", "w_mod", "b_mod", "g_norm1", "w_in", "rpb", "conv_w", "conv_b", "ln_g", "ln_b", "w_out", "g_norm2",
             "w_up", "ffn_conv_w", "ffn_conv_b", "w_down", "g_final"]
    shapes = {n: grads[n].shape for n in names}
    outs = [loss, grad_x] + [grads[n] for n in names]
    for j in range(3):
        outs += [upd[n][j].reshape(shapes[n]) for n in names]
    return tuple(outs)
```

```python
import numpy as np
import jax
import jax.numpy as jnp
from jax import lax
from jax.experimental import pallas as pl
from jax.experimental.pallas import tpu as pltpu

f32 = jnp.float32
bf16 = jnp.bfloat16

D = 1024
T = 4096
TC = 256
TA = T + TC
DA = 512
NH = 8
HD = 64
GW = 64
WR = 8
NCOL = 16
F = 2816
F2 = 2 * F
CW = 31
NDEV = 8
EPS = 1e-6
SCALE = HD ** -0.5
NEG = -1e30
MESH = pl.DeviceIdType.MESH

NT = (((1,), (1,)), ((), ()))
TN = (((0,), (0,)), ((), ()))

ADAM_LR, ADAM_B1, ADAM_B2, ADAM_EPS, ADAM_WD, ADAM_STEP = 0.001, 0.9, 0.999, 1e-08, 0.01, 10

VMEM_LIMIT = 56 * 1024 * 1024


def _cp(*sem):
    return pltpu.CompilerParams(dimension_semantics=sem or None, vmem_limit_bytes=VMEM_LIMIT)


def _dot(a, b):
    return jnp.dot(a, b, preferred_element_type=f32)


def _dg(a, b, dims):
    return lax.dot_general(a, b, dims, preferred_element_type=f32)


def _sigmoid(x):
    return 1.0 / (1.0 + jnp.exp(-x))


def _full(shape):
    n = len(shape)
    return pl.BlockSpec(shape, lambda *_: (0,) * n)


def _resident(shape):
    n = len(shape)
    return pl.BlockSpec(shape, lambda *_: (0,) * n, pipeline_mode=pl.Buffered(1))


def _my_pos():
    return lax.axis_index("x"), lax.axis_index("y"), lax.axis_index("c")


def _small_gather_plan(v_ref, out_ref, send_sems, recv_sems):
    x, y, c = _my_pos()
    me = 4 * x + 2 * y + c
    peers = []
    for k in range(1, NDEV):
        kx, ky, kc = (k >> 2) & 1, (k >> 1) & 1, k & 1
        peers.append((x ^ kx, y ^ ky, c ^ kc))

    def copy(k, slot, to):
        return pltpu.make_async_remote_copy(
            src_ref=v_ref, dst_ref=out_ref.at[slot], send_sem=send_sems.at[k], recv_sem=recv_sems.at[k],
            device_id=to, device_id_type=MESH)

    def start():
        out_ref[me] = v_ref[...]
        for k, p in enumerate(peers):
            copy(k, me, p).start()

    def finish():
        for k, (px, py, pc) in enumerate(peers):
            copy(k, 4 * px + 2 * py + pc, (x, y, c)).wait_recv()
        for k, p in enumerate(peers):
            copy(k, me, p).wait_send()

    return start, finish


def _small_gather_scratch():
    return [pltpu.SemaphoreType.DMA((NDEV - 1,)), pltpu.SemaphoreType.DMA((NDEV - 1,))]


def _small_allgather(v, name):
    n = v.shape[0]

    def body(v_ref, out_ref, send_sems, recv_sems):
        start, finish = _small_gather_plan(v_ref, out_ref, send_sems, recv_sems)
        start()
        finish()

    return pl.pallas_call(
        body, name=name,
        out_shape=jax.ShapeDtypeStruct((NDEV, n, 128), f32),
        in_specs=[pl.BlockSpec(memory_space=pltpu.VMEM)],
        out_specs=pl.BlockSpec(memory_space=pltpu.VMEM),
        scratch_shapes=_small_gather_scratch(),
    )(v)


def _ag2_plan(x_refs, out_refs, send_sems, recv_sems, local_sems):
    na = len(x_refs)
    x, y, c = _my_pos()
    me, sibling = (x, y, c), (x, y, 1 - c)
    chips = [(1 - x, y), (x, 1 - y), (1 - x, 1 - y)]

    def rows(i, px, py, pc):
        m_per = x_refs[i].shape[0]
        return out_refs[i].at[pl.ds(pl.multiple_of((4 * px + 2 * py + pc) * m_per, 16 if m_per % 16 == 0 else 8), m_per), :]

    def copies(k, block, to, from_shard=False):
        return [pltpu.make_async_remote_copy(
            src_ref=x_refs[i] if from_shard else rows(i, *block), dst_ref=rows(i, *block),
            send_sem=send_sems.at[k * na + i], recv_sem=recv_sems.at[k * na + i], device_id=to, device_id_type=MESH)
            for i in range(na)]

    def mine():
        return [pltpu.make_async_copy(x_refs[i], rows(i, *me), local_sems.at[i]) for i in range(na)]

    def first():
        cps = copies(0, me, sibling, True)
        for j, chip in enumerate(chips):
            cps += copies(1 + j, me, (*chip, c), True)
        return cps

    def start():
        for cp in mine() + first():
            cp.start()

    def forward():
        for j, chip in enumerate(chips):
            for cp in copies(1 + j, (*chip, c), me):
                cp.wait_recv()
            for cp in copies(4 + j, (*chip, c), sibling):
                cp.start()

    def finish():
        for cp in copies(0, sibling, me):
            cp.wait_recv()
        for j, chip in enumerate(chips):
            for cp in copies(4 + j, (*chip, 1 - c), me):
                cp.wait_recv()
        for cp in first():
            cp.wait_send()
        for j, chip in enumerate(chips):
            for cp in copies(4 + j, (*chip, c), sibling):
                cp.wait_send()
        for cp in mine():
            cp.wait()

    return start, forward, finish


def _ag2_scratch(na):
    return [pltpu.SemaphoreType.DMA((7 * na,)), pltpu.SemaphoreType.DMA((7 * na,)), pltpu.SemaphoreType.DMA((na,))]


def _a2a_plan(g_ref, recv_ref, send_sems, recv_sems, local_sem):
    x, y, c = _my_pos()
    me = 4 * x + 2 * y + c
    peers = []
    for k in range(1, NDEV):
        kx, ky, kc = (k >> 2) & 1, (k >> 1) & 1, k & 1
        peers.append((x ^ kx, y ^ ky, c ^ kc))

    def sends():
        return [pltpu.make_async_remote_copy(
            src_ref=g_ref.at[4 * px + 2 * py + pc], dst_ref=recv_ref.at[me], send_sem=send_sems.at[k], recv_sem=recv_sems.at[k],
            device_id=(px, py, pc), device_id_type=MESH) for k, (px, py, pc) in enumerate(peers)]

    def own():
        return pltpu.make_async_copy(g_ref.at[me], recv_ref.at[me], local_sem)

    def start():
        own().start()
        for cp in sends():
            cp.start()

    def finish():
        for k, (px, py, pc) in enumerate(peers):
            pltpu.make_async_remote_copy(
                src_ref=g_ref.at[me], dst_ref=recv_ref.at[4 * px + 2 * py + pc], send_sem=send_sems.at[k],
                recv_sem=recv_sems.at[k], device_id=(x, y, c), device_id_type=MESH).wait_recv()
        for cp in sends():
            cp.wait_send()
        own().wait()

    return start, finish


def _a2a_scratch():
    return [pltpu.SemaphoreType.DMA((NDEV - 1,)), pltpu.SemaphoreType.DMA((NDEV - 1,)), pltpu.SemaphoreType.DMA]


def _reduce_scatter2(g, small, name):
    _, r, n = g.shape
    ch = 16
    nch = r // ch
    ns = len(small)

    def body(g_ref, *rest):
        v_refs, out_ref, vout_refs = rest[:ns], rest[ns], rest[ns + 1:2 * ns + 1]
        a_ref, h_ref, b_ref, s1_send, s1_recv, s2_send, s2_recv = rest[2 * ns + 1:2 * ns + 8]
        gather_sems = rest[2 * ns + 8:]
        gathers = [_small_gather_plan(v_refs[i], vout_refs[i], *gather_sems[2 * i:2 * i + 2]) for i in range(ns)]
        for start, _ in gathers:
            start()
        x, y, c = _my_pos()
        sibling = (x, y, 1 - c)
        s1 = []
        for j in range(4):
            cp = pltpu.make_async_remote_copy(
                src_ref=g_ref.at[2 * j + (1 - c)], dst_ref=a_ref.at[j], send_sem=s1_send.at[j], recv_sem=s1_recv.at[j],
                device_id=sibling, device_id_type=MESH)
            cp.start()
            s1.append(cp)
        for cp in s1:
            cp.wait_recv()

        def add1(i, _):
            rr = pl.ds(pl.multiple_of(i * ch, ch), ch)
            for j in range(4):
                h_ref[j, rr, :] = (g_ref[2 * j + c, rr, :].astype(f32) + a_ref[j, rr, :].astype(f32)).astype(bf16)
            return 0
        lax.fori_loop(0, nch, add1, 0)
        mychip = 2 * x + y
        s2 = []
        for m in range(1, 4):
            mx, my_ = (m >> 1) & 1, m & 1
            px, py = x ^ mx, y ^ my_
            cp = pltpu.make_async_remote_copy(
                src_ref=h_ref.at[2 * px + py], dst_ref=b_ref.at[m - 1], send_sem=s2_send.at[m - 1], recv_sem=s2_recv.at[m - 1],
                device_id=(px, py, c), device_id_type=MESH)
            cp.start()
            s2.append(cp)
        for cp in s2:
            cp.wait_recv()

        def add2(i, _):
            rr = pl.ds(pl.multiple_of(i * ch, ch), ch)
            acc = h_ref[mychip, rr, :].astype(f32)
            for m in range(3):
                acc = acc + b_ref[m, rr, :].astype(f32)
            out_ref[rr, :] = acc
            return 0
        lax.fori_loop(0, nch, add2, 0)
        for cp in s1 + s2:
            cp.wait_send()
        for _, finish in gathers:
            finish()

    vmem = pl.BlockSpec(memory_space=pltpu.VMEM)
    return pl.pallas_call(
        body, name=name,
        out_shape=[jax.ShapeDtypeStruct((r, n), f32)] + [jax.ShapeDtypeStruct((NDEV,) + v.shape, v.dtype) for v in small],
        in_specs=[vmem] * (1 + ns),
        out_specs=[vmem] * (1 + ns),
        scratch_shapes=[pltpu.VMEM((4, r, n), bf16), pltpu.VMEM((4, r, n), bf16), pltpu.VMEM((3, r, n), bf16),
                        pltpu.SemaphoreType.DMA((4,)), pltpu.SemaphoreType.DMA((4,)),
                        pltpu.SemaphoreType.DMA((3,)), pltpu.SemaphoreType.DMA((3,))] + _small_gather_scratch() * ns,
        compiler_params=pltpu.CompilerParams(vmem_limit_bytes=VMEM_LIMIT),
    )(g, *small)


def _head(w_in_sh, c8, cctx8, w_mod_sh, b_sh, convpay, bias_base, bias_valid):
    nmod = w_mod_sh.shape[1]
    npay = convpay.shape[0]

    def body(w_ref, c_ref, cc_ref, wm_ref, b_ref, pay_ref, base_ref, valid_ref,
             wout_ref, call_ref, mall_ref, pall_ref, bias_ref, mod_s,
             ag_send, ag_recv, ag_local, c_send, c_recv, m_send, m_recv, p_send, p_recv):
        start, forward, finish = _ag2_plan([w_ref], [wout_ref], ag_send, ag_recv, ag_local)
        c_start, c_finish = _small_gather_plan(c_ref, call_ref, c_send, c_recv)
        m_start, m_finish = _small_gather_plan(mod_s, mall_ref, m_send, m_recv)
        p_start, p_finish = _small_gather_plan(pay_ref, pall_ref, p_send, p_recv)
        c_start()
        start()
        p_start()
        _write_bias_table(base_ref, valid_ref, bias_ref)
        c_finish()
        acc = jnp.zeros((16, nmod), f32)
        for j in range(D // 128):
            rows = jnp.concatenate([call_ref[:, j, :], cc_ref[j:j + 1, :], jnp.zeros((7, 128), f32)], axis=0)
            act = (rows * _sigmoid(rows)).astype(bf16)
            acc = acc + _dot(act, wm_ref[j * 128:(j + 1) * 128, :].astype(bf16))
        mod_s[...] = acc + b_ref[...]
        m_start()
        forward()
        finish()
        m_finish()
        p_finish()

    vmem = pl.BlockSpec(memory_space=pltpu.VMEM)
    return pl.pallas_call(
        body, name="head",
        out_shape=[jax.ShapeDtypeStruct((NDEV * w_in_sh.shape[0], D), bf16), jax.ShapeDtypeStruct((NDEV, 8, 128), f32),
                   jax.ShapeDtypeStruct((NDEV, 16, nmod), f32), jax.ShapeDtypeStruct((NDEV, npay, 128), f32),
                   jax.ShapeDtypeStruct((8, NH, GW, WR * GW), f32)],
        in_specs=[vmem] * 8, out_specs=[vmem] * 5,
        scratch_shapes=[pltpu.VMEM((16, nmod), f32)] + _ag2_scratch(1) + _small_gather_scratch() * 3,
        compiler_params=pltpu.CompilerParams(vmem_limit_bytes=VMEM_LIMIT),
    )(w_in_sh, c8, cctx8, w_mod_sh, b_sh, convpay, bias_base, bias_valid)


def _mod_bwd(cvec, dm_sh, w_sh):
    def body(c_ref, dm_ref, w_ref, gw_ref, gc_ref):
        cv = c_ref[...]
        act = (cv * _sigmoid(cv)).astype(bf16)
        gw_ref[...] = _dg(act, dm_ref[...].astype(bf16), TN)
        gc_ref[...] = _dg(dm_ref[8:16, :].astype(bf16), w_ref[...].astype(bf16), NT)
    return pl.pallas_call(
        body, name="mod_bwd",
        out_shape=(jax.ShapeDtypeStruct(w_sh.shape, f32), jax.ShapeDtypeStruct((8, D), f32)))(cvec, dm_sh, w_sh)


def _sum_rows8(a, name):
    n = a.shape[1]

    def body(a_ref, o_ref):
        acc = a_ref[0]
        for d in range(1, NDEV):
            acc = acc + a_ref[d]
        o_ref[...] = acc
    return pl.pallas_call(body, name=name, out_shape=jax.ShapeDtypeStruct((n, 128), f32))(a)


def _in_proj(x0, ctx0, g1, modv, w_inT, shards):
    tm = 256
    nt = TA // tm
    nx = T // tm
    na = len(shards)

    def body(x_ref, c_ref, g_ref, mod_ref, w_ref, *rest):
        x_refs, (h_ref, q_ref, k_ref, v_ref, a_ref, gg_ref) = rest[:na], rest[na:na + 6]
        out_refs, sems = rest[na + 6:2 * na + 6], rest[2 * na + 6:]
        i = pl.program_id(0)
        if na:
            start, forward, finish = _ag2_plan(x_refs, out_refs, *sems)
            pl.when(i == 0)(start)
            pl.when(i == nt - 2)(forward)
        is_ctx = i == nt - 1
        xv = jnp.where(is_ctx, c_ref[...], x_ref[...])
        rstd = lax.rsqrt(jnp.mean(xv * xv, axis=-1, keepdims=True) + EPS)
        sh = jnp.where(is_ctx, mod_ref[6:7, :], mod_ref[0:1, :])
        sc = jnp.where(is_ctx, mod_ref[7:8, :], mod_ref[1:2, :])
        h = ((xv * rstd * g_ref[...]) * (1.0 + sc) + sh).astype(bf16)
        h_ref[...] = h
        for j, o_ref in enumerate((q_ref, k_ref, v_ref, a_ref, gg_ref)):
            o_ref[...] = _dg(h, w_ref[j * DA:(j + 1) * DA, :], NT).astype(o_ref.dtype)
        if na:
            pl.when(is_ctx)(finish)

    row = lambda w: pl.BlockSpec((tm, w), lambda i: (i, 0))
    hbm = pl.BlockSpec(memory_space=pl.ANY)
    return pl.pallas_call(
        body, name="in_proj", grid=(nt,),
        in_specs=[pl.BlockSpec((tm, D), lambda i: (jnp.minimum(i, nx - 1), 0)), _full((TC, D)),
                  _full((1, D)), _full((8, D)), _full((5 * DA, D))] + [hbm] * na,
        out_specs=[row(D), row(DA), row(DA), row(DA), row(DA), row(DA)] + [hbm] * na,
        out_shape=[jax.ShapeDtypeStruct((TA, D), bf16)] + [jax.ShapeDtypeStruct((TA, DA), bf16)] * 3
                  + [jax.ShapeDtypeStruct((TA, DA), f32)] * 2
                  + [jax.ShapeDtypeStruct((NDEV * sh.shape[0], sh.shape[1]), sh.dtype) for sh in shards],
        scratch_shapes=_ag2_scratch(na) if na else [],
        compiler_params=_cp("arbitrary"),
    )(x0, ctx0, g1, modv, w_inT, *shards)


def _win_start(r):
    return jnp.clip(r - WR // 2, 0, GW - WR)


def _pattern(r):
    return _win_start(r) - r + (WR - 1)


def _bias_table_inputs(rpb):
    qc = np.arange(GW)[:, None]
    kc = np.arange(GW)[None, :]
    cs = np.clip(qc - NCOL // 2, 0, GW - NCOL)
    valid = np.tile(((kc >= cs) & (kc < cs + NCOL)).astype(np.int32), (1, WR))
    pad = jnp.pad(rpb, ((0, 0), (0, 0), (0, GW - (2 * NCOL - 1))))
    base = jnp.stack([pad[:, p:p + WR, :].reshape(NH, WR * GW) for p in range(8)])
    return base, jnp.asarray(valid)


def _write_bias_table(base_ref, valid_ref, o_ref):
    ok = valid_ref[...] != 0
    for p in range(8):
        for h in range(NH):
            row = jnp.broadcast_to(base_ref[p, h:h + 1, :], (GW, WR * GW))
            skew = pltpu.roll(row, WR * GW - (NCOL - 1), 1, stride=1, stride_axis=0)
            o_ref[p, h] = jnp.where(ok, skew, NEG)


def _rpb_tables():
    lane_map = np.zeros((WR * GW, WR, 2 * NCOL - 1), np.float32)
    for i in range(WR):
        for t in range(GW):
            if t >= GW - NCOL:
                lane_map[i * GW + t, i, t - (GW - NCOL)] = 1.0
            elif t < NCOL - 1:
                lane_map[i * GW + t, (i - 1) % WR, t + NCOL] = 1.0
    p = np.arange(8)[:, None]
    i = np.arange(WR)[None, :]
    r_hot = ((p + i)[:, :, None] == np.arange(2 * WR - 1)[None, None, :]).astype(np.float32)
    return lane_map, r_hot


CONV_TT = 512
HALO = 16


def _halo_specs(tt, w, nrows_blocks):
    per = tt // HALO
    prev = pl.BlockSpec((HALO, w), lambda i: (jnp.maximum(i * per - 1, 0), 0))
    cur = pl.BlockSpec((tt, w), lambda i: (i, 0))
    nxt = pl.BlockSpec((HALO, w), lambda i: (jnp.minimum((i + 1) * per, nrows_blocks - 1), 0))
    return [prev, cur, nxt]


def _shifted_copies(rot, wn):
    for b in range(1, 8):
        rot[b, 0:wn - 8, :] = rot[0, pl.ds(b, wn - 8), :]


ATTN_RPS = 4
MIX_FORWARD_STEP = 14


def _stack_pair(x2, lo):
    z = jnp.zeros_like(x2)
    return jnp.concatenate([jnp.where(lo, x2, z), jnp.where(lo, z, x2)], axis=0)


def _mix_fwd(q, k, v, bias_tab, a, g, conv_w, conv_b, ln_g, ln_b, shards):
    na = len(shards)
    rps = ATTN_RPS
    nsteps = GW // rps
    tt = CONV_TT
    nt = T // tt
    every = nsteps // nt
    sub = 32
    wn = tt + 2 * HALO

    def body(q_ref, k_ref, v_ref, *rest):
        b_refs, rest = rest[:rps], rest[rps:]
        (ap, ac, an, gp, gc, gn, w_ref, cb_ref, lg_ref, lb_ref), rest = rest[:10], rest[10:]
        x_refs, (y_ref, lse_ref, ycv_ref, cv_ref), out_refs = rest[:na], rest[na:na + 4], rest[na + 4:2 * na + 4]
        rot, *sems = rest[2 * na + 4:]
        i = pl.program_id(0)
        if na:
            start, forward, finish = _ag2_plan(x_refs, out_refs, *sems)
            pl.when(i == 0)(start)
            pl.when(i == MIX_FORWARD_STEP)(forward)

        @pl.when(i % every == 0)
        def _():
            t = i // every
            rot[0, 0:HALO, :] = jnp.where(t > 0, ap[...] * _sigmoid(gp[...]), 0.0)
            rot[0, HALO:HALO + tt, :] = ac[...] * _sigmoid(gc[...])
            rot[0, HALO + tt:, :] = jnp.where(t < nt - 1, an[...] * _sigmoid(gn[...]), 0.0)
            _shifted_copies(rot, wn)
            w = w_ref[...]
            for s in range(tt // sub):
                acc = jnp.zeros((sub, DA), f32)
                for j in range(CW):
                    a8, b8 = divmod(1 + j, 8)
                    acc = acc + rot[b8, pl.ds(s * sub + 8 * a8, sub), :] * w[j:j + 1, :]
                cv = acc + cb_ref[...]
                cv_ref[pl.ds(s * sub, sub), :] = cv
                mu = jnp.mean(cv, axis=-1, keepdims=True)
                xc = cv - mu
                rstd = lax.rsqrt(jnp.mean(xc * xc, axis=-1, keepdims=True) + EPS)
                z = xc * rstd * lg_ref[...] + lb_ref[...]
                ycv_ref[pl.ds(s * sub, sub), :] = (z * _sigmoid(z)).astype(bf16)

        lo = lax.broadcasted_iota(jnp.int32, (GW, 2 * HD), 1) < HD
        for rr in range(rps):
            rows = slice(rr * GW, (rr + 1) * GW)
            ks = pl.multiple_of(_win_start(i * rps + rr) * GW, GW)
            qq = q_ref[rows, :]
            kv, scores = [], []
            for pr in range(NH // 2):
                ps = slice(pr * 2 * HD, (pr + 1) * 2 * HD)
                qst = _stack_pair(qq[:, ps], lo)
                kw, kc = k_ref[pl.ds(ks, WR * GW), ps], k_ref[T:TA, ps]
                kv.append((v_ref[pl.ds(ks, WR * GW), ps], v_ref[T:TA, ps]))
                bias2 = b_refs[rr][0, 2 * pr:2 * pr + 2].reshape(2 * GW, WR * GW)
                scores.append((_dg(qst, kw, NT) * SCALE + bias2, _dg(qst, kc, NT) * SCALE))
            probs = []
            for pr, (sl, sc) in enumerate(scores):
                m = jnp.maximum(jnp.max(sl, axis=-1, keepdims=True), jnp.max(sc, axis=-1, keepdims=True))
                pl_ = jnp.exp(sl - m)
                pc = jnp.exp(sc - m)
                l = jnp.sum(pl_, axis=-1, keepdims=True) + jnp.sum(pc, axis=-1, keepdims=True)
                lse = m + jnp.log(l)
                lse_ref[rows, 2 * pr:2 * pr + 1] = lse[0:GW]
                lse_ref[rows, 2 * pr + 1:2 * pr + 2] = lse[GW:]
                probs.append((pl_.astype(bf16), pc.astype(bf16), 1.0 / l))
            for pr in range(NH // 2):
                ps = slice(pr * 2 * HD, (pr + 1) * 2 * HD)
                vw, vc = kv[pr]
                pb, cb, rl = probs[pr]
                o = (_dot(pb, vw) + _dot(cb, vc)) * rl
                y_ref[rows, ps] = jnp.where(lo, o[0:GW], o[GW:]).astype(bf16)
        if na:
            pl.when(i == nsteps - 1)(finish)

    hbm = pl.BlockSpec(memory_space=pl.ANY)
    rowq = lambda w: pl.BlockSpec((rps * GW, w), lambda i: (i, 0))
    bias = [pl.BlockSpec((1, NH, GW, WR * GW), lambda i, rr=rr: (_pattern(i * rps + rr), 0, 0, 0)) for rr in range(rps)]
    per = tt // HALO
    halo = lambda: [pl.BlockSpec((HALO, DA), lambda i: (jnp.maximum((i // every) * per - 1, 0), 0)),
                    pl.BlockSpec((tt, DA), lambda i: (i // every, 0)),
                    pl.BlockSpec((HALO, DA), lambda i: (jnp.minimum((i // every + 1) * per, T // HALO - 1), 0))]
    tile = lambda: pl.BlockSpec((tt, DA), lambda i: (i // every, 0))
    return pl.pallas_call(
        body, name="mix_fwd", grid=(nsteps,),
        in_specs=[rowq(DA), _full((TA, DA)), _full((TA, DA))] + bias + halo() + halo()
                 + [_full((CW, DA)), _full((1, DA)), _full((1, DA)), _full((1, DA))] + [hbm] * na,
        out_specs=[rowq(DA), rowq(NH), tile(), tile()] + [hbm] * na,
        out_shape=[jax.ShapeDtypeStruct((T, DA), bf16), jax.ShapeDtypeStruct((T, NH), f32),
                   jax.ShapeDtypeStruct((T, DA), bf16), jax.ShapeDtypeStruct((T, DA), f32)]
                  + [jax.ShapeDtypeStruct((NDEV * s.shape[0], s.shape[1]), s.dtype) for s in shards],
        scratch_shapes=[pltpu.VMEM((8, wn, DA), f32)] + (_ag2_scratch(na) if na else []),
        compiler_params=_cp("arbitrary"),
    )(q, k, v, *([bias_tab] * rps), a, a, a, g, g, g, conv_w, conv_b, ln_g, ln_b, *shards)


def _out_proj(xa, y_na, y_cv, w_out, modv, g2):
    tm = 512

    def body(x_ref, ya_ref, yc_ref, w_ref, mod_ref, g_ref, x1_ref, pj_ref, h2_ref):
        proj = _dot(ya_ref[...], w_ref[0:DA, :]) + _dot(yc_ref[...], w_ref[DA:D, :])
        x1 = x_ref[...] + mod_ref[2:3, :] * proj
        x1_ref[...] = x1
        pj_ref[...] = proj.astype(bf16)
        rstd = lax.rsqrt(jnp.mean(x1 * x1, axis=-1, keepdims=True) + EPS)
        h2_ref[...] = ((x1 * rstd * g_ref[...]) * (1.0 + mod_ref[4:5, :]) + mod_ref[3:4, :]).astype(bf16)

    row = lambda w: pl.BlockSpec((tm, w), lambda i: (i, 0))
    return pl.pallas_call(
        body, name="out_proj", grid=(T // tm,),
        in_specs=[row(D), row(DA), row(DA), _full((D, D)), _full((8, D)), _full((1, D))],
        out_specs=[row(D), row(D), row(D)],
        out_shape=[jax.ShapeDtypeStruct((T, D), f32), jax.ShapeDtypeStruct((T, D), bf16), jax.ShapeDtypeStruct((T, D), bf16)],
        compiler_params=_cp("parallel"),
    )(xa, y_na, y_cv, w_out, modv, g2)


FFN_TT = 2048
FFN_CT = 256
FFN_NC = F // FFN_CT
FFN_SUB = 32


def _row_neighbours(ref, r, n):
    blk = ref[pl.ds(r - 8, n + 16), :]
    return blk[8:8 + n, :], pltpu.roll(blk, 1, 0)[8:8 + n, :], pltpu.roll(blk, n + 15, 0)[8:8 + n, :]


def _ffn_specs(tt, ct, by_token_first):
    tc = (lambda f: (lambda t, c: f(t, c))) if by_token_first else (lambda f: (lambda c, t: f(t, c)))
    per = tt // HALO
    halo = [pl.BlockSpec((HALO, D), tc(lambda t, c: (jnp.maximum(t * per - 1, 0), 0))),
            pl.BlockSpec((tt, D), tc(lambda t, c: (t, 0))),
            pl.BlockSpec((HALO, D), tc(lambda t, c: (jnp.minimum((t + 1) * per, T // HALO - 1), 0)))]
    weights = [pl.BlockSpec((ct, D), tc(lambda t, c: (c, 0))), pl.BlockSpec((ct, D), tc(lambda t, c: (c + FFN_NC, 0))),
               pl.BlockSpec((3, ct), tc(lambda t, c: (0, c))), pl.BlockSpec((3, ct), tc(lambda t, c: (0, c + FFN_NC))),
               pl.BlockSpec((1, ct), tc(lambda t, c: (0, c))), pl.BlockSpec((1, ct), tc(lambda t, c: (0, c + FFN_NC))),
               pl.BlockSpec((ct, D), tc(lambda t, c: (c, 0)))]
    return halo, weights


def _ffn_fwd(h2, w_upT, fcw, fcb, w_down):
    tt, ct = FFN_TT, FFN_CT
    nt = T // tt
    wn = tt + 2 * HALO
    half = tt // 2

    def body(hp, hc, hn, wg_ref, wv_ref, cwg_ref, cwv_ref, cbg_ref, cbv_ref, wd_ref, o_ref, u_ref, u2_ref, hwin, uwin, act):
        t = pl.program_id(0)
        c = pl.program_id(1)

        @pl.when(c == 0)
        def _():
            hwin[0:HALO, :] = jnp.where(t > 0, hp[...], jnp.zeros_like(hp[...]))
            hwin[HALO:HALO + tt, :] = hc[...]
            hwin[HALO + tt:, :] = jnp.where(t < nt - 1, hn[...], jnp.zeros_like(hn[...]))
            o_ref[...] = jnp.zeros_like(o_ref)

        for r0, r1 in ((0, half + 2 * HALO), (half + 2 * HALO, wn)):
            hw = hwin[r0:r1, :]
            uwin[r0:r1, :ct] = _dg(hw, wg_ref[...], NT)
            uwin[r0:r1, ct:] = _dg(hw, wv_ref[...], NT)
        cw = jnp.concatenate([cwg_ref[...], cwv_ref[...]], axis=1)
        cb = jnp.concatenate([cbg_ref[...], cbv_ref[...]], axis=1)
        for p in range(2):
            for r in range(p * half, (p + 1) * half, FFN_SUB):
                uc, prev, nxt = _row_neighbours(uwin, HALO + r, FFN_SUB)
                u2 = prev * cw[0:1, :] + uc * cw[1:2, :] + nxt * cw[2:3, :] + cb
                u_ref[r:r + FFN_SUB, :] = uc.astype(bf16)
                u2_ref[r:r + FFN_SUB, :] = u2
                gate = u2[:, :ct]
                act[r:r + FFN_SUB, :] = (gate * _sigmoid(gate) * u2[:, ct:]).astype(bf16)
            rows = slice(p * half, (p + 1) * half)
            o_ref[rows, :] += _dot(act[rows, :], wd_ref[...])

    halo, weights = _ffn_specs(tt, ct, True)
    pair = pl.BlockSpec((tt, 2 * ct), lambda t, c: (t, c))
    return pl.pallas_call(
        body, name="ffn_fwd", grid=(nt, FFN_NC),
        in_specs=halo + weights,
        out_specs=[pl.BlockSpec((tt, D), lambda t, c: (t, 0)), pair, pair],
        out_shape=[jax.ShapeDtypeStruct((T, D), f32), jax.ShapeDtypeStruct((T, F2), bf16), jax.ShapeDtypeStruct((T, F2), f32)],
        scratch_shapes=[pltpu.VMEM((wn, D), bf16), pltpu.VMEM((wn, 2 * ct), f32), pltpu.VMEM((tt, ct), bf16)],
        compiler_params=_cp("parallel", "arbitrary"),
    )(h2, h2, h2, w_upT, w_upT, fcw, fcw, fcb, fcb, w_down)


def _loss_bwd(ffn, x1, tgt, modv, gf):
    tm = 1024
    nt = T // tm

    def body(f_ref, x1_ref, t_ref, mod_ref, g_ref, dx2_ref, df_ref, s_ref):
        i = pl.program_id(0)

        @pl.when(i == 0)
        def _():
            s_ref[...] = jnp.zeros_like(s_ref)

        ff = f_ref[...]
        gt2 = mod_ref[5:6, :]
        x2 = x1_ref[...] + gt2 * ff
        rstd = lax.rsqrt(jnp.mean(x2 * x2, axis=-1, keepdims=True) + EPS)
        xh = x2 * rstd
        gfv = g_ref[...]
        e = xh * gfv - t_ref[...]
        dy = e * (1.0 / D)
        dxh = dy * gfv
        dx2 = rstd * (dxh - xh * jnp.mean(dxh * xh, axis=-1, keepdims=True))
        dx2_ref[...] = dx2
        df_ref[...] = (dx2 * gt2).astype(bf16)
        s_ref[0:1, :] += jnp.sum(dy * xh, axis=0, keepdims=True)
        s_ref[1:2, :] += jnp.sum(dx2 * ff, axis=0, keepdims=True)
        s_ref[2:3, :] += jnp.sum(e * e, axis=0, keepdims=True)

        @pl.when(i == nt - 1)
        def _():
            tot = jnp.sum(s_ref[2:3, :], axis=-1, keepdims=True) * (0.5 / D)
            s_ref[3:4, :] = jnp.broadcast_to(tot, (1, D))

    row = lambda: pl.BlockSpec((tm, D), lambda i: (i, 0))
    return pl.pallas_call(
        body, name="loss_bwd", grid=(nt,),
        in_specs=[row(), row(), row(), _full((8, D)), _full((1, D))],
        out_specs=[row(), row(), _full((8, D))],
        out_shape=[jax.ShapeDtypeStruct((T, D), f32), jax.ShapeDtypeStruct((T, D), bf16), jax.ShapeDtypeStruct((8, D), f32)],
        compiler_params=_cp("arbitrary"),
    )(ffn, x1, tgt, modv, gf)


def _ffn_bwd(h2, dffn, u_t, u2_t, fcw, w_down):
    tt, ct = FFN_TT, FFN_CT
    nt = T // tt
    wn = tt + 2 * HALO
    half = tt // 2

    def body(dp, dc, dn, hc, uc_ref, u2p, u2c, u2n, cwg_ref, cwv_ref, wd_ref,
             dug_ref, duv_ref, dwu_ref, dwd_ref, dcwg_ref, dcwv_ref, dcbg_ref, dcbv_ref,
             dwin, d2win, dawin, accu, accd, act, du):
        t = pl.program_id(1)
        first, last = t == 0, t == nt - 1
        zero = jnp.zeros((HALO, D), bf16)
        dwin[0:HALO, :] = jnp.where(first, zero, dp[...])
        dwin[HALO:HALO + tt, :] = dc[...]
        dwin[HALO + tt:, :] = jnp.where(last, zero, dn[...])

        @pl.when(first)
        def _():
            for r in (accu, accd, dcwg_ref, dcwv_ref, dcbg_ref, dcbv_ref):
                r[...] = jnp.zeros_like(r)

        cw = jnp.concatenate([cwg_ref[...], cwv_ref[...]], axis=1)
        split = half + 2 * HALO
        for r0, r1 in ((0, split), (split, wn)):
            dawin[r0:r1, :] = _dg(dwin[r0:r1, :], wd_ref[...], NT)

        def grads(u2v, dact):
            gate, val = u2v[:, :ct], u2v[:, ct:]
            sg = _sigmoid(gate)
            silu = gate * sg
            return dact * val * (sg * (1.0 + gate * (1.0 - sg))), dact * silu, silu * val

        for blk, r0 in ((u2p, 0), (u2n, HALO + tt)):
            dgate, dval, _ = grads(blk[...], dawin[r0:r0 + HALO, :])
            d2win[r0:r0 + HALO, :ct] = dgate
            d2win[r0:r0 + HALO, ct:] = dval
        for p in range(2):
            rows = slice(p * half, (p + 1) * half)
            for r in range(p * half, (p + 1) * half, FFN_SUB):
                dgate, dval, av = grads(u2c[r:r + FFN_SUB, :], dawin[HALO + r:HALO + r + FFN_SUB, :])
                d2win[HALO + r:HALO + r + FFN_SUB, :ct] = dgate
                d2win[HALO + r:HALO + r + FFN_SUB, ct:] = dval
                act[r:r + FFN_SUB, :] = av.astype(bf16)
            accd[...] += _dg(act[rows, :], dc[rows, :], TN)

        def fold8(x):
            out = x[0:8]
            for k in range(8, FFN_SUB, 8):
                out = out + x[k:k + 8]
            return out

        for p in range(2):
            rows = slice(p * half, (p + 1) * half)
            sums = [jnp.zeros((8, 2 * ct), f32) for _ in range(4)]
            for r in range(p * half, (p + 1) * half, FFN_SUB):
                d2c, d2m, d2p = _row_neighbours(d2win, HALO + r, FFN_SUB)
                ucur = uc_ref[r:r + FFN_SUB, :].astype(f32)
                sums[0] = sums[0] + fold8(d2c)
                for kk, dd in enumerate((d2p, d2c, d2m)):
                    sums[1 + kk] = sums[1 + kk] + fold8(ucur * dd)
                du[r:r + FFN_SUB, :] = (d2p * cw[0:1, :] + d2c * cw[1:2, :] + d2m * cw[2:3, :]).astype(bf16)
            dcb = jnp.sum(sums[0], axis=0, keepdims=True)
            dcbg_ref[...] += dcb[:, :ct]
            dcbv_ref[...] += dcb[:, ct:]
            for kk in range(3):
                dck = jnp.sum(sums[1 + kk], axis=0, keepdims=True)
                dcwg_ref[kk:kk + 1, :] += dck[:, :ct]
                dcwv_ref[kk:kk + 1, :] += dck[:, ct:]
            dug_ref[rows, :] = du[rows, :ct]
            duv_ref[rows, :] = du[rows, ct:]
            accu[...] += _dg(du[rows, :], hc[rows, :], TN)

        @pl.when(last)
        def _():
            dwu_ref[0] = accu[0:ct, :].astype(bf16)
            dwu_ref[1] = accu[ct:, :].astype(bf16)
            dwd_ref[...] = accd[...].astype(bf16)

    per = tt // HALO
    prev = lambda w: (lambda c, t: (jnp.maximum(t * per - 1, 0), c if w else 0))
    nxt = lambda w: (lambda c, t: (jnp.minimum((t + 1) * per, T // HALO - 1), c if w else 0))
    tile = lambda: pl.BlockSpec((ct, D), lambda c, t: (c, 0))
    lane = lambda r, off: pl.BlockSpec((r, ct), lambda c, t: (0, c + off))
    return pl.pallas_call(
        body, name="ffn_bwd", grid=(FFN_NC, nt),
        in_specs=[pl.BlockSpec((HALO, D), prev(False)), pl.BlockSpec((tt, D), lambda c, t: (t, 0)), pl.BlockSpec((HALO, D), nxt(False)),
                  pl.BlockSpec((tt, D), lambda c, t: (t, 0)), pl.BlockSpec((tt, 2 * ct), lambda c, t: (t, c)),
                  pl.BlockSpec((HALO, 2 * ct), prev(True)), pl.BlockSpec((tt, 2 * ct), lambda c, t: (t, c)),
                  pl.BlockSpec((HALO, 2 * ct), nxt(True)), lane(3, 0), lane(3, FFN_NC), tile()],
        out_specs=[pl.BlockSpec((tt, ct), lambda c, t: (t, c)), pl.BlockSpec((tt, ct), lambda c, t: (t, c)),
                   pl.BlockSpec((2, ct, D), lambda c, t: (0, c, 0)), tile(), lane(3, 0), lane(3, 0), lane(1, 0), lane(1, 0)],
        out_shape=[jax.ShapeDtypeStruct((T, F), bf16), jax.ShapeDtypeStruct((T, F), bf16),
                   jax.ShapeDtypeStruct((2, F, D), bf16), jax.ShapeDtypeStruct((F, D), bf16),
                   jax.ShapeDtypeStruct((3, F), f32), jax.ShapeDtypeStruct((3, F), f32),
                   jax.ShapeDtypeStruct((1, F), f32), jax.ShapeDtypeStruct((1, F), f32)],
        scratch_shapes=[pltpu.VMEM((wn, D), bf16), pltpu.VMEM((wn, 2 * ct), f32), pltpu.VMEM((wn, ct), f32),
                        pltpu.VMEM((2 * ct, D), f32), pltpu.VMEM((ct, D), f32),
                        pltpu.VMEM((tt, ct), bf16), pltpu.VMEM((tt, 2 * ct), bf16)],
        compiler_params=_cp("parallel", "arbitrary"),
    )(dffn, dffn, dffn, h2, u_t, u2_t, u2_t, u2_t, fcw, fcw, w_down)


def _norm_bwd(dh, xv, gain, sh_sc, rstd):
    xh = xv * rstd
    n = xh * gain
    dn = dh * (1.0 + sh_sc)
    dxh = dn * gain
    dx = rstd * (dxh - xh * jnp.mean(dxh * xh, axis=-1, keepdims=True))
    return (dx, jnp.sum(dh, axis=0, keepdims=True), jnp.sum(dh * n, axis=0, keepdims=True),
            jnp.sum(dn * xh, axis=0, keepdims=True))


def _norm2_bwd(dug, duv, w_upT, x1, dx2, proj, w_out, y_na, y_cv, modv, g2):
    tm = 512
    nt = T // tm

    def body(dug_ref, duv_ref, w_ref, x1_ref, dx2_ref, pj_ref, wo_ref, ya_ref, yc_ref, mod_ref, g_ref,
             dx1_ref, dya_ref, dyc_ref, dwo_ref, s_ref, acc):
        i = pl.program_id(0)

        @pl.when(i == 0)
        def _():
            s_ref[...] = jnp.zeros_like(s_ref)
            acc[...] = jnp.zeros_like(acc)

        dh2 = _dot(dug_ref[...], w_ref[0:F, :]) + _dot(duv_ref[...], w_ref[F:F2, :])
        x1 = x1_ref[...]
        rstd = lax.rsqrt(jnp.mean(x1 * x1, axis=-1, keepdims=True) + EPS)
        dxn, dsh, dsc, dgn = _norm_bwd(dh2, x1, g_ref[...], mod_ref[4:5, :], rstd)
        dx1 = dx2_ref[...] + dxn
        dx1_ref[...] = dx1
        dpj = (dx1 * mod_ref[2:3, :]).astype(bf16)
        dyc = _dg(dpj, wo_ref[...], NT)
        dya_ref[...] = dyc[:, :DA].astype(bf16)
        dyc_ref[...] = dyc[:, DA:]
        acc[0:DA, :] += _dg(ya_ref[...], dpj, TN)
        acc[DA:D, :] += _dg(yc_ref[...], dpj, TN)

        @pl.when(i == nt - 1)
        def _():
            dwo_ref[...] = acc[...].astype(bf16)

        s_ref[0:1, :] += dsh
        s_ref[1:2, :] += dsc
        s_ref[2:3, :] += dgn
        s_ref[3:4, :] += jnp.sum(dx1 * pj_ref[...].astype(f32), axis=0, keepdims=True)

    row = lambda w: pl.BlockSpec((tm, w), lambda i: (i, 0))
    return pl.pallas_call(
        body, name="norm2_bwd", grid=(nt,),
        in_specs=[row(F), row(F), _resident((F2, D)), row(D), row(D), row(D), _resident((D, D)), row(DA), row(DA),
                  _full((8, D)), _full((1, D))],
        out_specs=[row(D), row(DA), row(DA), _full((D, D)), _full((8, D))],
        out_shape=[jax.ShapeDtypeStruct((T, D), f32), jax.ShapeDtypeStruct((T, DA), bf16), jax.ShapeDtypeStruct((T, DA), f32),
                   jax.ShapeDtypeStruct((D, D), bf16), jax.ShapeDtypeStruct((8, D), f32)],
        scratch_shapes=[pltpu.VMEM((D, D), f32)],
        compiler_params=_cp("arbitrary"),
    )(dug, duv, w_upT, x1, dx2, proj, w_out, y_na, y_cv, modv, g2)


def _conf_bwd(a, g, cv, dy, conv_w, ln_g, ln_b, blocks):
    tt = CONV_TT
    nt = T // tt
    sub = 32
    wn = tt + 2 * HALO
    nb = len(blocks)

    def body(ap, ac, an, gp, gc, gn, cp_, cc, cn, dp, dc, dn, w_ref, lg_ref, lb_ref, *rest):
        g_refs, (da_ref, dg_ref, dcw_ref, s_ref) = rest[:nb], rest[nb:nb + 4]
        recv_refs, (urot, drot, wacc), a2a_sems = rest[nb + 4:2 * nb + 4], rest[2 * nb + 4:2 * nb + 7], rest[2 * nb + 7:]
        i = pl.program_id(0)
        first, last = i == 0, i == nt - 1
        plans = [_a2a_plan(g_refs[k], recv_refs[k], *a2a_sems[3 * k:3 * k + 3]) for k in range(nb)]
        for start, _ in plans:
            pl.when(first)(start)

        @pl.when(first)
        def _():
            s_ref[...] = jnp.zeros_like(s_ref)
            wacc[...] = jnp.zeros_like(wacc)

        lg, lb = lg_ref[...], lb_ref[...]

        def ln_bwd(cvv, dyv):
            mu = jnp.mean(cvv, axis=-1, keepdims=True)
            xc = cvv - mu
            rstd = lax.rsqrt(jnp.mean(xc * xc, axis=-1, keepdims=True) + EPS)
            yn = xc * rstd
            z = yn * lg + lb
            sz = _sigmoid(z)
            dz = dyv * (sz * (1.0 + z * (1.0 - sz)))
            dyn = dz * lg
            dcv = rstd * (dyn - jnp.mean(dyn, axis=-1, keepdims=True) - yn * jnp.mean(dyn * yn, axis=-1, keepdims=True))
            return dcv, dz, yn

        urot[0, 0:HALO, :] = jnp.where(first, 0.0, ap[...] * _sigmoid(gp[...]))
        urot[0, HALO + tt:, :] = jnp.where(last, 0.0, an[...] * _sigmoid(gn[...]))
        drot[0, 0:HALO, :] = jnp.where(first, 0.0, ln_bwd(cp_[...], dp[...])[0])
        drot[0, HALO + tt:, :] = jnp.where(last, 0.0, ln_bwd(cn[...], dn[...])[0])
        for s in range(tt // sub):
            rr = pl.ds(s * sub, sub)
            urot[0, pl.ds(HALO + s * sub, sub), :] = ac[rr, :] * _sigmoid(gc[rr, :])
            dcv, dz, yn = ln_bwd(cc[rr, :], dc[rr, :])
            drot[0, pl.ds(HALO + s * sub, sub), :] = dcv
            s_ref[0:1, :] += jnp.sum(dcv, axis=0, keepdims=True)
            s_ref[1:2, :] += jnp.sum(dz * yn, axis=0, keepdims=True)
            s_ref[2:3, :] += jnp.sum(dz, axis=0, keepdims=True)
        _shifted_copies(urot, wn)
        _shifted_copies(drot, wn)
        w = w_ref[...]
        for s in range(tt // sub):
            rr = pl.ds(s * sub, sub)
            dcv = drot[0, pl.ds(HALO + s * sub, sub), :]
            acc = jnp.zeros((sub, DA), f32)
            for j in range(CW):
                ad, bd = divmod(2 * HALO - 1 - j, 8)
                au, bu = divmod(1 + j, 8)
                acc = acc + drot[bd, pl.ds(s * sub + 8 * ad, sub), :] * w[j:j + 1, :]
                part = urot[bu, pl.ds(s * sub + 8 * au, sub), :] * dcv
                wacc[j] += part[0:8] + part[8:16] + part[16:24] + part[24:32]
            av, gv = ac[rr, :], gc[rr, :]
            sg = _sigmoid(gv)
            da_ref[rr, :] = (acc * sg).astype(bf16)
            dg_ref[rr, :] = (acc * av * sg * (1.0 - sg)).astype(bf16)

        @pl.when(last)
        def _():
            for j in range(CW):
                dcw_ref[j:j + 1, :] = jnp.sum(wacc[j], axis=0, keepdims=True)
            dcw_ref[CW:CW + 1, :] = jnp.zeros((1, DA), f32)

        for _, finish in plans:
            pl.when(last)(finish)

    hs = _halo_specs(tt, DA, T // HALO)
    hbm = pl.BlockSpec(memory_space=pl.ANY)
    return pl.pallas_call(
        body, name="conf_bwd", grid=(nt,),
        in_specs=hs * 4 + [_full((CW, DA)), _full((1, DA)), _full((1, DA))] + [hbm] * nb,
        out_specs=[pl.BlockSpec((tt, DA), lambda i: (i, 0)), pl.BlockSpec((tt, DA), lambda i: (i, 0)),
                   _full((CW + 1, DA)), _full((8, DA))] + [hbm] * nb,
        out_shape=[jax.ShapeDtypeStruct((T, DA), bf16), jax.ShapeDtypeStruct((T, DA), bf16),
                   jax.ShapeDtypeStruct((CW + 1, DA), f32), jax.ShapeDtypeStruct((8, DA), f32)]
                  + [jax.ShapeDtypeStruct(b.shape, b.dtype) for b in blocks],
        scratch_shapes=[pltpu.VMEM((8, wn, DA), f32), pltpu.VMEM((8, wn, DA), f32), pltpu.VMEM((CW, 8, DA), f32)]
                       + _a2a_scratch() * nb,
        compiler_params=_cp("arbitrary"),
    )(a, a, a, g, g, g, cv, cv, cv, dy, dy, dy, conv_w, ln_g, ln_b, *blocks)


def _attn_bwd(q, k, v, y, dy, lse, bias_tab, blocks):
    zr = 256
    nb = len(blocks)

    def body(q_ref, k_ref, v_ref, y_ref, dy_ref, lse_ref, b_ref, *rest):
        g_refs, (dq_ref, dk_hbm, dv_hbm, db_ref) = rest[:nb], rest[nb:nb + 4]
        recv_refs, (dk_s, dv_s, sem), a2a_sems = rest[nb + 4:2 * nb + 4], rest[2 * nb + 4:2 * nb + 7], rest[2 * nb + 7:]
        r = pl.program_id(0)
        plans = [_a2a_plan(g_refs[i], recv_refs[i], *a2a_sems[3 * i:3 * i + 3]) for i in range(nb)]
        for start, _ in plans:
            pl.when(r == 0)(start)

        @pl.when(r == 0)
        def _():
            def z(i, _):
                rr = pl.ds(pl.multiple_of(i * zr, zr), zr)
                dk_s[rr, :] = jnp.zeros((zr, DA), f32)
                dv_s[rr, :] = jnp.zeros((zr, DA), f32)
                return 0
            lax.fori_loop(0, TA // zr, z, 0)

        @pl.when((r <= WR // 2) | (r > GW - WR // 2))
        def _():
            db_ref[...] = jnp.zeros_like(db_ref)

        ks = pl.multiple_of(_win_start(r) * GW, GW)
        win = pl.ds(ks, WR * GW)
        qq, yy, dyy, lse_v = q_ref[...], y_ref[...], dy_ref[...], lse_ref[...]
        lo = lax.broadcasted_iota(jnp.int32, (GW, 2 * HD), 1) < HD
        ops, pairs = [], []
        for pr in range(NH // 2):
            ps = slice(pr * 2 * HD, (pr + 1) * 2 * HD)
            q2, do2 = qq[:, ps], dyy[:, ps]
            prod = do2.astype(f32) * yy[:, ps].astype(f32)
            delta = jnp.concatenate([jnp.sum(jnp.where(lo, prod, 0.0), axis=-1, keepdims=True),
                                     jnp.sum(jnp.where(lo, 0.0, prod), axis=-1, keepdims=True)], axis=0)
            qst, dost = _stack_pair(q2, lo), _stack_pair(do2, lo)
            kw, vw = k_ref[win, ps], v_ref[win, ps]
            kc, vc = k_ref[T:TA, ps], v_ref[T:TA, ps]
            ops.append((kw, kc))
            pairs.append((qst, dost, delta, _dg(qst, kw, NT), _dg(qst, kc, NT), _dg(dost, vw, NT), _dg(dost, vc, NT)))
        grads = []
        for pr, (qst, dost, delta, sl, sc, dpl, dpc) in enumerate(pairs):
            lh = jnp.concatenate([lse_v[:, 2 * pr:2 * pr + 1], lse_v[:, 2 * pr + 1:2 * pr + 2]], axis=0)
            bias2 = b_ref[0, 2 * pr:2 * pr + 2].reshape(2 * GW, WR * GW)
            pl_ = jnp.exp(sl * SCALE + bias2 - lh)
            pc = jnp.exp(sc * SCALE - lh)
            dsl = pl_ * (dpl - delta)
            dsc = pc * (dpc - delta)
            db_ref[0, 2 * pr:2 * pr + 2] += dsl.reshape(2, GW, WR * GW)
            grads.append((qst, dost, pl_.astype(bf16), pc.astype(bf16), dsl.astype(bf16), dsc.astype(bf16)))
        for pr in range(NH // 2):
            ps = slice(pr * 2 * HD, (pr + 1) * 2 * HD)
            kw, kc = ops[pr]
            qst, dost, plb, pcb, dslb, dscb = grads[pr]
            dqst = _dot(dslb, kw) + _dot(dscb, kc)
            dq_ref[:, ps] = (jnp.where(lo, dqst[0:GW], dqst[GW:]) * SCALE).astype(bf16)
            dk_s[win, ps] += _dg(dslb, qst, TN) * SCALE
            dv_s[win, ps] += _dg(plb, dost, TN)
            dk_s[T:TA, ps] += _dg(dscb, qst, TN) * SCALE
            dv_s[T:TA, ps] += _dg(pcb, dost, TN)

        @pl.when(r == GW - 1)
        def _():
            c1 = pltpu.make_async_copy(dk_s, dk_hbm, sem.at[0])
            c2 = pltpu.make_async_copy(dv_s, dv_hbm, sem.at[1])
            c1.start()
            c2.start()
            c1.wait()
            c2.wait()

        for _, finish in plans:
            pl.when(r == GW - 1)(finish)

    rowq = lambda: pl.BlockSpec((GW, DA), lambda r: (r, 0))
    hbm = pl.BlockSpec(memory_space=pl.ANY)
    return pl.pallas_call(
        body, name="attn_bwd", grid=(GW,),
        in_specs=[rowq(), _full((TA, DA)), _full((TA, DA)), rowq(), rowq(), pl.BlockSpec((GW, NH), lambda r: (r, 0)),
                  pl.BlockSpec((1, NH, GW, WR * GW), lambda r: (_pattern(r), 0, 0, 0))] + [hbm] * nb,
        out_specs=[rowq(), hbm, hbm, pl.BlockSpec((1, NH, GW, WR * GW), lambda r: (_pattern(r), 0, 0, 0))] + [hbm] * nb,
        out_shape=[jax.ShapeDtypeStruct((T, DA), bf16), jax.ShapeDtypeStruct((TA, DA), f32), jax.ShapeDtypeStruct((TA, DA), f32),
                   jax.ShapeDtypeStruct((8, NH, GW, WR * GW), f32)] + [jax.ShapeDtypeStruct(b.shape, b.dtype) for b in blocks],
        scratch_shapes=[pltpu.VMEM((TA, DA), f32), pltpu.VMEM((TA, DA), f32), pltpu.SemaphoreType.DMA((2,))] + _a2a_scratch() * nb,
        compiler_params=_cp("arbitrary"),
    )(q, k, v, y, dy, lse, bias_tab, *blocks)


def _rpb_reduce(dbias):
    rev = np.eye(GW, dtype=np.float32)[::-1]

    def body(d_ref, rev_ref, o_ref):
        rv = rev_ref[...]
        for h in range(NH):
            dv = d_ref[0, h]
            r0 = dv.astype(bf16)
            e1 = dv - r0.astype(f32)
            r1 = e1.astype(bf16)
            r2 = (e1 - r1.astype(f32)).astype(bf16)
            rr = _dot(rv, r0) + _dot(rv, r1) + _dot(rv, r2)
            skew = pltpu.roll(rr, 0, 1, stride=1, stride_axis=0)
            o_ref[0, h:h + 1, :] = jnp.sum(skew, axis=0, keepdims=True)

    return pl.pallas_call(
        body, name="rpb_reduce", grid=(8,),
        in_specs=[pl.BlockSpec((1, NH, GW, WR * GW), lambda p: (p, 0, 0, 0)), _full((GW, GW))],
        out_specs=pl.BlockSpec((1, NH, WR * GW), lambda p: (p, 0, 0)),
        out_shape=jax.ShapeDtypeStruct((8, NH, WR * GW), f32),
        compiler_params=_cp("parallel"),
    )(dbias, jnp.asarray(rev, dtype=bf16))


def _norm1_bwd(dq, dk, dv, da, dg, w_inT, x0, ctx0, h, dx1, modv, g1):
    tm = 256
    nt = TA // tm
    nx = T // tm

    def body(dq_ref, dk_ref, dv_ref, da_ref, dg_ref, w_ref, x_ref, c_ref, h_ref, dx1_ref, mod_ref, g_ref,
             dx_ref, dwo_ref, s_ref, dw_ref):
        i = pl.program_id(0)
        is_ctx = i == nt - 1

        @pl.when(i == 0)
        def _():
            s_ref[...] = jnp.zeros_like(s_ref)
            dw_ref[...] = jnp.zeros_like(dw_ref)

        hb = h_ref[...]
        dkb, dvb = dk_ref[...].astype(bf16), dv_ref[...].astype(bf16)
        dw_ref[DA:2 * DA, :] += _dg(dkb, hb, TN)
        dw_ref[2 * DA:3 * DA, :] += _dg(dvb, hb, TN)
        dh_kv = _dot(dkb, w_ref[DA:2 * DA, :]) + _dot(dvb, w_ref[2 * DA:3 * DA, :])
        gain = g_ref[...]

        @pl.when(is_ctx)
        def _():
            xv = c_ref[...]
            rstd = lax.rsqrt(jnp.mean(xv * xv, axis=-1, keepdims=True) + EPS)
            _, dsh, dsc, dgn = _norm_bwd(dh_kv, xv, gain, mod_ref[7:8, :], rstd)
            s_ref[2:3, :] += dgn
            s_ref[3:4, :] += dsh
            s_ref[4:5, :] += dsc
            dwo_ref[...] = dw_ref[...].astype(bf16)

        @pl.when(jnp.logical_not(is_ctx))
        def _():
            dqb, dab, dgb = dq_ref[...], da_ref[...], dg_ref[...]
            dw_ref[0:DA, :] += _dg(dqb, hb, TN)
            dw_ref[3 * DA:4 * DA, :] += _dg(dab, hb, TN)
            dw_ref[4 * DA:5 * DA, :] += _dg(dgb, hb, TN)
            dh = (dh_kv + _dot(dqb, w_ref[0:DA, :]) + _dot(dab, w_ref[3 * DA:4 * DA, :])
                  + _dot(dgb, w_ref[4 * DA:5 * DA, :]))
            xv = x_ref[...]
            rstd = lax.rsqrt(jnp.mean(xv * xv, axis=-1, keepdims=True) + EPS)
            dxn, dsh, dsc, dgn = _norm_bwd(dh, xv, gain, mod_ref[1:2, :], rstd)
            dx_ref[...] = dx1_ref[...] + dxn
            s_ref[0:1, :] += dsh
            s_ref[1:2, :] += dsc
            s_ref[2:3, :] += dgn

    row = lambda w: pl.BlockSpec((tm, w), lambda i: (i, 0))
    lrow = lambda w: pl.BlockSpec((tm, w), lambda i: (jnp.minimum(i, nx - 1), 0))
    return pl.pallas_call(
        body, name="norm1_bwd", grid=(nt,),
        in_specs=[lrow(DA), row(DA), row(DA), lrow(DA), lrow(DA), _full((5 * DA, D)), lrow(D), _full((TC, D)), row(D),
                  lrow(D), _full((8, D)), _full((1, D))],
        out_specs=[lrow(D), _full((5 * DA, D)), _full((8, D))],
        out_shape=[jax.ShapeDtypeStruct((T, D), f32), jax.ShapeDtypeStruct((5 * DA, D), bf16), jax.ShapeDtypeStruct((8, D), f32)],
        scratch_shapes=[pltpu.VMEM((5 * DA, D), f32)],
        compiler_params=_cp("arbitrary"),
    )(dq, dk, dv, da, dg, w_inT, x0, ctx0, h, dx1, modv, g1)


def _adam_tile(r, c):
    return max(t for t in range(8, r + 1, 8) if r % t == 0 and t * c * 4 <= 2 * 1024 * 1024)


def _adam_update(wv, gv, mv, vv):
    nm = ADAM_B1 * mv + (1.0 - ADAM_B1) * gv
    nv = ADAM_B2 * vv + (1.0 - ADAM_B2) * (gv * gv)
    m_hat = nm * (1.0 / (1.0 - ADAM_B1 ** ADAM_STEP))
    v_hat = nv * (1.0 / (1.0 - ADAM_B2 ** ADAM_STEP))
    return -ADAM_LR * (m_hat / (jnp.sqrt(v_hat) + ADAM_EPS) + ADAM_WD * wv), nm, nv


def _adamw(w, g, m, v, name):
    r, c = w.shape
    tr = _adam_tile(r, c)

    def body(w_ref, g_ref, m_ref, v_ref, d_ref, nm_ref, nv_ref):
        d_ref[...], nm_ref[...], nv_ref[...] = _adam_update(w_ref[...], g_ref[...], m_ref[...], v_ref[...])

    spec = pl.BlockSpec((tr, c), lambda i: (i, 0))
    return pl.pallas_call(
        body, name=name, grid=(r // tr,),
        in_specs=[spec] * 4, out_specs=[spec] * 3,
        out_shape=[jax.ShapeDtypeStruct((r, c), f32)] * 3,
        compiler_params=_cp("parallel"),
    )(w, g, m, v)


def _adamw_blocks(w, recv, m, v, name):
    r, c = w.shape
    tr = _adam_tile(r, c)

    def body(w_ref, a_ref, m_ref, v_ref, g_ref, d_ref, nm_ref, nv_ref):
        gv = a_ref[0].astype(f32)
        for d in range(1, NDEV):
            gv = gv + a_ref[d].astype(f32)
        g_ref[...] = gv
        d_ref[...], nm_ref[...], nv_ref[...] = _adam_update(w_ref[...], gv, m_ref[...], v_ref[...])

    spec = pl.BlockSpec((tr, c), lambda i: (i, 0))
    return pl.pallas_call(
        body, name=name, grid=(r // tr,),
        in_specs=[spec, pl.BlockSpec((NDEV, tr, c), lambda i: (0, i, 0)), spec, spec], out_specs=[spec] * 4,
        out_shape=[jax.ShapeDtypeStruct((r, c), f32)] * 4,
        compiler_params=_cp("parallel"),
    )(w, recv, m, v)


def _pad_rows128(vec):
    n = vec.shape[0]
    rows = -(-n // 1024) * 8
    return jnp.pad(vec, (0, rows * 128 - n)).reshape(rows, 128)


def _grad_rpb(dbias):
    lane_map, r_hot = _rpb_tables()
    return jnp.einsum("phl,lic,pir->hrc", _rpb_reduce(dbias), jnp.asarray(lane_map), jnp.asarray(r_hot),
                      precision=lax.Precision.HIGHEST)


def kernel(x, c, ctx, c_ctx, w_mod, b_mod, g_norm1, w_in, rpb, conv_w, conv_b, ln_g, ln_b, w_out, g_norm2, w_up, ffn_conv_w, ffn_conv_b, w_down, g_final, loss_target, m_c_ctx, m_w_mod, m_b_mod, m_g_norm1, m_w_in, m_rpb, m_conv_w, m_conv_b, m_ln_g, m_ln_b, m_w_out, m_g_norm2, m_w_up, m_ffn_conv_w, m_ffn_conv_b, m_w_down, m_g_final, v_c_ctx, v_w_mod, v_b_mod, v_g_norm1, v_w_in, v_rpb, v_conv_w, v_conv_b, v_ln_g, v_ln_b, v_w_out, v_g_norm2, v_w_up, v_ffn_conv_w, v_ffn_conv_b, v_w_down, v_g_final):
    me = 4 * lax.axis_index("x") + 2 * lax.axis_index("y") + lax.axis_index("c")
    nmod = w_mod.shape[2]
    n_in = w_in.shape[2]
    n_out = w_out.shape[1]
    n_up = w_up.shape[2]
    n_dn = w_down.shape[1]
    n_cw = conv_w.shape[2]

    b_sh = lax.dynamic_slice(b_mod, (0, me * nmod), (1, nmod))
    convpay = _pad_rows128(jnp.concatenate([conv_w[0].reshape(-1), ffn_conv_w[0].reshape(-1)]))
    w_inT, c_all, mod_all, flat, bias_tab = _head(w_in[0].T.astype(bf16), c.reshape(8, 128), c_ctx.reshape(8, 128),
                                                  w_mod[0], b_sh, convpay, *_bias_table_inputs(rpb[0]))
    cvec = jnp.concatenate([c_all.reshape(NDEV, D), c_ctx[None, :], jnp.zeros((7, D), f32)], axis=0)
    mod_all = mod_all.transpose(1, 0, 2).reshape(16, 6 * D)
    mod_me = lax.dynamic_index_in_dim(mod_all, me, 0, keepdims=False).reshape(6, D)
    mod_c = mod_all[8]
    modv = jnp.concatenate([mod_me, mod_c[None, 0:D], mod_c[None, D:2 * D]], axis=0)
    flat = flat.reshape(NDEV, -1)
    o1 = CW * n_cw
    conv_w_f = flat[:, :o1].reshape(NDEV, CW, n_cw).transpose(1, 0, 2).reshape(CW, DA)
    fcw_f = flat[:, o1:o1 + 3 * n_up].reshape(NDEV, 3, n_up).transpose(1, 0, 2).reshape(3, F2)

    x0, ctx0 = x[0], ctx[0]
    h, q, k, v, a, g, w_down_f = _in_proj(x0, ctx0, g_norm1, modv, w_inT, [w_down[0].astype(bf16)])
    y_na, lse, y_cv, cv, w_upT, w_out_f = _mix_fwd(q, k, v, bias_tab, a, g, conv_w_f, conv_b, ln_g, ln_b,
                                                   [w_up[0].T.astype(bf16), w_out[0].astype(bf16)])
    x1, proj, h2 = _out_proj(x0, y_na, y_cv, w_out_f, modv, g_norm2)
    ffn, u_t, u2_t = _ffn_fwd(h2, w_upT, fcw_f, ffn_conv_b, w_down_f)
    dx2, dffn, s_loss = _loss_bwd(ffn, x1, loss_target[0], modv, g_final[None, :])

    dug, duv, dw_up, dw_down, dcwg, dcwv, dcbg, dcbv = _ffn_bwd(h2, dffn, u_t, u2_t, fcw_f, w_down_f)
    dx1, dy_na, dy_cv, dw_out, s_n2 = _norm2_bwd(dug, duv, w_upT, x1, dx2, proj, w_out_f, y_na, y_cv, modv, g_norm2)
    da, dg, dcw, s_cf, rv_down = _conf_bwd(a, g, cv, dy_cv, conv_w_f, ln_g, ln_b, [dw_down.reshape(NDEV, n_dn, D)])
    dq, dk, dv, dbias, rv_up = _attn_bwd(q, k, v, y_na, dy_na, lse, bias_tab, [dw_up.reshape(NDEV, n_up, D)])
    grad_rpb_part = _grad_rpb(dbias)
    grad_x, dw_inT, s_n1 = _norm1_bwd(dq, dk, dv, da, dg, w_inT, x0, ctx0, h, dx1, modv, g_norm1)
    grad_x = grad_x[None]
    dfcw = jnp.concatenate([dcwg, dcwv], axis=1)
    dfcb = jnp.concatenate([dcbg[0], dcbv[0]])
    small = jnp.concatenate([dcw[:CW].reshape(CW, NDEV, n_cw).transpose(1, 0, 2).reshape(NDEV, CW * n_cw),
                             dfcw.reshape(3, NDEV, n_up).transpose(1, 0, 2).reshape(NDEV, 3 * n_up)], axis=1)
    n_sm = small.shape[1] // D
    pad_sm = -n_sm % 16
    small = jnp.pad(small.reshape(NDEV, n_sm, D), ((0, 0), (0, pad_sm), (0, 0))).astype(bf16)
    dmod = jnp.concatenate([s_n1[0], s_n1[1], s_n2[3], s_n2[0], s_n2[1], s_loss[1]])
    dmodc = jnp.concatenate([s_n1[3], s_n1[4]])
    parts = [dmodc, s_n1[2], grad_rpb_part.reshape(-1), s_cf[0], s_cf[1], s_cf[2], s_n2[2], dfcb, s_loss[0], s_loss[3, 0:1]]
    sizes = [p.shape[0] for p in parts]
    pvec = _pad_rows128(jnp.concatenate([dmod] + parts))
    slab = jnp.concatenate([dw_inT.reshape(NDEV, n_in, D), small, dw_out.reshape(NDEV, n_out, D)], axis=1)
    r_a, gath = _reduce_scatter2(slab, [pvec], "rs_w_in")
    g_w_out = r_a[n_in + n_sm + pad_sm:]
    g_w_inT = r_a[:n_in]
    sm = r_a[n_in:n_in + n_sm].reshape(-1)
    g_conv_w = sm[:CW * n_cw].reshape(CW, n_cw)
    g_fcw = sm[CW * n_cw:].reshape(3, n_up)
    tot = _sum_rows8(gath, "sum_small").reshape(-1)
    dmod_all = gath.reshape(NDEV, -1)[:, :6 * D]
    offs = np.cumsum([6 * D] + sizes)
    pick = lambda j: tot[offs[j]:offs[j + 1]]
    dmodc_t = jnp.pad(pick(0), (0, 4 * D))
    g_b_mod = (tot[:6 * D] + dmodc_t)[None, :]
    g_g_norm1 = pick(1)[None, :]
    g_rpb = pick(2).reshape(1, NH, 2 * WR - 1, 2 * NCOL - 1)
    g_conv_b, g_ln_g, g_ln_b = pick(3)[None, :], pick(4)[None, :], pick(5)[None, :]
    g_g_norm2 = pick(6)[None, :]
    g_fcb = pick(7)[None, :]
    g_g_final = pick(8)
    loss = pick(9)[0]
    dm_rows = jnp.concatenate([dmod_all, dmodc_t[None, :], jnp.zeros((7, 6 * D), f32)], axis=0)
    dm_sh = lax.dynamic_slice(dm_rows, (0, me * nmod), (16, nmod))
    g_w_mod, gc_part = _mod_bwd(cvec, dm_sh, w_mod[0])
    gc_sum = _sum_rows8(_small_allgather(gc_part[0].reshape(8, 128), "ag_cctx"), "sum_cctx").reshape(D)
    sg_c = _sigmoid(c_ctx)
    g_c_ctx = gc_sum * (sg_c * (1.0 + c_ctx * (1.0 - sg_c)))

    big = [("w_mod", w_mod[0], g_w_mod, m_w_mod[0], v_w_mod[0]), ("w_in", w_in[0].T, g_w_inT, m_w_in[0].T, v_w_in[0].T),
           ("w_out", w_out[0], g_w_out, m_w_out[0], v_w_out[0])]
    upd = {n: _adamw(wv, gv, mv, vv, "adamw_" + n) for n, wv, gv, mv, vv in big}
    g_w_upT, *upd["w_up"] = _adamw_blocks(w_up[0].T, rv_up, m_w_up[0].T, v_w_up[0].T, "adamw_w_up")
    g_w_down, *upd["w_down"] = _adamw_blocks(w_down[0], rv_down, m_w_down[0], v_w_down[0], "adamw_w_down")
    for n in ("w_in", "w_up"):
        upd[n] = tuple(arr.T for arr in upd[n])
    g_w_in, g_w_up = g_w_inT.T, g_w_upT.T
    smalls = [("c_ctx", c_ctx, g_c_ctx, m_c_ctx, v_c_ctx), ("b_mod", b_mod, g_b_mod, m_b_mod, v_b_mod),
              ("g_norm1", g_norm1, g_g_norm1, m_g_norm1, v_g_norm1), ("rpb", rpb, g_rpb, m_rpb, v_rpb),
              ("conv_w", conv_w, g_conv_w[None], m_conv_w, v_conv_w), ("conv_b", conv_b, g_conv_b, m_conv_b, v_conv_b),
              ("ln_g", ln_g, g_ln_g, m_ln_g, v_ln_g), ("ln_b", ln_b, g_ln_b, m_ln_b, v_ln_b),
              ("g_norm2", g_norm2, g_g_norm2, m_g_norm2, v_g_norm2),
              ("ffn_conv_w", ffn_conv_w, g_fcw[None], m_ffn_conv_w, v_ffn_conv_w),
              ("ffn_conv_b", ffn_conv_b, g_fcb, m_ffn_conv_b, v_ffn_conv_b), ("g_final", g_final, g_g_final, m_g_final, v_g_final)]
    packed = [_pad_rows128(jnp.concatenate([t[j].reshape(-1) for t in smalls])) for j in (1, 2, 3, 4)]
    sd, sm_, sv = _adamw(*packed, "adamw_small")
    so = np.cumsum([0] + [int(np.prod(t[1].shape)) for t in smalls])
    for j, t in enumerate(smalls):
        shp = t[1].shape
        upd[t[0]] = tuple(arr.reshape(-1)[so[j]:so[j + 1]].reshape(shp) for arr in (sd, sm_, sv))
    grads = {"c_ctx": g_c_ctx, "w_mod": g_w_mod[None], "b_mod": g_b_mod, "g_norm1": g_g_norm1, "w_in": g_w_in[None],
             "rpb": g_rpb, "conv_w": g_conv_w[None], "conv_b": g_conv_b, "ln_g": g_ln_g, "ln_b": g_ln_b,
             "w_out": g_w_out[None], "g_norm2": g_g_norm2, "w_up": g_w_up[None], "ffn_conv_w": g_fcw[None],
             "ffn_conv_b": g_fcb, "w_down": g_w_down[None], "g_final": g_g_final}
    names = ["c_ctx", "w_mod", "b_mod", "g_norm1", "w_in", "rpb", "conv_w", "conv_b", "ln_g", "ln_b", "w_out", "g_norm2",
             "w_up", "ffn_conv_w", "ffn_conv_b", "w_down", "g_final"]
    shapes = {n: grads[n].shape for n in names}
    outs = [loss, grad_x] + [grads[n] for n in names]
    for j in range(3):
        outs += [upd[n][j].reshape(shapes[n]) for n in names]
    return tuple(outs)
```

```python
import numpy as np
import jax
import jax.numpy as jnp
from jax import lax
from jax.experimental import pallas as pl
from jax.experimental.pallas import tpu as pltpu

f32 = jnp.float32
bf16 = jnp.bfloat16

D = 1024
T = 4096
TC = 256
TA = T + TC
DA = 512
NH = 8
HD = 64
GW = 64
WR = 8
NCOL = 16
F = 2816
F2 = 2 * F
CW = 31
NDEV = 8
EPS = 1e-6
SCALE = HD ** -0.5
NEG = -1e30
MESH = pl.DeviceIdType.MESH

NT = (((1,), (1,)), ((), ()))
TN = (((0,), (0,)), ((), ()))

ADAM_LR, ADAM_B1, ADAM_B2, ADAM_EPS, ADAM_WD, ADAM_STEP = 0.001, 0.9, 0.999, 1e-08, 0.01, 10

VMEM_LIMIT = 56 * 1024 * 1024


def _cp(*sem):
    return pltpu.CompilerParams(dimension_semantics=sem or None, vmem_limit_bytes=VMEM_LIMIT)


def _dot(a, b):
    return jnp.dot(a, b, preferred_element_type=f32)


def _dg(a, b, dims):
    return lax.dot_general(a, b, dims, preferred_element_type=f32)


def _sigmoid(x):
    return 1.0 / (1.0 + jnp.exp(-x))


def _full(shape):
    n = len(shape)
    return pl.BlockSpec(shape, lambda *_: (0,) * n)


def _resident(shape):
    n = len(shape)
    return pl.BlockSpec(shape, lambda *_: (0,) * n, pipeline_mode=pl.Buffered(1))


def _my_pos():
    return lax.axis_index("x"), lax.axis_index("y"), lax.axis_index("c")


def _small_gather_plan(v_ref, out_ref, send_sems, recv_sems):
    x, y, c = _my_pos()
    me = 4 * x + 2 * y + c
    peers = []
    for k in range(1, NDEV):
        kx, ky, kc = (k >> 2) & 1, (k >> 1) & 1, k & 1
        peers.append((x ^ kx, y ^ ky, c ^ kc))

    def copy(k, slot, to):
        return pltpu.make_async_remote_copy(
            src_ref=v_ref, dst_ref=out_ref.at[slot], send_sem=send_sems.at[k], recv_sem=recv_sems.at[k],
            device_id=to, device_id_type=MESH)

    def start():
        out_ref[me] = v_ref[...]
        for k, p in enumerate(peers):
            copy(k, me, p).start()

    def finish():
        for k, (px, py, pc) in enumerate(peers):
            copy(k, 4 * px + 2 * py + pc, (x, y, c)).wait_recv()
        for k, p in enumerate(peers):
            copy(k, me, p).wait_send()

    return start, finish


def _small_gather_scratch():
    return [pltpu.SemaphoreType.DMA((NDEV - 1,)), pltpu.SemaphoreType.DMA((NDEV - 1,))]


def _small_allgather(v, name):
    n = v.shape[0]

    def body(v_ref, out_ref, send_sems, recv_sems):
        start, finish = _small_gather_plan(v_ref, out_ref, send_sems, recv_sems)
        start()
        finish()

    return pl.pallas_call(
        body, name=name,
        out_shape=jax.ShapeDtypeStruct((NDEV, n, 128), f32),
        in_specs=[pl.BlockSpec(memory_space=pltpu.VMEM)],
        out_specs=pl.BlockSpec(memory_space=pltpu.VMEM),
        scratch_shapes=_small_gather_scratch(),
    )(v)


def _ag2_plan(x_refs, out_refs, send_sems, recv_sems, local_sems):
    na = len(x_refs)
    x, y, c = _my_pos()
    me, sibling = (x, y, c), (x, y, 1 - c)
    chips = [(1 - x, y), (x, 1 - y), (1 - x, 1 - y)]

    def rows(i, px, py, pc):
        m_per = x_refs[i].shape[0]
        return out_refs[i].at[pl.ds(pl.multiple_of((4 * px + 2 * py + pc) * m_per, 16 if m_per % 16 == 0 else 8), m_per), :]

    def copies(k, block, to, from_shard=False):
        return [pltpu.make_async_remote_copy(
            src_ref=x_refs[i] if from_shard else rows(i, *block), dst_ref=rows(i, *block),
            send_sem=send_sems.at[k * na + i], recv_sem=recv_sems.at[k * na + i], device_id=to, device_id_type=MESH)
            for i in range(na)]

    def mine():
        return [pltpu.make_async_copy(x_refs[i], rows(i, *me), local_sems.at[i]) for i in range(na)]

    def first():
        cps = copies(0, me, sibling, True)
        for j, chip in enumerate(chips):
            cps += copies(1 + j, me, (*chip, c), True)
        return cps

    def start():
        for cp in mine() + first():
            cp.start()

    def forward():
        for j, chip in enumerate(chips):
            for cp in copies(1 + j, (*chip, c), me):
                cp.wait_recv()
            for cp in copies(4 + j, (*chip, c), sibling):
                cp.start()

    def finish():
        for cp in copies(0, sibling, me):
            cp.wait_recv()
        for j, chip in enumerate(chips):
            for cp in copies(4 + j, (*chip, 1 - c), me):
                cp.wait_recv()
        for cp in first():
            cp.wait_send()
        for j, chip in enumerate(chips):
            for cp in copies(4 + j, (*chip, c), sibling):
                cp.wait_send()
        for cp in mine():
            cp.wait()

    return start, forward, finish


def _ag2_scratch(na):
    return [pltpu.SemaphoreType.DMA((7 * na,)), pltpu.SemaphoreType.DMA((7 * na,)), pltpu.SemaphoreType.DMA((na,))]


def _a2a_plan(g_ref, recv_ref, send_sems, recv_sems, local_sem):
    x, y, c = _my_pos()
    me = 4 * x + 2 * y + c
    peers = []
    for k in range(1, NDEV):
        kx, ky, kc = (k >> 2) & 1, (k >> 1) & 1, k & 1
        peers.append((x ^ kx, y ^ ky, c ^ kc))

    def sends():
        return [pltpu.make_async_remote_copy(
            src_ref=g_ref.at[4 * px + 2 * py + pc], dst_ref=recv_ref.at[me], send_sem=send_sems.at[k], recv_sem=recv_sems.at[k],
            device_id=(px, py, pc), device_id_type=MESH) for k, (px, py, pc) in enumerate(peers)]

    def own():
        return pltpu.make_async_copy(g_ref.at[me], recv_ref.at[me], local_sem)

    def start():
        own().start()
        for cp in sends():
            cp.start()

    def finish():
        for k, (px, py, pc) in enumerate(peers):
            pltpu.make_async_remote_copy(
                src_ref=g_ref.at[me], dst_ref=recv_ref.at[4 * px + 2 * py + pc], send_sem=send_sems.at[k],
                recv_sem=recv_sems.at[k], device_id=(x, y, c), device_id_type=MESH).wait_recv()
        for cp in sends():
            cp.wait_send()
        own().wait()

    return start, finish


def _a2a_scratch():
    return [pltpu.SemaphoreType.DMA((NDEV - 1,)), pltpu.SemaphoreType.DMA((NDEV - 1,)), pltpu.SemaphoreType.DMA]


def _reduce_scatter2(g, small, name):
    _, r, n = g.shape
    ch = 16
    nch = r // ch
    ns = len(small)
    npart = 4
    bounds = [nch * p // npart for p in range(npart + 1)]

    def body(g_ref, *rest):
        v_refs, out_ref, vout_refs = rest[:ns], rest[ns], rest[ns + 1:2 * ns + 1]
        a_ref, h_ref, b_ref, s1_send, s1_recv, s2_send, s2_recv = rest[2 * ns + 1:2 * ns + 8]
        gather_sems = rest[2 * ns + 8:]
        gathers = [_small_gather_plan(v_refs[i], vout_refs[i], *gather_sems[2 * i:2 * i + 2]) for i in range(ns)]
        for start, _ in gathers:
            start()
        x, y, c = _my_pos()
        sibling = (x, y, 1 - c)
        s1 = []
        for p in range(npart):
            rows = pl.ds(bounds[p] * ch, (bounds[p + 1] - bounds[p]) * ch)
            part = []
            for j in range(4):
                cp = pltpu.make_async_remote_copy(
                    src_ref=g_ref.at[2 * j + (1 - c), rows], dst_ref=a_ref.at[j, rows],
                    send_sem=s1_send.at[4 * p + j], recv_sem=s1_recv.at[4 * p + j], device_id=sibling, device_id_type=MESH)
                cp.start()
                part.append(cp)
            s1.append(part)

        def add1(i, _):
            rr = pl.ds(pl.multiple_of(i * ch, ch), ch)
            for j in range(4):
                h_ref[j, rr, :] = (g_ref[2 * j + c, rr, :].astype(f32) + a_ref[j, rr, :].astype(f32)).astype(bf16)
            return 0
        mychip = 2 * x + y
        s2 = []
        for p in range(npart):
            rows = pl.ds(bounds[p] * ch, (bounds[p + 1] - bounds[p]) * ch)
            for cp in s1[p]:
                cp.wait_recv()
            lax.fori_loop(bounds[p], bounds[p + 1], add1, 0)
            part = []
            for m in range(1, 4):
                mx, my_ = (m >> 1) & 1, m & 1
                px, py = x ^ mx, y ^ my_
                cp = pltpu.make_async_remote_copy(
                    src_ref=h_ref.at[2 * px + py, rows], dst_ref=b_ref.at[m - 1, rows],
                    send_sem=s2_send.at[3 * p + m - 1], recv_sem=s2_recv.at[3 * p + m - 1],
                    device_id=(px, py, c), device_id_type=MESH)
                cp.start()
                part.append(cp)
            s2.append(part)

        def add2(i, _):
            rr = pl.ds(pl.multiple_of(i * ch, ch), ch)
            acc = h_ref[mychip, rr, :].astype(f32)
            for m in range(3):
                acc = acc + b_ref[m, rr, :].astype(f32)
            out_ref[rr, :] = acc
            return 0
        for p in range(npart):
            for cp in s2[p]:
                cp.wait_recv()
            lax.fori_loop(bounds[p], bounds[p + 1], add2, 0)
        for part in s1 + s2:
            for cp in part:
                cp.wait_send()
        for _, finish in gathers:
            finish()

    vmem = pl.BlockSpec(memory_space=pltpu.VMEM)
    return pl.pallas_call(
        body, name=name,
        out_shape=[jax.ShapeDtypeStruct((r, n), f32)] + [jax.ShapeDtypeStruct((NDEV,) + v.shape, v.dtype) for v in small],
        in_specs=[vmem] * (1 + ns),
        out_specs=[vmem] * (1 + ns),
        scratch_shapes=[pltpu.VMEM((4, r, n), bf16), pltpu.VMEM((4, r, n), bf16), pltpu.VMEM((3, r, n), bf16),
                        pltpu.SemaphoreType.DMA((4 * npart,)), pltpu.SemaphoreType.DMA((4 * npart,)),
                        pltpu.SemaphoreType.DMA((3 * npart,)), pltpu.SemaphoreType.DMA((3 * npart,))]
                       + _small_gather_scratch() * ns,
        compiler_params=pltpu.CompilerParams(vmem_limit_bytes=VMEM_LIMIT),
    )(g, *small)


def _head(w_in_sh, c8, cctx8, w_mod_sh, b_sh, convpay, bias_base, bias_valid):
    nmod = w_mod_sh.shape[1]
    npay = convpay.shape[0]

    def body(w_ref, c_ref, cc_ref, wm_ref, b_ref, pay_ref, base_ref, valid_ref,
             wout_ref, call_ref, mall_ref, pall_ref, bias_ref, mod_s,
             ag_send, ag_recv, ag_local, c_send, c_recv, m_send, m_recv, p_send, p_recv):
        start, forward, finish = _ag2_plan([w_ref], [wout_ref], ag_send, ag_recv, ag_local)
        c_start, c_finish = _small_gather_plan(c_ref, call_ref, c_send, c_recv)
        m_start, m_finish = _small_gather_plan(mod_s, mall_ref, m_send, m_recv)
        p_start, p_finish = _small_gather_plan(pay_ref, pall_ref, p_send, p_recv)
        c_start()
        start()
        p_start()
        _write_bias_table(base_ref, valid_ref, bias_ref)
        c_finish()
        acc = jnp.zeros((16, nmod), f32)
        for j in range(D // 128):
            rows = jnp.concatenate([call_ref[:, j, :], cc_ref[j:j + 1, :], jnp.zeros((7, 128), f32)], axis=0)
            act = (rows * _sigmoid(rows)).astype(bf16)
            acc = acc + _dot(act, wm_ref[j * 128:(j + 1) * 128, :].astype(bf16))
        mod_s[...] = acc + b_ref[...]
        m_start()
        forward()
        finish()
        m_finish()
        p_finish()

    vmem = pl.BlockSpec(memory_space=pltpu.VMEM)
    return pl.pallas_call(
        body, name="head",
        out_shape=[jax.ShapeDtypeStruct((NDEV * w_in_sh.shape[0], D), bf16), jax.ShapeDtypeStruct((NDEV, 8, 128), f32),
                   jax.ShapeDtypeStruct((NDEV, 16, nmod), f32), jax.ShapeDtypeStruct((NDEV, npay, 128), f32),
                   jax.ShapeDtypeStruct((8, NH, GW, WR * GW), f32)],
        in_specs=[vmem] * 8, out_specs=[vmem] * 5,
        scratch_shapes=[pltpu.VMEM((16, nmod), f32)] + _ag2_scratch(1) + _small_gather_scratch() * 3,
        compiler_params=pltpu.CompilerParams(vmem_limit_bytes=VMEM_LIMIT),
    )(w_in_sh, c8, cctx8, w_mod_sh, b_sh, convpay, bias_base, bias_valid)


def _mod_bwd(cvec, dm_sh, w_sh):
    def body(c_ref, dm_ref, w_ref, gw_ref, gc_ref):
        cv = c_ref[...]
        act = (cv * _sigmoid(cv)).astype(bf16)
        gw_ref[...] = _dg(act, dm_ref[...].astype(bf16), TN)
        gc_ref[...] = _dg(dm_ref[8:16, :].astype(bf16), w_ref[...].astype(bf16), NT)
    return pl.pallas_call(
        body, name="mod_bwd",
        out_shape=(jax.ShapeDtypeStruct(w_sh.shape, f32), jax.ShapeDtypeStruct((8, D), f32)))(cvec, dm_sh, w_sh)


def _sum_rows8(a, name):
    n = a.shape[1]

    def body(a_ref, o_ref):
        acc = a_ref[0]
        for d in range(1, NDEV):
            acc = acc + a_ref[d]
        o_ref[...] = acc
    return pl.pallas_call(body, name=name, out_shape=jax.ShapeDtypeStruct((n, 128), f32))(a)


def _in_proj(x0, ctx0, g1, modv, w_inT, shards):
    tm = 256
    nt = TA // tm
    nx = T // tm
    na = len(shards)

    def body(x_ref, c_ref, g_ref, mod_ref, w_ref, *rest):
        x_refs, (h_ref, q_ref, k_ref, v_ref, a_ref, gg_ref) = rest[:na], rest[na:na + 6]
        out_refs, sems = rest[na + 6:2 * na + 6], rest[2 * na + 6:]
        i = pl.program_id(0)
        if na:
            start, forward, finish = _ag2_plan(x_refs, out_refs, *sems)
            pl.when(i == 0)(start)
            pl.when(i == nt - 2)(forward)
        is_ctx = i == nt - 1
        xv = jnp.where(is_ctx, c_ref[...], x_ref[...])
        rstd = lax.rsqrt(jnp.mean(xv * xv, axis=-1, keepdims=True) + EPS)
        sh = jnp.where(is_ctx, mod_ref[6:7, :], mod_ref[0:1, :])
        sc = jnp.where(is_ctx, mod_ref[7:8, :], mod_ref[1:2, :])
        h = ((xv * rstd * g_ref[...]) * (1.0 + sc) + sh).astype(bf16)
        h_ref[...] = h
        for j, o_ref in enumerate((q_ref, k_ref, v_ref, a_ref, gg_ref)):
            o_ref[...] = _dg(h, w_ref[j * DA:(j + 1) * DA, :], NT).astype(o_ref.dtype)
        if na:
            pl.when(is_ctx)(finish)

    row = lambda w: pl.BlockSpec((tm, w), lambda i: (i, 0))
    hbm = pl.BlockSpec(memory_space=pl.ANY)
    return pl.pallas_call(
        body, name="in_proj", grid=(nt,),
        in_specs=[pl.BlockSpec((tm, D), lambda i: (jnp.minimum(i, nx - 1), 0)), _full((TC, D)),
                  _full((1, D)), _full((8, D)), _full((5 * DA, D))] + [hbm] * na,
        out_specs=[row(D), row(DA), row(DA), row(DA), row(DA), row(DA)] + [hbm] * na,
        out_shape=[jax.ShapeDtypeStruct((TA, D), bf16)] + [jax.ShapeDtypeStruct((TA, DA), bf16)] * 3
                  + [jax.ShapeDtypeStruct((TA, DA), f32)] * 2
                  + [jax.ShapeDtypeStruct((NDEV * sh.shape[0], sh.shape[1]), sh.dtype) for sh in shards],
        scratch_shapes=_ag2_scratch(na) if na else [],
        compiler_params=_cp("arbitrary"),
    )(x0, ctx0, g1, modv, w_inT, *shards)


def _win_start(r):
    return jnp.clip(r - WR // 2, 0, GW - WR)


def _pattern(r):
    return _win_start(r) - r + (WR - 1)


def _bias_table_inputs(rpb):
    qc = np.arange(GW)[:, None]
    kc = np.arange(GW)[None, :]
    cs = np.clip(qc - NCOL // 2, 0, GW - NCOL)
    valid = np.tile(((kc >= cs) & (kc < cs + NCOL)).astype(np.int32), (1, WR))
    pad = jnp.pad(rpb, ((0, 0), (0, 0), (0, GW - (2 * NCOL - 1))))
    base = jnp.stack([pad[:, p:p + WR, :].reshape(NH, WR * GW) for p in range(8)])
    return base, jnp.asarray(valid)


def _write_bias_table(base_ref, valid_ref, o_ref):
    ok = valid_ref[...] != 0
    for p in range(8):
        for h in range(NH):
            row = jnp.broadcast_to(base_ref[p, h:h + 1, :], (GW, WR * GW))
            skew = pltpu.roll(row, WR * GW - (NCOL - 1), 1, stride=1, stride_axis=0)
            o_ref[p, h] = jnp.where(ok, skew, NEG)


def _rpb_tables():
    lane_map = np.zeros((WR * GW, WR, 2 * NCOL - 1), np.float32)
    for i in range(WR):
        for t in range(GW):
            if t >= GW - NCOL:
                lane_map[i * GW + t, i, t - (GW - NCOL)] = 1.0
            elif t < NCOL - 1:
                lane_map[i * GW + t, (i - 1) % WR, t + NCOL] = 1.0
    p = np.arange(8)[:, None]
    i = np.arange(WR)[None, :]
    r_hot = ((p + i)[:, :, None] == np.arange(2 * WR - 1)[None, None, :]).astype(np.float32)
    return lane_map, r_hot


CONV_TT = 512
HALO = 16


def _halo_specs(tt, w, nrows_blocks):
    per = tt // HALO
    prev = pl.BlockSpec((HALO, w), lambda i: (jnp.maximum(i * per - 1, 0), 0))
    cur = pl.BlockSpec((tt, w), lambda i: (i, 0))
    nxt = pl.BlockSpec((HALO, w), lambda i: (jnp.minimum((i + 1) * per, nrows_blocks - 1), 0))
    return [prev, cur, nxt]


def _shifted_copies(rot, wn):
    for b in range(1, 8):
        rot[b, 0:wn - 8, :] = rot[0, pl.ds(b, wn - 8), :]


ATTN_RPS = 4
MIX_FORWARD_STEP = 14


def _stack_pair(x2, lo):
    z = jnp.zeros_like(x2)
    return jnp.concatenate([jnp.where(lo, x2, z), jnp.where(lo, z, x2)], axis=0)


def _mix_fwd(q, k, v, bias_tab, a, g, conv_w, conv_b, ln_g, ln_b, shards):
    na = len(shards)
    rps = ATTN_RPS
    nsteps = GW // rps
    tt = CONV_TT
    nt = T // tt
    every = nsteps // nt
    sub = 32
    wn = tt + 2 * HALO

    def body(q_ref, k_ref, v_ref, *rest):
        b_refs, rest = rest[:rps], rest[rps:]
        (ap, ac, an, gp, gc, gn, w_ref, cb_ref, lg_ref, lb_ref), rest = rest[:10], rest[10:]
        x_refs, (y_ref, lse_ref, ycv_ref, cv_ref), out_refs = rest[:na], rest[na:na + 4], rest[na + 4:2 * na + 4]
        rot, *sems = rest[2 * na + 4:]
        i = pl.program_id(0)
        if na:
            start, forward, finish = _ag2_plan(x_refs, out_refs, *sems)
            pl.when(i == 0)(start)
            pl.when(i == MIX_FORWARD_STEP)(forward)

        @pl.when(i % every == 0)
        def _():
            t = i // every
            rot[0, 0:HALO, :] = jnp.where(t > 0, ap[...] * _sigmoid(gp[...]), 0.0)
            rot[0, HALO:HALO + tt, :] = ac[...] * _sigmoid(gc[...])
            rot[0, HALO + tt:, :] = jnp.where(t < nt - 1, an[...] * _sigmoid(gn[...]), 0.0)
            _shifted_copies(rot, wn)
            w = w_ref[...]
            for s in range(tt // sub):
                acc = jnp.zeros((sub, DA), f32)
                for j in range(CW):
                    a8, b8 = divmod(1 + j, 8)
                    acc = acc + rot[b8, pl.ds(s * sub + 8 * a8, sub), :] * w[j:j + 1, :]
                cv = acc + cb_ref[...]
                cv_ref[pl.ds(s * sub, sub), :] = cv
                mu = jnp.mean(cv, axis=-1, keepdims=True)
                xc = cv - mu
                rstd = lax.rsqrt(jnp.mean(xc * xc, axis=-1, keepdims=True) + EPS)
                z = xc * rstd * lg_ref[...] + lb_ref[...]
                ycv_ref[pl.ds(s * sub, sub), :] = (z * _sigmoid(z)).astype(bf16)

        lo = lax.broadcasted_iota(jnp.int32, (GW, 2 * HD), 1) < HD
        for rr in range(rps):
            rows = slice(rr * GW, (rr + 1) * GW)
            ks = pl.multiple_of(_win_start(i * rps + rr) * GW, GW)
            qq = q_ref[rows, :]
            kv, scores = [], []
            for pr in range(NH // 2):
                ps = slice(pr * 2 * HD, (pr + 1) * 2 * HD)
                qst = _stack_pair(qq[:, ps], lo)
                kw, kc = k_ref[pl.ds(ks, WR * GW), ps], k_ref[T:TA, ps]
                kv.append((v_ref[pl.ds(ks, WR * GW), ps], v_ref[T:TA, ps]))
                bias2 = b_refs[rr][0, 2 * pr:2 * pr + 2].reshape(2 * GW, WR * GW)
                scores.append((_dg(qst, kw, NT) * SCALE + bias2, _dg(qst, kc, NT) * SCALE))
            probs = []
            for pr, (sl, sc) in enumerate(scores):
                m = jnp.maximum(jnp.max(sl, axis=-1, keepdims=True), jnp.max(sc, axis=-1, keepdims=True))
                pl_ = jnp.exp(sl - m)
                pc = jnp.exp(sc - m)
                l = jnp.sum(pl_, axis=-1, keepdims=True) + jnp.sum(pc, axis=-1, keepdims=True)
                lse = m + jnp.log(l)
                lse_ref[rows, 2 * pr:2 * pr + 1] = lse[0:GW]
                lse_ref[rows, 2 * pr + 1:2 * pr + 2] = lse[GW:]
                probs.append((pl_.astype(bf16), pc.astype(bf16), 1.0 / l))
            for pr in range(NH // 2):
                ps = slice(pr * 2 * HD, (pr + 1) * 2 * HD)
                vw, vc = kv[pr]
                pb, cb, rl = probs[pr]
                o = (_dot(pb, vw) + _dot(cb, vc)) * rl
                y_ref[rows, ps] = jnp.where(lo, o[0:GW], o[GW:]).astype(bf16)
        if na:
            pl.when(i == nsteps - 1)(finish)

    hbm = pl.BlockSpec(memory_space=pl.ANY)
    rowq = lambda w: pl.BlockSpec((rps * GW, w), lambda i: (i, 0))
    bias = [pl.BlockSpec((1, NH, GW, WR * GW), lambda i, rr=rr: (_pattern(i * rps + rr), 0, 0, 0)) for rr in range(rps)]
    per = tt // HALO
    halo = lambda: [pl.BlockSpec((HALO, DA), lambda i: (jnp.maximum((i // every) * per - 1, 0), 0)),
                    pl.BlockSpec((tt, DA), lambda i: (i // every, 0)),
                    pl.BlockSpec((HALO, DA), lambda i: (jnp.minimum((i // every + 1) * per, T // HALO - 1), 0))]
    tile = lambda: pl.BlockSpec((tt, DA), lambda i: (i // every, 0))
    return pl.pallas_call(
        body, name="mix_fwd", grid=(nsteps,),
        in_specs=[rowq(DA), _full((TA, DA)), _full((TA, DA))] + bias + halo() + halo()
                 + [_full((CW, DA)), _full((1, DA)), _full((1, DA)), _full((1, DA))] + [hbm] * na,
        out_specs=[rowq(DA), rowq(NH), tile(), tile()] + [hbm] * na,
        out_shape=[jax.ShapeDtypeStruct((T, DA), bf16), jax.ShapeDtypeStruct((T, NH), f32),
                   jax.ShapeDtypeStruct((T, DA), bf16), jax.ShapeDtypeStruct((T, DA), f32)]
                  + [jax.ShapeDtypeStruct((NDEV * s.shape[0], s.shape[1]), s.dtype) for s in shards],
        scratch_shapes=[pltpu.VMEM((8, wn, DA), f32)] + (_ag2_scratch(na) if na else []),
        compiler_params=_cp("arbitrary"),
    )(q, k, v, *([bias_tab] * rps), a, a, a, g, g, g, conv_w, conv_b, ln_g, ln_b, *shards)


def _out_proj(xa, y_na, y_cv, w_out, modv, g2):
    tm = 512

    def body(x_ref, ya_ref, yc_ref, w_ref, mod_ref, g_ref, x1_ref, pj_ref, h2_ref):
        proj = _dot(ya_ref[...], w_ref[0:DA, :]) + _dot(yc_ref[...], w_ref[DA:D, :])
        x1 = x_ref[...] + mod_ref[2:3, :] * proj
        x1_ref[...] = x1
        pj_ref[...] = proj.astype(bf16)
        rstd = lax.rsqrt(jnp.mean(x1 * x1, axis=-1, keepdims=True) + EPS)
        h2_ref[...] = ((x1 * rstd * g_ref[...]) * (1.0 + mod_ref[4:5, :]) + mod_ref[3:4, :]).astype(bf16)

    row = lambda w: pl.BlockSpec((tm, w), lambda i: (i, 0))
    return pl.pallas_call(
        body, name="out_proj", grid=(T // tm,),
        in_specs=[row(D), row(DA), row(DA), _full((D, D)), _full((8, D)), _full((1, D))],
        out_specs=[row(D), row(D), row(D)],
        out_shape=[jax.ShapeDtypeStruct((T, D), f32), jax.ShapeDtypeStruct((T, D), bf16), jax.ShapeDtypeStruct((T, D), bf16)],
        compiler_params=_cp("parallel"),
    )(xa, y_na, y_cv, w_out, modv, g2)


FFN_TT = 2048
FFN_CT = 256
FFN_NC = F // FFN_CT
FFN_SUB = 32


def _row_neighbours(ref, r, n):
    blk = ref[pl.ds(r - 8, n + 16), :]
    return blk[8:8 + n, :], pltpu.roll(blk, 1, 0)[8:8 + n, :], pltpu.roll(blk, n + 15, 0)[8:8 + n, :]


def _ffn_specs(tt, ct, by_token_first):
    tc = (lambda f: (lambda t, c: f(t, c))) if by_token_first else (lambda f: (lambda c, t: f(t, c)))
    per = tt // HALO
    halo = [pl.BlockSpec((HALO, D), tc(lambda t, c: (jnp.maximum(t * per - 1, 0), 0))),
            pl.BlockSpec((tt, D), tc(lambda t, c: (t, 0))),
            pl.BlockSpec((HALO, D), tc(lambda t, c: (jnp.minimum((t + 1) * per, T // HALO - 1), 0)))]
    weights = [pl.BlockSpec((ct, D), tc(lambda t, c: (c, 0))), pl.BlockSpec((ct, D), tc(lambda t, c: (c + FFN_NC, 0))),
               pl.BlockSpec((3, ct), tc(lambda t, c: (0, c))), pl.BlockSpec((3, ct), tc(lambda t, c: (0, c + FFN_NC))),
               pl.BlockSpec((1, ct), tc(lambda t, c: (0, c))), pl.BlockSpec((1, ct), tc(lambda t, c: (0, c + FFN_NC))),
               pl.BlockSpec((ct, D), tc(lambda t, c: (c, 0)))]
    return halo, weights


def _ffn_fwd(h2, w_upT, fcw, fcb, w_down):
    tt, ct = FFN_TT, FFN_CT
    nt = T // tt
    wn = tt + 2 * HALO
    half = tt // 2

    def body(hp, hc, hn, wg_ref, wv_ref, cwg_ref, cwv_ref, cbg_ref, cbv_ref, wd_ref, o_ref, u_ref, u2_ref, hwin, uwin, act):
        t = pl.program_id(0)
        c = pl.program_id(1)

        @pl.when(c == 0)
        def _():
            hwin[0:HALO, :] = jnp.where(t > 0, hp[...], jnp.zeros_like(hp[...]))
            hwin[HALO:HALO + tt, :] = hc[...]
            hwin[HALO + tt:, :] = jnp.where(t < nt - 1, hn[...], jnp.zeros_like(hn[...]))
            o_ref[...] = jnp.zeros_like(o_ref)

        for r0, r1 in ((0, half + 2 * HALO), (half + 2 * HALO, wn)):
            hw = hwin[r0:r1, :]
            uwin[r0:r1, :ct] = _dg(hw, wg_ref[...], NT)
            uwin[r0:r1, ct:] = _dg(hw, wv_ref[...], NT)
        cw = jnp.concatenate([cwg_ref[...], cwv_ref[...]], axis=1)
        cb = jnp.concatenate([cbg_ref[...], cbv_ref[...]], axis=1)
        for p in range(2):
            for r in range(p * half, (p + 1) * half, FFN_SUB):
                uc, prev, nxt = _row_neighbours(uwin, HALO + r, FFN_SUB)
                u2 = prev * cw[0:1, :] + uc * cw[1:2, :] + nxt * cw[2:3, :] + cb
                u_ref[r:r + FFN_SUB, :] = uc.astype(bf16)
                u2_ref[r:r + FFN_SUB, :] = u2
                gate = u2[:, :ct]
                act[r:r + FFN_SUB, :] = (gate * _sigmoid(gate) * u2[:, ct:]).astype(bf16)
            rows = slice(p * half, (p + 1) * half)
            o_ref[rows, :] += _dot(act[rows, :], wd_ref[...])

    halo, weights = _ffn_specs(tt, ct, True)
    pair = pl.BlockSpec((tt, 2 * ct), lambda t, c: (t, c))
    return pl.pallas_call(
        body, name="ffn_fwd", grid=(nt, FFN_NC),
        in_specs=halo + weights,
        out_specs=[pl.BlockSpec((tt, D), lambda t, c: (t, 0)), pair, pair],
        out_shape=[jax.ShapeDtypeStruct((T, D), f32), jax.ShapeDtypeStruct((T, F2), bf16), jax.ShapeDtypeStruct((T, F2), f32)],
        scratch_shapes=[pltpu.VMEM((wn, D), bf16), pltpu.VMEM((wn, 2 * ct), f32), pltpu.VMEM((tt, ct), bf16)],
        compiler_params=_cp("parallel", "arbitrary"),
    )(h2, h2, h2, w_upT, w_upT, fcw, fcw, fcb, fcb, w_down)


def _loss_bwd(ffn, x1, tgt, modv, gf):
    tm = 1024
    nt = T // tm

    def body(f_ref, x1_ref, t_ref, mod_ref, g_ref, dx2_ref, df_ref, s_ref):
        i = pl.program_id(0)

        @pl.when(i == 0)
        def _():
            s_ref[...] = jnp.zeros_like(s_ref)

        ff = f_ref[...]
        gt2 = mod_ref[5:6, :]
        x2 = x1_ref[...] + gt2 * ff
        rstd = lax.rsqrt(jnp.mean(x2 * x2, axis=-1, keepdims=True) + EPS)
        xh = x2 * rstd
        gfv = g_ref[...]
        e = xh * gfv - t_ref[...]
        dy = e * (1.0 / D)
        dxh = dy * gfv
        dx2 = rstd * (dxh - xh * jnp.mean(dxh * xh, axis=-1, keepdims=True))
        dx2_ref[...] = dx2
        df_ref[...] = (dx2 * gt2).astype(bf16)
        s_ref[0:1, :] += jnp.sum(dy * xh, axis=0, keepdims=True)
        s_ref[1:2, :] += jnp.sum(dx2 * ff, axis=0, keepdims=True)
        s_ref[2:3, :] += jnp.sum(e * e, axis=0, keepdims=True)

        @pl.when(i == nt - 1)
        def _():
            tot = jnp.sum(s_ref[2:3, :], axis=-1, keepdims=True) * (0.5 / D)
            s_ref[3:4, :] = jnp.broadcast_to(tot, (1, D))

    row = lambda: pl.BlockSpec((tm, D), lambda i: (i, 0))
    return pl.pallas_call(
        body, name="loss_bwd", grid=(nt,),
        in_specs=[row(), row(), row(), _full((8, D)), _full((1, D))],
        out_specs=[row(), row(), _full((8, D))],
        out_shape=[jax.ShapeDtypeStruct((T, D), f32), jax.ShapeDtypeStruct((T, D), bf16), jax.ShapeDtypeStruct((8, D), f32)],
        compiler_params=_cp("arbitrary"),
    )(ffn, x1, tgt, modv, gf)


def _ffn_bwd(h2, dffn, u_t, u2_t, fcw, w_down):
    tt, ct = FFN_TT, FFN_CT
    nt = T // tt
    wn = tt + 2 * HALO
    half = tt // 2

    def body(dp, dc, dn, hc, uc_ref, u2p, u2c, u2n, cwg_ref, cwv_ref, wd_ref,
             dug_ref, duv_ref, dwu_ref, dwd_ref, dcwg_ref, dcwv_ref, dcbg_ref, dcbv_ref,
             dwin, d2win, dawin, accu, accd, act, du):
        t = pl.program_id(1)
        first, last = t == 0, t == nt - 1
        zero = jnp.zeros((HALO, D), bf16)
        dwin[0:HALO, :] = jnp.where(first, zero, dp[...])
        dwin[HALO:HALO + tt, :] = dc[...]
        dwin[HALO + tt:, :] = jnp.where(last, zero, dn[...])

        @pl.when(first)
        def _():
            for r in (accu, accd, dcwg_ref, dcwv_ref, dcbg_ref, dcbv_ref):
                r[...] = jnp.zeros_like(r)

        cw = jnp.concatenate([cwg_ref[...], cwv_ref[...]], axis=1)
        split = half + 2 * HALO
        for r0, r1 in ((0, split), (split, wn)):
            dawin[r0:r1, :] = _dg(dwin[r0:r1, :], wd_ref[...], NT)

        def grads(u2v, dact):
            gate, val = u2v[:, :ct], u2v[:, ct:]
            sg = _sigmoid(gate)
            silu = gate * sg
            return dact * val * (sg * (1.0 + gate * (1.0 - sg))), dact * silu, silu * val

        for blk, r0 in ((u2p, 0), (u2n, HALO + tt)):
            dgate, dval, _ = grads(blk[...], dawin[r0:r0 + HALO, :])
            d2win[r0:r0 + HALO, :ct] = dgate
            d2win[r0:r0 + HALO, ct:] = dval
        for p in range(2):
            rows = slice(p * half, (p + 1) * half)
            for r in range(p * half, (p + 1) * half, FFN_SUB):
                dgate, dval, av = grads(u2c[r:r + FFN_SUB, :], dawin[HALO + r:HALO + r + FFN_SUB, :])
                d2win[HALO + r:HALO + r + FFN_SUB, :ct] = dgate
                d2win[HALO + r:HALO + r + FFN_SUB, ct:] = dval
                act[r:r + FFN_SUB, :] = av.astype(bf16)
            accd[...] += _dg(act[rows, :], dc[rows, :], TN)

        def fold8(x):
            out = x[0:8]
            for k in range(8, FFN_SUB, 8):
                out = out + x[k:k + 8]
            return out

        for p in range(2):
            rows = slice(p * half, (p + 1) * half)
            sums = [jnp.zeros((8, 2 * ct), f32) for _ in range(4)]
            for r in range(p * half, (p + 1) * half, FFN_SUB):
                d2c, d2m, d2p = _row_neighbours(d2win, HALO + r, FFN_SUB)
                ucur = uc_ref[r:r + FFN_SUB, :].astype(f32)
                sums[0] = sums[0] + fold8(d2c)
                for kk, dd in enumerate((d2p, d2c, d2m)):
                    sums[1 + kk] = sums[1 + kk] + fold8(ucur * dd)
                du[r:r + FFN_SUB, :] = (d2p * cw[0:1, :] + d2c * cw[1:2, :] + d2m * cw[2:3, :]).astype(bf16)
            dcb = jnp.sum(sums[0], axis=0, keepdims=True)
            dcbg_ref[...] += dcb[:, :ct]
            dcbv_ref[...] += dcb[:, ct:]
            for kk in range(3):
                dck = jnp.sum(sums[1 + kk], axis=0, keepdims=True)
                dcwg_ref[kk:kk + 1, :] += dck[:, :ct]
                dcwv_ref[kk:kk + 1, :] += dck[:, ct:]
            dug_ref[rows, :] = du[rows, :ct]
            duv_ref[rows, :] = du[rows, ct:]
            accu[...] += _dg(du[rows, :], hc[rows, :], TN)

        @pl.when(last)
        def _():
            dwu_ref[0] = accu[0:ct, :].astype(bf16)
            dwu_ref[1] = accu[ct:, :].astype(bf16)
            dwd_ref[...] = accd[...].astype(bf16)

    per = tt // HALO
    prev = lambda w: (lambda c, t: (jnp.maximum(t * per - 1, 0), c if w else 0))
    nxt = lambda w: (lambda c, t: (jnp.minimum((t + 1) * per, T // HALO - 1), c if w else 0))
    tile = lambda: pl.BlockSpec((ct, D), lambda c, t: (c, 0))
    lane = lambda r, off: pl.BlockSpec((r, ct), lambda c, t: (0, c + off))
    return pl.pallas_call(
        body, name="ffn_bwd", grid=(FFN_NC, nt),
        in_specs=[pl.BlockSpec((HALO, D), prev(False)), pl.BlockSpec((tt, D), lambda c, t: (t, 0)), pl.BlockSpec((HALO, D), nxt(False)),
                  pl.BlockSpec((tt, D), lambda c, t: (t, 0)), pl.BlockSpec((tt, 2 * ct), lambda c, t: (t, c)),
                  pl.BlockSpec((HALO, 2 * ct), prev(True)), pl.BlockSpec((tt, 2 * ct), lambda c, t: (t, c)),
                  pl.BlockSpec((HALO, 2 * ct), nxt(True)), lane(3, 0), lane(3, FFN_NC), tile()],
        out_specs=[pl.BlockSpec((tt, ct), lambda c, t: (t, c)), pl.BlockSpec((tt, ct), lambda c, t: (t, c)),
                   pl.BlockSpec((2, ct, D), lambda c, t: (0, c, 0)), tile(), lane(3, 0), lane(3, 0), lane(1, 0), lane(1, 0)],
        out_shape=[jax.ShapeDtypeStruct((T, F), bf16), jax.ShapeDtypeStruct((T, F), bf16),
                   jax.ShapeDtypeStruct((2, F, D), bf16), jax.ShapeDtypeStruct((F, D), bf16),
                   jax.ShapeDtypeStruct((3, F), f32), jax.ShapeDtypeStruct((3, F), f32),
                   jax.ShapeDtypeStruct((1, F), f32), jax.ShapeDtypeStruct((1, F), f32)],
        scratch_shapes=[pltpu.VMEM((wn, D), bf16), pltpu.VMEM((wn, 2 * ct), f32), pltpu.VMEM((wn, ct), f32),
                        pltpu.VMEM((2 * ct, D), f32), pltpu.VMEM((ct, D), f32),
                        pltpu.VMEM((tt, ct), bf16), pltpu.VMEM((tt, 2 * ct), bf16)],
        compiler_params=_cp("parallel", "arbitrary"),
    )(dffn, dffn, dffn, h2, u_t, u2_t, u2_t, u2_t, fcw, fcw, w_down)


def _norm_bwd(dh, xv, gain, sh_sc, rstd):
    xh = xv * rstd
    n = xh * gain
    dn = dh * (1.0 + sh_sc)
    dxh = dn * gain
    dx = rstd * (dxh - xh * jnp.mean(dxh * xh, axis=-1, keepdims=True))
    return (dx, jnp.sum(dh, axis=0, keepdims=True), jnp.sum(dh * n, axis=0, keepdims=True),
            jnp.sum(dn * xh, axis=0, keepdims=True))


def _norm2_bwd(dug, duv, w_upT, x1, dx2, proj, w_out, y_na, y_cv, modv, g2):
    tm = 512
    nt = T // tm

    def body(dug_ref, duv_ref, w_ref, x1_ref, dx2_ref, pj_ref, wo_ref, ya_ref, yc_ref, mod_ref, g_ref,
             dx1_ref, dya_ref, dyc_ref, dwo_ref, s_ref, acc):
        i = pl.program_id(0)

        @pl.when(i == 0)
        def _():
            s_ref[...] = jnp.zeros_like(s_ref)
            acc[...] = jnp.zeros_like(acc)

        dh2 = _dot(dug_ref[...], w_ref[0:F, :]) + _dot(duv_ref[...], w_ref[F:F2, :])
        x1 = x1_ref[...]
        rstd = lax.rsqrt(jnp.mean(x1 * x1, axis=-1, keepdims=True) + EPS)
        dxn, dsh, dsc, dgn = _norm_bwd(dh2, x1, g_ref[...], mod_ref[4:5, :], rstd)
        dx1 = dx2_ref[...] + dxn
        dx1_ref[...] = dx1
        dpj = (dx1 * mod_ref[2:3, :]).astype(bf16)
        dyc = _dg(dpj, wo_ref[...], NT)
        dya_ref[...] = dyc[:, :DA].astype(bf16)
        dyc_ref[...] = dyc[:, DA:]
        acc[0:DA, :] += _dg(ya_ref[...], dpj, TN)
        acc[DA:D, :] += _dg(yc_ref[...], dpj, TN)

        @pl.when(i == nt - 1)
        def _():
            dwo_ref[...] = acc[...].astype(bf16)

        s_ref[0:1, :] += dsh
        s_ref[1:2, :] += dsc
        s_ref[2:3, :] += dgn
        s_ref[3:4, :] += jnp.sum(dx1 * pj_ref[...].astype(f32), axis=0, keepdims=True)

    row = lambda w: pl.BlockSpec((tm, w), lambda i: (i, 0))
    return pl.pallas_call(
        body, name="norm2_bwd", grid=(nt,),
        in_specs=[row(F), row(F), _resident((F2, D)), row(D), row(D), row(D), _resident((D, D)), row(DA), row(DA),
                  _full((8, D)), _full((1, D))],
        out_specs=[row(D), row(DA), row(DA), _full((D, D)), _full((8, D))],
        out_shape=[jax.ShapeDtypeStruct((T, D), f32), jax.ShapeDtypeStruct((T, DA), bf16), jax.ShapeDtypeStruct((T, DA), f32),
                   jax.ShapeDtypeStruct((D, D), bf16), jax.ShapeDtypeStruct((8, D), f32)],
        scratch_shapes=[pltpu.VMEM((D, D), f32)],
        compiler_params=_cp("arbitrary"),
    )(dug, duv, w_upT, x1, dx2, proj, w_out, y_na, y_cv, modv, g2)


def _conf_bwd(a, g, cv, dy, conv_w, ln_g, ln_b, blocks):
    tt = CONV_TT
    nt = T // tt
    sub = 32
    wn = tt + 2 * HALO
    nb = len(blocks)

    def body(ap, ac, an, gp, gc, gn, cp_, cc, cn, dp, dc, dn, w_ref, lg_ref, lb_ref, *rest):
        g_refs, (da_ref, dg_ref, dcw_ref, s_ref) = rest[:nb], rest[nb:nb + 4]
        recv_refs, (urot, drot, wacc), a2a_sems = rest[nb + 4:2 * nb + 4], rest[2 * nb + 4:2 * nb + 7], rest[2 * nb + 7:]
        i = pl.program_id(0)
        first, last = i == 0, i == nt - 1
        plans = [_a2a_plan(g_refs[k], recv_refs[k], *a2a_sems[3 * k:3 * k + 3]) for k in range(nb)]
        for start, _ in plans:
            pl.when(first)(start)

        @pl.when(first)
        def _():
            s_ref[...] = jnp.zeros_like(s_ref)
            wacc[...] = jnp.zeros_like(wacc)

        lg, lb = lg_ref[...], lb_ref[...]

        def ln_bwd(cvv, dyv):
            mu = jnp.mean(cvv, axis=-1, keepdims=True)
            xc = cvv - mu
            rstd = lax.rsqrt(jnp.mean(xc * xc, axis=-1, keepdims=True) + EPS)
            yn = xc * rstd
            z = yn * lg + lb
            sz = _sigmoid(z)
            dz = dyv * (sz * (1.0 + z * (1.0 - sz)))
            dyn = dz * lg
            dcv = rstd * (dyn - jnp.mean(dyn, axis=-1, keepdims=True) - yn * jnp.mean(dyn * yn, axis=-1, keepdims=True))
            return dcv, dz, yn

        urot[0, 0:HALO, :] = jnp.where(first, 0.0, ap[...] * _sigmoid(gp[...]))
        urot[0, HALO + tt:, :] = jnp.where(last, 0.0, an[...] * _sigmoid(gn[...]))
        drot[0, 0:HALO, :] = jnp.where(first, 0.0, ln_bwd(cp_[...], dp[...])[0])
        drot[0, HALO + tt:, :] = jnp.where(last, 0.0, ln_bwd(cn[...], dn[...])[0])
        for s in range(tt // sub):
            rr = pl.ds(s * sub, sub)
            urot[0, pl.ds(HALO + s * sub, sub), :] = ac[rr, :] * _sigmoid(gc[rr, :])
            dcv, dz, yn = ln_bwd(cc[rr, :], dc[rr, :])
            drot[0, pl.ds(HALO + s * sub, sub), :] = dcv
            s_ref[0:1, :] += jnp.sum(dcv, axis=0, keepdims=True)
            s_ref[1:2, :] += jnp.sum(dz * yn, axis=0, keepdims=True)
            s_ref[2:3, :] += jnp.sum(dz, axis=0, keepdims=True)
        _shifted_copies(urot, wn)
        _shifted_copies(drot, wn)
        w = w_ref[...]
        for s in range(tt // sub):
            rr = pl.ds(s * sub, sub)
            dcv = drot[0, pl.ds(HALO + s * sub, sub), :]
            acc = jnp.zeros((sub, DA), f32)
            for j in range(CW):
                ad, bd = divmod(2 * HALO - 1 - j, 8)
                au, bu = divmod(1 + j, 8)
                acc = acc + drot[bd, pl.ds(s * sub + 8 * ad, sub), :] * w[j:j + 1, :]
                part = urot[bu, pl.ds(s * sub + 8 * au, sub), :] * dcv
                wacc[j] += part[0:8] + part[8:16] + part[16:24] + part[24:32]
            av, gv = ac[rr, :], gc[rr, :]
            sg = _sigmoid(gv)
            da_ref[rr, :] = (acc * sg).astype(bf16)
            dg_ref[rr, :] = (acc * av * sg * (1.0 - sg)).astype(bf16)

        @pl.when(last)
        def _():
            for j in range(CW):
                dcw_ref[j:j + 1, :] = jnp.sum(wacc[j], axis=0, keepdims=True)
            dcw_ref[CW:CW + 1, :] = jnp.zeros((1, DA), f32)

        for _, finish in plans:
            pl.when(last)(finish)

    hs = _halo_specs(tt, DA, T // HALO)
    hbm = pl.BlockSpec(memory_space=pl.ANY)
    return pl.pallas_call(
        body, name="conf_bwd", grid=(nt,),
        in_specs=hs * 4 + [_full((CW, DA)), _full((1, DA)), _full((1, DA))] + [hbm] * nb,
        out_specs=[pl.BlockSpec((tt, DA), lambda i: (i, 0)), pl.BlockSpec((tt, DA), lambda i: (i, 0)),
                   _full((CW + 1, DA)), _full((8, DA))] + [hbm] * nb,
        out_shape=[jax.ShapeDtypeStruct((T, DA), bf16), jax.ShapeDtypeStruct((T, DA), bf16),
                   jax.ShapeDtypeStruct((CW + 1, DA), f32), jax.ShapeDtypeStruct((8, DA), f32)]
                  + [jax.ShapeDtypeStruct(b.shape, b.dtype) for b in blocks],
        scratch_shapes=[pltpu.VMEM((8, wn, DA), f32), pltpu.VMEM((8, wn, DA), f32), pltpu.VMEM((CW, 8, DA), f32)]
                       + _a2a_scratch() * nb,
        compiler_params=_cp("arbitrary"),
    )(a, a, a, g, g, g, cv, cv, cv, dy, dy, dy, conv_w, ln_g, ln_b, *blocks)


def _attn_bwd(q, k, v, y, dy, lse, bias_tab, blocks):
    zr = 256
    nb = len(blocks)

    def body(q_ref, k_ref, v_ref, y_ref, dy_ref, lse_ref, b_ref, *rest):
        g_refs, (dq_ref, dk_hbm, dv_hbm, db_ref) = rest[:nb], rest[nb:nb + 4]
        recv_refs, (dk_s, dv_s, sem), a2a_sems = rest[nb + 4:2 * nb + 4], rest[2 * nb + 4:2 * nb + 7], rest[2 * nb + 7:]
        r = pl.program_id(0)
        plans = [_a2a_plan(g_refs[i], recv_refs[i], *a2a_sems[3 * i:3 * i + 3]) for i in range(nb)]
        for start, _ in plans:
            pl.when(r == 0)(start)

        @pl.when(r == 0)
        def _():
            def z(i, _):
                rr = pl.ds(pl.multiple_of(i * zr, zr), zr)
                dk_s[rr, :] = jnp.zeros((zr, DA), f32)
                dv_s[rr, :] = jnp.zeros((zr, DA), f32)
                return 0
            lax.fori_loop(0, TA // zr, z, 0)

        @pl.when((r <= WR // 2) | (r > GW - WR // 2))
        def _():
            db_ref[...] = jnp.zeros_like(db_ref)

        ks = pl.multiple_of(_win_start(r) * GW, GW)
        win = pl.ds(ks, WR * GW)
        qq, yy, dyy, lse_v = q_ref[...], y_ref[...], dy_ref[...], lse_ref[...]
        lo = lax.broadcasted_iota(jnp.int32, (GW, 2 * HD), 1) < HD
        ops, pairs = [], []
        for pr in range(NH // 2):
            ps = slice(pr * 2 * HD, (pr + 1) * 2 * HD)
            q2, do2 = qq[:, ps], dyy[:, ps]
            prod = do2.astype(f32) * yy[:, ps].astype(f32)
            delta = jnp.concatenate([jnp.sum(jnp.where(lo, prod, 0.0), axis=-1, keepdims=True),
                                     jnp.sum(jnp.where(lo, 0.0, prod), axis=-1, keepdims=True)], axis=0)
            qst, dost = _stack_pair(q2, lo), _stack_pair(do2, lo)
            kw, vw = k_ref[win, ps], v_ref[win, ps]
            kc, vc = k_ref[T:TA, ps], v_ref[T:TA, ps]
            ops.append((kw, kc))
            pairs.append((qst, dost, delta, _dg(qst, kw, NT), _dg(qst, kc, NT), _dg(dost, vw, NT), _dg(dost, vc, NT)))
        grads = []
        for pr, (qst, dost, delta, sl, sc, dpl, dpc) in enumerate(pairs):
            lh = jnp.concatenate([lse_v[:, 2 * pr:2 * pr + 1], lse_v[:, 2 * pr + 1:2 * pr + 2]], axis=0)
            bias2 = b_ref[0, 2 * pr:2 * pr + 2].reshape(2 * GW, WR * GW)
            pl_ = jnp.exp(sl * SCALE + bias2 - lh)
            pc = jnp.exp(sc * SCALE - lh)
            dsl = pl_ * (dpl - delta)
            dsc = pc * (dpc - delta)
            db_ref[0, 2 * pr:2 * pr + 2] += dsl.reshape(2, GW, WR * GW)
            grads.append((qst, dost, pl_.astype(bf16), pc.astype(bf16), dsl.astype(bf16), dsc.astype(bf16)))
        for pr in range(NH // 2):
            ps = slice(pr * 2 * HD, (pr + 1) * 2 * HD)
            kw, kc = ops[pr]
            qst, dost, plb, pcb, dslb, dscb = grads[pr]
            dqst = _dot(dslb, kw) + _dot(dscb, kc)
            dq_ref[:, ps] = (jnp.where(lo, dqst[0:GW], dqst[GW:]) * SCALE).astype(bf16)
            dk_s[win, ps] += _dg(dslb, qst, TN) * SCALE
            dv_s[win, ps] += _dg(plb, dost, TN)
            dk_s[T:TA, ps] += _dg(dscb, qst, TN) * SCALE
            dv_s[T:TA, ps] += _dg(pcb, dost, TN)

        @pl.when(r == GW - 1)
        def _():
            c1 = pltpu.make_async_copy(dk_s, dk_hbm, sem.at[0])
            c2 = pltpu.make_async_copy(dv_s, dv_hbm, sem.at[1])
            c1.start()
            c2.start()
            c1.wait()
            c2.wait()

        for _, finish in plans:
            pl.when(r == GW - 1)(finish)

    rowq = lambda: pl.BlockSpec((GW, DA), lambda r: (r, 0))
    hbm = pl.BlockSpec(memory_space=pl.ANY)
    return pl.pallas_call(
        body, name="attn_bwd", grid=(GW,),
        in_specs=[rowq(), _full((TA, DA)), _full((TA, DA)), rowq(), rowq(), pl.BlockSpec((GW, NH), lambda r: (r, 0)),
                  pl.BlockSpec((1, NH, GW, WR * GW), lambda r: (_pattern(r), 0, 0, 0))] + [hbm] * nb,
        out_specs=[rowq(), hbm, hbm, pl.BlockSpec((1, NH, GW, WR * GW), lambda r: (_pattern(r), 0, 0, 0))] + [hbm] * nb,
        out_shape=[jax.ShapeDtypeStruct((T, DA), bf16), jax.ShapeDtypeStruct((TA, DA), f32), jax.ShapeDtypeStruct((TA, DA), f32),
                   jax.ShapeDtypeStruct((8, NH, GW, WR * GW), f32)] + [jax.ShapeDtypeStruct(b.shape, b.dtype) for b in blocks],
        scratch_shapes=[pltpu.VMEM((TA, DA), f32), pltpu.VMEM((TA, DA), f32), pltpu.SemaphoreType.DMA((2,))] + _a2a_scratch() * nb,
        compiler_params=_cp("arbitrary"),
    )(q, k, v, y, dy, lse, bias_tab, *blocks)


def _rpb_reduce(dbias):
    rev = np.eye(GW, dtype=np.float32)[::-1]

    def body(d_ref, rev_ref, o_ref):
        rv = rev_ref[...]
        for h in range(NH):
            dv = d_ref[0, h]
            r0 = dv.astype(bf16)
            e1 = dv - r0.astype(f32)
            r1 = e1.astype(bf16)
            r2 = (e1 - r1.astype(f32)).astype(bf16)
            rr = _dot(rv, r0) + _dot(rv, r1) + _dot(rv, r2)
            skew = pltpu.roll(rr, 0, 1, stride=1, stride_axis=0)
            o_ref[0, h:h + 1, :] = jnp.sum(skew, axis=0, keepdims=True)

    return pl.pallas_call(
        body, name="rpb_reduce", grid=(8,),
        in_specs=[pl.BlockSpec((1, NH, GW, WR * GW), lambda p: (p, 0, 0, 0)), _full((GW, GW))],
        out_specs=pl.BlockSpec((1, NH, WR * GW), lambda p: (p, 0, 0)),
        out_shape=jax.ShapeDtypeStruct((8, NH, WR * GW), f32),
        compiler_params=_cp("parallel"),
    )(dbias, jnp.asarray(rev, dtype=bf16))


def _norm1_bwd(dq, dk, dv, da, dg, w_inT, x0, ctx0, h, dx1, modv, g1):
    tm = 256
    nt = TA // tm
    nx = T // tm

    def body(dq_ref, dk_ref, dv_ref, da_ref, dg_ref, w_ref, x_ref, c_ref, h_ref, dx1_ref, mod_ref, g_ref,
             dx_ref, dwo_ref, s_ref, dw_ref):
        i = pl.program_id(0)
        is_ctx = i == nt - 1

        @pl.when(i == 0)
        def _():
            s_ref[...] = jnp.zeros_like(s_ref)
            dw_ref[...] = jnp.zeros_like(dw_ref)

        hb = h_ref[...]
        dkb, dvb = dk_ref[...].astype(bf16), dv_ref[...].astype(bf16)
        dw_ref[DA:2 * DA, :] += _dg(dkb, hb, TN)
        dw_ref[2 * DA:3 * DA, :] += _dg(dvb, hb, TN)
        dh_kv = _dot(dkb, w_ref[DA:2 * DA, :]) + _dot(dvb, w_ref[2 * DA:3 * DA, :])
        gain = g_ref[...]

        @pl.when(is_ctx)
        def _():
            xv = c_ref[...]
            rstd = lax.rsqrt(jnp.mean(xv * xv, axis=-1, keepdims=True) + EPS)
            _, dsh, dsc, dgn = _norm_bwd(dh_kv, xv, gain, mod_ref[7:8, :], rstd)
            s_ref[2:3, :] += dgn
            s_ref[3:4, :] += dsh
            s_ref[4:5, :] += dsc
            dwo_ref[...] = dw_ref[...].astype(bf16)

        @pl.when(jnp.logical_not(is_ctx))
        def _():
            dqb, dab, dgb = dq_ref[...], da_ref[...], dg_ref[...]
            dw_ref[0:DA, :] += _dg(dqb, hb, TN)
            dw_ref[3 * DA:4 * DA, :] += _dg(dab, hb, TN)
            dw_ref[4 * DA:5 * DA, :] += _dg(dgb, hb, TN)
            dh = (dh_kv + _dot(dqb, w_ref[0:DA, :]) + _dot(dab, w_ref[3 * DA:4 * DA, :])
                  + _dot(dgb, w_ref[4 * DA:5 * DA, :]))
            xv = x_ref[...]
            rstd = lax.rsqrt(jnp.mean(xv * xv, axis=-1, keepdims=True) + EPS)
            dxn, dsh, dsc, dgn = _norm_bwd(dh, xv, gain, mod_ref[1:2, :], rstd)
            dx_ref[...] = dx1_ref[...] + dxn
            s_ref[0:1, :] += dsh
            s_ref[1:2, :] += dsc
            s_ref[2:3, :] += dgn

    row = lambda w: pl.BlockSpec((tm, w), lambda i: (i, 0))
    lrow = lambda w: pl.BlockSpec((tm, w), lambda i: (jnp.minimum(i, nx - 1), 0))
    return pl.pallas_call(
        body, name="norm1_bwd", grid=(nt,),
        in_specs=[lrow(DA), row(DA), row(DA), lrow(DA), lrow(DA), _full((5 * DA, D)), lrow(D), _full((TC, D)), row(D),
                  lrow(D), _full((8, D)), _full((1, D))],
        out_specs=[lrow(D), _full((5 * DA, D)), _full((8, D))],
        out_shape=[jax.ShapeDtypeStruct((T, D), f32), jax.ShapeDtypeStruct((5 * DA, D), bf16), jax.ShapeDtypeStruct((8, D), f32)],
        scratch_shapes=[pltpu.VMEM((5 * DA, D), f32)],
        compiler_params=_cp("arbitrary"),
    )(dq, dk, dv, da, dg, w_inT, x0, ctx0, h, dx1, modv, g1)


def _adam_tile(r, c):
    return max(t for t in range(8, r + 1, 8) if r % t == 0 and t * c * 4 <= 2 * 1024 * 1024)


def _adam_update(wv, gv, mv, vv):
    nm = ADAM_B1 * mv + (1.0 - ADAM_B1) * gv
    nv = ADAM_B2 * vv + (1.0 - ADAM_B2) * (gv * gv)
    m_hat = nm * (1.0 / (1.0 - ADAM_B1 ** ADAM_STEP))
    v_hat = nv * (1.0 / (1.0 - ADAM_B2 ** ADAM_STEP))
    return -ADAM_LR * (m_hat / (jnp.sqrt(v_hat) + ADAM_EPS) + ADAM_WD * wv), nm, nv


def _adamw(w, g, m, v, name):
    r, c = w.shape
    tr = _adam_tile(r, c)

    def body(w_ref, g_ref, m_ref, v_ref, d_ref, nm_ref, nv_ref):
        d_ref[...], nm_ref[...], nv_ref[...] = _adam_update(w_ref[...], g_ref[...], m_ref[...], v_ref[...])

    spec = pl.BlockSpec((tr, c), lambda i: (i, 0))
    return pl.pallas_call(
        body, name=name, grid=(r // tr,),
        in_specs=[spec] * 4, out_specs=[spec] * 3,
        out_shape=[jax.ShapeDtypeStruct((r, c), f32)] * 3,
        compiler_params=_cp("parallel"),
    )(w, g, m, v)


def _adamw_blocks(w, recv, m, v, name):
    r, c = w.shape
    tr = _adam_tile(r, c)

    def body(w_ref, a_ref, m_ref, v_ref, g_ref, d_ref, nm_ref, nv_ref):
        gv = a_ref[0].astype(f32)
        for d in range(1, NDEV):
            gv = gv + a_ref[d].astype(f32)
        g_ref[...] = gv
        d_ref[...], nm_ref[...], nv_ref[...] = _adam_update(w_ref[...], gv, m_ref[...], v_ref[...])

    spec = pl.BlockSpec((tr, c), lambda i: (i, 0))
    return pl.pallas_call(
        body, name=name, grid=(r // tr,),
        in_specs=[spec, pl.BlockSpec((NDEV, tr, c), lambda i: (0, i, 0)), spec, spec], out_specs=[spec] * 4,
        out_shape=[jax.ShapeDtypeStruct((r, c), f32)] * 4,
        compiler_params=_cp("parallel"),
    )(w, recv, m, v)


def _pad_rows128(vec):
    n = vec.shape[0]
    rows = -(-n // 1024) * 8
    return jnp.pad(vec, (0, rows * 128 - n)).reshape(rows, 128)


def _grad_rpb(dbias):
    lane_map, r_hot = _rpb_tables()
    return jnp.einsum("phl,lic,pir->hrc", _rpb_reduce(dbias), jnp.asarray(lane_map), jnp.asarray(r_hot),
                      precision=lax.Precision.HIGHEST)


def kernel(x, c, ctx, c_ctx, w_mod, b_mod, g_norm1, w_in, rpb, conv_w, conv_b, ln_g, ln_b, w_out, g_norm2, w_up, ffn_conv_w, ffn_conv_b, w_down, g_final, loss_target, m_c_ctx, m_w_mod, m_b_mod, m_g_norm1, m_w_in, m_rpb, m_conv_w, m_conv_b, m_ln_g, m_ln_b, m_w_out, m_g_norm2, m_w_up, m_ffn_conv_w, m_ffn_conv_b, m_w_down, m_g_final, v_c_ctx, v_w_mod, v_b_mod, v_g_norm1, v_w_in, v_rpb, v_conv_w, v_conv_b, v_ln_g, v_ln_b, v_w_out, v_g_norm2, v_w_up, v_ffn_conv_w, v_ffn_conv_b, v_w_down, v_g_final):
    me = 4 * lax.axis_index("x") + 2 * lax.axis_index("y") + lax.axis_index("c")
    nmod = w_mod.shape[2]
    n_in = w_in.shape[2]
    n_out = w_out.shape[1]
    n_up = w_up.shape[2]
    n_dn = w_down.shape[1]
    n_cw = conv_w.shape[2]

    b_sh = lax.dynamic_slice(b_mod, (0, me * nmod), (1, nmod))
    convpay = _pad_rows128(jnp.concatenate([conv_w[0].reshape(-1), ffn_conv_w[0].reshape(-1)]))
    w_inT, c_all, mod_all, flat, bias_tab = _head(w_in[0].T.astype(bf16), c.reshape(8, 128), c_ctx.reshape(8, 128),
                                                  w_mod[0], b_sh, convpay, *_bias_table_inputs(rpb[0]))
    cvec = jnp.concatenate([c_all.reshape(NDEV, D), c_ctx[None, :], jnp.zeros((7, D), f32)], axis=0)
    mod_all = mod_all.transpose(1, 0, 2).reshape(16, 6 * D)
    mod_me = lax.dynamic_index_in_dim(mod_all, me, 0, keepdims=False).reshape(6, D)
    mod_c = mod_all[8]
    modv = jnp.concatenate([mod_me, mod_c[None, 0:D], mod_c[None, D:2 * D]], axis=0)
    flat = flat.reshape(NDEV, -1)
    o1 = CW * n_cw
    conv_w_f = flat[:, :o1].reshape(NDEV, CW, n_cw).transpose(1, 0, 2).reshape(CW, DA)
    fcw_f = flat[:, o1:o1 + 3 * n_up].reshape(NDEV, 3, n_up).transpose(1, 0, 2).reshape(3, F2)

    x0, ctx0 = x[0], ctx[0]
    h, q, k, v, a, g, w_down_f = _in_proj(x0, ctx0, g_norm1, modv, w_inT, [w_down[0].astype(bf16)])
    y_na, lse, y_cv, cv, w_upT, w_out_f = _mix_fwd(q, k, v, bias_tab, a, g, conv_w_f, conv_b, ln_g, ln_b,
                                                   [w_up[0].T.astype(bf16), w_out[0].astype(bf16)])
    x1, proj, h2 = _out_proj(x0, y_na, y_cv, w_out_f, modv, g_norm2)
    ffn, u_t, u2_t = _ffn_fwd(h2, w_upT, fcw_f, ffn_conv_b, w_down_f)
    dx2, dffn, s_loss = _loss_bwd(ffn, x1, loss_target[0], modv, g_final[None, :])

    dug, duv, dw_up, dw_down, dcwg, dcwv, dcbg, dcbv = _ffn_bwd(h2, dffn, u_t, u2_t, fcw_f, w_down_f)
    dx1, dy_na, dy_cv, dw_out, s_n2 = _norm2_bwd(dug, duv, w_upT, x1, dx2, proj, w_out_f, y_na, y_cv, modv, g_norm2)
    da, dg, dcw, s_cf, rv_down = _conf_bwd(a, g, cv, dy_cv, conv_w_f, ln_g, ln_b, [dw_down.reshape(NDEV, n_dn, D)])
    dq, dk, dv, dbias, rv_up = _attn_bwd(q, k, v, y_na, dy_na, lse, bias_tab, [dw_up.reshape(NDEV, n_up, D)])
    grad_rpb_part = _grad_rpb(dbias)
    grad_x, dw_inT, s_n1 = _norm1_bwd(dq, dk, dv, da, dg, w_inT, x0, ctx0, h, dx1, modv, g_norm1)
    grad_x = grad_x[None]
    dfcw = jnp.concatenate([dcwg, dcwv], axis=1)
    dfcb = jnp.concatenate([dcbg[0], dcbv[0]])
    small = jnp.concatenate([dcw[:CW].reshape(CW, NDEV, n_cw).transpose(1, 0, 2).reshape(NDEV, CW * n_cw),
                             dfcw.reshape(3, NDEV, n_up).transpose(1, 0, 2).reshape(NDEV, 3 * n_up)], axis=1)
    n_sm = small.shape[1] // D
    pad_sm = -n_sm % 16
    small = jnp.pad(small.reshape(NDEV, n_sm, D), ((0, 0), (0, pad_sm), (0, 0))).astype(bf16)
    dmod = jnp.concatenate([s_n1[0], s_n1[1], s_n2[3], s_n2[0], s_n2[1], s_loss[1]])
    dmodc = jnp.concatenate([s_n1[3], s_n1[4]])
    parts = [dmodc, s_n1[2], grad_rpb_part.reshape(-1), s_cf[0], s_cf[1], s_cf[2], s_n2[2], dfcb, s_loss[0], s_loss[3, 0:1]]
    sizes = [p.shape[0] for p in parts]
    pvec = _pad_rows128(jnp.concatenate([dmod] + parts))
    slab = jnp.concatenate([dw_inT.reshape(NDEV, n_in, D), small, dw_out.reshape(NDEV, n_out, D)], axis=1)
    r_a, gath = _reduce_scatter2(slab, [pvec], "rs_w_in")
    g_w_out = r_a[n_in + n_sm + pad_sm:]
    g_w_inT = r_a[:n_in]
    sm = r_a[n_in:n_in + n_sm].reshape(-1)
    g_conv_w = sm[:CW * n_cw].reshape(CW, n_cw)
    g_fcw = sm[CW * n_cw:].reshape(3, n_up)
    tot = _sum_rows8(gath, "sum_small").reshape(-1)
    dmod_all = gath.reshape(NDEV, -1)[:, :6 * D]
    offs = np.cumsum([6 * D] + sizes)
    pick = lambda j: tot[offs[j]:offs[j + 1]]
    dmodc_t = jnp.pad(pick(0), (0, 4 * D))
    g_b_mod = (tot[:6 * D] + dmodc_t)[None, :]
    g_g_norm1 = pick(1)[None, :]
    g_rpb = pick(2).reshape(1, NH, 2 * WR - 1, 2 * NCOL - 1)
    g_conv_b, g_ln_g, g_ln_b = pick(3)[None, :], pick(4)[None, :], pick(5)[None, :]
    g_g_norm2 = pick(6)[None, :]
    g_fcb = pick(7)[None, :]
    g_g_final = pick(8)
    loss = pick(9)[0]
    dm_rows = jnp.concatenate([dmod_all, dmodc_t[None, :], jnp.zeros((7, 6 * D), f32)], axis=0)
    dm_sh = lax.dynamic_slice(dm_rows, (0, me * nmod), (16, nmod))
    g_w_mod, gc_part = _mod_bwd(cvec, dm_sh, w_mod[0])
    gc_sum = _sum_rows8(_small_allgather(gc_part[0].reshape(8, 128), "ag_cctx"), "sum_cctx").reshape(D)
    sg_c = _sigmoid(c_ctx)
    g_c_ctx = gc_sum * (sg_c * (1.0 + c_ctx * (1.0 - sg_c)))

    big = [("w_mod", w_mod[0], g_w_mod, m_w_mod[0], v_w_mod[0]), ("w_in", w_in[0].T, g_w_inT, m_w_in[0].T, v_w_in[0].T),
           ("w_out", w_out[0], g_w_out, m_w_out[0], v_w_out[0])]
    upd = {n: _adamw(wv, gv, mv, vv, "adamw_" + n) for n, wv, gv, mv, vv in big}
    g_w_upT, *upd["w_up"] = _adamw_blocks(w_up[0].T, rv_up, m_w_up[0].T, v_w_up[0].T, "adamw_w_up")
    g_w_down, *upd["w_down"] = _adamw_blocks(w_down[0], rv_down, m_w_down[0], v_w_down[0], "adamw_w_down")
    for n in ("w_in", "w_up"):
        upd[n] = tuple(arr.T for arr in upd[n])
    g_w_in, g_w_up = g_w_inT.T, g_w_upT.T
    smalls = [("c_ctx", c_ctx, g_c_ctx, m_c_ctx, v_c_ctx), ("b_mod", b_mod, g_b_mod, m_b_mod, v_b_mod),
              ("g_norm1", g_norm1, g_g_norm1, m_g_norm1, v_g_norm1), ("rpb", rpb, g_rpb, m_rpb, v_rpb),
              ("conv_w", conv_w, g_conv_w[None], m_conv_w, v_conv_w), ("conv_b", conv_b, g_conv_b, m_conv_b, v_conv_b),
              ("ln_g", ln_g, g_ln_g, m_ln_g, v_ln_g), ("ln_b", ln_b, g_ln_b, m_ln_b, v_ln_b),
              ("g_norm2", g_norm2, g_g_norm2, m_g_norm2, v_g_norm2),
              ("ffn_conv_w", ffn_conv_w, g_fcw[None], m_ffn_conv_w, v_ffn_conv_w),
              ("ffn_conv_b", ffn_conv_b, g_fcb, m_ffn_conv_b, v_ffn_conv_b), ("g_final", g_final, g_g_final, m_g_final, v_g_final)]
    packed = [_pad_rows128(jnp.concatenate([t[j].reshape(-1) for t in smalls])) for j in (1, 2, 3, 4)]
    sd, sm_, sv = _adamw(*packed, "adamw_small")
    so = np.cumsum([0] + [int(np.prod(t[1].shape)) for t in smalls])
    for j, t in enumerate(smalls):
        shp = t[1].shape
        upd[t[0]] = tuple(arr.reshape(-1)[so[j]:so[j + 1]].reshape(shp) for arr in (sd, sm_, sv))
    grads = {"c_ctx": g_c_ctx, "w_mod": g_w_mod[None], "b_mod": g_b_mod, "g_norm1": g_g_norm1, "w_in": g_w_in[None],
             "rpb": g_rpb, "conv_w": g_conv_w[None], "conv_b": g_conv_b, "ln_g": g_ln_g, "ln_b": g_ln_b,
             "w_out": g_w_out[None], "g_norm2": g_g_norm2, "w_up": g_w_up[None], "ffn_conv_w": g_fcw[None],
             "ffn_conv_b": g_fcb, "w_down": g_w_down[None], "g_final": g_g_final}
    names = ["c_ctx", "w_mod", "b_mod", "g_norm1", "w_in", "rpb", "conv_w", "conv_b", "ln_g", "ln_b", "w_out", "g_norm2",
             "w_up", "ffn_conv_w", "ffn_conv_b", "w_down", "g_final"]
    shapes = {n: grads[n].shape for n in names}
    outs = [loss, grad_x] + [grads[n] for n in names]
    for j in range(3):
        outs += [upd[n][j].reshape(shapes[n]) for n in names]
    return tuple(outs)
```
